```python
import jax, jax.numpy as jnp
from jax import lax
import numpy as np

D_MODEL = 1024
BATCH = 16
SEQ = 4096
DEPTH = 2

D_MIX = D_MODEL
CA = D_MIX // 2
A_GROUPS = 8
CONV_WIDTH = 31
CB = D_MIX // 2
GB = 4
DB = CB // GB
CHUNK = 128
N_HEADS = 16
HEAD_DIM = D_MIX // N_HEADS
Q_BLOCK = 128
D_FF = 2816
N_EVEN = (DEPTH + 1) // 2
N_ODD = DEPTH // 2
EPS = 1e-6

kernel_name = "hybrid_conv_gmlp_stickbreaking_macaron"


def rms_norm(x, g):
    xf = x.astype(jnp.float32)
    y = xf * lax.rsqrt(jnp.mean(xf * xf, axis=-1, keepdims=True) + EPS)
    return (y * g.astype(jnp.float32)).astype(x.dtype)


def layer_norm(x, g, b):
    xf = x.astype(jnp.float32)
    mu = jnp.mean(xf, axis=-1, keepdims=True)
    xc = xf - mu
    var = jnp.mean(xc * xc, axis=-1, keepdims=True)
    y = xc * lax.rsqrt(var + EPS)
    return (y * g.astype(jnp.float32) + b.astype(jnp.float32)).astype(x.dtype)


def swiglu(h, w_gate, w_up, w_down):
    return (jax.nn.silu(h @ w_gate) * (h @ w_up)) @ w_down


def causal_depthwise_conv(x, w, b):
    k, c = w.shape
    y = lax.conv_general_dilated(
        x, w[:, None, :], window_strides=(1,), padding=[(k - 1, 0)],
        dimension_numbers=("NWC", "WIO", "NWC"), feature_group_count=c)
    return y + b


def conv_gating_mixer(h, w_in, conv_w, conv_b, ln_a_g, ln_a_b, ln_v_g, ln_v_b,
                      sp_w, sp_b, w_out):
    bsz, s, _ = h.shape
    z = h @ w_in
    a_val, a_gate, u, v = jnp.split(z, [CA, 2 * CA, 2 * CA + CB], axis=-1)
    a = a_val * jax.nn.sigmoid(a_gate)
    a = causal_depthwise_conv(a, conv_w, conv_b)
    a = jax.nn.silu(layer_norm(a, ln_a_g, ln_a_b))
    v = layer_norm(v.reshape(bsz, s, GB, DB), ln_v_g, ln_v_b)
    v = v.reshape(bsz, s // CHUNK, CHUNK, GB, DB)
    causal = jnp.tril(jnp.ones((CHUNK, CHUNK), dtype=bool))
    w_s = jnp.where(causal[None], sp_w, 0)
    v = jnp.einsum("gts,bcsgd->bctgd", w_s, v) + sp_b.T[:, :, None]
    b_out = u * v.reshape(bsz, s, CB)
    return jnp.concatenate([a, b_out], axis=-1) @ w_out


def stick_breaking_attention(q, k, v):
    bsz, s, h, dh = q.shape
    nb = s // Q_BLOCK
    scale = dh ** -0.5

    def to_blocks(t):
        return t.reshape(bsz, nb, Q_BLOCK, h, dh).transpose(1, 0, 3, 2, 4)

    qb, kb, vb = to_blocks(q), to_blocks(k), to_blocks(v)
    pos = jnp.arange(Q_BLOCK)

    def per_query_block(args):
        qi, qblk = args
        qf = qblk.astype(jnp.float32)

        def step(carry, xs):
            kj, kblk, vblk = xs

            def accumulate(c):
                log_rem, acc = c
                z = jnp.einsum("bhtd,bhsd->bhts", qf, kblk.astype(jnp.float32)) * scale
                mask = (kj * Q_BLOCK + pos[None, :]) < (qi * Q_BLOCK + pos[:, None])
                log_keep = jnp.where(mask, -jax.nn.softplus(z), 0.0)
                suffix = lax.cumsum(log_keep, axis=3, reverse=True) - log_keep
                log_a = jax.nn.log_sigmoid(z) + suffix + log_rem[..., None]
                a = jnp.where(mask, jnp.exp(log_a), 0.0)
                acc = acc + jnp.einsum("bhts,bhsd->bhtd", a, vblk.astype(jnp.float32))
                return (log_rem + jnp.sum(log_keep, axis=-1), acc)

            return lax.cond(kj <= qi, accumulate, lambda c: c, carry), None

        init = (jnp.zeros((bsz, h, Q_BLOCK), jnp.float32),
                jnp.zeros((bsz, h, Q_BLOCK, dh), jnp.float32))
        (_, acc), _ = lax.scan(step, init, (jnp.arange(nb), kb, vb), reverse=True)
        return acc.astype(q.dtype)

    out = lax.map(per_query_block, (jnp.arange(nb), qb))
    return out.transpose(1, 0, 3, 2, 4).reshape(bsz, s, h * dh)


def stick_breaking_mixer(h, w_qkv, w_o):
    bsz, s, _ = h.shape
    q, k, v = jnp.split(h @ w_qkv, 3, axis=-1)
    q = q.reshape(bsz, s, N_HEADS, HEAD_DIM)
    k = k.reshape(bsz, s, N_HEADS, HEAD_DIM)
    v = v.reshape(bsz, s, N_HEADS, HEAD_DIM)
    return stick_breaking_attention(q, k, v) @ w_o


def _fwd_setup_inputs(seed: int = 0) -> dict:
    key = jax.random.key(seed)
    ks = iter(jax.random.split(key, 32))
    f32 = jnp.float32

    def nrm(shape, fan_in):
        return jax.random.normal(next(ks), shape, f32) * (fan_in ** -0.5)

    def gain(shape):
        return 1.0 + 0.02 * jax.random.normal(next(ks), shape, f32)

    def bias(shape, s=0.02):
        return s * jax.random.normal(next(ks), shape, f32)

    return {
        "x": jax.random.normal(next(ks), (BATCH, SEQ, D_MODEL), f32),
        "g_ffn1": gain((DEPTH, D_MODEL)),
        "w_ffn1_gate": nrm((DEPTH, D_MODEL, D_FF), D_MODEL),
        "w_ffn1_up": nrm((DEPTH, D_MODEL, D_FF), D_MODEL),
        "w_ffn1_down": nrm((DEPTH, D_FF, D_MODEL), D_FF),
        "g_mix": gain((DEPTH, D_MODEL)),
        "w_in_ab": nrm((N_EVEN, D_MODEL, 2 * CA + 2 * CB), D_MODEL),
        "conv_w": nrm((N_EVEN, CONV_WIDTH, CA), CONV_WIDTH),
        "conv_b": bias((N_EVEN, CA)),
        "ln_a_g": gain((N_EVEN, CA)),
        "ln_a_b": bias((N_EVEN, CA)),
        "ln_v_g": gain((N_EVEN, GB, DB)),
        "ln_v_b": bias((N_EVEN, GB, DB)),
        "sp_w": nrm((N_EVEN, GB, CHUNK, CHUNK), CHUNK),
        "sp_b": 1.0 + bias((N_EVEN, GB, CHUNK), 0.1),
        "w_out_ab": nrm((N_EVEN, CA + CB, D_MODEL), CA + CB),
        "w_qkv": nrm((N_ODD, D_MODEL, 3 * D_MIX), D_MODEL),
        "w_o": nrm((N_ODD, D_MIX, D_MODEL), D_MIX),
        "g_ffn2": gain((DEPTH, D_MODEL)),
        "w_ffn2_gate": nrm((DEPTH, D_MODEL, D_FF), D_MODEL),
        "w_ffn2_up": nrm((DEPTH, D_MODEL, D_FF), D_MODEL),
        "w_ffn2_down": nrm((DEPTH, D_FF, D_MODEL), D_FF),
        "g_final": gain((D_MODEL,)),
    }


def _fwd_reference(x, g_ffn1, w_ffn1_gate, w_ffn1_up, w_ffn1_down, g_mix, w_in_ab,
              conv_w, conv_b, ln_a_g, ln_a_b, ln_v_g, ln_v_b, sp_w, sp_b, w_out_ab,
              w_qkv, w_o, g_ffn2, w_ffn2_gate, w_ffn2_up, w_ffn2_down, g_final):
    for l in range(DEPTH):
        x = x + 0.5 * swiglu(rms_norm(x, g_ffn1[l]), w_ffn1_gate[l], w_ffn1_up[l], w_ffn1_down[l])
        h = rms_norm(x, g_mix[l])
        i = l // 2
        if l % 2 == 0:
            x = x + conv_gating_mixer(h, w_in_ab[i], conv_w[i], conv_b[i], ln_a_g[i], ln_a_b[i],
                                      ln_v_g[i], ln_v_b[i], sp_w[i], sp_b[i], w_out_ab[i])
        else:
            x = x + stick_breaking_mixer(h, w_qkv[i], w_o[i])
        x = x + 0.5 * swiglu(rms_norm(x, g_ffn2[l]), w_ffn2_gate[l], w_ffn2_up[l], w_ffn2_down[l])
    return rms_norm(x, g_final)


import jax as _jax
import jax.numpy as _jnp

TWIN_FORMAT = 'train_step'
FWD_PARAMS = ['x', 'g_ffn1', 'w_ffn1_gate', 'w_ffn1_up', 'w_ffn1_down', 'g_mix', 'w_in_ab', 'conv_w', 'conv_b', 'ln_a_g', 'ln_a_b', 'ln_v_g', 'ln_v_b', 'sp_w', 'sp_b', 'w_out_ab', 'w_qkv', 'w_o', 'g_ffn2', 'w_ffn2_gate', 'w_ffn2_up', 'w_ffn2_down', 'g_final']
TWIN_WEIGHTS = ['g_ffn1', 'w_ffn1_gate', 'w_ffn1_up', 'w_ffn1_down', 'g_mix', 'w_in_ab', 'conv_w', 'conv_b', 'ln_a_g', 'ln_a_b', 'ln_v_g', 'ln_v_b', 'sp_w', 'sp_b', 'w_out_ab', 'w_qkv', 'w_o', 'g_ffn2', 'w_ffn2_gate', 'w_ffn2_up', 'w_ffn2_down', 'g_final']
TWIN_DIFF_INPUT = 'x'
TWIN_INPUTS = ['x', 'g_ffn1', 'w_ffn1_gate', 'w_ffn1_up', 'w_ffn1_down', 'g_mix', 'w_in_ab', 'conv_w', 'conv_b', 'ln_a_g', 'ln_a_b', 'ln_v_g', 'ln_v_b', 'sp_w', 'sp_b', 'w_out_ab', 'w_qkv', 'w_o', 'g_ffn2', 'w_ffn2_gate', 'w_ffn2_up', 'w_ffn2_down', 'g_final', 'loss_target', 'm_g_ffn1', 'm_w_ffn1_gate', 'm_w_ffn1_up', 'm_w_ffn1_down', 'm_g_mix', 'm_w_in_ab', 'm_conv_w', 'm_conv_b', 'm_ln_a_g', 'm_ln_a_b', 'm_ln_v_g', 'm_ln_v_b', 'm_sp_w', 'm_sp_b', 'm_w_out_ab', 'm_w_qkv', 'm_w_o', 'm_g_ffn2', 'm_w_ffn2_gate', 'm_w_ffn2_up', 'm_w_ffn2_down', 'm_g_final', 'v_g_ffn1', 'v_w_ffn1_gate', 'v_w_ffn1_up', 'v_w_ffn1_down', 'v_g_mix', 'v_w_in_ab', 'v_conv_w', 'v_conv_b', 'v_ln_a_g', 'v_ln_a_b', 'v_ln_v_g', 'v_ln_v_b', 'v_sp_w', 'v_sp_b', 'v_w_out_ab', 'v_w_qkv', 'v_w_o', 'v_g_ffn2', 'v_w_ffn2_gate', 'v_w_ffn2_up', 'v_w_ffn2_down', 'v_g_final']
TWIN_OUTPUTS = ['loss', 'grad_x', 'grad_g_ffn1', 'grad_w_ffn1_gate', 'grad_w_ffn1_up', 'grad_w_ffn1_down', 'grad_g_mix', 'grad_w_in_ab', 'grad_conv_w', 'grad_conv_b', 'grad_ln_a_g', 'grad_ln_a_b', 'grad_ln_v_g', 'grad_ln_v_b', 'grad_sp_w', 'grad_sp_b', 'grad_w_out_ab', 'grad_w_qkv', 'grad_w_o', 'grad_g_ffn2', 'grad_w_ffn2_gate', 'grad_w_ffn2_up', 'grad_w_ffn2_down', 'grad_g_final', 'delta_g_ffn1', 'delta_w_ffn1_gate', 'delta_w_ffn1_up', 'delta_w_ffn1_down', 'delta_g_mix', 'delta_w_in_ab', 'delta_conv_w', 'delta_conv_b', 'delta_ln_a_g', 'delta_ln_a_b', 'delta_ln_v_g', 'delta_ln_v_b', 'delta_sp_w', 'delta_sp_b', 'delta_w_out_ab', 'delta_w_qkv', 'delta_w_o', 'delta_g_ffn2', 'delta_w_ffn2_gate', 'delta_w_ffn2_up', 'delta_w_ffn2_down', 'delta_g_final', 'new_m_g_ffn1', 'new_m_w_ffn1_gate', 'new_m_w_ffn1_up', 'new_m_w_ffn1_down', 'new_m_g_mix', 'new_m_w_in_ab', 'new_m_conv_w', 'new_m_conv_b', 'new_m_ln_a_g', 'new_m_ln_a_b', 'new_m_ln_v_g', 'new_m_ln_v_b', 'new_m_sp_w', 'new_m_sp_b', 'new_m_w_out_ab', 'new_m_w_qkv', 'new_m_w_o', 'new_m_g_ffn2', 'new_m_w_ffn2_gate', 'new_m_w_ffn2_up', 'new_m_w_ffn2_down', 'new_m_g_final', 'new_v_g_ffn1', 'new_v_w_ffn1_gate', 'new_v_w_ffn1_up', 'new_v_w_ffn1_down', 'new_v_g_mix', 'new_v_w_in_ab', 'new_v_conv_w', 'new_v_conv_b', 'new_v_ln_a_g', 'new_v_ln_a_b', 'new_v_ln_v_g', 'new_v_ln_v_b', 'new_v_sp_w', 'new_v_sp_b', 'new_v_w_out_ab', 'new_v_w_qkv', 'new_v_w_o', 'new_v_g_ffn2', 'new_v_w_ffn2_gate', 'new_v_w_ffn2_up', 'new_v_w_ffn2_down', 'new_v_g_final']
TWIN_LEAF_KINDS = {'loss': 'loss', 'grad_x': 'grad_x', 'grad_g_ffn1': 'grad_w', 'grad_w_ffn1_gate': 'grad_w', 'grad_w_ffn1_up': 'grad_w', 'grad_w_ffn1_down': 'grad_w', 'grad_g_mix': 'grad_w', 'grad_w_in_ab': 'grad_w', 'grad_conv_w': 'grad_w', 'grad_conv_b': 'grad_w', 'grad_ln_a_g': 'grad_w', 'grad_ln_a_b': 'grad_w', 'grad_ln_v_g': 'grad_w', 'grad_ln_v_b': 'grad_w', 'grad_sp_w': 'grad_w', 'grad_sp_b': 'grad_w', 'grad_w_out_ab': 'grad_w', 'grad_w_qkv': 'grad_w', 'grad_w_o': 'grad_w', 'grad_g_ffn2': 'grad_w', 'grad_w_ffn2_gate': 'grad_w', 'grad_w_ffn2_up': 'grad_w', 'grad_w_ffn2_down': 'grad_w', 'grad_g_final': 'grad_w', 'delta_g_ffn1': 'delta_w', 'delta_w_ffn1_gate': 'delta_w', 'delta_w_ffn1_up': 'delta_w', 'delta_w_ffn1_down': 'delta_w', 'delta_g_mix': 'delta_w', 'delta_w_in_ab': 'delta_w', 'delta_conv_w': 'delta_w', 'delta_conv_b': 'delta_w', 'delta_ln_a_g': 'delta_w', 'delta_ln_a_b': 'delta_w', 'delta_ln_v_g': 'delta_w', 'delta_ln_v_b': 'delta_w', 'delta_sp_w': 'delta_w', 'delta_sp_b': 'delta_w', 'delta_w_out_ab': 'delta_w', 'delta_w_qkv': 'delta_w', 'delta_w_o': 'delta_w', 'delta_g_ffn2': 'delta_w', 'delta_w_ffn2_gate': 'delta_w', 'delta_w_ffn2_up': 'delta_w', 'delta_w_ffn2_down': 'delta_w', 'delta_g_final': 'delta_w', 'new_m_g_ffn1': 'new_m', 'new_m_w_ffn1_gate': 'new_m', 'new_m_w_ffn1_up': 'new_m', 'new_m_w_ffn1_down': 'new_m', 'new_m_g_mix': 'new_m', 'new_m_w_in_ab': 'new_m', 'new_m_conv_w': 'new_m', 'new_m_conv_b': 'new_m', 'new_m_ln_a_g': 'new_m', 'new_m_ln_a_b': 'new_m', 'new_m_ln_v_g': 'new_m', 'new_m_ln_v_b': 'new_m', 'new_m_sp_w': 'new_m', 'new_m_sp_b': 'new_m', 'new_m_w_out_ab': 'new_m', 'new_m_w_qkv': 'new_m', 'new_m_w_o': 'new_m', 'new_m_g_ffn2': 'new_m', 'new_m_w_ffn2_gate': 'new_m', 'new_m_w_ffn2_up': 'new_m', 'new_m_w_ffn2_down': 'new_m', 'new_m_g_final': 'new_m', 'new_v_g_ffn1': 'new_v', 'new_v_w_ffn1_gate': 'new_v', 'new_v_w_ffn1_up': 'new_v', 'new_v_w_ffn1_down': 'new_v', 'new_v_g_mix': 'new_v', 'new_v_w_in_ab': 'new_v', 'new_v_conv_w': 'new_v', 'new_v_conv_b': 'new_v', 'new_v_ln_a_g': 'new_v', 'new_v_ln_a_b': 'new_v', 'new_v_ln_v_g': 'new_v', 'new_v_ln_v_b': 'new_v', 'new_v_sp_w': 'new_v', 'new_v_sp_b': 'new_v', 'new_v_w_out_ab': 'new_v', 'new_v_w_qkv': 'new_v', 'new_v_w_o': 'new_v', 'new_v_g_ffn2': 'new_v', 'new_v_w_ffn2_gate': 'new_v', 'new_v_w_ffn2_up': 'new_v', 'new_v_w_ffn2_down': 'new_v', 'new_v_g_final': 'new_v'}


def _forward(args):
    return _fwd_reference(*[args[k] for k in FWD_PARAMS])


def _output_shape():
    out = _jax.eval_shape(lambda: _forward(_fwd_setup_inputs(0)))
    return out.shape, out.dtype

N_MICROBATCH = 1
ADAM_LR = 0.001
ADAM_B1 = 0.9
ADAM_B2 = 0.999
ADAM_EPS = 1e-08
ADAM_WD = 0.01
ADAM_STEP = 10
PER_EXAMPLE_BATCH_AXIS = {'x': 0, 'loss_target': 0}
SHARED_INPUTS = []
_WEIGHT_DTYPES = {'g_ffn1': _jnp.float32, 'w_ffn1_gate': _jnp.float32, 'w_ffn1_up': _jnp.float32, 'w_ffn1_down': _jnp.float32, 'g_mix': _jnp.float32, 'w_in_ab': _jnp.float32, 'conv_w': _jnp.float32, 'conv_b': _jnp.float32, 'ln_a_g': _jnp.float32, 'ln_a_b': _jnp.float32, 'ln_v_g': _jnp.float32, 'ln_v_b': _jnp.float32, 'sp_w': _jnp.float32, 'sp_b': _jnp.float32, 'w_out_ab': _jnp.float32, 'w_qkv': _jnp.float32, 'w_o': _jnp.float32, 'g_ffn2': _jnp.float32, 'w_ffn2_gate': _jnp.float32, 'w_ffn2_up': _jnp.float32, 'w_ffn2_down': _jnp.float32, 'g_final': _jnp.float32}
MOMENT_SCALE = {'g_ffn1': 1.025884e-01, 'w_ffn1_gate': 4.374029e-02, 'w_ffn1_up': 4.237556e-02, 'w_ffn1_down': 7.021516e-02, 'g_mix': 1.854371e-01, 'w_in_ab': 1.510062e-01, 'conv_w': 1.225867e-01, 'conv_b': 2.898995e-01, 'ln_a_g': 1.394397e-01, 'ln_a_b': 1.321180e-01, 'ln_v_g': 1.453679e-01, 'ln_v_b': 1.255303e-01, 'sp_w': 1.336990e-01, 'sp_b': 1.943538e-01, 'w_out_ab': 1.884452e-01, 'w_qkv': 7.085133e-02, 'w_o': 1.028765e-01, 'g_ffn2': 7.466594e-02, 'w_ffn2_gate': 3.153305e-02, 'w_ffn2_up': 3.060651e-02, 'w_ffn2_down': 5.078891e-02, 'g_final': 6.394658e+01}


def _to_microbatches(a, axis):
    t = _jnp.moveaxis(a, axis, 0)
    t = t.reshape((N_MICROBATCH, t.shape[0] // N_MICROBATCH) + t.shape[1:])
    return _jnp.moveaxis(t, 1, axis + 1)


def setup_inputs(seed: int = 0) -> dict:
    inp = _fwd_setup_inputs(seed)
    key = _jax.random.fold_in(_jax.random.key(seed), 7919)
    shape, _ = _output_shape()
    out = dict(inp)
    out["loss_target"] = _jax.random.normal(_jax.random.fold_in(key, 0), shape, _jnp.float32)
    for i, name in enumerate(TWIN_WEIGHTS):
        w = inp[name].astype(_jnp.float32)
        if MOMENT_SCALE is None:
            s = _jnp.sqrt(_jnp.mean(_jnp.square(w)) + 1e-30)
        else:
            s = MOMENT_SCALE[name]
        km, kv = _jax.random.split(_jax.random.fold_in(key, i + 1))
        out[name] = w
        out["m_" + name] = s * _jax.random.normal(km, w.shape, _jnp.float32)
        out["v_" + name] = (s * s) * _jax.random.uniform(kv, w.shape, _jnp.float32, 0.5, 1.5)
    if N_MICROBATCH > 1:
        for name, axis in PER_EXAMPLE_BATCH_AXIS.items():
            out[name] = _to_microbatches(out[name], axis)
    return {'x': out['x'], 'g_ffn1': out['g_ffn1'], 'w_ffn1_gate': out['w_ffn1_gate'], 'w_ffn1_up': out['w_ffn1_up'], 'w_ffn1_down': out['w_ffn1_down'], 'g_mix': out['g_mix'], 'w_in_ab': out['w_in_ab'], 'conv_w': out['conv_w'], 'conv_b': out['conv_b'], 'ln_a_g': out['ln_a_g'], 'ln_a_b': out['ln_a_b'], 'ln_v_g': out['ln_v_g'], 'ln_v_b': out['ln_v_b'], 'sp_w': out['sp_w'], 'sp_b': out['sp_b'], 'w_out_ab': out['w_out_ab'], 'w_qkv': out['w_qkv'], 'w_o': out['w_o'], 'g_ffn2': out['g_ffn2'], 'w_ffn2_gate': out['w_ffn2_gate'], 'w_ffn2_up': out['w_ffn2_up'], 'w_ffn2_down': out['w_ffn2_down'], 'g_final': out['g_final'], 'loss_target': out['loss_target'], 'm_g_ffn1': out['m_g_ffn1'], 'm_w_ffn1_gate': out['m_w_ffn1_gate'], 'm_w_ffn1_up': out['m_w_ffn1_up'], 'm_w_ffn1_down': out['m_w_ffn1_down'], 'm_g_mix': out['m_g_mix'], 'm_w_in_ab': out['m_w_in_ab'], 'm_conv_w': out['m_conv_w'], 'm_conv_b': out['m_conv_b'], 'm_ln_a_g': out['m_ln_a_g'], 'm_ln_a_b': out['m_ln_a_b'], 'm_ln_v_g': out['m_ln_v_g'], 'm_ln_v_b': out['m_ln_v_b'], 'm_sp_w': out['m_sp_w'], 'm_sp_b': out['m_sp_b'], 'm_w_out_ab': out['m_w_out_ab'], 'm_w_qkv': out['m_w_qkv'], 'm_w_o': out['m_w_o'], 'm_g_ffn2': out['m_g_ffn2'], 'm_w_ffn2_gate': out['m_w_ffn2_gate'], 'm_w_ffn2_up': out['m_w_ffn2_up'], 'm_w_ffn2_down': out['m_w_ffn2_down'], 'm_g_final': out['m_g_final'], 'v_g_ffn1': out['v_g_ffn1'], 'v_w_ffn1_gate': out['v_w_ffn1_gate'], 'v_w_ffn1_up': out['v_w_ffn1_up'], 'v_w_ffn1_down': out['v_w_ffn1_down'], 'v_g_mix': out['v_g_mix'], 'v_w_in_ab': out['v_w_in_ab'], 'v_conv_w': out['v_conv_w'], 'v_conv_b': out['v_conv_b'], 'v_ln_a_g': out['v_ln_a_g'], 'v_ln_a_b': out['v_ln_a_b'], 'v_ln_v_g': out['v_ln_v_g'], 'v_ln_v_b': out['v_ln_v_b'], 'v_sp_w': out['v_sp_w'], 'v_sp_b': out['v_sp_b'], 'v_w_out_ab': out['v_w_out_ab'], 'v_w_qkv': out['v_w_qkv'], 'v_w_o': out['v_w_o'], 'v_g_ffn2': out['v_g_ffn2'], 'v_w_ffn2_gate': out['v_w_ffn2_gate'], 'v_w_ffn2_up': out['v_w_ffn2_up'], 'v_w_ffn2_down': out['v_w_ffn2_down'], 'v_g_final': out['v_g_final']}


def _loss(weights, diff, rest, loss_target):
    with _jax.named_scope("forward"):
        args = {**rest, TWIN_DIFF_INPUT: diff, **{k: w.astype(_WEIGHT_DTYPES[k]) for k, w in weights.items()}}
        y = _forward(args)
    with _jax.named_scope("loss_head"):
        err = _jnp.square(y.astype(_jnp.float32) - loss_target)
        return 0.5 * _jnp.sum(_jnp.mean(err, axis=-1)) if err.ndim else 0.5 * err


def _adamw(w, g, m, v):
    m = ADAM_B1 * m + (1.0 - ADAM_B1) * g
    v = ADAM_B2 * v + (1.0 - ADAM_B2) * _jnp.square(g)
    m_hat = m / (1.0 - ADAM_B1 ** ADAM_STEP)
    v_hat = v / (1.0 - ADAM_B2 ** ADAM_STEP)
    delta = -ADAM_LR * (m_hat / (_jnp.sqrt(v_hat) + ADAM_EPS) + ADAM_WD * w)
    return delta, m, v


def reference(x, g_ffn1, w_ffn1_gate, w_ffn1_up, w_ffn1_down, g_mix, w_in_ab, conv_w, conv_b, ln_a_g, ln_a_b, ln_v_g, ln_v_b, sp_w, sp_b, w_out_ab, w_qkv, w_o, g_ffn2, w_ffn2_gate, w_ffn2_up, w_ffn2_down, g_final, loss_target, m_g_ffn1, m_w_ffn1_gate, m_w_ffn1_up, m_w_ffn1_down, m_g_mix, m_w_in_ab, m_conv_w, m_conv_b, m_ln_a_g, m_ln_a_b, m_ln_v_g, m_ln_v_b, m_sp_w, m_sp_b, m_w_out_ab, m_w_qkv, m_w_o, m_g_ffn2, m_w_ffn2_gate, m_w_ffn2_up, m_w_ffn2_down, m_g_final, v_g_ffn1, v_w_ffn1_gate, v_w_ffn1_up, v_w_ffn1_down, v_g_mix, v_w_in_ab, v_conv_w, v_conv_b, v_ln_a_g, v_ln_a_b, v_ln_v_g, v_ln_v_b, v_sp_w, v_sp_b, v_w_out_ab, v_w_qkv, v_w_o, v_g_ffn2, v_w_ffn2_gate, v_w_ffn2_up, v_w_ffn2_down, v_g_final):
    given = dict(x=x, g_ffn1=g_ffn1, w_ffn1_gate=w_ffn1_gate, w_ffn1_up=w_ffn1_up, w_ffn1_down=w_ffn1_down, g_mix=g_mix, w_in_ab=w_in_ab, conv_w=conv_w, conv_b=conv_b, ln_a_g=ln_a_g, ln_a_b=ln_a_b, ln_v_g=ln_v_g, ln_v_b=ln_v_b, sp_w=sp_w, sp_b=sp_b, w_out_ab=w_out_ab, w_qkv=w_qkv, w_o=w_o, g_ffn2=g_ffn2, w_ffn2_gate=w_ffn2_gate, w_ffn2_up=w_ffn2_up, w_ffn2_down=w_ffn2_down, g_final=g_final, loss_target=loss_target, m_g_ffn1=m_g_ffn1, m_w_ffn1_gate=m_w_ffn1_gate, m_w_ffn1_up=m_w_ffn1_up, m_w_ffn1_down=m_w_ffn1_down, m_g_mix=m_g_mix, m_w_in_ab=m_w_in_ab, m_conv_w=m_conv_w, m_conv_b=m_conv_b, m_ln_a_g=m_ln_a_g, m_ln_a_b=m_ln_a_b, m_ln_v_g=m_ln_v_g, m_ln_v_b=m_ln_v_b, m_sp_w=m_sp_w, m_sp_b=m_sp_b, m_w_out_ab=m_w_out_ab, m_w_qkv=m_w_qkv, m_w_o=m_w_o, m_g_ffn2=m_g_ffn2, m_w_ffn2_gate=m_w_ffn2_gate, m_w_ffn2_up=m_w_ffn2_up, m_w_ffn2_down=m_w_ffn2_down, m_g_final=m_g_final, v_g_ffn1=v_g_ffn1, v_w_ffn1_gate=v_w_ffn1_gate, v_w_ffn1_up=v_w_ffn1_up, v_w_ffn1_down=v_w_ffn1_down, v_g_mix=v_g_mix, v_w_in_ab=v_w_in_ab, v_conv_w=v_conv_w, v_conv_b=v_conv_b, v_ln_a_g=v_ln_a_g, v_ln_a_b=v_ln_a_b, v_ln_v_g=v_ln_v_g, v_ln_v_b=v_ln_v_b, v_sp_w=v_sp_w, v_sp_b=v_sp_b, v_w_out_ab=v_w_out_ab, v_w_qkv=v_w_qkv, v_w_o=v_w_o, v_g_ffn2=v_g_ffn2, v_w_ffn2_gate=v_w_ffn2_gate, v_w_ffn2_up=v_w_ffn2_up, v_w_ffn2_down=v_w_ffn2_down, v_g_final=v_g_final)
    weights = {n: given[n] for n in TWIN_WEIGHTS}
    shared = {n: given[n] for n in SHARED_INPUTS}
    per_example = {n: given[n] for n in ['x']}
    grad_fn = _jax.value_and_grad(_loss, argnums=(0, 1))

    def one_microbatch(ex, loss_target):
        ex = dict(ex)
        diff = ex.pop(TWIN_DIFF_INPUT)
        return grad_fn(weights, diff, {**shared, **ex}, loss_target)

    if N_MICROBATCH == 1:
        loss, (grad_w, grad_x) = one_microbatch(per_example, given["loss_target"])
    else:
        def body(carry, xs):
            loss_sum, grad_sum = carry
            l_k, (gw_k, gx_k) = one_microbatch(xs[0], xs[1])
            with _jax.named_scope("update"):
                return (loss_sum + l_k, _jax.tree.map(_jnp.add, grad_sum, gw_k)), gx_k

        init = (_jnp.zeros((), _jnp.float32), _jax.tree.map(_jnp.zeros_like, weights))
        (loss, grad_w), grad_x = _jax.lax.scan(body, init, (per_example, given["loss_target"]))
    with _jax.named_scope("update"):
        delta_w, new_m, new_v = {}, {}, {}
        for n in TWIN_WEIGHTS:
            delta_w[n], new_m[n], new_v[n] = _adamw(weights[n], grad_w[n], given["m_" + n], given["v_" + n])
    return (loss, grad_x, *[grad_w[n] for n in TWIN_WEIGHTS], *[delta_w[n] for n in TWIN_WEIGHTS],
            *[new_m[n] for n in TWIN_WEIGHTS], *[new_v[n] for n in TWIN_WEIGHTS])
```

```python
import functools

import jax
import jax.numpy as jnp
from jax import lax
from jax.experimental import pallas as pl
from jax.experimental.pallas import tpu as pltpu

F32 = jnp.float32
BF16 = jnp.bfloat16

D_MODEL = 1024
CA = 512
CB = 512
GB = 4
DB = 128
CHUNK = 128
CONV_WIDTH = 31
N_HEADS = 16
HEAD_DIM = 64
EPS = 1e-6
N_DEV = 8
LANES = 128
QB = 128
HALO = 32
CONV_ROWS = 32
ATT_SCALE = HEAD_DIM ** -0.5

ADAM_LR = 0.001
ADAM_B1 = 0.9
ADAM_B2 = 0.999
ADAM_EPS = 1e-08
ADAM_WD = 0.01
ADAM_STEP = 10

NT = (((1,), (1,)), ((), ()))
NN = (((1,), (0,)), ((), ()))
TN = (((0,), (0,)), ((), ()))
MESH = pl.DeviceIdType.MESH
ANY = pl.BlockSpec(memory_space=pl.ANY)
VMEM_LIMIT = 56 * 1024 * 1024


def _dot(a, b, dims):
    return lax.dot_general(a, b, dims, preferred_element_type=F32)


def _cp(*sem):
    return pltpu.CompilerParams(dimension_semantics=sem, vmem_limit_bytes=VMEM_LIMIT)


def _tile(n, want):
    if n <= want:
        return n
    t = want - want % LANES
    while t > LANES and n % t:
        t -= LANES
    assert n % t == 0, (n, want)
    return t


def _sigmoid(x):
    return 1.0 / (1.0 + jnp.exp(-x))


def _rstd(x):
    return lax.rsqrt(jnp.mean(x * x, axis=-1, keepdims=True) + EPS)


def _rms_bwd(x, g, dh):
    r = _rstd(x)
    u = dh * g
    dx = r * (u - x * (r * r) * jnp.mean(u * x, axis=-1, keepdims=True))
    dg = jnp.sum(dh * x * r, axis=0, keepdims=True)
    return dx, dg


def _ln_fwd(x, g, b):
    mu = jnp.mean(x, axis=-1, keepdims=True)
    xc = x - mu
    r = lax.rsqrt(jnp.mean(xc * xc, axis=-1, keepdims=True) + EPS)
    xh = xc * r
    return xh * g + b, xh, r


def _ln_bwd(dy, xh, r, g):
    dxh = dy * g
    return r * (dxh - jnp.mean(dxh, axis=-1, keepdims=True)
                - xh * jnp.mean(dxh * xh, axis=-1, keepdims=True))


def _ffn_fwd(x, g, wall, base, name):
    T, D = x.shape
    F = wall.shape[1]
    tm, tn = _tile(T, 512), _tile(F, 256)
    n_j = F // tn

    def body(x_ref, g_ref, wg_ref, wu_ref, wd_ref, xo_ref, a_ref, b_ref, h_ref, acc_ref):
        j = pl.program_id(1)

        @pl.when(j == 0)
        def _():
            xv = x_ref[...]
            h_ref[...] = (xv * _rstd(xv) * g_ref[...]).astype(BF16)
            acc_ref[...] = jnp.zeros_like(acc_ref)

        h = h_ref[...]
        a = _dot(h, wg_ref[...], NT)
        b = _dot(h, wu_ref[...], NT)
        a_ref[...] = a.astype(BF16)
        b_ref[...] = b.astype(BF16)
        s = (a * _sigmoid(a) * b).astype(BF16)
        acc_ref[...] += _dot(s, wd_ref[...], NN)

        @pl.when(j == n_j - 1)
        def _():
            xo_ref[...] = x_ref[...] + 0.5 * acc_ref[...]

    wspec = lambda k: pl.BlockSpec((None, tn, D), lambda i, j: (base + k, j, 0))
    return pl.pallas_call(
        body, name=name, grid=(T // tm, n_j),
        in_specs=[pl.BlockSpec((tm, D), lambda i, j: (i, 0)), pl.BlockSpec((1, D), lambda i, j: (0, 0)),
                  wspec(0), wspec(1), wspec(2)],
        out_specs=[pl.BlockSpec((tm, D), lambda i, j: (i, 0)), pl.BlockSpec((tm, tn), lambda i, j: (i, j)),
                   pl.BlockSpec((tm, tn), lambda i, j: (i, j)), pl.BlockSpec((tm, D), lambda i, j: (i, 0))],
        out_shape=[jax.ShapeDtypeStruct((T, D), F32), jax.ShapeDtypeStruct((T, F), BF16),
                   jax.ShapeDtypeStruct((T, F), BF16), jax.ShapeDtypeStruct((T, D), BF16)],
        scratch_shapes=[pltpu.VMEM((tm, D), F32)],
        compiler_params=_cp("parallel", "arbitrary"),
    )(x, g, wall, wall, wall)


def _ffn_bwd(go, x, g, a, b, wall, base, name):
    T, D = x.shape
    F = wall.shape[1]
    tm, tn = _tile(T, 512), _tile(F, 256)
    n_j = F // tn

    def body(go_ref, x_ref, g_ref, a_ref, b_ref, wg_ref, wu_ref, wd_ref,
             gx_ref, dg_ref, da_ref, db_ref, s_ref, gh_ref, acc_ref):
        i, j = pl.program_id(0), pl.program_id(1)

        @pl.when(j == 0)
        def _():
            gh_ref[...] = (0.5 * go_ref[...]).astype(BF16)
            acc_ref[...] = jnp.zeros_like(acc_ref)

        @pl.when((i == 0) & (j == 0))
        def _():
            dg_ref[...] = jnp.zeros_like(dg_ref)

        ds = _dot(gh_ref[...], wd_ref[...], NT)
        av = a_ref[...].astype(F32)
        bv = b_ref[...].astype(F32)
        sig = _sigmoid(av)
        sl = av * sig
        dab = (ds * bv * (sig * (1.0 + av * (1.0 - sig)))).astype(BF16)
        dbb = (ds * sl).astype(BF16)
        s_ref[...] = (sl * bv).astype(BF16)
        da_ref[...] = dab
        db_ref[...] = dbb
        acc_ref[...] += _dot(dab, wg_ref[...], NN) + _dot(dbb, wu_ref[...], NN)

        @pl.when(j == n_j - 1)
        def _():
            dx, dg = _rms_bwd(x_ref[...], g_ref[...], acc_ref[...])
            gx_ref[...] = go_ref[...] + dx
            dg_ref[...] += dg

    wspec = lambda k: pl.BlockSpec((None, tn, D), lambda i, j: (base + k, j, 0))
    row = pl.BlockSpec((tm, D), lambda i, j: (i, 0))
    hid = pl.BlockSpec((tm, tn), lambda i, j: (i, j))
    vec = pl.BlockSpec((1, D), lambda i, j: (0, 0))
    return pl.pallas_call(
        body, name=name, grid=(T // tm, n_j),
        in_specs=[row, row, vec, hid, hid, wspec(0), wspec(1), wspec(2)],
        out_specs=[row, vec, hid, hid, hid, row],
        out_shape=[jax.ShapeDtypeStruct((T, D), F32), jax.ShapeDtypeStruct((1, D), F32),
                   jax.ShapeDtypeStruct((T, F), BF16), jax.ShapeDtypeStruct((T, F), BF16),
                   jax.ShapeDtypeStruct((T, F), BF16), jax.ShapeDtypeStruct((T, D), BF16)],
        scratch_shapes=[pltpu.VMEM((tm, D), F32)],
        compiler_params=_cp("arbitrary", "arbitrary"),
    )(go, x, g, a, b, wall, wall, wall)


def _mm_tn(a, b, name):
    T, M = a.shape
    N = b.shape[1]
    tmm, tk = _tile(M, 1536), _tile(T, 512)

    def body(a_ref, b_ref, o_ref):
        @pl.when(pl.program_id(1) == 0)
        def _():
            o_ref[...] = jnp.zeros_like(o_ref)

        o_ref[...] += _dot(a_ref[...].astype(BF16), b_ref[...].astype(BF16), TN)

    return pl.pallas_call(
        body, name=name, grid=(M // tmm, T // tk),
        in_specs=[pl.BlockSpec((tk, tmm), lambda m, k: (k, m)), pl.BlockSpec((tk, N), lambda m, k: (k, 0))],
        out_specs=pl.BlockSpec((tmm, N), lambda m, k: (m, 0)),
        out_shape=jax.ShapeDtypeStruct((M, N), F32),
        compiler_params=_cp("parallel", "arbitrary"),
    )(a, b)


def _mm_nt(x, wt, g, out_dtype, name):
    T, K = x.shape
    N = wt.shape[0]
    tm, tn = _tile(T, 512), _tile(N, 512)
    norm = g is not None

    def body(*refs):
        if norm:
            x_ref, g_ref, w_ref, o_ref, h_ref = refs
        else:
            x_ref, w_ref, o_ref, h_ref = refs

        @pl.when(pl.program_id(1) == 0)
        def _():
            xv = x_ref[...].astype(F32)
            if norm:
                xv = xv * _rstd(xv) * g_ref[...]
            h_ref[...] = xv.astype(BF16)

        o_ref[...] = _dot(h_ref[...], w_ref[...], NT).astype(out_dtype)

    row = pl.BlockSpec((tm, K), lambda i, j: (i, 0))
    wsp = pl.BlockSpec((tn, K), lambda i, j: (j, 0))
    osp = pl.BlockSpec((tm, tn), lambda i, j: (i, j))
    if norm:
        return pl.pallas_call(
            body, name=name, grid=(T // tm, N // tn),
            in_specs=[row, pl.BlockSpec((1, K), lambda i, j: (0, 0)), wsp],
            out_specs=[osp, row],
            out_shape=[jax.ShapeDtypeStruct((T, N), out_dtype), jax.ShapeDtypeStruct((T, K), BF16)],
            compiler_params=_cp("parallel", "arbitrary"),
        )(x, g, wt)
    return pl.pallas_call(
        body, name=name, grid=(T // tm, N // tn),
        in_specs=[row, wsp], out_specs=osp,
        out_shape=jax.ShapeDtypeStruct((T, N), out_dtype),
        scratch_shapes=[pltpu.VMEM((tm, K), BF16)],
        compiler_params=_cp("parallel", "arbitrary"),
    )(x, wt)


def _mm_nn_res(act, w, resid, name):
    T, K = act.shape
    D = w.shape[1]
    tm = _tile(T, 512)

    def body(a_ref, w_ref, r_ref, o_ref):
        o_ref[...] = r_ref[...] + _dot(a_ref[...].astype(BF16), w_ref[...], NN)

    return pl.pallas_call(
        body, name=name, grid=(T // tm,),
        in_specs=[pl.BlockSpec((tm, K), lambda i: (i, 0)), pl.BlockSpec((K, D), lambda i: (0, 0)),
                  pl.BlockSpec((tm, D), lambda i: (i, 0))],
        out_specs=pl.BlockSpec((tm, D), lambda i: (i, 0)),
        out_shape=jax.ShapeDtypeStruct((T, D), F32),
        compiler_params=_cp("parallel"),
    )(act, w, resid)


def _mm_nn_rmsbwd(act, w, x, g, gprev, name):
    T, K = act.shape
    D = w.shape[1]
    tm = _tile(T, 512)

    def body(a_ref, w_ref, x_ref, g_ref, gp_ref, o_ref, dg_ref):
        @pl.when(pl.program_id(0) == 0)
        def _():
            dg_ref[...] = jnp.zeros_like(dg_ref)

        dh = _dot(a_ref[...].astype(BF16), w_ref[...], NN)
        dx, dg = _rms_bwd(x_ref[...], g_ref[...], dh)
        o_ref[...] = gp_ref[...] + dx
        dg_ref[...] += dg

    row = pl.BlockSpec((tm, D), lambda i: (i, 0))
    vec = pl.BlockSpec((1, D), lambda i: (0, 0))
    return pl.pallas_call(
        body, name=name, grid=(T // tm,),
        in_specs=[pl.BlockSpec((tm, K), lambda i: (i, 0)), pl.BlockSpec((K, D), lambda i: (0, 0)), row, vec, row],
        out_specs=[row, vec],
        out_shape=[jax.ShapeDtypeStruct((T, D), F32), jax.ShapeDtypeStruct((1, D), F32)],
        compiler_params=_cp("arbitrary"),
    )(act, w, x, g, gprev)


def _loss_head(x, g, target):
    T, D = x.shape
    tm = _tile(T, 512)

    def body(x_ref, g_ref, t_ref, dx_ref, loss_ref, dg_ref):
        @pl.when(pl.program_id(0) == 0)
        def _():
            loss_ref[...] = jnp.zeros_like(loss_ref)
            dg_ref[...] = jnp.zeros_like(dg_ref)

        xv = x_ref[...]
        gv = g_ref[...]
        e = xv * _rstd(xv) * gv - t_ref[...]
        per_tok = jnp.sum(e * e, axis=-1, keepdims=True) * (1.0 / D)
        loss_ref[...] += 0.5 * jnp.sum(per_tok, axis=0, keepdims=True)
        dx, dg = _rms_bwd(xv, gv, e * (1.0 / D))
        dx_ref[...] = dx
        dg_ref[...] += dg

    row = pl.BlockSpec((tm, D), lambda i: (i, 0))
    vec = pl.BlockSpec((1, D), lambda i: (0, 0))
    return pl.pallas_call(
        body, name="loss_head", grid=(T // tm,),
        in_specs=[row, vec, row],
        out_specs=[row, pl.BlockSpec((1, LANES), lambda i: (0, 0)), vec],
        out_shape=[jax.ShapeDtypeStruct((T, D), F32), jax.ShapeDtypeStruct((1, LANES), F32),
                   jax.ShapeDtypeStruct((1, D), F32)],
        compiler_params=_cp("arbitrary"),
    )(x, g, target)


def _softplus(z):
    return jnp.maximum(z, 0.0) + jnp.log(1.0 + jnp.exp(-jnp.abs(z)))


def _cumsum_mm(v, u):
    hi = v.astype(BF16)
    lo = (v - hi.astype(F32)).astype(BF16)
    return _dot(hi, u, NN) + _dot(lo, u, NN)


def _half_rowsum(v):
    s0 = jnp.sum(v[:, :QB], axis=1, keepdims=True)
    s1 = jnp.sum(v[:, QB:], axis=1, keepdims=True)
    return jnp.concatenate([jnp.broadcast_to(s0, (QB, QB)), jnp.broadcast_to(s1, (QB, QB))], axis=1)


def _stack_heads(src_ref, dst_ref, n_blk):
    m0 = lax.broadcasted_iota(jnp.int32, (1, LANES), 1) < HEAD_DIM

    def fill(c, carry):
        blk = src_ref[pl.ds(pl.multiple_of(c * QB, QB), QB), :]
        zero = jnp.zeros_like(blk)
        dst_ref[c, 0:QB, :] = jnp.where(m0, blk, zero)
        dst_ref[c, QB:2 * QB, :] = jnp.where(m0, zero, blk)
        return carry

    lax.fori_loop(0, n_blk, fill, 0)


def _causal_diff():
    row = lax.broadcasted_iota(jnp.int32, (QB, 2 * QB), 0)
    col = lax.broadcasted_iota(jnp.int32, (QB, 2 * QB), 1)
    return row - (col & (QB - 1))


def _tri_blockdiag(upper):
    r = lax.broadcasted_iota(jnp.int32, (2 * QB, 2 * QB), 0)
    c = lax.broadcasted_iota(jnp.int32, (2 * QB, 2 * QB), 1)
    same = (r // QB) == (c // QB)
    return (same & ((r > c) if upper else (r < c))).astype(BF16)


def _attn_fwd(qkv, n_seq):
    T = qkv.shape[0]
    S = T // n_seq
    n_q = S // QB
    n_p = D_MODEL // LANES
    u_suffix = _tri_blockdiag(True)

    def body(q_ref, k_ref, v_ref, u_ref, o_ref, tot_ref, kk_ref, vv_ref):
        qi = pl.program_id(2)

        @pl.when(qi == 0)
        def _():
            _stack_heads(k_ref, kk_ref, n_q)
            _stack_heads(v_ref, vv_ref, n_q)

        q = q_ref[...]
        u = u_ref[...]
        diff = _causal_diff()

        def step(it, carry):
            lr, acc = carry
            kj = qi - it
            z = _dot(q, kk_ref[kj], NT) * ATT_SCALE
            mask = (kj - qi) * QB < diff
            sp = _softplus(z)
            lk = jnp.where(mask, -sp, 0.0)
            log_a = (z - sp) + _cumsum_mm(lk, u) + lr
            a = jnp.where(mask, jnp.exp(log_a), 0.0).astype(BF16)
            return lr + _half_rowsum(lk), acc + _dot(a, vv_ref[kj], NN)

        lr, acc = lax.fori_loop(0, qi + 1, step,
                                (jnp.zeros((QB, 2 * QB), F32), jnp.zeros((QB, LANES), F32)))
        o_ref[...] = acc.astype(BF16)
        tot_ref[...] = lr

    return pl.pallas_call(
        body, name="attn_fwd", grid=(n_seq, n_p, n_q),
        in_specs=[pl.BlockSpec((QB, LANES), lambda b, p, qi: (b * n_q + qi, p)),
                  pl.BlockSpec((S, LANES), lambda b, p, qi: (b, n_p + p)),
                  pl.BlockSpec((S, LANES), lambda b, p, qi: (b, 2 * n_p + p)),
                  pl.BlockSpec((2 * QB, 2 * QB), lambda b, p, qi: (0, 0))],
        out_specs=[pl.BlockSpec((QB, LANES), lambda b, p, qi: (b * n_q + qi, p)),
                   pl.BlockSpec((QB, 2 * QB), lambda b, p, qi: (b * n_q + qi, p))],
        out_shape=[jax.ShapeDtypeStruct((T, D_MODEL), BF16), jax.ShapeDtypeStruct((T, 2 * D_MODEL), F32)],
        scratch_shapes=[pltpu.VMEM((n_q, 2 * QB, LANES), BF16), pltpu.VMEM((n_q, 2 * QB, LANES), BF16)],
        compiler_params=_cp("parallel", "parallel", "arbitrary"),
    )(qkv, qkv, qkv, u_suffix)


def _attn_bwd(qkv, do, tot, n_seq):
    T = qkv.shape[0]
    S = T // n_seq
    n_q = S // QB
    n_p = D_MODEL // LANES
    u_prefix = _tri_blockdiag(False)

    def body(q_ref, k_ref, v_ref, do_ref, tot_ref, u_ref, dq_ref, dk_ref, dv_ref, kk_ref, vv_ref):
        qi = pl.program_id(2)

        @pl.when(qi == 0)
        def _():
            _stack_heads(k_ref, kk_ref, n_q)
            _stack_heads(v_ref, vv_ref, n_q)
            dk_ref[...] = jnp.zeros_like(dk_ref)
            dv_ref[...] = jnp.zeros_like(dv_ref)

        q = q_ref[...]
        dov = do_ref[...]
        tot2 = tot_ref[...]
        u = u_ref[...]
        diff = _causal_diff()
        m0 = lax.broadcasted_iota(jnp.int32, (1, LANES), 1) < HEAD_DIM

        def step(kj, carry):
            cl, cg, dq = carry
            kk = kk_ref[kj]
            z = _dot(q, kk, NT) * ATT_SCALE
            mask = (kj - qi) * QB < diff
            sp = _softplus(z)
            lk = jnp.where(mask, -sp, 0.0)
            ls = z - sp
            rest = tot2 - cl - (_cumsum_mm(lk, u) + lk)
            a = jnp.where(mask, jnp.exp(ls + rest), 0.0)
            g = a * _dot(dov, vv_ref[kj], NT)
            pex = _cumsum_mm(g, u) + cg
            beta = jnp.exp(ls)
            dz = (jnp.where(mask, g * (1.0 - beta) - pex * beta, 0.0) * ATT_SCALE).astype(BF16)
            rows = pl.ds(pl.multiple_of(kj * QB, QB), QB)
            dvt = _dot(a.astype(BF16), dov, TN)
            dv_ref[rows, :] += jnp.where(m0, dvt[:QB], dvt[QB:])
            dkt = _dot(dz, q, TN)
            dk_ref[rows, :] += jnp.where(m0, dkt[:QB], dkt[QB:])
            return cl + _half_rowsum(lk), cg + _half_rowsum(g), dq + _dot(dz, kk, NN)

        zero2 = jnp.zeros((QB, 2 * QB), F32)
        _, _, dq = lax.fori_loop(0, qi + 1, step, (zero2, zero2, jnp.zeros((QB, LANES), F32)))
        dq_ref[...] = dq

    qspec = pl.BlockSpec((QB, LANES), lambda b, p, qi: (b * n_q + qi, p))
    seq = lambda off: pl.BlockSpec((S, LANES), lambda b, p, qi: (b, off + p))
    return pl.pallas_call(
        body, name="attn_bwd", grid=(n_seq, n_p, n_q),
        in_specs=[qspec, seq(n_p), seq(2 * n_p), qspec,
                  pl.BlockSpec((QB, 2 * QB), lambda b, p, qi: (b * n_q + qi, p)),
                  pl.BlockSpec((2 * QB, 2 * QB), lambda b, p, qi: (0, 0))],
        out_specs=[qspec, seq(0), seq(0)],
        out_shape=[jax.ShapeDtypeStruct((T, D_MODEL), F32)] * 3,
        scratch_shapes=[pltpu.VMEM((n_q, 2 * QB, LANES), BF16), pltpu.VMEM((n_q, 2 * QB, LANES), BF16)],
        compiler_params=_cp("parallel", "parallel", "arbitrary"),
    )(qkv, qkv, qkv, do, tot, u_prefix)


def _glu_with_halo(av_ref, ag_ref, avh_ref, agh_ref, a0_s, first, ts):
    hal = avh_ref[...] * _sigmoid(agh_ref[...])
    a0_s[0:HALO, :] = jnp.where(first, 0.0, hal)
    a0_s[HALO:HALO + ts, :] = av_ref[...] * _sigmoid(ag_ref[...])


def _mix_specs(ts, n_r, with_left):
    blk = lambda c: pl.BlockSpec((ts, CA), lambda b, r: (b * n_r + r, c))
    per = ts // HALO
    left = lambda c: pl.BlockSpec((HALO, CA), lambda b, r: (jnp.maximum((b * n_r + r) * per - 1, 0), c))
    return blk, (left if with_left else None)


def _mix_fwd(z, conv_w, conv_b, ln_a_g, ln_a_b, ln_v_g, ln_v_b, ws, bias2d, n_seq):
    T = z.shape[0]
    S = T // n_seq
    ts = _tile(S, 512)
    n_r = S // ts
    shift = HALO - (CONV_WIDTH - 1)

    def body(av_ref, ag_ref, avh_ref, agh_ref, u_ref, v_ref, cw_ref, cb_ref, lag_ref, lab_ref,
             lvg_ref, lvb_ref, ws_ref, bias_ref, cat_ref, a1_ref, a0_s):
        _glu_with_halo(av_ref, ag_ref, avh_ref, agh_ref, a0_s, pl.program_id(1) == 0, ts)
        for rb in range(ts // CONV_ROWS):
            base = rb * CONV_ROWS
            acc = jnp.broadcast_to(cb_ref[...], (CONV_ROWS, CA))
            for k in range(CONV_WIDTH):
                acc = acc + cw_ref[k:k + 1, :] * a0_s[base + shift + k:base + shift + k + CONV_ROWS, :]
            a1_ref[base:base + CONV_ROWS, :] = acc
        y, _, _ = _ln_fwd(a1_ref[...], lag_ref[...], lab_ref[...])
        cat_ref[:, 0:CA] = (y * _sigmoid(y)).astype(BF16)
        for gi in range(GB):
            sl = slice(gi * DB, (gi + 1) * DB)
            v1, _, _ = _ln_fwd(v_ref[:, sl], lvg_ref[:, sl], lvb_ref[:, sl])
            v1 = v1.astype(BF16)
            for c in range(ts // CHUNK):
                rs = slice(c * CHUNK, (c + 1) * CHUNK)
                v2 = _dot(ws_ref[gi], v1[rs], NN) + bias_ref[:, sl]
                cat_ref[rs, CA + gi * DB:CA + (gi + 1) * DB] = (u_ref[rs, sl] * v2).astype(BF16)

    blk, left = _mix_specs(ts, n_r, True)
    vec = pl.BlockSpec((1, CA), lambda b, r: (0, 0))
    return pl.pallas_call(
        body, name="mix_fwd", grid=(n_seq, n_r),
        in_specs=[blk(0), blk(1), left(0), left(1), blk(2), blk(3),
                  pl.BlockSpec((CONV_WIDTH, CA), lambda b, r: (0, 0)), vec, vec, vec, vec, vec,
                  pl.BlockSpec((GB, CHUNK, CHUNK), lambda b, r: (0, 0, 0)),
                  pl.BlockSpec((CHUNK, CB), lambda b, r: (0, 0))],
        out_specs=[pl.BlockSpec((ts, CA + CB), lambda b, r: (b * n_r + r, 0)), blk(0)],
        out_shape=[jax.ShapeDtypeStruct((T, CA + CB), BF16), jax.ShapeDtypeStruct((T, CA), F32)],
        scratch_shapes=[pltpu.VMEM((HALO + ts, CA), F32)],
        compiler_params=_cp("parallel", "parallel"),
    )(z, z, z, z, z, z, conv_w, conv_b, ln_a_g, ln_a_b, ln_v_g, ln_v_b, ws, bias2d)


def _mix_bwd_rows(dcat, z, a1, ln_a_g, ln_a_b, ln_v_g, ln_v_b, ws, ws_t, bias2d, n_seq):
    T = z.shape[0]
    S = T // n_seq
    ts = _tile(S, 512)
    n_r = S // ts

    def body(dc_ref, u_ref, v_ref, a1_ref, lag_ref, lab_ref, lvg_ref, lvb_ref, ws_ref, wst_ref, bias_ref,
             da1_ref, dz_ref, dlag_ref, dlab_ref, dlvg_ref, dlvb_ref, dws_ref, dsb_ref, dv1_s, dbias_s):
        first = (pl.program_id(0) == 0) & (pl.program_id(1) == 0)
        last = (pl.program_id(0) == n_seq - 1) & (pl.program_id(1) == n_r - 1)

        @pl.when(first)
        def _():
            for ref in (dlag_ref, dlab_ref, dlvg_ref, dlvb_ref, dws_ref, dbias_s):
                ref[...] = jnp.zeros_like(ref)

        lag = lag_ref[...]
        y, xh, r = _ln_fwd(a1_ref[...], lag, lab_ref[...])
        sig = _sigmoid(y)
        dy = dc_ref[:, 0:CA] * (sig * (1.0 + y * (1.0 - sig)))
        dlag_ref[...] += jnp.sum(dy * xh, axis=0, keepdims=True)
        dlab_ref[...] += jnp.sum(dy, axis=0, keepdims=True)
        da1_ref[...] = _ln_bwd(dy, xh, r, lag)

        tril = (lax.broadcasted_iota(jnp.int32, (CHUNK, CHUNK), 0)
                >= lax.broadcasted_iota(jnp.int32, (CHUNK, CHUNK), 1))
        for gi in range(GB):
            sl = slice(gi * DB, (gi + 1) * DB)
            lvg = lvg_ref[:, sl]
            v1, vh, vr = _ln_fwd(v_ref[:, sl], lvg, lvb_ref[:, sl])
            v1 = v1.astype(BF16)
            for c in range(ts // CHUNK):
                rs = slice(c * CHUNK, (c + 1) * CHUNK)
                v2 = _dot(ws_ref[gi], v1[rs], NN) + bias_ref[:, sl]
                dbo = dc_ref[rs, CA + gi * DB:CA + (gi + 1) * DB]
                dz_ref[rs, sl] = (dbo * v2).astype(BF16)
                dv2 = dbo * u_ref[rs, sl]
                dbias_s[:, sl] += dv2
                dv2b = dv2.astype(BF16)
                dws_ref[gi] += jnp.where(tril, _dot(dv2b, v1[rs], NT), 0.0)
                dv1_s[rs, :] = _dot(wst_ref[gi], dv2b, NN)
            dv1 = dv1_s[...]
            dlvg_ref[:, sl] += jnp.sum(dv1 * vh, axis=0, keepdims=True)
            dlvb_ref[:, sl] += jnp.sum(dv1, axis=0, keepdims=True)
            dz_ref[:, CB + gi * DB:CB + (gi + 1) * DB] = _ln_bwd(dv1, vh, vr, lvg).astype(BF16)

        @pl.when(last)
        def _():
            col = lax.broadcasted_iota(jnp.int32, (CHUNK, GB), 1)
            out = jnp.zeros((CHUNK, GB), F32)
            for gi in range(GB):
                s = jnp.sum(dbias_s[:, gi * DB:(gi + 1) * DB], axis=1, keepdims=True)
                out = out + jnp.where(col == gi, s, 0.0)
            dsb_ref[...] = out

    blk, _ = _mix_specs(ts, n_r, False)
    vec = pl.BlockSpec((1, CA), lambda b, r: (0, 0))
    mat = pl.BlockSpec((GB, CHUNK, CHUNK), lambda b, r: (0, 0, 0))
    wide = pl.BlockSpec((ts, CA + CB), lambda b, r: (b * n_r + r, 0))
    return pl.pallas_call(
        body, name="mix_bwd_rows", grid=(n_seq, n_r),
        in_specs=[wide, blk(2), blk(3), blk(0), vec, vec, vec, vec, mat, mat,
                  pl.BlockSpec((CHUNK, CB), lambda b, r: (0, 0))],
        out_specs=[blk(0), wide, vec, vec, vec, vec, mat, pl.BlockSpec((CHUNK, GB), lambda b, r: (0, 0))],
        out_shape=[jax.ShapeDtypeStruct((T, CA), F32), jax.ShapeDtypeStruct((T, 2 * CB), BF16)]
        + [jax.ShapeDtypeStruct((1, CA), F32)] * 4
        + [jax.ShapeDtypeStruct((GB, CHUNK, CHUNK), F32), jax.ShapeDtypeStruct((CHUNK, GB), F32)],
        scratch_shapes=[pltpu.VMEM((ts, DB), F32), pltpu.VMEM((CHUNK, CB), F32)],
        compiler_params=_cp("arbitrary", "arbitrary"),
    )(dcat, z, z, a1, ln_a_g, ln_a_b, ln_v_g, ln_v_b, ws, ws_t, bias2d)


def _mix_bwd_conv(da1, z, conv_w, n_seq):
    T = z.shape[0]
    S = T // n_seq
    ts = _tile(S, 512)
    n_r = S // ts
    per = ts // HALO
    shift = HALO - (CONV_WIDTH - 1)
    fold = CONV_ROWS // 8

    def body(d_ref, dh_ref, av_ref, ag_ref, avh_ref, agh_ref, cw_ref,
             dz_ref, dcw_ref, dcb_ref, a0_s, d1_s, da0_s, dw8_s):
        first = (pl.program_id(0) == 0) & (pl.program_id(1) == 0)
        last = (pl.program_id(0) == n_seq - 1) & (pl.program_id(1) == n_r - 1)

        @pl.when(first)
        def _():
            dw8_s[...] = jnp.zeros_like(dw8_s)
            dcb_ref[...] = jnp.zeros_like(dcb_ref)

        _glu_with_halo(av_ref, ag_ref, avh_ref, agh_ref, a0_s, pl.program_id(1) == 0, ts)
        d1_s[0:ts, :] = d_ref[...]
        d1_s[ts:ts + HALO, :] = jnp.where(pl.program_id(1) == n_r - 1, 0.0, dh_ref[...])
        dcb_ref[...] += jnp.sum(d_ref[...], axis=0, keepdims=True)
        for rb in range(ts // CONV_ROWS):
            base = rb * CONV_ROWS
            dcur = d1_s[base:base + CONV_ROWS, :]
            acc = jnp.zeros((CONV_ROWS, CA), F32)
            for k in range(CONV_WIDTH):
                back = CONV_WIDTH - 1 - k
                acc = acc + cw_ref[k:k + 1, :] * d1_s[base + back:base + back + CONV_ROWS, :]
                prod = dcur * a0_s[base + shift + k:base + shift + k + CONV_ROWS, :]
                part = prod[0:8]
                for f in range(1, fold):
                    part = part + prod[8 * f:8 * f + 8]
                dw8_s[k] += part
            da0_s[base:base + CONV_ROWS, :] = acc
        da0 = da0_s[...]
        sig = _sigmoid(ag_ref[...])
        dz_ref[:, 0:CA] = (da0 * sig).astype(BF16)
        dz_ref[:, CA:2 * CA] = (da0 * av_ref[...] * sig * (1.0 - sig)).astype(BF16)

        @pl.when(last)
        def _():
            for k in range(CONV_WIDTH):
                dcw_ref[k:k + 1, :] = jnp.sum(dw8_s[k], axis=0, keepdims=True)

    blk, left = _mix_specs(ts, n_r, True)
    n_halo_blocks = T // HALO
    right = pl.BlockSpec((HALO, CA), lambda b, r: (jnp.minimum((b * n_r + r + 1) * per, n_halo_blocks - 1), 0))
    return pl.pallas_call(
        body, name="mix_bwd_conv", grid=(n_seq, n_r),
        in_specs=[blk(0), right, blk(0), blk(1), left(0), left(1),
                  pl.BlockSpec((CONV_WIDTH, CA), lambda b, r: (0, 0))],
        out_specs=[pl.BlockSpec((ts, 2 * CA), lambda b, r: (b * n_r + r, 0)),
                   pl.BlockSpec((CONV_WIDTH, CA), lambda b, r: (0, 0)), pl.BlockSpec((1, CA), lambda b, r: (0, 0))],
        out_shape=[jax.ShapeDtypeStruct((T, 2 * CA), BF16), jax.ShapeDtypeStruct((CONV_WIDTH, CA), F32),
                   jax.ShapeDtypeStruct((1, CA), F32)],
        scratch_shapes=[pltpu.VMEM((HALO + ts, CA), F32), pltpu.VMEM((ts + HALO, CA), F32),
                        pltpu.VMEM((ts, CA), F32), pltpu.VMEM((CONV_WIDTH, 8, CA), F32)],
        compiler_params=_cp("arbitrary", "arbitrary"),
    )(da1, da1, z, z, z, z, conv_w)


def _sum_parts(parts, name):
    n, R, C = parts.shape
    tr = _tile(R, 512) if R % 8 == 0 and R > 512 else R
    if R % tr:
        tr = R

    def body(p_ref, o_ref):
        acc = p_ref[0]
        for k in range(1, n):
            acc = acc + p_ref[k]
        o_ref[...] = acc

    return pl.pallas_call(
        body, name=name, grid=(R // tr,),
        in_specs=[pl.BlockSpec((n, tr, C), lambda i: (0, i, 0))],
        out_specs=pl.BlockSpec((tr, C), lambda i: (i, 0)),
        out_shape=jax.ShapeDtypeStruct((R, C), F32),
        compiler_params=_cp("parallel"),
    )(parts)


def _row_tile(R, want):
    t = min(R, want)
    t -= t % 8
    while t > 8 and R % t:
        t -= 8
    return t if t >= 8 and R % t == 0 else R


def _adamw(w, g, m, v, name):
    R, C = w.shape
    tr = _row_tile(R, 256)
    c1 = 1.0 - ADAM_B1 ** ADAM_STEP
    c2 = 1.0 - ADAM_B2 ** ADAM_STEP

    def body(w_ref, g_ref, m_ref, v_ref, d_ref, nm_ref, nv_ref):
        gv = g_ref[...]
        nm = ADAM_B1 * m_ref[...] + (1.0 - ADAM_B1) * gv
        nv = ADAM_B2 * v_ref[...] + (1.0 - ADAM_B2) * (gv * gv)
        d_ref[...] = -ADAM_LR * ((nm / c1) / (jnp.sqrt(nv / c2) + ADAM_EPS) + ADAM_WD * w_ref[...])
        nm_ref[...] = nm
        nv_ref[...] = nv

    blk = pl.BlockSpec((tr, C), lambda i: (i, 0))
    return pl.pallas_call(
        body, name=name, grid=(R // tr,),
        in_specs=[blk] * 4, out_specs=[blk] * 3,
        out_shape=[jax.ShapeDtypeStruct((R, C), F32)] * 3,
        compiler_params=_cp("parallel"),
    )(w, g, m, v)


def _me():
    return lax.axis_index("x"), lax.axis_index("y"), lax.axis_index("c")


def _block_rows(ref, dev, n):
    start = (4 * dev[0] + 2 * dev[1] + dev[2]) * n
    if len(ref.shape) == 2:
        return ref.at[pl.ds(start, n), :]
    return ref.at[:, pl.ds(start, n), :]


def _all_gather(shards):
    na = len(shards)
    ns = [s.shape[-2] for s in shards]

    def body(*refs):
        ins, outs = refs[:na], refs[na:2 * na]
        send_sems, recv_sems, local_sems = refs[2 * na:]
        x, y, c = _me()
        me, sibling = (x, y, c), (x, y, 1 - c)
        chips = [(1 - x, y), (x, 1 - y), (1 - x, 1 - y)]

        def copy(a, k, block, to, src=None):
            dst = _block_rows(outs[a], block, ns[a])
            return pltpu.make_async_remote_copy(
                src_ref=dst if src is None else src, dst_ref=dst,
                send_sem=send_sems.at[a, k], recv_sem=recv_sems.at[a, k], device_id=to, device_id_type=MESH)

        mine = [pltpu.make_async_copy(ins[a], _block_rows(outs[a], me, ns[a]), local_sems.at[a]) for a in range(na)]
        for cp in mine:
            cp.start()
        first = []
        for a in range(na):
            first.append(copy(a, 0, me, sibling, src=ins[a]))
            first += [copy(a, 1 + j, me, (*chip, c), src=ins[a]) for j, chip in enumerate(chips)]
        for cp in first:
            cp.start()
        passed = []
        for j, chip in enumerate(chips):
            for a in range(na):
                copy(a, 1 + j, (*chip, c), me).wait_recv()
                fwd = copy(a, 4 + j, (*chip, c), sibling)
                fwd.start()
                passed.append(fwd)
        for a in range(na):
            copy(a, 0, sibling, me).wait_recv()
            for j, chip in enumerate(chips):
                copy(a, 4 + j, (*chip, 1 - c), me).wait_recv()
        for cp in first + passed:
            cp.wait_send()
        for cp in mine:
            cp.wait()

    out_shape = [jax.ShapeDtypeStruct(s.shape[:-2] + (N_DEV * s.shape[-2], s.shape[-1]), s.dtype) for s in shards]
    return pl.pallas_call(
        body, name="weights_all_gather",
        in_specs=[ANY] * na, out_specs=[ANY] * na, out_shape=out_shape,
        scratch_shapes=[pltpu.SemaphoreType.DMA((na, 7)), pltpu.SemaphoreType.DMA((na, 7)),
                        pltpu.SemaphoreType.DMA((na,))],
    )(*shards)


def _grad_exchange(grads):
    na = len(grads)
    ns = [g.shape[-2] // N_DEV for g in grads]

    def body(*refs):
        ins, outs = refs[:na], refs[na:2 * na]
        send_sems, recv_sems, local_sems = refs[2 * na:]
        x, y, c = _me()
        me = (x, y, c)
        my_slot = 4 * x + 2 * y + c
        local = [pltpu.make_async_copy(_block_rows(ins[a], me, ns[a]), outs[a].at[my_slot], local_sems.at[a])
                 for a in range(na)]
        for cp in local:
            cp.start()
        copies = []
        for mask in range(1, N_DEV):
            peer = (x ^ (mask >> 2), y ^ ((mask >> 1) & 1), c ^ (mask & 1))
            for a in range(na):
                copies.append(pltpu.make_async_remote_copy(
                    src_ref=_block_rows(ins[a], peer, ns[a]), dst_ref=outs[a].at[my_slot],
                    send_sem=send_sems.at[a, mask - 1], recv_sem=recv_sems.at[a, mask - 1],
                    device_id=peer, device_id_type=MESH))
        for cp in copies:
            cp.start()
        for cp in copies:
            cp.wait()
        for cp in local:
            cp.wait()

    out_shape = [jax.ShapeDtypeStruct((N_DEV,) + g.shape[:-2] + (g.shape[-2] // N_DEV, g.shape[-1]), g.dtype)
                 for g in grads]
    return pl.pallas_call(
        body, name="grad_exchange",
        in_specs=[ANY] * na, out_specs=[ANY] * na, out_shape=out_shape,
        scratch_shapes=[pltpu.SemaphoreType.DMA((na, 7)), pltpu.SemaphoreType.DMA((na, 7)),
                        pltpu.SemaphoreType.DMA((na,))],
    )(*grads)


def _small_all_reduce(buf):
    R = buf.shape[0]

    def body(b_ref, o_ref, recv_ref, send_sems, recv_sems):
        x, y, c = _me()
        my_slot = 4 * x + 2 * y + c
        recv_ref[my_slot] = b_ref[...]
        copies = []
        for mask in range(1, N_DEV):
            peer = (x ^ (mask >> 2), y ^ ((mask >> 1) & 1), c ^ (mask & 1))
            copies.append(pltpu.make_async_remote_copy(
                src_ref=b_ref, dst_ref=recv_ref.at[my_slot],
                send_sem=send_sems.at[mask - 1], recv_sem=recv_sems.at[mask - 1],
                device_id=peer, device_id_type=MESH))
        for cp in copies:
            cp.start()
        for cp in copies:
            cp.wait()
        acc = recv_ref[0]
        for k in range(1, N_DEV):
            acc = acc + recv_ref[k]
        o_ref[...] = acc

    return pl.pallas_call(
        body, name="small_all_reduce",
        in_specs=[pl.BlockSpec(memory_space=pltpu.VMEM)], out_specs=pl.BlockSpec(memory_space=pltpu.VMEM),
        out_shape=jax.ShapeDtypeStruct((R, LANES), F32),
        scratch_shapes=[pltpu.VMEM((N_DEV, R, LANES), F32), pltpu.SemaphoreType.DMA((7,)),
                        pltpu.SemaphoreType.DMA((7,))],
        compiler_params=pltpu.CompilerParams(vmem_limit_bytes=VMEM_LIMIT),
    )(buf)


def _pack(arrays):
    flat = jnp.concatenate([a.reshape(-1) for a in arrays])
    pad = (-flat.shape[0]) % (8 * LANES)
    return jnp.pad(flat, (0, pad)).reshape(-1, LANES)


def _unpack(buf, shapes):
    flat = buf.reshape(-1)
    out, off = [], 0
    for s in shapes:
        n = 1
        for d in s:
            n *= d
        out.append(flat[off:off + n].reshape(s))
        off += n
    return out


def _ffn_index(layer, second):
    return (2 * layer + second) * 3


def kernel(x, g_ffn1, w_ffn1_gate, w_ffn1_up, w_ffn1_down, g_mix, w_in_ab, conv_w, conv_b, ln_a_g, ln_a_b, ln_v_g, ln_v_b, sp_w, sp_b, w_out_ab, w_qkv, w_o, g_ffn2, w_ffn2_gate, w_ffn2_up, w_ffn2_down, g_final, loss_target, m_g_ffn1, m_w_ffn1_gate, m_w_ffn1_up, m_w_ffn1_down, m_g_mix, m_w_in_ab, m_conv_w, m_conv_b, m_ln_a_g, m_ln_a_b, m_ln_v_g, m_ln_v_b, m_sp_w, m_sp_b, m_w_out_ab, m_w_qkv, m_w_o, m_g_ffn2, m_w_ffn2_gate, m_w_ffn2_up, m_w_ffn2_down, m_g_final, v_g_ffn1, v_w_ffn1_gate, v_w_ffn1_up, v_w_ffn1_down, v_g_mix, v_w_in_ab, v_conv_w, v_conv_b, v_ln_a_g, v_ln_a_b, v_ln_v_g, v_ln_v_b, v_sp_w, v_sp_b, v_w_out_ab, v_w_qkv, v_w_o, v_g_ffn2, v_w_ffn2_gate, v_w_ffn2_up, v_w_ffn2_down, v_g_final):
    n_seq, S, D = x.shape
    T = n_seq * S
    depth = g_ffn1.shape[0]
    assert depth == 2 and D == D_MODEL
    my_block = 4 * lax.axis_index("x") + 2 * lax.axis_index("y") + lax.axis_index("c")

    ffn_parts = []
    for l in range(depth):
        for gate, up, down in ((w_ffn1_gate, w_ffn1_up, w_ffn1_down), (w_ffn2_gate, w_ffn2_up, w_ffn2_down)):
            ffn_parts += [gate[l].T, up[l].T, down[l]]
    ffn_shard = jnp.stack(ffn_parts).astype(BF16)
    shards = [ffn_shard, w_in_ab[0].T.astype(BF16), w_out_ab[0].astype(BF16),
              w_qkv[0].T.astype(BF16), w_o[0].astype(BF16)]
    conv_w_pad = jnp.zeros((HALO, conv_w.shape[2]), F32).at[:CONV_WIDTH].set(conv_w[0]).T
    shards.append(conv_w_pad)
    wall, w_in_t, w_out, w_qkv_t, w_o_full, conv_w_t = _all_gather(shards)
    conv_w_full = conv_w_t.T[:CONV_WIDTH]

    row = lambda a: a.reshape(1, -1)
    tril = jnp.tril(jnp.ones((CHUNK, CHUNK), dtype=bool))
    ws = jnp.where(tril[None], sp_w[0], 0.0).astype(BF16)
    ws_t = jnp.swapaxes(ws, 1, 2)
    bias2d = jnp.repeat(sp_b[0].T, DB, axis=1)
    conv_b2, lag, lab = row(conv_b[0]), row(ln_a_g[0]), row(ln_a_b[0])
    lvg, lvb = row(ln_v_g[0]), row(ln_v_b[0])

    x0 = x.reshape(T, D)
    target = loss_target.reshape(T, D)
    saved = []
    xc = x0
    for l in range(depth):
        xa, a1, b1, h1 = _ffn_fwd(xc, row(g_ffn1[l]), wall, _ffn_index(l, 0), f"ffn1_fwd_{l}")
        if l % 2 == 0:
            z, hm = _mm_nt(xa, w_in_t, row(g_mix[l]), F32, "mix_in_proj")
            cat, conv_out = _mix_fwd(z, conv_w_full, conv_b2, lag, lab, lvg, lvb, ws, bias2d, n_seq)
            xb = _mm_nn_res(cat, w_out, xa, "mix_out_proj")
            mixer = (z, hm, cat, conv_out)
        else:
            qkv, hm = _mm_nt(xa, w_qkv_t, row(g_mix[l]), BF16, "qkv_proj")
            o, tot = _attn_fwd(qkv, n_seq)
            xb = _mm_nn_res(o, w_o_full, xa, "attn_out_proj")
            mixer = (qkv, hm, o, tot)
        xn, a2, b2, h2 = _ffn_fwd(xb, row(g_ffn2[l]), wall, _ffn_index(l, 1), f"ffn2_fwd_{l}")
        saved.append((xc, a1, b1, h1, xa, mixer, xb, a2, b2, h2))
        xc = xn

    g, loss_part, dg_final = _loss_head(xc, row(g_final), target)

    F = wall.shape[1]
    ffn_grads = [None] * (6 * depth)
    dg_ffn1, dg_ffn2, dg_mix = [None] * depth, [None] * depth, [None] * depth
    for l in reversed(range(depth)):
        xin, a1, b1, h1, xa, mixer, xb, a2, b2, h2 = saved[l]
        base = _ffn_index(l, 1)
        g, dg_ffn2[l], da, db, s, gh = _ffn_bwd(g, xb, row(g_ffn2[l]), a2, b2, wall, base, f"ffn2_bwd_{l}")
        ffn_grads[base:base + 3] = [_mm_tn(da, h2, f"dw_gate2_{l}"), _mm_tn(db, h2, f"dw_up2_{l}"),
                                    _mm_tn(s, gh, f"dw_down2_{l}")]
        if l % 2 == 0:
            z, hm, cat, conv_out = mixer
            dcat = _mm_nt(g, w_out, None, F32, "mix_out_bwd")
            d_w_out = _mm_tn(cat, g, "dw_out")
            (da1, dz_uv, d_lag, d_lab, d_lvg, d_lvb, d_ws, d_sb) = _mix_bwd_rows(
                dcat, z, conv_out, lag, lab, lvg, lvb, ws, ws_t, bias2d, n_seq)
            dz_a, d_cw, d_cb = _mix_bwd_conv(da1, z, conv_w_full, n_seq)
            dz = jnp.concatenate([dz_a, dz_uv], axis=1)
            d_w_in_t = _mm_tn(dz, hm, "dw_in")
            g, dg_mix[l] = _mm_nn_rmsbwd(dz, w_in_t, xa, row(g_mix[l]), g, "mix_in_bwd")
        else:
            qkv, hm, o, tot = mixer
            do = _mm_nt(g, w_o_full, None, BF16, "attn_out_bwd")
            d_w_o = _mm_tn(o, g, "dw_o")
            dq, dk, dv = _attn_bwd(qkv, do, tot, n_seq)
            dqkv = jnp.concatenate([dq, dk, dv], axis=1).astype(BF16)
            d_w_qkv_t = _mm_tn(dqkv, hm, "dw_qkv")
            g, dg_mix[l] = _mm_nn_rmsbwd(dqkv, w_qkv_t, xa, row(g_mix[l]), g, "qkv_bwd")
        base = _ffn_index(l, 0)
        g, dg_ffn1[l], da, db, s, gh = _ffn_bwd(g, xin, row(g_ffn1[l]), a1, b1, wall, base, f"ffn1_bwd_{l}")
        ffn_grads[base:base + 3] = [_mm_tn(da, h1, f"dw_gate1_{l}"), _mm_tn(db, h1, f"dw_up1_{l}"),
                                    _mm_tn(s, gh, f"dw_down1_{l}")]
    grad_x = g.reshape(n_seq, S, D)

    parts = _grad_exchange([jnp.stack(ffn_grads), d_w_in_t, d_w_out, d_w_qkv_t, d_w_o])
    g_ffn_t = _sum_parts(parts[0].reshape(N_DEV, -1, D), "sum_ffn").reshape(6 * depth, F // N_DEV, D)
    g_in = _sum_parts(parts[1], "sum_w_in").T[None]
    g_out = _sum_parts(parts[2], "sum_w_out")[None]
    g_qkv = _sum_parts(parts[3], "sum_w_qkv").T[None]
    g_o = _sum_parts(parts[4], "sum_w_o")[None]

    small = [jnp.concatenate(dg_ffn1), jnp.concatenate(dg_mix), d_cw, d_cb, d_lag, d_lab, d_lvg, d_lvb,
             jnp.where(tril[None], d_ws, 0.0), d_sb.T, jnp.concatenate(dg_ffn2), dg_final, loss_part[:, :1]]
    small_shapes = [(depth, D), (depth, D), (CONV_WIDTH, CA), (1, CA), (1, CA), (1, CA), (1, GB, DB), (1, GB, DB),
                    (1, GB, CHUNK, CHUNK), (1, GB, CHUNK), (depth, D), (D,), ()]
    red = _unpack(_small_all_reduce(_pack(small)), small_shapes)
    (gr_g_ffn1, gr_g_mix, gr_cw_full, gr_cb, gr_lag, gr_lab, gr_lvg, gr_lvb, gr_sp_w, gr_sp_b,
     gr_g_ffn2, gr_g_final, loss) = red
    n_cw = conv_w.shape[2]
    gr_cw = lax.dynamic_slice(gr_cw_full, (0, my_block * n_cw), (CONV_WIDTH, n_cw))[None]

    def ffn_grad(second, which):
        return jnp.stack([g_ffn_t[_ffn_index(l, second) + which].T if which < 2
                          else g_ffn_t[_ffn_index(l, second) + which] for l in range(depth)])

    grads = {
        "g_ffn1": gr_g_ffn1, "w_ffn1_gate": ffn_grad(0, 0), "w_ffn1_up": ffn_grad(0, 1), "w_ffn1_down": ffn_grad(0, 2),
        "g_mix": gr_g_mix, "w_in_ab": g_in, "conv_w": gr_cw, "conv_b": gr_cb, "ln_a_g": gr_lag, "ln_a_b": gr_lab,
        "ln_v_g": gr_lvg, "ln_v_b": gr_lvb, "sp_w": gr_sp_w, "sp_b": gr_sp_b, "w_out_ab": g_out, "w_qkv": g_qkv,
        "w_o": g_o, "g_ffn2": gr_g_ffn2, "w_ffn2_gate": ffn_grad(1, 0), "w_ffn2_up": ffn_grad(1, 1),
        "w_ffn2_down": ffn_grad(1, 2), "g_final": gr_g_final,
    }
    weights = dict(g_ffn1=g_ffn1, w_ffn1_gate=w_ffn1_gate, w_ffn1_up=w_ffn1_up, w_ffn1_down=w_ffn1_down, g_mix=g_mix,
                   w_in_ab=w_in_ab, conv_w=conv_w, conv_b=conv_b, ln_a_g=ln_a_g, ln_a_b=ln_a_b, ln_v_g=ln_v_g,
                   ln_v_b=ln_v_b, sp_w=sp_w, sp_b=sp_b, w_out_ab=w_out_ab, w_qkv=w_qkv, w_o=w_o, g_ffn2=g_ffn2,
                   w_ffn2_gate=w_ffn2_gate, w_ffn2_up=w_ffn2_up, w_ffn2_down=w_ffn2_down, g_final=g_final)
    m_in = dict(g_ffn1=m_g_ffn1, w_ffn1_gate=m_w_ffn1_gate, w_ffn1_up=m_w_ffn1_up, w_ffn1_down=m_w_ffn1_down,
                g_mix=m_g_mix, w_in_ab=m_w_in_ab, conv_w=m_conv_w, conv_b=m_conv_b, ln_a_g=m_ln_a_g, ln_a_b=m_ln_a_b,
                ln_v_g=m_ln_v_g, ln_v_b=m_ln_v_b, sp_w=m_sp_w, sp_b=m_sp_b, w_out_ab=m_w_out_ab, w_qkv=m_w_qkv,
                w_o=m_w_o, g_ffn2=m_g_ffn2, w_ffn2_gate=m_w_ffn2_gate, w_ffn2_up=m_w_ffn2_up,
                w_ffn2_down=m_w_ffn2_down, g_final=m_g_final)
    v_in = dict(g_ffn1=v_g_ffn1, w_ffn1_gate=v_w_ffn1_gate, w_ffn1_up=v_w_ffn1_up, w_ffn1_down=v_w_ffn1_down,
                g_mix=v_g_mix, w_in_ab=v_w_in_ab, conv_w=v_conv_w, conv_b=v_conv_b, ln_a_g=v_ln_a_g, ln_a_b=v_ln_a_b,
                ln_v_g=v_ln_v_g, ln_v_b=v_ln_v_b, sp_w=v_sp_w, sp_b=v_sp_b, w_out_ab=v_w_out_ab, w_qkv=v_w_qkv,
                w_o=v_w_o, g_ffn2=v_g_ffn2, w_ffn2_gate=v_w_ffn2_gate, w_ffn2_up=v_w_ffn2_up,
                w_ffn2_down=v_w_ffn2_down, g_final=v_g_final)
    names = list(weights)
    grads = {n: grads[n].reshape(weights[n].shape) for n in names}

    delta, new_m, new_v = {}, {}, {}
    big = [n for n in names if n.startswith("w_")]
    for n in big:
        shp = weights[n].shape
        two = lambda a: a.reshape(-1, shp[-1])
        d, nm, nv = _adamw(two(weights[n]), two(grads[n]), two(m_in[n]), two(v_in[n]), f"adamw_{n}")
        delta[n], new_m[n], new_v[n] = d.reshape(shp), nm.reshape(shp), nv.reshape(shp)
    little = [n for n in names if n not in big]
    shapes = [weights[n].shape for n in little]
    d, nm, nv = _adamw(_pack([weights[n] for n in little]), _pack([grads[n] for n in little]),
                       _pack([m_in[n] for n in little]), _pack([v_in[n] for n in little]), "adamw_small")
    for n, dd, mm, vv in zip(little, _unpack(d, shapes), _unpack(nm, shapes), _unpack(nv, shapes)):
        delta[n], new_m[n], new_v[n] = dd, mm, vv

    return (loss, grad_x, *[grads[n] for n in names], *[delta[n] for n in names],
            *[new_m[n] for n in names], *[new_v[n] for n in names])
```

```python
import functools

import jax
import jax.numpy as jnp
from jax import lax
from jax.experimental import pallas as pl
from jax.experimental.pallas import tpu as pltpu

F32 = jnp.float32
BF16 = jnp.bfloat16

D_MODEL = 1024
CA = 512
CB = 512
GB = 4
DB = 128
CHUNK = 128
CONV_WIDTH = 31
N_HEADS = 16
HEAD_DIM = 64
EPS = 1e-6
N_DEV = 8
LANES = 128
QB = 128
ATT_TQ = 512
HALO = 32
CONV_ROWS = 32
ATT_SCALE = HEAD_DIM ** -0.5

ADAM_LR = 0.001
ADAM_B1 = 0.9
ADAM_B2 = 0.999
ADAM_EPS = 1e-08
ADAM_WD = 0.01
ADAM_STEP = 10

NT = (((1,), (1,)), ((), ()))
NN = (((1,), (0,)), ((), ()))
TN = (((0,), (0,)), ((), ()))
MESH = pl.DeviceIdType.MESH
ANY = pl.BlockSpec(memory_space=pl.ANY)
VMEM_LIMIT = 56 * 1024 * 1024


def _dot(a, b, dims):
    return lax.dot_general(a, b, dims, preferred_element_type=F32)


def _cp(*sem):
    return pltpu.CompilerParams(dimension_semantics=sem, vmem_limit_bytes=VMEM_LIMIT)


def _tile(n, want):
    if n <= want:
        return n
    t = want - want % LANES
    while t > LANES and n % t:
        t -= LANES
    assert n % t == 0, (n, want)
    return t


def _sigmoid(x):
    return 1.0 / (1.0 + jnp.exp(-x))


def _rstd(x):
    return lax.rsqrt(jnp.mean(x * x, axis=-1, keepdims=True) + EPS)


def _rms_bwd(x, g, dh):
    r = _rstd(x)
    u = dh * g
    dx = r * (u - x * (r * r) * jnp.mean(u * x, axis=-1, keepdims=True))
    dg = jnp.sum(dh * x * r, axis=0, keepdims=True)
    return dx, dg


def _ln_fwd(x, g, b):
    mu = jnp.mean(x, axis=-1, keepdims=True)
    xc = x - mu
    r = lax.rsqrt(jnp.mean(xc * xc, axis=-1, keepdims=True) + EPS)
    xh = xc * r
    return xh * g + b, xh, r


def _ln_bwd(dy, xh, r, g):
    dxh = dy * g
    return r * (dxh - jnp.mean(dxh, axis=-1, keepdims=True)
                - xh * jnp.mean(dxh * xh, axis=-1, keepdims=True))


def _ffn_fwd(x, g, wall, base, name):
    T, D = x.shape
    F = wall.shape[1]
    tm, tn = _tile(T, 512), _tile(F, 256)
    n_j = F // tn

    def body(x_ref, g_ref, wg_ref, wu_ref, wd_ref, xo_ref, a_ref, b_ref, h_ref, acc_ref):
        j = pl.program_id(1)

        @pl.when(j == 0)
        def _():
            xv = x_ref[...]
            h_ref[...] = (xv * _rstd(xv) * g_ref[...]).astype(BF16)
            acc_ref[...] = jnp.zeros_like(acc_ref)

        h = h_ref[...]
        a = _dot(h, wg_ref[...], NT)
        b = _dot(h, wu_ref[...], NT)
        a_ref[...] = a.astype(BF16)
        b_ref[...] = b.astype(BF16)
        s = (a * _sigmoid(a) * b).astype(BF16)
        acc_ref[...] += _dot(s, wd_ref[...], NN)

        @pl.when(j == n_j - 1)
        def _():
            xo_ref[...] = x_ref[...] + 0.5 * acc_ref[...]

    wspec = lambda k: pl.BlockSpec((None, tn, D), lambda i, j: (base + k, j, 0))
    return pl.pallas_call(
        body, name=name, grid=(T // tm, n_j),
        in_specs=[pl.BlockSpec((tm, D), lambda i, j: (i, 0)), pl.BlockSpec((1, D), lambda i, j: (0, 0)),
                  wspec(0), wspec(1), wspec(2)],
        out_specs=[pl.BlockSpec((tm, D), lambda i, j: (i, 0)), pl.BlockSpec((tm, tn), lambda i, j: (i, j)),
                   pl.BlockSpec((tm, tn), lambda i, j: (i, j)), pl.BlockSpec((tm, D), lambda i, j: (i, 0))],
        out_shape=[jax.ShapeDtypeStruct((T, D), F32), jax.ShapeDtypeStruct((T, F), BF16),
                   jax.ShapeDtypeStruct((T, F), BF16), jax.ShapeDtypeStruct((T, D), BF16)],
        scratch_shapes=[pltpu.VMEM((tm, D), F32)],
        compiler_params=_cp("parallel", "arbitrary"),
    )(x, g, wall, wall, wall)


def _ffn_bwd(go, x, g, a, b, wall, base, name):
    T, D = x.shape
    F = wall.shape[1]
    tm, tn = _tile(T, 512), _tile(F, 256)
    n_j = F // tn

    def body(go_ref, x_ref, g_ref, a_ref, b_ref, wg_ref, wu_ref, wd_ref,
             gx_ref, dg_ref, da_ref, db_ref, s_ref, gh_ref, acc_ref):
        i, j = pl.program_id(0), pl.program_id(1)

        @pl.when(j == 0)
        def _():
            gh_ref[...] = (0.5 * go_ref[...]).astype(BF16)
            acc_ref[...] = jnp.zeros_like(acc_ref)

        @pl.when((i == 0) & (j == 0))
        def _():
            dg_ref[...] = jnp.zeros_like(dg_ref)

        ds = _dot(gh_ref[...], wd_ref[...], NT)
        av = a_ref[...].astype(F32)
        bv = b_ref[...].astype(F32)
        sig = _sigmoid(av)
        sl = av * sig
        dab = (ds * bv * (sig * (1.0 + av * (1.0 - sig)))).astype(BF16)
        dbb = (ds * sl).astype(BF16)
        s_ref[...] = (sl * bv).astype(BF16)
        da_ref[...] = dab
        db_ref[...] = dbb
        acc_ref[...] += _dot(dab, wg_ref[...], NN) + _dot(dbb, wu_ref[...], NN)

        @pl.when(j == n_j - 1)
        def _():
            dx, dg = _rms_bwd(x_ref[...], g_ref[...], acc_ref[...])
            gx_ref[...] = go_ref[...] + dx
            dg_ref[...] += dg

    wspec = lambda k: pl.BlockSpec((None, tn, D), lambda i, j: (base + k, j, 0))
    row = pl.BlockSpec((tm, D), lambda i, j: (i, 0))
    hid = pl.BlockSpec((tm, tn), lambda i, j: (i, j))
    vec = pl.BlockSpec((1, D), lambda i, j: (0, 0))
    return pl.pallas_call(
        body, name=name, grid=(T // tm, n_j),
        in_specs=[row, row, vec, hid, hid, wspec(0), wspec(1), wspec(2)],
        out_specs=[row, vec, hid, hid, hid, row],
        out_shape=[jax.ShapeDtypeStruct((T, D), F32), jax.ShapeDtypeStruct((1, D), F32),
                   jax.ShapeDtypeStruct((T, F), BF16), jax.ShapeDtypeStruct((T, F), BF16),
                   jax.ShapeDtypeStruct((T, F), BF16), jax.ShapeDtypeStruct((T, D), BF16)],
        scratch_shapes=[pltpu.VMEM((tm, D), F32)],
        compiler_params=_cp("arbitrary", "arbitrary"),
    )(go, x, g, a, b, wall, wall, wall)


def _mm_tn(a, b, name):
    T, M = a.shape
    N = b.shape[1]
    tmm, tk = _tile(M, 1536), _tile(T, 512)

    def body(a_ref, b_ref, o_ref):
        @pl.when(pl.program_id(1) == 0)
        def _():
            o_ref[...] = jnp.zeros_like(o_ref)

        o_ref[...] += _dot(a_ref[...].astype(BF16), b_ref[...].astype(BF16), TN)

    return pl.pallas_call(
        body, name=name, grid=(M // tmm, T // tk),
        in_specs=[pl.BlockSpec((tk, tmm), lambda m, k: (k, m)), pl.BlockSpec((tk, N), lambda m, k: (k, 0))],
        out_specs=pl.BlockSpec((tmm, N), lambda m, k: (m, 0)),
        out_shape=jax.ShapeDtypeStruct((M, N), F32),
        compiler_params=_cp("parallel", "arbitrary"),
    )(a, b)


def _mm_nt(x, wt, g, out_dtype, name):
    T, K = x.shape
    N = wt.shape[0]
    tm, tn = _tile(T, 512), _tile(N, 512)
    norm = g is not None

    def body(*refs):
        if norm:
            x_ref, g_ref, w_ref, o_ref, h_ref = refs
        else:
            x_ref, w_ref, o_ref, h_ref = refs

        @pl.when(pl.program_id(1) == 0)
        def _():
            xv = x_ref[...].astype(F32)
            if norm:
                xv = xv * _rstd(xv) * g_ref[...]
            h_ref[...] = xv.astype(BF16)

        o_ref[...] = _dot(h_ref[...], w_ref[...], NT).astype(out_dtype)

    row = pl.BlockSpec((tm, K), lambda i, j: (i, 0))
    wsp = pl.BlockSpec((tn, K), lambda i, j: (j, 0))
    osp = pl.BlockSpec((tm, tn), lambda i, j: (i, j))
    if norm:
        return pl.pallas_call(
            body, name=name, grid=(T // tm, N // tn),
            in_specs=[row, pl.BlockSpec((1, K), lambda i, j: (0, 0)), wsp],
            out_specs=[osp, row],
            out_shape=[jax.ShapeDtypeStruct((T, N), out_dtype), jax.ShapeDtypeStruct((T, K), BF16)],
            compiler_params=_cp("parallel", "arbitrary"),
        )(x, g, wt)
    return pl.pallas_call(
        body, name=name, grid=(T // tm, N // tn),
        in_specs=[row, wsp], out_specs=osp,
        out_shape=jax.ShapeDtypeStruct((T, N), out_dtype),
        scratch_shapes=[pltpu.VMEM((tm, K), BF16)],
        compiler_params=_cp("parallel", "arbitrary"),
    )(x, wt)


def _mm_nn_res(act, w, resid, name):
    T, K = act.shape
    D = w.shape[1]
    tm = _tile(T, 512)

    def body(a_ref, w_ref, r_ref, o_ref):
        o_ref[...] = r_ref[...] + _dot(a_ref[...].astype(BF16), w_ref[...], NN)

    return pl.pallas_call(
        body, name=name, grid=(T // tm,),
        in_specs=[pl.BlockSpec((tm, K), lambda i: (i, 0)), pl.BlockSpec((K, D), lambda i: (0, 0)),
                  pl.BlockSpec((tm, D), lambda i: (i, 0))],
        out_specs=pl.BlockSpec((tm, D), lambda i: (i, 0)),
        out_shape=jax.ShapeDtypeStruct((T, D), F32),
        compiler_params=_cp("parallel"),
    )(act, w, resid)


def _mm_nn_rmsbwd(act, w, x, g, gprev, name):
    T, K = act.shape
    D = w.shape[1]
    tm = _tile(T, 512)

    def body(a_ref, w_ref, x_ref, g_ref, gp_ref, o_ref, dg_ref):
        @pl.when(pl.program_id(0) == 0)
        def _():
            dg_ref[...] = jnp.zeros_like(dg_ref)

        dh = _dot(a_ref[...].astype(BF16), w_ref[...], NN)
        dx, dg = _rms_bwd(x_ref[...], g_ref[...], dh)
        o_ref[...] = gp_ref[...] + dx
        dg_ref[...] += dg

    row = pl.BlockSpec((tm, D), lambda i: (i, 0))
    vec = pl.BlockSpec((1, D), lambda i: (0, 0))
    return pl.pallas_call(
        body, name=name, grid=(T // tm,),
        in_specs=[pl.BlockSpec((tm, K), lambda i: (i, 0)), pl.BlockSpec((K, D), lambda i: (0, 0)), row, vec, row],
        out_specs=[row, vec],
        out_shape=[jax.ShapeDtypeStruct((T, D), F32), jax.ShapeDtypeStruct((1, D), F32)],
        compiler_params=_cp("arbitrary"),
    )(act, w, x, g, gprev)


def _loss_head(x, g, target):
    T, D = x.shape
    tm = _tile(T, 512)

    def body(x_ref, g_ref, t_ref, dx_ref, loss_ref, dg_ref):
        @pl.when(pl.program_id(0) == 0)
        def _():
            loss_ref[...] = jnp.zeros_like(loss_ref)
            dg_ref[...] = jnp.zeros_like(dg_ref)

        xv = x_ref[...]
        gv = g_ref[...]
        e = xv * _rstd(xv) * gv - t_ref[...]
        per_tok = jnp.sum(e * e, axis=-1, keepdims=True) * (1.0 / D)
        loss_ref[...] += 0.5 * jnp.sum(per_tok, axis=0, keepdims=True)
        dx, dg = _rms_bwd(xv, gv, e * (1.0 / D))
        dx_ref[...] = dx
        dg_ref[...] += dg

    row = pl.BlockSpec((tm, D), lambda i: (i, 0))
    vec = pl.BlockSpec((1, D), lambda i: (0, 0))
    return pl.pallas_call(
        body, name="loss_head", grid=(T // tm,),
        in_specs=[row, vec, row],
        out_specs=[row, pl.BlockSpec((1, LANES), lambda i: (0, 0)), vec],
        out_shape=[jax.ShapeDtypeStruct((T, D), F32), jax.ShapeDtypeStruct((1, LANES), F32),
                   jax.ShapeDtypeStruct((1, D), F32)],
        compiler_params=_cp("arbitrary"),
    )(x, g, target)


def _softplus(z):
    return jnp.maximum(z, 0.0) + jnp.log(1.0 + jnp.exp(-jnp.abs(z)))


def _cumsum_mm(v, u):
    hi = v.astype(BF16)
    lo = (v - hi.astype(F32)).astype(BF16)
    return _dot(hi, u, NN) + _dot(lo, u, NN)


def _half_rowsum(v):
    n = v.shape[0]
    s0 = jnp.sum(v[:, :QB], axis=1, keepdims=True)
    s1 = jnp.sum(v[:, QB:], axis=1, keepdims=True)
    return jnp.concatenate([jnp.broadcast_to(s0, (n, QB)), jnp.broadcast_to(s1, (n, QB))], axis=1)


def _stack_heads(src_ref, dst_ref, n_blk):
    m0 = lax.broadcasted_iota(jnp.int32, (1, LANES), 1) < HEAD_DIM

    def fill(c, carry):
        blk = src_ref[pl.ds(pl.multiple_of(c * QB, QB), QB), :]
        zero = jnp.zeros_like(blk)
        dst_ref[c, 0:QB, :] = jnp.where(m0, blk, zero)
        dst_ref[c, QB:2 * QB, :] = jnp.where(m0, zero, blk)
        return carry

    lax.fori_loop(0, n_blk, fill, 0)


def _causal_diff(tq):
    row = lax.broadcasted_iota(jnp.int32, (tq, 2 * QB), 0)
    col = lax.broadcasted_iota(jnp.int32, (tq, 2 * QB), 1)
    return row - (col & (QB - 1))


def _tri_blockdiag(upper):
    r = lax.broadcasted_iota(jnp.int32, (2 * QB, 2 * QB), 0)
    c = lax.broadcasted_iota(jnp.int32, (2 * QB, 2 * QB), 1)
    same = (r // QB) == (c // QB)
    return (same & ((r > c) if upper else (r < c))).astype(BF16)


def _attn_tiles(T, n_seq):
    S = T // n_seq
    tq = ATT_TQ if S % ATT_TQ == 0 else QB
    return S, tq, tq // QB, S // tq, S // QB


def _attn_fwd(qkv, n_seq):
    T = qkv.shape[0]
    S, tq, r, n_q, n_k = _attn_tiles(T, n_seq)
    n_p = D_MODEL // LANES
    u_suffix = _tri_blockdiag(True)

    def body(q_ref, k_ref, v_ref, u_ref, o_ref, tot_ref, kk_ref, vv_ref, lr_s, acc_s):
        qi = pl.program_id(2)

        @pl.when(qi == 0)
        def _():
            _stack_heads(k_ref, kk_ref, n_k)
            _stack_heads(v_ref, vv_ref, n_k)

        q = q_ref[...] * ATT_SCALE
        u = u_ref[...]
        diff = _causal_diff(tq)
        lr_s[...] = jnp.zeros_like(lr_s)
        acc_s[...] = jnp.zeros_like(acc_s)

        def step(kj, masked, lr, acc):
            z = _dot(q, kk_ref[kj], NT)
            sp = _softplus(z)
            lk = -sp
            if masked:
                mask = kj * QB - qi * tq < diff
                lk = jnp.where(mask, lk, 0.0)
            a = jnp.exp((z - sp) + _cumsum_mm(lk, u) + lr)
            if masked:
                a = jnp.where(mask, a, 0.0)
            return lr + _half_rowsum(lk), acc + _dot(a.astype(BF16), vv_ref[kj], NN)

        def group(first, masked):
            lr, acc = lr_s[...], acc_s[...]
            for j in range(r):
                lr, acc = step(first - j, masked, lr, acc)
            lr_s[...] = lr
            acc_s[...] = acc

        group((qi + 1) * r - 1, True)

        def off(it, carry):
            group((qi - it) * r - 1, False)
            return carry

        lax.fori_loop(0, qi, off, 0)
        o_ref[...] = acc_s[...].astype(BF16)
        tot_ref[...] = lr_s[...]

    return pl.pallas_call(
        body, name="attn_fwd", grid=(n_seq, n_p, n_q),
        in_specs=[pl.BlockSpec((tq, LANES), lambda b, p, qi: (b * n_q + qi, p)),
                  pl.BlockSpec((S, LANES), lambda b, p, qi: (b, n_p + p)),
                  pl.BlockSpec((S, LANES), lambda b, p, qi: (b, 2 * n_p + p)),
                  pl.BlockSpec((2 * QB, 2 * QB), lambda b, p, qi: (0, 0))],
        out_specs=[pl.BlockSpec((tq, LANES), lambda b, p, qi: (b * n_q + qi, p)),
                   pl.BlockSpec((tq, 2 * QB), lambda b, p, qi: (b * n_q + qi, p))],
        out_shape=[jax.ShapeDtypeStruct((T, D_MODEL), BF16), jax.ShapeDtypeStruct((T, 2 * D_MODEL), F32)],
        scratch_shapes=[pltpu.VMEM((n_k, 2 * QB, LANES), BF16), pltpu.VMEM((n_k, 2 * QB, LANES), BF16),
                        pltpu.VMEM((tq, 2 * QB), F32), pltpu.VMEM((tq, LANES), F32)],
        compiler_params=_cp("parallel", "parallel", "arbitrary"),
    )(qkv, qkv, qkv, u_suffix)


def _attn_bwd(qkv, do, tot, n_seq):
    T = qkv.shape[0]
    S, tq, r, n_q, n_k = _attn_tiles(T, n_seq)
    n_p = D_MODEL // LANES
    u_prefix = _tri_blockdiag(False)

    def body(q_ref, k_ref, v_ref, do_ref, tot_ref, u_ref, dq_ref, dk_ref, dv_ref,
             kk_ref, vv_ref, cl_s, cg_s, dq_s):
        qi = pl.program_id(2)

        @pl.when(qi == 0)
        def _():
            _stack_heads(k_ref, kk_ref, n_k)
            _stack_heads(v_ref, vv_ref, n_k)
            dk_ref[...] = jnp.zeros_like(dk_ref)
            dv_ref[...] = jnp.zeros_like(dv_ref)

        q = q_ref[...] * ATT_SCALE
        dov = do_ref[...]
        tot2 = tot_ref[...]
        u = u_ref[...]
        diff = _causal_diff(tq)
        m0 = lax.broadcasted_iota(jnp.int32, (1, LANES), 1) < HEAD_DIM
        cl_s[...] = jnp.zeros_like(cl_s)
        cg_s[...] = jnp.zeros_like(cg_s)
        dq_s[...] = jnp.zeros_like(dq_s)

        def step(kj, masked, cl, cg, dq):
            kk = kk_ref[kj]
            z = _dot(q, kk, NT)
            sp = _softplus(z)
            lk = -sp
            if masked:
                mask = kj * QB - qi * tq < diff
                lk = jnp.where(mask, lk, 0.0)
            ls = z - sp
            a = jnp.exp(ls + (tot2 - cl - (_cumsum_mm(lk, u) + lk)))
            if masked:
                a = jnp.where(mask, a, 0.0)
            g = a * _dot(dov, vv_ref[kj], NT)
            dz = g - (g + _cumsum_mm(g, u) + cg) * jnp.exp(ls)
            if masked:
                dz = jnp.where(mask, dz, 0.0)
            dz = dz.astype(BF16)
            rows = pl.ds(pl.multiple_of(kj * QB, QB), QB)
            dvt = _dot(a.astype(BF16), dov, TN)
            dv_ref[rows, :] += jnp.where(m0, dvt[:QB], dvt[QB:])
            dkt = _dot(dz, q, TN)
            dk_ref[rows, :] += jnp.where(m0, dkt[:QB], dkt[QB:])
            return cl + _half_rowsum(lk), cg + _half_rowsum(g), dq + _dot(dz, kk, NN)

        def group(first, masked):
            cl, cg, dq = cl_s[...], cg_s[...], dq_s[...]
            for j in range(r):
                cl, cg, dq = step(first + j, masked, cl, cg, dq)
            cl_s[...] = cl
            cg_s[...] = cg
            dq_s[...] = dq

        def off(it, carry):
            group(it * r, False)
            return carry

        lax.fori_loop(0, qi, off, 0)
        group(qi * r, True)
        dq_ref[...] = dq_s[...] * ATT_SCALE

    qspec = pl.BlockSpec((tq, LANES), lambda b, p, qi: (b * n_q + qi, p))
    seq = lambda off: pl.BlockSpec((S, LANES), lambda b, p, qi: (b, off + p))
    return pl.pallas_call(
        body, name="attn_bwd", grid=(n_seq, n_p, n_q),
        in_specs=[qspec, seq(n_p), seq(2 * n_p), qspec,
                  pl.BlockSpec((tq, 2 * QB), lambda b, p, qi: (b * n_q + qi, p)),
                  pl.BlockSpec((2 * QB, 2 * QB), lambda b, p, qi: (0, 0))],
        out_specs=[qspec, seq(0), seq(0)],
        out_shape=[jax.ShapeDtypeStruct((T, D_MODEL), F32)] * 3,
        scratch_shapes=[pltpu.VMEM((n_k, 2 * QB, LANES), BF16), pltpu.VMEM((n_k, 2 * QB, LANES), BF16),
                        pltpu.VMEM((tq, 2 * QB), F32), pltpu.VMEM((tq, 2 * QB), F32), pltpu.VMEM((tq, LANES), F32)],
        compiler_params=_cp("parallel", "parallel", "arbitrary"),
    )(qkv, qkv, qkv, do, tot, u_prefix)


def _glu_with_halo(av_ref, ag_ref, avh_ref, agh_ref, a0_s, first, ts):
    hal = avh_ref[...] * _sigmoid(agh_ref[...])
    a0_s[0:HALO, :] = jnp.where(first, 0.0, hal)
    a0_s[HALO:HALO + ts, :] = av_ref[...] * _sigmoid(ag_ref[...])


def _mix_specs(ts, n_r, with_left):
    blk = lambda c: pl.BlockSpec((ts, CA), lambda b, r: (b * n_r + r, c))
    per = ts // HALO
    left = lambda c: pl.BlockSpec((HALO, CA), lambda b, r: (jnp.maximum((b * n_r + r) * per - 1, 0), c))
    return blk, (left if with_left else None)


def _mix_fwd(z, conv_w, conv_b, ln_a_g, ln_a_b, ln_v_g, ln_v_b, ws, bias2d, n_seq):
    T = z.shape[0]
    S = T // n_seq
    ts = _tile(S, 512)
    n_r = S // ts
    shift = HALO - (CONV_WIDTH - 1)

    def body(av_ref, ag_ref, avh_ref, agh_ref, u_ref, v_ref, cw_ref, cb_ref, lag_ref, lab_ref,
             lvg_ref, lvb_ref, ws_ref, bias_ref, cat_ref, a1_ref, a0_s):
        _glu_with_halo(av_ref, ag_ref, avh_ref, agh_ref, a0_s, pl.program_id(1) == 0, ts)
        for rb in range(ts // CONV_ROWS):
            base = rb * CONV_ROWS
            acc = jnp.broadcast_to(cb_ref[...], (CONV_ROWS, CA))
            for k in range(CONV_WIDTH):
                acc = acc + cw_ref[k:k + 1, :] * a0_s[base + shift + k:base + shift + k + CONV_ROWS, :]
            a1_ref[base:base + CONV_ROWS, :] = acc
        y, _, _ = _ln_fwd(a1_ref[...], lag_ref[...], lab_ref[...])
        cat_ref[:, 0:CA] = (y * _sigmoid(y)).astype(BF16)
        for gi in range(GB):
            sl = slice(gi * DB, (gi + 1) * DB)
            v1, _, _ = _ln_fwd(v_ref[:, sl], lvg_ref[:, sl], lvb_ref[:, sl])
            v1 = v1.astype(BF16)
            for c in range(ts // CHUNK):
                rs = slice(c * CHUNK, (c + 1) * CHUNK)
                v2 = _dot(ws_ref[gi], v1[rs], NN) + bias_ref[:, sl]
                cat_ref[rs, CA + gi * DB:CA + (gi + 1) * DB] = (u_ref[rs, sl] * v2).astype(BF16)

    blk, left = _mix_specs(ts, n_r, True)
    vec = pl.BlockSpec((1, CA), lambda b, r: (0, 0))
    return pl.pallas_call(
        body, name="mix_fwd", grid=(n_seq, n_r),
        in_specs=[blk(0), blk(1), left(0), left(1), blk(2), blk(3),
                  pl.BlockSpec((CONV_WIDTH, CA), lambda b, r: (0, 0)), vec, vec, vec, vec, vec,
                  pl.BlockSpec((GB, CHUNK, CHUNK), lambda b, r: (0, 0, 0)),
                  pl.BlockSpec((CHUNK, CB), lambda b, r: (0, 0))],
        out_specs=[pl.BlockSpec((ts, CA + CB), lambda b, r: (b * n_r + r, 0)), blk(0)],
        out_shape=[jax.ShapeDtypeStruct((T, CA + CB), BF16), jax.ShapeDtypeStruct((T, CA), F32)],
        scratch_shapes=[pltpu.VMEM((HALO + ts, CA), F32)],
        compiler_params=_cp("parallel", "parallel"),
    )(z, z, z, z, z, z, conv_w, conv_b, ln_a_g, ln_a_b, ln_v_g, ln_v_b, ws, bias2d)


def _mix_bwd_rows(dcat, z, a1, ln_a_g, ln_a_b, ln_v_g, ln_v_b, ws, ws_t, bias2d, n_seq):
    T = z.shape[0]
    S = T // n_seq
    ts = _tile(S, 512)
    n_r = S // ts

    def body(dc_ref, u_ref, v_ref, a1_ref, lag_ref, lab_ref, lvg_ref, lvb_ref, ws_ref, wst_ref, bias_ref,
             da1_ref, dz_ref, dlag_ref, dlab_ref, dlvg_ref, dlvb_ref, dws_ref, dsb_ref, dv1_s, dbias_s):
        first = (pl.program_id(0) == 0) & (pl.program_id(1) == 0)
        last = (pl.program_id(0) == n_seq - 1) & (pl.program_id(1) == n_r - 1)

        @pl.when(first)
        def _():
            for ref in (dlag_ref, dlab_ref, dlvg_ref, dlvb_ref, dws_ref, dbias_s):
                ref[...] = jnp.zeros_like(ref)

        lag = lag_ref[...]
        y, xh, r = _ln_fwd(a1_ref[...], lag, lab_ref[...])
        sig = _sigmoid(y)
        dy = dc_ref[:, 0:CA] * (sig * (1.0 + y * (1.0 - sig)))
        dlag_ref[...] += jnp.sum(dy * xh, axis=0, keepdims=True)
        dlab_ref[...] += jnp.sum(dy, axis=0, keepdims=True)
        da1_ref[...] = _ln_bwd(dy, xh, r, lag)

        tril = (lax.broadcasted_iota(jnp.int32, (CHUNK, CHUNK), 0)
                >= lax.broadcasted_iota(jnp.int32, (CHUNK, CHUNK), 1))
        for gi in range(GB):
            sl = slice(gi * DB, (gi + 1) * DB)
            lvg = lvg_ref[:, sl]
            v1, vh, vr = _ln_fwd(v_ref[:, sl], lvg, lvb_ref[:, sl])
            v1 = v1.astype(BF16)
            for c in range(ts // CHUNK):
                rs = slice(c * CHUNK, (c + 1) * CHUNK)
                v2 = _dot(ws_ref[gi], v1[rs], NN) + bias_ref[:, sl]
                dbo = dc_ref[rs, CA + gi * DB:CA + (gi + 1) * DB]
                dz_ref[rs, sl] = (dbo * v2).astype(BF16)
                dv2 = dbo * u_ref[rs, sl]
                dbias_s[:, sl] += dv2
                dv2b = dv2.astype(BF16)
                dws_ref[gi] += jnp.where(tril, _dot(dv2b, v1[rs], NT), 0.0)
                dv1_s[rs, :] = _dot(wst_ref[gi], dv2b, NN)
            dv1 = dv1_s[...]
            dlvg_ref[:, sl] += jnp.sum(dv1 * vh, axis=0, keepdims=True)
            dlvb_ref[:, sl] += jnp.sum(dv1, axis=0, keepdims=True)
            dz_ref[:, CB + gi * DB:CB + (gi + 1) * DB] = _ln_bwd(dv1, vh, vr, lvg).astype(BF16)

        @pl.when(last)
        def _():
            col = lax.broadcasted_iota(jnp.int32, (CHUNK, GB), 1)
            out = jnp.zeros((CHUNK, GB), F32)
            for gi in range(GB):
                s = jnp.sum(dbias_s[:, gi * DB:(gi + 1) * DB], axis=1, keepdims=True)
                out = out + jnp.where(col == gi, s, 0.0)
            dsb_ref[...] = out

    blk, _ = _mix_specs(ts, n_r, False)
    vec = pl.BlockSpec((1, CA), lambda b, r: (0, 0))
    mat = pl.BlockSpec((GB, CHUNK, CHUNK), lambda b, r: (0, 0, 0))
    wide = pl.BlockSpec((ts, CA + CB), lambda b, r: (b * n_r + r, 0))
    return pl.pallas_call(
        body, name="mix_bwd_rows", grid=(n_seq, n_r),
        in_specs=[wide, blk(2), blk(3), blk(0), vec, vec, vec, vec, mat, mat,
                  pl.BlockSpec((CHUNK, CB), lambda b, r: (0, 0))],
        out_specs=[blk(0), wide, vec, vec, vec, vec, mat, pl.BlockSpec((CHUNK, GB), lambda b, r: (0, 0))],
        out_shape=[jax.ShapeDtypeStruct((T, CA), F32), jax.ShapeDtypeStruct((T, 2 * CB), BF16)]
        + [jax.ShapeDtypeStruct((1, CA), F32)] * 4
        + [jax.ShapeDtypeStruct((GB, CHUNK, CHUNK), F32), jax.ShapeDtypeStruct((CHUNK, GB), F32)],
        scratch_shapes=[pltpu.VMEM((ts, DB), F32), pltpu.VMEM((CHUNK, CB), F32)],
        compiler_params=_cp("arbitrary", "arbitrary"),
    )(dcat, z, z, a1, ln_a_g, ln_a_b, ln_v_g, ln_v_b, ws, ws_t, bias2d)


def _mix_bwd_conv(da1, z, conv_w, n_seq):
    T = z.shape[0]
    S = T // n_seq
    ts = _tile(S, 512)
    n_r = S // ts
    per = ts // HALO
    shift = HALO - (CONV_WIDTH - 1)
    fold = CONV_ROWS // 8

    def body(d_ref, dh_ref, av_ref, ag_ref, avh_ref, agh_ref, cw_ref,
             dz_ref, dcw_ref, dcb_ref, a0_s, d1_s, da0_s, dw8_s):
        first = (pl.program_id(0) == 0) & (pl.program_id(1) == 0)
        last = (pl.program_id(0) == n_seq - 1) & (pl.program_id(1) == n_r - 1)

        @pl.when(first)
        def _():
            dw8_s[...] = jnp.zeros_like(dw8_s)
            dcb_ref[...] = jnp.zeros_like(dcb_ref)

        _glu_with_halo(av_ref, ag_ref, avh_ref, agh_ref, a0_s, pl.program_id(1) == 0, ts)
        d1_s[0:ts, :] = d_ref[...]
        d1_s[ts:ts + HALO, :] = jnp.where(pl.program_id(1) == n_r - 1, 0.0, dh_ref[...])
        dcb_ref[...] += jnp.sum(d_ref[...], axis=0, keepdims=True)
        for rb in range(ts // CONV_ROWS):
            base = rb * CONV_ROWS
            dcur = d1_s[base:base + CONV_ROWS, :]
            acc = jnp.zeros((CONV_ROWS, CA), F32)
            for k in range(CONV_WIDTH):
                back = CONV_WIDTH - 1 - k
                acc = acc + cw_ref[k:k + 1, :] * d1_s[base + back:base + back + CONV_ROWS, :]
                prod = dcur * a0_s[base + shift + k:base + shift + k + CONV_ROWS, :]
                part = prod[0:8]
                for f in range(1, fold):
                    part = part + prod[8 * f:8 * f + 8]
                dw8_s[k] += part
            da0_s[base:base + CONV_ROWS, :] = acc
        da0 = da0_s[...]
        sig = _sigmoid(ag_ref[...])
        dz_ref[:, 0:CA] = (da0 * sig).astype(BF16)
        dz_ref[:, CA:2 * CA] = (da0 * av_ref[...] * sig * (1.0 - sig)).astype(BF16)

        @pl.when(last)
        def _():
            for k in range(CONV_WIDTH):
                dcw_ref[k:k + 1, :] = jnp.sum(dw8_s[k], axis=0, keepdims=True)

    blk, left = _mix_specs(ts, n_r, True)
    n_halo_blocks = T // HALO
    right = pl.BlockSpec((HALO, CA), lambda b, r: (jnp.minimum((b * n_r + r + 1) * per, n_halo_blocks - 1), 0))
    return pl.pallas_call(
        body, name="mix_bwd_conv", grid=(n_seq, n_r),
        in_specs=[blk(0), right, blk(0), blk(1), left(0), left(1),
                  pl.BlockSpec((CONV_WIDTH, CA), lambda b, r: (0, 0))],
        out_specs=[pl.BlockSpec((ts, 2 * CA), lambda b, r: (b * n_r + r, 0)),
                   pl.BlockSpec((CONV_WIDTH, CA), lambda b, r: (0, 0)), pl.BlockSpec((1, CA), lambda b, r: (0, 0))],
        out_shape=[jax.ShapeDtypeStruct((T, 2 * CA), BF16), jax.ShapeDtypeStruct((CONV_WIDTH, CA), F32),
                   jax.ShapeDtypeStruct((1, CA), F32)],
        scratch_shapes=[pltpu.VMEM((HALO + ts, CA), F32), pltpu.VMEM((ts + HALO, CA), F32),
                        pltpu.VMEM((ts, CA), F32), pltpu.VMEM((CONV_WIDTH, 8, CA), F32)],
        compiler_params=_cp("arbitrary", "arbitrary"),
    )(da1, da1, z, z, z, z, conv_w)


def _sum_parts(parts, name):
    n, R, C = parts.shape
    tr = _tile(R, 512) if R % 8 == 0 and R > 512 else R
    if R % tr:
        tr = R

    def body(p_ref, o_ref):
        acc = p_ref[0]
        for k in range(1, n):
            acc = acc + p_ref[k]
        o_ref[...] = acc

    return pl.pallas_call(
        body, name=name, grid=(R // tr,),
        in_specs=[pl.BlockSpec((n, tr, C), lambda i: (0, i, 0))],
        out_specs=pl.BlockSpec((tr, C), lambda i: (i, 0)),
        out_shape=jax.ShapeDtypeStruct((R, C), F32),
        compiler_params=_cp("parallel"),
    )(parts)


def _row_tile(R, want):
    t = min(R, want)
    t -= t % 8
    while t > 8 and R % t:
        t -= 8
    return t if t >= 8 and R % t == 0 else R


def _adamw(w, g, m, v, name):
    R, C = w.shape
    tr = _row_tile(R, 256)
    c1 = 1.0 - ADAM_B1 ** ADAM_STEP
    c2 = 1.0 - ADAM_B2 ** ADAM_STEP

    def body(w_ref, g_ref, m_ref, v_ref, d_ref, nm_ref, nv_ref):
        gv = g_ref[...]
        nm = ADAM_B1 * m_ref[...] + (1.0 - ADAM_B1) * gv
        nv = ADAM_B2 * v_ref[...] + (1.0 - ADAM_B2) * (gv * gv)
        d_ref[...] = -ADAM_LR * ((nm / c1) / (jnp.sqrt(nv / c2) + ADAM_EPS) + ADAM_WD * w_ref[...])
        nm_ref[...] = nm
        nv_ref[...] = nv

    blk = pl.BlockSpec((tr, C), lambda i: (i, 0))
    return pl.pallas_call(
        body, name=name, grid=(R // tr,),
        in_specs=[blk] * 4, out_specs=[blk] * 3,
        out_shape=[jax.ShapeDtypeStruct((R, C), F32)] * 3,
        compiler_params=_cp("parallel"),
    )(w, g, m, v)


def _me():
    return lax.axis_index("x"), lax.axis_index("y"), lax.axis_index("c")


def _block_rows(ref, dev, n):
    start = (4 * dev[0] + 2 * dev[1] + dev[2]) * n
    if len(ref.shape) == 2:
        return ref.at[pl.ds(start, n), :]
    return ref.at[:, pl.ds(start, n), :]


def _all_gather(shards):
    na = len(shards)
    ns = [s.shape[-2] for s in shards]

    def body(*refs):
        ins, outs = refs[:na], refs[na:2 * na]
        send_sems, recv_sems, local_sems = refs[2 * na:]
        x, y, c = _me()
        me, sibling = (x, y, c), (x, y, 1 - c)
        chips = [(1 - x, y), (x, 1 - y), (1 - x, 1 - y)]

        def copy(a, k, block, to, src=None):
            dst = _block_rows(outs[a], block, ns[a])
            return pltpu.make_async_remote_copy(
                src_ref=dst if src is None else src, dst_ref=dst,
                send_sem=send_sems.at[a, k], recv_sem=recv_sems.at[a, k], device_id=to, device_id_type=MESH)

        mine = [pltpu.make_async_copy(ins[a], _block_rows(outs[a], me, ns[a]), local_sems.at[a]) for a in range(na)]
        for cp in mine:
            cp.start()
        first = []
        for a in range(na):
            first.append(copy(a, 0, me, sibling, src=ins[a]))
            first += [copy(a, 1 + j, me, (*chip, c), src=ins[a]) for j, chip in enumerate(chips)]
        for cp in first:
            cp.start()
        passed = []
        for j, chip in enumerate(chips):
            for a in range(na):
                copy(a, 1 + j, (*chip, c), me).wait_recv()
                fwd = copy(a, 4 + j, (*chip, c), sibling)
                fwd.start()
                passed.append(fwd)
        for a in range(na):
            copy(a, 0, sibling, me).wait_recv()
            for j, chip in enumerate(chips):
                copy(a, 4 + j, (*chip, 1 - c), me).wait_recv()
        for cp in first + passed:
            cp.wait_send()
        for cp in mine:
            cp.wait()

    out_shape = [jax.ShapeDtypeStruct(s.shape[:-2] + (N_DEV * s.shape[-2], s.shape[-1]), s.dtype) for s in shards]
    return pl.pallas_call(
        body, name="weights_all_gather",
        in_specs=[ANY] * na, out_specs=[ANY] * na, out_shape=out_shape,
        scratch_shapes=[pltpu.SemaphoreType.DMA((na, 7)), pltpu.SemaphoreType.DMA((na, 7)),
                        pltpu.SemaphoreType.DMA((na,))],
    )(*shards)


def _grad_exchange(grads):
    na = len(grads)
    ns = [g.shape[-2] // N_DEV for g in grads]

    def body(*refs):
        ins, outs = refs[:na], refs[na:2 * na]
        send_sems, recv_sems, local_sems = refs[2 * na:]
        x, y, c = _me()
        me = (x, y, c)
        my_slot = 4 * x + 2 * y + c
        local = [pltpu.make_async_copy(_block_rows(ins[a], me, ns[a]), outs[a].at[my_slot], local_sems.at[a])
                 for a in range(na)]
        for cp in local:
            cp.start()
        copies = []
        for mask in range(1, N_DEV):
            peer = (x ^ (mask >> 2), y ^ ((mask >> 1) & 1), c ^ (mask & 1))
            for a in range(na):
                copies.append(pltpu.make_async_remote_copy(
                    src_ref=_block_rows(ins[a], peer, ns[a]), dst_ref=outs[a].at[my_slot],
                    send_sem=send_sems.at[a, mask - 1], recv_sem=recv_sems.at[a, mask - 1],
                    device_id=peer, device_id_type=MESH))
        for cp in copies:
            cp.start()
        for cp in copies:
            cp.wait()
        for cp in local:
            cp.wait()

    out_shape = [jax.ShapeDtypeStruct((N_DEV,) + g.shape[:-2] + (g.shape[-2] // N_DEV, g.shape[-1]), g.dtype)
                 for g in grads]
    return pl.pallas_call(
        body, name="grad_exchange",
        in_specs=[ANY] * na, out_specs=[ANY] * na, out_shape=out_shape,
        scratch_shapes=[pltpu.SemaphoreType.DMA((na, 7)), pltpu.SemaphoreType.DMA((na, 7)),
                        pltpu.SemaphoreType.DMA((na,))],
    )(*grads)


def _small_all_reduce(buf):
    R = buf.shape[0]

    def body(b_ref, o_ref, recv_ref, send_sems, recv_sems):
        x, y, c = _me()
        my_slot = 4 * x + 2 * y + c
        recv_ref[my_slot] = b_ref[...]
        copies = []
        for mask in range(1, N_DEV):
            peer = (x ^ (mask >> 2), y ^ ((mask >> 1) & 1), c ^ (mask & 1))
            copies.append(pltpu.make_async_remote_copy(
                src_ref=b_ref, dst_ref=recv_ref.at[my_slot],
                send_sem=send_sems.at[mask - 1], recv_sem=recv_sems.at[mask - 1],
                device_id=peer, device_id_type=MESH))
        for cp in copies:
            cp.start()
        for cp in copies:
            cp.wait()
        acc = recv_ref[0]
        for k in range(1, N_DEV):
            acc = acc + recv_ref[k]
        o_ref[...] = acc

    return pl.pallas_call(
        body, name="small_all_reduce",
        in_specs=[pl.BlockSpec(memory_space=pltpu.VMEM)], out_specs=pl.BlockSpec(memory_space=pltpu.VMEM),
        out_shape=jax.ShapeDtypeStruct((R, LANES), F32),
        scratch_shapes=[pltpu.VMEM((N_DEV, R, LANES), F32), pltpu.SemaphoreType.DMA((7,)),
                        pltpu.SemaphoreType.DMA((7,))],
        compiler_params=pltpu.CompilerParams(vmem_limit_bytes=VMEM_LIMIT),
    )(buf)


def _pack(arrays):
    flat = jnp.concatenate([a.reshape(-1) for a in arrays])
    pad = (-flat.shape[0]) % (8 * LANES)
    return jnp.pad(flat, (0, pad)).reshape(-1, LANES)


def _unpack(buf, shapes):
    flat = buf.reshape(-1)
    out, off = [], 0
    for s in shapes:
        n = 1
        for d in s:
            n *= d
        out.append(flat[off:off + n].reshape(s))
        off += n
    return out


def _ffn_index(layer, second):
    return (2 * layer + second) * 3


def kernel(x, g_ffn1, w_ffn1_gate, w_ffn1_up, w_ffn1_down, g_mix, w_in_ab, conv_w, conv_b, ln_a_g, ln_a_b, ln_v_g, ln_v_b, sp_w, sp_b, w_out_ab, w_qkv, w_o, g_ffn2, w_ffn2_gate, w_ffn2_up, w_ffn2_down, g_final, loss_target, m_g_ffn1, m_w_ffn1_gate, m_w_ffn1_up, m_w_ffn1_down, m_g_mix, m_w_in_ab, m_conv_w, m_conv_b, m_ln_a_g, m_ln_a_b, m_ln_v_g, m_ln_v_b, m_sp_w, m_sp_b, m_w_out_ab, m_w_qkv, m_w_o, m_g_ffn2, m_w_ffn2_gate, m_w_ffn2_up, m_w_ffn2_down, m_g_final, v_g_ffn1, v_w_ffn1_gate, v_w_ffn1_up, v_w_ffn1_down, v_g_mix, v_w_in_ab, v_conv_w, v_conv_b, v_ln_a_g, v_ln_a_b, v_ln_v_g, v_ln_v_b, v_sp_w, v_sp_b, v_w_out_ab, v_w_qkv, v_w_o, v_g_ffn2, v_w_ffn2_gate, v_w_ffn2_up, v_w_ffn2_down, v_g_final):
    n_seq, S, D = x.shape
    T = n_seq * S
    depth = g_ffn1.shape[0]
    assert depth == 2 and D == D_MODEL
    my_block = 4 * lax.axis_index("x") + 2 * lax.axis_index("y") + lax.axis_index("c")

    ffn_parts = []
    for l in range(depth):
        for gate, up, down in ((w_ffn1_gate, w_ffn1_up, w_ffn1_down), (w_ffn2_gate, w_ffn2_up, w_ffn2_down)):
            ffn_parts += [gate[l].T, up[l].T, down[l]]
    ffn_shard = jnp.stack(ffn_parts).astype(BF16)
    shards = [ffn_shard, w_in_ab[0].T.astype(BF16), w_out_ab[0].astype(BF16),
              w_qkv[0].T.astype(BF16), w_o[0].astype(BF16)]
    conv_w_pad = jnp.zeros((HALO, conv_w.shape[2]), F32).at[:CONV_WIDTH].set(conv_w[0]).T
    shards.append(conv_w_pad)
    wall, w_in_t, w_out, w_qkv_t, w_o_full, conv_w_t = _all_gather(shards)
    conv_w_full = conv_w_t.T[:CONV_WIDTH]

    row = lambda a: a.reshape(1, -1)
    tril = jnp.tril(jnp.ones((CHUNK, CHUNK), dtype=bool))
    ws = jnp.where(tril[None], sp_w[0], 0.0).astype(BF16)
    ws_t = jnp.swapaxes(ws, 1, 2)
    bias2d = jnp.repeat(sp_b[0].T, DB, axis=1)
    conv_b2, lag, lab = row(conv_b[0]), row(ln_a_g[0]), row(ln_a_b[0])
    lvg, lvb = row(ln_v_g[0]), row(ln_v_b[0])

    x0 = x.reshape(T, D)
    target = loss_target.reshape(T, D)
    saved = []
    xc = x0
    for l in range(depth):
        xa, a1, b1, h1 = _ffn_fwd(xc, row(g_ffn1[l]), wall, _ffn_index(l, 0), f"ffn1_fwd_{l}")
        if l % 2 == 0:
            z, hm = _mm_nt(xa, w_in_t, row(g_mix[l]), F32, "mix_in_proj")
            cat, conv_out = _mix_fwd(z, conv_w_full, conv_b2, lag, lab, lvg, lvb, ws, bias2d, n_seq)
            xb = _mm_nn_res(cat, w_out, xa, "mix_out_proj")
            mixer = (z, hm, cat, conv_out)
        else:
            qkv, hm = _mm_nt(xa, w_qkv_t, row(g_mix[l]), BF16, "qkv_proj")
            o, tot = _attn_fwd(qkv, n_seq)
            xb = _mm_nn_res(o, w_o_full, xa, "attn_out_proj")
            mixer = (qkv, hm, o, tot)
        xn, a2, b2, h2 = _ffn_fwd(xb, row(g_ffn2[l]), wall, _ffn_index(l, 1), f"ffn2_fwd_{l}")
        saved.append((xc, a1, b1, h1, xa, mixer, xb, a2, b2, h2))
        xc = xn

    g, loss_part, dg_final = _loss_head(xc, row(g_final), target)

    F = wall.shape[1]
    ffn_grads = [None] * (6 * depth)
    dg_ffn1, dg_ffn2, dg_mix = [None] * depth, [None] * depth, [None] * depth
    for l in reversed(range(depth)):
        xin, a1, b1, h1, xa, mixer, xb, a2, b2, h2 = saved[l]
        base = _ffn_index(l, 1)
        g, dg_ffn2[l], da, db, s, gh = _ffn_bwd(g, xb, row(g_ffn2[l]), a2, b2, wall, base, f"ffn2_bwd_{l}")
        ffn_grads[base:base + 3] = [_mm_tn(da, h2, f"dw_gate2_{l}"), _mm_tn(db, h2, f"dw_up2_{l}"),
                                    _mm_tn(s, gh, f"dw_down2_{l}")]
        if l % 2 == 0:
            z, hm, cat, conv_out = mixer
            dcat = _mm_nt(g, w_out, None, F32, "mix_out_bwd")
            d_w_out = _mm_tn(cat, g, "dw_out")
            (da1, dz_uv, d_lag, d_lab, d_lvg, d_lvb, d_ws, d_sb) = _mix_bwd_rows(
                dcat, z, conv_out, lag, lab, lvg, lvb, ws, ws_t, bias2d, n_seq)
            dz_a, d_cw, d_cb = _mix_bwd_conv(da1, z, conv_w_full, n_seq)
            dz = jnp.concatenate([dz_a, dz_uv], axis=1)
            d_w_in_t = _mm_tn(dz, hm, "dw_in")
            g, dg_mix[l] = _mm_nn_rmsbwd(dz, w_in_t, xa, row(g_mix[l]), g, "mix_in_bwd")
        else:
            qkv, hm, o, tot = mixer
            do = _mm_nt(g, w_o_full, None, BF16, "attn_out_bwd")
            d_w_o = _mm_tn(o, g, "dw_o")
            dq, dk, dv = _attn_bwd(qkv, do, tot, n_seq)
            dqkv = jnp.concatenate([dq, dk, dv], axis=1).astype(BF16)
            d_w_qkv_t = _mm_tn(dqkv, hm, "dw_qkv")
            g, dg_mix[l] = _mm_nn_rmsbwd(dqkv, w_qkv_t, xa, row(g_mix[l]), g, "qkv_bwd")
        base = _ffn_index(l, 0)
        g, dg_ffn1[l], da, db, s, gh = _ffn_bwd(g, xin, row(g_ffn1[l]), a1, b1, wall, base, f"ffn1_bwd_{l}")
        ffn_grads[base:base + 3] = [_mm_tn(da, h1, f"dw_gate1_{l}"), _mm_tn(db, h1, f"dw_up1_{l}"),
                                    _mm_tn(s, gh, f"dw_down1_{l}")]
    grad_x = g.reshape(n_seq, S, D)

    parts = _grad_exchange([jnp.stack(ffn_grads), d_w_in_t, d_w_out, d_w_qkv_t, d_w_o])
    g_ffn_t = _sum_parts(parts[0].reshape(N_DEV, -1, D), "sum_ffn").reshape(6 * depth, F // N_DEV, D)
    g_in = _sum_parts(parts[1], "sum_w_in").T[None]
    g_out = _sum_parts(parts[2], "sum_w_out")[None]
    g_qkv = _sum_parts(parts[3], "sum_w_qkv").T[None]
    g_o = _sum_parts(parts[4], "sum_w_o")[None]

    small = [jnp.concatenate(dg_ffn1), jnp.concatenate(dg_mix), d_cw, d_cb, d_lag, d_lab, d_lvg, d_lvb,
             jnp.where(tril[None], d_ws, 0.0), d_sb.T, jnp.concatenate(dg_ffn2), dg_final, loss_part[:, :1]]
    small_shapes = [(depth, D), (depth, D), (CONV_WIDTH, CA), (1, CA), (1, CA), (1, CA), (1, GB, DB), (1, GB, DB),
                    (1, GB, CHUNK, CHUNK), (1, GB, CHUNK), (depth, D), (D,), ()]
    red = _unpack(_small_all_reduce(_pack(small)), small_shapes)
    (gr_g_ffn1, gr_g_mix, gr_cw_full, gr_cb, gr_lag, gr_lab, gr_lvg, gr_lvb, gr_sp_w, gr_sp_b,
     gr_g_ffn2, gr_g_final, loss) = red
    n_cw = conv_w.shape[2]
    gr_cw = lax.dynamic_slice(gr_cw_full, (0, my_block * n_cw), (CONV_WIDTH, n_cw))[None]

    def ffn_grad(second, which):
        return jnp.stack([g_ffn_t[_ffn_index(l, second) + which].T if which < 2
                          else g_ffn_t[_ffn_index(l, second) + which] for l in range(depth)])

    grads = {
        "g_ffn1": gr_g_ffn1, "w_ffn1_gate": ffn_grad(0, 0), "w_ffn1_up": ffn_grad(0, 1), "w_ffn1_down": ffn_grad(0, 2),
        "g_mix": gr_g_mix, "w_in_ab": g_in, "conv_w": gr_cw, "conv_b": gr_cb, "ln_a_g": gr_lag, "ln_a_b": gr_lab,
        "ln_v_g": gr_lvg, "ln_v_b": gr_lvb, "sp_w": gr_sp_w, "sp_b": gr_sp_b, "w_out_ab": g_out, "w_qkv": g_qkv,
        "w_o": g_o, "g_ffn2": gr_g_ffn2, "w_ffn2_gate": ffn_grad(1, 0), "w_ffn2_up": ffn_grad(1, 1),
        "w_ffn2_down": ffn_grad(1, 2), "g_final": gr_g_final,
    }
    weights = dict(g_ffn1=g_ffn1, w_ffn1_gate=w_ffn1_gate, w_ffn1_up=w_ffn1_up, w_ffn1_down=w_ffn1_down, g_mix=g_mix,
                   w_in_ab=w_in_ab, conv_w=conv_w, conv_b=conv_b, ln_a_g=ln_a_g, ln_a_b=ln_a_b, ln_v_g=ln_v_g,
                   ln_v_b=ln_v_b, sp_w=sp_w, sp_b=sp_b, w_out_ab=w_out_ab, w_qkv=w_qkv, w_o=w_o, g_ffn2=g_ffn2,
                   w_ffn2_gate=w_ffn2_gate, w_ffn2_up=w_ffn2_up, w_ffn2_down=w_ffn2_down, g_final=g_final)
    m_in = dict(g_ffn1=m_g_ffn1, w_ffn1_gate=m_w_ffn1_gate, w_ffn1_up=m_w_ffn1_up, w_ffn1_down=m_w_ffn1_down,
                g_mix=m_g_mix, w_in_ab=m_w_in_ab, conv_w=m_conv_w, conv_b=m_conv_b, ln_a_g=m_ln_a_g, ln_a_b=m_ln_a_b,
                ln_v_g=m_ln_v_g, ln_v_b=m_ln_v_b, sp_w=m_sp_w, sp_b=m_sp_b, w_out_ab=m_w_out_ab, w_qkv=m_w_qkv,
                w_o=m_w_o, g_ffn2=m_g_ffn2, w_ffn2_gate=m_w_ffn2_gate, w_ffn2_up=m_w_ffn2_up,
                w_ffn2_down=m_w_ffn2_down, g_final=m_g_final)
    v_in = dict(g_ffn1=v_g_ffn1, w_ffn1_gate=v_w_ffn1_gate, w_ffn1_up=v_w_ffn1_up, w_ffn1_down=v_w_ffn1_down,
                g_mix=v_g_mix, w_in_ab=v_w_in_ab, conv_w=v_conv_w, conv_b=v_conv_b, ln_a_g=v_ln_a_g, ln_a_b=v_ln_a_b,
                ln_v_g=v_ln_v_g, ln_v_b=v_ln_v_b, sp_w=v_sp_w, sp_b=v_sp_b, w_out_ab=v_w_out_ab, w_qkv=v_w_qkv,
                w_o=v_w_o, g_ffn2=v_g_ffn2, w_ffn2_gate=v_w_ffn2_gate, w_ffn2_up=v_w_ffn2_up,
                w_ffn2_down=v_w_ffn2_down, g_final=v_g_final)
    names = list(weights)
    grads = {n: grads[n].reshape(weights[n].shape) for n in names}

    delta, new_m, new_v = {}, {}, {}
    big = [n for n in names if n.startswith("w_")]
    for n in big:
        shp = weights[n].shape
        two = lambda a: a.reshape(-1, shp[-1])
        d, nm, nv = _adamw(two(weights[n]), two(grads[n]), two(m_in[n]), two(v_in[n]), f"adamw_{n}")
        delta[n], new_m[n], new_v[n] = d.reshape(shp), nm.reshape(shp), nv.reshape(shp)
    little = [n for n in names if n not in big]
    shapes = [weights[n].shape for n in little]
    d, nm, nv = _adamw(_pack([weights[n] for n in little]), _pack([grads[n] for n in little]),
                       _pack([m_in[n] for n in little]), _pack([v_in[n] for n in little]), "adamw_small")
    for n, dd, mm, vv in zip(little, _unpack(d, shapes), _unpack(nm, shapes), _unpack(nv, shapes)):
        delta[n], new_m[n], new_v[n] = dd, mm, vv

    return (loss, grad_x, *[grads[n] for n in names], *[delta[n] for n in names],
            *[new_m[n] for n in names], *[new_v[n] for n in names])
```

```python
import functools

import jax
import jax.numpy as jnp
from jax import lax
from jax.experimental import pallas as pl
from jax.experimental.pallas import tpu as pltpu

F32 = jnp.float32
BF16 = jnp.bfloat16

D_MODEL = 1024
CA = 512
CB = 512
GB = 4
DB = 128
CHUNK = 128
CONV_WIDTH = 31
N_HEADS = 16
HEAD_DIM = 64
EPS = 1e-6
N_DEV = 8
LANES = 128
QB = 128
ATT_TQ = 512
HALO = 32
CONV_ROWS = 32
ATT_SCALE = HEAD_DIM ** -0.5

ADAM_LR = 0.001
ADAM_B1 = 0.9
ADAM_B2 = 0.999
ADAM_EPS = 1e-08
ADAM_WD = 0.01
ADAM_STEP = 10

NT = (((1,), (1,)), ((), ()))
NN = (((1,), (0,)), ((), ()))
TN = (((0,), (0,)), ((), ()))
MESH = pl.DeviceIdType.MESH
ANY = pl.BlockSpec(memory_space=pl.ANY)
VMEM_LIMIT = 56 * 1024 * 1024


def _dot(a, b, dims):
    return lax.dot_general(a, b, dims, preferred_element_type=F32)


def _cp(*sem):
    return pltpu.CompilerParams(dimension_semantics=sem, vmem_limit_bytes=VMEM_LIMIT)


def _pcall(body, *, in_specs, args, dep=None, **kw):
    if dep is not None:
        n_in = len(in_specs)
        inner = body

        def body(*refs):
            inner(*refs[:n_in], *refs[n_in + 1:])

        in_specs = list(in_specs) + [ANY]
        args = tuple(args) + (dep,)
    return pl.pallas_call(body, in_specs=list(in_specs), **kw)(*args)


def _tile(n, want):
    if n <= want:
        return n
    t = want - want % LANES
    while t > LANES and n % t:
        t -= LANES
    assert n % t == 0, (n, want)
    return t


def _sigmoid(x):
    return 1.0 / (1.0 + jnp.exp(-x))


def _rstd(x):
    return lax.rsqrt(jnp.mean(x * x, axis=-1, keepdims=True) + EPS)


def _rms_bwd(x, g, dh):
    r = _rstd(x)
    u = dh * g
    dx = r * (u - x * (r * r) * jnp.mean(u * x, axis=-1, keepdims=True))
    dg = jnp.sum(dh * x * r, axis=0, keepdims=True)
    return dx, dg


def _ln_fwd(x, g, b):
    mu = jnp.mean(x, axis=-1, keepdims=True)
    xc = x - mu
    r = lax.rsqrt(jnp.mean(xc * xc, axis=-1, keepdims=True) + EPS)
    xh = xc * r
    return xh * g + b, xh, r


def _ln_bwd(dy, xh, r, g):
    dxh = dy * g
    return r * (dxh - jnp.mean(dxh, axis=-1, keepdims=True)
                - xh * jnp.mean(dxh * xh, axis=-1, keepdims=True))


def _ffn_fwd(x, g, wall, base, name, dep=None):
    T, D = x.shape
    F = wall.shape[1]
    tm, tn = _tile(T, 512), _tile(F, 256)
    n_j = F // tn

    def body(x_ref, g_ref, wg_ref, wu_ref, wd_ref, xo_ref, a_ref, b_ref, h_ref, acc_ref):
        j = pl.program_id(1)

        @pl.when(j == 0)
        def _():
            xv = x_ref[...]
            h_ref[...] = (xv * _rstd(xv) * g_ref[...]).astype(BF16)
            acc_ref[...] = jnp.zeros_like(acc_ref)

        h = h_ref[...]
        a = _dot(h, wg_ref[...], NT)
        b = _dot(h, wu_ref[...], NT)
        a_ref[...] = a.astype(BF16)
        b_ref[...] = b.astype(BF16)
        s = (a * _sigmoid(a) * b).astype(BF16)
        acc_ref[...] += _dot(s, wd_ref[...], NN)

        @pl.when(j == n_j - 1)
        def _():
            xo_ref[...] = x_ref[...] + 0.5 * acc_ref[...]

    wspec = lambda k: pl.BlockSpec((None, tn, D), lambda i, j: (base + k, j, 0))
    return _pcall(
        body, name=name, grid=(T // tm, n_j), dep=dep, args=(x, g, wall, wall, wall),
        in_specs=[pl.BlockSpec((tm, D), lambda i, j: (i, 0)), pl.BlockSpec((1, D), lambda i, j: (0, 0)),
                  wspec(0), wspec(1), wspec(2)],
        out_specs=[pl.BlockSpec((tm, D), lambda i, j: (i, 0)), pl.BlockSpec((tm, tn), lambda i, j: (i, j)),
                   pl.BlockSpec((tm, tn), lambda i, j: (i, j)), pl.BlockSpec((tm, D), lambda i, j: (i, 0))],
        out_shape=[jax.ShapeDtypeStruct((T, D), F32), jax.ShapeDtypeStruct((T, F), BF16),
                   jax.ShapeDtypeStruct((T, F), BF16), jax.ShapeDtypeStruct((T, D), BF16)],
        scratch_shapes=[pltpu.VMEM((tm, D), F32)],
        compiler_params=_cp("parallel", "arbitrary"),
    )


def _ffn_bwd(go, x, g, a, b, wall, base, name, dep=None):
    T, D = x.shape
    F = wall.shape[1]
    tm, tn = _tile(T, 512), _tile(F, 256)
    n_j = F // tn

    def body(go_ref, x_ref, g_ref, a_ref, b_ref, wg_ref, wu_ref, wd_ref,
             gx_ref, dg_ref, da_ref, db_ref, s_ref, gh_ref, acc_ref):
        i, j = pl.program_id(0), pl.program_id(1)

        @pl.when(j == 0)
        def _():
            gh_ref[...] = (0.5 * go_ref[...]).astype(BF16)
            acc_ref[...] = jnp.zeros_like(acc_ref)

        @pl.when((i == 0) & (j == 0))
        def _():
            dg_ref[...] = jnp.zeros_like(dg_ref)

        ds = _dot(gh_ref[...], wd_ref[...], NT)
        av = a_ref[...].astype(F32)
        bv = b_ref[...].astype(F32)
        sig = _sigmoid(av)
        sl = av * sig
        dab = (ds * bv * (sig * (1.0 + av * (1.0 - sig)))).astype(BF16)
        dbb = (ds * sl).astype(BF16)
        s_ref[...] = (sl * bv).astype(BF16)
        da_ref[...] = dab
        db_ref[...] = dbb
        acc_ref[...] += _dot(dab, wg_ref[...], NN) + _dot(dbb, wu_ref[...], NN)

        @pl.when(j == n_j - 1)
        def _():
            dx, dg = _rms_bwd(x_ref[...], g_ref[...], acc_ref[...])
            gx_ref[...] = go_ref[...] + dx
            dg_ref[...] += dg

    wspec = lambda k: pl.BlockSpec((None, tn, D), lambda i, j: (base + k, j, 0))
    row = pl.BlockSpec((tm, D), lambda i, j: (i, 0))
    hid = pl.BlockSpec((tm, tn), lambda i, j: (i, j))
    vec = pl.BlockSpec((1, D), lambda i, j: (0, 0))
    return _pcall(
        body, name=name, grid=(T // tm, n_j), dep=dep, args=(go, x, g, a, b, wall, wall, wall),
        in_specs=[row, row, vec, hid, hid, wspec(0), wspec(1), wspec(2)],
        out_specs=[row, vec, hid, hid, hid, row],
        out_shape=[jax.ShapeDtypeStruct((T, D), F32), jax.ShapeDtypeStruct((1, D), F32),
                   jax.ShapeDtypeStruct((T, F), BF16), jax.ShapeDtypeStruct((T, F), BF16),
                   jax.ShapeDtypeStruct((T, F), BF16), jax.ShapeDtypeStruct((T, D), BF16)],
        scratch_shapes=[pltpu.VMEM((tm, D), F32)],
        compiler_params=_cp("arbitrary", "arbitrary"),
    )


def _mm_tn(a, b, name):
    T, M = a.shape
    N = b.shape[1]
    tmm, tk = _tile(M, 1536), _tile(T, 512)

    def body(a_ref, b_ref, o_ref):
        @pl.when(pl.program_id(1) == 0)
        def _():
            o_ref[...] = jnp.zeros_like(o_ref)

        o_ref[...] += _dot(a_ref[...].astype(BF16), b_ref[...].astype(BF16), TN)

    return pl.pallas_call(
        body, name=name, grid=(M // tmm, T // tk),
        in_specs=[pl.BlockSpec((tk, tmm), lambda m, k: (k, m)), pl.BlockSpec((tk, N), lambda m, k: (k, 0))],
        out_specs=pl.BlockSpec((tmm, N), lambda m, k: (m, 0)),
        out_shape=jax.ShapeDtypeStruct((M, N), F32),
        compiler_params=_cp("parallel", "arbitrary"),
    )(a, b)


def _mm_nt(x, wt, g, out_dtype, name, dep=None):
    T, K = x.shape
    N = wt.shape[0]
    tm, tn = _tile(T, 512), _tile(N, 512)
    norm = g is not None

    def body(*refs):
        if norm:
            x_ref, g_ref, w_ref, o_ref, h_ref = refs
        else:
            x_ref, w_ref, o_ref, h_ref = refs

        @pl.when(pl.program_id(1) == 0)
        def _():
            xv = x_ref[...].astype(F32)
            if norm:
                xv = xv * _rstd(xv) * g_ref[...]
            h_ref[...] = xv.astype(BF16)

        o_ref[...] = _dot(h_ref[...], w_ref[...], NT).astype(out_dtype)

    row = pl.BlockSpec((tm, K), lambda i, j: (i, 0))
    wsp = pl.BlockSpec((tn, K), lambda i, j: (j, 0))
    osp = pl.BlockSpec((tm, tn), lambda i, j: (i, j))
    if norm:
        return pl.pallas_call(
            body, name=name, grid=(T // tm, N // tn),
            in_specs=[row, pl.BlockSpec((1, K), lambda i, j: (0, 0)), wsp],
            out_specs=[osp, row],
            out_shape=[jax.ShapeDtypeStruct((T, N), out_dtype), jax.ShapeDtypeStruct((T, K), BF16)],
            compiler_params=_cp("parallel", "arbitrary"),
        )(x, g, wt)
    return _pcall(
        body, name=name, grid=(T // tm, N // tn), dep=dep, args=(x, wt),
        in_specs=[row, wsp], out_specs=osp,
        out_shape=jax.ShapeDtypeStruct((T, N), out_dtype),
        scratch_shapes=[pltpu.VMEM((tm, K), BF16)],
        compiler_params=_cp("parallel", "arbitrary"),
    )


def _mm_nn_res(act, w, resid, name):
    T, K = act.shape
    D = w.shape[1]
    tm = _tile(T, 512)

    def body(a_ref, w_ref, r_ref, o_ref):
        o_ref[...] = r_ref[...] + _dot(a_ref[...].astype(BF16), w_ref[...], NN)

    return pl.pallas_call(
        body, name=name, grid=(T // tm,),
        in_specs=[pl.BlockSpec((tm, K), lambda i: (i, 0)), pl.BlockSpec((K, D), lambda i: (0, 0)),
                  pl.BlockSpec((tm, D), lambda i: (i, 0))],
        out_specs=pl.BlockSpec((tm, D), lambda i: (i, 0)),
        out_shape=jax.ShapeDtypeStruct((T, D), F32),
        compiler_params=_cp("parallel"),
    )(act, w, resid)


def _mm_nn_rmsbwd(act, w, x, g, gprev, name, dep=None):
    T, K = act.shape
    D = w.shape[1]
    tm = _tile(T, 512)

    def body(a_ref, w_ref, x_ref, g_ref, gp_ref, o_ref, dg_ref):
        @pl.when(pl.program_id(0) == 0)
        def _():
            dg_ref[...] = jnp.zeros_like(dg_ref)

        dh = _dot(a_ref[...].astype(BF16), w_ref[...], NN)
        dx, dg = _rms_bwd(x_ref[...], g_ref[...], dh)
        o_ref[...] = gp_ref[...] + dx
        dg_ref[...] += dg

    row = pl.BlockSpec((tm, D), lambda i: (i, 0))
    vec = pl.BlockSpec((1, D), lambda i: (0, 0))
    return _pcall(
        body, name=name, grid=(T // tm,), dep=dep, args=(act, w, x, g, gprev),
        in_specs=[pl.BlockSpec((tm, K), lambda i: (i, 0)), pl.BlockSpec((K, D), lambda i: (0, 0)), row, vec, row],
        out_specs=[row, vec],
        out_shape=[jax.ShapeDtypeStruct((T, D), F32), jax.ShapeDtypeStruct((1, D), F32)],
        compiler_params=_cp("arbitrary"),
    )


def _loss_head(x, g, target):
    T, D = x.shape
    tm = _tile(T, 512)

    def body(x_ref, g_ref, t_ref, dx_ref, loss_ref, dg_ref):
        @pl.when(pl.program_id(0) == 0)
        def _():
            loss_ref[...] = jnp.zeros_like(loss_ref)
            dg_ref[...] = jnp.zeros_like(dg_ref)

        xv = x_ref[...]
        gv = g_ref[...]
        e = xv * _rstd(xv) * gv - t_ref[...]
        per_tok = jnp.sum(e * e, axis=-1, keepdims=True) * (1.0 / D)
        loss_ref[...] += 0.5 * jnp.sum(per_tok, axis=0, keepdims=True)
        dx, dg = _rms_bwd(xv, gv, e * (1.0 / D))
        dx_ref[...] = dx
        dg_ref[...] += dg

    row = pl.BlockSpec((tm, D), lambda i: (i, 0))
    vec = pl.BlockSpec((1, D), lambda i: (0, 0))
    return pl.pallas_call(
        body, name="loss_head", grid=(T // tm,),
        in_specs=[row, vec, row],
        out_specs=[row, pl.BlockSpec((1, LANES), lambda i: (0, 0)), vec],
        out_shape=[jax.ShapeDtypeStruct((T, D), F32), jax.ShapeDtypeStruct((1, LANES), F32),
                   jax.ShapeDtypeStruct((1, D), F32)],
        compiler_params=_cp("arbitrary"),
    )(x, g, target)


def _softplus(z):
    return jnp.maximum(z, 0.0) + jnp.log(1.0 + jnp.exp(-jnp.abs(z)))


def _cumsum_mm(v, u):
    hi = v.astype(BF16)
    lo = (v - hi.astype(F32)).astype(BF16)
    return _dot(hi, u, NN) + _dot(lo, u, NN)


def _half_rowsum(v):
    n = v.shape[0]
    s0 = jnp.sum(v[:, :QB], axis=1, keepdims=True)
    s1 = jnp.sum(v[:, QB:], axis=1, keepdims=True)
    return jnp.concatenate([jnp.broadcast_to(s0, (n, QB)), jnp.broadcast_to(s1, (n, QB))], axis=1)


def _stack_heads(src_ref, dst_ref, n_blk):
    m0 = lax.broadcasted_iota(jnp.int32, (1, LANES), 1) < HEAD_DIM

    def fill(c, carry):
        blk = src_ref[pl.ds(pl.multiple_of(c * QB, QB), QB), :]
        zero = jnp.zeros_like(blk)
        dst_ref[c, 0:QB, :] = jnp.where(m0, blk, zero)
        dst_ref[c, QB:2 * QB, :] = jnp.where(m0, zero, blk)
        return carry

    lax.fori_loop(0, n_blk, fill, 0)


def _causal_diff(tq):
    row = lax.broadcasted_iota(jnp.int32, (tq, 2 * QB), 0)
    col = lax.broadcasted_iota(jnp.int32, (tq, 2 * QB), 1)
    return row - (col & (QB - 1))


def _tri_blockdiag(upper):
    r = lax.broadcasted_iota(jnp.int32, (2 * QB, 2 * QB), 0)
    c = lax.broadcasted_iota(jnp.int32, (2 * QB, 2 * QB), 1)
    same = (r // QB) == (c // QB)
    return (same & ((r > c) if upper else (r < c))).astype(BF16)


def _attn_tiles(T, n_seq):
    S = T // n_seq
    tq = ATT_TQ if S % ATT_TQ == 0 else QB
    return S, tq, tq // QB, S // tq, S // QB


def _attn_fwd(qkv, n_seq):
    T = qkv.shape[0]
    S, tq, r, n_q, n_k = _attn_tiles(T, n_seq)
    n_p = D_MODEL // LANES
    u_suffix = _tri_blockdiag(True)

    def body(q_ref, k_ref, v_ref, u_ref, o_ref, tot_ref, kk_ref, vv_ref, lr_s, acc_s):
        qi = pl.program_id(2)

        @pl.when(qi == 0)
        def _():
            _stack_heads(k_ref, kk_ref, n_k)
            _stack_heads(v_ref, vv_ref, n_k)

        q = q_ref[...] * ATT_SCALE
        u = u_ref[...]
        diff = _causal_diff(tq)
        lr_s[...] = jnp.zeros_like(lr_s)
        acc_s[...] = jnp.zeros_like(acc_s)

        def step(kj, masked, lr, acc):
            z = _dot(q, kk_ref[kj], NT)
            sp = _softplus(z)
            lk = -sp
            if masked:
                mask = kj * QB - qi * tq < diff
                lk = jnp.where(mask, lk, 0.0)
            a = jnp.exp((z - sp) + _cumsum_mm(lk, u) + lr)
            if masked:
                a = jnp.where(mask, a, 0.0)
            return lr + _half_rowsum(lk), acc + _dot(a.astype(BF16), vv_ref[kj], NN)

        def group(first, masked):
            lr, acc = lr_s[...], acc_s[...]
            for j in range(r):
                lr, acc = step(first - j, masked, lr, acc)
            lr_s[...] = lr
            acc_s[...] = acc

        group((qi + 1) * r - 1, True)

        def off(it, carry):
            group((qi - it) * r - 1, False)
            return carry

        lax.fori_loop(0, qi, off, 0)
        o_ref[...] = acc_s[...].astype(BF16)
        tot_ref[...] = lr_s[...]

    return pl.pallas_call(
        body, name="attn_fwd", grid=(n_seq, n_p, n_q),
        in_specs=[pl.BlockSpec((tq, LANES), lambda b, p, qi: (b * n_q + qi, p)),
                  pl.BlockSpec((S, LANES), lambda b, p, qi: (b, n_p + p)),
                  pl.BlockSpec((S, LANES), lambda b, p, qi: (b, 2 * n_p + p)),
                  pl.BlockSpec((2 * QB, 2 * QB), lambda b, p, qi: (0, 0))],
        out_specs=[pl.BlockSpec((tq, LANES), lambda b, p, qi: (b * n_q + qi, p)),
                   pl.BlockSpec((tq, 2 * QB), lambda b, p, qi: (b * n_q + qi, p))],
        out_shape=[jax.ShapeDtypeStruct((T, D_MODEL), BF16), jax.ShapeDtypeStruct((T, 2 * D_MODEL), F32)],
        scratch_shapes=[pltpu.VMEM((n_k, 2 * QB, LANES), BF16), pltpu.VMEM((n_k, 2 * QB, LANES), BF16),
                        pltpu.VMEM((tq, 2 * QB), F32), pltpu.VMEM((tq, LANES), F32)],
        compiler_params=_cp("parallel", "parallel", "arbitrary"),
    )(qkv, qkv, qkv, u_suffix)


def _attn_bwd(qkv, do, tot, n_seq):
    T = qkv.shape[0]
    S, tq, r, n_q, n_k = _attn_tiles(T, n_seq)
    n_p = D_MODEL // LANES
    u_prefix = _tri_blockdiag(False)

    def body(q_ref, k_ref, v_ref, do_ref, tot_ref, u_ref, dq_ref, dk_ref, dv_ref,
             kk_ref, vv_ref, cl_s, cg_s, dq_s):
        qi = pl.program_id(2)

        @pl.when(qi == 0)
        def _():
            _stack_heads(k_ref, kk_ref, n_k)
            _stack_heads(v_ref, vv_ref, n_k)
            dk_ref[...] = jnp.zeros_like(dk_ref)
            dv_ref[...] = jnp.zeros_like(dv_ref)

        q = q_ref[...] * ATT_SCALE
        dov = do_ref[...]
        tot2 = tot_ref[...]
        u = u_ref[...]
        diff = _causal_diff(tq)
        m0 = lax.broadcasted_iota(jnp.int32, (1, LANES), 1) < HEAD_DIM
        cl_s[...] = jnp.zeros_like(cl_s)
        cg_s[...] = jnp.zeros_like(cg_s)
        dq_s[...] = jnp.zeros_like(dq_s)

        def step(kj, masked, cl, cg, dq):
            kk = kk_ref[kj]
            z = _dot(q, kk, NT)
            sp = _softplus(z)
            lk = -sp
            if masked:
                mask = kj * QB - qi * tq < diff
                lk = jnp.where(mask, lk, 0.0)
            ls = z - sp
            a = jnp.exp(ls + (tot2 - cl - (_cumsum_mm(lk, u) + lk)))
            if masked:
                a = jnp.where(mask, a, 0.0)
            g = a * _dot(dov, vv_ref[kj], NT)
            dz = g - (g + _cumsum_mm(g, u) + cg) * jnp.exp(ls)
            if masked:
                dz = jnp.where(mask, dz, 0.0)
            dz = dz.astype(BF16)
            rows = pl.ds(pl.multiple_of(kj * QB, QB), QB)
            dvt = _dot(a.astype(BF16), dov, TN)
            dv_ref[rows, :] += jnp.where(m0, dvt[:QB], dvt[QB:])
            dkt = _dot(dz, q, TN)
            dk_ref[rows, :] += jnp.where(m0, dkt[:QB], dkt[QB:])
            return cl + _half_rowsum(lk), cg + _half_rowsum(g), dq + _dot(dz, kk, NN)

        def group(first, masked):
            cl, cg, dq = cl_s[...], cg_s[...], dq_s[...]
            for j in range(r):
                cl, cg, dq = step(first + j, masked, cl, cg, dq)
            cl_s[...] = cl
            cg_s[...] = cg
            dq_s[...] = dq

        def off(it, carry):
            group(it * r, False)
            return carry

        lax.fori_loop(0, qi, off, 0)
        group(qi * r, True)
        dq_ref[...] = dq_s[...] * ATT_SCALE

    qspec = pl.BlockSpec((tq, LANES), lambda b, p, qi: (b * n_q + qi, p))
    seq = lambda off: pl.BlockSpec((S, LANES), lambda b, p, qi: (b, off + p))
    return pl.pallas_call(
        body, name="attn_bwd", grid=(n_seq, n_p, n_q),
        in_specs=[qspec, seq(n_p), seq(2 * n_p), qspec,
                  pl.BlockSpec((tq, 2 * QB), lambda b, p, qi: (b * n_q + qi, p)),
                  pl.BlockSpec((2 * QB, 2 * QB), lambda b, p, qi: (0, 0))],
        out_specs=[qspec, seq(0), seq(0)],
        out_shape=[jax.ShapeDtypeStruct((T, D_MODEL), F32)] * 3,
        scratch_shapes=[pltpu.VMEM((n_k, 2 * QB, LANES), BF16), pltpu.VMEM((n_k, 2 * QB, LANES), BF16),
                        pltpu.VMEM((tq, 2 * QB), F32), pltpu.VMEM((tq, 2 * QB), F32), pltpu.VMEM((tq, LANES), F32)],
        compiler_params=_cp("parallel", "parallel", "arbitrary"),
    )(qkv, qkv, qkv, do, tot, u_prefix)


def _glu_with_halo(av_ref, ag_ref, avh_ref, agh_ref, a0_s, first, ts):
    hal = avh_ref[...] * _sigmoid(agh_ref[...])
    a0_s[0:HALO, :] = jnp.where(first, 0.0, hal)
    a0_s[HALO:HALO + ts, :] = av_ref[...] * _sigmoid(ag_ref[...])


def _mix_specs(ts, n_r, with_left):
    blk = lambda c: pl.BlockSpec((ts, CA), lambda b, r: (b * n_r + r, c))
    per = ts // HALO
    left = lambda c: pl.BlockSpec((HALO, CA), lambda b, r: (jnp.maximum((b * n_r + r) * per - 1, 0), c))
    return blk, (left if with_left else None)


def _mix_fwd(z, conv_w, conv_b, ln_a_g, ln_a_b, ln_v_g, ln_v_b, ws, bias2d, n_seq):
    T = z.shape[0]
    S = T // n_seq
    ts = _tile(S, 512)
    n_r = S // ts
    shift = HALO - (CONV_WIDTH - 1)

    def body(av_ref, ag_ref, avh_ref, agh_ref, u_ref, v_ref, cw_ref, cb_ref, lag_ref, lab_ref,
             lvg_ref, lvb_ref, ws_ref, bias_ref, cat_ref, a1_ref, a0_s):
        _glu_with_halo(av_ref, ag_ref, avh_ref, agh_ref, a0_s, pl.program_id(1) == 0, ts)
        for rb in range(ts // CONV_ROWS):
            base = rb * CONV_ROWS
            acc = jnp.broadcast_to(cb_ref[...], (CONV_ROWS, CA))
            for k in range(CONV_WIDTH):
                acc = acc + cw_ref[k:k + 1, :] * a0_s[base + shift + k:base + shift + k + CONV_ROWS, :]
            a1_ref[base:base + CONV_ROWS, :] = acc
        y, _, _ = _ln_fwd(a1_ref[...], lag_ref[...], lab_ref[...])
        cat_ref[:, 0:CA] = (y * _sigmoid(y)).astype(BF16)
        for gi in range(GB):
            sl = slice(gi * DB, (gi + 1) * DB)
            v1, _, _ = _ln_fwd(v_ref[:, sl], lvg_ref[:, sl], lvb_ref[:, sl])
            v1 = v1.astype(BF16)
            for c in range(ts // CHUNK):
                rs = slice(c * CHUNK, (c + 1) * CHUNK)
                v2 = _dot(ws_ref[gi], v1[rs], NN) + bias_ref[:, sl]
                cat_ref[rs, CA + gi * DB:CA + (gi + 1) * DB] = (u_ref[rs, sl] * v2).astype(BF16)

    blk, left = _mix_specs(ts, n_r, True)
    vec = pl.BlockSpec((1, CA), lambda b, r: (0, 0))
    return pl.pallas_call(
        body, name="mix_fwd", grid=(n_seq, n_r),
        in_specs=[blk(0), blk(1), left(0), left(1), blk(2), blk(3),
                  pl.BlockSpec((CONV_WIDTH, CA), lambda b, r: (0, 0)), vec, vec, vec, vec, vec,
                  pl.BlockSpec((GB, CHUNK, CHUNK), lambda b, r: (0, 0, 0)),
                  pl.BlockSpec((CHUNK, CB), lambda b, r: (0, 0))],
        out_specs=[pl.BlockSpec((ts, CA + CB), lambda b, r: (b * n_r + r, 0)), blk(0)],
        out_shape=[jax.ShapeDtypeStruct((T, CA + CB), BF16), jax.ShapeDtypeStruct((T, CA), F32)],
        scratch_shapes=[pltpu.VMEM((HALO + ts, CA), F32)],
        compiler_params=_cp("parallel", "parallel"),
    )(z, z, z, z, z, z, conv_w, conv_b, ln_a_g, ln_a_b, ln_v_g, ln_v_b, ws, bias2d)


def _mix_bwd_rows(dcat, z, a1, ln_a_g, ln_a_b, ln_v_g, ln_v_b, ws, ws_t, bias2d, n_seq):
    T = z.shape[0]
    S = T // n_seq
    ts = _tile(S, 512)
    n_r = S // ts

    def body(dc_ref, u_ref, v_ref, a1_ref, lag_ref, lab_ref, lvg_ref, lvb_ref, ws_ref, wst_ref, bias_ref,
             da1_ref, dz_ref, dlag_ref, dlab_ref, dlvg_ref, dlvb_ref, dws_ref, dsb_ref, dv1_s, dbias_s):
        first = (pl.program_id(0) == 0) & (pl.program_id(1) == 0)
        last = (pl.program_id(0) == n_seq - 1) & (pl.program_id(1) == n_r - 1)

        @pl.when(first)
        def _():
            for ref in (dlag_ref, dlab_ref, dlvg_ref, dlvb_ref, dws_ref, dbias_s):
                ref[...] = jnp.zeros_like(ref)

        lag = lag_ref[...]
        y, xh, r = _ln_fwd(a1_ref[...], lag, lab_ref[...])
        sig = _sigmoid(y)
        dy = dc_ref[:, 0:CA] * (sig * (1.0 + y * (1.0 - sig)))
        dlag_ref[...] += jnp.sum(dy * xh, axis=0, keepdims=True)
        dlab_ref[...] += jnp.sum(dy, axis=0, keepdims=True)
        da1_ref[...] = _ln_bwd(dy, xh, r, lag)

        tril = (lax.broadcasted_iota(jnp.int32, (CHUNK, CHUNK), 0)
                >= lax.broadcasted_iota(jnp.int32, (CHUNK, CHUNK), 1))
        for gi in range(GB):
            sl = slice(gi * DB, (gi + 1) * DB)
            lvg = lvg_ref[:, sl]
            v1, vh, vr = _ln_fwd(v_ref[:, sl], lvg, lvb_ref[:, sl])
            v1 = v1.astype(BF16)
            for c in range(ts // CHUNK):
                rs = slice(c * CHUNK, (c + 1) * CHUNK)
                v2 = _dot(ws_ref[gi], v1[rs], NN) + bias_ref[:, sl]
                dbo = dc_ref[rs, CA + gi * DB:CA + (gi + 1) * DB]
                dz_ref[rs, sl] = (dbo * v2).astype(BF16)
                dv2 = dbo * u_ref[rs, sl]
                dbias_s[:, sl] += dv2
                dv2b = dv2.astype(BF16)
                dws_ref[gi] += jnp.where(tril, _dot(dv2b, v1[rs], NT), 0.0)
                dv1_s[rs, :] = _dot(wst_ref[gi], dv2b, NN)
            dv1 = dv1_s[...]
            dlvg_ref[:, sl] += jnp.sum(dv1 * vh, axis=0, keepdims=True)
            dlvb_ref[:, sl] += jnp.sum(dv1, axis=0, keepdims=True)
            dz_ref[:, CB + gi * DB:CB + (gi + 1) * DB] = _ln_bwd(dv1, vh, vr, lvg).astype(BF16)

        @pl.when(last)
        def _():
            col = lax.broadcasted_iota(jnp.int32, (CHUNK, GB), 1)
            out = jnp.zeros((CHUNK, GB), F32)
            for gi in range(GB):
                s = jnp.sum(dbias_s[:, gi * DB:(gi + 1) * DB], axis=1, keepdims=True)
                out = out + jnp.where(col == gi, s, 0.0)
            dsb_ref[...] = out

    blk, _ = _mix_specs(ts, n_r, False)
    vec = pl.BlockSpec((1, CA), lambda b, r: (0, 0))
    mat = pl.BlockSpec((GB, CHUNK, CHUNK), lambda b, r: (0, 0, 0))
    wide = pl.BlockSpec((ts, CA + CB), lambda b, r: (b * n_r + r, 0))
    return pl.pallas_call(
        body, name="mix_bwd_rows", grid=(n_seq, n_r),
        in_specs=[wide, blk(2), blk(3), blk(0), vec, vec, vec, vec, mat, mat,
                  pl.BlockSpec((CHUNK, CB), lambda b, r: (0, 0))],
        out_specs=[blk(0), wide, vec, vec, vec, vec, mat, pl.BlockSpec((CHUNK, GB), lambda b, r: (0, 0))],
        out_shape=[jax.ShapeDtypeStruct((T, CA), F32), jax.ShapeDtypeStruct((T, 2 * CB), BF16)]
        + [jax.ShapeDtypeStruct((1, CA), F32)] * 4
        + [jax.ShapeDtypeStruct((GB, CHUNK, CHUNK), F32), jax.ShapeDtypeStruct((CHUNK, GB), F32)],
        scratch_shapes=[pltpu.VMEM((ts, DB), F32), pltpu.VMEM((CHUNK, CB), F32)],
        compiler_params=_cp("arbitrary", "arbitrary"),
    )(dcat, z, z, a1, ln_a_g, ln_a_b, ln_v_g, ln_v_b, ws, ws_t, bias2d)


def _mix_bwd_conv(da1, z, conv_w, n_seq):
    T = z.shape[0]
    S = T // n_seq
    ts = _tile(S, 512)
    n_r = S // ts
    per = ts // HALO
    shift = HALO - (CONV_WIDTH - 1)
    fold = CONV_ROWS // 8

    def body(d_ref, dh_ref, av_ref, ag_ref, avh_ref, agh_ref, cw_ref,
             dz_ref, dcw_ref, dcb_ref, a0_s, d1_s, da0_s, dw8_s):
        first = (pl.program_id(0) == 0) & (pl.program_id(1) == 0)
        last = (pl.program_id(0) == n_seq - 1) & (pl.program_id(1) == n_r - 1)

        @pl.when(first)
        def _():
            dw8_s[...] = jnp.zeros_like(dw8_s)
            dcb_ref[...] = jnp.zeros_like(dcb_ref)

        _glu_with_halo(av_ref, ag_ref, avh_ref, agh_ref, a0_s, pl.program_id(1) == 0, ts)
        d1_s[0:ts, :] = d_ref[...]
        d1_s[ts:ts + HALO, :] = jnp.where(pl.program_id(1) == n_r - 1, 0.0, dh_ref[...])
        dcb_ref[...] += jnp.sum(d_ref[...], axis=0, keepdims=True)
        for rb in range(ts // CONV_ROWS):
            base = rb * CONV_ROWS
            dcur = d1_s[base:base + CONV_ROWS, :]
            acc = jnp.zeros((CONV_ROWS, CA), F32)
            for k in range(CONV_WIDTH):
                back = CONV_WIDTH - 1 - k
                acc = acc + cw_ref[k:k + 1, :] * d1_s[base + back:base + back + CONV_ROWS, :]
                prod = dcur * a0_s[base + shift + k:base + shift + k + CONV_ROWS, :]
                part = prod[0:8]
                for f in range(1, fold):
                    part = part + prod[8 * f:8 * f + 8]
                dw8_s[k] += part
            da0_s[base:base + CONV_ROWS, :] = acc
        da0 = da0_s[...]
        sig = _sigmoid(ag_ref[...])
        dz_ref[:, 0:CA] = (da0 * sig).astype(BF16)
        dz_ref[:, CA:2 * CA] = (da0 * av_ref[...] * sig * (1.0 - sig)).astype(BF16)

        @pl.when(last)
        def _():
            for k in range(CONV_WIDTH):
                dcw_ref[k:k + 1, :] = jnp.sum(dw8_s[k], axis=0, keepdims=True)

    blk, left = _mix_specs(ts, n_r, True)
    n_halo_blocks = T // HALO
    right = pl.BlockSpec((HALO, CA), lambda b, r: (jnp.minimum((b * n_r + r + 1) * per, n_halo_blocks - 1), 0))
    return pl.pallas_call(
        body, name="mix_bwd_conv", grid=(n_seq, n_r),
        in_specs=[blk(0), right, blk(0), blk(1), left(0), left(1),
                  pl.BlockSpec((CONV_WIDTH, CA), lambda b, r: (0, 0))],
        out_specs=[pl.BlockSpec((ts, 2 * CA), lambda b, r: (b * n_r + r, 0)),
                   pl.BlockSpec((CONV_WIDTH, CA), lambda b, r: (0, 0)), pl.BlockSpec((1, CA), lambda b, r: (0, 0))],
        out_shape=[jax.ShapeDtypeStruct((T, 2 * CA), BF16), jax.ShapeDtypeStruct((CONV_WIDTH, CA), F32),
                   jax.ShapeDtypeStruct((1, CA), F32)],
        scratch_shapes=[pltpu.VMEM((HALO + ts, CA), F32), pltpu.VMEM((ts + HALO, CA), F32),
                        pltpu.VMEM((ts, CA), F32), pltpu.VMEM((CONV_WIDTH, 8, CA), F32)],
        compiler_params=_cp("arbitrary", "arbitrary"),
    )(da1, da1, z, z, z, z, conv_w)


def _sum_parts(parts, name):
    n, R, C = parts.shape
    tr = _tile(R, 512) if R % 8 == 0 and R > 512 else R
    if R % tr:
        tr = R

    def body(p_ref, o_ref):
        acc = p_ref[0]
        for k in range(1, n):
            acc = acc + p_ref[k]
        o_ref[...] = acc

    return pl.pallas_call(
        body, name=name, grid=(R // tr,),
        in_specs=[pl.BlockSpec((n, tr, C), lambda i: (0, i, 0))],
        out_specs=pl.BlockSpec((tr, C), lambda i: (i, 0)),
        out_shape=jax.ShapeDtypeStruct((R, C), F32),
        compiler_params=_cp("parallel"),
    )(parts)


def _row_tile(R, want):
    t = min(R, want)
    t -= t % 8
    while t > 8 and R % t:
        t -= 8
    return t if t >= 8 and R % t == 0 else R


def _adamw(w, g, m, v, name):
    R, C = w.shape
    tr = _row_tile(R, 256)
    c1 = 1.0 - ADAM_B1 ** ADAM_STEP
    c2 = 1.0 - ADAM_B2 ** ADAM_STEP

    def body(w_ref, g_ref, m_ref, v_ref, d_ref, nm_ref, nv_ref):
        gv = g_ref[...]
        nm = ADAM_B1 * m_ref[...] + (1.0 - ADAM_B1) * gv
        nv = ADAM_B2 * v_ref[...] + (1.0 - ADAM_B2) * (gv * gv)
        d_ref[...] = -ADAM_LR * ((nm / c1) / (jnp.sqrt(nv / c2) + ADAM_EPS) + ADAM_WD * w_ref[...])
        nm_ref[...] = nm
        nv_ref[...] = nv

    blk = pl.BlockSpec((tr, C), lambda i: (i, 0))
    return pl.pallas_call(
        body, name=name, grid=(R // tr,),
        in_specs=[blk] * 4, out_specs=[blk] * 3,
        out_shape=[jax.ShapeDtypeStruct((R, C), F32)] * 3,
        compiler_params=_cp("parallel"),
    )(w, g, m, v)


def _me():
    return lax.axis_index("x"), lax.axis_index("y"), lax.axis_index("c")


def _block_rows(ref, dev, n):
    start = (4 * dev[0] + 2 * dev[1] + dev[2]) * n
    if len(ref.shape) == 2:
        return ref.at[pl.ds(start, n), :]
    return ref.at[:, pl.ds(start, n), :]


def _all_gather(shards):
    na = len(shards)
    ns = [s.shape[-2] for s in shards]

    def body(*refs):
        ins, outs = refs[:na], refs[na:2 * na]
        send_sems, recv_sems, local_sems = refs[2 * na:]
        x, y, c = _me()
        me, sibling = (x, y, c), (x, y, 1 - c)
        chips = [(1 - x, y), (x, 1 - y), (1 - x, 1 - y)]

        def copy(a, k, block, to, src=None):
            dst = _block_rows(outs[a], block, ns[a])
            return pltpu.make_async_remote_copy(
                src_ref=dst if src is None else src, dst_ref=dst,
                send_sem=send_sems.at[a, k], recv_sem=recv_sems.at[a, k], device_id=to, device_id_type=MESH)

        mine = [pltpu.make_async_copy(ins[a], _block_rows(outs[a], me, ns[a]), local_sems.at[a]) for a in range(na)]
        for cp in mine:
            cp.start()
        first = []
        for a in range(na):
            first.append(copy(a, 0, me, sibling, src=ins[a]))
            first += [copy(a, 1 + j, me, (*chip, c), src=ins[a]) for j, chip in enumerate(chips)]
        for cp in first:
            cp.start()
        passed = []
        for j, chip in enumerate(chips):
            for a in range(na):
                copy(a, 1 + j, (*chip, c), me).wait_recv()
                fwd = copy(a, 4 + j, (*chip, c), sibling)
                fwd.start()
                passed.append(fwd)
        for a in range(na):
            copy(a, 0, sibling, me).wait_recv()
            for j, chip in enumerate(chips):
                copy(a, 4 + j, (*chip, 1 - c), me).wait_recv()
        for cp in first + passed:
            cp.wait_send()
        for cp in mine:
            cp.wait()

    out_shape = [jax.ShapeDtypeStruct(s.shape[:-2] + (N_DEV * s.shape[-2], s.shape[-1]), s.dtype) for s in shards]
    return pl.pallas_call(
        body, name="weights_all_gather",
        in_specs=[ANY] * na, out_specs=[ANY] * na, out_shape=out_shape,
        scratch_shapes=[pltpu.SemaphoreType.DMA((na, 7)), pltpu.SemaphoreType.DMA((na, 7)),
                        pltpu.SemaphoreType.DMA((na,))],
    )(*shards)


def _split_copies(gather, srcs, lands, send_sems, recv_sems, ns):
    x, y, c = _me()
    me = (x, y, c)
    my_slot = 4 * x + 2 * y + c
    copies = []
    for mask in range(1, N_DEV):
        peer = (x ^ (mask >> 2), y ^ ((mask >> 1) & 1), c ^ (mask & 1))
        for a in range(len(srcs)):
            if gather:
                src, dst = srcs[a], _block_rows(lands[a], me, ns[a])
            else:
                src, dst = _block_rows(srcs[a], peer, ns[a]), lands[a].at[my_slot]
            sem = a * (N_DEV - 1) + mask - 1
            copies.append(pltpu.make_async_remote_copy(
                src_ref=src, dst_ref=dst, send_sem=send_sems.at[sem], recv_sem=recv_sems.at[sem],
                device_id=peer, device_id_type=MESH))
    return copies


HBM_SPEC = pl.BlockSpec(memory_space=pltpu.HBM)
SEM_SPEC = pl.BlockSpec(memory_space=pltpu.SEMAPHORE)


def _split_start(gather, srcs, name, dep=None):
    na = len(srcs)
    if gather:
        ns = [s.shape[-2] for s in srcs]
        lands = [lax.empty(s.shape[:-2] + (N_DEV * s.shape[-2], s.shape[-1]), s.dtype) for s in srcs]
    else:
        ns = [s.shape[-2] // N_DEV for s in srcs]
        lands = [lax.empty((N_DEV, s.shape[-2] // N_DEV, s.shape[-1]), s.dtype) for s in srcs]
    n_in = 2 * na + (dep is not None)

    def body(*refs):
        send_sems, recv_sems = refs[n_in], refs[n_in + 1]
        for cp in _split_copies(gather, refs[:na], refs[na:2 * na], send_sems, recv_sems, ns):
            cp.start()
        refs[-1][...] = jnp.zeros_like(refs[-1])

    hbm = lambda a: pltpu.with_memory_space_constraint(a, pltpu.HBM)
    args = [hbm(a) for a in srcs] + [hbm(a) for a in lands] + ([dep] if dep is not None else [])
    out = pl.pallas_call(
        body, name=name,
        in_specs=[HBM_SPEC] * (2 * na) + ([ANY] if dep is not None else []),
        out_specs=[SEM_SPEC, SEM_SPEC] + [HBM_SPEC] * (2 * na) + [pl.BlockSpec(memory_space=pltpu.VMEM)],
        out_shape=[pltpu.SemaphoreType.DMA((na * (N_DEV - 1),)), pltpu.SemaphoreType.DMA((na * (N_DEV - 1),))]
        + [pltpu.HBM(a.shape, a.dtype) for a in srcs + lands] + [jax.ShapeDtypeStruct((8, LANES), F32)],
        input_output_aliases={i: 2 + i for i in range(2 * na)},
        compiler_params=pltpu.CompilerParams(has_side_effects=pltpu.SideEffectType.DATAFLOW_SIDE_EFFECTING),
    )(*args)
    return (gather, ns, out[0], out[1], list(out[2:2 + na]), list(out[2 + na:2 + 2 * na])), out[-1]


def _split_wait(handle, after, name):
    gather, ns, send, recv, srcs, lands = handle
    na = len(srcs)

    def body(*refs):
        send_sems, recv_sems = refs[2 * na], refs[2 * na + 1]
        for cp in _split_copies(gather, refs[:na], refs[na:2 * na], send_sems, recv_sems, ns):
            cp.wait_send()
            cp.wait_recv()

    out = pl.pallas_call(
        body, name=name,
        in_specs=[HBM_SPEC] * (2 * na) + [SEM_SPEC, SEM_SPEC, ANY],
        out_specs=[HBM_SPEC] * (2 * na),
        out_shape=[pltpu.HBM(a.shape, a.dtype) for a in srcs + lands],
        input_output_aliases={i: i for i in range(2 * na)},
        compiler_params=pltpu.CompilerParams(has_side_effects=pltpu.SideEffectType.DATAFLOW_SIDE_EFFECTING),
    )(*srcs, *lands, send, recv, after)
    return list(out[:na]), list(out[na:])


def _small_all_reduce(buf):
    R = buf.shape[0]

    def body(b_ref, o_ref, recv_ref, send_sems, recv_sems):
        x, y, c = _me()
        my_slot = 4 * x + 2 * y + c
        recv_ref[my_slot] = b_ref[...]
        copies = []
        for mask in range(1, N_DEV):
            peer = (x ^ (mask >> 2), y ^ ((mask >> 1) & 1), c ^ (mask & 1))
            copies.append(pltpu.make_async_remote_copy(
                src_ref=b_ref, dst_ref=recv_ref.at[my_slot],
                send_sem=send_sems.at[mask - 1], recv_sem=recv_sems.at[mask - 1],
                device_id=peer, device_id_type=MESH))
        for cp in copies:
            cp.start()
        for cp in copies:
            cp.wait()
        acc = recv_ref[0]
        for k in range(1, N_DEV):
            acc = acc + recv_ref[k]
        o_ref[...] = acc

    return pl.pallas_call(
        body, name="small_all_reduce",
        in_specs=[pl.BlockSpec(memory_space=pltpu.VMEM)], out_specs=pl.BlockSpec(memory_space=pltpu.VMEM),
        out_shape=jax.ShapeDtypeStruct((R, LANES), F32),
        scratch_shapes=[pltpu.VMEM((N_DEV, R, LANES), F32), pltpu.SemaphoreType.DMA((7,)),
                        pltpu.SemaphoreType.DMA((7,))],
        compiler_params=pltpu.CompilerParams(vmem_limit_bytes=VMEM_LIMIT),
    )(buf)


def _pack(arrays):
    flat = jnp.concatenate([a.reshape(-1) for a in arrays])
    pad = (-flat.shape[0]) % (8 * LANES)
    return jnp.pad(flat, (0, pad)).reshape(-1, LANES)


def _unpack(buf, shapes):
    flat = buf.reshape(-1)
    out, off = [], 0
    for s in shapes:
        n = 1
        for d in s:
            n *= d
        out.append(flat[off:off + n].reshape(s))
        off += n
    return out


def _ffn_index(layer, second):
    return (2 * layer + second) * 3


def kernel(x, g_ffn1, w_ffn1_gate, w_ffn1_up, w_ffn1_down, g_mix, w_in_ab, conv_w, conv_b, ln_a_g, ln_a_b, ln_v_g, ln_v_b, sp_w, sp_b, w_out_ab, w_qkv, w_o, g_ffn2, w_ffn2_gate, w_ffn2_up, w_ffn2_down, g_final, loss_target, m_g_ffn1, m_w_ffn1_gate, m_w_ffn1_up, m_w_ffn1_down, m_g_mix, m_w_in_ab, m_conv_w, m_conv_b, m_ln_a_g, m_ln_a_b, m_ln_v_g, m_ln_v_b, m_sp_w, m_sp_b, m_w_out_ab, m_w_qkv, m_w_o, m_g_ffn2, m_w_ffn2_gate, m_w_ffn2_up, m_w_ffn2_down, m_g_final, v_g_ffn1, v_w_ffn1_gate, v_w_ffn1_up, v_w_ffn1_down, v_g_mix, v_w_in_ab, v_conv_w, v_conv_b, v_ln_a_g, v_ln_a_b, v_ln_v_g, v_ln_v_b, v_sp_w, v_sp_b, v_w_out_ab, v_w_qkv, v_w_o, v_g_ffn2, v_w_ffn2_gate, v_w_ffn2_up, v_w_ffn2_down, v_g_final):
    n_seq, S, D = x.shape
    T = n_seq * S
    depth = g_ffn1.shape[0]
    assert depth == 2 and D == D_MODEL
    my_block = 4 * lax.axis_index("x") + 2 * lax.axis_index("y") + lax.axis_index("c")

    ffn_parts = []
    for l in range(depth):
        for gate, up, down in ((w_ffn1_gate, w_ffn1_up, w_ffn1_down), (w_ffn2_gate, w_ffn2_up, w_ffn2_down)):
            ffn_parts += [gate[l].T, up[l].T, down[l]]
    ffn_shard = lambda k: jnp.stack(ffn_parts[3 * k:3 * k + 3]).astype(BF16)
    conv_w_pad = jnp.zeros((HALO, conv_w.shape[2]), F32).at[:CONV_WIDTH].set(conv_w[0]).T
    w_ffn = [None] * (2 * depth)
    w_ffn[0], conv_w_t = _all_gather([ffn_shard(0), conv_w_pad])
    conv_w_full = conv_w_t.T[:CONV_WIDTH]
    shards_a = [w_in_ab[0].T.astype(BF16), w_out_ab[0].astype(BF16), ffn_shard(1)]
    shards_b = [ffn_shard(2), w_qkv[0].T.astype(BF16), w_o[0].astype(BF16), ffn_shard(3)]
    gather_a, token_a = _split_start(True, shards_a, "gather_a_start", dep=conv_w_t)
    gather_b, token = _split_start(True, shards_b, "gather_b_start", dep=token_a)

    def gathered(handle, after, name):
        shards, lands = _split_wait(handle, after, name)
        out = []
        for land, shard in zip(lands, shards):
            n = shard.shape[-2]
            start = (0,) * (land.ndim - 2) + (my_block * n, 0)
            out.append(lax.dynamic_update_slice(land, shard, start))
        return out

    row = lambda a: a.reshape(1, -1)
    tril = jnp.tril(jnp.ones((CHUNK, CHUNK), dtype=bool))
    ws = jnp.where(tril[None], sp_w[0], 0.0).astype(BF16)
    ws_t = jnp.swapaxes(ws, 1, 2)
    bias2d = jnp.repeat(sp_b[0].T, DB, axis=1)
    conv_b2, lag, lab = row(conv_b[0]), row(ln_a_g[0]), row(ln_a_b[0])
    lvg, lvb = row(ln_v_g[0]), row(ln_v_b[0])

    x0 = x.reshape(T, D)
    target = loss_target.reshape(T, D)
    saved = []
    xc = x0
    for l in range(depth):
        xa, a1, b1, h1 = _ffn_fwd(xc, row(g_ffn1[l]), w_ffn[2 * l], 0, f"ffn1_fwd_{l}", dep=token)
        if l % 2 == 0:
            w_in_t, w_out, w_ffn[1] = gathered(gather_a, xa, "gather_a_wait")
            z, hm = _mm_nt(xa, w_in_t, row(g_mix[l]), F32, "mix_in_proj")
            cat, conv_out = _mix_fwd(z, conv_w_full, conv_b2, lag, lab, lvg, lvb, ws, bias2d, n_seq)
            xb = _mm_nn_res(cat, w_out, xa, "mix_out_proj")
            mixer = (z, hm, cat, conv_out)
        else:
            qkv, hm = _mm_nt(xa, w_qkv_t, row(g_mix[l]), BF16, "qkv_proj")
            o, tot = _attn_fwd(qkv, n_seq)
            xb = _mm_nn_res(o, w_o_full, xa, "attn_out_proj")
            mixer = (qkv, hm, o, tot)
        xn, a2, b2, h2 = _ffn_fwd(xb, row(g_ffn2[l]), w_ffn[2 * l + 1], 0, f"ffn2_fwd_{l}")
        saved.append((xc, a1, b1, h1, xa, mixer, xb, a2, b2, h2))
        xc = xn
        if l == 0:
            w_ffn[2], w_qkv_t, w_o_full, w_ffn[3] = gathered(gather_b, xc, "gather_b_wait")

    g, loss_part, dg_final = _loss_head(xc, row(g_final), target)

    dg_ffn1, dg_ffn2, dg_mix = [None] * depth, [None] * depth, [None] * depth
    exchanges = {}
    token = None

    def ffn_back(g, xin, gvec, a, b, h, k, tag, token):
        g, dg, da, db, s, gh = _ffn_bwd(g, xin, gvec, a, b, w_ffn[k], 0, f"ffn{tag}_bwd", dep=token)
        dws = [_mm_tn(da, h, f"dw_gate{tag}"), _mm_tn(db, h, f"dw_up{tag}"), _mm_tn(s, gh, f"dw_down{tag}")]
        exchanges[f"ffn{k}"], token = _split_start(False, dws, f"exchange_ffn{tag}_start")
        return g, dg, token

    for l in reversed(range(depth)):
        xin, a1, b1, h1, xa, mixer, xb, a2, b2, h2 = saved[l]
        g, dg_ffn2[l], token = ffn_back(g, xb, row(g_ffn2[l]), a2, b2, h2, 2 * l + 1, f"2_{l}", token)
        if l % 2 == 0:
            z, hm, cat, conv_out = mixer
            dcat = _mm_nt(g, w_out, None, F32, "mix_out_bwd", dep=token)
            d_w_out = _mm_tn(cat, g, "dw_out")
            (da1, dz_uv, d_lag, d_lab, d_lvg, d_lvb, d_ws, d_sb) = _mix_bwd_rows(
                dcat, z, conv_out, lag, lab, lvg, lvb, ws, ws_t, bias2d, n_seq)
            dz_a, d_cw, d_cb = _mix_bwd_conv(da1, z, conv_w_full, n_seq)
            dz = jnp.concatenate([dz_a, dz_uv], axis=1)
            d_w_in_t = _mm_tn(dz, hm, "dw_in")
            exchanges["mix"], token = _split_start(False, [d_w_out, d_w_in_t], "exchange_mix_start")
            g, dg_mix[l] = _mm_nn_rmsbwd(dz, w_in_t, xa, row(g_mix[l]), g, "mix_in_bwd", dep=token)
        else:
            qkv, hm, o, tot = mixer
            do = _mm_nt(g, w_o_full, None, BF16, "attn_out_bwd", dep=token)
            d_w_o = _mm_tn(o, g, "dw_o")
            dq, dk, dv = _attn_bwd(qkv, do, tot, n_seq)
            dqkv = jnp.concatenate([dq, dk, dv], axis=1).astype(BF16)
            d_w_qkv_t = _mm_tn(dqkv, hm, "dw_qkv")
            exchanges["attn"], token = _split_start(False, [d_w_o, d_w_qkv_t], "exchange_attn_start")
            g, dg_mix[l] = _mm_nn_rmsbwd(dqkv, w_qkv_t, xa, row(g_mix[l]), g, "qkv_bwd", dep=token)
        g, dg_ffn1[l], token = ffn_back(g, xin, row(g_ffn1[l]), a1, b1, h1, 2 * l, f"1_{l}", token)
    grad_x = g.reshape(n_seq, S, D)

    def reduced(key, tag):
        out = []
        fulls, lands = _split_wait(exchanges[key], token, f"exchange_{tag}_wait")
        for i, (land, full) in enumerate(zip(lands, fulls)):
            n = land.shape[1]
            own = lax.dynamic_slice(full, (my_block * n, 0), (n, full.shape[1]))
            parts = lax.dynamic_update_slice(land, own[None], (my_block, 0, 0))
            out.append(_sum_parts(parts, f"sum_{tag}_{i}"))
        return out

    g_ffn_t = [None] * (6 * depth)
    for k in range(2 * depth):
        g_ffn_t[3 * k:3 * k + 3] = reduced(f"ffn{k}", f"ffn{k}")
    g_out, g_in = reduced("mix", "mix")
    g_o, g_qkv = reduced("attn", "attn")
    g_in, g_out, g_qkv, g_o = g_in.T[None], g_out[None], g_qkv.T[None], g_o[None]

    small = [jnp.concatenate(dg_ffn1), jnp.concatenate(dg_mix), d_cw, d_cb, d_lag, d_lab, d_lvg, d_lvb,
             jnp.where(tril[None], d_ws, 0.0), d_sb.T, jnp.concatenate(dg_ffn2), dg_final, loss_part[:, :1]]
    small_shapes = [(depth, D), (depth, D), (CONV_WIDTH, CA), (1, CA), (1, CA), (1, CA), (1, GB, DB), (1, GB, DB),
                    (1, GB, CHUNK, CHUNK), (1, GB, CHUNK), (depth, D), (D,), ()]
    red = _unpack(_small_all_reduce(_pack(small)), small_shapes)
    (gr_g_ffn1, gr_g_mix, gr_cw_full, gr_cb, gr_lag, gr_lab, gr_lvg, gr_lvb, gr_sp_w, gr_sp_b,
     gr_g_ffn2, gr_g_final, loss) = red
    n_cw = conv_w.shape[2]
    gr_cw = lax.dynamic_slice(gr_cw_full, (0, my_block * n_cw), (CONV_WIDTH, n_cw))[None]

    def ffn_grad(second, which):
        return jnp.stack([g_ffn_t[_ffn_index(l, second) + which].T if which < 2
                          else g_ffn_t[_ffn_index(l, second) + which] for l in range(depth)])

    grads = {
        "g_ffn1": gr_g_ffn1, "w_ffn1_gate": ffn_grad(0, 0), "w_ffn1_up": ffn_grad(0, 1), "w_ffn1_down": ffn_grad(0, 2),
        "g_mix": gr_g_mix, "w_in_ab": g_in, "conv_w": gr_cw, "conv_b": gr_cb, "ln_a_g": gr_lag, "ln_a_b": gr_lab,
        "ln_v_g": gr_lvg, "ln_v_b": gr_lvb, "sp_w": gr_sp_w, "sp_b": gr_sp_b, "w_out_ab": g_out, "w_qkv": g_qkv,
        "w_o": g_o, "g_ffn2": gr_g_ffn2, "w_ffn2_gate": ffn_grad(1, 0), "w_ffn2_up": ffn_grad(1, 1),
        "w_ffn2_down": ffn_grad(1, 2), "g_final": gr_g_final,
    }
    weights = dict(g_ffn1=g_ffn1, w_ffn1_gate=w_ffn1_gate, w_ffn1_up=w_ffn1_up, w_ffn1_down=w_ffn1_down, g_mix=g_mix,
                   w_in_ab=w_in_ab, conv_w=conv_w, conv_b=conv_b, ln_a_g=ln_a_g, ln_a_b=ln_a_b, ln_v_g=ln_v_g,
                   ln_v_b=ln_v_b, sp_w=sp_w, sp_b=sp_b, w_out_ab=w_out_ab, w_qkv=w_qkv, w_o=w_o, g_ffn2=g_ffn2,
                   w_ffn2_gate=w_ffn2_gate, w_ffn2_up=w_ffn2_up, w_ffn2_down=w_ffn2_down, g_final=g_final)
    m_in = dict(g_ffn1=m_g_ffn1, w_ffn1_gate=m_w_ffn1_gate, w_ffn1_up=m_w_ffn1_up, w_ffn1_down=m_w_ffn1_down,
                g_mix=m_g_mix, w_in_ab=m_w_in_ab, conv_w=m_conv_w, conv_b=m_conv_b, ln_a_g=m_ln_a_g, ln_a_b=m_ln_a_b,
                ln_v_g=m_ln_v_g, ln_v_b=m_ln_v_b, sp_w=m_sp_w, sp_b=m_sp_b, w_out_ab=m_w_out_ab, w_qkv=m_w_qkv,
                w_o=m_w_o, g_ffn2=m_g_ffn2, w_ffn2_gate=m_w_ffn2_gate, w_ffn2_up=m_w_ffn2_up,
                w_ffn2_down=m_w_ffn2_down, g_final=m_g_final)
    v_in = dict(g_ffn1=v_g_ffn1, w_ffn1_gate=v_w_ffn1_gate, w_ffn1_up=v_w_ffn1_up, w_ffn1_down=v_w_ffn1_down,
                g_mix=v_g_mix, w_in_ab=v_w_in_ab, conv_w=v_conv_w, conv_b=v_conv_b, ln_a_g=v_ln_a_g, ln_a_b=v_ln_a_b,
                ln_v_g=v_ln_v_g, ln_v_b=v_ln_v_b, sp_w=v_sp_w, sp_b=v_sp_b, w_out_ab=v_w_out_ab, w_qkv=v_w_qkv,
                w_o=v_w_o, g_ffn2=v_g_ffn2, w_ffn2_gate=v_w_ffn2_gate, w_ffn2_up=v_w_ffn2_up,
                w_ffn2_down=v_w_ffn2_down, g_final=v_g_final)
    names = list(weights)
    grads = {n: grads[n].reshape(weights[n].shape) for n in names}

    delta, new_m, new_v = {}, {}, {}
    big = [n for n in names if n.startswith("w_")]
    for n in big:
        shp = weights[n].shape
        two = lambda a: a.reshape(-1, shp[-1])
        d, nm, nv = _adamw(two(weights[n]), two(grads[n]), two(m_in[n]), two(v_in[n]), f"adamw_{n}")
        delta[n], new_m[n], new_v[n] = d.reshape(shp), nm.reshape(shp), nv.reshape(shp)
    little = [n for n in names if n not in big]
    shapes = [weights[n].shape for n in little]
    d, nm, nv = _adamw(_pack([weights[n] for n in little]), _pack([grads[n] for n in little]),
                       _pack([m_in[n] for n in little]), _pack([v_in[n] for n in little]), "adamw_small")
    for n, dd, mm, vv in zip(little, _unpack(d, shapes), _unpack(nm, shapes), _unpack(nv, shapes)):
        delta[n], new_m[n], new_v[n] = dd, mm, vv

    return (loss, grad_x, *[grads[n] for n in names], *[delta[n] for n in names],
            *[new_m[n] for n in names], *[new_v[n] for n in names])
```

```python
import functools

import jax
import jax.numpy as jnp
from jax import lax
from jax.experimental import pallas as pl
from jax.experimental.pallas import tpu as pltpu

F32 = jnp.float32
BF16 = jnp.bfloat16

D_MODEL = 1024
CA = 512
CB = 512
GB = 4
DB = 128
CHUNK = 128
CONV_WIDTH = 31
N_HEADS = 16
HEAD_DIM = 64
EPS = 1e-6
N_DEV = 8
LANES = 128
QB = 128
ATT_TQ = 512
FFN_TN = 1408
HALO = 32
CONV_ROWS = 32
ATT_SCALE = HEAD_DIM ** -0.5

ADAM_LR = 0.001
ADAM_B1 = 0.9
ADAM_B2 = 0.999
ADAM_EPS = 1e-08
ADAM_WD = 0.01
ADAM_STEP = 10

NT = (((1,), (1,)), ((), ()))
NN = (((1,), (0,)), ((), ()))
TN = (((0,), (0,)), ((), ()))
MESH = pl.DeviceIdType.MESH
ANY = pl.BlockSpec(memory_space=pl.ANY)
VMEM_LIMIT = 60 * 1024 * 1024


def _dot(a, b, dims):
    return lax.dot_general(a, b, dims, preferred_element_type=F32)


def _cp(*sem):
    return pltpu.CompilerParams(dimension_semantics=sem, vmem_limit_bytes=VMEM_LIMIT)


def _pcall(body, *, in_specs, args, dep=None, **kw):
    if dep is not None:
        n_in = len(in_specs)
        inner = body

        def body(*refs):
            inner(*refs[:n_in], *refs[n_in + 1:])

        in_specs = list(in_specs) + [ANY]
        args = tuple(args) + (dep,)
    return pl.pallas_call(body, in_specs=list(in_specs), **kw)(*args)


def _tile(n, want):
    if n <= want:
        return n
    t = want - want % LANES
    while t > LANES and n % t:
        t -= LANES
    assert n % t == 0, (n, want)
    return t


def _sigmoid(x):
    return 1.0 / (1.0 + jnp.exp(-x))


def _rstd(x):
    return lax.rsqrt(jnp.mean(x * x, axis=-1, keepdims=True) + EPS)


def _rms_bwd(x, g, dh):
    r = _rstd(x)
    u = dh * g
    dx = r * (u - x * (r * r) * jnp.mean(u * x, axis=-1, keepdims=True))
    dg = jnp.sum(dh * x * r, axis=0, keepdims=True)
    return dx, dg


def _ln_fwd(x, g, b):
    mu = jnp.mean(x, axis=-1, keepdims=True)
    xc = x - mu
    r = lax.rsqrt(jnp.mean(xc * xc, axis=-1, keepdims=True) + EPS)
    xh = xc * r
    return xh * g + b, xh, r


def _ln_bwd(dy, xh, r, g):
    dxh = dy * g
    return r * (dxh - jnp.mean(dxh, axis=-1, keepdims=True)
                - xh * jnp.mean(dxh * xh, axis=-1, keepdims=True))


def _ffn_fwd(x, g, wall, base, name, dep=None):
    T, D = x.shape
    F = wall.shape[1]
    tm, tn = _tile(T, 512), _tile(F, FFN_TN)
    n_j = F // tn

    def body(x_ref, g_ref, wg_ref, wu_ref, wd_ref, xo_ref, a_ref, b_ref, h_ref, acc_ref):
        j = pl.program_id(1)

        @pl.when(j == 0)
        def _():
            xv = x_ref[...]
            h_ref[...] = (xv * _rstd(xv) * g_ref[...]).astype(BF16)
            acc_ref[...] = jnp.zeros_like(acc_ref)

        h = h_ref[...]
        a = _dot(h, wg_ref[...], NT)
        b = _dot(h, wu_ref[...], NT)
        a_ref[...] = a.astype(BF16)
        b_ref[...] = b.astype(BF16)
        s = (a * _sigmoid(a) * b).astype(BF16)
        acc_ref[...] += _dot(s, wd_ref[...], NN)

        @pl.when(j == n_j - 1)
        def _():
            xo_ref[...] = x_ref[...] + 0.5 * acc_ref[...]

    wspec = lambda k: pl.BlockSpec((None, tn, D), lambda i, j: (base + k, j, 0))
    return _pcall(
        body, name=name, grid=(T // tm, n_j), dep=dep, args=(x, g, wall, wall, wall),
        in_specs=[pl.BlockSpec((tm, D), lambda i, j: (i, 0)), pl.BlockSpec((1, D), lambda i, j: (0, 0)),
                  wspec(0), wspec(1), wspec(2)],
        out_specs=[pl.BlockSpec((tm, D), lambda i, j: (i, 0)), pl.BlockSpec((tm, tn), lambda i, j: (i, j)),
                   pl.BlockSpec((tm, tn), lambda i, j: (i, j)), pl.BlockSpec((tm, D), lambda i, j: (i, 0))],
        out_shape=[jax.ShapeDtypeStruct((T, D), F32), jax.ShapeDtypeStruct((T, F), BF16),
                   jax.ShapeDtypeStruct((T, F), BF16), jax.ShapeDtypeStruct((T, D), BF16)],
        scratch_shapes=[pltpu.VMEM((tm, D), F32)],
        compiler_params=_cp("parallel", "arbitrary"),
    )


def _ffn_bwd(go, x, g, a, b, wall, base, name, dep=None):
    T, D = x.shape
    F = wall.shape[1]
    tm, tn = _tile(T, 512), _tile(F, FFN_TN)
    n_j = F // tn

    def body(go_ref, x_ref, g_ref, a_ref, b_ref, wg_ref, wu_ref, wd_ref,
             gx_ref, dg_ref, da_ref, db_ref, s_ref, gh_ref, acc_ref):
        i, j = pl.program_id(0), pl.program_id(1)

        @pl.when(j == 0)
        def _():
            gh_ref[...] = (0.5 * go_ref[...]).astype(BF16)
            acc_ref[...] = jnp.zeros_like(acc_ref)

        @pl.when((i == 0) & (j == 0))
        def _():
            dg_ref[...] = jnp.zeros_like(dg_ref)

        ds = _dot(gh_ref[...], wd_ref[...], NT)
        av = a_ref[...].astype(F32)
        bv = b_ref[...].astype(F32)
        sig = _sigmoid(av)
        sl = av * sig
        dab = (ds * bv * (sig * (1.0 + av * (1.0 - sig)))).astype(BF16)
        dbb = (ds * sl).astype(BF16)
        s_ref[...] = (sl * bv).astype(BF16)
        da_ref[...] = dab
        db_ref[...] = dbb
        acc_ref[...] += _dot(dab, wg_ref[...], NN) + _dot(dbb, wu_ref[...], NN)

        @pl.when(j == n_j - 1)
        def _():
            dx, dg = _rms_bwd(x_ref[...], g_ref[...], acc_ref[...])
            gx_ref[...] = go_ref[...] + dx
            dg_ref[...] += dg

    wspec = lambda k: pl.BlockSpec((None, tn, D), lambda i, j: (base + k, j, 0))
    row = pl.BlockSpec((tm, D), lambda i, j: (i, 0))
    hid = pl.BlockSpec((tm, tn), lambda i, j: (i, j))
    vec = pl.BlockSpec((1, D), lambda i, j: (0, 0))
    return _pcall(
        body, name=name, grid=(T // tm, n_j), dep=dep, args=(go, x, g, a, b, wall, wall, wall),
        in_specs=[row, row, vec, hid, hid, wspec(0), wspec(1), wspec(2)],
        out_specs=[row, vec, hid, hid, hid, row],
        out_shape=[jax.ShapeDtypeStruct((T, D), F32), jax.ShapeDtypeStruct((1, D), F32),
                   jax.ShapeDtypeStruct((T, F), BF16), jax.ShapeDtypeStruct((T, F), BF16),
                   jax.ShapeDtypeStruct((T, F), BF16), jax.ShapeDtypeStruct((T, D), BF16)],
        scratch_shapes=[pltpu.VMEM((tm, D), F32)],
        compiler_params=_cp("arbitrary", "arbitrary"),
    )


def _mm_tn(a, b, name, dep=None):
    T, M = a.shape
    N = b.shape[1]
    tmm, tk = _tile(M, 1536), _tile(T, 512)

    def body(a_ref, b_ref, o_ref):
        @pl.when(pl.program_id(1) == 0)
        def _():
            o_ref[...] = jnp.zeros_like(o_ref)

        o_ref[...] += _dot(a_ref[...].astype(BF16), b_ref[...].astype(BF16), TN)

    return _pcall(
        body, name=name, grid=(M // tmm, T // tk), dep=dep, args=(a, b),
        in_specs=[pl.BlockSpec((tk, tmm), lambda m, k: (k, m)), pl.BlockSpec((tk, N), lambda m, k: (k, 0))],
        out_specs=pl.BlockSpec((tmm, N), lambda m, k: (m, 0)),
        out_shape=jax.ShapeDtypeStruct((M, N), F32),
        compiler_params=_cp("parallel", "arbitrary"),
    )


def _mm_nt(x, wt, g, out_dtype, name, dep=None):
    T, K = x.shape
    N = wt.shape[0]
    tm, tn = _tile(T, 512), _tile(N, 1536)
    norm = g is not None

    def body(*refs):
        if norm:
            x_ref, g_ref, w_ref, o_ref, h_ref = refs
        else:
            x_ref, w_ref, o_ref, h_ref = refs

        @pl.when(pl.program_id(1) == 0)
        def _():
            xv = x_ref[...].astype(F32)
            if norm:
                xv = xv * _rstd(xv) * g_ref[...]
            h_ref[...] = xv.astype(BF16)

        o_ref[...] = _dot(h_ref[...], w_ref[...], NT).astype(out_dtype)

    row = pl.BlockSpec((tm, K), lambda i, j: (i, 0))
    wsp = pl.BlockSpec((tn, K), lambda i, j: (j, 0))
    osp = pl.BlockSpec((tm, tn), lambda i, j: (i, j))
    if norm:
        return pl.pallas_call(
            body, name=name, grid=(T // tm, N // tn),
            in_specs=[row, pl.BlockSpec((1, K), lambda i, j: (0, 0)), wsp],
            out_specs=[osp, row],
            out_shape=[jax.ShapeDtypeStruct((T, N), out_dtype), jax.ShapeDtypeStruct((T, K), BF16)],
            compiler_params=_cp("parallel", "arbitrary"),
        )(x, g, wt)
    return _pcall(
        body, name=name, grid=(T // tm, N // tn), dep=dep, args=(x, wt),
        in_specs=[row, wsp], out_specs=osp,
        out_shape=jax.ShapeDtypeStruct((T, N), out_dtype),
        scratch_shapes=[pltpu.VMEM((tm, K), BF16)],
        compiler_params=_cp("parallel", "arbitrary"),
    )


def _mm_nn_res(act, w, resid, name):
    T, K = act.shape
    D = w.shape[1]
    tm = _tile(T, 512)

    def body(a_ref, w_ref, r_ref, o_ref):
        o_ref[...] = r_ref[...] + _dot(a_ref[...].astype(BF16), w_ref[...], NN)

    return pl.pallas_call(
        body, name=name, grid=(T // tm,),
        in_specs=[pl.BlockSpec((tm, K), lambda i: (i, 0)), pl.BlockSpec((K, D), lambda i: (0, 0)),
                  pl.BlockSpec((tm, D), lambda i: (i, 0))],
        out_specs=pl.BlockSpec((tm, D), lambda i: (i, 0)),
        out_shape=jax.ShapeDtypeStruct((T, D), F32),
        compiler_params=_cp("parallel"),
    )(act, w, resid)


def _mm_nn_rmsbwd(act, w, x, g, gprev, name, dep=None):
    T, K = act.shape
    D = w.shape[1]
    tm = _tile(T, 512)

    def body(a_ref, w_ref, x_ref, g_ref, gp_ref, o_ref, dg_ref):
        @pl.when(pl.program_id(0) == 0)
        def _():
            dg_ref[...] = jnp.zeros_like(dg_ref)

        dh = _dot(a_ref[...].astype(BF16), w_ref[...], NN)
        dx, dg = _rms_bwd(x_ref[...], g_ref[...], dh)
        o_ref[...] = gp_ref[...] + dx
        dg_ref[...] += dg

    row = pl.BlockSpec((tm, D), lambda i: (i, 0))
    vec = pl.BlockSpec((1, D), lambda i: (0, 0))
    return _pcall(
        body, name=name, grid=(T // tm,), dep=dep, args=(act, w, x, g, gprev),
        in_specs=[pl.BlockSpec((tm, K), lambda i: (i, 0)), pl.BlockSpec((K, D), lambda i: (0, 0)), row, vec, row],
        out_specs=[row, vec],
        out_shape=[jax.ShapeDtypeStruct((T, D), F32), jax.ShapeDtypeStruct((1, D), F32)],
        compiler_params=_cp("arbitrary"),
    )


def _loss_head(x, g, target):
    T, D = x.shape
    tm = _tile(T, 512)

    def body(x_ref, g_ref, t_ref, dx_ref, loss_ref, dg_ref):
        @pl.when(pl.program_id(0) == 0)
        def _():
            loss_ref[...] = jnp.zeros_like(loss_ref)
            dg_ref[...] = jnp.zeros_like(dg_ref)

        xv = x_ref[...]
        gv = g_ref[...]
        e = xv * _rstd(xv) * gv - t_ref[...]
        per_tok = jnp.sum(e * e, axis=-1, keepdims=True) * (1.0 / D)
        loss_ref[...] += 0.5 * jnp.sum(per_tok, axis=0, keepdims=True)
        dx, dg = _rms_bwd(xv, gv, e * (1.0 / D))
        dx_ref[...] = dx
        dg_ref[...] += dg

    row = pl.BlockSpec((tm, D), lambda i: (i, 0))
    vec = pl.BlockSpec((1, D), lambda i: (0, 0))
    return pl.pallas_call(
        body, name="loss_head", grid=(T // tm,),
        in_specs=[row, vec, row],
        out_specs=[row, pl.BlockSpec((1, LANES), lambda i: (0, 0)), vec],
        out_shape=[jax.ShapeDtypeStruct((T, D), F32), jax.ShapeDtypeStruct((1, LANES), F32),
                   jax.ShapeDtypeStruct((1, D), F32)],
        compiler_params=_cp("arbitrary"),
    )(x, g, target)


def _log_gates(z):
    ls = jnp.minimum(z, 0.0) - jnp.log(1.0 + jnp.exp(-jnp.abs(z)))
    return ls, ls - z


def _cumsum_mm(v, u2):
    hi = v.astype(BF16)
    lo = (v - hi.astype(F32)).astype(BF16)
    return _dot(jnp.concatenate([hi, lo], axis=1), u2, NN)


def _half_rowsum(v):
    n = v.shape[0]
    s0 = jnp.sum(v[:, :QB], axis=1, keepdims=True)
    s1 = jnp.sum(v[:, QB:], axis=1, keepdims=True)
    return jnp.concatenate([jnp.broadcast_to(s0, (n, QB)), jnp.broadcast_to(s1, (n, QB))], axis=1)


def _stack_heads(src_ref, dst_ref, n_blk):
    m0 = lax.broadcasted_iota(jnp.int32, (1, LANES), 1) < HEAD_DIM

    def fill(c, carry):
        blk = src_ref[pl.ds(pl.multiple_of(c * QB, QB), QB), :]
        zero = jnp.zeros_like(blk)
        dst_ref[c, 0:QB, :] = jnp.where(m0, blk, zero)
        dst_ref[c, QB:2 * QB, :] = jnp.where(m0, zero, blk)
        return carry

    lax.fori_loop(0, n_blk, fill, 0)


def _diag_mask(tq, j):
    n = tq - j * QB
    row = lax.broadcasted_iota(jnp.int32, (n, 2 * QB), 0)
    col = lax.broadcasted_iota(jnp.int32, (n, 2 * QB), 1)
    return (col & (QB - 1)) < row


def _tri_blockdiag(upper):
    r = lax.broadcasted_iota(jnp.int32, (2 * QB, 2 * QB), 0)
    c = lax.broadcasted_iota(jnp.int32, (2 * QB, 2 * QB), 1)
    same = (r // QB) == (c // QB)
    u = (same & ((r > c) if upper else (r < c))).astype(BF16)
    return jnp.concatenate([u, u], axis=0)


def _attn_tiles(T, n_seq):
    S = T // n_seq
    tq = ATT_TQ if S % ATT_TQ == 0 else QB
    return S, tq, tq // QB, S // tq, S // QB


def _attn_fwd(qkv, n_seq):
    T = qkv.shape[0]
    S, tq, r, n_q, n_k = _attn_tiles(T, n_seq)
    n_p = D_MODEL // LANES
    u_suffix = _tri_blockdiag(True)

    def body(q_ref, k_ref, v_ref, u_ref, o_ref, tot_ref, kk_ref, vv_ref, lr_s, acc_s):
        qi = pl.program_id(2)

        @pl.when(qi == 0)
        def _():
            _stack_heads(k_ref, kk_ref, n_k)
            _stack_heads(v_ref, vv_ref, n_k)

        u = u_ref[...]
        lr_s[...] = jnp.zeros_like(lr_s)
        acc_s[...] = jnp.zeros_like(acc_s)

        def step(kj, q, mask, lr, acc):
            ls, lk = _log_gates(_dot(q, kk_ref[kj], NT))
            if mask is not None:
                lk = jnp.where(mask, lk, 0.0)
            a = jnp.exp(ls + _cumsum_mm(lk, u) + lr)
            if mask is not None:
                a = jnp.where(mask, a, 0.0)
            return lr + _half_rowsum(lk), acc + _dot(a.astype(BF16), vv_ref[kj], NN)

        for j in reversed(range(r)):
            rows = slice(j * QB, tq)
            lr, acc = step(qi * r + j, q_ref[rows, :] * ATT_SCALE, _diag_mask(tq, j), lr_s[rows, :], acc_s[rows, :])
            lr_s[rows, :] = lr
            acc_s[rows, :] = acc

        q = q_ref[...] * ATT_SCALE

        def off(it, carry):
            lr, acc = lr_s[...], acc_s[...]
            for j in range(r):
                lr, acc = step((qi - it) * r - 1 - j, q, None, lr, acc)
            lr_s[...] = lr
            acc_s[...] = acc
            return carry

        lax.fori_loop(0, qi, off, 0)
        o_ref[...] = acc_s[...].astype(BF16)
        tot_ref[...] = lr_s[...]

    return pl.pallas_call(
        body, name="attn_fwd", grid=(n_seq, n_p, n_q),
        in_specs=[pl.BlockSpec((tq, LANES), lambda b, p, qi: (b * n_q + qi, p)),
                  pl.BlockSpec((S, LANES), lambda b, p, qi: (b, n_p + p)),
                  pl.BlockSpec((S, LANES), lambda b, p, qi: (b, 2 * n_p + p)),
                  pl.BlockSpec((4 * QB, 2 * QB), lambda b, p, qi: (0, 0))],
        out_specs=[pl.BlockSpec((tq, LANES), lambda b, p, qi: (b * n_q + qi, p)),
                   pl.BlockSpec((tq, 2 * QB), lambda b, p, qi: (b * n_q + qi, p))],
        out_shape=[jax.ShapeDtypeStruct((T, D_MODEL), BF16), jax.ShapeDtypeStruct((T, 2 * D_MODEL), F32)],
        scratch_shapes=[pltpu.VMEM((n_k, 2 * QB, LANES), BF16), pltpu.VMEM((n_k, 2 * QB, LANES), BF16),
                        pltpu.VMEM((tq, 2 * QB), F32), pltpu.VMEM((tq, LANES), F32)],
        compiler_params=_cp("parallel", "parallel", "arbitrary"),
    )(qkv, qkv, qkv, u_suffix)


def _attn_bwd(qkv, do, tot, n_seq):
    T = qkv.shape[0]
    S, tq, r, n_q, n_k = _attn_tiles(T, n_seq)
    n_p = D_MODEL // LANES
    u_prefix = _tri_blockdiag(False)

    def body(q_ref, k_ref, v_ref, do_ref, tot_ref, u_ref, dq_ref, dk_ref, dv_ref,
             kk_ref, vv_ref, cl_s, cg_s, dq_s):
        qi = pl.program_id(2)

        @pl.when(qi == 0)
        def _():
            _stack_heads(k_ref, kk_ref, n_k)
            _stack_heads(v_ref, vv_ref, n_k)
            dk_ref[...] = jnp.zeros_like(dk_ref)
            dv_ref[...] = jnp.zeros_like(dv_ref)

        u = u_ref[...]
        m0 = lax.broadcasted_iota(jnp.int32, (1, LANES), 1) < HEAD_DIM
        cl_s[...] = jnp.zeros_like(cl_s)
        cg_s[...] = jnp.zeros_like(cg_s)
        dq_s[...] = jnp.zeros_like(dq_s)

        def step(kj, q, dov, tot2, mask, cl, cg, dq):
            kk = kk_ref[kj]
            ls, lk = _log_gates(_dot(q, kk, NT))
            if mask is not None:
                lk = jnp.where(mask, lk, 0.0)
            a = jnp.exp(ls + (tot2 - cl - (_cumsum_mm(lk, u) + lk)))
            if mask is not None:
                a = jnp.where(mask, a, 0.0)
            g = a * _dot(dov, vv_ref[kj], NT)
            dz = g - (g + _cumsum_mm(g, u) + cg) * jnp.exp(ls)
            if mask is not None:
                dz = jnp.where(mask, dz, 0.0)
            dz = dz.astype(BF16)
            rows = pl.ds(pl.multiple_of(kj * QB, QB), QB)
            dvt = _dot(a.astype(BF16), dov, TN)
            dv_ref[rows, :] += jnp.where(m0, dvt[:QB], dvt[QB:])
            dkt = _dot(dz, q, TN)
            dk_ref[rows, :] += jnp.where(m0, dkt[:QB], dkt[QB:])
            return cl + _half_rowsum(lk), cg + _half_rowsum(g), dq + _dot(dz, kk, NN)

        q = q_ref[...] * ATT_SCALE
        dov = do_ref[...]
        tot2 = tot_ref[...]

        def off(it, carry):
            cl, cg, dq = cl_s[...], cg_s[...], dq_s[...]
            for j in range(r):
                cl, cg, dq = step(it * r + j, q, dov, tot2, None, cl, cg, dq)
            cl_s[...] = cl
            cg_s[...] = cg
            dq_s[...] = dq
            return carry

        lax.fori_loop(0, qi, off, 0)

        for j in range(r):
            rows = slice(j * QB, tq)
            cl, cg, dq = step(qi * r + j, q_ref[rows, :] * ATT_SCALE, do_ref[rows, :], tot_ref[rows, :],
                              _diag_mask(tq, j), cl_s[rows, :], cg_s[rows, :], dq_s[rows, :])
            cl_s[rows, :] = cl
            cg_s[rows, :] = cg
            dq_s[rows, :] = dq
        dq_ref[...] = dq_s[...] * ATT_SCALE

    qspec = pl.BlockSpec((tq, LANES), lambda b, p, qi: (b * n_q + qi, p))
    seq = lambda off: pl.BlockSpec((S, LANES), lambda b, p, qi: (b, off + p))
    return pl.pallas_call(
        body, name="attn_bwd", grid=(n_seq, n_p, n_q),
        in_specs=[qspec, seq(n_p), seq(2 * n_p), qspec,
                  pl.BlockSpec((tq, 2 * QB), lambda b, p, qi: (b * n_q + qi, p)),
                  pl.BlockSpec((4 * QB, 2 * QB), lambda b, p, qi: (0, 0))],
        out_specs=[qspec, seq(0), seq(0)],
        out_shape=[jax.ShapeDtypeStruct((T, D_MODEL), F32)] * 3,
        scratch_shapes=[pltpu.VMEM((n_k, 2 * QB, LANES), BF16), pltpu.VMEM((n_k, 2 * QB, LANES), BF16),
                        pltpu.VMEM((tq, 2 * QB), F32), pltpu.VMEM((tq, 2 * QB), F32), pltpu.VMEM((tq, LANES), F32)],
        compiler_params=_cp("parallel", "parallel", "arbitrary"),
    )(qkv, qkv, qkv, do, tot, u_prefix)


def _glu_with_halo(av_ref, ag_ref, avh_ref, agh_ref, a0_s, first, ts):
    hal = avh_ref[...] * _sigmoid(agh_ref[...])
    a0_s[0:HALO, :] = jnp.where(first, 0.0, hal)
    a0_s[HALO:HALO + ts, :] = av_ref[...] * _sigmoid(ag_ref[...])


def _mix_specs(ts, n_r, with_left):
    blk = lambda c: pl.BlockSpec((ts, CA), lambda b, r: (b * n_r + r, c))
    per = ts // HALO
    left = lambda c: pl.BlockSpec((HALO, CA), lambda b, r: (jnp.maximum((b * n_r + r) * per - 1, 0), c))
    return blk, (left if with_left else None)


def _mix_fwd(z, conv_w, conv_b, ln_a_g, ln_a_b, ln_v_g, ln_v_b, ws, bias2d, n_seq):
    T = z.shape[0]
    S = T // n_seq
    ts = _tile(S, 512)
    n_r = S // ts
    shift = HALO - (CONV_WIDTH - 1)

    def body(av_ref, ag_ref, avh_ref, agh_ref, u_ref, v_ref, cw_ref, cb_ref, lag_ref, lab_ref,
             lvg_ref, lvb_ref, ws_ref, bias_ref, cat_ref, a1_ref, a0_s):
        _glu_with_halo(av_ref, ag_ref, avh_ref, agh_ref, a0_s, pl.program_id(1) == 0, ts)
        for rb in range(ts // CONV_ROWS):
            base = rb * CONV_ROWS
            acc = jnp.broadcast_to(cb_ref[...], (CONV_ROWS, CA))
            for k in range(CONV_WIDTH):
                acc = acc + cw_ref[k:k + 1, :] * a0_s[base + shift + k:base + shift + k + CONV_ROWS, :]
            a1_ref[base:base + CONV_ROWS, :] = acc
        y, _, _ = _ln_fwd(a1_ref[...], lag_ref[...], lab_ref[...])
        cat_ref[:, 0:CA] = (y * _sigmoid(y)).astype(BF16)
        for gi in range(GB):
            sl = slice(gi * DB, (gi + 1) * DB)
            v1, _, _ = _ln_fwd(v_ref[:, sl], lvg_ref[:, sl], lvb_ref[:, sl])
            v1 = v1.astype(BF16)
            for c in range(ts // CHUNK):
                rs = slice(c * CHUNK, (c + 1) * CHUNK)
                v2 = _dot(ws_ref[gi], v1[rs], NN) + bias_ref[:, sl]
                cat_ref[rs, CA + gi * DB:CA + (gi + 1) * DB] = (u_ref[rs, sl] * v2).astype(BF16)

    blk, left = _mix_specs(ts, n_r, True)
    vec = pl.BlockSpec((1, CA), lambda b, r: (0, 0))
    return pl.pallas_call(
        body, name="mix_fwd", grid=(n_seq, n_r),
        in_specs=[blk(0), blk(1), left(0), left(1), blk(2), blk(3),
                  pl.BlockSpec((CONV_WIDTH, CA), lambda b, r: (0, 0)), vec, vec, vec, vec, vec,
                  pl.BlockSpec((GB, CHUNK, CHUNK), lambda b, r: (0, 0, 0)),
                  pl.BlockSpec((CHUNK, CB), lambda b, r: (0, 0))],
        out_specs=[pl.BlockSpec((ts, CA + CB), lambda b, r: (b * n_r + r, 0)), blk(0)],
        out_shape=[jax.ShapeDtypeStruct((T, CA + CB), BF16), jax.ShapeDtypeStruct((T, CA), F32)],
        scratch_shapes=[pltpu.VMEM((HALO + ts, CA), F32)],
        compiler_params=_cp("parallel", "parallel"),
    )(z, z, z, z, z, z, conv_w, conv_b, ln_a_g, ln_a_b, ln_v_g, ln_v_b, ws, bias2d)


def _mix_bwd_rows(dcat, z, a1, ln_a_g, ln_a_b, ln_v_g, ln_v_b, ws, ws_t, bias2d, n_seq):
    T = z.shape[0]
    S = T // n_seq
    ts = _tile(S, 512)
    n_r = S // ts

    def body(dc_ref, u_ref, v_ref, a1_ref, lag_ref, lab_ref, lvg_ref, lvb_ref, ws_ref, wst_ref, bias_ref,
             da1_ref, dz_ref, dlag_ref, dlab_ref, dlvg_ref, dlvb_ref, dws_ref, dsb_ref, dv1_s, dbias_s):
        first = (pl.program_id(0) == 0) & (pl.program_id(1) == 0)
        last = (pl.program_id(0) == n_seq - 1) & (pl.program_id(1) == n_r - 1)

        @pl.when(first)
        def _():
            for ref in (dlag_ref, dlab_ref, dlvg_ref, dlvb_ref, dws_ref, dbias_s):
                ref[...] = jnp.zeros_like(ref)

        lag = lag_ref[...]
        y, xh, r = _ln_fwd(a1_ref[...], lag, lab_ref[...])
        sig = _sigmoid(y)
        dy = dc_ref[:, 0:CA] * (sig * (1.0 + y * (1.0 - sig)))
        dlag_ref[...] += jnp.sum(dy * xh, axis=0, keepdims=True)
        dlab_ref[...] += jnp.sum(dy, axis=0, keepdims=True)
        da1_ref[...] = _ln_bwd(dy, xh, r, lag)

        tril = (lax.broadcasted_iota(jnp.int32, (CHUNK, CHUNK), 0)
                >= lax.broadcasted_iota(jnp.int32, (CHUNK, CHUNK), 1))
        for gi in range(GB):
            sl = slice(gi * DB, (gi + 1) * DB)
            lvg = lvg_ref[:, sl]
            v1, vh, vr = _ln_fwd(v_ref[:, sl], lvg, lvb_ref[:, sl])
            v1 = v1.astype(BF16)
            for c in range(ts // CHUNK):
                rs = slice(c * CHUNK, (c + 1) * CHUNK)
                v2 = _dot(ws_ref[gi], v1[rs], NN) + bias_ref[:, sl]
                dbo = dc_ref[rs, CA + gi * DB:CA + (gi + 1) * DB]
                dz_ref[rs, sl] = (dbo * v2).astype(BF16)
                dv2 = dbo * u_ref[rs, sl]
                dbias_s[:, sl] += dv2
                dv2b = dv2.astype(BF16)
                dws_ref[gi] += jnp.where(tril, _dot(dv2b, v1[rs], NT), 0.0)
                dv1_s[rs, :] = _dot(wst_ref[gi], dv2b, NN)
            dv1 = dv1_s[...]
            dlvg_ref[:, sl] += jnp.sum(dv1 * vh, axis=0, keepdims=True)
            dlvb_ref[:, sl] += jnp.sum(dv1, axis=0, keepdims=True)
            dz_ref[:, CB + gi * DB:CB + (gi + 1) * DB] = _ln_bwd(dv1, vh, vr, lvg).astype(BF16)

        @pl.when(last)
        def _():
            col = lax.broadcasted_iota(jnp.int32, (CHUNK, GB), 1)
            out = jnp.zeros((CHUNK, GB), F32)
            for gi in range(GB):
                s = jnp.sum(dbias_s[:, gi * DB:(gi + 1) * DB], axis=1, keepdims=True)
                out = out + jnp.where(col == gi, s, 0.0)
            dsb_ref[...] = out

    blk, _ = _mix_specs(ts, n_r, False)
    vec = pl.BlockSpec((1, CA), lambda b, r: (0, 0))
    mat = pl.BlockSpec((GB, CHUNK, CHUNK), lambda b, r: (0, 0, 0))
    wide = pl.BlockSpec((ts, CA + CB), lambda b, r: (b * n_r + r, 0))
    return pl.pallas_call(
        body, name="mix_bwd_rows", grid=(n_seq, n_r),
        in_specs=[wide, blk(2), blk(3), blk(0), vec, vec, vec, vec, mat, mat,
                  pl.BlockSpec((CHUNK, CB), lambda b, r: (0, 0))],
        out_specs=[blk(0), wide, vec, vec, vec, vec, mat, pl.BlockSpec((CHUNK, GB), lambda b, r: (0, 0))],
        out_shape=[jax.ShapeDtypeStruct((T, CA), F32), jax.ShapeDtypeStruct((T, 2 * CB), BF16)]
        + [jax.ShapeDtypeStruct((1, CA), F32)] * 4
        + [jax.ShapeDtypeStruct((GB, CHUNK, CHUNK), F32), jax.ShapeDtypeStruct((CHUNK, GB), F32)],
        scratch_shapes=[pltpu.VMEM((ts, DB), F32), pltpu.VMEM((CHUNK, CB), F32)],
        compiler_params=_cp("arbitrary", "arbitrary"),
    )(dcat, z, z, a1, ln_a_g, ln_a_b, ln_v_g, ln_v_b, ws, ws_t, bias2d)


def _mix_bwd_conv(da1, z, conv_w, n_seq):
    T = z.shape[0]
    S = T // n_seq
    ts = _tile(S, 512)
    n_r = S // ts
    per = ts // HALO
    shift = HALO - (CONV_WIDTH - 1)
    fold = CONV_ROWS // 8

    def body(d_ref, dh_ref, av_ref, ag_ref, avh_ref, agh_ref, cw_ref,
             dz_ref, dcw_ref, dcb_ref, a0_s, d1_s, da0_s, dw8_s):
        first = (pl.program_id(0) == 0) & (pl.program_id(1) == 0)
        last = (pl.program_id(0) == n_seq - 1) & (pl.program_id(1) == n_r - 1)

        @pl.when(first)
        def _():
            dw8_s[...] = jnp.zeros_like(dw8_s)
            dcb_ref[...] = jnp.zeros_like(dcb_ref)

        _glu_with_halo(av_ref, ag_ref, avh_ref, agh_ref, a0_s, pl.program_id(1) == 0, ts)
        d1_s[0:ts, :] = d_ref[...]
        d1_s[ts:ts + HALO, :] = jnp.where(pl.program_id(1) == n_r - 1, 0.0, dh_ref[...])
        dcb_ref[...] += jnp.sum(d_ref[...], axis=0, keepdims=True)
        for rb in range(ts // CONV_ROWS):
            base = rb * CONV_ROWS
            dcur = d1_s[base:base + CONV_ROWS, :]
            acc = jnp.zeros((CONV_ROWS, CA), F32)
            for k in range(CONV_WIDTH):
                back = CONV_WIDTH - 1 - k
                acc = acc + cw_ref[k:k + 1, :] * d1_s[base + back:base + back + CONV_ROWS, :]
                prod = dcur * a0_s[base + shift + k:base + shift + k + CONV_ROWS, :]
                part = prod[0:8]
                for f in range(1, fold):
                    part = part + prod[8 * f:8 * f + 8]
                dw8_s[k] += part
            da0_s[base:base + CONV_ROWS, :] = acc
        da0 = da0_s[...]
        sig = _sigmoid(ag_ref[...])
        dz_ref[:, 0:CA] = (da0 * sig).astype(BF16)
        dz_ref[:, CA:2 * CA] = (da0 * av_ref[...] * sig * (1.0 - sig)).astype(BF16)

        @pl.when(last)
        def _():
            for k in range(CONV_WIDTH):
                dcw_ref[k:k + 1, :] = jnp.sum(dw8_s[k], axis=0, keepdims=True)

    blk, left = _mix_specs(ts, n_r, True)
    n_halo_blocks = T // HALO
    right = pl.BlockSpec((HALO, CA), lambda b, r: (jnp.minimum((b * n_r + r + 1) * per, n_halo_blocks - 1), 0))
    return pl.pallas_call(
        body, name="mix_bwd_conv", grid=(n_seq, n_r),
        in_specs=[blk(0), right, blk(0), blk(1), left(0), left(1),
                  pl.BlockSpec((CONV_WIDTH, CA), lambda b, r: (0, 0))],
        out_specs=[pl.BlockSpec((ts, 2 * CA), lambda b, r: (b * n_r + r, 0)),
                   pl.BlockSpec((CONV_WIDTH, CA), lambda b, r: (0, 0)), pl.BlockSpec((1, CA), lambda b, r: (0, 0))],
        out_shape=[jax.ShapeDtypeStruct((T, 2 * CA), BF16), jax.ShapeDtypeStruct((CONV_WIDTH, CA), F32),
                   jax.ShapeDtypeStruct((1, CA), F32)],
        scratch_shapes=[pltpu.VMEM((HALO + ts, CA), F32), pltpu.VMEM((ts + HALO, CA), F32),
                        pltpu.VMEM((ts, CA), F32), pltpu.VMEM((CONV_WIDTH, 8, CA), F32)],
        compiler_params=_cp("arbitrary", "arbitrary"),
    )(da1, da1, z, z, z, z, conv_w)


def _sum_parts(parts, name):
    n, R, C = parts.shape
    tr = _tile(R, 512) if R % 8 == 0 and R > 512 else R
    if R % tr:
        tr = R

    def body(p_ref, o_ref):
        acc = p_ref[0]
        for k in range(1, n):
            acc = acc + p_ref[k]
        o_ref[...] = acc

    return pl.pallas_call(
        body, name=name, grid=(R // tr,),
        in_specs=[pl.BlockSpec((n, tr, C), lambda i: (0, i, 0))],
        out_specs=pl.BlockSpec((tr, C), lambda i: (i, 0)),
        out_shape=jax.ShapeDtypeStruct((R, C), F32),
        compiler_params=_cp("parallel"),
    )(parts)


def _row_tile(R, want):
    t = min(R, want)
    t -= t % 8
    while t > 8 and R % t:
        t -= 8
    return t if t >= 8 and R % t == 0 else R


def _adamw(w, g, m, v, name):
    R, C = w.shape
    tr = _row_tile(R, 256)
    c1 = 1.0 - ADAM_B1 ** ADAM_STEP
    c2 = 1.0 - ADAM_B2 ** ADAM_STEP

    def body(w_ref, g_ref, m_ref, v_ref, d_ref, nm_ref, nv_ref):
        gv = g_ref[...]
        nm = ADAM_B1 * m_ref[...] + (1.0 - ADAM_B1) * gv
        nv = ADAM_B2 * v_ref[...] + (1.0 - ADAM_B2) * (gv * gv)
        d_ref[...] = -ADAM_LR * ((nm / c1) / (jnp.sqrt(nv / c2) + ADAM_EPS) + ADAM_WD * w_ref[...])
        nm_ref[...] = nm
        nv_ref[...] = nv

    blk = pl.BlockSpec((tr, C), lambda i: (i, 0))
    return pl.pallas_call(
        body, name=name, grid=(R // tr,),
        in_specs=[blk] * 4, out_specs=[blk] * 3,
        out_shape=[jax.ShapeDtypeStruct((R, C), F32)] * 3,
        compiler_params=_cp("parallel"),
    )(w, g, m, v)


def _me():
    return lax.axis_index("x"), lax.axis_index("y"), lax.axis_index("c")


def _block_rows(ref, dev, n):
    start = (4 * dev[0] + 2 * dev[1] + dev[2]) * n
    if len(ref.shape) == 2:
        return ref.at[pl.ds(start, n), :]
    return ref.at[:, pl.ds(start, n), :]


def _all_gather(shards):
    na = len(shards)
    ns = [s.shape[-2] for s in shards]

    def body(*refs):
        ins, outs = refs[:na], refs[na:2 * na]
        send_sems, recv_sems, local_sems = refs[2 * na:]
        x, y, c = _me()
        me, sibling = (x, y, c), (x, y, 1 - c)
        chips = [(1 - x, y), (x, 1 - y), (1 - x, 1 - y)]

        def copy(a, k, block, to, src=None):
            dst = _block_rows(outs[a], block, ns[a])
            return pltpu.make_async_remote_copy(
                src_ref=dst if src is None else src, dst_ref=dst,
                send_sem=send_sems.at[a, k], recv_sem=recv_sems.at[a, k], device_id=to, device_id_type=MESH)

        mine = [pltpu.make_async_copy(ins[a], _block_rows(outs[a], me, ns[a]), local_sems.at[a]) for a in range(na)]
        for cp in mine:
            cp.start()
        first = []
        for a in range(na):
            first.append(copy(a, 0, me, sibling, src=ins[a]))
            first += [copy(a, 1 + j, me, (*chip, c), src=ins[a]) for j, chip in enumerate(chips)]
        for cp in first:
            cp.start()
        passed = []
        for j, chip in enumerate(chips):
            for a in range(na):
                copy(a, 1 + j, (*chip, c), me).wait_recv()
                fwd = copy(a, 4 + j, (*chip, c), sibling)
                fwd.start()
                passed.append(fwd)
        for a in range(na):
            copy(a, 0, sibling, me).wait_recv()
            for j, chip in enumerate(chips):
                copy(a, 4 + j, (*chip, 1 - c), me).wait_recv()
        for cp in first + passed:
            cp.wait_send()
        for cp in mine:
            cp.wait()

    out_shape = [jax.ShapeDtypeStruct(s.shape[:-2] + (N_DEV * s.shape[-2], s.shape[-1]), s.dtype) for s in shards]
    return pl.pallas_call(
        body, name="weights_all_gather",
        in_specs=[ANY] * na, out_specs=[ANY] * na, out_shape=out_shape,
        scratch_shapes=[pltpu.SemaphoreType.DMA((na, 7)), pltpu.SemaphoreType.DMA((na, 7)),
                        pltpu.SemaphoreType.DMA((na,))],
    )(*shards)


def _split_copies(gather, srcs, lands, send_sems, recv_sems, ns):
    x, y, c = _me()
    me = (x, y, c)
    my_slot = 4 * x + 2 * y + c
    copies = []
    for mask in range(1, N_DEV):
        peer = (x ^ (mask >> 2), y ^ ((mask >> 1) & 1), c ^ (mask & 1))
        for a in range(len(srcs)):
            if gather:
                src, dst = srcs[a], _block_rows(lands[a], me, ns[a])
            else:
                src, dst = _block_rows(srcs[a], peer, ns[a]), lands[a].at[my_slot]
            sem = a * (N_DEV - 1) + mask - 1
            copies.append(pltpu.make_async_remote_copy(
                src_ref=src, dst_ref=dst, send_sem=send_sems.at[sem], recv_sem=recv_sems.at[sem],
                device_id=peer, device_id_type=MESH))
    return copies


HBM_SPEC = pl.BlockSpec(memory_space=pltpu.HBM)
SEM_SPEC = pl.BlockSpec(memory_space=pltpu.SEMAPHORE)


def _split_start(gather, srcs, name, dep=None):
    na = len(srcs)
    if gather:
        ns = [s.shape[-2] for s in srcs]
        lands = [lax.empty(s.shape[:-2] + (N_DEV * s.shape[-2], s.shape[-1]), s.dtype) for s in srcs]
    else:
        ns = [s.shape[-2] // N_DEV for s in srcs]
        lands = [lax.empty((N_DEV, s.shape[-2] // N_DEV, s.shape[-1]), s.dtype) for s in srcs]
    n_in = 2 * na + (dep is not None)

    def body(*refs):
        send_sems, recv_sems = refs[n_in], refs[n_in + 1]
        for cp in _split_copies(gather, refs[:na], refs[na:2 * na], send_sems, recv_sems, ns):
            cp.start()
        refs[-1][...] = jnp.zeros_like(refs[-1])

    hbm = lambda a: pltpu.with_memory_space_constraint(a, pltpu.HBM)
    args = [hbm(a) for a in srcs] + [hbm(a) for a in lands] + ([dep] if dep is not None else [])
    out = pl.pallas_call(
        body, name=name,
        in_specs=[HBM_SPEC] * (2 * na) + ([ANY] if dep is not None else []),
        out_specs=[SEM_SPEC, SEM_SPEC] + [HBM_SPEC] * (2 * na) + [pl.BlockSpec(memory_space=pltpu.VMEM)],
        out_shape=[pltpu.SemaphoreType.DMA((na * (N_DEV - 1),)), pltpu.SemaphoreType.DMA((na * (N_DEV - 1),))]
        + [pltpu.HBM(a.shape, a.dtype) for a in srcs + lands] + [jax.ShapeDtypeStruct((8, LANES), F32)],
        input_output_aliases={i: 2 + i for i in range(2 * na)},
        compiler_params=pltpu.CompilerParams(has_side_effects=pltpu.SideEffectType.DATAFLOW_SIDE_EFFECTING),
    )(*args)
    return (gather, ns, out[0], out[1], list(out[2:2 + na]), list(out[2 + na:2 + 2 * na])), out[-1]


def _split_wait(handle, after, name):
    gather, ns, send, recv, srcs, lands = handle
    na = len(srcs)

    def body(*refs):
        send_sems, recv_sems = refs[2 * na], refs[2 * na + 1]
        for cp in _split_copies(gather, refs[:na], refs[na:2 * na], send_sems, recv_sems, ns):
            cp.wait_send()
            cp.wait_recv()

    out = pl.pallas_call(
        body, name=name,
        in_specs=[HBM_SPEC] * (2 * na) + [SEM_SPEC, SEM_SPEC, ANY],
        out_specs=[HBM_SPEC] * (2 * na),
        out_shape=[pltpu.HBM(a.shape, a.dtype) for a in srcs + lands],
        input_output_aliases={i: i for i in range(2 * na)},
        compiler_params=pltpu.CompilerParams(has_side_effects=pltpu.SideEffectType.DATAFLOW_SIDE_EFFECTING),
    )(*srcs, *lands, send, recv, after)
    return list(out[:na]), list(out[na:])


def _small_all_reduce(buf):
    R = buf.shape[0]

    def body(b_ref, o_ref, recv_ref, send_sems, recv_sems):
        x, y, c = _me()
        my_slot = 4 * x + 2 * y + c
        recv_ref[my_slot] = b_ref[...]
        copies = []
        for mask in range(1, N_DEV):
            peer = (x ^ (mask >> 2), y ^ ((mask >> 1) & 1), c ^ (mask & 1))
            copies.append(pltpu.make_async_remote_copy(
                src_ref=b_ref, dst_ref=recv_ref.at[my_slot],
                send_sem=send_sems.at[mask - 1], recv_sem=recv_sems.at[mask - 1],
                device_id=peer, device_id_type=MESH))
        for cp in copies:
            cp.start()
        for cp in copies:
            cp.wait()
        acc = recv_ref[0]
        for k in range(1, N_DEV):
            acc = acc + recv_ref[k]
        o_ref[...] = acc

    return pl.pallas_call(
        body, name="small_all_reduce",
        in_specs=[pl.BlockSpec(memory_space=pltpu.VMEM)], out_specs=pl.BlockSpec(memory_space=pltpu.VMEM),
        out_shape=jax.ShapeDtypeStruct((R, LANES), F32),
        scratch_shapes=[pltpu.VMEM((N_DEV, R, LANES), F32), pltpu.SemaphoreType.DMA((7,)),
                        pltpu.SemaphoreType.DMA((7,))],
        compiler_params=pltpu.CompilerParams(vmem_limit_bytes=VMEM_LIMIT),
    )(buf)


def _pack(arrays):
    flat = jnp.concatenate([a.reshape(-1) for a in arrays])
    pad = (-flat.shape[0]) % (8 * LANES)
    return jnp.pad(flat, (0, pad)).reshape(-1, LANES)


def _unpack(buf, shapes):
    flat = buf.reshape(-1)
    out, off = [], 0
    for s in shapes:
        n = 1
        for d in s:
            n *= d
        out.append(flat[off:off + n].reshape(s))
        off += n
    return out


def _ffn_index(layer, second):
    return (2 * layer + second) * 3


def kernel(x, g_ffn1, w_ffn1_gate, w_ffn1_up, w_ffn1_down, g_mix, w_in_ab, conv_w, conv_b, ln_a_g, ln_a_b, ln_v_g, ln_v_b, sp_w, sp_b, w_out_ab, w_qkv, w_o, g_ffn2, w_ffn2_gate, w_ffn2_up, w_ffn2_down, g_final, loss_target, m_g_ffn1, m_w_ffn1_gate, m_w_ffn1_up, m_w_ffn1_down, m_g_mix, m_w_in_ab, m_conv_w, m_conv_b, m_ln_a_g, m_ln_a_b, m_ln_v_g, m_ln_v_b, m_sp_w, m_sp_b, m_w_out_ab, m_w_qkv, m_w_o, m_g_ffn2, m_w_ffn2_gate, m_w_ffn2_up, m_w_ffn2_down, m_g_final, v_g_ffn1, v_w_ffn1_gate, v_w_ffn1_up, v_w_ffn1_down, v_g_mix, v_w_in_ab, v_conv_w, v_conv_b, v_ln_a_g, v_ln_a_b, v_ln_v_g, v_ln_v_b, v_sp_w, v_sp_b, v_w_out_ab, v_w_qkv, v_w_o, v_g_ffn2, v_w_ffn2_gate, v_w_ffn2_up, v_w_ffn2_down, v_g_final):
    n_seq, S, D = x.shape
    T = n_seq * S
    depth = g_ffn1.shape[0]
    assert depth == 2 and D == D_MODEL
    my_block = 4 * lax.axis_index("x") + 2 * lax.axis_index("y") + lax.axis_index("c")

    ffn_parts = []
    for l in range(depth):
        for gate, up, down in ((w_ffn1_gate, w_ffn1_up, w_ffn1_down), (w_ffn2_gate, w_ffn2_up, w_ffn2_down)):
            ffn_parts += [gate[l].T, up[l].T, down[l]]
    ffn_shard = lambda k: jnp.stack(ffn_parts[3 * k:3 * k + 3]).astype(BF16)
    conv_w_pad = jnp.zeros((HALO, conv_w.shape[2]), F32).at[:CONV_WIDTH].set(conv_w[0]).T
    w_ffn = [None] * (2 * depth)
    w_ffn[0], conv_w_t = _all_gather([ffn_shard(0), conv_w_pad])
    conv_w_full = conv_w_t.T[:CONV_WIDTH]
    shards_a = [w_in_ab[0].T.astype(BF16), w_out_ab[0].astype(BF16), ffn_shard(1)]
    shards_b = [ffn_shard(2), w_qkv[0].T.astype(BF16), w_o[0].astype(BF16), ffn_shard(3)]
    gather_a, token_a = _split_start(True, shards_a, "gather_a_start", dep=conv_w_t)
    gather_b, token = _split_start(True, shards_b, "gather_b_start", dep=token_a)

    def gathered(handle, after, name):
        shards, lands = _split_wait(handle, after, name)
        out = []
        for land, shard in zip(lands, shards):
            n = shard.shape[-2]
            start = (0,) * (land.ndim - 2) + (my_block * n, 0)
            out.append(lax.dynamic_update_slice(land, shard, start))
        return out

    row = lambda a: a.reshape(1, -1)
    tril = jnp.tril(jnp.ones((CHUNK, CHUNK), dtype=bool))
    ws = jnp.where(tril[None], sp_w[0], 0.0).astype(BF16)
    ws_t = jnp.swapaxes(ws, 1, 2)
    bias2d = jnp.repeat(sp_b[0].T, DB, axis=1)
    conv_b2, lag, lab = row(conv_b[0]), row(ln_a_g[0]), row(ln_a_b[0])
    lvg, lvb = row(ln_v_g[0]), row(ln_v_b[0])

    x0 = x.reshape(T, D)
    target = loss_target.reshape(T, D)
    saved = []
    xc = x0
    for l in range(depth):
        xa, a1, b1, h1 = _ffn_fwd(xc, row(g_ffn1[l]), w_ffn[2 * l], 0, f"ffn1_fwd_{l}", dep=token)
        if l % 2 == 0:
            w_in_t, w_out, w_ffn[1] = gathered(gather_a, xa, "gather_a_wait")
            z, hm = _mm_nt(xa, w_in_t, row(g_mix[l]), F32, "mix_in_proj")
            cat, conv_out = _mix_fwd(z, conv_w_full, conv_b2, lag, lab, lvg, lvb, ws, bias2d, n_seq)
            xb = _mm_nn_res(cat, w_out, xa, "mix_out_proj")
            mixer = (z, hm, cat, conv_out)
        else:
            qkv, hm = _mm_nt(xa, w_qkv_t, row(g_mix[l]), BF16, "qkv_proj")
            o, tot = _attn_fwd(qkv, n_seq)
            xb = _mm_nn_res(o, w_o_full, xa, "attn_out_proj")
            mixer = (qkv, hm, o, tot)
        xn, a2, b2, h2 = _ffn_fwd(xb, row(g_ffn2[l]), w_ffn[2 * l + 1], 0, f"ffn2_fwd_{l}")
        saved.append((xc, a1, b1, h1, xa, mixer, xb, a2, b2, h2))
        xc = xn
        if l == 0:
            w_ffn[2], w_qkv_t, w_o_full, w_ffn[3] = gathered(gather_b, xc, "gather_b_wait")

    g, loss_part, dg_final = _loss_head(xc, row(g_final), target)

    dg_ffn1, dg_ffn2, dg_mix = [None] * depth, [None] * depth, [None] * depth
    exchanges = {}
    token = None

    def ffn_back(g, xin, gvec, a, b, h, k, tag, token):
        g, dg, da, db, s, gh = _ffn_bwd(g, xin, gvec, a, b, w_ffn[k], 0, f"ffn{tag}_bwd", dep=token)
        if k == 0:
            return g, dg, (da, db, s, gh, h)
        dws = [_mm_tn(da, h, f"dw_gate{tag}"), _mm_tn(db, h, f"dw_up{tag}"), _mm_tn(s, gh, f"dw_down{tag}")]
        exchanges[f"ffn{k}"], token = _split_start(False, dws, f"exchange_ffn{tag}_start")
        return g, dg, token

    for l in reversed(range(depth)):
        xin, a1, b1, h1, xa, mixer, xb, a2, b2, h2 = saved[l]
        g, dg_ffn2[l], token = ffn_back(g, xb, row(g_ffn2[l]), a2, b2, h2, 2 * l + 1, f"2_{l}", token)
        if l % 2 == 0:
            z, hm, cat, conv_out = mixer
            dcat = _mm_nt(g, w_out, None, F32, "mix_out_bwd", dep=token)
            d_w_out = _mm_tn(cat, g, "dw_out")
            (da1, dz_uv, d_lag, d_lab, d_lvg, d_lvb, d_ws, d_sb) = _mix_bwd_rows(
                dcat, z, conv_out, lag, lab, lvg, lvb, ws, ws_t, bias2d, n_seq)
            dz_a, d_cw, d_cb = _mix_bwd_conv(da1, z, conv_w_full, n_seq)
            dz = jnp.concatenate([dz_a, dz_uv], axis=1)
            d_w_in_t = _mm_tn(dz, hm, "dw_in")
            exchanges["mix"], token = _split_start(False, [d_w_out, d_w_in_t], "exchange_mix_start")
            g, dg_mix[l] = _mm_nn_rmsbwd(dz, w_in_t, xa, row(g_mix[l]), g, "mix_in_bwd", dep=token)
        else:
            qkv, hm, o, tot = mixer
            do = _mm_nt(g, w_o_full, None, BF16, "attn_out_bwd", dep=token)
            d_w_o = _mm_tn(o, g, "dw_o")
            dq, dk, dv = _attn_bwd(qkv, do, tot, n_seq)
            dqkv = jnp.concatenate([dq, dk, dv], axis=1).astype(BF16)
            d_w_qkv_t = _mm_tn(dqkv, hm, "dw_qkv")
            exchanges["attn"], token = _split_start(False, [d_w_o, d_w_qkv_t], "exchange_attn_start")
            g, dg_mix[l] = _mm_nn_rmsbwd(dqkv, w_qkv_t, xa, row(g_mix[l]), g, "qkv_bwd", dep=token)
        g, dg_ffn1[l], token = ffn_back(g, xin, row(g_ffn1[l]), a1, b1, h1, 2 * l, f"1_{l}", token)
    grad_x = g.reshape(n_seq, S, D)

    small = [jnp.concatenate(dg_ffn1), jnp.concatenate(dg_mix), d_cw, d_cb, d_lag, d_lab, d_lvg, d_lvb,
             jnp.where(tril[None], d_ws, 0.0), d_sb.T, jnp.concatenate(dg_ffn2), dg_final, loss_part[:, :1]]
    small_shapes = [(depth, D), (depth, D), (CONV_WIDTH, CA), (1, CA), (1, CA), (1, CA), (1, GB, DB), (1, GB, DB),
                    (1, GB, CHUNK, CHUNK), (1, GB, CHUNK), (depth, D), (D,), ()]
    small_sum = _small_all_reduce(_pack(small))
    red = _unpack(small_sum, small_shapes)
    (gr_g_ffn1, gr_g_mix, gr_cw_full, gr_cb, gr_lag, gr_lab, gr_lvg, gr_lvb, gr_sp_w, gr_sp_b,
     gr_g_ffn2, gr_g_final, loss) = red
    n_cw = conv_w.shape[2]
    gr_cw = lax.dynamic_slice(gr_cw_full, (0, my_block * n_cw), (CONV_WIDTH, n_cw))[None]

    da, db, s, gh, h = token
    token = small_sum
    for which, lhs, rhs in ((2, s, gh), (1, db, h), (0, da, h)):
        dw = _mm_tn(lhs, rhs, f"dw_ffn0_{which}", dep=token)
        exchanges[f"ffn0_{which}"], token = _split_start(False, [dw], f"exchange_ffn0_{which}_start")

    def reduced(key, after):
        out = []
        fulls, lands = _split_wait(exchanges[key], after, f"exchange_{key}_wait")
        for i, (land, full) in enumerate(zip(lands, fulls)):
            n = land.shape[1]
            own = lax.dynamic_slice(full, (my_block * n, 0), (n, full.shape[1]))
            parts = lax.dynamic_update_slice(land, own[None], (my_block, 0, 0))
            out.append(_sum_parts(parts, f"sum_{key}_{i}"))
        return out

    g_ffn_t = [None] * (6 * depth)
    for k in range(1, 2 * depth):
        g_ffn_t[3 * k:3 * k + 3] = reduced(f"ffn{k}", token)
    g_out, g_in = reduced("mix", token)
    g_o, g_qkv = reduced("attn", token)
    g_in, g_out, g_qkv, g_o = g_in.T[None], g_out[None], g_qkv.T[None], g_o[None]

    def ffn_grad(second, which):
        return jnp.stack([g_ffn_t[_ffn_index(l, second) + which].T if which < 2
                          else g_ffn_t[_ffn_index(l, second) + which] for l in range(depth)])

    grads = {
        "g_ffn1": gr_g_ffn1, "g_mix": gr_g_mix, "w_in_ab": g_in, "conv_w": gr_cw, "conv_b": gr_cb, "ln_a_g": gr_lag,
        "ln_a_b": gr_lab, "ln_v_g": gr_lvg, "ln_v_b": gr_lvb, "sp_w": gr_sp_w, "sp_b": gr_sp_b, "w_out_ab": g_out,
        "w_qkv": g_qkv, "w_o": g_o, "g_ffn2": gr_g_ffn2, "w_ffn2_gate": ffn_grad(1, 0), "w_ffn2_up": ffn_grad(1, 1),
        "w_ffn2_down": ffn_grad(1, 2), "g_final": gr_g_final,
    }
    weights = dict(g_ffn1=g_ffn1, w_ffn1_gate=w_ffn1_gate, w_ffn1_up=w_ffn1_up, w_ffn1_down=w_ffn1_down, g_mix=g_mix,
                   w_in_ab=w_in_ab, conv_w=conv_w, conv_b=conv_b, ln_a_g=ln_a_g, ln_a_b=ln_a_b, ln_v_g=ln_v_g,
                   ln_v_b=ln_v_b, sp_w=sp_w, sp_b=sp_b, w_out_ab=w_out_ab, w_qkv=w_qkv, w_o=w_o, g_ffn2=g_ffn2,
                   w_ffn2_gate=w_ffn2_gate, w_ffn2_up=w_ffn2_up, w_ffn2_down=w_ffn2_down, g_final=g_final)
    m_in = dict(g_ffn1=m_g_ffn1, w_ffn1_gate=m_w_ffn1_gate, w_ffn1_up=m_w_ffn1_up, w_ffn1_down=m_w_ffn1_down,
                g_mix=m_g_mix, w_in_ab=m_w_in_ab, conv_w=m_conv_w, conv_b=m_conv_b, ln_a_g=m_ln_a_g, ln_a_b=m_ln_a_b,
                ln_v_g=m_ln_v_g, ln_v_b=m_ln_v_b, sp_w=m_sp_w, sp_b=m_sp_b, w_out_ab=m_w_out_ab, w_qkv=m_w_qkv,
                w_o=m_w_o, g_ffn2=m_g_ffn2, w_ffn2_gate=m_w_ffn2_gate, w_ffn2_up=m_w_ffn2_up,
                w_ffn2_down=m_w_ffn2_down, g_final=m_g_final)
    v_in = dict(g_ffn1=v_g_ffn1, w_ffn1_gate=v_w_ffn1_gate, w_ffn1_up=v_w_ffn1_up, w_ffn1_down=v_w_ffn1_down,
                g_mix=v_g_mix, w_in_ab=v_w_in_ab, conv_w=v_conv_w, conv_b=v_conv_b, ln_a_g=v_ln_a_g, ln_a_b=v_ln_a_b,
                ln_v_g=v_ln_v_g, ln_v_b=v_ln_v_b, sp_w=v_sp_w, sp_b=v_sp_b, w_out_ab=v_w_out_ab, w_qkv=v_w_qkv,
                w_o=v_w_o, g_ffn2=v_g_ffn2, w_ffn2_gate=v_w_ffn2_gate, w_ffn2_up=v_w_ffn2_up,
                w_ffn2_down=v_w_ffn2_down, g_final=v_g_final)
    names = list(weights)
    grads = {n: grads[n].reshape(weights[n].shape) for n in grads}

    delta, new_m, new_v = {}, {}, {}

    def adamw_big(n):
        shp = weights[n].shape
        two = lambda a: a.reshape(-1, shp[-1])
        d, nm, nv = _adamw(two(weights[n]), two(grads[n]), two(m_in[n]), two(v_in[n]), f"adamw_{n}")
        delta[n], new_m[n], new_v[n] = d.reshape(shp), nm.reshape(shp), nv.reshape(shp)

    big = [n for n in names if n.startswith("w_")]
    late = ["w_ffn1_gate", "w_ffn1_up", "w_ffn1_down"]
    for n in big:
        if n not in late:
            adamw_big(n)
    after = jnp.concatenate([delta[n].reshape(-1)[:1] for n in big if n not in late]).reshape(1, -1)
    for which in (2, 1, 0):
        g_ffn_t[which], = reduced(f"ffn0_{which}", after)
    for which, n in enumerate(late):
        grads[n] = ffn_grad(0, which).reshape(weights[n].shape)
        adamw_big(n)
    little = [n for n in names if n not in big]
    shapes = [weights[n].shape for n in little]
    d, nm, nv = _adamw(_pack([weights[n] for n in little]), _pack([grads[n] for n in little]),
                       _pack([m_in[n] for n in little]), _pack([v_in[n] for n in little]), "adamw_small")
    for n, dd, mm, vv in zip(little, _unpack(d, shapes), _unpack(nm, shapes), _unpack(nv, shapes)):
        delta[n], new_m[n], new_v[n] = dd, mm, vv

    return (loss, grad_x, *[grads[n] for n in names], *[delta[n] for n in names],
            *[new_m[n] for n in names], *[new_v[n] for n in names])
```

```python
import functools

import jax
import jax.numpy as jnp
from jax import lax
from jax.experimental import pallas as pl
from jax.experimental.pallas import tpu as pltpu

F32 = jnp.float32
BF16 = jnp.bfloat16

D_MODEL = 1024
CA = 512
CB = 512
GB = 4
DB = 128
CHUNK = 128
CONV_WIDTH = 31
N_HEADS = 16
HEAD_DIM = 64
EPS = 1e-6
N_DEV = 8
LANES = 128
QB = 128
ATT_TQ = 512
FFN_TN = 1408
HALO = 32
CONV_ROWS = 32
ATT_SCALE = HEAD_DIM ** -0.5

ADAM_LR = 0.001
ADAM_B1 = 0.9
ADAM_B2 = 0.999
ADAM_EPS = 1e-08
ADAM_WD = 0.01
ADAM_STEP = 10

NT = (((1,), (1,)), ((), ()))
NN = (((1,), (0,)), ((), ()))
TN = (((0,), (0,)), ((), ()))
MESH = pl.DeviceIdType.MESH
ANY = pl.BlockSpec(memory_space=pl.ANY)
VMEM_LIMIT = 60 * 1024 * 1024


def _dot(a, b, dims):
    return lax.dot_general(a, b, dims, preferred_element_type=F32)


def _cp(*sem):
    return pltpu.CompilerParams(dimension_semantics=sem, vmem_limit_bytes=VMEM_LIMIT)


def _pcall(body, *, in_specs, args, dep=None, **kw):
    if dep is not None:
        n_in = len(in_specs)
        inner = body

        def body(*refs):
            inner(*refs[:n_in], *refs[n_in + 1:])

        in_specs = list(in_specs) + [ANY]
        args = tuple(args) + (dep,)
    return pl.pallas_call(body, in_specs=list(in_specs), **kw)(*args)


def _tile(n, want):
    if n <= want:
        return n
    t = want - want % LANES
    while t > LANES and n % t:
        t -= LANES
    assert n % t == 0, (n, want)
    return t


def _sigmoid(x):
    return 1.0 / (1.0 + jnp.exp(-x))


def _rstd(x):
    return lax.rsqrt(jnp.mean(x * x, axis=-1, keepdims=True) + EPS)


def _rms_bwd(x, g, dh):
    r = _rstd(x)
    u = dh * g
    dx = r * (u - x * (r * r) * jnp.mean(u * x, axis=-1, keepdims=True))
    dg = jnp.sum(dh * x * r, axis=0, keepdims=True)
    return dx, dg


def _ln_fwd(x, g, b):
    mu = jnp.mean(x, axis=-1, keepdims=True)
    xc = x - mu
    r = lax.rsqrt(jnp.mean(xc * xc, axis=-1, keepdims=True) + EPS)
    xh = xc * r
    return xh * g + b, xh, r


def _ln_bwd(dy, xh, r, g):
    dxh = dy * g
    return r * (dxh - jnp.mean(dxh, axis=-1, keepdims=True)
                - xh * jnp.mean(dxh * xh, axis=-1, keepdims=True))


def _ffn_fwd(x, g, wall, base, name, dep=None):
    T, D = x.shape
    F = wall.shape[1]
    tm, tn = _tile(T, 512), _tile(F, FFN_TN)
    n_j = F // tn

    def body(x_ref, g_ref, wg_ref, wu_ref, wd_ref, xo_ref, a_ref, b_ref, h_ref, acc_ref):
        j = pl.program_id(1)

        @pl.when(j == 0)
        def _():
            xv = x_ref[...]
            h_ref[...] = (xv * _rstd(xv) * g_ref[...]).astype(BF16)
            acc_ref[...] = jnp.zeros_like(acc_ref)

        h = h_ref[...]
        a = _dot(h, wg_ref[...], NT)
        b = _dot(h, wu_ref[...], NT)
        a_ref[...] = a.astype(BF16)
        b_ref[...] = b.astype(BF16)
        s = (a * _sigmoid(a) * b).astype(BF16)
        acc_ref[...] += _dot(s, wd_ref[...], NN)

        @pl.when(j == n_j - 1)
        def _():
            xo_ref[...] = x_ref[...] + 0.5 * acc_ref[...]

    wspec = lambda k: pl.BlockSpec((None, tn, D), lambda i, j: (base + k, j, 0))
    return _pcall(
        body, name=name, grid=(T // tm, n_j), dep=dep, args=(x, g, wall, wall, wall),
        in_specs=[pl.BlockSpec((tm, D), lambda i, j: (i, 0)), pl.BlockSpec((1, D), lambda i, j: (0, 0)),
                  wspec(0), wspec(1), wspec(2)],
        out_specs=[pl.BlockSpec((tm, D), lambda i, j: (i, 0)), pl.BlockSpec((tm, tn), lambda i, j: (i, j)),
                   pl.BlockSpec((tm, tn), lambda i, j: (i, j)), pl.BlockSpec((tm, D), lambda i, j: (i, 0))],
        out_shape=[jax.ShapeDtypeStruct((T, D), F32), jax.ShapeDtypeStruct((T, F), BF16),
                   jax.ShapeDtypeStruct((T, F), BF16), jax.ShapeDtypeStruct((T, D), BF16)],
        scratch_shapes=[pltpu.VMEM((tm, D), F32)],
        compiler_params=_cp("parallel", "arbitrary"),
    )


def _ffn_bwd(go, x, g, a, b, wall, base, name, dep=None):
    T, D = x.shape
    F = wall.shape[1]
    tm, tn = _tile(T, 512), _tile(F, FFN_TN)
    n_j = F // tn

    def body(go_ref, x_ref, g_ref, a_ref, b_ref, wg_ref, wu_ref, wd_ref,
             gx_ref, dg_ref, da_ref, db_ref, s_ref, gh_ref, acc_ref):
        i, j = pl.program_id(0), pl.program_id(1)

        @pl.when(j == 0)
        def _():
            gh_ref[...] = (0.5 * go_ref[...]).astype(BF16)
            acc_ref[...] = jnp.zeros_like(acc_ref)

        @pl.when((i == 0) & (j == 0))
        def _():
            dg_ref[...] = jnp.zeros_like(dg_ref)

        ds = _dot(gh_ref[...], wd_ref[...], NT)
        av = a_ref[...].astype(F32)
        bv = b_ref[...].astype(F32)
        sig = _sigmoid(av)
        sl = av * sig
        dab = (ds * bv * (sig * (1.0 + av * (1.0 - sig)))).astype(BF16)
        dbb = (ds * sl).astype(BF16)
        s_ref[...] = (sl * bv).astype(BF16)
        da_ref[...] = dab
        db_ref[...] = dbb
        acc_ref[...] += _dot(dab, wg_ref[...], NN) + _dot(dbb, wu_ref[...], NN)

        @pl.when(j == n_j - 1)
        def _():
            dx, dg = _rms_bwd(x_ref[...], g_ref[...], acc_ref[...])
            gx_ref[...] = go_ref[...] + dx
            dg_ref[...] += dg

    wspec = lambda k: pl.BlockSpec((None, tn, D), lambda i, j: (base + k, j, 0))
    row = pl.BlockSpec((tm, D), lambda i, j: (i, 0))
    hid = pl.BlockSpec((tm, tn), lambda i, j: (i, j))
    vec = pl.BlockSpec((1, D), lambda i, j: (0, 0))
    return _pcall(
        body, name=name, grid=(T // tm, n_j), dep=dep, args=(go, x, g, a, b, wall, wall, wall),
        in_specs=[row, row, vec, hid, hid, wspec(0), wspec(1), wspec(2)],
        out_specs=[row, vec, hid, hid, hid, row],
        out_shape=[jax.ShapeDtypeStruct((T, D), F32), jax.ShapeDtypeStruct((1, D), F32),
                   jax.ShapeDtypeStruct((T, F), BF16), jax.ShapeDtypeStruct((T, F), BF16),
                   jax.ShapeDtypeStruct((T, F), BF16), jax.ShapeDtypeStruct((T, D), BF16)],
        scratch_shapes=[pltpu.VMEM((tm, D), F32)],
        compiler_params=_cp("arbitrary", "arbitrary"),
    )


def _mm_tn(a, b, name, dep=None):
    T, M = a.shape
    N = b.shape[1]
    tmm, tk = _tile(M, 1536), _tile(T, 1024)

    def body(a_ref, b_ref, o_ref):
        @pl.when(pl.program_id(1) == 0)
        def _():
            o_ref[...] = jnp.zeros_like(o_ref)

        o_ref[...] += _dot(a_ref[...].astype(BF16), b_ref[...].astype(BF16), TN)

    return _pcall(
        body, name=name, grid=(M // tmm, T // tk), dep=dep, args=(a, b),
        in_specs=[pl.BlockSpec((tk, tmm), lambda m, k: (k, m)), pl.BlockSpec((tk, N), lambda m, k: (k, 0))],
        out_specs=pl.BlockSpec((tmm, N), lambda m, k: (m, 0)),
        out_shape=jax.ShapeDtypeStruct((M, N), F32),
        compiler_params=_cp("parallel", "arbitrary"),
    )


def _mm_nt(x, wt, g, out_dtype, name, dep=None):
    T, K = x.shape
    N = wt.shape[0]
    tm, tn = _tile(T, 512), _tile(N, 1536)
    norm = g is not None

    def body(*refs):
        if norm:
            x_ref, g_ref, w_ref, o_ref, h_ref = refs
        else:
            x_ref, w_ref, o_ref, h_ref = refs

        @pl.when(pl.program_id(1) == 0)
        def _():
            xv = x_ref[...].astype(F32)
            if norm:
                xv = xv * _rstd(xv) * g_ref[...]
            h_ref[...] = xv.astype(BF16)

        o_ref[...] = _dot(h_ref[...], w_ref[...], NT).astype(out_dtype)

    row = pl.BlockSpec((tm, K), lambda i, j: (i, 0))
    wsp = pl.BlockSpec((tn, K), lambda i, j: (j, 0))
    osp = pl.BlockSpec((tm, tn), lambda i, j: (i, j))
    if norm:
        return pl.pallas_call(
            body, name=name, grid=(T // tm, N // tn),
            in_specs=[row, pl.BlockSpec((1, K), lambda i, j: (0, 0)), wsp],
            out_specs=[osp, row],
            out_shape=[jax.ShapeDtypeStruct((T, N), out_dtype), jax.ShapeDtypeStruct((T, K), BF16)],
            compiler_params=_cp("parallel", "arbitrary"),
        )(x, g, wt)
    return _pcall(
        body, name=name, grid=(T // tm, N // tn), dep=dep, args=(x, wt),
        in_specs=[row, wsp], out_specs=osp,
        out_shape=jax.ShapeDtypeStruct((T, N), out_dtype),
        scratch_shapes=[pltpu.VMEM((tm, K), BF16)],
        compiler_params=_cp("parallel", "arbitrary"),
    )


def _mm_nn_res(act, w, resid, name):
    T, K = act.shape
    D = w.shape[1]
    tm = _tile(T, 512)

    def body(a_ref, w_ref, r_ref, o_ref):
        o_ref[...] = r_ref[...] + _dot(a_ref[...].astype(BF16), w_ref[...], NN)

    return pl.pallas_call(
        body, name=name, grid=(T // tm,),
        in_specs=[pl.BlockSpec((tm, K), lambda i: (i, 0)), pl.BlockSpec((K, D), lambda i: (0, 0)),
                  pl.BlockSpec((tm, D), lambda i: (i, 0))],
        out_specs=pl.BlockSpec((tm, D), lambda i: (i, 0)),
        out_shape=jax.ShapeDtypeStruct((T, D), F32),
        compiler_params=_cp("parallel"),
    )(act, w, resid)


def _mm_nn_rmsbwd(acts, w, x, g, gprev, name, dep=None):
    T = acts[0].shape[0]
    ks = [a.shape[1] for a in acts]
    K, D = w.shape
    assert sum(ks) == K
    tm = _tile(T, 512)
    na = len(acts)

    def body(*refs):
        a_refs = refs[:na]
        w_ref, x_ref, g_ref, gp_ref, o_ref, dg_ref = refs[na:]

        @pl.when(pl.program_id(0) == 0)
        def _():
            dg_ref[...] = jnp.zeros_like(dg_ref)

        dh, off = None, 0
        for a_ref, k in zip(a_refs, ks):
            part = _dot(a_ref[...].astype(BF16), w_ref[off:off + k, :], NN)
            dh = part if dh is None else dh + part
            off += k
        dx, dg = _rms_bwd(x_ref[...], g_ref[...], dh)
        o_ref[...] = gp_ref[...] + dx
        dg_ref[...] += dg

    row = pl.BlockSpec((tm, D), lambda i: (i, 0))
    vec = pl.BlockSpec((1, D), lambda i: (0, 0))
    return _pcall(
        body, name=name, grid=(T // tm,), dep=dep, args=(*acts, w, x, g, gprev),
        in_specs=[pl.BlockSpec((tm, k), lambda i: (i, 0)) for k in ks]
        + [pl.BlockSpec((K, D), lambda i: (0, 0)), row, vec, row],
        out_specs=[row, vec],
        out_shape=[jax.ShapeDtypeStruct((T, D), F32), jax.ShapeDtypeStruct((1, D), F32)],
        compiler_params=_cp("arbitrary"),
    )


def _loss_head(x, g, target):
    T, D = x.shape
    tm = _tile(T, 512)

    def body(x_ref, g_ref, t_ref, dx_ref, loss_ref, dg_ref):
        @pl.when(pl.program_id(0) == 0)
        def _():
            loss_ref[...] = jnp.zeros_like(loss_ref)
            dg_ref[...] = jnp.zeros_like(dg_ref)

        xv = x_ref[...]
        gv = g_ref[...]
        e = xv * _rstd(xv) * gv - t_ref[...]
        per_tok = jnp.sum(e * e, axis=-1, keepdims=True) * (1.0 / D)
        loss_ref[...] += 0.5 * jnp.sum(per_tok, axis=0, keepdims=True)
        dx, dg = _rms_bwd(xv, gv, e * (1.0 / D))
        dx_ref[...] = dx
        dg_ref[...] += dg

    row = pl.BlockSpec((tm, D), lambda i: (i, 0))
    vec = pl.BlockSpec((1, D), lambda i: (0, 0))
    return pl.pallas_call(
        body, name="loss_head", grid=(T // tm,),
        in_specs=[row, vec, row],
        out_specs=[row, pl.BlockSpec((1, LANES), lambda i: (0, 0)), vec],
        out_shape=[jax.ShapeDtypeStruct((T, D), F32), jax.ShapeDtypeStruct((1, LANES), F32),
                   jax.ShapeDtypeStruct((1, D), F32)],
        compiler_params=_cp("arbitrary"),
    )(x, g, target)


def _log_gates(z):
    ls = jnp.minimum(z, 0.0) - jnp.log(1.0 + jnp.exp(-jnp.abs(z)))
    return ls, ls - z


def _cumsum_mm(v, u2):
    hi = v.astype(BF16)
    lo = (v - hi.astype(F32)).astype(BF16)
    return _dot(jnp.concatenate([hi, lo], axis=1), u2, NN)


def _half_rowsum(v):
    n = v.shape[0]
    s0 = jnp.sum(v[:, :QB], axis=1, keepdims=True)
    s1 = jnp.sum(v[:, QB:], axis=1, keepdims=True)
    return jnp.concatenate([jnp.broadcast_to(s0, (n, QB)), jnp.broadcast_to(s1, (n, QB))], axis=1)


def _stack_heads(src_ref, dst_ref, n_blk):
    m0 = lax.broadcasted_iota(jnp.int32, (1, LANES), 1) < HEAD_DIM

    def fill(c, carry):
        blk = src_ref[pl.ds(pl.multiple_of(c * QB, QB), QB), :]
        zero = jnp.zeros_like(blk)
        dst_ref[c, 0:QB, :] = jnp.where(m0, blk, zero)
        dst_ref[c, QB:2 * QB, :] = jnp.where(m0, zero, blk)
        return carry

    lax.fori_loop(0, n_blk, fill, 0)


def _diag_mask(tq, j):
    n = tq - j * QB
    row = lax.broadcasted_iota(jnp.int32, (n, 2 * QB), 0)
    col = lax.broadcasted_iota(jnp.int32, (n, 2 * QB), 1)
    return (col & (QB - 1)) < row


def _tri_blockdiag(upper):
    r = lax.broadcasted_iota(jnp.int32, (2 * QB, 2 * QB), 0)
    c = lax.broadcasted_iota(jnp.int32, (2 * QB, 2 * QB), 1)
    same = (r // QB) == (c // QB)
    u = (same & ((r > c) if upper else (r < c))).astype(BF16)
    return jnp.concatenate([u, u], axis=0)


def _attn_tiles(T, n_seq):
    S = T // n_seq
    tq = ATT_TQ if S % ATT_TQ == 0 else QB
    return S, tq, tq // QB, S // tq, S // QB


def _attn_fwd(qkv, n_seq):
    T = qkv.shape[0]
    S, tq, r, n_q, n_k = _attn_tiles(T, n_seq)
    n_p = D_MODEL // LANES
    u_suffix = _tri_blockdiag(True)

    def body(q_ref, k_ref, v_ref, u_ref, o_ref, tot_ref, kk_ref, vv_ref, lr_s, acc_s):
        qi = pl.program_id(2)

        @pl.when(qi == 0)
        def _():
            _stack_heads(k_ref, kk_ref, n_k)
            _stack_heads(v_ref, vv_ref, n_k)

        u = u_ref[...]
        lr_s[...] = jnp.zeros_like(lr_s)
        acc_s[...] = jnp.zeros_like(acc_s)

        def step(kj, q, mask, lr, acc):
            ls, lk = _log_gates(_dot(q, kk_ref[kj], NT))
            if mask is not None:
                lk = jnp.where(mask, lk, 0.0)
            a = jnp.exp(ls + _cumsum_mm(lk, u) + lr)
            if mask is not None:
                a = jnp.where(mask, a, 0.0)
            return lr + _half_rowsum(lk), acc + _dot(a.astype(BF16), vv_ref[kj], NN)

        for j in reversed(range(r)):
            rows = slice(j * QB, tq)
            lr, acc = step(qi * r + j, q_ref[rows, :] * ATT_SCALE, _diag_mask(tq, j), lr_s[rows, :], acc_s[rows, :])
            lr_s[rows, :] = lr
            acc_s[rows, :] = acc

        q = q_ref[...] * ATT_SCALE

        def off(it, carry):
            lr, acc = lr_s[...], acc_s[...]
            for j in range(r):
                lr, acc = step((qi - it) * r - 1 - j, q, None, lr, acc)
            lr_s[...] = lr
            acc_s[...] = acc
            return carry

        lax.fori_loop(0, qi, off, 0)
        o_ref[...] = acc_s[...].astype(BF16)
        tot_ref[...] = lr_s[...]

    return pl.pallas_call(
        body, name="attn_fwd", grid=(n_seq, n_p, n_q),
        in_specs=[pl.BlockSpec((tq, LANES), lambda b, p, qi: (b * n_q + qi, p)),
                  pl.BlockSpec((S, LANES), lambda b, p, qi: (b, n_p + p)),
                  pl.BlockSpec((S, LANES), lambda b, p, qi: (b, 2 * n_p + p)),
                  pl.BlockSpec((4 * QB, 2 * QB), lambda b, p, qi: (0, 0))],
        out_specs=[pl.BlockSpec((tq, LANES), lambda b, p, qi: (b * n_q + qi, p)),
                   pl.BlockSpec((tq, 2 * QB), lambda b, p, qi: (b * n_q + qi, p))],
        out_shape=[jax.ShapeDtypeStruct((T, D_MODEL), BF16), jax.ShapeDtypeStruct((T, 2 * D_MODEL), F32)],
        scratch_shapes=[pltpu.VMEM((n_k, 2 * QB, LANES), BF16), pltpu.VMEM((n_k, 2 * QB, LANES), BF16),
                        pltpu.VMEM((tq, 2 * QB), F32), pltpu.VMEM((tq, LANES), F32)],
        compiler_params=_cp("parallel", "parallel", "arbitrary"),
    )(qkv, qkv, qkv, u_suffix)


def _attn_bwd(qkv, do, tot, n_seq):
    T = qkv.shape[0]
    S, tq, r, n_q, n_k = _attn_tiles(T, n_seq)
    n_p = D_MODEL // LANES
    u_prefix = _tri_blockdiag(False)

    def body(q_ref, k_ref, v_ref, do_ref, tot_ref, u_ref, dq_ref, dk_out, dv_out,
             kk_ref, vv_ref, cl_s, cg_s, dq_s, dk_ref, dv_ref):
        qi = pl.program_id(2)

        @pl.when(qi == 0)
        def _():
            _stack_heads(k_ref, kk_ref, n_k)
            _stack_heads(v_ref, vv_ref, n_k)
            dk_ref[...] = jnp.zeros_like(dk_ref)
            dv_ref[...] = jnp.zeros_like(dv_ref)

        u = u_ref[...]
        m0 = lax.broadcasted_iota(jnp.int32, (1, LANES), 1) < HEAD_DIM
        cl_s[...] = jnp.zeros_like(cl_s)
        cg_s[...] = jnp.zeros_like(cg_s)
        dq_s[...] = jnp.zeros_like(dq_s)

        def step(kj, q, dov, tot2, mask, cl, cg, dq):
            kk = kk_ref[kj]
            ls, lk = _log_gates(_dot(q, kk, NT))
            if mask is not None:
                lk = jnp.where(mask, lk, 0.0)
            a = jnp.exp(ls + (tot2 - cl - (_cumsum_mm(lk, u) + lk)))
            if mask is not None:
                a = jnp.where(mask, a, 0.0)
            g = a * _dot(dov, vv_ref[kj], NT)
            dz = g - (g + _dot(g.astype(BF16), u[:2 * QB], NN) + cg) * jnp.exp(ls)
            if mask is not None:
                dz = jnp.where(mask, dz, 0.0)
            dz = dz.astype(BF16)
            rows = pl.ds(pl.multiple_of(kj * QB, QB), QB)
            dvt = _dot(a.astype(BF16), dov, TN)
            dv_ref[rows, :] += jnp.where(m0, dvt[:QB], dvt[QB:])
            dkt = _dot(dz, q, TN)
            dk_ref[rows, :] += jnp.where(m0, dkt[:QB], dkt[QB:])
            return cl + _half_rowsum(lk), cg + _half_rowsum(g), dq + _dot(dz, kk, NN)

        q = q_ref[...] * ATT_SCALE
        dov = do_ref[...]
        tot2 = tot_ref[...]

        def off(it, carry):
            cl, cg, dq = cl_s[...], cg_s[...], dq_s[...]
            for j in range(r):
                cl, cg, dq = step(it * r + j, q, dov, tot2, None, cl, cg, dq)
            cl_s[...] = cl
            cg_s[...] = cg
            dq_s[...] = dq
            return carry

        lax.fori_loop(0, qi, off, 0)

        for j in range(r):
            rows = slice(j * QB, tq)
            cl, cg, dq = step(qi * r + j, q_ref[rows, :] * ATT_SCALE, do_ref[rows, :], tot_ref[rows, :],
                              _diag_mask(tq, j), cl_s[rows, :], cg_s[rows, :], dq_s[rows, :])
            cl_s[rows, :] = cl
            cg_s[rows, :] = cg
            dq_s[rows, :] = dq
        dq_ref[...] = (dq_s[...] * ATT_SCALE).astype(BF16)

        @pl.when(qi == n_q - 1)
        def _():
            dk_out[...] = dk_ref[...].astype(BF16)
            dv_out[...] = dv_ref[...].astype(BF16)

    qspec = pl.BlockSpec((tq, LANES), lambda b, p, qi: (b * n_q + qi, p))
    seq = lambda off: pl.BlockSpec((S, LANES), lambda b, p, qi: (b, off + p))
    return pl.pallas_call(
        body, name="attn_bwd", grid=(n_seq, n_p, n_q),
        in_specs=[qspec, seq(n_p), seq(2 * n_p), qspec,
                  pl.BlockSpec((tq, 2 * QB), lambda b, p, qi: (b * n_q + qi, p)),
                  pl.BlockSpec((4 * QB, 2 * QB), lambda b, p, qi: (0, 0))],
        out_specs=[qspec, seq(0), seq(0)],
        out_shape=[jax.ShapeDtypeStruct((T, D_MODEL), BF16)] * 3,
        scratch_shapes=[pltpu.VMEM((n_k, 2 * QB, LANES), BF16), pltpu.VMEM((n_k, 2 * QB, LANES), BF16),
                        pltpu.VMEM((tq, 2 * QB), F32), pltpu.VMEM((tq, 2 * QB), F32), pltpu.VMEM((tq, LANES), F32),
                        pltpu.VMEM((S, LANES), F32), pltpu.VMEM((S, LANES), F32)],
        compiler_params=_cp("parallel", "parallel", "arbitrary"),
    )(qkv, qkv, qkv, do, tot, u_prefix)


def _glu_with_halo(av_ref, ag_ref, avh_ref, agh_ref, a0_s, first, ts):
    hal = avh_ref[...] * _sigmoid(agh_ref[...])
    a0_s[0:HALO, :] = jnp.where(first, 0.0, hal)
    a0_s[HALO:HALO + ts, :] = av_ref[...] * _sigmoid(ag_ref[...])


def _mix_specs(ts, n_r, with_left):
    blk = lambda c: pl.BlockSpec((ts, CA), lambda b, r: (b * n_r + r, c))
    per = ts // HALO
    left = lambda c: pl.BlockSpec((HALO, CA), lambda b, r: (jnp.maximum((b * n_r + r) * per - 1, 0), c))
    return blk, (left if with_left else None)


def _mix_fwd(z, conv_w, conv_b, ln_a_g, ln_a_b, ln_v_g, ln_v_b, ws, bias2d, n_seq):
    T = z.shape[0]
    S = T // n_seq
    ts = _tile(S, 512)
    n_r = S // ts
    shift = HALO - (CONV_WIDTH - 1)

    def body(av_ref, ag_ref, avh_ref, agh_ref, u_ref, v_ref, cw_ref, cb_ref, lag_ref, lab_ref,
             lvg_ref, lvb_ref, ws_ref, bias_ref, cat_ref, a1_ref, a0_s):
        _glu_with_halo(av_ref, ag_ref, avh_ref, agh_ref, a0_s, pl.program_id(1) == 0, ts)
        for rb in range(ts // CONV_ROWS):
            base = rb * CONV_ROWS
            acc = jnp.broadcast_to(cb_ref[...], (CONV_ROWS, CA))
            for k in range(CONV_WIDTH):
                acc = acc + cw_ref[k:k + 1, :] * a0_s[base + shift + k:base + shift + k + CONV_ROWS, :]
            a1_ref[base:base + CONV_ROWS, :] = acc
        y, _, _ = _ln_fwd(a1_ref[...], lag_ref[...], lab_ref[...])
        cat_ref[:, 0:CA] = (y * _sigmoid(y)).astype(BF16)
        for gi in range(GB):
            sl = slice(gi * DB, (gi + 1) * DB)
            v1, _, _ = _ln_fwd(v_ref[:, sl], lvg_ref[:, sl], lvb_ref[:, sl])
            v1 = v1.astype(BF16)
            for c in range(ts // CHUNK):
                rs = slice(c * CHUNK, (c + 1) * CHUNK)
                v2 = _dot(ws_ref[gi], v1[rs], NN) + bias_ref[:, sl]
                cat_ref[rs, CA + gi * DB:CA + (gi + 1) * DB] = (u_ref[rs, sl] * v2).astype(BF16)

    blk, left = _mix_specs(ts, n_r, True)
    vec = pl.BlockSpec((1, CA), lambda b, r: (0, 0))
    return pl.pallas_call(
        body, name="mix_fwd", grid=(n_seq, n_r),
        in_specs=[blk(0), blk(1), left(0), left(1), blk(2), blk(3),
                  pl.BlockSpec((CONV_WIDTH, CA), lambda b, r: (0, 0)), vec, vec, vec, vec, vec,
                  pl.BlockSpec((GB, CHUNK, CHUNK), lambda b, r: (0, 0, 0)),
                  pl.BlockSpec((CHUNK, CB), lambda b, r: (0, 0))],
        out_specs=[pl.BlockSpec((ts, CA + CB), lambda b, r: (b * n_r + r, 0)), blk(0)],
        out_shape=[jax.ShapeDtypeStruct((T, CA + CB), BF16), jax.ShapeDtypeStruct((T, CA), F32)],
        scratch_shapes=[pltpu.VMEM((HALO + ts, CA), F32)],
        compiler_params=_cp("parallel", "parallel"),
    )(z, z, z, z, z, z, conv_w, conv_b, ln_a_g, ln_a_b, ln_v_g, ln_v_b, ws, bias2d)


def _mix_bwd_rows(dcat, z, a1, ln_a_g, ln_a_b, ln_v_g, ln_v_b, ws, ws_t, bias2d, n_seq):
    T = z.shape[0]
    S = T // n_seq
    ts = _tile(S, 512)
    n_r = S // ts

    def body(dc_ref, u_ref, v_ref, a1_ref, lag_ref, lab_ref, lvg_ref, lvb_ref, ws_ref, wst_ref, bias_ref,
             da1_ref, dz_ref, dlag_ref, dlab_ref, dlvg_ref, dlvb_ref, dws_ref, dsb_ref, dv1_s, dbias_s):
        first = (pl.program_id(0) == 0) & (pl.program_id(1) == 0)
        last = (pl.program_id(0) == n_seq - 1) & (pl.program_id(1) == n_r - 1)

        @pl.when(first)
        def _():
            for ref in (dlag_ref, dlab_ref, dlvg_ref, dlvb_ref, dws_ref, dbias_s):
                ref[...] = jnp.zeros_like(ref)

        lag = lag_ref[...]
        y, xh, r = _ln_fwd(a1_ref[...], lag, lab_ref[...])
        sig = _sigmoid(y)
        dy = dc_ref[:, 0:CA] * (sig * (1.0 + y * (1.0 - sig)))
        dlag_ref[...] += jnp.sum(dy * xh, axis=0, keepdims=True)
        dlab_ref[...] += jnp.sum(dy, axis=0, keepdims=True)
        da1_ref[...] = _ln_bwd(dy, xh, r, lag)

        tril = (lax.broadcasted_iota(jnp.int32, (CHUNK, CHUNK), 0)
                >= lax.broadcasted_iota(jnp.int32, (CHUNK, CHUNK), 1))
        for gi in range(GB):
            sl = slice(gi * DB, (gi + 1) * DB)
            lvg = lvg_ref[:, sl]
            v1, vh, vr = _ln_fwd(v_ref[:, sl], lvg, lvb_ref[:, sl])
            v1 = v1.astype(BF16)
            for c in range(ts // CHUNK):
                rs = slice(c * CHUNK, (c + 1) * CHUNK)
                v2 = _dot(ws_ref[gi], v1[rs], NN) + bias_ref[:, sl]
                dbo = dc_ref[rs, CA + gi * DB:CA + (gi + 1) * DB]
                dz_ref[rs, sl] = (dbo * v2).astype(BF16)
                dv2 = dbo * u_ref[rs, sl]
                dbias_s[:, sl] += dv2
                dv2b = dv2.astype(BF16)
                dws_ref[gi] += jnp.where(tril, _dot(dv2b, v1[rs], NT), 0.0)
                dv1_s[rs, :] = _dot(wst_ref[gi], dv2b, NN)
            dv1 = dv1_s[...]
            dlvg_ref[:, sl] += jnp.sum(dv1 * vh, axis=0, keepdims=True)
            dlvb_ref[:, sl] += jnp.sum(dv1, axis=0, keepdims=True)
            dz_ref[:, CB + gi * DB:CB + (gi + 1) * DB] = _ln_bwd(dv1, vh, vr, lvg).astype(BF16)

        @pl.when(last)
        def _():
            col = lax.broadcasted_iota(jnp.int32, (CHUNK, GB), 1)
            out = jnp.zeros((CHUNK, GB), F32)
            for gi in range(GB):
                s = jnp.sum(dbias_s[:, gi * DB:(gi + 1) * DB], axis=1, keepdims=True)
                out = out + jnp.where(col == gi, s, 0.0)
            dsb_ref[...] = out

    blk, _ = _mix_specs(ts, n_r, False)
    vec = pl.BlockSpec((1, CA), lambda b, r: (0, 0))
    mat = pl.BlockSpec((GB, CHUNK, CHUNK), lambda b, r: (0, 0, 0))
    wide = pl.BlockSpec((ts, CA + CB), lambda b, r: (b * n_r + r, 0))
    return pl.pallas_call(
        body, name="mix_bwd_rows", grid=(n_seq, n_r),
        in_specs=[wide, blk(2), blk(3), blk(0), vec, vec, vec, vec, mat, mat,
                  pl.BlockSpec((CHUNK, CB), lambda b, r: (0, 0))],
        out_specs=[blk(0), wide, vec, vec, vec, vec, mat, pl.BlockSpec((CHUNK, GB), lambda b, r: (0, 0))],
        out_shape=[jax.ShapeDtypeStruct((T, CA), F32), jax.ShapeDtypeStruct((T, 2 * CB), BF16)]
        + [jax.ShapeDtypeStruct((1, CA), F32)] * 4
        + [jax.ShapeDtypeStruct((GB, CHUNK, CHUNK), F32), jax.ShapeDtypeStruct((CHUNK, GB), F32)],
        scratch_shapes=[pltpu.VMEM((ts, DB), F32), pltpu.VMEM((CHUNK, CB), F32)],
        compiler_params=_cp("arbitrary", "arbitrary"),
    )(dcat, z, z, a1, ln_a_g, ln_a_b, ln_v_g, ln_v_b, ws, ws_t, bias2d)


def _mix_bwd_conv(da1, z, conv_w, n_seq):
    T = z.shape[0]
    S = T // n_seq
    ts = _tile(S, 512)
    n_r = S // ts
    per = ts // HALO
    shift = HALO - (CONV_WIDTH - 1)
    fold = CONV_ROWS // 8

    def body(d_ref, dh_ref, av_ref, ag_ref, avh_ref, agh_ref, cw_ref,
             dz_ref, dcw_ref, dcb_ref, a0_s, d1_s, da0_s, dw8_s):
        first = (pl.program_id(0) == 0) & (pl.program_id(1) == 0)
        last = (pl.program_id(0) == n_seq - 1) & (pl.program_id(1) == n_r - 1)

        @pl.when(first)
        def _():
            dw8_s[...] = jnp.zeros_like(dw8_s)
            dcb_ref[...] = jnp.zeros_like(dcb_ref)

        _glu_with_halo(av_ref, ag_ref, avh_ref, agh_ref, a0_s, pl.program_id(1) == 0, ts)
        d1_s[0:ts, :] = d_ref[...]
        d1_s[ts:ts + HALO, :] = jnp.where(pl.program_id(1) == n_r - 1, 0.0, dh_ref[...])
        dcb_ref[...] += jnp.sum(d_ref[...], axis=0, keepdims=True)
        for rb in range(ts // CONV_ROWS):
            base = rb * CONV_ROWS
            dcur = d1_s[base:base + CONV_ROWS, :]
            acc = jnp.zeros((CONV_ROWS, CA), F32)
            for k in range(CONV_WIDTH):
                back = CONV_WIDTH - 1 - k
                acc = acc + cw_ref[k:k + 1, :] * d1_s[base + back:base + back + CONV_ROWS, :]
                prod = dcur * a0_s[base + shift + k:base + shift + k + CONV_ROWS, :]
                part = prod[0:8]
                for f in range(1, fold):
                    part = part + prod[8 * f:8 * f + 8]
                dw8_s[k] += part
            da0_s[base:base + CONV_ROWS, :] = acc
        da0 = da0_s[...]
        sig = _sigmoid(ag_ref[...])
        dz_ref[:, 0:CA] = (da0 * sig).astype(BF16)
        dz_ref[:, CA:2 * CA] = (da0 * av_ref[...] * sig * (1.0 - sig)).astype(BF16)

        @pl.when(last)
        def _():
            for k in range(CONV_WIDTH):
                dcw_ref[k:k + 1, :] = jnp.sum(dw8_s[k], axis=0, keepdims=True)

    blk, left = _mix_specs(ts, n_r, True)
    n_halo_blocks = T // HALO
    right = pl.BlockSpec((HALO, CA), lambda b, r: (jnp.minimum((b * n_r + r + 1) * per, n_halo_blocks - 1), 0))
    return pl.pallas_call(
        body, name="mix_bwd_conv", grid=(n_seq, n_r),
        in_specs=[blk(0), right, blk(0), blk(1), left(0), left(1),
                  pl.BlockSpec((CONV_WIDTH, CA), lambda b, r: (0, 0))],
        out_specs=[pl.BlockSpec((ts, 2 * CA), lambda b, r: (b * n_r + r, 0)),
                   pl.BlockSpec((CONV_WIDTH, CA), lambda b, r: (0, 0)), pl.BlockSpec((1, CA), lambda b, r: (0, 0))],
        out_shape=[jax.ShapeDtypeStruct((T, 2 * CA), BF16), jax.ShapeDtypeStruct((CONV_WIDTH, CA), F32),
                   jax.ShapeDtypeStruct((1, CA), F32)],
        scratch_shapes=[pltpu.VMEM((HALO + ts, CA), F32), pltpu.VMEM((ts + HALO, CA), F32),
                        pltpu.VMEM((ts, CA), F32), pltpu.VMEM((CONV_WIDTH, 8, CA), F32)],
        compiler_params=_cp("arbitrary", "arbitrary"),
    )(da1, da1, z, z, z, z, conv_w)


def _sum_parts(parts, name):
    n, R, C = parts.shape
    tr = _tile(R, 512) if R % 8 == 0 and R > 512 else R
    if R % tr:
        tr = R

    def body(p_ref, o_ref):
        acc = p_ref[0]
        for k in range(1, n):
            acc = acc + p_ref[k]
        o_ref[...] = acc

    return pl.pallas_call(
        body, name=name, grid=(R // tr,),
        in_specs=[pl.BlockSpec((n, tr, C), lambda i: (0, i, 0))],
        out_specs=pl.BlockSpec((tr, C), lambda i: (i, 0)),
        out_shape=jax.ShapeDtypeStruct((R, C), F32),
        compiler_params=_cp("parallel"),
    )(parts)


def _row_tile(R, want):
    t = min(R, want)
    t -= t % 8
    while t > 8 and R % t:
        t -= 8
    return t if t >= 8 and R % t == 0 else R


def _adamw(w, g, m, v, name):
    R, C = w.shape
    tr = _row_tile(R, 256)
    c1 = 1.0 - ADAM_B1 ** ADAM_STEP
    c2 = 1.0 - ADAM_B2 ** ADAM_STEP

    def body(w_ref, g_ref, m_ref, v_ref, d_ref, nm_ref, nv_ref):
        gv = g_ref[...]
        nm = ADAM_B1 * m_ref[...] + (1.0 - ADAM_B1) * gv
        nv = ADAM_B2 * v_ref[...] + (1.0 - ADAM_B2) * (gv * gv)
        d_ref[...] = -ADAM_LR * ((nm / c1) / (jnp.sqrt(nv / c2) + ADAM_EPS) + ADAM_WD * w_ref[...])
        nm_ref[...] = nm
        nv_ref[...] = nv

    blk = pl.BlockSpec((tr, C), lambda i: (i, 0))
    return pl.pallas_call(
        body, name=name, grid=(R // tr,),
        in_specs=[blk] * 4, out_specs=[blk] * 3,
        out_shape=[jax.ShapeDtypeStruct((R, C), F32)] * 3,
        compiler_params=_cp("parallel"),
    )(w, g, m, v)


def _me():
    return lax.axis_index("x"), lax.axis_index("y"), lax.axis_index("c")


def _block_rows(ref, dev, n):
    start = (4 * dev[0] + 2 * dev[1] + dev[2]) * n
    if len(ref.shape) == 2:
        return ref.at[pl.ds(start, n), :]
    return ref.at[:, pl.ds(start, n), :]


def _all_gather(shards):
    na = len(shards)
    ns = [s.shape[-2] for s in shards]

    def body(*refs):
        ins, outs = refs[:na], refs[na:2 * na]
        send_sems, recv_sems, local_sems = refs[2 * na:]
        x, y, c = _me()
        me, sibling = (x, y, c), (x, y, 1 - c)
        chips = [(1 - x, y), (x, 1 - y), (1 - x, 1 - y)]

        def copy(a, k, block, to, src=None):
            dst = _block_rows(outs[a], block, ns[a])
            return pltpu.make_async_remote_copy(
                src_ref=dst if src is None else src, dst_ref=dst,
                send_sem=send_sems.at[a, k], recv_sem=recv_sems.at[a, k], device_id=to, device_id_type=MESH)

        mine = [pltpu.make_async_copy(ins[a], _block_rows(outs[a], me, ns[a]), local_sems.at[a]) for a in range(na)]
        for cp in mine:
            cp.start()
        first = []
        for a in range(na):
            first.append(copy(a, 0, me, sibling, src=ins[a]))
            first += [copy(a, 1 + j, me, (*chip, c), src=ins[a]) for j, chip in enumerate(chips)]
        for cp in first:
            cp.start()
        passed = []
        for j, chip in enumerate(chips):
            for a in range(na):
                copy(a, 1 + j, (*chip, c), me).wait_recv()
                fwd = copy(a, 4 + j, (*chip, c), sibling)
                fwd.start()
                passed.append(fwd)
        for a in range(na):
            copy(a, 0, sibling, me).wait_recv()
            for j, chip in enumerate(chips):
                copy(a, 4 + j, (*chip, 1 - c), me).wait_recv()
        for cp in first + passed:
            cp.wait_send()
        for cp in mine:
            cp.wait()

    out_shape = [jax.ShapeDtypeStruct(s.shape[:-2] + (N_DEV * s.shape[-2], s.shape[-1]), s.dtype) for s in shards]
    return pl.pallas_call(
        body, name="weights_all_gather",
        in_specs=[ANY] * na, out_specs=[ANY] * na, out_shape=out_shape,
        scratch_shapes=[pltpu.SemaphoreType.DMA((na, 7)), pltpu.SemaphoreType.DMA((na, 7)),
                        pltpu.SemaphoreType.DMA((na,))],
    )(*shards)


def _split_copies(gather, srcs, lands, send_sems, recv_sems, ns):
    x, y, c = _me()
    me = (x, y, c)
    my_slot = 4 * x + 2 * y + c
    copies = []
    for mask in range(1, N_DEV):
        peer = (x ^ (mask >> 2), y ^ ((mask >> 1) & 1), c ^ (mask & 1))
        for a in range(len(srcs)):
            if gather:
                src, dst = srcs[a], _block_rows(lands[a], me, ns[a])
            else:
                src, dst = _block_rows(srcs[a], peer, ns[a]), lands[a].at[my_slot]
            sem = a * (N_DEV - 1) + mask - 1
            copies.append(pltpu.make_async_remote_copy(
                src_ref=src, dst_ref=dst, send_sem=send_sems.at[sem], recv_sem=recv_sems.at[sem],
                device_id=peer, device_id_type=MESH))
    return copies


HBM_SPEC = pl.BlockSpec(memory_space=pltpu.HBM)
SEM_SPEC = pl.BlockSpec(memory_space=pltpu.SEMAPHORE)


def _split_start(gather, srcs, name, dep=None):
    na = len(srcs)
    if gather:
        ns = [s.shape[-2] for s in srcs]
        lands = [lax.empty(s.shape[:-2] + (N_DEV * s.shape[-2], s.shape[-1]), s.dtype) for s in srcs]
    else:
        ns = [s.shape[-2] // N_DEV for s in srcs]
        lands = [lax.empty((N_DEV, s.shape[-2] // N_DEV, s.shape[-1]), s.dtype) for s in srcs]
    n_in = 2 * na + (dep is not None)

    def body(*refs):
        send_sems, recv_sems = refs[n_in], refs[n_in + 1]
        for cp in _split_copies(gather, refs[:na], refs[na:2 * na], send_sems, recv_sems, ns):
            cp.start()
        refs[-1][...] = jnp.zeros_like(refs[-1])

    hbm = lambda a: pltpu.with_memory_space_constraint(a, pltpu.HBM)
    args = [hbm(a) for a in srcs] + [hbm(a) for a in lands] + ([dep] if dep is not None else [])
    out = pl.pallas_call(
        body, name=name,
        in_specs=[HBM_SPEC] * (2 * na) + ([ANY] if dep is not None else []),
        out_specs=[SEM_SPEC, SEM_SPEC] + [HBM_SPEC] * (2 * na) + [pl.BlockSpec(memory_space=pltpu.VMEM)],
        out_shape=[pltpu.SemaphoreType.DMA((na * (N_DEV - 1),)), pltpu.SemaphoreType.DMA((na * (N_DEV - 1),))]
        + [pltpu.HBM(a.shape, a.dtype) for a in srcs + lands] + [jax.ShapeDtypeStruct((8, LANES), F32)],
        input_output_aliases={i: 2 + i for i in range(2 * na)},
        compiler_params=pltpu.CompilerParams(has_side_effects=pltpu.SideEffectType.DATAFLOW_SIDE_EFFECTING),
    )(*args)
    return (gather, ns, out[0], out[1], list(out[2:2 + na]), list(out[2 + na:2 + 2 * na])), out[-1]


def _split_wait(handle, after, name):
    gather, ns, send, recv, srcs, lands = handle
    na = len(srcs)

    def body(*refs):
        send_sems, recv_sems = refs[2 * na], refs[2 * na + 1]
        for cp in _split_copies(gather, refs[:na], refs[na:2 * na], send_sems, recv_sems, ns):
            cp.wait_send()
            cp.wait_recv()

    out = pl.pallas_call(
        body, name=name,
        in_specs=[HBM_SPEC] * (2 * na) + [SEM_SPEC, SEM_SPEC, ANY],
        out_specs=[HBM_SPEC] * (2 * na),
        out_shape=[pltpu.HBM(a.shape, a.dtype) for a in srcs + lands],
        input_output_aliases={i: i for i in range(2 * na)},
        compiler_params=pltpu.CompilerParams(has_side_effects=pltpu.SideEffectType.DATAFLOW_SIDE_EFFECTING),
    )(*srcs, *lands, send, recv, after)
    return list(out[:na]), list(out[na:])


def _small_all_reduce(buf):
    R = buf.shape[0]

    def body(b_ref, o_ref, recv_ref, send_sems, recv_sems):
        x, y, c = _me()
        my_slot = 4 * x + 2 * y + c
        recv_ref[my_slot] = b_ref[...]
        copies = []
        for mask in range(1, N_DEV):
            peer = (x ^ (mask >> 2), y ^ ((mask >> 1) & 1), c ^ (mask & 1))
            copies.append(pltpu.make_async_remote_copy(
                src_ref=b_ref, dst_ref=recv_ref.at[my_slot],
                send_sem=send_sems.at[mask - 1], recv_sem=recv_sems.at[mask - 1],
                device_id=peer, device_id_type=MESH))
        for cp in copies:
            cp.start()
        for cp in copies:
            cp.wait()
        acc = recv_ref[0]
        for k in range(1, N_DEV):
            acc = acc + recv_ref[k]
        o_ref[...] = acc

    return pl.pallas_call(
        body, name="small_all_reduce",
        in_specs=[pl.BlockSpec(memory_space=pltpu.VMEM)], out_specs=pl.BlockSpec(memory_space=pltpu.VMEM),
        out_shape=jax.ShapeDtypeStruct((R, LANES), F32),
        scratch_shapes=[pltpu.VMEM((N_DEV, R, LANES), F32), pltpu.SemaphoreType.DMA((7,)),
                        pltpu.SemaphoreType.DMA((7,))],
        compiler_params=pltpu.CompilerParams(vmem_limit_bytes=VMEM_LIMIT),
    )(buf)


def _pack(arrays):
    flat = jnp.concatenate([a.reshape(-1) for a in arrays])
    pad = (-flat.shape[0]) % (8 * LANES)
    return jnp.pad(flat, (0, pad)).reshape(-1, LANES)


def _unpack(buf, shapes):
    flat = buf.reshape(-1)
    out, off = [], 0
    for s in shapes:
        n = 1
        for d in s:
            n *= d
        out.append(flat[off:off + n].reshape(s))
        off += n
    return out


def _ffn_index(layer, second):
    return (2 * layer + second) * 3


def kernel(x, g_ffn1, w_ffn1_gate, w_ffn1_up, w_ffn1_down, g_mix, w_in_ab, conv_w, conv_b, ln_a_g, ln_a_b, ln_v_g, ln_v_b, sp_w, sp_b, w_out_ab, w_qkv, w_o, g_ffn2, w_ffn2_gate, w_ffn2_up, w_ffn2_down, g_final, loss_target, m_g_ffn1, m_w_ffn1_gate, m_w_ffn1_up, m_w_ffn1_down, m_g_mix, m_w_in_ab, m_conv_w, m_conv_b, m_ln_a_g, m_ln_a_b, m_ln_v_g, m_ln_v_b, m_sp_w, m_sp_b, m_w_out_ab, m_w_qkv, m_w_o, m_g_ffn2, m_w_ffn2_gate, m_w_ffn2_up, m_w_ffn2_down, m_g_final, v_g_ffn1, v_w_ffn1_gate, v_w_ffn1_up, v_w_ffn1_down, v_g_mix, v_w_in_ab, v_conv_w, v_conv_b, v_ln_a_g, v_ln_a_b, v_ln_v_g, v_ln_v_b, v_sp_w, v_sp_b, v_w_out_ab, v_w_qkv, v_w_o, v_g_ffn2, v_w_ffn2_gate, v_w_ffn2_up, v_w_ffn2_down, v_g_final):
    n_seq, S, D = x.shape
    T = n_seq * S
    depth = g_ffn1.shape[0]
    assert depth == 2 and D == D_MODEL
    my_block = 4 * lax.axis_index("x") + 2 * lax.axis_index("y") + lax.axis_index("c")

    ffn_parts = []
    for l in range(depth):
        for gate, up, down in ((w_ffn1_gate, w_ffn1_up, w_ffn1_down), (w_ffn2_gate, w_ffn2_up, w_ffn2_down)):
            ffn_parts += [gate[l].T, up[l].T, down[l]]
    ffn_shard = lambda k: jnp.stack(ffn_parts[3 * k:3 * k + 3]).astype(BF16)
    conv_w_pad = jnp.zeros((HALO, conv_w.shape[2]), F32).at[:CONV_WIDTH].set(conv_w[0]).T
    w_ffn = [None] * (2 * depth)
    w_ffn[0], conv_w_t = _all_gather([ffn_shard(0), conv_w_pad])
    conv_w_full = conv_w_t.T[:CONV_WIDTH]
    shards_a = [w_in_ab[0].T.astype(BF16), w_out_ab[0].astype(BF16), ffn_shard(1)]
    shards_b = [ffn_shard(2), w_qkv[0].T.astype(BF16), w_o[0].astype(BF16), ffn_shard(3)]
    gather_a, token_a = _split_start(True, shards_a, "gather_a_start", dep=conv_w_t)
    gather_b, token = _split_start(True, shards_b, "gather_b_start", dep=token_a)

    def gathered(handle, after, name):
        shards, lands = _split_wait(handle, after, name)
        out = []
        for land, shard in zip(lands, shards):
            n = shard.shape[-2]
            start = (0,) * (land.ndim - 2) + (my_block * n, 0)
            out.append(lax.dynamic_update_slice(land, shard, start))
        return out

    row = lambda a: a.reshape(1, -1)
    tril = jnp.tril(jnp.ones((CHUNK, CHUNK), dtype=bool))
    ws = jnp.where(tril[None], sp_w[0], 0.0).astype(BF16)
    ws_t = jnp.swapaxes(ws, 1, 2)
    bias2d = jnp.repeat(sp_b[0].T, DB, axis=1)
    conv_b2, lag, lab = row(conv_b[0]), row(ln_a_g[0]), row(ln_a_b[0])
    lvg, lvb = row(ln_v_g[0]), row(ln_v_b[0])

    x0 = x.reshape(T, D)
    target = loss_target.reshape(T, D)
    saved = []
    xc = x0
    for l in range(depth):
        xa, a1, b1, h1 = _ffn_fwd(xc, row(g_ffn1[l]), w_ffn[2 * l], 0, f"ffn1_fwd_{l}", dep=token)
        if l % 2 == 0:
            w_in_t, w_out, w_ffn[1] = gathered(gather_a, xa, "gather_a_wait")
            z, hm = _mm_nt(xa, w_in_t, row(g_mix[l]), F32, "mix_in_proj")
            cat, conv_out = _mix_fwd(z, conv_w_full, conv_b2, lag, lab, lvg, lvb, ws, bias2d, n_seq)
            xb = _mm_nn_res(cat, w_out, xa, "mix_out_proj")
            mixer = (z, hm, cat, conv_out)
        else:
            qkv, hm = _mm_nt(xa, w_qkv_t, row(g_mix[l]), BF16, "qkv_proj")
            o, tot = _attn_fwd(qkv, n_seq)
            xb = _mm_nn_res(o, w_o_full, xa, "attn_out_proj")
            mixer = (qkv, hm, o, tot)
        xn, a2, b2, h2 = _ffn_fwd(xb, row(g_ffn2[l]), w_ffn[2 * l + 1], 0, f"ffn2_fwd_{l}")
        saved.append((xc, a1, b1, h1, xa, mixer, xb, a2, b2, h2))
        xc = xn
        if l == 0:
            w_ffn[2], w_qkv_t, w_o_full, w_ffn[3] = gathered(gather_b, xc, "gather_b_wait")

    g, loss_part, dg_final = _loss_head(xc, row(g_final), target)

    dg_ffn1, dg_ffn2, dg_mix = [None] * depth, [None] * depth, [None] * depth
    exchanges = {}
    token = None

    def ffn_back(g, xin, gvec, a, b, h, k, tag, token):
        g, dg, da, db, s, gh = _ffn_bwd(g, xin, gvec, a, b, w_ffn[k], 0, f"ffn{tag}_bwd", dep=token)
        if k == 0:
            return g, dg, (da, db, s, gh, h)
        dws = [_mm_tn(da, h, f"dw_gate{tag}"), _mm_tn(db, h, f"dw_up{tag}"), _mm_tn(s, gh, f"dw_down{tag}")]
        exchanges[f"ffn{k}"], token = _split_start(False, dws, f"exchange_ffn{tag}_start")
        return g, dg, token

    for l in reversed(range(depth)):
        xin, a1, b1, h1, xa, mixer, xb, a2, b2, h2 = saved[l]
        g, dg_ffn2[l], token = ffn_back(g, xb, row(g_ffn2[l]), a2, b2, h2, 2 * l + 1, f"2_{l}", token)
        if l % 2 == 0:
            z, hm, cat, conv_out = mixer
            dcat = _mm_nt(g, w_out, None, F32, "mix_out_bwd", dep=token)
            d_w_out = _mm_tn(cat, g, "dw_out")
            (da1, dz_uv, d_lag, d_lab, d_lvg, d_lvb, d_ws, d_sb) = _mix_bwd_rows(
                dcat, z, conv_out, lag, lab, lvg, lvb, ws, ws_t, bias2d, n_seq)
            dz_a, d_cw, d_cb = _mix_bwd_conv(da1, z, conv_w_full, n_seq)
            d_w_in_t = jnp.concatenate([_mm_tn(dz_a, hm, "dw_in_a"), _mm_tn(dz_uv, hm, "dw_in_uv")])
            exchanges["mix"], token = _split_start(False, [d_w_out, d_w_in_t], "exchange_mix_start")
            g, dg_mix[l] = _mm_nn_rmsbwd([dz_a, dz_uv], w_in_t, xa, row(g_mix[l]), g, "mix_in_bwd", dep=token)
        else:
            qkv, hm, o, tot = mixer
            do = _mm_nt(g, w_o_full, None, BF16, "attn_out_bwd", dep=token)
            d_w_o = _mm_tn(o, g, "dw_o")
            dq, dk, dv = _attn_bwd(qkv, do, tot, n_seq)
            d_w_qkv_t = jnp.concatenate([_mm_tn(dq, hm, "dw_q"), _mm_tn(dk, hm, "dw_k"), _mm_tn(dv, hm, "dw_v")])
            exchanges["attn"], token = _split_start(False, [d_w_o, d_w_qkv_t], "exchange_attn_start")
            g, dg_mix[l] = _mm_nn_rmsbwd([dq, dk, dv], w_qkv_t, xa, row(g_mix[l]), g, "qkv_bwd", dep=token)
        g, dg_ffn1[l], token = ffn_back(g, xin, row(g_ffn1[l]), a1, b1, h1, 2 * l, f"1_{l}", token)
    grad_x = g.reshape(n_seq, S, D)

    small = [jnp.concatenate(dg_ffn1), jnp.concatenate(dg_mix), d_cw, d_cb, d_lag, d_lab, d_lvg, d_lvb,
             jnp.where(tril[None], d_ws, 0.0), d_sb.T, jnp.concatenate(dg_ffn2), dg_final, loss_part[:, :1]]
    small_shapes = [(depth, D), (depth, D), (CONV_WIDTH, CA), (1, CA), (1, CA), (1, CA), (1, GB, DB), (1, GB, DB),
                    (1, GB, CHUNK, CHUNK), (1, GB, CHUNK), (depth, D), (D,), ()]
    small_sum = _small_all_reduce(_pack(small))
    red = _unpack(small_sum, small_shapes)
    (gr_g_ffn1, gr_g_mix, gr_cw_full, gr_cb, gr_lag, gr_lab, gr_lvg, gr_lvb, gr_sp_w, gr_sp_b,
     gr_g_ffn2, gr_g_final, loss) = red
    n_cw = conv_w.shape[2]
    gr_cw = lax.dynamic_slice(gr_cw_full, (0, my_block * n_cw), (CONV_WIDTH, n_cw))[None]

    da, db, s, gh, h = token
    token = small_sum
    for which, lhs, rhs in ((2, s, gh), (1, db, h), (0, da, h)):
        dw = _mm_tn(lhs, rhs, f"dw_ffn0_{which}", dep=token)
        exchanges[f"ffn0_{which}"], token = _split_start(False, [dw], f"exchange_ffn0_{which}_start")

    def reduced(key, after):
        out = []
        fulls, lands = _split_wait(exchanges[key], after, f"exchange_{key}_wait")
        for i, (land, full) in enumerate(zip(lands, fulls)):
            n = land.shape[1]
            own = lax.dynamic_slice(full, (my_block * n, 0), (n, full.shape[1]))
            parts = lax.dynamic_update_slice(land, own[None], (my_block, 0, 0))
            out.append(_sum_parts(parts, f"sum_{key}_{i}"))
        return out

    g_ffn_t = [None] * (6 * depth)
    for k in range(1, 2 * depth):
        g_ffn_t[3 * k:3 * k + 3] = reduced(f"ffn{k}", token)
    g_out, g_in = reduced("mix", token)
    g_o, g_qkv = reduced("attn", token)
    g_in, g_out, g_qkv, g_o = g_in.T[None], g_out[None], g_qkv.T[None], g_o[None]

    def ffn_grad(second, which):
        return jnp.stack([g_ffn_t[_ffn_index(l, second) + which].T if which < 2
                          else g_ffn_t[_ffn_index(l, second) + which] for l in range(depth)])

    grads = {
        "g_ffn1": gr_g_ffn1, "g_mix": gr_g_mix, "w_in_ab": g_in, "conv_w": gr_cw, "conv_b": gr_cb, "ln_a_g": gr_lag,
        "ln_a_b": gr_lab, "ln_v_g": gr_lvg, "ln_v_b": gr_lvb, "sp_w": gr_sp_w, "sp_b": gr_sp_b, "w_out_ab": g_out,
        "w_qkv": g_qkv, "w_o": g_o, "g_ffn2": gr_g_ffn2, "w_ffn2_gate": ffn_grad(1, 0), "w_ffn2_up": ffn_grad(1, 1),
        "w_ffn2_down": ffn_grad(1, 2), "g_final": gr_g_final,
    }
    weights = dict(g_ffn1=g_ffn1, w_ffn1_gate=w_ffn1_gate, w_ffn1_up=w_ffn1_up, w_ffn1_down=w_ffn1_down, g_mix=g_mix,
                   w_in_ab=w_in_ab, conv_w=conv_w, conv_b=conv_b, ln_a_g=ln_a_g, ln_a_b=ln_a_b, ln_v_g=ln_v_g,
                   ln_v_b=ln_v_b, sp_w=sp_w, sp_b=sp_b, w_out_ab=w_out_ab, w_qkv=w_qkv, w_o=w_o, g_ffn2=g_ffn2,
                   w_ffn2_gate=w_ffn2_gate, w_ffn2_up=w_ffn2_up, w_ffn2_down=w_ffn2_down, g_final=g_final)
    m_in = dict(g_ffn1=m_g_ffn1, w_ffn1_gate=m_w_ffn1_gate, w_ffn1_up=m_w_ffn1_up, w_ffn1_down=m_w_ffn1_down,
                g_mix=m_g_mix, w_in_ab=m_w_in_ab, conv_w=m_conv_w, conv_b=m_conv_b, ln_a_g=m_ln_a_g, ln_a_b=m_ln_a_b,
                ln_v_g=m_ln_v_g, ln_v_b=m_ln_v_b, sp_w=m_sp_w, sp_b=m_sp_b, w_out_ab=m_w_out_ab, w_qkv=m_w_qkv,
                w_o=m_w_o, g_ffn2=m_g_ffn2, w_ffn2_gate=m_w_ffn2_gate, w_ffn2_up=m_w_ffn2_up,
                w_ffn2_down=m_w_ffn2_down, g_final=m_g_final)
    v_in = dict(g_ffn1=v_g_ffn1, w_ffn1_gate=v_w_ffn1_gate, w_ffn1_up=v_w_ffn1_up, w_ffn1_down=v_w_ffn1_down,
                g_mix=v_g_mix, w_in_ab=v_w_in_ab, conv_w=v_conv_w, conv_b=v_conv_b, ln_a_g=v_ln_a_g, ln_a_b=v_ln_a_b,
                ln_v_g=v_ln_v_g, ln_v_b=v_ln_v_b, sp_w=v_sp_w, sp_b=v_sp_b, w_out_ab=v_w_out_ab, w_qkv=v_w_qkv,
                w_o=v_w_o, g_ffn2=v_g_ffn2, w_ffn2_gate=v_w_ffn2_gate, w_ffn2_up=v_w_ffn2_up,
                w_ffn2_down=v_w_ffn2_down, g_final=v_g_final)
    names = list(weights)
    grads = {n: grads[n].reshape(weights[n].shape) for n in grads}

    delta, new_m, new_v = {}, {}, {}

    def adamw_big(n):
        shp = weights[n].shape
        if shp[-1] == D:
            two, back = (lambda a: a.reshape(-1, D)), (lambda a: a.reshape(shp))
        else:
            two = lambda a: jnp.swapaxes(a, 1, 2).reshape(-1, D)
            back = lambda a: jnp.swapaxes(a.reshape(shp[0], shp[2], D), 1, 2)
        d, nm, nv = _adamw(two(weights[n]), two(grads[n]), two(m_in[n]), two(v_in[n]), f"adamw_{n}")
        delta[n], new_m[n], new_v[n] = back(d), back(nm), back(nv)

    big = [n for n in names if n.startswith("w_")]
    late = ["w_ffn1_gate", "w_ffn1_up", "w_ffn1_down"]
    for n in big:
        if n not in late:
            adamw_big(n)
    after = jnp.concatenate([delta[n].reshape(-1)[:1] for n in big if n not in late]).reshape(1, -1)
    for which in (2, 1, 0):
        g_ffn_t[which], = reduced(f"ffn0_{which}", after)
    for which, n in enumerate(late):
        grads[n] = ffn_grad(0, which).reshape(weights[n].shape)
        adamw_big(n)
    little = [n for n in names if n not in big]
    shapes = [weights[n].shape for n in little]
    d, nm, nv = _adamw(_pack([weights[n] for n in little]), _pack([grads[n] for n in little]),
                       _pack([m_in[n] for n in little]), _pack([v_in[n] for n in little]), "adamw_small")
    for n, dd, mm, vv in zip(little, _unpack(d, shapes), _unpack(nm, shapes), _unpack(nv, shapes)):
        delta[n], new_m[n], new_v[n] = dd, mm, vv

    return (loss, grad_x, *[grads[n] for n in names], *[delta[n] for n in names],
            *[new_m[n] for n in names], *[new_v[n] for n in names])
```

```python
import functools

import jax
import jax.numpy as jnp
from jax import lax
from jax.experimental import pallas as pl
from jax.experimental.pallas import tpu as pltpu

F32 = jnp.float32
BF16 = jnp.bfloat16

D_MODEL = 1024
CA = 512
CB = 512
GB = 4
DB = 128
CHUNK = 128
CONV_WIDTH = 31
N_HEADS = 16
HEAD_DIM = 64
EPS = 1e-6
N_DEV = 8
LANES = 128
QB = 128
ATT_TQ = 1024
FFN_TN = 1408
HALO = 32
CONV_ROWS = 32
ATT_SCALE = HEAD_DIM ** -0.5

ADAM_LR = 0.001
ADAM_B1 = 0.9
ADAM_B2 = 0.999
ADAM_EPS = 1e-08
ADAM_WD = 0.01
ADAM_STEP = 10

NT = (((1,), (1,)), ((), ()))
NN = (((1,), (0,)), ((), ()))
TN = (((0,), (0,)), ((), ()))
MESH = pl.DeviceIdType.MESH
ANY = pl.BlockSpec(memory_space=pl.ANY)
VMEM_LIMIT = 60 * 1024 * 1024


def _dot(a, b, dims):
    return lax.dot_general(a, b, dims, preferred_element_type=F32)


def _cp(*sem):
    return pltpu.CompilerParams(dimension_semantics=sem, vmem_limit_bytes=VMEM_LIMIT)


def _pcall(body, *, in_specs, args, dep=None, **kw):
    if dep is not None:
        n_in = len(in_specs)
        inner = body

        def body(*refs):
            inner(*refs[:n_in], *refs[n_in + 1:])

        in_specs = list(in_specs) + [ANY]
        args = tuple(args) + (dep,)
    return pl.pallas_call(body, in_specs=list(in_specs), **kw)(*args)


def _tile(n, want):
    if n <= want:
        return n
    t = want - want % LANES
    while t > LANES and n % t:
        t -= LANES
    assert n % t == 0, (n, want)
    return t


def _sigmoid(x):
    return 1.0 / (1.0 + jnp.exp(-x))


def _rstd(x):
    return lax.rsqrt(jnp.mean(x * x, axis=-1, keepdims=True) + EPS)


def _rms_bwd(x, g, dh):
    r = _rstd(x)
    u = dh * g
    dx = r * (u - x * (r * r) * jnp.mean(u * x, axis=-1, keepdims=True))
    dg = jnp.sum(dh * x * r, axis=0, keepdims=True)
    return dx, dg


def _ln_fwd(x, g, b):
    mu = jnp.mean(x, axis=-1, keepdims=True)
    xc = x - mu
    r = lax.rsqrt(jnp.mean(xc * xc, axis=-1, keepdims=True) + EPS)
    xh = xc * r
    return xh * g + b, xh, r


def _ln_bwd(dy, xh, r, g):
    dxh = dy * g
    return r * (dxh - jnp.mean(dxh, axis=-1, keepdims=True)
                - xh * jnp.mean(dxh * xh, axis=-1, keepdims=True))


def _ffn_fwd(x, g, wall, base, name, dep=None):
    T, D = x.shape
    F = wall.shape[1]
    tm, tn = _tile(T, 512), _tile(F, FFN_TN)
    n_j = F // tn

    def body(x_ref, g_ref, wg_ref, wu_ref, wd_ref, xo_ref, a_ref, b_ref, h_ref, acc_ref):
        j = pl.program_id(1)

        @pl.when(j == 0)
        def _():
            xv = x_ref[...]
            h_ref[...] = (xv * _rstd(xv) * g_ref[...]).astype(BF16)
            acc_ref[...] = jnp.zeros_like(acc_ref)

        h = h_ref[...]
        a = _dot(h, wg_ref[...], NT)
        b = _dot(h, wu_ref[...], NT)
        a_ref[...] = a.astype(BF16)
        b_ref[...] = b.astype(BF16)
        s = (a * _sigmoid(a) * b).astype(BF16)
        acc_ref[...] += _dot(s, wd_ref[...], NN)

        @pl.when(j == n_j - 1)
        def _():
            xo_ref[...] = x_ref[...] + 0.5 * acc_ref[...]

    wspec = lambda k: pl.BlockSpec((None, tn, D), lambda i, j: (base + k, j, 0))
    return _pcall(
        body, name=name, grid=(T // tm, n_j), dep=dep, args=(x, g, wall, wall, wall),
        in_specs=[pl.BlockSpec((tm, D), lambda i, j: (i, 0)), pl.BlockSpec((1, D), lambda i, j: (0, 0)),
                  wspec(0), wspec(1), wspec(2)],
        out_specs=[pl.BlockSpec((tm, D), lambda i, j: (i, 0)), pl.BlockSpec((tm, tn), lambda i, j: (i, j)),
                   pl.BlockSpec((tm, tn), lambda i, j: (i, j)), pl.BlockSpec((tm, D), lambda i, j: (i, 0))],
        out_shape=[jax.ShapeDtypeStruct((T, D), F32), jax.ShapeDtypeStruct((T, F), BF16),
                   jax.ShapeDtypeStruct((T, F), BF16), jax.ShapeDtypeStruct((T, D), BF16)],
        scratch_shapes=[pltpu.VMEM((tm, D), F32)],
        compiler_params=_cp("parallel", "arbitrary"),
    )


def _ffn_bwd(go, x, g, a, b, wall, base, name, dep=None):
    T, D = x.shape
    F = wall.shape[1]
    tm, tn = _tile(T, 512), _tile(F, FFN_TN)
    n_j = F // tn

    def body(go_ref, x_ref, g_ref, a_ref, b_ref, wg_ref, wu_ref, wd_ref,
             gx_ref, dg_ref, da_ref, db_ref, s_ref, gh_ref, acc_ref):
        i, j = pl.program_id(0), pl.program_id(1)

        @pl.when(j == 0)
        def _():
            gh_ref[...] = (0.5 * go_ref[...]).astype(BF16)
            acc_ref[...] = jnp.zeros_like(acc_ref)

        @pl.when((i == 0) & (j == 0))
        def _():
            dg_ref[...] = jnp.zeros_like(dg_ref)

        ds = _dot(gh_ref[...], wd_ref[...], NT)
        av = a_ref[...].astype(F32)
        bv = b_ref[...].astype(F32)
        sig = _sigmoid(av)
        sl = av * sig
        dab = (ds * bv * (sig * (1.0 + av * (1.0 - sig)))).astype(BF16)
        dbb = (ds * sl).astype(BF16)
        s_ref[...] = (sl * bv).astype(BF16)
        da_ref[...] = dab
        db_ref[...] = dbb
        acc_ref[...] += _dot(dab, wg_ref[...], NN) + _dot(dbb, wu_ref[...], NN)

        @pl.when(j == n_j - 1)
        def _():
            dx, dg = _rms_bwd(x_ref[...], g_ref[...], acc_ref[...])
            gx_ref[...] = go_ref[...] + dx
            dg_ref[...] += dg

    wspec = lambda k: pl.BlockSpec((None, tn, D), lambda i, j: (base + k, j, 0))
    row = pl.BlockSpec((tm, D), lambda i, j: (i, 0))
    hid = pl.BlockSpec((tm, tn), lambda i, j: (i, j))
    vec = pl.BlockSpec((1, D), lambda i, j: (0, 0))
    return _pcall(
        body, name=name, grid=(T // tm, n_j), dep=dep, args=(go, x, g, a, b, wall, wall, wall),
        in_specs=[row, row, vec, hid, hid, wspec(0), wspec(1), wspec(2)],
        out_specs=[row, vec, hid, hid, hid, row],
        out_shape=[jax.ShapeDtypeStruct((T, D), F32), jax.ShapeDtypeStruct((1, D), F32),
                   jax.ShapeDtypeStruct((T, F), BF16), jax.ShapeDtypeStruct((T, F), BF16),
                   jax.ShapeDtypeStruct((T, F), BF16), jax.ShapeDtypeStruct((T, D), BF16)],
        scratch_shapes=[pltpu.VMEM((tm, D), F32)],
        compiler_params=_cp("arbitrary", "arbitrary"),
    )


def _mm_tn(a, b, name, dep=None):
    T, M = a.shape
    N = b.shape[1]
    tmm, tk = _tile(M, 1536), _tile(T, 1024)

    def body(a_ref, b_ref, o_ref):
        @pl.when(pl.program_id(1) == 0)
        def _():
            o_ref[...] = jnp.zeros_like(o_ref)

        o_ref[...] += _dot(a_ref[...].astype(BF16), b_ref[...].astype(BF16), TN)

    return _pcall(
        body, name=name, grid=(M // tmm, T // tk), dep=dep, args=(a, b),
        in_specs=[pl.BlockSpec((tk, tmm), lambda m, k: (k, m)), pl.BlockSpec((tk, N), lambda m, k: (k, 0))],
        out_specs=pl.BlockSpec((tmm, N), lambda m, k: (m, 0)),
        out_shape=jax.ShapeDtypeStruct((M, N), F32),
        compiler_params=_cp("parallel", "arbitrary"),
    )


def _mm_nt(x, wt, g, out_dtype, name, dep=None):
    T, K = x.shape
    N = wt.shape[0]
    tm, tn = _tile(T, 512), _tile(N, 1536)
    norm = g is not None

    def body(*refs):
        if norm:
            x_ref, g_ref, w_ref, o_ref, h_ref = refs
        else:
            x_ref, w_ref, o_ref, h_ref = refs

        @pl.when(pl.program_id(1) == 0)
        def _():
            xv = x_ref[...].astype(F32)
            if norm:
                xv = xv * _rstd(xv) * g_ref[...]
            h_ref[...] = xv.astype(BF16)

        o_ref[...] = _dot(h_ref[...], w_ref[...], NT).astype(out_dtype)

    row = pl.BlockSpec((tm, K), lambda i, j: (i, 0))
    wsp = pl.BlockSpec((tn, K), lambda i, j: (j, 0))
    osp = pl.BlockSpec((tm, tn), lambda i, j: (i, j))
    if norm:
        return pl.pallas_call(
            body, name=name, grid=(T // tm, N // tn),
            in_specs=[row, pl.BlockSpec((1, K), lambda i, j: (0, 0)), wsp],
            out_specs=[osp, row],
            out_shape=[jax.ShapeDtypeStruct((T, N), out_dtype), jax.ShapeDtypeStruct((T, K), BF16)],
            compiler_params=_cp("parallel", "arbitrary"),
        )(x, g, wt)
    return _pcall(
        body, name=name, grid=(T // tm, N // tn), dep=dep, args=(x, wt),
        in_specs=[row, wsp], out_specs=osp,
        out_shape=jax.ShapeDtypeStruct((T, N), out_dtype),
        scratch_shapes=[pltpu.VMEM((tm, K), BF16)],
        compiler_params=_cp("parallel", "arbitrary"),
    )


def _mm_nn_res(act, w, resid, name):
    T, K = act.shape
    D = w.shape[1]
    tm = _tile(T, 512)

    def body(a_ref, w_ref, r_ref, o_ref):
        o_ref[...] = r_ref[...] + _dot(a_ref[...].astype(BF16), w_ref[...], NN)

    return pl.pallas_call(
        body, name=name, grid=(T // tm,),
        in_specs=[pl.BlockSpec((tm, K), lambda i: (i, 0)), pl.BlockSpec((K, D), lambda i: (0, 0)),
                  pl.BlockSpec((tm, D), lambda i: (i, 0))],
        out_specs=pl.BlockSpec((tm, D), lambda i: (i, 0)),
        out_shape=jax.ShapeDtypeStruct((T, D), F32),
        compiler_params=_cp("parallel"),
    )(act, w, resid)


def _mm_nn_rmsbwd(acts, w, x, g, gprev, name, dep=None):
    T = acts[0].shape[0]
    ks = [a.shape[1] for a in acts]
    K, D = w.shape
    assert sum(ks) == K
    tm = _tile(T, 512)
    na = len(acts)

    def body(*refs):
        a_refs = refs[:na]
        w_ref, x_ref, g_ref, gp_ref, o_ref, dg_ref = refs[na:]

        @pl.when(pl.program_id(0) == 0)
        def _():
            dg_ref[...] = jnp.zeros_like(dg_ref)

        dh, off = None, 0
        for a_ref, k in zip(a_refs, ks):
            part = _dot(a_ref[...].astype(BF16), w_ref[off:off + k, :], NN)
            dh = part if dh is None else dh + part
            off += k
        dx, dg = _rms_bwd(x_ref[...], g_ref[...], dh)
        o_ref[...] = gp_ref[...] + dx
        dg_ref[...] += dg

    row = pl.BlockSpec((tm, D), lambda i: (i, 0))
    vec = pl.BlockSpec((1, D), lambda i: (0, 0))
    return _pcall(
        body, name=name, grid=(T // tm,), dep=dep, args=(*acts, w, x, g, gprev),
        in_specs=[pl.BlockSpec((tm, k), lambda i: (i, 0)) for k in ks]
        + [pl.BlockSpec((K, D), lambda i: (0, 0)), row, vec, row],
        out_specs=[row, vec],
        out_shape=[jax.ShapeDtypeStruct((T, D), F32), jax.ShapeDtypeStruct((1, D), F32)],
        compiler_params=_cp("arbitrary"),
    )


def _loss_head(x, g, target):
    T, D = x.shape
    tm = _tile(T, 512)

    def body(x_ref, g_ref, t_ref, dx_ref, loss_ref, dg_ref):
        @pl.when(pl.program_id(0) == 0)
        def _():
            loss_ref[...] = jnp.zeros_like(loss_ref)
            dg_ref[...] = jnp.zeros_like(dg_ref)

        xv = x_ref[...]
        gv = g_ref[...]
        e = xv * _rstd(xv) * gv - t_ref[...]
        per_tok = jnp.sum(e * e, axis=-1, keepdims=True) * (1.0 / D)
        loss_ref[...] += 0.5 * jnp.sum(per_tok, axis=0, keepdims=True)
        dx, dg = _rms_bwd(xv, gv, e * (1.0 / D))
        dx_ref[...] = dx
        dg_ref[...] += dg

    row = pl.BlockSpec((tm, D), lambda i: (i, 0))
    vec = pl.BlockSpec((1, D), lambda i: (0, 0))
    return pl.pallas_call(
        body, name="loss_head", grid=(T // tm,),
        in_specs=[row, vec, row],
        out_specs=[row, pl.BlockSpec((1, LANES), lambda i: (0, 0)), vec],
        out_shape=[jax.ShapeDtypeStruct((T, D), F32), jax.ShapeDtypeStruct((1, LANES), F32),
                   jax.ShapeDtypeStruct((1, D), F32)],
        compiler_params=_cp("arbitrary"),
    )(x, g, target)


def _log_gates(z):
    ls = jnp.minimum(z, 0.0) - jnp.log(1.0 + jnp.exp(-jnp.abs(z)))
    return ls, ls - z


def _cumsum_mm(v, u2):
    hi = v.astype(BF16)
    lo = (v - hi.astype(F32)).astype(BF16)
    return _dot(jnp.concatenate([hi, lo], axis=1), u2, NN)


def _half_rowsum(v):
    n = v.shape[0]
    s0 = jnp.sum(v[:, :QB], axis=1, keepdims=True)
    s1 = jnp.sum(v[:, QB:], axis=1, keepdims=True)
    return jnp.concatenate([jnp.broadcast_to(s0, (n, QB)), jnp.broadcast_to(s1, (n, QB))], axis=1)


def _stack_heads(src_ref, dst_ref, n_blk):
    m0 = lax.broadcasted_iota(jnp.int32, (1, LANES), 1) < HEAD_DIM

    def fill(c, carry):
        blk = src_ref[pl.ds(pl.multiple_of(c * QB, QB), QB), :]
        zero = jnp.zeros_like(blk)
        dst_ref[c, 0:QB, :] = jnp.where(m0, blk, zero)
        dst_ref[c, QB:2 * QB, :] = jnp.where(m0, zero, blk)
        return carry

    lax.fori_loop(0, n_blk, fill, 0)


def _diag_mask(tq, j):
    n = tq - j * QB
    row = lax.broadcasted_iota(jnp.int32, (n, 2 * QB), 0)
    col = lax.broadcasted_iota(jnp.int32, (n, 2 * QB), 1)
    return (col & (QB - 1)) < row


def _tri_blockdiag(upper):
    r = lax.broadcasted_iota(jnp.int32, (2 * QB, 2 * QB), 0)
    c = lax.broadcasted_iota(jnp.int32, (2 * QB, 2 * QB), 1)
    same = (r // QB) == (c // QB)
    u = (same & ((r > c) if upper else (r < c))).astype(BF16)
    return jnp.concatenate([u, u], axis=0)


def _attn_tiles(T, n_seq):
    S = T // n_seq
    tq = ATT_TQ if S % ATT_TQ == 0 else QB
    return S, tq, tq // QB, S // tq, S // QB


def _attn_fwd(qkv, n_seq):
    T = qkv.shape[0]
    S, tq, r, n_q, n_k = _attn_tiles(T, n_seq)
    n_p = D_MODEL // LANES
    u_suffix = _tri_blockdiag(True)

    def body(q_ref, k_ref, v_ref, u_ref, o_ref, tot_ref, kk_ref, vv_ref, lr_s, acc_s):
        qi = pl.program_id(2)

        @pl.when(qi == 0)
        def _():
            _stack_heads(k_ref, kk_ref, n_k)
            _stack_heads(v_ref, vv_ref, n_k)

        u = u_ref[...]
        lr_s[...] = jnp.zeros_like(lr_s)
        acc_s[...] = jnp.zeros_like(acc_s)

        def step(kj, q, mask, lr, acc):
            ls, lk = _log_gates(_dot(q, kk_ref[kj], NT))
            if mask is not None:
                lk = jnp.where(mask, lk, 0.0)
            a = jnp.exp(ls + _cumsum_mm(lk, u) + lr)
            if mask is not None:
                a = jnp.where(mask, a, 0.0)
            return lr + _half_rowsum(lk), acc + _dot(a.astype(BF16), vv_ref[kj], NN)

        for j in reversed(range(r)):
            rows = slice(j * QB, tq)
            lr, acc = step(qi * r + j, q_ref[rows, :] * ATT_SCALE, _diag_mask(tq, j), lr_s[rows, :], acc_s[rows, :])
            lr_s[rows, :] = lr
            acc_s[rows, :] = acc

        q = q_ref[...] * ATT_SCALE

        def off(it, carry):
            lr, acc = lr_s[...], acc_s[...]
            for j in range(r):
                lr, acc = step((qi - it) * r - 1 - j, q, None, lr, acc)
            lr_s[...] = lr
            acc_s[...] = acc
            return carry

        lax.fori_loop(0, qi, off, 0)
        o_ref[...] = acc_s[...].astype(BF16)
        tot_ref[...] = lr_s[...]

    return pl.pallas_call(
        body, name="attn_fwd", grid=(n_seq, n_p, n_q),
        in_specs=[pl.BlockSpec((tq, LANES), lambda b, p, qi: (b * n_q + qi, p)),
                  pl.BlockSpec((S, LANES), lambda b, p, qi: (b, n_p + p)),
                  pl.BlockSpec((S, LANES), lambda b, p, qi: (b, 2 * n_p + p)),
                  pl.BlockSpec((4 * QB, 2 * QB), lambda b, p, qi: (0, 0))],
        out_specs=[pl.BlockSpec((tq, LANES), lambda b, p, qi: (b * n_q + qi, p)),
                   pl.BlockSpec((tq, 2 * QB), lambda b, p, qi: (b * n_q + qi, p))],
        out_shape=[jax.ShapeDtypeStruct((T, D_MODEL), BF16), jax.ShapeDtypeStruct((T, 2 * D_MODEL), F32)],
        scratch_shapes=[pltpu.VMEM((n_k, 2 * QB, LANES), BF16), pltpu.VMEM((n_k, 2 * QB, LANES), BF16),
                        pltpu.VMEM((tq, 2 * QB), F32), pltpu.VMEM((tq, LANES), F32)],
        compiler_params=_cp("parallel", "parallel", "arbitrary"),
    )(qkv, qkv, qkv, u_suffix)


def _attn_bwd(qkv, do, tot, n_seq):
    T = qkv.shape[0]
    S, tq, r, n_q, n_k = _attn_tiles(T, n_seq)
    n_p = D_MODEL // LANES
    u_prefix = _tri_blockdiag(False)

    def body(q_ref, k_ref, v_ref, do_ref, tot_ref, u_ref, dq_ref, dk_out, dv_out,
             kk_ref, vv_ref, cl_s, cg_s, dq_s, dk_ref, dv_ref):
        qi = pl.program_id(2)

        @pl.when(qi == 0)
        def _():
            _stack_heads(k_ref, kk_ref, n_k)
            _stack_heads(v_ref, vv_ref, n_k)
            dk_ref[...] = jnp.zeros_like(dk_ref)
            dv_ref[...] = jnp.zeros_like(dv_ref)

        u = u_ref[...]
        m0 = lax.broadcasted_iota(jnp.int32, (1, LANES), 1) < HEAD_DIM
        cl_s[...] = tot_ref[...]
        cg_s[...] = jnp.zeros_like(cg_s)
        dq_s[...] = jnp.zeros_like(dq_s)

        def step(kj, q, dov, mask, rest, cg, dq):
            kk = kk_ref[kj]
            ls, lk = _log_gates(_dot(q, kk, NT))
            if mask is not None:
                lk = jnp.where(mask, lk, 0.0)
            a = jnp.exp(ls + (rest - (_cumsum_mm(lk, u) + lk)))
            if mask is not None:
                a = jnp.where(mask, a, 0.0)
            g = a * _dot(dov, vv_ref[kj], NT)
            dz = g - (g + _dot(g.astype(BF16), u[:2 * QB], NN) + cg) * jnp.exp(ls)
            if mask is not None:
                dz = jnp.where(mask, dz, 0.0)
            dz = dz.astype(BF16)
            rows = pl.ds(pl.multiple_of(kj * QB, QB), QB)
            dvt = _dot(a.astype(BF16), dov, TN)
            dv_ref[rows, :] += jnp.where(m0, dvt[:QB], dvt[QB:])
            dkt = _dot(dz, q, TN)
            dk_ref[rows, :] += jnp.where(m0, dkt[:QB], dkt[QB:])
            return rest - _half_rowsum(lk), cg + _half_rowsum(g), dq + _dot(dz, kk, NN)

        q = q_ref[...] * ATT_SCALE
        dov = do_ref[...]

        def off(it, carry):
            cl, cg, dq = cl_s[...], cg_s[...], dq_s[...]
            for j in range(r):
                cl, cg, dq = step(it * r + j, q, dov, None, cl, cg, dq)
            cl_s[...] = cl
            cg_s[...] = cg
            dq_s[...] = dq
            return carry

        lax.fori_loop(0, qi, off, 0)

        for j in range(r):
            rows = slice(j * QB, tq)
            cl, cg, dq = step(qi * r + j, q_ref[rows, :] * ATT_SCALE, do_ref[rows, :],
                              _diag_mask(tq, j), cl_s[rows, :], cg_s[rows, :], dq_s[rows, :])
            cl_s[rows, :] = cl
            cg_s[rows, :] = cg
            dq_s[rows, :] = dq
        dq_ref[...] = (dq_s[...] * ATT_SCALE).astype(BF16)

        @pl.when(qi == n_q - 1)
        def _():
            dk_out[...] = dk_ref[...].astype(BF16)
            dv_out[...] = dv_ref[...].astype(BF16)

    qspec = pl.BlockSpec((tq, LANES), lambda b, p, qi: (b * n_q + qi, p))
    seq = lambda off: pl.BlockSpec((S, LANES), lambda b, p, qi: (b, off + p))
    return pl.pallas_call(
        body, name="attn_bwd", grid=(n_seq, n_p, n_q),
        in_specs=[qspec, seq(n_p), seq(2 * n_p), qspec,
                  pl.BlockSpec((tq, 2 * QB), lambda b, p, qi: (b * n_q + qi, p)),
                  pl.BlockSpec((4 * QB, 2 * QB), lambda b, p, qi: (0, 0))],
        out_specs=[qspec, seq(0), seq(0)],
        out_shape=[jax.ShapeDtypeStruct((T, D_MODEL), BF16)] * 3,
        scratch_shapes=[pltpu.VMEM((n_k, 2 * QB, LANES), BF16), pltpu.VMEM((n_k, 2 * QB, LANES), BF16),
                        pltpu.VMEM((tq, 2 * QB), F32), pltpu.VMEM((tq, 2 * QB), F32), pltpu.VMEM((tq, LANES), F32),
                        pltpu.VMEM((S, LANES), F32), pltpu.VMEM((S, LANES), F32)],
        compiler_params=_cp("parallel", "parallel", "arbitrary"),
    )(qkv, qkv, qkv, do, tot, u_prefix)


def _glu_with_halo(av_ref, ag_ref, avh_ref, agh_ref, a0_s, first, ts):
    hal = avh_ref[...] * _sigmoid(agh_ref[...])
    a0_s[0:HALO, :] = jnp.where(first, 0.0, hal)
    a0_s[HALO:HALO + ts, :] = av_ref[...] * _sigmoid(ag_ref[...])


def _mix_specs(ts, n_r, with_left):
    blk = lambda c: pl.BlockSpec((ts, CA), lambda b, r: (b * n_r + r, c))
    per = ts // HALO
    left = lambda c: pl.BlockSpec((HALO, CA), lambda b, r: (jnp.maximum((b * n_r + r) * per - 1, 0), c))
    return blk, (left if with_left else None)


def _mix_fwd(z, conv_w, conv_b, ln_a_g, ln_a_b, ln_v_g, ln_v_b, ws, bias2d, n_seq):
    T = z.shape[0]
    S = T // n_seq
    ts = _tile(S, 512)
    n_r = S // ts
    shift = HALO - (CONV_WIDTH - 1)

    def body(av_ref, ag_ref, avh_ref, agh_ref, u_ref, v_ref, cw_ref, cb_ref, lag_ref, lab_ref,
             lvg_ref, lvb_ref, ws_ref, bias_ref, cat_ref, a1_ref, a0_s):
        _glu_with_halo(av_ref, ag_ref, avh_ref, agh_ref, a0_s, pl.program_id(1) == 0, ts)
        for rb in range(ts // CONV_ROWS):
            base = rb * CONV_ROWS
            acc = jnp.broadcast_to(cb_ref[...], (CONV_ROWS, CA))
            for k in range(CONV_WIDTH):
                acc = acc + cw_ref[k:k + 1, :] * a0_s[base + shift + k:base + shift + k + CONV_ROWS, :]
            a1_ref[base:base + CONV_ROWS, :] = acc
        y, _, _ = _ln_fwd(a1_ref[...], lag_ref[...], lab_ref[...])
        cat_ref[:, 0:CA] = (y * _sigmoid(y)).astype(BF16)
        for gi in range(GB):
            sl = slice(gi * DB, (gi + 1) * DB)
            v1, _, _ = _ln_fwd(v_ref[:, sl], lvg_ref[:, sl], lvb_ref[:, sl])
            v1 = v1.astype(BF16)
            for c in range(ts // CHUNK):
                rs = slice(c * CHUNK, (c + 1) * CHUNK)
                v2 = _dot(ws_ref[gi], v1[rs], NN) + bias_ref[:, sl]
                cat_ref[rs, CA + gi * DB:CA + (gi + 1) * DB] = (u_ref[rs, sl] * v2).astype(BF16)

    blk, left = _mix_specs(ts, n_r, True)
    vec = pl.BlockSpec((1, CA), lambda b, r: (0, 0))
    return pl.pallas_call(
        body, name="mix_fwd", grid=(n_seq, n_r),
        in_specs=[blk(0), blk(1), left(0), left(1), blk(2), blk(3),
                  pl.BlockSpec((CONV_WIDTH, CA), lambda b, r: (0, 0)), vec, vec, vec, vec, vec,
                  pl.BlockSpec((GB, CHUNK, CHUNK), lambda b, r: (0, 0, 0)),
                  pl.BlockSpec((CHUNK, CB), lambda b, r: (0, 0))],
        out_specs=[pl.BlockSpec((ts, CA + CB), lambda b, r: (b * n_r + r, 0)), blk(0)],
        out_shape=[jax.ShapeDtypeStruct((T, CA + CB), BF16), jax.ShapeDtypeStruct((T, CA), F32)],
        scratch_shapes=[pltpu.VMEM((HALO + ts, CA), F32)],
        compiler_params=_cp("parallel", "parallel"),
    )(z, z, z, z, z, z, conv_w, conv_b, ln_a_g, ln_a_b, ln_v_g, ln_v_b, ws, bias2d)


def _mix_bwd_rows(dcat, z, a1, ln_a_g, ln_a_b, ln_v_g, ln_v_b, ws, ws_t, bias2d, n_seq):
    T = z.shape[0]
    S = T // n_seq
    ts = _tile(S, 512)
    n_r = S // ts

    def body(dc_ref, u_ref, v_ref, a1_ref, lag_ref, lab_ref, lvg_ref, lvb_ref, ws_ref, wst_ref, bias_ref,
             da1_ref, dz_ref, dlag_ref, dlab_ref, dlvg_ref, dlvb_ref, dws_ref, dsb_ref, dv1_s, dbias_s):
        first = (pl.program_id(0) == 0) & (pl.program_id(1) == 0)
        last = (pl.program_id(0) == n_seq - 1) & (pl.program_id(1) == n_r - 1)

        @pl.when(first)
        def _():
            for ref in (dlag_ref, dlab_ref, dlvg_ref, dlvb_ref, dws_ref, dbias_s):
                ref[...] = jnp.zeros_like(ref)

        lag = lag_ref[...]
        y, xh, r = _ln_fwd(a1_ref[...], lag, lab_ref[...])
        sig = _sigmoid(y)
        dy = dc_ref[:, 0:CA] * (sig * (1.0 + y * (1.0 - sig)))
        dlag_ref[...] += jnp.sum(dy * xh, axis=0, keepdims=True)
        dlab_ref[...] += jnp.sum(dy, axis=0, keepdims=True)
        da1_ref[...] = _ln_bwd(dy, xh, r, lag)

        tril = (lax.broadcasted_iota(jnp.int32, (CHUNK, CHUNK), 0)
                >= lax.broadcasted_iota(jnp.int32, (CHUNK, CHUNK), 1))
        for gi in range(GB):
            sl = slice(gi * DB, (gi + 1) * DB)
            lvg = lvg_ref[:, sl]
            v1, vh, vr = _ln_fwd(v_ref[:, sl], lvg, lvb_ref[:, sl])
            v1 = v1.astype(BF16)
            for c in range(ts // CHUNK):
                rs = slice(c * CHUNK, (c + 1) * CHUNK)
                v2 = _dot(ws_ref[gi], v1[rs], NN) + bias_ref[:, sl]
                dbo = dc_ref[rs, CA + gi * DB:CA + (gi + 1) * DB]
                dz_ref[rs, sl] = (dbo * v2).astype(BF16)
                dv2 = dbo * u_ref[rs, sl]
                dbias_s[:, sl] += dv2
                dv2b = dv2.astype(BF16)
                dws_ref[gi] += jnp.where(tril, _dot(dv2b, v1[rs], NT), 0.0)
                dv1_s[rs, :] = _dot(wst_ref[gi], dv2b, NN)
            dv1 = dv1_s[...]
            dlvg_ref[:, sl] += jnp.sum(dv1 * vh, axis=0, keepdims=True)
            dlvb_ref[:, sl] += jnp.sum(dv1, axis=0, keepdims=True)
            dz_ref[:, CB + gi * DB:CB + (gi + 1) * DB] = _ln_bwd(dv1, vh, vr, lvg).astype(BF16)

        @pl.when(last)
        def _():
            col = lax.broadcasted_iota(jnp.int32, (CHUNK, GB), 1)
            out = jnp.zeros((CHUNK, GB), F32)
            for gi in range(GB):
                s = jnp.sum(dbias_s[:, gi * DB:(gi + 1) * DB], axis=1, keepdims=True)
                out = out + jnp.where(col == gi, s, 0.0)
            dsb_ref[...] = out

    blk, _ = _mix_specs(ts, n_r, False)
    vec = pl.BlockSpec((1, CA), lambda b, r: (0, 0))
    mat = pl.BlockSpec((GB, CHUNK, CHUNK), lambda b, r: (0, 0, 0))
    wide = pl.BlockSpec((ts, CA + CB), lambda b, r: (b * n_r + r, 0))
    return pl.pallas_call(
        body, name="mix_bwd_rows", grid=(n_seq, n_r),
        in_specs=[wide, blk(2), blk(3), blk(0), vec, vec, vec, vec, mat, mat,
                  pl.BlockSpec((CHUNK, CB), lambda b, r: (0, 0))],
        out_specs=[blk(0), wide, vec, vec, vec, vec, mat, pl.BlockSpec((CHUNK, GB), lambda b, r: (0, 0))],
        out_shape=[jax.ShapeDtypeStruct((T, CA), F32), jax.ShapeDtypeStruct((T, 2 * CB), BF16)]
        + [jax.ShapeDtypeStruct((1, CA), F32)] * 4
        + [jax.ShapeDtypeStruct((GB, CHUNK, CHUNK), F32), jax.ShapeDtypeStruct((CHUNK, GB), F32)],
        scratch_shapes=[pltpu.VMEM((ts, DB), F32), pltpu.VMEM((CHUNK, CB), F32)],
        compiler_params=_cp("arbitrary", "arbitrary"),
    )(dcat, z, z, a1, ln_a_g, ln_a_b, ln_v_g, ln_v_b, ws, ws_t, bias2d)


def _mix_bwd_conv(da1, z, conv_w, n_seq):
    T = z.shape[0]
    S = T // n_seq
    ts = _tile(S, 512)
    n_r = S // ts
    per = ts // HALO
    shift = HALO - (CONV_WIDTH - 1)
    fold = CONV_ROWS // 8

    def body(d_ref, dh_ref, av_ref, ag_ref, avh_ref, agh_ref, cw_ref,
             dz_ref, dcw_ref, dcb_ref, a0_s, d1_s, da0_s, dw8_s):
        first = (pl.program_id(0) == 0) & (pl.program_id(1) == 0)
        last = (pl.program_id(0) == n_seq - 1) & (pl.program_id(1) == n_r - 1)

        @pl.when(first)
        def _():
            dw8_s[...] = jnp.zeros_like(dw8_s)
            dcb_ref[...] = jnp.zeros_like(dcb_ref)

        _glu_with_halo(av_ref, ag_ref, avh_ref, agh_ref, a0_s, pl.program_id(1) == 0, ts)
        d1_s[0:ts, :] = d_ref[...]
        d1_s[ts:ts + HALO, :] = jnp.where(pl.program_id(1) == n_r - 1, 0.0, dh_ref[...])
        dcb_ref[...] += jnp.sum(d_ref[...], axis=0, keepdims=True)
        for rb in range(ts // CONV_ROWS):
            base = rb * CONV_ROWS
            dcur = d1_s[base:base + CONV_ROWS, :]
            acc = jnp.zeros((CONV_ROWS, CA), F32)
            for k in range(CONV_WIDTH):
                back = CONV_WIDTH - 1 - k
                acc = acc + cw_ref[k:k + 1, :] * d1_s[base + back:base + back + CONV_ROWS, :]
                prod = dcur * a0_s[base + shift + k:base + shift + k + CONV_ROWS, :]
                part = prod[0:8]
                for f in range(1, fold):
                    part = part + prod[8 * f:8 * f + 8]
                dw8_s[k] += part
            da0_s[base:base + CONV_ROWS, :] = acc
        da0 = da0_s[...]
        sig = _sigmoid(ag_ref[...])
        dz_ref[:, 0:CA] = (da0 * sig).astype(BF16)
        dz_ref[:, CA:2 * CA] = (da0 * av_ref[...] * sig * (1.0 - sig)).astype(BF16)

        @pl.when(last)
        def _():
            for k in range(CONV_WIDTH):
                dcw_ref[k:k + 1, :] = jnp.sum(dw8_s[k], axis=0, keepdims=True)

    blk, left = _mix_specs(ts, n_r, True)
    n_halo_blocks = T // HALO
    right = pl.BlockSpec((HALO, CA), lambda b, r: (jnp.minimum((b * n_r + r + 1) * per, n_halo_blocks - 1), 0))
    return pl.pallas_call(
        body, name="mix_bwd_conv", grid=(n_seq, n_r),
        in_specs=[blk(0), right, blk(0), blk(1), left(0), left(1),
                  pl.BlockSpec((CONV_WIDTH, CA), lambda b, r: (0, 0))],
        out_specs=[pl.BlockSpec((ts, 2 * CA), lambda b, r: (b * n_r + r, 0)),
                   pl.BlockSpec((CONV_WIDTH, CA), lambda b, r: (0, 0)), pl.BlockSpec((1, CA), lambda b, r: (0, 0))],
        out_shape=[jax.ShapeDtypeStruct((T, 2 * CA), BF16), jax.ShapeDtypeStruct((CONV_WIDTH, CA), F32),
                   jax.ShapeDtypeStruct((1, CA), F32)],
        scratch_shapes=[pltpu.VMEM((HALO + ts, CA), F32), pltpu.VMEM((ts + HALO, CA), F32),
                        pltpu.VMEM((ts, CA), F32), pltpu.VMEM((CONV_WIDTH, 8, CA), F32)],
        compiler_params=_cp("arbitrary", "arbitrary"),
    )(da1, da1, z, z, z, z, conv_w)


def _sum_parts(parts, name):
    n, R, C = parts.shape
    tr = _tile(R, 512) if R % 8 == 0 and R > 512 else R
    if R % tr:
        tr = R

    def body(p_ref, o_ref):
        acc = p_ref[0]
        for k in range(1, n):
            acc = acc + p_ref[k]
        o_ref[...] = acc

    return pl.pallas_call(
        body, name=name, grid=(R // tr,),
        in_specs=[pl.BlockSpec((n, tr, C), lambda i: (0, i, 0))],
        out_specs=pl.BlockSpec((tr, C), lambda i: (i, 0)),
        out_shape=jax.ShapeDtypeStruct((R, C), F32),
        compiler_params=_cp("parallel"),
    )(parts)


def _row_tile(R, want):
    t = min(R, want)
    t -= t % 8
    while t > 8 and R % t:
        t -= 8
    return t if t >= 8 and R % t == 0 else R


def _adamw(w, g, m, v, name):
    R, C = w.shape
    tr = _row_tile(R, 256)
    c1 = 1.0 - ADAM_B1 ** ADAM_STEP
    c2 = 1.0 - ADAM_B2 ** ADAM_STEP

    def body(w_ref, g_ref, m_ref, v_ref, d_ref, nm_ref, nv_ref):
        gv = g_ref[...]
        nm = ADAM_B1 * m_ref[...] + (1.0 - ADAM_B1) * gv
        nv = ADAM_B2 * v_ref[...] + (1.0 - ADAM_B2) * (gv * gv)
        d_ref[...] = -ADAM_LR * ((nm / c1) / (jnp.sqrt(nv / c2) + ADAM_EPS) + ADAM_WD * w_ref[...])
        nm_ref[...] = nm
        nv_ref[...] = nv

    blk = pl.BlockSpec((tr, C), lambda i: (i, 0))
    return pl.pallas_call(
        body, name=name, grid=(R // tr,),
        in_specs=[blk] * 4, out_specs=[blk] * 3,
        out_shape=[jax.ShapeDtypeStruct((R, C), F32)] * 3,
        compiler_params=_cp("parallel"),
    )(w, g, m, v)


def _me():
    return lax.axis_index("x"), lax.axis_index("y"), lax.axis_index("c")


def _block_rows(ref, dev, n):
    start = (4 * dev[0] + 2 * dev[1] + dev[2]) * n
    if len(ref.shape) == 2:
        return ref.at[pl.ds(start, n), :]
    return ref.at[:, pl.ds(start, n), :]


def _all_gather(shards):
    na = len(shards)
    ns = [s.shape[-2] for s in shards]

    def body(*refs):
        ins, outs = refs[:na], refs[na:2 * na]
        send_sems, recv_sems, local_sems = refs[2 * na:]
        x, y, c = _me()
        me, sibling = (x, y, c), (x, y, 1 - c)
        chips = [(1 - x, y), (x, 1 - y), (1 - x, 1 - y)]

        def copy(a, k, block, to, src=None):
            dst = _block_rows(outs[a], block, ns[a])
            return pltpu.make_async_remote_copy(
                src_ref=dst if src is None else src, dst_ref=dst,
                send_sem=send_sems.at[a, k], recv_sem=recv_sems.at[a, k], device_id=to, device_id_type=MESH)

        mine = [pltpu.make_async_copy(ins[a], _block_rows(outs[a], me, ns[a]), local_sems.at[a]) for a in range(na)]
        for cp in mine:
            cp.start()
        first = []
        for a in range(na):
            first.append(copy(a, 0, me, sibling, src=ins[a]))
            first += [copy(a, 1 + j, me, (*chip, c), src=ins[a]) for j, chip in enumerate(chips)]
        for cp in first:
            cp.start()
        passed = []
        for j, chip in enumerate(chips):
            for a in range(na):
                copy(a, 1 + j, (*chip, c), me).wait_recv()
                fwd = copy(a, 4 + j, (*chip, c), sibling)
                fwd.start()
                passed.append(fwd)
        for a in range(na):
            copy(a, 0, sibling, me).wait_recv()
            for j, chip in enumerate(chips):
                copy(a, 4 + j, (*chip, 1 - c), me).wait_recv()
        for cp in first + passed:
            cp.wait_send()
        for cp in mine:
            cp.wait()

    out_shape = [jax.ShapeDtypeStruct(s.shape[:-2] + (N_DEV * s.shape[-2], s.shape[-1]), s.dtype) for s in shards]
    return pl.pallas_call(
        body, name="weights_all_gather",
        in_specs=[ANY] * na, out_specs=[ANY] * na, out_shape=out_shape,
        scratch_shapes=[pltpu.SemaphoreType.DMA((na, 7)), pltpu.SemaphoreType.DMA((na, 7)),
                        pltpu.SemaphoreType.DMA((na,))],
    )(*shards)


def _split_copies(gather, srcs, lands, send_sems, recv_sems, ns):
    x, y, c = _me()
    me = (x, y, c)
    my_slot = 4 * x + 2 * y + c
    copies = []
    for mask in range(1, N_DEV):
        peer = (x ^ (mask >> 2), y ^ ((mask >> 1) & 1), c ^ (mask & 1))
        for a in range(len(srcs)):
            if gather:
                src, dst = srcs[a], _block_rows(lands[a], me, ns[a])
            else:
                src, dst = _block_rows(srcs[a], peer, ns[a]), lands[a].at[my_slot]
            sem = a * (N_DEV - 1) + mask - 1
            copies.append(pltpu.make_async_remote_copy(
                src_ref=src, dst_ref=dst, send_sem=send_sems.at[sem], recv_sem=recv_sems.at[sem],
                device_id=peer, device_id_type=MESH))
    return copies


HBM_SPEC = pl.BlockSpec(memory_space=pltpu.HBM)
SEM_SPEC = pl.BlockSpec(memory_space=pltpu.SEMAPHORE)


def _split_start(gather, srcs, name, dep=None):
    na = len(srcs)
    if gather:
        ns = [s.shape[-2] for s in srcs]
        lands = [lax.empty(s.shape[:-2] + (N_DEV * s.shape[-2], s.shape[-1]), s.dtype) for s in srcs]
    else:
        ns = [s.shape[-2] // N_DEV for s in srcs]
        lands = [lax.empty((N_DEV, s.shape[-2] // N_DEV, s.shape[-1]), s.dtype) for s in srcs]
    n_in = 2 * na + (dep is not None)

    def body(*refs):
        send_sems, recv_sems = refs[n_in], refs[n_in + 1]
        for cp in _split_copies(gather, refs[:na], refs[na:2 * na], send_sems, recv_sems, ns):
            cp.start()
        refs[-1][...] = jnp.zeros_like(refs[-1])

    hbm = lambda a: pltpu.with_memory_space_constraint(a, pltpu.HBM)
    args = [hbm(a) for a in srcs] + [hbm(a) for a in lands] + ([dep] if dep is not None else [])
    out = pl.pallas_call(
        body, name=name,
        in_specs=[HBM_SPEC] * (2 * na) + ([ANY] if dep is not None else []),
        out_specs=[SEM_SPEC, SEM_SPEC] + [HBM_SPEC] * (2 * na) + [pl.BlockSpec(memory_space=pltpu.VMEM)],
        out_shape=[pltpu.SemaphoreType.DMA((na * (N_DEV - 1),)), pltpu.SemaphoreType.DMA((na * (N_DEV - 1),))]
        + [pltpu.HBM(a.shape, a.dtype) for a in srcs + lands] + [jax.ShapeDtypeStruct((8, LANES), F32)],
        input_output_aliases={i: 2 + i for i in range(2 * na)},
        compiler_params=pltpu.CompilerParams(has_side_effects=pltpu.SideEffectType.DATAFLOW_SIDE_EFFECTING),
    )(*args)
    return (gather, ns, out[0], out[1], list(out[2:2 + na]), list(out[2 + na:2 + 2 * na])), out[-1]


def _split_wait(handle, after, name):
    gather, ns, send, recv, srcs, lands = handle
    na = len(srcs)

    def body(*refs):
        send_sems, recv_sems = refs[2 * na], refs[2 * na + 1]
        for cp in _split_copies(gather, refs[:na], refs[na:2 * na], send_sems, recv_sems, ns):
            cp.wait_send()
            cp.wait_recv()

    out = pl.pallas_call(
        body, name=name,
        in_specs=[HBM_SPEC] * (2 * na) + [SEM_SPEC, SEM_SPEC, ANY],
        out_specs=[HBM_SPEC] * (2 * na),
        out_shape=[pltpu.HBM(a.shape, a.dtype) for a in srcs + lands],
        input_output_aliases={i: i for i in range(2 * na)},
        compiler_params=pltpu.CompilerParams(has_side_effects=pltpu.SideEffectType.DATAFLOW_SIDE_EFFECTING),
    )(*srcs, *lands, send, recv, after)
    return list(out[:na]), list(out[na:])


def _small_all_reduce(buf):
    R = buf.shape[0]

    def body(b_ref, o_ref, recv_ref, send_sems, recv_sems):
        x, y, c = _me()
        my_slot = 4 * x + 2 * y + c
        recv_ref[my_slot] = b_ref[...]
        copies = []
        for mask in range(1, N_DEV):
            peer = (x ^ (mask >> 2), y ^ ((mask >> 1) & 1), c ^ (mask & 1))
            copies.append(pltpu.make_async_remote_copy(
                src_ref=b_ref, dst_ref=recv_ref.at[my_slot],
                send_sem=send_sems.at[mask - 1], recv_sem=recv_sems.at[mask - 1],
                device_id=peer, device_id_type=MESH))
        for cp in copies:
            cp.start()
        for cp in copies:
            cp.wait()
        acc = recv_ref[0]
        for k in range(1, N_DEV):
            acc = acc + recv_ref[k]
        o_ref[...] = acc

    return pl.pallas_call(
        body, name="small_all_reduce",
        in_specs=[pl.BlockSpec(memory_space=pltpu.VMEM)], out_specs=pl.BlockSpec(memory_space=pltpu.VMEM),
        out_shape=jax.ShapeDtypeStruct((R, LANES), F32),
        scratch_shapes=[pltpu.VMEM((N_DEV, R, LANES), F32), pltpu.SemaphoreType.DMA((7,)),
                        pltpu.SemaphoreType.DMA((7,))],
        compiler_params=pltpu.CompilerParams(vmem_limit_bytes=VMEM_LIMIT),
    )(buf)


def _pack(arrays):
    flat = jnp.concatenate([a.reshape(-1) for a in arrays])
    pad = (-flat.shape[0]) % (8 * LANES)
    return jnp.pad(flat, (0, pad)).reshape(-1, LANES)


def _unpack(buf, shapes):
    flat = buf.reshape(-1)
    out, off = [], 0
    for s in shapes:
        n = 1
        for d in s:
            n *= d
        out.append(flat[off:off + n].reshape(s))
        off += n
    return out


def _ffn_index(layer, second):
    return (2 * layer + second) * 3


def kernel(x, g_ffn1, w_ffn1_gate, w_ffn1_up, w_ffn1_down, g_mix, w_in_ab, conv_w, conv_b, ln_a_g, ln_a_b, ln_v_g, ln_v_b, sp_w, sp_b, w_out_ab, w_qkv, w_o, g_ffn2, w_ffn2_gate, w_ffn2_up, w_ffn2_down, g_final, loss_target, m_g_ffn1, m_w_ffn1_gate, m_w_ffn1_up, m_w_ffn1_down, m_g_mix, m_w_in_ab, m_conv_w, m_conv_b, m_ln_a_g, m_ln_a_b, m_ln_v_g, m_ln_v_b, m_sp_w, m_sp_b, m_w_out_ab, m_w_qkv, m_w_o, m_g_ffn2, m_w_ffn2_gate, m_w_ffn2_up, m_w_ffn2_down, m_g_final, v_g_ffn1, v_w_ffn1_gate, v_w_ffn1_up, v_w_ffn1_down, v_g_mix, v_w_in_ab, v_conv_w, v_conv_b, v_ln_a_g, v_ln_a_b, v_ln_v_g, v_ln_v_b, v_sp_w, v_sp_b, v_w_out_ab, v_w_qkv, v_w_o, v_g_ffn2, v_w_ffn2_gate, v_w_ffn2_up, v_w_ffn2_down, v_g_final):
    n_seq, S, D = x.shape
    T = n_seq * S
    depth = g_ffn1.shape[0]
    assert depth == 2 and D == D_MODEL
    my_block = 4 * lax.axis_index("x") + 2 * lax.axis_index("y") + lax.axis_index("c")

    ffn_parts = []
    for l in range(depth):
        for gate, up, down in ((w_ffn1_gate, w_ffn1_up, w_ffn1_down), (w_ffn2_gate, w_ffn2_up, w_ffn2_down)):
            ffn_parts += [gate[l].T, up[l].T, down[l]]
    ffn_shard = lambda k: jnp.stack(ffn_parts[3 * k:3 * k + 3]).astype(BF16)
    conv_w_pad = jnp.zeros((HALO, conv_w.shape[2]), F32).at[:CONV_WIDTH].set(conv_w[0]).T
    w_ffn = [None] * (2 * depth)
    w_ffn[0], conv_w_t = _all_gather([ffn_shard(0), conv_w_pad])
    conv_w_full = conv_w_t.T[:CONV_WIDTH]
    shards_b = [w_out_ab[0].astype(BF16), ffn_shard(1)]
    shards_c = [ffn_shard(2), w_qkv[0].T.astype(BF16), w_o[0].astype(BF16), ffn_shard(3)]
    gather_a, token = _split_start(True, [w_in_ab[0].T.astype(BF16)], "gather_a_start", dep=conv_w_t)
    gather_b, token = _split_start(True, shards_b, "gather_b_start", dep=token)
    gather_c, token = _split_start(True, shards_c, "gather_c_start", dep=token)

    def gathered(handle, after, name):
        shards, lands = _split_wait(handle, after, name)
        out = []
        for land, shard in zip(lands, shards):
            n = shard.shape[-2]
            start = (0,) * (land.ndim - 2) + (my_block * n, 0)
            out.append(lax.dynamic_update_slice(land, shard, start))
        return out

    row = lambda a: a.reshape(1, -1)
    tril = jnp.tril(jnp.ones((CHUNK, CHUNK), dtype=bool))
    ws = jnp.where(tril[None], sp_w[0], 0.0).astype(BF16)
    ws_t = jnp.swapaxes(ws, 1, 2)
    bias2d = jnp.repeat(sp_b[0].T, DB, axis=1)
    conv_b2, lag, lab = row(conv_b[0]), row(ln_a_g[0]), row(ln_a_b[0])
    lvg, lvb = row(ln_v_g[0]), row(ln_v_b[0])

    x0 = x.reshape(T, D)
    target = loss_target.reshape(T, D)
    saved = []
    xc = x0
    for l in range(depth):
        xa, a1, b1, h1 = _ffn_fwd(xc, row(g_ffn1[l]), w_ffn[2 * l], 0, f"ffn1_fwd_{l}", dep=token)
        if l % 2 == 0:
            w_in_t, = gathered(gather_a, xa, "gather_a_wait")
            z, hm = _mm_nt(xa, w_in_t, row(g_mix[l]), F32, "mix_in_proj")
            cat, conv_out = _mix_fwd(z, conv_w_full, conv_b2, lag, lab, lvg, lvb, ws, bias2d, n_seq)
            w_out, w_ffn[1] = gathered(gather_b, cat, "gather_b_wait")
            xb = _mm_nn_res(cat, w_out, xa, "mix_out_proj")
            mixer = (z, hm, cat, conv_out)
        else:
            qkv, hm = _mm_nt(xa, w_qkv_t, row(g_mix[l]), BF16, "qkv_proj")
            o, tot = _attn_fwd(qkv, n_seq)
            xb = _mm_nn_res(o, w_o_full, xa, "attn_out_proj")
            mixer = (qkv, hm, o, tot)
        xn, a2, b2, h2 = _ffn_fwd(xb, row(g_ffn2[l]), w_ffn[2 * l + 1], 0, f"ffn2_fwd_{l}")
        saved.append((xc, a1, b1, h1, xa, mixer, xb, a2, b2, h2))
        xc = xn
        if l == 0:
            w_ffn[2], w_qkv_t, w_o_full, w_ffn[3] = gathered(gather_c, xc, "gather_c_wait")

    g, loss_part, dg_final = _loss_head(xc, row(g_final), target)

    dg_ffn1, dg_ffn2, dg_mix = [None] * depth, [None] * depth, [None] * depth
    exchanges = {}
    token = None

    def ffn_back(g, xin, gvec, a, b, h, k, tag, token):
        g, dg, da, db, s, gh = _ffn_bwd(g, xin, gvec, a, b, w_ffn[k], 0, f"ffn{tag}_bwd", dep=token)
        if k == 0:
            return g, dg, (da, db, s, gh, h)
        dws = [_mm_tn(da, h, f"dw_gate{tag}"), _mm_tn(db, h, f"dw_up{tag}"), _mm_tn(s, gh, f"dw_down{tag}")]
        exchanges[f"ffn{k}"], token = _split_start(False, dws, f"exchange_ffn{tag}_start")
        return g, dg, token

    for l in reversed(range(depth)):
        xin, a1, b1, h1, xa, mixer, xb, a2, b2, h2 = saved[l]
        g, dg_ffn2[l], token = ffn_back(g, xb, row(g_ffn2[l]), a2, b2, h2, 2 * l + 1, f"2_{l}", token)
        if l % 2 == 0:
            z, hm, cat, conv_out = mixer
            dcat = _mm_nt(g, w_out, None, F32, "mix_out_bwd", dep=token)
            d_w_out = _mm_tn(cat, g, "dw_out")
            (da1, dz_uv, d_lag, d_lab, d_lvg, d_lvb, d_ws, d_sb) = _mix_bwd_rows(
                dcat, z, conv_out, lag, lab, lvg, lvb, ws, ws_t, bias2d, n_seq)
            dz_a, d_cw, d_cb = _mix_bwd_conv(da1, z, conv_w_full, n_seq)
            d_w_in_t = jnp.concatenate([_mm_tn(dz_a, hm, "dw_in_a"), _mm_tn(dz_uv, hm, "dw_in_uv")])
            exchanges["mix"], token = _split_start(False, [d_w_out, d_w_in_t], "exchange_mix_start")
            g, dg_mix[l] = _mm_nn_rmsbwd([dz_a, dz_uv], w_in_t, xa, row(g_mix[l]), g, "mix_in_bwd", dep=token)
        else:
            qkv, hm, o, tot = mixer
            do = _mm_nt(g, w_o_full, None, BF16, "attn_out_bwd", dep=token)
            d_w_o = _mm_tn(o, g, "dw_o")
            dq, dk, dv = _attn_bwd(qkv, do, tot, n_seq)
            d_w_qkv_t = jnp.concatenate([_mm_tn(dq, hm, "dw_q"), _mm_tn(dk, hm, "dw_k"), _mm_tn(dv, hm, "dw_v")])
            exchanges["attn"], token = _split_start(False, [d_w_o, d_w_qkv_t], "exchange_attn_start")
            g, dg_mix[l] = _mm_nn_rmsbwd([dq, dk, dv], w_qkv_t, xa, row(g_mix[l]), g, "qkv_bwd", dep=token)
        g, dg_ffn1[l], token = ffn_back(g, xin, row(g_ffn1[l]), a1, b1, h1, 2 * l, f"1_{l}", token)
    grad_x = g.reshape(n_seq, S, D)

    small = [jnp.concatenate(dg_ffn1), jnp.concatenate(dg_mix), d_cw, d_cb, d_lag, d_lab, d_lvg, d_lvb,
             jnp.where(tril[None], d_ws, 0.0), d_sb.T, jnp.concatenate(dg_ffn2), dg_final, loss_part[:, :1]]
    small_shapes = [(depth, D), (depth, D), (CONV_WIDTH, CA), (1, CA), (1, CA), (1, CA), (1, GB, DB), (1, GB, DB),
                    (1, GB, CHUNK, CHUNK), (1, GB, CHUNK), (depth, D), (D,), ()]
    small_sum = _small_all_reduce(_pack(small))
    red = _unpack(small_sum, small_shapes)
    (gr_g_ffn1, gr_g_mix, gr_cw_full, gr_cb, gr_lag, gr_lab, gr_lvg, gr_lvb, gr_sp_w, gr_sp_b,
     gr_g_ffn2, gr_g_final, loss) = red
    n_cw = conv_w.shape[2]
    gr_cw = lax.dynamic_slice(gr_cw_full, (0, my_block * n_cw), (CONV_WIDTH, n_cw))[None]

    da, db, s, gh, h = token
    token = small_sum
    for which, lhs, rhs in ((2, s, gh), (1, db, h), (0, da, h)):
        dw = _mm_tn(lhs, rhs, f"dw_ffn0_{which}", dep=token)
        exchanges[f"ffn0_{which}"], token = _split_start(False, [dw], f"exchange_ffn0_{which}_start")

    def reduced(key, after):
        out = []
        fulls, lands = _split_wait(exchanges[key], after, f"exchange_{key}_wait")
        for i, (land, full) in enumerate(zip(lands, fulls)):
            n = land.shape[1]
            own = lax.dynamic_slice(full, (my_block * n, 0), (n, full.shape[1]))
            parts = lax.dynamic_update_slice(land, own[None], (my_block, 0, 0))
            out.append(_sum_parts(parts, f"sum_{key}_{i}"))
        return out

    g_ffn_t = [None] * (6 * depth)
    for k in range(1, 2 * depth):
        g_ffn_t[3 * k:3 * k + 3] = reduced(f"ffn{k}", token)
    g_out, g_in = reduced("mix", token)
    g_o, g_qkv = reduced("attn", token)
    g_in, g_out, g_qkv, g_o = g_in.T[None], g_out[None], g_qkv.T[None], g_o[None]

    def ffn_grad(second, which):
        return jnp.stack([g_ffn_t[_ffn_index(l, second) + which].T if which < 2
                          else g_ffn_t[_ffn_index(l, second) + which] for l in range(depth)])

    grads = {
        "g_ffn1": gr_g_ffn1, "g_mix": gr_g_mix, "w_in_ab": g_in, "conv_w": gr_cw, "conv_b": gr_cb, "ln_a_g": gr_lag,
        "ln_a_b": gr_lab, "ln_v_g": gr_lvg, "ln_v_b": gr_lvb, "sp_w": gr_sp_w, "sp_b": gr_sp_b, "w_out_ab": g_out,
        "w_qkv": g_qkv, "w_o": g_o, "g_ffn2": gr_g_ffn2, "w_ffn2_gate": ffn_grad(1, 0), "w_ffn2_up": ffn_grad(1, 1),
        "w_ffn2_down": ffn_grad(1, 2), "g_final": gr_g_final,
    }
    weights = dict(g_ffn1=g_ffn1, w_ffn1_gate=w_ffn1_gate, w_ffn1_up=w_ffn1_up, w_ffn1_down=w_ffn1_down, g_mix=g_mix,
                   w_in_ab=w_in_ab, conv_w=conv_w, conv_b=conv_b, ln_a_g=ln_a_g, ln_a_b=ln_a_b, ln_v_g=ln_v_g,
                   ln_v_b=ln_v_b, sp_w=sp_w, sp_b=sp_b, w_out_ab=w_out_ab, w_qkv=w_qkv, w_o=w_o, g_ffn2=g_ffn2,
                   w_ffn2_gate=w_ffn2_gate, w_ffn2_up=w_ffn2_up, w_ffn2_down=w_ffn2_down, g_final=g_final)
    m_in = dict(g_ffn1=m_g_ffn1, w_ffn1_gate=m_w_ffn1_gate, w_ffn1_up=m_w_ffn1_up, w_ffn1_down=m_w_ffn1_down,
                g_mix=m_g_mix, w_in_ab=m_w_in_ab, conv_w=m_conv_w, conv_b=m_conv_b, ln_a_g=m_ln_a_g, ln_a_b=m_ln_a_b,
                ln_v_g=m_ln_v_g, ln_v_b=m_ln_v_b, sp_w=m_sp_w, sp_b=m_sp_b, w_out_ab=m_w_out_ab, w_qkv=m_w_qkv,
                w_o=m_w_o, g_ffn2=m_g_ffn2, w_ffn2_gate=m_w_ffn2_gate, w_ffn2_up=m_w_ffn2_up,
                w_ffn2_down=m_w_ffn2_down, g_final=m_g_final)
    v_in = dict(g_ffn1=v_g_ffn1, w_ffn1_gate=v_w_ffn1_gate, w_ffn1_up=v_w_ffn1_up, w_ffn1_down=v_w_ffn1_down,
                g_mix=v_g_mix, w_in_ab=v_w_in_ab, conv_w=v_conv_w, conv_b=v_conv_b, ln_a_g=v_ln_a_g, ln_a_b=v_ln_a_b,
                ln_v_g=v_ln_v_g, ln_v_b=v_ln_v_b, sp_w=v_sp_w, sp_b=v_sp_b, w_out_ab=v_w_out_ab, w_qkv=v_w_qkv,
                w_o=v_w_o, g_ffn2=v_g_ffn2, w_ffn2_gate=v_w_ffn2_gate, w_ffn2_up=v_w_ffn2_up,
                w_ffn2_down=v_w_ffn2_down, g_final=v_g_final)
    names = list(weights)
    grads = {n: grads[n].reshape(weights[n].shape) for n in grads}

    delta, new_m, new_v = {}, {}, {}

    def adamw_big(n):
        shp = weights[n].shape
        if shp[-1] == D:
            two, back = (lambda a: a.reshape(-1, D)), (lambda a: a.reshape(shp))
        else:
            two = lambda a: jnp.swapaxes(a, 1, 2).reshape(-1, D)
            back = lambda a: jnp.swapaxes(a.reshape(shp[0], shp[2], D), 1, 2)
        d, nm, nv = _adamw(two(weights[n]), two(grads[n]), two(m_in[n]), two(v_in[n]), f"adamw_{n}")
        delta[n], new_m[n], new_v[n] = back(d), back(nm), back(nv)

    big = [n for n in names if n.startswith("w_")]
    late = ["w_ffn1_gate", "w_ffn1_up", "w_ffn1_down"]
    for n in big:
        if n not in late:
            adamw_big(n)
    after = jnp.concatenate([delta[n].reshape(-1)[:1] for n in big if n not in late]).reshape(1, -1)
    for which in (2, 1, 0):
        g_ffn_t[which], = reduced(f"ffn0_{which}", after)
    for which, n in enumerate(late):
        grads[n] = ffn_grad(0, which).reshape(weights[n].shape)
        adamw_big(n)
    little = [n for n in names if n not in big]
    shapes = [weights[n].shape for n in little]
    d, nm, nv = _adamw(_pack([weights[n] for n in little]), _pack([grads[n] for n in little]),
                       _pack([m_in[n] for n in little]), _pack([v_in[n] for n in little]), "adamw_small")
    for n, dd, mm, vv in zip(little, _unpack(d, shapes), _unpack(nm, shapes), _unpack(nv, shapes)):
        delta[n], new_m[n], new_v[n] = dd, mm, vv

    return (loss, grad_x, *[grads[n] for n in names], *[delta[n] for n in names],
            *[new_m[n] for n in names], *[new_v[n] for n in names])
```

```python
import functools

import jax
import jax.numpy as jnp
from jax import lax
from jax.experimental import pallas as pl
from jax.experimental.pallas import tpu as pltpu

F32 = jnp.float32
BF16 = jnp.bfloat16

D_MODEL = 1024
CA = 512
CB = 512
GB = 4
DB = 128
CHUNK = 128
CONV_WIDTH = 31
N_HEADS = 16
HEAD_DIM = 64
EPS = 1e-6
N_DEV = 8
LANES = 128
SUBLANES = 8
QB = 128
ATT_TQ = 1024
FFN_TN = 1408
HALO = 32
CONV_ROWS = 32
ATT_SCALE = HEAD_DIM ** -0.5

ADAM_LR = 0.001
ADAM_B1 = 0.9
ADAM_B2 = 0.999
ADAM_EPS = 1e-08
ADAM_WD = 0.01
ADAM_STEP = 10

NT = (((1,), (1,)), ((), ()))
NN = (((1,), (0,)), ((), ()))
TN = (((0,), (0,)), ((), ()))
MESH = pl.DeviceIdType.MESH
ANY = pl.BlockSpec(memory_space=pl.ANY)
VMEM_LIMIT = 60 * 1024 * 1024


def _dot(a, b, dims):
    return lax.dot_general(a, b, dims, preferred_element_type=F32)


def _cp(*sem):
    return pltpu.CompilerParams(dimension_semantics=sem, vmem_limit_bytes=VMEM_LIMIT)


def _pcall(body, *, in_specs, args, dep=None, **kw):
    if dep is not None:
        n_in = len(in_specs)
        inner = body

        def body(*refs):
            inner(*refs[:n_in], *refs[n_in + 1:])

        in_specs = list(in_specs) + [ANY]
        args = tuple(args) + (dep,)
    return pl.pallas_call(body, in_specs=list(in_specs), **kw)(*args)


def _tile(n, want):
    if n <= want:
        return n
    t = want - want % LANES
    while t > LANES and n % t:
        t -= LANES
    assert n % t == 0, (n, want)
    return t


def _sigmoid(x):
    return 1.0 / (1.0 + jnp.exp(-x))


def _rstd(x):
    return lax.rsqrt(jnp.mean(x * x, axis=-1, keepdims=True) + EPS)


def _rms_bwd(x, g, dh):
    r = _rstd(x)
    u = dh * g
    dx = r * (u - x * (r * r) * jnp.mean(u * x, axis=-1, keepdims=True))
    dg = jnp.sum(dh * x * r, axis=0, keepdims=True)
    return dx, dg


def _ln_fwd(x, g, b):
    mu = jnp.mean(x, axis=-1, keepdims=True)
    xc = x - mu
    r = lax.rsqrt(jnp.mean(xc * xc, axis=-1, keepdims=True) + EPS)
    xh = xc * r
    return xh * g + b, xh, r


def _ln_bwd(dy, xh, r, g):
    dxh = dy * g
    return r * (dxh - jnp.mean(dxh, axis=-1, keepdims=True)
                - xh * jnp.mean(dxh * xh, axis=-1, keepdims=True))


def _ffn_fwd(x, g, wall, base, name, dep=None):
    T, D = x.shape
    F = wall.shape[1]
    tm, tn = _tile(T, 512), _tile(F, FFN_TN)
    n_j = F // tn

    def body(x_ref, g_ref, wg_ref, wu_ref, wd_ref, xo_ref, a_ref, b_ref, h_ref, acc_ref):
        j = pl.program_id(1)

        @pl.when(j == 0)
        def _():
            xv = x_ref[...]
            h_ref[...] = (xv * _rstd(xv) * g_ref[...]).astype(BF16)
            acc_ref[...] = jnp.zeros_like(acc_ref)

        h = h_ref[...]
        a = _dot(h, wg_ref[...], NT)
        b = _dot(h, wu_ref[...], NT)
        a_ref[...] = a.astype(BF16)
        b_ref[...] = b.astype(BF16)
        s = (a * _sigmoid(a) * b).astype(BF16)
        acc_ref[...] += _dot(s, wd_ref[...], NN)

        @pl.when(j == n_j - 1)
        def _():
            xo_ref[...] = x_ref[...] + 0.5 * acc_ref[...]

    wspec = lambda k: pl.BlockSpec((None, tn, D), lambda i, j: (base + k, j, 0))
    return _pcall(
        body, name=name, grid=(T // tm, n_j), dep=dep, args=(x, g, wall, wall, wall),
        in_specs=[pl.BlockSpec((tm, D), lambda i, j: (i, 0)), pl.BlockSpec((1, D), lambda i, j: (0, 0)),
                  wspec(0), wspec(1), wspec(2)],
        out_specs=[pl.BlockSpec((tm, D), lambda i, j: (i, 0)), pl.BlockSpec((tm, tn), lambda i, j: (i, j)),
                   pl.BlockSpec((tm, tn), lambda i, j: (i, j)), pl.BlockSpec((tm, D), lambda i, j: (i, 0))],
        out_shape=[jax.ShapeDtypeStruct((T, D), F32), jax.ShapeDtypeStruct((T, F), BF16),
                   jax.ShapeDtypeStruct((T, F), BF16), jax.ShapeDtypeStruct((T, D), BF16)],
        scratch_shapes=[pltpu.VMEM((tm, D), F32)],
        compiler_params=_cp("parallel", "arbitrary"),
    )


def _ffn_bwd(go, x, g, a, b, wall, base, name, dep=None):
    T, D = x.shape
    F = wall.shape[1]
    tm, tn = _tile(T, 512), _tile(F, FFN_TN)
    n_j = F // tn

    def body(go_ref, x_ref, g_ref, a_ref, b_ref, wg_ref, wu_ref, wd_ref,
             gx_ref, dg_ref, da_ref, db_ref, s_ref, gh_ref, acc_ref):
        i, j = pl.program_id(0), pl.program_id(1)

        @pl.when(j == 0)
        def _():
            gh_ref[...] = (0.5 * go_ref[...]).astype(BF16)
            acc_ref[...] = jnp.zeros_like(acc_ref)

        @pl.when((i == 0) & (j == 0))
        def _():
            dg_ref[...] = jnp.zeros_like(dg_ref)

        ds = _dot(gh_ref[...], wd_ref[...], NT)
        av = a_ref[...].astype(F32)
        bv = b_ref[...].astype(F32)
        sig = _sigmoid(av)
        sl = av * sig
        dab = (ds * bv * (sig * (1.0 + av * (1.0 - sig)))).astype(BF16)
        dbb = (ds * sl).astype(BF16)
        s_ref[...] = (sl * bv).astype(BF16)
        da_ref[...] = dab
        db_ref[...] = dbb
        acc_ref[...] += _dot(dab, wg_ref[...], NN) + _dot(dbb, wu_ref[...], NN)

        @pl.when(j == n_j - 1)
        def _():
            dx, dg = _rms_bwd(x_ref[...], g_ref[...], acc_ref[...])
            gx_ref[...] = go_ref[...] + dx
            dg_ref[...] += dg

    wspec = lambda k: pl.BlockSpec((None, tn, D), lambda i, j: (base + k, j, 0))
    row = pl.BlockSpec((tm, D), lambda i, j: (i, 0))
    hid = pl.BlockSpec((tm, tn), lambda i, j: (i, j))
    vec = pl.BlockSpec((1, D), lambda i, j: (0, 0))
    return _pcall(
        body, name=name, grid=(T // tm, n_j), dep=dep, args=(go, x, g, a, b, wall, wall, wall),
        in_specs=[row, row, vec, hid, hid, wspec(0), wspec(1), wspec(2)],
        out_specs=[row, vec, hid, hid, hid, row],
        out_shape=[jax.ShapeDtypeStruct((T, D), F32), jax.ShapeDtypeStruct((1, D), F32),
                   jax.ShapeDtypeStruct((T, F), BF16), jax.ShapeDtypeStruct((T, F), BF16),
                   jax.ShapeDtypeStruct((T, F), BF16), jax.ShapeDtypeStruct((T, D), BF16)],
        scratch_shapes=[pltpu.VMEM((tm, D), F32)],
        compiler_params=_cp("arbitrary", "arbitrary"),
    )


def _mm_tn(a, b, name, dep=None):
    T, M = a.shape
    N = b.shape[1]
    tmm, tk = _tile(M, 1536), _tile(T, 1024)

    def body(a_ref, b_ref, o_ref):
        @pl.when(pl.program_id(1) == 0)
        def _():
            o_ref[...] = jnp.zeros_like(o_ref)

        o_ref[...] += _dot(a_ref[...].astype(BF16), b_ref[...].astype(BF16), TN)

    return _pcall(
        body, name=name, grid=(M // tmm, T // tk), dep=dep, args=(a, b),
        in_specs=[pl.BlockSpec((tk, tmm), lambda m, k: (k, m)), pl.BlockSpec((tk, N), lambda m, k: (k, 0))],
        out_specs=pl.BlockSpec((tmm, N), lambda m, k: (m, 0)),
        out_shape=jax.ShapeDtypeStruct((M, N), F32),
        compiler_params=_cp("parallel", "arbitrary"),
    )


def _mm_nt(x, wt, g, out_dtype, name, dep=None):
    T, K = x.shape
    N = wt.shape[0]
    tm, tn = _tile(T, 512), _tile(N, 1536)
    norm = g is not None

    def body(*refs):
        if norm:
            x_ref, g_ref, w_ref, o_ref, h_ref = refs
        else:
            x_ref, w_ref, o_ref, h_ref = refs

        @pl.when(pl.program_id(1) == 0)
        def _():
            xv = x_ref[...].astype(F32)
            if norm:
                xv = xv * _rstd(xv) * g_ref[...]
            h_ref[...] = xv.astype(BF16)

        o_ref[...] = _dot(h_ref[...], w_ref[...], NT).astype(out_dtype)

    row = pl.BlockSpec((tm, K), lambda i, j: (i, 0))
    wsp = pl.BlockSpec((tn, K), lambda i, j: (j, 0))
    osp = pl.BlockSpec((tm, tn), lambda i, j: (i, j))
    if norm:
        return pl.pallas_call(
            body, name=name, grid=(T // tm, N // tn),
            in_specs=[row, pl.BlockSpec((1, K), lambda i, j: (0, 0)), wsp],
            out_specs=[osp, row],
            out_shape=[jax.ShapeDtypeStruct((T, N), out_dtype), jax.ShapeDtypeStruct((T, K), BF16)],
            compiler_params=_cp("parallel", "arbitrary"),
        )(x, g, wt)
    return _pcall(
        body, name=name, grid=(T // tm, N // tn), dep=dep, args=(x, wt),
        in_specs=[row, wsp], out_specs=osp,
        out_shape=jax.ShapeDtypeStruct((T, N), out_dtype),
        scratch_shapes=[pltpu.VMEM((tm, K), BF16)],
        compiler_params=_cp("parallel", "arbitrary"),
    )


def _mm_nn_res(act, w, resid, name):
    T, K = act.shape
    D = w.shape[1]
    tm = _tile(T, 512)

    def body(a_ref, w_ref, r_ref, o_ref):
        o_ref[...] = r_ref[...] + _dot(a_ref[...].astype(BF16), w_ref[...], NN)

    return pl.pallas_call(
        body, name=name, grid=(T // tm,),
        in_specs=[pl.BlockSpec((tm, K), lambda i: (i, 0)), pl.BlockSpec((K, D), lambda i: (0, 0)),
                  pl.BlockSpec((tm, D), lambda i: (i, 0))],
        out_specs=pl.BlockSpec((tm, D), lambda i: (i, 0)),
        out_shape=jax.ShapeDtypeStruct((T, D), F32),
        compiler_params=_cp("parallel"),
    )(act, w, resid)


def _mm_nn_rmsbwd(acts, w, x, g, gprev, name, dep=None):
    T = acts[0].shape[0]
    ks = [a.shape[1] for a in acts]
    K, D = w.shape
    assert sum(ks) == K
    tm = _tile(T, 512)
    na = len(acts)

    def body(*refs):
        a_refs = refs[:na]
        w_ref, x_ref, g_ref, gp_ref, o_ref, dg_ref = refs[na:]

        @pl.when(pl.program_id(0) == 0)
        def _():
            dg_ref[...] = jnp.zeros_like(dg_ref)

        dh, off = None, 0
        for a_ref, k in zip(a_refs, ks):
            part = _dot(a_ref[...].astype(BF16), w_ref[off:off + k, :], NN)
            dh = part if dh is None else dh + part
            off += k
        dx, dg = _rms_bwd(x_ref[...], g_ref[...], dh)
        o_ref[...] = gp_ref[...] + dx
        dg_ref[...] += dg

    row = pl.BlockSpec((tm, D), lambda i: (i, 0))
    vec = pl.BlockSpec((1, D), lambda i: (0, 0))
    return _pcall(
        body, name=name, grid=(T // tm,), dep=dep, args=(*acts, w, x, g, gprev),
        in_specs=[pl.BlockSpec((tm, k), lambda i: (i, 0)) for k in ks]
        + [pl.BlockSpec((K, D), lambda i: (0, 0)), row, vec, row],
        out_specs=[row, vec],
        out_shape=[jax.ShapeDtypeStruct((T, D), F32), jax.ShapeDtypeStruct((1, D), F32)],
        compiler_params=_cp("arbitrary"),
    )


def _loss_head(x, g, target):
    T, D = x.shape
    tm = _tile(T, 512)

    def body(x_ref, g_ref, t_ref, dx_ref, loss_ref, dg_ref):
        @pl.when(pl.program_id(0) == 0)
        def _():
            loss_ref[...] = jnp.zeros_like(loss_ref)
            dg_ref[...] = jnp.zeros_like(dg_ref)

        xv = x_ref[...]
        gv = g_ref[...]
        e = xv * _rstd(xv) * gv - t_ref[...]
        per_tok = jnp.sum(e * e, axis=-1, keepdims=True) * (1.0 / D)
        loss_ref[...] += 0.5 * jnp.sum(per_tok, axis=0, keepdims=True)
        dx, dg = _rms_bwd(xv, gv, e * (1.0 / D))
        dx_ref[...] = dx
        dg_ref[...] += dg

    row = pl.BlockSpec((tm, D), lambda i: (i, 0))
    vec = pl.BlockSpec((1, D), lambda i: (0, 0))
    return pl.pallas_call(
        body, name="loss_head", grid=(T // tm,),
        in_specs=[row, vec, row],
        out_specs=[row, pl.BlockSpec((1, LANES), lambda i: (0, 0)), vec],
        out_shape=[jax.ShapeDtypeStruct((T, D), F32), jax.ShapeDtypeStruct((1, LANES), F32),
                   jax.ShapeDtypeStruct((1, D), F32)],
        compiler_params=_cp("arbitrary"),
    )(x, g, target)


def _log_gates(z):
    ls = jnp.minimum(z, 0.0) - jnp.log(1.0 + jnp.exp(-jnp.abs(z)))
    return ls, ls - z


def _cumsum_mm(v, u2):
    hi = v.astype(BF16)
    lo = (v - hi.astype(F32)).astype(BF16)
    return _dot(jnp.concatenate([hi, lo], axis=1), u2, NN)


def _half_rowsum(v):
    n = v.shape[0]
    s0 = jnp.sum(v[:, :QB], axis=1, keepdims=True)
    s1 = jnp.sum(v[:, QB:], axis=1, keepdims=True)
    return jnp.concatenate([jnp.broadcast_to(s0, (n, QB)), jnp.broadcast_to(s1, (n, QB))], axis=1)


def _stack_heads(src_ref, dst_ref, n_blk):
    m0 = lax.broadcasted_iota(jnp.int32, (1, LANES), 1) < HEAD_DIM

    def fill(c, carry):
        blk = src_ref[pl.ds(pl.multiple_of(c * QB, QB), QB), :]
        zero = jnp.zeros_like(blk)
        dst_ref[c, 0:QB, :] = jnp.where(m0, blk, zero)
        dst_ref[c, QB:2 * QB, :] = jnp.where(m0, zero, blk)
        return carry

    lax.fori_loop(0, n_blk, fill, 0)


def _diag_mask(tq, j):
    n = tq - j * QB
    row = lax.broadcasted_iota(jnp.int32, (n, 2 * QB), 0)
    col = lax.broadcasted_iota(jnp.int32, (n, 2 * QB), 1)
    return (col & (QB - 1)) < row


def _tri_blockdiag(upper):
    r = lax.broadcasted_iota(jnp.int32, (2 * QB, 2 * QB), 0)
    c = lax.broadcasted_iota(jnp.int32, (2 * QB, 2 * QB), 1)
    same = (r // QB) == (c // QB)
    u = (same & ((r > c) if upper else (r < c))).astype(BF16)
    return jnp.concatenate([u, u], axis=0)


def _attn_tiles(T, n_seq):
    S = T // n_seq
    tq = ATT_TQ if S % ATT_TQ == 0 else QB
    return S, tq, tq // QB, S // tq, S // QB


def _attn_fwd(qkv, n_seq):
    T = qkv.shape[0]
    S, tq, r, n_q, n_k = _attn_tiles(T, n_seq)
    n_p = D_MODEL // LANES
    u_suffix = _tri_blockdiag(True)

    def body(q_ref, k_ref, v_ref, u_ref, o_ref, tot_ref, kk_ref, vv_ref, lr_s, acc_s):
        qi = pl.program_id(2)

        @pl.when(qi == 0)
        def _():
            _stack_heads(k_ref, kk_ref, n_k)
            _stack_heads(v_ref, vv_ref, n_k)

        u = u_ref[...]
        lr_s[...] = jnp.zeros_like(lr_s)
        acc_s[...] = jnp.zeros_like(acc_s)

        def step(kj, q, mask, lr, acc):
            ls, lk = _log_gates(_dot(q, kk_ref[kj], NT))
            if mask is not None:
                lk = jnp.where(mask, lk, 0.0)
            a = jnp.exp(ls + _cumsum_mm(lk, u) + lr)
            if mask is not None:
                a = jnp.where(mask, a, 0.0)
            return lr + _half_rowsum(lk), acc + _dot(a.astype(BF16), vv_ref[kj], NN)

        for j in reversed(range(r)):
            rows = slice(j * QB, tq)
            lr, acc = step(qi * r + j, q_ref[rows, :] * ATT_SCALE, _diag_mask(tq, j), lr_s[rows, :], acc_s[rows, :])
            lr_s[rows, :] = lr
            acc_s[rows, :] = acc

        q = q_ref[...] * ATT_SCALE

        def off(it, carry):
            lr, acc = lr_s[...], acc_s[...]
            for j in range(r):
                lr, acc = step((qi - it) * r - 1 - j, q, None, lr, acc)
            lr_s[...] = lr
            acc_s[...] = acc
            return carry

        lax.fori_loop(0, qi, off, 0)
        o_ref[...] = acc_s[...].astype(BF16)
        tot_ref[...] = lr_s[...]

    return pl.pallas_call(
        body, name="attn_fwd", grid=(n_seq, n_p, n_q),
        in_specs=[pl.BlockSpec((tq, LANES), lambda b, p, qi: (b * n_q + qi, p)),
                  pl.BlockSpec((S, LANES), lambda b, p, qi: (b, n_p + p)),
                  pl.BlockSpec((S, LANES), lambda b, p, qi: (b, 2 * n_p + p)),
                  pl.BlockSpec((4 * QB, 2 * QB), lambda b, p, qi: (0, 0))],
        out_specs=[pl.BlockSpec((tq, LANES), lambda b, p, qi: (b * n_q + qi, p)),
                   pl.BlockSpec((tq, 2 * QB), lambda b, p, qi: (b * n_q + qi, p))],
        out_shape=[jax.ShapeDtypeStruct((T, D_MODEL), BF16), jax.ShapeDtypeStruct((T, 2 * D_MODEL), F32)],
        scratch_shapes=[pltpu.VMEM((n_k, 2 * QB, LANES), BF16), pltpu.VMEM((n_k, 2 * QB, LANES), BF16),
                        pltpu.VMEM((tq, 2 * QB), F32), pltpu.VMEM((tq, LANES), F32)],
        compiler_params=_cp("parallel", "parallel", "arbitrary"),
    )(qkv, qkv, qkv, u_suffix)


def _attn_bwd(qkv, do, tot, n_seq):
    T = qkv.shape[0]
    S, tq, r, n_q, n_k = _attn_tiles(T, n_seq)
    n_p = D_MODEL // LANES
    u_prefix = _tri_blockdiag(False)

    def body(q_ref, k_ref, v_ref, do_ref, tot_ref, u_ref, dq_ref, dk_out, dv_out,
             kk_ref, vv_ref, cl_s, cg_s, dq_s, dk_ref, dv_ref):
        qi = pl.program_id(2)

        @pl.when(qi == 0)
        def _():
            _stack_heads(k_ref, kk_ref, n_k)
            _stack_heads(v_ref, vv_ref, n_k)
            dk_ref[...] = jnp.zeros_like(dk_ref)
            dv_ref[...] = jnp.zeros_like(dv_ref)

        u = u_ref[...]
        m0 = lax.broadcasted_iota(jnp.int32, (1, LANES), 1) < HEAD_DIM
        cl_s[...] = tot_ref[...]
        cg_s[...] = jnp.zeros_like(cg_s)
        dq_s[...] = jnp.zeros_like(dq_s)

        def step(kj, q, dov, mask, rest, cg, dq):
            kk = kk_ref[kj]
            ls, lk = _log_gates(_dot(q, kk, NT))
            if mask is not None:
                lk = jnp.where(mask, lk, 0.0)
            a = jnp.exp(ls + (rest - (_cumsum_mm(lk, u) + lk)))
            if mask is not None:
                a = jnp.where(mask, a, 0.0)
            g = a * _dot(dov, vv_ref[kj], NT)
            dz = g - (g + _dot(g.astype(BF16), u[:2 * QB], NN) + cg) * jnp.exp(ls)
            if mask is not None:
                dz = jnp.where(mask, dz, 0.0)
            dz = dz.astype(BF16)
            rows = pl.ds(pl.multiple_of(kj * QB, QB), QB)
            dvt = _dot(a.astype(BF16), dov, TN)
            dv_ref[rows, :] += jnp.where(m0, dvt[:QB], dvt[QB:])
            dkt = _dot(dz, q, TN)
            dk_ref[rows, :] += jnp.where(m0, dkt[:QB], dkt[QB:])
            return rest - _half_rowsum(lk), cg + _half_rowsum(g), dq + _dot(dz, kk, NN)

        q = q_ref[...] * ATT_SCALE
        dov = do_ref[...]

        def off(it, carry):
            cl, cg, dq = cl_s[...], cg_s[...], dq_s[...]
            for j in range(r):
                cl, cg, dq = step(it * r + j, q, dov, None, cl, cg, dq)
            cl_s[...] = cl
            cg_s[...] = cg
            dq_s[...] = dq
            return carry

        lax.fori_loop(0, qi, off, 0)

        for j in range(r):
            rows = slice(j * QB, tq)
            cl, cg, dq = step(qi * r + j, q_ref[rows, :] * ATT_SCALE, do_ref[rows, :],
                              _diag_mask(tq, j), cl_s[rows, :], cg_s[rows, :], dq_s[rows, :])
            cl_s[rows, :] = cl
            cg_s[rows, :] = cg
            dq_s[rows, :] = dq
        dq_ref[...] = (dq_s[...] * ATT_SCALE).astype(BF16)

        @pl.when(qi == n_q - 1)
        def _():
            dk_out[...] = dk_ref[...].astype(BF16)
            dv_out[...] = dv_ref[...].astype(BF16)

    qspec = pl.BlockSpec((tq, LANES), lambda b, p, qi: (b * n_q + qi, p))
    seq = lambda off: pl.BlockSpec((S, LANES), lambda b, p, qi: (b, off + p))
    return pl.pallas_call(
        body, name="attn_bwd", grid=(n_seq, n_p, n_q),
        in_specs=[qspec, seq(n_p), seq(2 * n_p), qspec,
                  pl.BlockSpec((tq, 2 * QB), lambda b, p, qi: (b * n_q + qi, p)),
                  pl.BlockSpec((4 * QB, 2 * QB), lambda b, p, qi: (0, 0))],
        out_specs=[qspec, seq(0), seq(0)],
        out_shape=[jax.ShapeDtypeStruct((T, D_MODEL), BF16)] * 3,
        scratch_shapes=[pltpu.VMEM((n_k, 2 * QB, LANES), BF16), pltpu.VMEM((n_k, 2 * QB, LANES), BF16),
                        pltpu.VMEM((tq, 2 * QB), F32), pltpu.VMEM((tq, 2 * QB), F32), pltpu.VMEM((tq, LANES), F32),
                        pltpu.VMEM((S, LANES), F32), pltpu.VMEM((S, LANES), F32)],
        compiler_params=_cp("parallel", "parallel", "arbitrary"),
    )(qkv, qkv, qkv, do, tot, u_prefix)


def _shifted_copies(sh_ref):
    rows = sh_ref.shape[1] - SUBLANES
    for s in range(1, SUBLANES):
        sh_ref[s, 0:rows, :] = sh_ref[0, s:s + rows, :]


def _shifted(sh_ref, start, n):
    s = start % SUBLANES
    return sh_ref[s, start - s:start - s + n, :]


def _glu_with_halo(av_ref, ag_ref, avh_ref, agh_ref, a0_s, first, ts):
    hal = avh_ref[...] * _sigmoid(agh_ref[...])
    a0_s[0, 0:HALO, :] = jnp.where(first, 0.0, hal)
    a0_s[0, HALO:HALO + ts, :] = av_ref[...] * _sigmoid(ag_ref[...])
    _shifted_copies(a0_s)


def _mix_specs(ts, n_r, with_left):
    blk = lambda c: pl.BlockSpec((ts, CA), lambda b, r: (b * n_r + r, c))
    per = ts // HALO
    left = lambda c: pl.BlockSpec((HALO, CA), lambda b, r: (jnp.maximum((b * n_r + r) * per - 1, 0), c))
    return blk, (left if with_left else None)


def _mix_fwd(z, conv_w, conv_b, ln_a_g, ln_a_b, ln_v_g, ln_v_b, ws, bias2d, n_seq):
    T = z.shape[0]
    S = T // n_seq
    ts = _tile(S, 512)
    n_r = S // ts
    shift = HALO - (CONV_WIDTH - 1)

    def body(av_ref, ag_ref, avh_ref, agh_ref, u_ref, v_ref, cw_ref, cb_ref, lag_ref, lab_ref,
             lvg_ref, lvb_ref, ws_ref, bias_ref, cat_ref, a1_ref, a0_s):
        _glu_with_halo(av_ref, ag_ref, avh_ref, agh_ref, a0_s, pl.program_id(1) == 0, ts)
        for rb in range(ts // CONV_ROWS):
            base = rb * CONV_ROWS
            acc = jnp.broadcast_to(cb_ref[...], (CONV_ROWS, CA))
            for k in range(CONV_WIDTH):
                acc = acc + cw_ref[k:k + 1, :] * _shifted(a0_s, base + shift + k, CONV_ROWS)
            a1_ref[base:base + CONV_ROWS, :] = acc
        y, _, _ = _ln_fwd(a1_ref[...], lag_ref[...], lab_ref[...])
        cat_ref[:, 0:CA] = (y * _sigmoid(y)).astype(BF16)
        for gi in range(GB):
            sl = slice(gi * DB, (gi + 1) * DB)
            v1, _, _ = _ln_fwd(v_ref[:, sl], lvg_ref[:, sl], lvb_ref[:, sl])
            v1 = v1.astype(BF16)
            for c in range(ts // CHUNK):
                rs = slice(c * CHUNK, (c + 1) * CHUNK)
                v2 = _dot(ws_ref[gi], v1[rs], NN) + bias_ref[:, sl]
                cat_ref[rs, CA + gi * DB:CA + (gi + 1) * DB] = (u_ref[rs, sl] * v2).astype(BF16)

    blk, left = _mix_specs(ts, n_r, True)
    vec = pl.BlockSpec((1, CA), lambda b, r: (0, 0))
    return pl.pallas_call(
        body, name="mix_fwd", grid=(n_seq, n_r),
        in_specs=[blk(0), blk(1), left(0), left(1), blk(2), blk(3),
                  pl.BlockSpec((CONV_WIDTH, CA), lambda b, r: (0, 0)), vec, vec, vec, vec, vec,
                  pl.BlockSpec((GB, CHUNK, CHUNK), lambda b, r: (0, 0, 0)),
                  pl.BlockSpec((CHUNK, CB), lambda b, r: (0, 0))],
        out_specs=[pl.BlockSpec((ts, CA + CB), lambda b, r: (b * n_r + r, 0)), blk(0)],
        out_shape=[jax.ShapeDtypeStruct((T, CA + CB), BF16), jax.ShapeDtypeStruct((T, CA), F32)],
        scratch_shapes=[pltpu.VMEM((SUBLANES, HALO + ts, CA), F32)],
        compiler_params=_cp("parallel", "parallel"),
    )(z, z, z, z, z, z, conv_w, conv_b, ln_a_g, ln_a_b, ln_v_g, ln_v_b, ws, bias2d)


def _mix_bwd_rows(dcat, z, a1, ln_a_g, ln_a_b, ln_v_g, ln_v_b, ws, ws_t, bias2d, n_seq):
    T = z.shape[0]
    S = T // n_seq
    ts = _tile(S, 512)
    n_r = S // ts

    def body(dc_ref, u_ref, v_ref, a1_ref, lag_ref, lab_ref, lvg_ref, lvb_ref, ws_ref, wst_ref, bias_ref,
             da1_ref, dz_ref, dlag_ref, dlab_ref, dlvg_ref, dlvb_ref, dws_ref, dsb_ref, dv1_s, dbias_s):
        first = (pl.program_id(0) == 0) & (pl.program_id(1) == 0)
        last = (pl.program_id(0) == n_seq - 1) & (pl.program_id(1) == n_r - 1)

        @pl.when(first)
        def _():
            for ref in (dlag_ref, dlab_ref, dlvg_ref, dlvb_ref, dws_ref, dbias_s):
                ref[...] = jnp.zeros_like(ref)

        lag = lag_ref[...]
        y, xh, r = _ln_fwd(a1_ref[...], lag, lab_ref[...])
        sig = _sigmoid(y)
        dy = dc_ref[:, 0:CA] * (sig * (1.0 + y * (1.0 - sig)))
        dlag_ref[...] += jnp.sum(dy * xh, axis=0, keepdims=True)
        dlab_ref[...] += jnp.sum(dy, axis=0, keepdims=True)
        da1_ref[...] = _ln_bwd(dy, xh, r, lag)

        tril = (lax.broadcasted_iota(jnp.int32, (CHUNK, CHUNK), 0)
                >= lax.broadcasted_iota(jnp.int32, (CHUNK, CHUNK), 1))
        for gi in range(GB):
            sl = slice(gi * DB, (gi + 1) * DB)
            lvg = lvg_ref[:, sl]
            v1, vh, vr = _ln_fwd(v_ref[:, sl], lvg, lvb_ref[:, sl])
            v1 = v1.astype(BF16)
            for c in range(ts // CHUNK):
                rs = slice(c * CHUNK, (c + 1) * CHUNK)
                v2 = _dot(ws_ref[gi], v1[rs], NN) + bias_ref[:, sl]
                dbo = dc_ref[rs, CA + gi * DB:CA + (gi + 1) * DB]
                dz_ref[rs, sl] = (dbo * v2).astype(BF16)
                dv2 = dbo * u_ref[rs, sl]
                dbias_s[:, sl] += dv2
                dv2b = dv2.astype(BF16)
                dws_ref[gi] += jnp.where(tril, _dot(dv2b, v1[rs], NT), 0.0)
                dv1_s[rs, :] = _dot(wst_ref[gi], dv2b, NN)
            dv1 = dv1_s[...]
            dlvg_ref[:, sl] += jnp.sum(dv1 * vh, axis=0, keepdims=True)
            dlvb_ref[:, sl] += jnp.sum(dv1, axis=0, keepdims=True)
            dz_ref[:, CB + gi * DB:CB + (gi + 1) * DB] = _ln_bwd(dv1, vh, vr, lvg).astype(BF16)

        @pl.when(last)
        def _():
            col = lax.broadcasted_iota(jnp.int32, (CHUNK, GB), 1)
            out = jnp.zeros((CHUNK, GB), F32)
            for gi in range(GB):
                s = jnp.sum(dbias_s[:, gi * DB:(gi + 1) * DB], axis=1, keepdims=True)
                out = out + jnp.where(col == gi, s, 0.0)
            dsb_ref[...] = out

    blk, _ = _mix_specs(ts, n_r, False)
    vec = pl.BlockSpec((1, CA), lambda b, r: (0, 0))
    mat = pl.BlockSpec((GB, CHUNK, CHUNK), lambda b, r: (0, 0, 0))
    wide = pl.BlockSpec((ts, CA + CB), lambda b, r: (b * n_r + r, 0))
    return pl.pallas_call(
        body, name="mix_bwd_rows", grid=(n_seq, n_r),
        in_specs=[wide, blk(2), blk(3), blk(0), vec, vec, vec, vec, mat, mat,
                  pl.BlockSpec((CHUNK, CB), lambda b, r: (0, 0))],
        out_specs=[blk(0), wide, vec, vec, vec, vec, mat, pl.BlockSpec((CHUNK, GB), lambda b, r: (0, 0))],
        out_shape=[jax.ShapeDtypeStruct((T, CA), F32), jax.ShapeDtypeStruct((T, 2 * CB), BF16)]
        + [jax.ShapeDtypeStruct((1, CA), F32)] * 4
        + [jax.ShapeDtypeStruct((GB, CHUNK, CHUNK), F32), jax.ShapeDtypeStruct((CHUNK, GB), F32)],
        scratch_shapes=[pltpu.VMEM((ts, DB), F32), pltpu.VMEM((CHUNK, CB), F32)],
        compiler_params=_cp("arbitrary", "arbitrary"),
    )(dcat, z, z, a1, ln_a_g, ln_a_b, ln_v_g, ln_v_b, ws, ws_t, bias2d)


def _mix_bwd_conv(da1, z, conv_w, n_seq):
    T = z.shape[0]
    S = T // n_seq
    ts = _tile(S, 512)
    n_r = S // ts
    per = ts // HALO
    shift = HALO - (CONV_WIDTH - 1)
    fold = CONV_ROWS // 8

    def body(d_ref, dh_ref, av_ref, ag_ref, avh_ref, agh_ref, cw_ref,
             dz_ref, dcw_ref, dcb_ref, a0_s, d1_s, da0_s, dw8_s):
        first = (pl.program_id(0) == 0) & (pl.program_id(1) == 0)
        last = (pl.program_id(0) == n_seq - 1) & (pl.program_id(1) == n_r - 1)

        @pl.when(first)
        def _():
            dw8_s[...] = jnp.zeros_like(dw8_s)
            dcb_ref[...] = jnp.zeros_like(dcb_ref)

        _glu_with_halo(av_ref, ag_ref, avh_ref, agh_ref, a0_s, pl.program_id(1) == 0, ts)
        d1_s[0, 0:ts, :] = d_ref[...]
        d1_s[0, ts:ts + HALO, :] = jnp.where(pl.program_id(1) == n_r - 1, 0.0, dh_ref[...])
        _shifted_copies(d1_s)
        dcb_ref[...] += jnp.sum(d_ref[...], axis=0, keepdims=True)
        for rb in range(ts // CONV_ROWS):
            base = rb * CONV_ROWS
            dcur = d1_s[0, base:base + CONV_ROWS, :]
            acc = jnp.zeros((CONV_ROWS, CA), F32)
            for k in range(CONV_WIDTH):
                back = CONV_WIDTH - 1 - k
                acc = acc + cw_ref[k:k + 1, :] * _shifted(d1_s, base + back, CONV_ROWS)
                prod = dcur * _shifted(a0_s, base + shift + k, CONV_ROWS)
                part = prod[0:8]
                for f in range(1, fold):
                    part = part + prod[8 * f:8 * f + 8]
                dw8_s[k] += part
            da0_s[base:base + CONV_ROWS, :] = acc
        da0 = da0_s[...]
        sig = _sigmoid(ag_ref[...])
        dz_ref[:, 0:CA] = (da0 * sig).astype(BF16)
        dz_ref[:, CA:2 * CA] = (da0 * av_ref[...] * sig * (1.0 - sig)).astype(BF16)

        @pl.when(last)
        def _():
            for k in range(CONV_WIDTH):
                dcw_ref[k:k + 1, :] = jnp.sum(dw8_s[k], axis=0, keepdims=True)

    blk, left = _mix_specs(ts, n_r, True)
    n_halo_blocks = T // HALO
    right = pl.BlockSpec((HALO, CA), lambda b, r: (jnp.minimum((b * n_r + r + 1) * per, n_halo_blocks - 1), 0))
    return pl.pallas_call(
        body, name="mix_bwd_conv", grid=(n_seq, n_r),
        in_specs=[blk(0), right, blk(0), blk(1), left(0), left(1),
                  pl.BlockSpec((CONV_WIDTH, CA), lambda b, r: (0, 0))],
        out_specs=[pl.BlockSpec((ts, 2 * CA), lambda b, r: (b * n_r + r, 0)),
                   pl.BlockSpec((CONV_WIDTH, CA), lambda b, r: (0, 0)), pl.BlockSpec((1, CA), lambda b, r: (0, 0))],
        out_shape=[jax.ShapeDtypeStruct((T, 2 * CA), BF16), jax.ShapeDtypeStruct((CONV_WIDTH, CA), F32),
                   jax.ShapeDtypeStruct((1, CA), F32)],
        scratch_shapes=[pltpu.VMEM((SUBLANES, HALO + ts, CA), F32), pltpu.VMEM((SUBLANES, ts + HALO, CA), F32),
                        pltpu.VMEM((ts, CA), F32), pltpu.VMEM((CONV_WIDTH, 8, CA), F32)],
        compiler_params=_cp("arbitrary", "arbitrary"),
    )(da1, da1, z, z, z, z, conv_w)


def _row_tile(R, want):
    t = min(R, want)
    t -= t % 8
    while t > 8 and R % t:
        t -= 8
    return t if t >= 8 and R % t == 0 else R


def _adam_step(w, g, m, v):
    nm = ADAM_B1 * m + (1.0 - ADAM_B1) * g
    nv = ADAM_B2 * v + (1.0 - ADAM_B2) * (g * g)
    m_hat = nm / (1.0 - ADAM_B1 ** ADAM_STEP)
    v_hat = nv / (1.0 - ADAM_B2 ** ADAM_STEP)
    return -ADAM_LR * (m_hat / (jnp.sqrt(v_hat) + ADAM_EPS) + ADAM_WD * w), nm, nv


def _adamw_parts(parts, w, m, v, name):
    L, n, C = w.shape
    assert len(parts) == L
    tr = _row_tile(n, 192)
    n_i = n // tr

    def body(*refs):
        p_refs = refs[:L]
        w_ref, m_ref, v_ref, g_ref, d_ref, nm_ref, nv_ref = refs[L:]
        for k in range(L):
            @pl.when(pl.program_id(0) == k)
            def _(k=k):
                acc = p_refs[k][0]
                for s in range(1, N_DEV):
                    acc = acc + p_refs[k][s]
                g_ref[...] = acc

        d_ref[...], nm_ref[...], nv_ref[...] = _adam_step(w_ref[...], g_ref[...], m_ref[...], v_ref[...])

    def part_spec(k):
        return pl.BlockSpec((N_DEV, tr, C),
                            lambda l, i: (0, jnp.where(l == k, i, jnp.where(l < k, 0, n_i - 1)), 0))

    blk = pl.BlockSpec((None, tr, C), lambda l, i: (l, i, 0))
    return pl.pallas_call(
        body, name=name, grid=(L, n_i),
        in_specs=[part_spec(k) for k in range(L)] + [blk] * 3, out_specs=[blk] * 4,
        out_shape=[jax.ShapeDtypeStruct((L, n, C), F32)] * 4,
        compiler_params=_cp("arbitrary", "arbitrary"),
    )(*parts, w, m, v)


def _adamw(w, g, m, v, name):
    R, C = w.shape
    tr = _row_tile(R, 256)

    def body(w_ref, g_ref, m_ref, v_ref, d_ref, nm_ref, nv_ref):
        d_ref[...], nm_ref[...], nv_ref[...] = _adam_step(w_ref[...], g_ref[...], m_ref[...], v_ref[...])

    blk = pl.BlockSpec((tr, C), lambda i: (i, 0))
    return pl.pallas_call(
        body, name=name, grid=(R // tr,),
        in_specs=[blk] * 4, out_specs=[blk] * 3,
        out_shape=[jax.ShapeDtypeStruct((R, C), F32)] * 3,
        compiler_params=_cp("parallel"),
    )(w, g, m, v)


def _me():
    return lax.axis_index("x"), lax.axis_index("y"), lax.axis_index("c")


def _block_rows(ref, dev, n):
    start = (4 * dev[0] + 2 * dev[1] + dev[2]) * n
    if len(ref.shape) == 2:
        return ref.at[pl.ds(start, n), :]
    return ref.at[:, pl.ds(start, n), :]


def _all_gather(shards):
    na = len(shards)
    ns = [s.shape[-2] for s in shards]

    def body(*refs):
        ins, outs = refs[:na], refs[na:2 * na]
        send_sems, recv_sems, local_sems = refs[2 * na:]
        x, y, c = _me()
        me, sibling = (x, y, c), (x, y, 1 - c)
        chips = [(1 - x, y), (x, 1 - y), (1 - x, 1 - y)]

        def copy(a, k, block, to, src=None):
            dst = _block_rows(outs[a], block, ns[a])
            return pltpu.make_async_remote_copy(
                src_ref=dst if src is None else src, dst_ref=dst,
                send_sem=send_sems.at[a, k], recv_sem=recv_sems.at[a, k], device_id=to, device_id_type=MESH)

        mine = [pltpu.make_async_copy(ins[a], _block_rows(outs[a], me, ns[a]), local_sems.at[a]) for a in range(na)]
        for cp in mine:
            cp.start()
        first = []
        for a in range(na):
            first.append(copy(a, 0, me, sibling, src=ins[a]))
            first += [copy(a, 1 + j, me, (*chip, c), src=ins[a]) for j, chip in enumerate(chips)]
        for cp in first:
            cp.start()
        passed = []
        for j, chip in enumerate(chips):
            for a in range(na):
                copy(a, 1 + j, (*chip, c), me).wait_recv()
                fwd = copy(a, 4 + j, (*chip, c), sibling)
                fwd.start()
                passed.append(fwd)
        for a in range(na):
            copy(a, 0, sibling, me).wait_recv()
            for j, chip in enumerate(chips):
                copy(a, 4 + j, (*chip, 1 - c), me).wait_recv()
        for cp in first + passed:
            cp.wait_send()
        for cp in mine:
            cp.wait()

    out_shape = [jax.ShapeDtypeStruct(s.shape[:-2] + (N_DEV * s.shape[-2], s.shape[-1]), s.dtype) for s in shards]
    return pl.pallas_call(
        body, name="weights_all_gather",
        in_specs=[ANY] * na, out_specs=[ANY] * na, out_shape=out_shape,
        scratch_shapes=[pltpu.SemaphoreType.DMA((na, 7)), pltpu.SemaphoreType.DMA((na, 7)),
                        pltpu.SemaphoreType.DMA((na,))],
    )(*shards)


def _split_copies(gather, srcs, lands, send_sems, recv_sems, ns):
    x, y, c = _me()
    me = (x, y, c)
    my_slot = 4 * x + 2 * y + c
    copies = []
    for mask in range(1, N_DEV):
        peer = (x ^ (mask >> 2), y ^ ((mask >> 1) & 1), c ^ (mask & 1))
        for a in range(len(srcs)):
            if gather:
                src, dst = srcs[a], _block_rows(lands[a], me, ns[a])
            else:
                src, dst = _block_rows(srcs[a], peer, ns[a]), lands[a].at[my_slot]
            sem = a * (N_DEV - 1) + mask - 1
            copies.append(pltpu.make_async_remote_copy(
                src_ref=src, dst_ref=dst, send_sem=send_sems.at[sem], recv_sem=recv_sems.at[sem],
                device_id=peer, device_id_type=MESH))
    return copies


HBM_SPEC = pl.BlockSpec(memory_space=pltpu.HBM)
SEM_SPEC = pl.BlockSpec(memory_space=pltpu.SEMAPHORE)


def _split_start(gather, srcs, name, dep=None):
    na = len(srcs)
    if gather:
        ns = [s.shape[-2] for s in srcs]
        lands = [lax.empty(s.shape[:-2] + (N_DEV * s.shape[-2], s.shape[-1]), s.dtype) for s in srcs]
    else:
        ns = [s.shape[-2] // N_DEV for s in srcs]
        lands = [lax.empty((N_DEV, s.shape[-2] // N_DEV, s.shape[-1]), s.dtype) for s in srcs]
    n_in = 2 * na + (dep is not None)

    def body(*refs):
        send_sems, recv_sems = refs[n_in], refs[n_in + 1]
        for cp in _split_copies(gather, refs[:na], refs[na:2 * na], send_sems, recv_sems, ns):
            cp.start()
        refs[-1][...] = jnp.zeros_like(refs[-1])

    hbm = lambda a: pltpu.with_memory_space_constraint(a, pltpu.HBM)
    args = [hbm(a) for a in srcs] + [hbm(a) for a in lands] + ([dep] if dep is not None else [])
    out = pl.pallas_call(
        body, name=name,
        in_specs=[HBM_SPEC] * (2 * na) + ([ANY] if dep is not None else []),
        out_specs=[SEM_SPEC, SEM_SPEC] + [HBM_SPEC] * (2 * na) + [pl.BlockSpec(memory_space=pltpu.VMEM)],
        out_shape=[pltpu.SemaphoreType.DMA((na * (N_DEV - 1),)), pltpu.SemaphoreType.DMA((na * (N_DEV - 1),))]
        + [pltpu.HBM(a.shape, a.dtype) for a in srcs + lands] + [jax.ShapeDtypeStruct((8, LANES), F32)],
        input_output_aliases={i: 2 + i for i in range(2 * na)},
        compiler_params=pltpu.CompilerParams(has_side_effects=pltpu.SideEffectType.DATAFLOW_SIDE_EFFECTING),
    )(*args)
    return (gather, ns, out[0], out[1], list(out[2:2 + na]), list(out[2 + na:2 + 2 * na])), out[-1]


def _split_wait(handle, after, name):
    gather, ns, send, recv, srcs, lands = handle
    na = len(srcs)

    def body(*refs):
        send_sems, recv_sems = refs[2 * na], refs[2 * na + 1]
        for cp in _split_copies(gather, refs[:na], refs[na:2 * na], send_sems, recv_sems, ns):
            cp.wait_send()
            cp.wait_recv()

    out = pl.pallas_call(
        body, name=name,
        in_specs=[HBM_SPEC] * (2 * na) + [SEM_SPEC, SEM_SPEC, ANY],
        out_specs=[HBM_SPEC] * (2 * na),
        out_shape=[pltpu.HBM(a.shape, a.dtype) for a in srcs + lands],
        input_output_aliases={i: i for i in range(2 * na)},
        compiler_params=pltpu.CompilerParams(has_side_effects=pltpu.SideEffectType.DATAFLOW_SIDE_EFFECTING),
    )(*srcs, *lands, send, recv, after)
    return list(out[:na]), list(out[na:])


def _small_all_reduce(buf):
    R = buf.shape[0]

    def body(b_ref, o_ref, recv_ref, send_sems, recv_sems):
        x, y, c = _me()
        my_slot = 4 * x + 2 * y + c
        recv_ref[my_slot] = b_ref[...]
        copies = []
        for mask in range(1, N_DEV):
            peer = (x ^ (mask >> 2), y ^ ((mask >> 1) & 1), c ^ (mask & 1))
            copies.append(pltpu.make_async_remote_copy(
                src_ref=b_ref, dst_ref=recv_ref.at[my_slot],
                send_sem=send_sems.at[mask - 1], recv_sem=recv_sems.at[mask - 1],
                device_id=peer, device_id_type=MESH))
        for cp in copies:
            cp.start()
        for cp in copies:
            cp.wait()
        acc = recv_ref[0]
        for k in range(1, N_DEV):
            acc = acc + recv_ref[k]
        o_ref[...] = acc

    return pl.pallas_call(
        body, name="small_all_reduce",
        in_specs=[pl.BlockSpec(memory_space=pltpu.VMEM)], out_specs=pl.BlockSpec(memory_space=pltpu.VMEM),
        out_shape=jax.ShapeDtypeStruct((R, LANES), F32),
        scratch_shapes=[pltpu.VMEM((N_DEV, R, LANES), F32), pltpu.SemaphoreType.DMA((7,)),
                        pltpu.SemaphoreType.DMA((7,))],
        compiler_params=pltpu.CompilerParams(vmem_limit_bytes=VMEM_LIMIT),
    )(buf)


def _pack(arrays):
    flat = jnp.concatenate([a.reshape(-1) for a in arrays])
    pad = (-flat.shape[0]) % (8 * LANES)
    return jnp.pad(flat, (0, pad)).reshape(-1, LANES)


def _unpack(buf, shapes):
    flat = buf.reshape(-1)
    out, off = [], 0
    for s in shapes:
        n = 1
        for d in s:
            n *= d
        out.append(flat[off:off + n].reshape(s))
        off += n
    return out


def _ffn_index(layer, second):
    return (2 * layer + second) * 3


def kernel(x, g_ffn1, w_ffn1_gate, w_ffn1_up, w_ffn1_down, g_mix, w_in_ab, conv_w, conv_b, ln_a_g, ln_a_b, ln_v_g, ln_v_b, sp_w, sp_b, w_out_ab, w_qkv, w_o, g_ffn2, w_ffn2_gate, w_ffn2_up, w_ffn2_down, g_final, loss_target, m_g_ffn1, m_w_ffn1_gate, m_w_ffn1_up, m_w_ffn1_down, m_g_mix, m_w_in_ab, m_conv_w, m_conv_b, m_ln_a_g, m_ln_a_b, m_ln_v_g, m_ln_v_b, m_sp_w, m_sp_b, m_w_out_ab, m_w_qkv, m_w_o, m_g_ffn2, m_w_ffn2_gate, m_w_ffn2_up, m_w_ffn2_down, m_g_final, v_g_ffn1, v_w_ffn1_gate, v_w_ffn1_up, v_w_ffn1_down, v_g_mix, v_w_in_ab, v_conv_w, v_conv_b, v_ln_a_g, v_ln_a_b, v_ln_v_g, v_ln_v_b, v_sp_w, v_sp_b, v_w_out_ab, v_w_qkv, v_w_o, v_g_ffn2, v_w_ffn2_gate, v_w_ffn2_up, v_w_ffn2_down, v_g_final):
    n_seq, S, D = x.shape
    T = n_seq * S
    depth = g_ffn1.shape[0]
    assert depth == 2 and D == D_MODEL
    my_block = 4 * lax.axis_index("x") + 2 * lax.axis_index("y") + lax.axis_index("c")

    ffn_parts = []
    for l in range(depth):
        for gate, up, down in ((w_ffn1_gate, w_ffn1_up, w_ffn1_down), (w_ffn2_gate, w_ffn2_up, w_ffn2_down)):
            ffn_parts += [gate[l].T, up[l].T, down[l]]
    ffn_shard = lambda k: jnp.stack(ffn_parts[3 * k:3 * k + 3]).astype(BF16)
    conv_w_pad = jnp.zeros((HALO, conv_w.shape[2]), F32).at[:CONV_WIDTH].set(conv_w[0]).T
    w_ffn = [None] * (2 * depth)
    w_ffn[0], conv_w_t = _all_gather([ffn_shard(0), conv_w_pad])
    conv_w_full = conv_w_t.T[:CONV_WIDTH]
    shards_b = [w_out_ab[0].astype(BF16), ffn_shard(1)]
    shards_c = [ffn_shard(2), w_qkv[0].T.astype(BF16), w_o[0].astype(BF16), ffn_shard(3)]
    gather_a, token = _split_start(True, [w_in_ab[0].T.astype(BF16)], "gather_a_start", dep=conv_w_t)
    gather_b, token = _split_start(True, shards_b, "gather_b_start", dep=token)
    gather_c, token = _split_start(True, shards_c, "gather_c_start", dep=token)

    def gathered(handle, after, name):
        shards, lands = _split_wait(handle, after, name)
        out = []
        for land, shard in zip(lands, shards):
            n = shard.shape[-2]
            start = (0,) * (land.ndim - 2) + (my_block * n, 0)
            out.append(lax.dynamic_update_slice(land, shard, start))
        return out

    row = lambda a: a.reshape(1, -1)
    tril = jnp.tril(jnp.ones((CHUNK, CHUNK), dtype=bool))
    ws = jnp.where(tril[None], sp_w[0], 0.0).astype(BF16)
    ws_t = jnp.swapaxes(ws, 1, 2)
    bias2d = jnp.repeat(sp_b[0].T, DB, axis=1)
    conv_b2, lag, lab = row(conv_b[0]), row(ln_a_g[0]), row(ln_a_b[0])
    lvg, lvb = row(ln_v_g[0]), row(ln_v_b[0])

    x0 = x.reshape(T, D)
    target = loss_target.reshape(T, D)
    saved = []
    xc = x0
    for l in range(depth):
        xa, a1, b1, h1 = _ffn_fwd(xc, row(g_ffn1[l]), w_ffn[2 * l], 0, f"ffn1_fwd_{l}", dep=token)
        if l % 2 == 0:
            w_in_t, = gathered(gather_a, xa, "gather_a_wait")
            z, hm = _mm_nt(xa, w_in_t, row(g_mix[l]), F32, "mix_in_proj")
            cat, conv_out = _mix_fwd(z, conv_w_full, conv_b2, lag, lab, lvg, lvb, ws, bias2d, n_seq)
            w_out, w_ffn[1] = gathered(gather_b, cat, "gather_b_wait")
            xb = _mm_nn_res(cat, w_out, xa, "mix_out_proj")
            mixer = (z, hm, cat, conv_out)
        else:
            qkv, hm = _mm_nt(xa, w_qkv_t, row(g_mix[l]), BF16, "qkv_proj")
            o, tot = _attn_fwd(qkv, n_seq)
            xb = _mm_nn_res(o, w_o_full, xa, "attn_out_proj")
            mixer = (qkv, hm, o, tot)
        xn, a2, b2, h2 = _ffn_fwd(xb, row(g_ffn2[l]), w_ffn[2 * l + 1], 0, f"ffn2_fwd_{l}")
        saved.append((xc, a1, b1, h1, xa, mixer, xb, a2, b2, h2))
        xc = xn
        if l == 0:
            w_ffn[2], w_qkv_t, w_o_full, w_ffn[3] = gathered(gather_c, xc, "gather_c_wait")

    g, loss_part, dg_final = _loss_head(xc, row(g_final), target)

    dg_ffn1, dg_ffn2, dg_mix = [None] * depth, [None] * depth, [None] * depth
    exchanges = {}
    token = None

    def ffn_back(g, xin, gvec, a, b, h, k, tag, token):
        g, dg, da, db, s, gh = _ffn_bwd(g, xin, gvec, a, b, w_ffn[k], 0, f"ffn{tag}_bwd", dep=token)
        if k == 0:
            return g, dg, (da, db, s, gh, h)
        dws = [_mm_tn(da, h, f"dw_gate{tag}"), _mm_tn(db, h, f"dw_up{tag}"), _mm_tn(s, gh, f"dw_down{tag}")]
        exchanges[f"ffn{k}"], token = _split_start(False, dws, f"exchange_ffn{tag}_start")
        return g, dg, token

    for l in reversed(range(depth)):
        xin, a1, b1, h1, xa, mixer, xb, a2, b2, h2 = saved[l]
        g, dg_ffn2[l], token = ffn_back(g, xb, row(g_ffn2[l]), a2, b2, h2, 2 * l + 1, f"2_{l}", token)
        if l % 2 == 0:
            z, hm, cat, conv_out = mixer
            dcat = _mm_nt(g, w_out, None, F32, "mix_out_bwd", dep=token)
            d_w_out = _mm_tn(cat, g, "dw_out")
            (da1, dz_uv, d_lag, d_lab, d_lvg, d_lvb, d_ws, d_sb) = _mix_bwd_rows(
                dcat, z, conv_out, lag, lab, lvg, lvb, ws, ws_t, bias2d, n_seq)
            dz_a, d_cw, d_cb = _mix_bwd_conv(da1, z, conv_w_full, n_seq)
            d_w_in_t = jnp.concatenate([_mm_tn(dz_a, hm, "dw_in_a"), _mm_tn(dz_uv, hm, "dw_in_uv")])
            exchanges["mix"], token = _split_start(False, [d_w_out, d_w_in_t], "exchange_mix_start")
            g, dg_mix[l] = _mm_nn_rmsbwd([dz_a, dz_uv], w_in_t, xa, row(g_mix[l]), g, "mix_in_bwd", dep=token)
        else:
            qkv, hm, o, tot = mixer
            do = _mm_nt(g, w_o_full, None, BF16, "attn_out_bwd", dep=token)
            d_w_o = _mm_tn(o, g, "dw_o")
            dq, dk, dv = _attn_bwd(qkv, do, tot, n_seq)
            d_w_qkv_t = jnp.concatenate([_mm_tn(dq, hm, "dw_q"), _mm_tn(dk, hm, "dw_k"), _mm_tn(dv, hm, "dw_v")])
            exchanges["attn"], token = _split_start(False, [d_w_o, d_w_qkv_t], "exchange_attn_start")
            g, dg_mix[l] = _mm_nn_rmsbwd([dq, dk, dv], w_qkv_t, xa, row(g_mix[l]), g, "qkv_bwd", dep=token)
        g, dg_ffn1[l], token = ffn_back(g, xin, row(g_ffn1[l]), a1, b1, h1, 2 * l, f"1_{l}", token)
    grad_x = g.reshape(n_seq, S, D)

    small = [jnp.concatenate(dg_ffn1), jnp.concatenate(dg_mix), d_cw, d_cb, d_lag, d_lab, d_lvg, d_lvb,
             jnp.where(tril[None], d_ws, 0.0), d_sb.T, jnp.concatenate(dg_ffn2), dg_final, loss_part[:, :1]]
    small_shapes = [(depth, D), (depth, D), (CONV_WIDTH, CA), (1, CA), (1, CA), (1, CA), (1, GB, DB), (1, GB, DB),
                    (1, GB, CHUNK, CHUNK), (1, GB, CHUNK), (depth, D), (D,), ()]
    small_sum = _small_all_reduce(_pack(small))
    red = _unpack(small_sum, small_shapes)
    (gr_g_ffn1, gr_g_mix, gr_cw_full, gr_cb, gr_lag, gr_lab, gr_lvg, gr_lvb, gr_sp_w, gr_sp_b,
     gr_g_ffn2, gr_g_final, loss) = red
    n_cw = conv_w.shape[2]
    gr_cw = lax.dynamic_slice(gr_cw_full, (0, my_block * n_cw), (CONV_WIDTH, n_cw))[None]

    da, db, s, gh, h = token
    token = small_sum
    for which, lhs, rhs in ((2, s, gh), (1, db, h), (0, da, h)):
        dw = _mm_tn(lhs, rhs, f"dw_ffn0_{which}", dep=token)
        exchanges[f"ffn0_{which}"], token = _split_start(False, [dw], f"exchange_ffn0_{which}_start")

    def landed(key, after):
        out = []
        fulls, lands = _split_wait(exchanges[key], after, f"exchange_{key}_wait")
        for land, full in zip(lands, fulls):
            n = land.shape[1]
            own = lax.dynamic_slice(full, (my_block * n, 0), (n, full.shape[1]))
            out.append(lax.dynamic_update_slice(land, own[None], (my_block, 0, 0)))
        return out

    parts_ffn = [None] * (6 * depth)
    for k in range(1, 2 * depth):
        parts_ffn[3 * k:3 * k + 3] = landed(f"ffn{k}", token)
    parts_out, parts_in = landed("mix", token)
    parts_o, parts_qkv = landed("attn", token)

    grads = {
        "g_ffn1": gr_g_ffn1, "g_mix": gr_g_mix, "conv_w": gr_cw, "conv_b": gr_cb, "ln_a_g": gr_lag,
        "ln_a_b": gr_lab, "ln_v_g": gr_lvg, "ln_v_b": gr_lvb, "sp_w": gr_sp_w, "sp_b": gr_sp_b,
        "g_ffn2": gr_g_ffn2, "g_final": gr_g_final,
    }
    weights = dict(g_ffn1=g_ffn1, w_ffn1_gate=w_ffn1_gate, w_ffn1_up=w_ffn1_up, w_ffn1_down=w_ffn1_down, g_mix=g_mix,
                   w_in_ab=w_in_ab, conv_w=conv_w, conv_b=conv_b, ln_a_g=ln_a_g, ln_a_b=ln_a_b, ln_v_g=ln_v_g,
                   ln_v_b=ln_v_b, sp_w=sp_w, sp_b=sp_b, w_out_ab=w_out_ab, w_qkv=w_qkv, w_o=w_o, g_ffn2=g_ffn2,
                   w_ffn2_gate=w_ffn2_gate, w_ffn2_up=w_ffn2_up, w_ffn2_down=w_ffn2_down, g_final=g_final)
    m_in = dict(g_ffn1=m_g_ffn1, w_ffn1_gate=m_w_ffn1_gate, w_ffn1_up=m_w_ffn1_up, w_ffn1_down=m_w_ffn1_down,
                g_mix=m_g_mix, w_in_ab=m_w_in_ab, conv_w=m_conv_w, conv_b=m_conv_b, ln_a_g=m_ln_a_g, ln_a_b=m_ln_a_b,
                ln_v_g=m_ln_v_g, ln_v_b=m_ln_v_b, sp_w=m_sp_w, sp_b=m_sp_b, w_out_ab=m_w_out_ab, w_qkv=m_w_qkv,
                w_o=m_w_o, g_ffn2=m_g_ffn2, w_ffn2_gate=m_w_ffn2_gate, w_ffn2_up=m_w_ffn2_up,
                w_ffn2_down=m_w_ffn2_down, g_final=m_g_final)
    v_in = dict(g_ffn1=v_g_ffn1, w_ffn1_gate=v_w_ffn1_gate, w_ffn1_up=v_w_ffn1_up, w_ffn1_down=v_w_ffn1_down,
                g_mix=v_g_mix, w_in_ab=v_w_in_ab, conv_w=v_conv_w, conv_b=v_conv_b, ln_a_g=v_ln_a_g, ln_a_b=v_ln_a_b,
                ln_v_g=v_ln_v_g, ln_v_b=v_ln_v_b, sp_w=v_sp_w, sp_b=v_sp_b, w_out_ab=v_w_out_ab, w_qkv=v_w_qkv,
                w_o=v_w_o, g_ffn2=v_g_ffn2, w_ffn2_gate=v_w_ffn2_gate, w_ffn2_up=v_w_ffn2_up,
                w_ffn2_down=v_w_ffn2_down, g_final=v_g_final)
    names = list(weights)
    grads = {n: grads[n].reshape(weights[n].shape) for n in grads}

    delta, new_m, new_v = {}, {}, {}

    def adamw_big(n, parts):
        if weights[n].shape[-1] == D:
            view = back = lambda a: a
        else:
            view = back = lambda a: jnp.swapaxes(a, 1, 2)
        out = _adamw_parts(parts, view(weights[n]), view(m_in[n]), view(v_in[n]), f"adamw_{n}")
        grads[n], delta[n], new_m[n], new_v[n] = [back(a) for a in out]

    adamw_big("w_in_ab", [parts_in])
    adamw_big("w_out_ab", [parts_out])
    adamw_big("w_qkv", [parts_qkv])
    adamw_big("w_o", [parts_o])
    kinds = ("gate", "up", "down")
    for which, kind in enumerate(kinds):
        adamw_big(f"w_ffn2_{kind}", [parts_ffn[_ffn_index(l, 1) + which] for l in range(depth)])
    big = [n for n in names if n.startswith("w_")]
    after = jnp.concatenate([delta[n].reshape(-1)[:1] for n in big if n in delta]).reshape(1, -1)
    for which in (2, 1, 0):
        parts_ffn[which], = landed(f"ffn0_{which}", after)
    for which, kind in enumerate(kinds):
        adamw_big(f"w_ffn1_{kind}", [parts_ffn[_ffn_index(l, 0) + which] for l in range(depth)])
    little = [n for n in names if n not in big]
    shapes = [weights[n].shape for n in little]
    d, nm, nv = _adamw(_pack([weights[n] for n in little]), _pack([grads[n] for n in little]),
                       _pack([m_in[n] for n in little]), _pack([v_in[n] for n in little]), "adamw_small")
    for n, dd, mm, vv in zip(little, _unpack(d, shapes), _unpack(nm, shapes), _unpack(nv, shapes)):
        delta[n], new_m[n], new_v[n] = dd, mm, vv

    return (loss, grad_x, *[grads[n] for n in names], *[delta[n] for n in names],
            *[new_m[n] for n in names], *[new_v[n] for n in names])
```

```python
import functools

import jax
import jax.numpy as jnp
from jax import lax
from jax.experimental import pallas as pl
from jax.experimental.pallas import tpu as pltpu

F32 = jnp.float32
BF16 = jnp.bfloat16

D_MODEL = 1024
CA = 512
CB = 512
GB = 4
DB = 128
CHUNK = 128
CONV_WIDTH = 31
N_HEADS = 16
HEAD_DIM = 64
EPS = 1e-6
N_DEV = 8
LANES = 128
SUBLANES = 8
QB = 128
ATT_TQ = 1024
FFN_TN = 1408
HALO = 32
CONV_ROWS = 32
ATT_SCALE = HEAD_DIM ** -0.5

ADAM_LR = 0.001
ADAM_B1 = 0.9
ADAM_B2 = 0.999
ADAM_EPS = 1e-08
ADAM_WD = 0.01
ADAM_STEP = 10

NT = (((1,), (1,)), ((), ()))
NN = (((1,), (0,)), ((), ()))
TN = (((0,), (0,)), ((), ()))
MESH = pl.DeviceIdType.MESH
ANY = pl.BlockSpec(memory_space=pl.ANY)
VMEM_LIMIT = 60 * 1024 * 1024


def _dot(a, b, dims):
    return lax.dot_general(a, b, dims, preferred_element_type=F32)


def _cp(*sem):
    return pltpu.CompilerParams(dimension_semantics=sem, vmem_limit_bytes=VMEM_LIMIT)


def _pcall(body, *, in_specs, args, dep=None, **kw):
    if dep is not None:
        n_in = len(in_specs)
        inner = body

        def body(*refs):
            inner(*refs[:n_in], *refs[n_in + 1:])

        in_specs = list(in_specs) + [ANY]
        args = tuple(args) + (dep,)
    return pl.pallas_call(body, in_specs=list(in_specs), **kw)(*args)


def _tile(n, want):
    if n <= want:
        return n
    t = want - want % LANES
    while t > LANES and n % t:
        t -= LANES
    assert n % t == 0, (n, want)
    return t


def _sigmoid(x):
    return 0.5 * jnp.tanh(0.5 * x) + 0.5


def _rstd(x):
    return lax.rsqrt(jnp.mean(x * x, axis=-1, keepdims=True) + EPS)


def _rms_bwd(x, g, dh):
    r = _rstd(x)
    u = dh * g
    dx = r * (u - x * (r * r) * jnp.mean(u * x, axis=-1, keepdims=True))
    dg = jnp.sum(dh * x * r, axis=0, keepdims=True)
    return dx, dg


def _ln_fwd(x, g, b):
    mu = jnp.mean(x, axis=-1, keepdims=True)
    xc = x - mu
    r = lax.rsqrt(jnp.mean(xc * xc, axis=-1, keepdims=True) + EPS)
    xh = xc * r
    return xh * g + b, xh, r


def _ln_bwd(dy, xh, r, g):
    dxh = dy * g
    return r * (dxh - jnp.mean(dxh, axis=-1, keepdims=True)
                - xh * jnp.mean(dxh * xh, axis=-1, keepdims=True))


def _ffn_fwd(x, g, wall, base, name, dep=None):
    T, D = x.shape
    F = wall.shape[1]
    tm, tn = _tile(T, 512), _tile(F, FFN_TN)
    n_j = F // tn

    def body(x_ref, g_ref, wg_ref, wu_ref, wd_ref, xo_ref, a_ref, b_ref, h_ref, acc_ref):
        j = pl.program_id(1)

        @pl.when(j == 0)
        def _():
            xv = x_ref[...]
            h_ref[...] = (xv * _rstd(xv) * g_ref[...]).astype(BF16)
            acc_ref[...] = jnp.zeros_like(acc_ref)

        h = h_ref[...]
        a = _dot(h, wg_ref[...], NT)
        b = _dot(h, wu_ref[...], NT)
        a_ref[...] = a.astype(BF16)
        b_ref[...] = b.astype(BF16)
        s = (a * _sigmoid(a) * b).astype(BF16)
        acc_ref[...] += _dot(s, wd_ref[...], NN)

        @pl.when(j == n_j - 1)
        def _():
            xo_ref[...] = x_ref[...] + 0.5 * acc_ref[...]

    wspec = lambda k: pl.BlockSpec((None, tn, D), lambda i, j: (base + k, j, 0))
    return _pcall(
        body, name=name, grid=(T // tm, n_j), dep=dep, args=(x, g, wall, wall, wall),
        in_specs=[pl.BlockSpec((tm, D), lambda i, j: (i, 0)), pl.BlockSpec((1, D), lambda i, j: (0, 0)),
                  wspec(0), wspec(1), wspec(2)],
        out_specs=[pl.BlockSpec((tm, D), lambda i, j: (i, 0)), pl.BlockSpec((tm, tn), lambda i, j: (i, j)),
                   pl.BlockSpec((tm, tn), lambda i, j: (i, j)), pl.BlockSpec((tm, D), lambda i, j: (i, 0))],
        out_shape=[jax.ShapeDtypeStruct((T, D), F32), jax.ShapeDtypeStruct((T, F), BF16),
                   jax.ShapeDtypeStruct((T, F), BF16), jax.ShapeDtypeStruct((T, D), BF16)],
        scratch_shapes=[pltpu.VMEM((tm, D), F32)],
        compiler_params=_cp("parallel", "arbitrary"),
    )


def _ffn_bwd(go, x, g, a, b, wall, base, name, dep=None):
    T, D = x.shape
    F = wall.shape[1]
    tm, tn = _tile(T, 512), _tile(F, FFN_TN)
    n_j = F // tn

    def body(go_ref, x_ref, g_ref, a_ref, b_ref, wg_ref, wu_ref, wd_ref,
             gx_ref, dg_ref, da_ref, db_ref, s_ref, gh_ref, acc_ref):
        i, j = pl.program_id(0), pl.program_id(1)

        @pl.when(j == 0)
        def _():
            gh_ref[...] = (0.5 * go_ref[...]).astype(BF16)
            acc_ref[...] = jnp.zeros_like(acc_ref)

        @pl.when((i == 0) & (j == 0))
        def _():
            dg_ref[...] = jnp.zeros_like(dg_ref)

        ds = _dot(gh_ref[...], wd_ref[...], NT)
        av = a_ref[...].astype(F32)
        bv = b_ref[...].astype(F32)
        sig = _sigmoid(av)
        sl = av * sig
        dab = ((ds * bv) * (sig + sl * (1.0 - sig))).astype(BF16)
        dbb = (ds * sl).astype(BF16)
        s_ref[...] = (sl * bv).astype(BF16)
        da_ref[...] = dab
        db_ref[...] = dbb
        acc_ref[...] += _dot(dab, wg_ref[...], NN) + _dot(dbb, wu_ref[...], NN)

        @pl.when(j == n_j - 1)
        def _():
            dx, dg = _rms_bwd(x_ref[...], g_ref[...], acc_ref[...])
            gx_ref[...] = go_ref[...] + dx
            dg_ref[...] += dg

    wspec = lambda k: pl.BlockSpec((None, tn, D), lambda i, j: (base + k, j, 0))
    row = pl.BlockSpec((tm, D), lambda i, j: (i, 0))
    hid = pl.BlockSpec((tm, tn), lambda i, j: (i, j))
    vec = pl.BlockSpec((1, D), lambda i, j: (0, 0))
    return _pcall(
        body, name=name, grid=(T // tm, n_j), dep=dep, args=(go, x, g, a, b, wall, wall, wall),
        in_specs=[row, row, vec, hid, hid, wspec(0), wspec(1), wspec(2)],
        out_specs=[row, vec, hid, hid, hid, row],
        out_shape=[jax.ShapeDtypeStruct((T, D), F32), jax.ShapeDtypeStruct((1, D), F32),
                   jax.ShapeDtypeStruct((T, F), BF16), jax.ShapeDtypeStruct((T, F), BF16),
                   jax.ShapeDtypeStruct((T, F), BF16), jax.ShapeDtypeStruct((T, D), BF16)],
        scratch_shapes=[pltpu.VMEM((tm, D), F32)],
        compiler_params=_cp("arbitrary", "arbitrary"),
    )


def _mm_tn(a, b, name, dep=None):
    T, M = a.shape
    N = b.shape[1]
    tmm, tk = _tile(M, 1536), _tile(T, 1024)

    def body(a_ref, b_ref, o_ref):
        @pl.when(pl.program_id(1) == 0)
        def _():
            o_ref[...] = jnp.zeros_like(o_ref)

        o_ref[...] += _dot(a_ref[...].astype(BF16), b_ref[...].astype(BF16), TN)

    return _pcall(
        body, name=name, grid=(M // tmm, T // tk), dep=dep, args=(a, b),
        in_specs=[pl.BlockSpec((tk, tmm), lambda m, k: (k, m)), pl.BlockSpec((tk, N), lambda m, k: (k, 0))],
        out_specs=pl.BlockSpec((tmm, N), lambda m, k: (m, 0)),
        out_shape=jax.ShapeDtypeStruct((M, N), F32),
        compiler_params=_cp("parallel", "arbitrary"),
    )


def _mm_nt(x, wt, g, out_dtype, name, dep=None):
    T, K = x.shape
    N = wt.shape[0]
    tm, tn = _tile(T, 512), _tile(N, 1536)
    norm = g is not None

    def body(*refs):
        if norm:
            x_ref, g_ref, w_ref, o_ref, h_ref = refs
        else:
            x_ref, w_ref, o_ref, h_ref = refs

        @pl.when(pl.program_id(1) == 0)
        def _():
            xv = x_ref[...].astype(F32)
            if norm:
                xv = xv * _rstd(xv) * g_ref[...]
            h_ref[...] = xv.astype(BF16)

        o_ref[...] = _dot(h_ref[...], w_ref[...], NT).astype(out_dtype)

    row = pl.BlockSpec((tm, K), lambda i, j: (i, 0))
    wsp = pl.BlockSpec((tn, K), lambda i, j: (j, 0))
    osp = pl.BlockSpec((tm, tn), lambda i, j: (i, j))
    if norm:
        return pl.pallas_call(
            body, name=name, grid=(T // tm, N // tn),
            in_specs=[row, pl.BlockSpec((1, K), lambda i, j: (0, 0)), wsp],
            out_specs=[osp, row],
            out_shape=[jax.ShapeDtypeStruct((T, N), out_dtype), jax.ShapeDtypeStruct((T, K), BF16)],
            compiler_params=_cp("parallel", "arbitrary"),
        )(x, g, wt)
    return _pcall(
        body, name=name, grid=(T // tm, N // tn), dep=dep, args=(x, wt),
        in_specs=[row, wsp], out_specs=osp,
        out_shape=jax.ShapeDtypeStruct((T, N), out_dtype),
        scratch_shapes=[pltpu.VMEM((tm, K), BF16)],
        compiler_params=_cp("parallel", "arbitrary"),
    )


def _mm_nn_res(act, w, resid, name):
    T, K = act.shape
    D = w.shape[1]
    tm = _tile(T, 512)

    def body(a_ref, w_ref, r_ref, o_ref):
        o_ref[...] = r_ref[...] + _dot(a_ref[...].astype(BF16), w_ref[...], NN)

    return pl.pallas_call(
        body, name=name, grid=(T // tm,),
        in_specs=[pl.BlockSpec((tm, K), lambda i: (i, 0)), pl.BlockSpec((K, D), lambda i: (0, 0)),
                  pl.BlockSpec((tm, D), lambda i: (i, 0))],
        out_specs=pl.BlockSpec((tm, D), lambda i: (i, 0)),
        out_shape=jax.ShapeDtypeStruct((T, D), F32),
        compiler_params=_cp("parallel"),
    )(act, w, resid)


def _mm_nn_rmsbwd(acts, w, x, g, gprev, name, dep=None):
    T = acts[0].shape[0]
    ks = [a.shape[1] for a in acts]
    K, D = w.shape
    assert sum(ks) == K
    tm = _tile(T, 512)
    na = len(acts)

    def body(*refs):
        a_refs = refs[:na]
        w_ref, x_ref, g_ref, gp_ref, o_ref, dg_ref = refs[na:]

        @pl.when(pl.program_id(0) == 0)
        def _():
            dg_ref[...] = jnp.zeros_like(dg_ref)

        dh, off = None, 0
        for a_ref, k in zip(a_refs, ks):
            part = _dot(a_ref[...].astype(BF16), w_ref[off:off + k, :], NN)
            dh = part if dh is None else dh + part
            off += k
        dx, dg = _rms_bwd(x_ref[...], g_ref[...], dh)
        o_ref[...] = gp_ref[...] + dx
        dg_ref[...] += dg

    row = pl.BlockSpec((tm, D), lambda i: (i, 0))
    vec = pl.BlockSpec((1, D), lambda i: (0, 0))
    return _pcall(
        body, name=name, grid=(T // tm,), dep=dep, args=(*acts, w, x, g, gprev),
        in_specs=[pl.BlockSpec((tm, k), lambda i: (i, 0)) for k in ks]
        + [pl.BlockSpec((K, D), lambda i: (0, 0)), row, vec, row],
        out_specs=[row, vec],
        out_shape=[jax.ShapeDtypeStruct((T, D), F32), jax.ShapeDtypeStruct((1, D), F32)],
        compiler_params=_cp("arbitrary"),
    )


def _loss_head(x, g, target):
    T, D = x.shape
    tm = _tile(T, 512)

    def body(x_ref, g_ref, t_ref, dx_ref, loss_ref, dg_ref):
        @pl.when(pl.program_id(0) == 0)
        def _():
            loss_ref[...] = jnp.zeros_like(loss_ref)
            dg_ref[...] = jnp.zeros_like(dg_ref)

        xv = x_ref[...]
        gv = g_ref[...]
        e = xv * _rstd(xv) * gv - t_ref[...]
        per_tok = jnp.sum(e * e, axis=-1, keepdims=True) * (1.0 / D)
        loss_ref[...] += 0.5 * jnp.sum(per_tok, axis=0, keepdims=True)
        dx, dg = _rms_bwd(xv, gv, e * (1.0 / D))
        dx_ref[...] = dx
        dg_ref[...] += dg

    row = pl.BlockSpec((tm, D), lambda i: (i, 0))
    vec = pl.BlockSpec((1, D), lambda i: (0, 0))
    return pl.pallas_call(
        body, name="loss_head", grid=(T // tm,),
        in_specs=[row, vec, row],
        out_specs=[row, pl.BlockSpec((1, LANES), lambda i: (0, 0)), vec],
        out_shape=[jax.ShapeDtypeStruct((T, D), F32), jax.ShapeDtypeStruct((1, LANES), F32),
                   jax.ShapeDtypeStruct((1, D), F32)],
        compiler_params=_cp("arbitrary"),
    )(x, g, target)


def _log_gates(z):
    ls = jnp.minimum(z, 0.0) - jnp.log(1.0 + jnp.exp(-jnp.abs(z)))
    return ls, ls - z


def _cumsum_mm(v, u2):
    hi = v.astype(BF16)
    lo = (v - hi.astype(F32)).astype(BF16)
    return _dot(jnp.concatenate([hi, lo], axis=1), u2, NN)


def _half_rowsum(v):
    n = v.shape[0]
    s0 = jnp.sum(v[:, :QB], axis=1, keepdims=True)
    s1 = jnp.sum(v[:, QB:], axis=1, keepdims=True)
    return jnp.concatenate([jnp.broadcast_to(s0, (n, QB)), jnp.broadcast_to(s1, (n, QB))], axis=1)


def _stack_heads(src_ref, dst_ref, n_blk):
    m0 = lax.broadcasted_iota(jnp.int32, (1, LANES), 1) < HEAD_DIM

    def fill(c, carry):
        blk = src_ref[pl.ds(pl.multiple_of(c * QB, QB), QB), :]
        zero = jnp.zeros_like(blk)
        dst_ref[c, 0:QB, :] = jnp.where(m0, blk, zero)
        dst_ref[c, QB:2 * QB, :] = jnp.where(m0, zero, blk)
        return carry

    lax.fori_loop(0, n_blk, fill, 0)


def _diag_mask(tq, j):
    n = tq - j * QB
    row = lax.broadcasted_iota(jnp.int32, (n, 2 * QB), 0)
    col = lax.broadcasted_iota(jnp.int32, (n, 2 * QB), 1)
    return (col & (QB - 1)) < row


def _tri_blockdiag(upper):
    r = lax.broadcasted_iota(jnp.int32, (2 * QB, 2 * QB), 0)
    c = lax.broadcasted_iota(jnp.int32, (2 * QB, 2 * QB), 1)
    same = (r // QB) == (c // QB)
    u = (same & ((r > c) if upper else (r < c))).astype(BF16)
    return jnp.concatenate([u, u], axis=0)


def _attn_tiles(T, n_seq):
    S = T // n_seq
    tq = ATT_TQ if S % ATT_TQ == 0 else QB
    return S, tq, tq // QB, S // tq, S // QB


def _attn_fwd(qkv, n_seq):
    T = qkv.shape[0]
    S, tq, r, n_q, n_k = _attn_tiles(T, n_seq)
    n_p = D_MODEL // LANES
    u_suffix = _tri_blockdiag(True)

    def body(q_ref, k_ref, v_ref, u_ref, o_ref, tot_ref, kk_ref, vv_ref, lr_s, acc_s):
        qi = pl.program_id(2)

        @pl.when(qi == 0)
        def _():
            _stack_heads(k_ref, kk_ref, n_k)
            _stack_heads(v_ref, vv_ref, n_k)

        u = u_ref[...]
        lr_s[...] = jnp.zeros_like(lr_s)
        acc_s[...] = jnp.zeros_like(acc_s)

        def step(kj, q, mask, lr, acc):
            ls, lk = _log_gates(_dot(q, kk_ref[kj], NT))
            if mask is not None:
                lk = jnp.where(mask, lk, 0.0)
            a = jnp.exp(ls + _cumsum_mm(lk, u) + lr)
            if mask is not None:
                a = jnp.where(mask, a, 0.0)
            return lr + _half_rowsum(lk), acc + _dot(a.astype(BF16), vv_ref[kj], NN)

        for j in reversed(range(r)):
            rows = slice(j * QB, tq)
            lr, acc = step(qi * r + j, q_ref[rows, :] * ATT_SCALE, _diag_mask(tq, j), lr_s[rows, :], acc_s[rows, :])
            lr_s[rows, :] = lr
            acc_s[rows, :] = acc

        q = q_ref[...] * ATT_SCALE

        def off(it, carry):
            lr, acc = lr_s[...], acc_s[...]
            for j in range(r):
                lr, acc = step((qi - it) * r - 1 - j, q, None, lr, acc)
            lr_s[...] = lr
            acc_s[...] = acc
            return carry

        lax.fori_loop(0, qi, off, 0)
        o_ref[...] = acc_s[...].astype(BF16)
        tot_ref[...] = lr_s[...]

    return pl.pallas_call(
        body, name="attn_fwd", grid=(n_seq, n_p, n_q),
        in_specs=[pl.BlockSpec((tq, LANES), lambda b, p, qi: (b * n_q + qi, p)),
                  pl.BlockSpec((S, LANES), lambda b, p, qi: (b, n_p + p)),
                  pl.BlockSpec((S, LANES), lambda b, p, qi: (b, 2 * n_p + p)),
                  pl.BlockSpec((4 * QB, 2 * QB), lambda b, p, qi: (0, 0))],
        out_specs=[pl.BlockSpec((tq, LANES), lambda b, p, qi: (b * n_q + qi, p)),
                   pl.BlockSpec((tq, 2 * QB), lambda b, p, qi: (b * n_q + qi, p))],
        out_shape=[jax.ShapeDtypeStruct((T, D_MODEL), BF16), jax.ShapeDtypeStruct((T, 2 * D_MODEL), F32)],
        scratch_shapes=[pltpu.VMEM((n_k, 2 * QB, LANES), BF16), pltpu.VMEM((n_k, 2 * QB, LANES), BF16),
                        pltpu.VMEM((tq, 2 * QB), F32), pltpu.VMEM((tq, LANES), F32)],
        compiler_params=_cp("parallel", "parallel", "arbitrary"),
    )(qkv, qkv, qkv, u_suffix)


def _attn_bwd(qkv, do, tot, n_seq):
    T = qkv.shape[0]
    S, tq, r, n_q, n_k = _attn_tiles(T, n_seq)
    n_p = D_MODEL // LANES
    u_prefix = _tri_blockdiag(False)

    def body(q_ref, k_ref, v_ref, do_ref, tot_ref, u_ref, dq_ref, dk_out, dv_out,
             kk_ref, vv_ref, cl_s, cg_s, dq_s, dk_ref, dv_ref):
        qi = pl.program_id(2)

        @pl.when(qi == 0)
        def _():
            _stack_heads(k_ref, kk_ref, n_k)
            _stack_heads(v_ref, vv_ref, n_k)
            dk_ref[...] = jnp.zeros_like(dk_ref)
            dv_ref[...] = jnp.zeros_like(dv_ref)

        u = u_ref[...]
        m0 = lax.broadcasted_iota(jnp.int32, (1, LANES), 1) < HEAD_DIM
        cl_s[...] = tot_ref[...]
        cg_s[...] = jnp.zeros_like(cg_s)
        dq_s[...] = jnp.zeros_like(dq_s)

        def step(kj, q, dov, mask, rest, cg, dq):
            kk = kk_ref[kj]
            ls, lk = _log_gates(_dot(q, kk, NT))
            if mask is not None:
                lk = jnp.where(mask, lk, 0.0)
            a = jnp.exp(ls + (rest - (_cumsum_mm(lk, u) + lk)))
            if mask is not None:
                a = jnp.where(mask, a, 0.0)
            g = a * _dot(dov, vv_ref[kj], NT)
            dz = g - (g + _dot(g.astype(BF16), u[:2 * QB], NN) + cg) * jnp.exp(ls)
            if mask is not None:
                dz = jnp.where(mask, dz, 0.0)
            dz = dz.astype(BF16)
            rows = pl.ds(pl.multiple_of(kj * QB, QB), QB)
            dvt = _dot(a.astype(BF16), dov, TN)
            dv_ref[rows, :] += jnp.where(m0, dvt[:QB], dvt[QB:])
            dkt = _dot(dz, q, TN)
            dk_ref[rows, :] += jnp.where(m0, dkt[:QB], dkt[QB:])
            return rest - _half_rowsum(lk), cg + _half_rowsum(g), dq + _dot(dz, kk, NN)

        q = q_ref[...] * ATT_SCALE
        dov = do_ref[...]

        def off(it, carry):
            cl, cg, dq = cl_s[...], cg_s[...], dq_s[...]
            for j in range(r):
                cl, cg, dq = step(it * r + j, q, dov, None, cl, cg, dq)
            cl_s[...] = cl
            cg_s[...] = cg
            dq_s[...] = dq
            return carry

        lax.fori_loop(0, qi, off, 0)

        for j in range(r):
            rows = slice(j * QB, tq)
            cl, cg, dq = step(qi * r + j, q_ref[rows, :] * ATT_SCALE, do_ref[rows, :],
                              _diag_mask(tq, j), cl_s[rows, :], cg_s[rows, :], dq_s[rows, :])
            cl_s[rows, :] = cl
            cg_s[rows, :] = cg
            dq_s[rows, :] = dq
        dq_ref[...] = (dq_s[...] * ATT_SCALE).astype(BF16)

        @pl.when(qi == n_q - 1)
        def _():
            dk_out[...] = dk_ref[...].astype(BF16)
            dv_out[...] = dv_ref[...].astype(BF16)

    qspec = pl.BlockSpec((tq, LANES), lambda b, p, qi: (b * n_q + qi, p))
    seq = lambda off: pl.BlockSpec((S, LANES), lambda b, p, qi: (b, off + p))
    return pl.pallas_call(
        body, name="attn_bwd", grid=(n_seq, n_p, n_q),
        in_specs=[qspec, seq(n_p), seq(2 * n_p), qspec,
                  pl.BlockSpec((tq, 2 * QB), lambda b, p, qi: (b * n_q + qi, p)),
                  pl.BlockSpec((4 * QB, 2 * QB), lambda b, p, qi: (0, 0))],
        out_specs=[qspec, seq(0), seq(0)],
        out_shape=[jax.ShapeDtypeStruct((T, D_MODEL), BF16)] * 3,
        scratch_shapes=[pltpu.VMEM((n_k, 2 * QB, LANES), BF16), pltpu.VMEM((n_k, 2 * QB, LANES), BF16),
                        pltpu.VMEM((tq, 2 * QB), F32), pltpu.VMEM((tq, 2 * QB), F32), pltpu.VMEM((tq, LANES), F32),
                        pltpu.VMEM((S, LANES), F32), pltpu.VMEM((S, LANES), F32)],
        compiler_params=_cp("parallel", "parallel", "arbitrary"),
    )(qkv, qkv, qkv, do, tot, u_prefix)


def _shifted_copies(sh_ref):
    rows = sh_ref.shape[1] - SUBLANES
    for s in range(1, SUBLANES):
        sh_ref[s, 0:rows, :] = sh_ref[0, s:s + rows, :]


def _shifted(sh_ref, start, n):
    s = start % SUBLANES
    return sh_ref[s, start - s:start - s + n, :]


def _glu_with_halo(av_ref, ag_ref, avh_ref, agh_ref, a0_s, first, ts):
    hal = avh_ref[...] * _sigmoid(agh_ref[...])
    a0_s[0, 0:HALO, :] = jnp.where(first, 0.0, hal)
    a0_s[0, HALO:HALO + ts, :] = av_ref[...] * _sigmoid(ag_ref[...])
    _shifted_copies(a0_s)


def _mix_specs(ts, n_r, with_left):
    blk = lambda c: pl.BlockSpec((ts, CA), lambda b, r: (b * n_r + r, c))
    per = ts // HALO
    left = lambda c: pl.BlockSpec((HALO, CA), lambda b, r: (jnp.maximum((b * n_r + r) * per - 1, 0), c))
    return blk, (left if with_left else None)


def _mix_fwd(z, conv_w, conv_b, ln_a_g, ln_a_b, ln_v_g, ln_v_b, ws, bias2d, n_seq):
    T = z.shape[0]
    S = T // n_seq
    ts = _tile(S, 512)
    n_r = S // ts
    shift = HALO - (CONV_WIDTH - 1)

    def body(av_ref, ag_ref, avh_ref, agh_ref, u_ref, v_ref, cw_ref, cb_ref, lag_ref, lab_ref,
             lvg_ref, lvb_ref, ws_ref, bias_ref, cat_ref, a1_ref, a0_s):
        _glu_with_halo(av_ref, ag_ref, avh_ref, agh_ref, a0_s, pl.program_id(1) == 0, ts)
        for rb in range(ts // CONV_ROWS):
            base = rb * CONV_ROWS
            acc = jnp.broadcast_to(cb_ref[...], (CONV_ROWS, CA))
            for k in range(CONV_WIDTH):
                acc = acc + cw_ref[k:k + 1, :] * _shifted(a0_s, base + shift + k, CONV_ROWS)
            a1_ref[base:base + CONV_ROWS, :] = acc
        y, _, _ = _ln_fwd(a1_ref[...], lag_ref[...], lab_ref[...])
        cat_ref[:, 0:CA] = (y * _sigmoid(y)).astype(BF16)
        for gi in range(GB):
            sl = slice(gi * DB, (gi + 1) * DB)
            v1, _, _ = _ln_fwd(v_ref[:, sl], lvg_ref[:, sl], lvb_ref[:, sl])
            v1 = v1.astype(BF16)
            for c in range(ts // CHUNK):
                rs = slice(c * CHUNK, (c + 1) * CHUNK)
                v2 = _dot(ws_ref[gi], v1[rs], NN) + bias_ref[:, sl]
                cat_ref[rs, CA + gi * DB:CA + (gi + 1) * DB] = (u_ref[rs, sl] * v2).astype(BF16)

    blk, left = _mix_specs(ts, n_r, True)
    vec = pl.BlockSpec((1, CA), lambda b, r: (0, 0))
    return pl.pallas_call(
        body, name="mix_fwd", grid=(n_seq, n_r),
        in_specs=[blk(0), blk(1), left(0), left(1), blk(2), blk(3),
                  pl.BlockSpec((CONV_WIDTH, CA), lambda b, r: (0, 0)), vec, vec, vec, vec, vec,
                  pl.BlockSpec((GB, CHUNK, CHUNK), lambda b, r: (0, 0, 0)),
                  pl.BlockSpec((CHUNK, CB), lambda b, r: (0, 0))],
        out_specs=[pl.BlockSpec((ts, CA + CB), lambda b, r: (b * n_r + r, 0)), blk(0)],
        out_shape=[jax.ShapeDtypeStruct((T, CA + CB), BF16), jax.ShapeDtypeStruct((T, CA), F32)],
        scratch_shapes=[pltpu.VMEM((SUBLANES, HALO + ts, CA), F32)],
        compiler_params=_cp("parallel", "parallel"),
    )(z, z, z, z, z, z, conv_w, conv_b, ln_a_g, ln_a_b, ln_v_g, ln_v_b, ws, bias2d)


def _mix_bwd_rows(dcat, z, a1, ln_a_g, ln_a_b, ln_v_g, ln_v_b, ws, ws_t, bias2d, n_seq):
    T = z.shape[0]
    S = T // n_seq
    ts = _tile(S, 512)
    n_r = S // ts

    def body(dc_ref, u_ref, v_ref, a1_ref, lag_ref, lab_ref, lvg_ref, lvb_ref, ws_ref, wst_ref, bias_ref,
             da1_ref, dz_ref, dlag_ref, dlab_ref, dlvg_ref, dlvb_ref, dws_ref, dsb_ref, dv1_s, dbias_s):
        first = (pl.program_id(0) == 0) & (pl.program_id(1) == 0)
        last = (pl.program_id(0) == n_seq - 1) & (pl.program_id(1) == n_r - 1)

        @pl.when(first)
        def _():
            for ref in (dlag_ref, dlab_ref, dlvg_ref, dlvb_ref, dws_ref, dbias_s):
                ref[...] = jnp.zeros_like(ref)

        lag = lag_ref[...]
        y, xh, r = _ln_fwd(a1_ref[...], lag, lab_ref[...])
        sig = _sigmoid(y)
        dy = dc_ref[:, 0:CA] * (sig * (1.0 + y * (1.0 - sig)))
        dlag_ref[...] += jnp.sum(dy * xh, axis=0, keepdims=True)
        dlab_ref[...] += jnp.sum(dy, axis=0, keepdims=True)
        da1_ref[...] = _ln_bwd(dy, xh, r, lag)

        tril = (lax.broadcasted_iota(jnp.int32, (CHUNK, CHUNK), 0)
                >= lax.broadcasted_iota(jnp.int32, (CHUNK, CHUNK), 1))
        for gi in range(GB):
            sl = slice(gi * DB, (gi + 1) * DB)
            lvg = lvg_ref[:, sl]
            v1, vh, vr = _ln_fwd(v_ref[:, sl], lvg, lvb_ref[:, sl])
            v1 = v1.astype(BF16)
            for c in range(ts // CHUNK):
                rs = slice(c * CHUNK, (c + 1) * CHUNK)
                v2 = _dot(ws_ref[gi], v1[rs], NN) + bias_ref[:, sl]
                dbo = dc_ref[rs, CA + gi * DB:CA + (gi + 1) * DB]
                dz_ref[rs, sl] = (dbo * v2).astype(BF16)
                dv2 = dbo * u_ref[rs, sl]
                dbias_s[:, sl] += dv2
                dv2b = dv2.astype(BF16)
                dws_ref[gi] += jnp.where(tril, _dot(dv2b, v1[rs], NT), 0.0)
                dv1_s[rs, :] = _dot(wst_ref[gi], dv2b, NN)
            dv1 = dv1_s[...]
            dlvg_ref[:, sl] += jnp.sum(dv1 * vh, axis=0, keepdims=True)
            dlvb_ref[:, sl] += jnp.sum(dv1, axis=0, keepdims=True)
            dz_ref[:, CB + gi * DB:CB + (gi + 1) * DB] = _ln_bwd(dv1, vh, vr, lvg).astype(BF16)

        @pl.when(last)
        def _():
            col = lax.broadcasted_iota(jnp.int32, (CHUNK, GB), 1)
            out = jnp.zeros((CHUNK, GB), F32)
            for gi in range(GB):
                s = jnp.sum(dbias_s[:, gi * DB:(gi + 1) * DB], axis=1, keepdims=True)
                out = out + jnp.where(col == gi, s, 0.0)
            dsb_ref[...] = out

    blk, _ = _mix_specs(ts, n_r, False)
    vec = pl.BlockSpec((1, CA), lambda b, r: (0, 0))
    mat = pl.BlockSpec((GB, CHUNK, CHUNK), lambda b, r: (0, 0, 0))
    wide = pl.BlockSpec((ts, CA + CB), lambda b, r: (b * n_r + r, 0))
    return pl.pallas_call(
        body, name="mix_bwd_rows", grid=(n_seq, n_r),
        in_specs=[wide, blk(2), blk(3), blk(0), vec, vec, vec, vec, mat, mat,
                  pl.BlockSpec((CHUNK, CB), lambda b, r: (0, 0))],
        out_specs=[blk(0), wide, vec, vec, vec, vec, mat, pl.BlockSpec((CHUNK, GB), lambda b, r: (0, 0))],
        out_shape=[jax.ShapeDtypeStruct((T, CA), F32), jax.ShapeDtypeStruct((T, 2 * CB), BF16)]
        + [jax.ShapeDtypeStruct((1, CA), F32)] * 4
        + [jax.ShapeDtypeStruct((GB, CHUNK, CHUNK), F32), jax.ShapeDtypeStruct((CHUNK, GB), F32)],
        scratch_shapes=[pltpu.VMEM((ts, DB), F32), pltpu.VMEM((CHUNK, CB), F32)],
        compiler_params=_cp("arbitrary", "arbitrary"),
    )(dcat, z, z, a1, ln_a_g, ln_a_b, ln_v_g, ln_v_b, ws, ws_t, bias2d)


def _mix_bwd_conv(da1, z, conv_w, n_seq):
    T = z.shape[0]
    S = T // n_seq
    ts = _tile(S, 512)
    n_r = S // ts
    per = ts // HALO
    shift = HALO - (CONV_WIDTH - 1)
    fold = CONV_ROWS // 8

    def body(d_ref, dh_ref, av_ref, ag_ref, avh_ref, agh_ref, cw_ref,
             dz_ref, dcw_ref, dcb_ref, a0_s, d1_s, da0_s, dw8_s):
        first = (pl.program_id(0) == 0) & (pl.program_id(1) == 0)
        last = (pl.program_id(0) == n_seq - 1) & (pl.program_id(1) == n_r - 1)

        @pl.when(first)
        def _():
            dw8_s[...] = jnp.zeros_like(dw8_s)
            dcb_ref[...] = jnp.zeros_like(dcb_ref)

        _glu_with_halo(av_ref, ag_ref, avh_ref, agh_ref, a0_s, pl.program_id(1) == 0, ts)
        d1_s[0, 0:ts, :] = d_ref[...]
        d1_s[0, ts:ts + HALO, :] = jnp.where(pl.program_id(1) == n_r - 1, 0.0, dh_ref[...])
        _shifted_copies(d1_s)
        dcb_ref[...] += jnp.sum(d_ref[...], axis=0, keepdims=True)
        for rb in range(ts // CONV_ROWS):
            base = rb * CONV_ROWS
            dcur = d1_s[0, base:base + CONV_ROWS, :]
            acc = jnp.zeros((CONV_ROWS, CA), F32)
            for k in range(CONV_WIDTH):
                back = CONV_WIDTH - 1 - k
                acc = acc + cw_ref[k:k + 1, :] * _shifted(d1_s, base + back, CONV_ROWS)
                prod = dcur * _shifted(a0_s, base + shift + k, CONV_ROWS)
                part = prod[0:8]
                for f in range(1, fold):
                    part = part + prod[8 * f:8 * f + 8]
                dw8_s[k] += part
            da0_s[base:base + CONV_ROWS, :] = acc
        da0 = da0_s[...]
        sig = _sigmoid(ag_ref[...])
        dz_ref[:, 0:CA] = (da0 * sig).astype(BF16)
        dz_ref[:, CA:2 * CA] = (da0 * av_ref[...] * sig * (1.0 - sig)).astype(BF16)

        @pl.when(last)
        def _():
            for k in range(CONV_WIDTH):
                dcw_ref[k:k + 1, :] = jnp.sum(dw8_s[k], axis=0, keepdims=True)

    blk, left = _mix_specs(ts, n_r, True)
    n_halo_blocks = T // HALO
    right = pl.BlockSpec((HALO, CA), lambda b, r: (jnp.minimum((b * n_r + r + 1) * per, n_halo_blocks - 1), 0))
    return pl.pallas_call(
        body, name="mix_bwd_conv", grid=(n_seq, n_r),
        in_specs=[blk(0), right, blk(0), blk(1), left(0), left(1),
                  pl.BlockSpec((CONV_WIDTH, CA), lambda b, r: (0, 0))],
        out_specs=[pl.BlockSpec((ts, 2 * CA), lambda b, r: (b * n_r + r, 0)),
                   pl.BlockSpec((CONV_WIDTH, CA), lambda b, r: (0, 0)), pl.BlockSpec((1, CA), lambda b, r: (0, 0))],
        out_shape=[jax.ShapeDtypeStruct((T, 2 * CA), BF16), jax.ShapeDtypeStruct((CONV_WIDTH, CA), F32),
                   jax.ShapeDtypeStruct((1, CA), F32)],
        scratch_shapes=[pltpu.VMEM((SUBLANES, HALO + ts, CA), F32), pltpu.VMEM((SUBLANES, ts + HALO, CA), F32),
                        pltpu.VMEM((ts, CA), F32), pltpu.VMEM((CONV_WIDTH, 8, CA), F32)],
        compiler_params=_cp("arbitrary", "arbitrary"),
    )(da1, da1, z, z, z, z, conv_w)


def _row_tile(R, want):
    t = min(R, want)
    t -= t % 8
    while t > 8 and R % t:
        t -= 8
    return t if t >= 8 and R % t == 0 else R


def _adam_step(w, g, m, v):
    nm = ADAM_B1 * m + (1.0 - ADAM_B1) * g
    nv = ADAM_B2 * v + (1.0 - ADAM_B2) * (g * g)
    m_hat = nm / (1.0 - ADAM_B1 ** ADAM_STEP)
    v_hat = nv / (1.0 - ADAM_B2 ** ADAM_STEP)
    return -ADAM_LR * (m_hat / (jnp.sqrt(v_hat) + ADAM_EPS) + ADAM_WD * w), nm, nv


def _adamw_parts(parts, w, m, v, name):
    L, n, C = w.shape
    assert len(parts) == L
    tr = _row_tile(n, 192)
    n_i = n // tr

    def body(*refs):
        p_refs = refs[:L]
        w_ref, m_ref, v_ref, g_ref, d_ref, nm_ref, nv_ref = refs[L:]
        for k in range(L):
            @pl.when(pl.program_id(0) == k)
            def _(k=k):
                acc = p_refs[k][0]
                for s in range(1, N_DEV):
                    acc = acc + p_refs[k][s]
                g_ref[...] = acc

        d_ref[...], nm_ref[...], nv_ref[...] = _adam_step(w_ref[...], g_ref[...], m_ref[...], v_ref[...])

    def part_spec(k):
        return pl.BlockSpec((N_DEV, tr, C),
                            lambda l, i: (0, jnp.where(l == k, i, jnp.where(l < k, 0, n_i - 1)), 0))

    blk = pl.BlockSpec((None, tr, C), lambda l, i: (l, i, 0))
    return pl.pallas_call(
        body, name=name, grid=(L, n_i),
        in_specs=[part_spec(k) for k in range(L)] + [blk] * 3, out_specs=[blk] * 4,
        out_shape=[jax.ShapeDtypeStruct((L, n, C), F32)] * 4,
        compiler_params=_cp("arbitrary", "arbitrary"),
    )(*parts, w, m, v)


def _adamw(w, g, m, v, name):
    R, C = w.shape
    tr = _row_tile(R, 256)

    def body(w_ref, g_ref, m_ref, v_ref, d_ref, nm_ref, nv_ref):
        d_ref[...], nm_ref[...], nv_ref[...] = _adam_step(w_ref[...], g_ref[...], m_ref[...], v_ref[...])

    blk = pl.BlockSpec((tr, C), lambda i: (i, 0))
    return pl.pallas_call(
        body, name=name, grid=(R // tr,),
        in_specs=[blk] * 4, out_specs=[blk] * 3,
        out_shape=[jax.ShapeDtypeStruct((R, C), F32)] * 3,
        compiler_params=_cp("parallel"),
    )(w, g, m, v)


def _me():
    return lax.axis_index("x"), lax.axis_index("y"), lax.axis_index("c")


def _block_rows(ref, dev, n):
    start = (4 * dev[0] + 2 * dev[1] + dev[2]) * n
    if len(ref.shape) == 2:
        return ref.at[pl.ds(start, n), :]
    return ref.at[:, pl.ds(start, n), :]


def _all_gather(shards):
    na = len(shards)
    ns = [s.shape[-2] for s in shards]

    def body(*refs):
        ins, outs = refs[:na], refs[na:2 * na]
        send_sems, recv_sems, local_sems = refs[2 * na:]
        x, y, c = _me()
        me, sibling = (x, y, c), (x, y, 1 - c)
        chips = [(1 - x, y), (x, 1 - y), (1 - x, 1 - y)]

        def copy(a, k, block, to, src=None):
            dst = _block_rows(outs[a], block, ns[a])
            return pltpu.make_async_remote_copy(
                src_ref=dst if src is None else src, dst_ref=dst,
                send_sem=send_sems.at[a, k], recv_sem=recv_sems.at[a, k], device_id=to, device_id_type=MESH)

        mine = [pltpu.make_async_copy(ins[a], _block_rows(outs[a], me, ns[a]), local_sems.at[a]) for a in range(na)]
        for cp in mine:
            cp.start()
        first = []
        for a in range(na):
            first.append(copy(a, 0, me, sibling, src=ins[a]))
            first += [copy(a, 1 + j, me, (*chip, c), src=ins[a]) for j, chip in enumerate(chips)]
        for cp in first:
            cp.start()
        passed = []
        for j, chip in enumerate(chips):
            for a in range(na):
                copy(a, 1 + j, (*chip, c), me).wait_recv()
                fwd = copy(a, 4 + j, (*chip, c), sibling)
                fwd.start()
                passed.append(fwd)
        for a in range(na):
            copy(a, 0, sibling, me).wait_recv()
            for j, chip in enumerate(chips):
                copy(a, 4 + j, (*chip, 1 - c), me).wait_recv()
        for cp in first + passed:
            cp.wait_send()
        for cp in mine:
            cp.wait()

    out_shape = [jax.ShapeDtypeStruct(s.shape[:-2] + (N_DEV * s.shape[-2], s.shape[-1]), s.dtype) for s in shards]
    return pl.pallas_call(
        body, name="weights_all_gather",
        in_specs=[ANY] * na, out_specs=[ANY] * na, out_shape=out_shape,
        scratch_shapes=[pltpu.SemaphoreType.DMA((na, 7)), pltpu.SemaphoreType.DMA((na, 7)),
                        pltpu.SemaphoreType.DMA((na,))],
    )(*shards)


def _split_copies(gather, srcs, lands, send_sems, recv_sems, ns):
    x, y, c = _me()
    me = (x, y, c)
    my_slot = 4 * x + 2 * y + c
    copies = []
    for mask in range(1, N_DEV):
        peer = (x ^ (mask >> 2), y ^ ((mask >> 1) & 1), c ^ (mask & 1))
        for a in range(len(srcs)):
            if gather:
                src, dst = srcs[a], _block_rows(lands[a], me, ns[a])
            else:
                src, dst = _block_rows(srcs[a], peer, ns[a]), lands[a].at[my_slot]
            sem = a * (N_DEV - 1) + mask - 1
            copies.append(pltpu.make_async_remote_copy(
                src_ref=src, dst_ref=dst, send_sem=send_sems.at[sem], recv_sem=recv_sems.at[sem],
                device_id=peer, device_id_type=MESH))
    return copies


HBM_SPEC = pl.BlockSpec(memory_space=pltpu.HBM)
SEM_SPEC = pl.BlockSpec(memory_space=pltpu.SEMAPHORE)


def _split_start(gather, srcs, name, dep=None):
    na = len(srcs)
    if gather:
        ns = [s.shape[-2] for s in srcs]
        lands = [lax.empty(s.shape[:-2] + (N_DEV * s.shape[-2], s.shape[-1]), s.dtype) for s in srcs]
    else:
        ns = [s.shape[-2] // N_DEV for s in srcs]
        lands = [lax.empty((N_DEV, s.shape[-2] // N_DEV, s.shape[-1]), s.dtype) for s in srcs]
    n_in = 2 * na + (dep is not None)

    def body(*refs):
        send_sems, recv_sems = refs[n_in], refs[n_in + 1]
        for cp in _split_copies(gather, refs[:na], refs[na:2 * na], send_sems, recv_sems, ns):
            cp.start()
        refs[-1][...] = jnp.zeros_like(refs[-1])

    hbm = lambda a: pltpu.with_memory_space_constraint(a, pltpu.HBM)
    args = [hbm(a) for a in srcs] + [hbm(a) for a in lands] + ([dep] if dep is not None else [])
    out = pl.pallas_call(
        body, name=name,
        in_specs=[HBM_SPEC] * (2 * na) + ([ANY] if dep is not None else []),
        out_specs=[SEM_SPEC, SEM_SPEC] + [HBM_SPEC] * (2 * na) + [pl.BlockSpec(memory_space=pltpu.VMEM)],
        out_shape=[pltpu.SemaphoreType.DMA((na * (N_DEV - 1),)), pltpu.SemaphoreType.DMA((na * (N_DEV - 1),))]
        + [pltpu.HBM(a.shape, a.dtype) for a in srcs + lands] + [jax.ShapeDtypeStruct((8, LANES), F32)],
        input_output_aliases={i: 2 + i for i in range(2 * na)},
        compiler_params=pltpu.CompilerParams(has_side_effects=pltpu.SideEffectType.DATAFLOW_SIDE_EFFECTING),
    )(*args)
    return (gather, ns, out[0], out[1], list(out[2:2 + na]), list(out[2 + na:2 + 2 * na])), out[-1]


def _split_wait(handle, after, name):
    gather, ns, send, recv, srcs, lands = handle
    na = len(srcs)

    def body(*refs):
        send_sems, recv_sems = refs[2 * na], refs[2 * na + 1]
        for cp in _split_copies(gather, refs[:na], refs[na:2 * na], send_sems, recv_sems, ns):
            cp.wait_send()
            cp.wait_recv()

    out = pl.pallas_call(
        body, name=name,
        in_specs=[HBM_SPEC] * (2 * na) + [SEM_SPEC, SEM_SPEC, ANY],
        out_specs=[HBM_SPEC] * (2 * na),
        out_shape=[pltpu.HBM(a.shape, a.dtype) for a in srcs + lands],
        input_output_aliases={i: i for i in range(2 * na)},
        compiler_params=pltpu.CompilerParams(has_side_effects=pltpu.SideEffectType.DATAFLOW_SIDE_EFFECTING),
    )(*srcs, *lands, send, recv, after)
    return list(out[:na]), list(out[na:])


def _small_all_reduce(buf):
    R = buf.shape[0]

    def body(b_ref, o_ref, recv_ref, send_sems, recv_sems):
        x, y, c = _me()
        my_slot = 4 * x + 2 * y + c
        recv_ref[my_slot] = b_ref[...]
        copies = []
        for mask in range(1, N_DEV):
            peer = (x ^ (mask >> 2), y ^ ((mask >> 1) & 1), c ^ (mask & 1))
            copies.append(pltpu.make_async_remote_copy(
                src_ref=b_ref, dst_ref=recv_ref.at[my_slot],
                send_sem=send_sems.at[mask - 1], recv_sem=recv_sems.at[mask - 1],
                device_id=peer, device_id_type=MESH))
        for cp in copies:
            cp.start()
        for cp in copies:
            cp.wait()
        acc = recv_ref[0]
        for k in range(1, N_DEV):
            acc = acc + recv_ref[k]
        o_ref[...] = acc

    return pl.pallas_call(
        body, name="small_all_reduce",
        in_specs=[pl.BlockSpec(memory_space=pltpu.VMEM)], out_specs=pl.BlockSpec(memory_space=pltpu.VMEM),
        out_shape=jax.ShapeDtypeStruct((R, LANES), F32),
        scratch_shapes=[pltpu.VMEM((N_DEV, R, LANES), F32), pltpu.SemaphoreType.DMA((7,)),
                        pltpu.SemaphoreType.DMA((7,))],
        compiler_params=pltpu.CompilerParams(vmem_limit_bytes=VMEM_LIMIT),
    )(buf)


def _pack(arrays):
    flat = jnp.concatenate([a.reshape(-1) for a in arrays])
    pad = (-flat.shape[0]) % (8 * LANES)
    return jnp.pad(flat, (0, pad)).reshape(-1, LANES)


def _unpack(buf, shapes):
    flat = buf.reshape(-1)
    out, off = [], 0
    for s in shapes:
        n = 1
        for d in s:
            n *= d
        out.append(flat[off:off + n].reshape(s))
        off += n
    return out


def _ffn_index(layer, second):
    return (2 * layer + second) * 3


def kernel(x, g_ffn1, w_ffn1_gate, w_ffn1_up, w_ffn1_down, g_mix, w_in_ab, conv_w, conv_b, ln_a_g, ln_a_b, ln_v_g, ln_v_b, sp_w, sp_b, w_out_ab, w_qkv, w_o, g_ffn2, w_ffn2_gate, w_ffn2_up, w_ffn2_down, g_final, loss_target, m_g_ffn1, m_w_ffn1_gate, m_w_ffn1_up, m_w_ffn1_down, m_g_mix, m_w_in_ab, m_conv_w, m_conv_b, m_ln_a_g, m_ln_a_b, m_ln_v_g, m_ln_v_b, m_sp_w, m_sp_b, m_w_out_ab, m_w_qkv, m_w_o, m_g_ffn2, m_w_ffn2_gate, m_w_ffn2_up, m_w_ffn2_down, m_g_final, v_g_ffn1, v_w_ffn1_gate, v_w_ffn1_up, v_w_ffn1_down, v_g_mix, v_w_in_ab, v_conv_w, v_conv_b, v_ln_a_g, v_ln_a_b, v_ln_v_g, v_ln_v_b, v_sp_w, v_sp_b, v_w_out_ab, v_w_qkv, v_w_o, v_g_ffn2, v_w_ffn2_gate, v_w_ffn2_up, v_w_ffn2_down, v_g_final):
    n_seq, S, D = x.shape
    T = n_seq * S
    depth = g_ffn1.shape[0]
    assert depth == 2 and D == D_MODEL
    my_block = 4 * lax.axis_index("x") + 2 * lax.axis_index("y") + lax.axis_index("c")

    ffn_parts = []
    for l in range(depth):
        for gate, up, down in ((w_ffn1_gate, w_ffn1_up, w_ffn1_down), (w_ffn2_gate, w_ffn2_up, w_ffn2_down)):
            ffn_parts += [gate[l].T, up[l].T, down[l]]
    ffn_shard = lambda k: jnp.stack(ffn_parts[3 * k:3 * k + 3]).astype(BF16)
    conv_w_pad = jnp.zeros((HALO, conv_w.shape[2]), F32).at[:CONV_WIDTH].set(conv_w[0]).T
    w_ffn = [None] * (2 * depth)
    w_ffn[0], conv_w_t = _all_gather([ffn_shard(0), conv_w_pad])
    conv_w_full = conv_w_t.T[:CONV_WIDTH]
    shards_b = [w_out_ab[0].astype(BF16), ffn_shard(1)]
    shards_d = [w_qkv[0].T.astype(BF16), w_o[0].astype(BF16), ffn_shard(3)]
    gather_a, token = _split_start(True, [w_in_ab[0].T.astype(BF16)], "gather_a_start", dep=conv_w_t)
    gather_b, token = _split_start(True, shards_b, "gather_b_start", dep=token)
    gather_c, token = _split_start(True, [ffn_shard(2)], "gather_c_start", dep=token)
    gather_d, token = _split_start(True, shards_d, "gather_d_start", dep=token)

    def gathered(handle, after, name):
        shards, lands = _split_wait(handle, after, name)
        out = []
        for land, shard in zip(lands, shards):
            n = shard.shape[-2]
            start = (0,) * (land.ndim - 2) + (my_block * n, 0)
            out.append(lax.dynamic_update_slice(land, shard, start))
        return out

    row = lambda a: a.reshape(1, -1)
    tril = jnp.tril(jnp.ones((CHUNK, CHUNK), dtype=bool))
    ws = jnp.where(tril[None], sp_w[0], 0.0).astype(BF16)
    ws_t = jnp.swapaxes(ws, 1, 2)
    bias2d = jnp.repeat(sp_b[0].T, DB, axis=1)
    conv_b2, lag, lab = row(conv_b[0]), row(ln_a_g[0]), row(ln_a_b[0])
    lvg, lvb = row(ln_v_g[0]), row(ln_v_b[0])

    x0 = x.reshape(T, D)
    target = loss_target.reshape(T, D)
    saved = []
    xc = x0
    for l in range(depth):
        xa, a1, b1, h1 = _ffn_fwd(xc, row(g_ffn1[l]), w_ffn[2 * l], 0, f"ffn1_fwd_{l}", dep=token)
        if l % 2 == 0:
            w_in_t, = gathered(gather_a, xa, "gather_a_wait")
            z, hm = _mm_nt(xa, w_in_t, row(g_mix[l]), F32, "mix_in_proj")
            cat, conv_out = _mix_fwd(z, conv_w_full, conv_b2, lag, lab, lvg, lvb, ws, bias2d, n_seq)
            w_out, w_ffn[1] = gathered(gather_b, cat, "gather_b_wait")
            xb = _mm_nn_res(cat, w_out, xa, "mix_out_proj")
            mixer = (z, hm, cat, conv_out)
        else:
            w_qkv_t, w_o_full, w_ffn[3] = gathered(gather_d, xa, "gather_d_wait")
            qkv, hm = _mm_nt(xa, w_qkv_t, row(g_mix[l]), BF16, "qkv_proj")
            o, tot = _attn_fwd(qkv, n_seq)
            xb = _mm_nn_res(o, w_o_full, xa, "attn_out_proj")
            mixer = (qkv, hm, o, tot)
        xn, a2, b2, h2 = _ffn_fwd(xb, row(g_ffn2[l]), w_ffn[2 * l + 1], 0, f"ffn2_fwd_{l}")
        saved.append((xc, a1, b1, h1, xa, mixer, xb, a2, b2, h2))
        xc = xn
        if l == 0:
            w_ffn[2], = gathered(gather_c, xc, "gather_c_wait")

    g, loss_part, dg_final = _loss_head(xc, row(g_final), target)

    dg_ffn1, dg_ffn2, dg_mix = [None] * depth, [None] * depth, [None] * depth
    exchanges = {}
    token = None

    def ffn_back(g, xin, gvec, a, b, h, k, tag, token):
        g, dg, da, db, s, gh = _ffn_bwd(g, xin, gvec, a, b, w_ffn[k], 0, f"ffn{tag}_bwd", dep=token)
        if k == 0:
            return g, dg, (da, db, s, gh, h)
        dws = [_mm_tn(da, h, f"dw_gate{tag}"), _mm_tn(db, h, f"dw_up{tag}"), _mm_tn(s, gh, f"dw_down{tag}")]
        exchanges[f"ffn{k}"], token = _split_start(False, dws, f"exchange_ffn{tag}_start")
        return g, dg, token

    for l in reversed(range(depth)):
        xin, a1, b1, h1, xa, mixer, xb, a2, b2, h2 = saved[l]
        g, dg_ffn2[l], token = ffn_back(g, xb, row(g_ffn2[l]), a2, b2, h2, 2 * l + 1, f"2_{l}", token)
        if l % 2 == 0:
            z, hm, cat, conv_out = mixer
            dcat = _mm_nt(g, w_out, None, F32, "mix_out_bwd", dep=token)
            d_w_out = _mm_tn(cat, g, "dw_out")
            (da1, dz_uv, d_lag, d_lab, d_lvg, d_lvb, d_ws, d_sb) = _mix_bwd_rows(
                dcat, z, conv_out, lag, lab, lvg, lvb, ws, ws_t, bias2d, n_seq)
            dz_a, d_cw, d_cb = _mix_bwd_conv(da1, z, conv_w_full, n_seq)
            d_w_in_t = jnp.concatenate([_mm_tn(dz_a, hm, "dw_in_a"), _mm_tn(dz_uv, hm, "dw_in_uv")])
            exchanges["mix"], token = _split_start(False, [d_w_out, d_w_in_t], "exchange_mix_start")
            g, dg_mix[l] = _mm_nn_rmsbwd([dz_a, dz_uv], w_in_t, xa, row(g_mix[l]), g, "mix_in_bwd", dep=token)
        else:
            qkv, hm, o, tot = mixer
            do = _mm_nt(g, w_o_full, None, BF16, "attn_out_bwd", dep=token)
            d_w_o = _mm_tn(o, g, "dw_o")
            dq, dk, dv = _attn_bwd(qkv, do, tot, n_seq)
            d_w_qkv_t = jnp.concatenate([_mm_tn(dq, hm, "dw_q"), _mm_tn(dk, hm, "dw_k"), _mm_tn(dv, hm, "dw_v")])
            exchanges["attn"], token = _split_start(False, [d_w_o, d_w_qkv_t], "exchange_attn_start")
            g, dg_mix[l] = _mm_nn_rmsbwd([dq, dk, dv], w_qkv_t, xa, row(g_mix[l]), g, "qkv_bwd", dep=token)
        g, dg_ffn1[l], token = ffn_back(g, xin, row(g_ffn1[l]), a1, b1, h1, 2 * l, f"1_{l}", token)
    grad_x = g.reshape(n_seq, S, D)

    small = [jnp.concatenate(dg_ffn1), jnp.concatenate(dg_mix), d_cw, d_cb, d_lag, d_lab, d_lvg, d_lvb,
             jnp.where(tril[None], d_ws, 0.0), d_sb.T, jnp.concatenate(dg_ffn2), dg_final, loss_part[:, :1]]
    small_shapes = [(depth, D), (depth, D), (CONV_WIDTH, CA), (1, CA), (1, CA), (1, CA), (1, GB, DB), (1, GB, DB),
                    (1, GB, CHUNK, CHUNK), (1, GB, CHUNK), (depth, D), (D,), ()]
    small_sum = _small_all_reduce(_pack(small))
    red = _unpack(small_sum, small_shapes)
    (gr_g_ffn1, gr_g_mix, gr_cw_full, gr_cb, gr_lag, gr_lab, gr_lvg, gr_lvb, gr_sp_w, gr_sp_b,
     gr_g_ffn2, gr_g_final, loss) = red
    n_cw = conv_w.shape[2]
    gr_cw = lax.dynamic_slice(gr_cw_full, (0, my_block * n_cw), (CONV_WIDTH, n_cw))[None]

    da, db, s, gh, h = token
    token = small_sum
    for which, lhs, rhs in ((2, s, gh), (1, db, h), (0, da, h)):
        dw = _mm_tn(lhs, rhs, f"dw_ffn0_{which}", dep=token)
        exchanges[f"ffn0_{which}"], token = _split_start(False, [dw], f"exchange_ffn0_{which}_start")

    def landed(key, after):
        out = []
        fulls, lands = _split_wait(exchanges[key], after, f"exchange_{key}_wait")
        for land, full in zip(lands, fulls):
            n = land.shape[1]
            own = lax.dynamic_slice(full, (my_block * n, 0), (n, full.shape[1]))
            out.append(lax.dynamic_update_slice(land, own[None], (my_block, 0, 0)))
        return out

    parts_ffn = [None] * (6 * depth)
    for k in range(1, 2 * depth):
        parts_ffn[3 * k:3 * k + 3] = landed(f"ffn{k}", token)
    parts_out, parts_in = landed("mix", token)
    parts_o, parts_qkv = landed("attn", token)

    grads = {
        "g_ffn1": gr_g_ffn1, "g_mix": gr_g_mix, "conv_w": gr_cw, "conv_b": gr_cb, "ln_a_g": gr_lag,
        "ln_a_b": gr_lab, "ln_v_g": gr_lvg, "ln_v_b": gr_lvb, "sp_w": gr_sp_w, "sp_b": gr_sp_b,
        "g_ffn2": gr_g_ffn2, "g_final": gr_g_final,
    }
    weights = dict(g_ffn1=g_ffn1, w_ffn1_gate=w_ffn1_gate, w_ffn1_up=w_ffn1_up, w_ffn1_down=w_ffn1_down, g_mix=g_mix,
                   w_in_ab=w_in_ab, conv_w=conv_w, conv_b=conv_b, ln_a_g=ln_a_g, ln_a_b=ln_a_b, ln_v_g=ln_v_g,
                   ln_v_b=ln_v_b, sp_w=sp_w, sp_b=sp_b, w_out_ab=w_out_ab, w_qkv=w_qkv, w_o=w_o, g_ffn2=g_ffn2,
                   w_ffn2_gate=w_ffn2_gate, w_ffn2_up=w_ffn2_up, w_ffn2_down=w_ffn2_down, g_final=g_final)
    m_in = dict(g_ffn1=m_g_ffn1, w_ffn1_gate=m_w_ffn1_gate, w_ffn1_up=m_w_ffn1_up, w_ffn1_down=m_w_ffn1_down,
                g_mix=m_g_mix, w_in_ab=m_w_in_ab, conv_w=m_conv_w, conv_b=m_conv_b, ln_a_g=m_ln_a_g, ln_a_b=m_ln_a_b,
                ln_v_g=m_ln_v_g, ln_v_b=m_ln_v_b, sp_w=m_sp_w, sp_b=m_sp_b, w_out_ab=m_w_out_ab, w_qkv=m_w_qkv,
                w_o=m_w_o, g_ffn2=m_g_ffn2, w_ffn2_gate=m_w_ffn2_gate, w_ffn2_up=m_w_ffn2_up,
                w_ffn2_down=m_w_ffn2_down, g_final=m_g_final)
    v_in = dict(g_ffn1=v_g_ffn1, w_ffn1_gate=v_w_ffn1_gate, w_ffn1_up=v_w_ffn1_up, w_ffn1_down=v_w_ffn1_down,
                g_mix=v_g_mix, w_in_ab=v_w_in_ab, conv_w=v_conv_w, conv_b=v_conv_b, ln_a_g=v_ln_a_g, ln_a_b=v_ln_a_b,
                ln_v_g=v_ln_v_g, ln_v_b=v_ln_v_b, sp_w=v_sp_w, sp_b=v_sp_b, w_out_ab=v_w_out_ab, w_qkv=v_w_qkv,
                w_o=v_w_o, g_ffn2=v_g_ffn2, w_ffn2_gate=v_w_ffn2_gate, w_ffn2_up=v_w_ffn2_up,
                w_ffn2_down=v_w_ffn2_down, g_final=v_g_final)
    names = list(weights)
    grads = {n: grads[n].reshape(weights[n].shape) for n in grads}

    delta, new_m, new_v = {}, {}, {}

    def adamw_big(n, parts):
        if weights[n].shape[-1] == D:
            view = back = lambda a: a
        else:
            view = back = lambda a: jnp.swapaxes(a, 1, 2)
        out = _adamw_parts(parts, view(weights[n]), view(m_in[n]), view(v_in[n]), f"adamw_{n}")
        grads[n], delta[n], new_m[n], new_v[n] = [back(a) for a in out]

    adamw_big("w_in_ab", [parts_in])
    adamw_big("w_out_ab", [parts_out])
    adamw_big("w_qkv", [parts_qkv])
    adamw_big("w_o", [parts_o])
    kinds = ("gate", "up", "down")
    for which, kind in enumerate(kinds):
        adamw_big(f"w_ffn2_{kind}", [parts_ffn[_ffn_index(l, 1) + which] for l in range(depth)])
    big = [n for n in names if n.startswith("w_")]
    after = jnp.concatenate([delta[n].reshape(-1)[:1] for n in big if n in delta]).reshape(1, -1)
    for which in (2, 1, 0):
        parts_ffn[which], = landed(f"ffn0_{which}", after)
    for which, kind in enumerate(kinds):
        adamw_big(f"w_ffn1_{kind}", [parts_ffn[_ffn_index(l, 0) + which] for l in range(depth)])
    little = [n for n in names if n not in big]
    shapes = [weights[n].shape for n in little]
    d, nm, nv = _adamw(_pack([weights[n] for n in little]), _pack([grads[n] for n in little]),
                       _pack([m_in[n] for n in little]), _pack([v_in[n] for n in little]), "adamw_small")
    for n, dd, mm, vv in zip(little, _unpack(d, shapes), _unpack(nm, shapes), _unpack(nv, shapes)):
        delta[n], new_m[n], new_v[n] = dd, mm, vv

    return (loss, grad_x, *[grads[n] for n in names], *[delta[n] for n in names],
            *[new_m[n] for n in names], *[new_v[n] for n in names])
```

```python
import functools

import jax
import jax.numpy as jnp
from jax import lax
from jax.experimental import pallas as pl
from jax.experimental.pallas import tpu as pltpu

F32 = jnp.float32
BF16 = jnp.bfloat16

D_MODEL = 1024
CA = 512
CB = 512
GB = 4
DB = 128
CHUNK = 128
CONV_WIDTH = 31
N_HEADS = 16
HEAD_DIM = 64
EPS = 1e-6
N_DEV = 8
LANES = 128
SUBLANES = 8
QB = 128
ATT_TQ = 1024
FFN_TN = 1408
HALO = 32
CONV_ROWS = 32
ATT_SCALE = HEAD_DIM ** -0.5

ADAM_LR = 0.001
ADAM_B1 = 0.9
ADAM_B2 = 0.999
ADAM_EPS = 1e-08
ADAM_WD = 0.01
ADAM_STEP = 10

NT = (((1,), (1,)), ((), ()))
NN = (((1,), (0,)), ((), ()))
TN = (((0,), (0,)), ((), ()))
MESH = pl.DeviceIdType.MESH
ANY = pl.BlockSpec(memory_space=pl.ANY)
VMEM_LIMIT = 60 * 1024 * 1024


def _dot(a, b, dims):
    return lax.dot_general(a, b, dims, preferred_element_type=F32)


def _cp(*sem):
    return pltpu.CompilerParams(dimension_semantics=sem, vmem_limit_bytes=VMEM_LIMIT)


def _pcall(body, *, in_specs, args, dep=None, **kw):
    if dep is not None:
        n_in = len(in_specs)
        inner = body

        def body(*refs):
            inner(*refs[:n_in], *refs[n_in + 1:])

        in_specs = list(in_specs) + [ANY]
        args = tuple(args) + (dep,)
    return pl.pallas_call(body, in_specs=list(in_specs), **kw)(*args)


def _tile(n, want):
    if n <= want:
        return n
    t = want - want % LANES
    while t > LANES and n % t:
        t -= LANES
    assert n % t == 0, (n, want)
    return t


def _sigmoid(x):
    return 0.5 * jnp.tanh(0.5 * x) + 0.5


def _rstd(x):
    return lax.rsqrt(jnp.mean(x * x, axis=-1, keepdims=True) + EPS)


def _rms_bwd(x, g, dh):
    r = _rstd(x)
    u = dh * g
    dx = r * (u - x * (r * r) * jnp.mean(u * x, axis=-1, keepdims=True))
    dg = jnp.sum(dh * x * r, axis=0, keepdims=True)
    return dx, dg


def _ln_fwd(x, g, b):
    mu = jnp.mean(x, axis=-1, keepdims=True)
    xc = x - mu
    r = lax.rsqrt(jnp.mean(xc * xc, axis=-1, keepdims=True) + EPS)
    xh = xc * r
    return xh * g + b, xh, r


def _ln_bwd(dy, xh, r, g):
    dxh = dy * g
    return r * (dxh - jnp.mean(dxh, axis=-1, keepdims=True)
                - xh * jnp.mean(dxh * xh, axis=-1, keepdims=True))


def _ffn_fwd(x, g, wall, base, name, dep=None):
    T, D = x.shape
    F = wall.shape[1]
    tm, tn = _tile(T, 512), _tile(F, FFN_TN)
    n_j = F // tn

    def body(x_ref, g_ref, wg_ref, wu_ref, wd_ref, xo_ref, a_ref, b_ref, h_ref, acc_ref):
        j = pl.program_id(1)

        @pl.when(j == 0)
        def _():
            xv = x_ref[...]
            h_ref[...] = (xv * _rstd(xv) * g_ref[...]).astype(BF16)
            acc_ref[...] = jnp.zeros_like(acc_ref)

        h = h_ref[...]
        a = _dot(h, wg_ref[...], NT)
        b = _dot(h, wu_ref[...], NT)
        a_ref[...] = a.astype(BF16)
        b_ref[...] = b.astype(BF16)
        s = (a * _sigmoid(a) * b).astype(BF16)
        acc_ref[...] += _dot(s, wd_ref[...], NN)

        @pl.when(j == n_j - 1)
        def _():
            xo_ref[...] = x_ref[...] + 0.5 * acc_ref[...]

    wspec = lambda k: pl.BlockSpec((None, tn, D), lambda i, j: (base + k, j, 0))
    return _pcall(
        body, name=name, grid=(T // tm, n_j), dep=dep, args=(x, g, wall, wall, wall),
        in_specs=[pl.BlockSpec((tm, D), lambda i, j: (i, 0)), pl.BlockSpec((1, D), lambda i, j: (0, 0)),
                  wspec(0), wspec(1), wspec(2)],
        out_specs=[pl.BlockSpec((tm, D), lambda i, j: (i, 0)), pl.BlockSpec((tm, tn), lambda i, j: (i, j)),
                   pl.BlockSpec((tm, tn), lambda i, j: (i, j)), pl.BlockSpec((tm, D), lambda i, j: (i, 0))],
        out_shape=[jax.ShapeDtypeStruct((T, D), F32), jax.ShapeDtypeStruct((T, F), BF16),
                   jax.ShapeDtypeStruct((T, F), BF16), jax.ShapeDtypeStruct((T, D), BF16)],
        scratch_shapes=[pltpu.VMEM((tm, D), F32)],
        compiler_params=_cp("parallel", "arbitrary"),
    )


def _ffn_bwd(go, x, g, a, b, wall, base, name, dep=None):
    T, D = x.shape
    F = wall.shape[1]
    tm, tn = _tile(T, 512), _tile(F, FFN_TN)
    n_j = F // tn

    def body(go_ref, x_ref, g_ref, a_ref, b_ref, wg_ref, wu_ref, wd_ref,
             gx_ref, dg_ref, da_ref, db_ref, s_ref, gh_ref, acc_ref):
        i, j = pl.program_id(0), pl.program_id(1)

        @pl.when(j == 0)
        def _():
            gh_ref[...] = (0.5 * go_ref[...]).astype(BF16)
            acc_ref[...] = jnp.zeros_like(acc_ref)

        @pl.when((i == 0) & (j == 0))
        def _():
            dg_ref[...] = jnp.zeros_like(dg_ref)

        ds = _dot(gh_ref[...], wd_ref[...], NT)
        av = a_ref[...].astype(F32)
        bv = b_ref[...].astype(F32)
        sig = _sigmoid(av)
        sl = av * sig
        dab = ((ds * bv) * (sig + sl * (1.0 - sig))).astype(BF16)
        dbb = (ds * sl).astype(BF16)
        s_ref[...] = (sl * bv).astype(BF16)
        da_ref[...] = dab
        db_ref[...] = dbb
        acc_ref[...] += _dot(dab, wg_ref[...], NN) + _dot(dbb, wu_ref[...], NN)

        @pl.when(j == n_j - 1)
        def _():
            dx, dg = _rms_bwd(x_ref[...], g_ref[...], acc_ref[...])
            gx_ref[...] = go_ref[...] + dx
            dg_ref[...] += dg

    wspec = lambda k: pl.BlockSpec((None, tn, D), lambda i, j: (base + k, j, 0))
    row = pl.BlockSpec((tm, D), lambda i, j: (i, 0))
    hid = pl.BlockSpec((tm, tn), lambda i, j: (i, j))
    vec = pl.BlockSpec((1, D), lambda i, j: (0, 0))
    return _pcall(
        body, name=name, grid=(T // tm, n_j), dep=dep, args=(go, x, g, a, b, wall, wall, wall),
        in_specs=[row, row, vec, hid, hid, wspec(0), wspec(1), wspec(2)],
        out_specs=[row, vec, hid, hid, hid, row],
        out_shape=[jax.ShapeDtypeStruct((T, D), F32), jax.ShapeDtypeStruct((1, D), F32),
                   jax.ShapeDtypeStruct((T, F), BF16), jax.ShapeDtypeStruct((T, F), BF16),
                   jax.ShapeDtypeStruct((T, F), BF16), jax.ShapeDtypeStruct((T, D), BF16)],
        scratch_shapes=[pltpu.VMEM((tm, D), F32)],
        compiler_params=_cp("arbitrary", "arbitrary"),
    )


def _mm_tn(a, b, name, dep=None):
    T, M = a.shape
    N = b.shape[1]
    tmm, tk = _tile(M, 1536), _tile(T, 1024)

    def body(a_ref, b_ref, o_ref):
        @pl.when(pl.program_id(1) == 0)
        def _():
            o_ref[...] = jnp.zeros_like(o_ref)

        o_ref[...] += _dot(a_ref[...].astype(BF16), b_ref[...].astype(BF16), TN)

    return _pcall(
        body, name=name, grid=(M // tmm, T // tk), dep=dep, args=(a, b),
        in_specs=[pl.BlockSpec((tk, tmm), lambda m, k: (k, m)), pl.BlockSpec((tk, N), lambda m, k: (k, 0))],
        out_specs=pl.BlockSpec((tmm, N), lambda m, k: (m, 0)),
        out_shape=jax.ShapeDtypeStruct((M, N), F32),
        compiler_params=_cp("parallel", "arbitrary"),
    )


def _mm_nt(x, wt, g, out_dtype, name, dep=None):
    T, K = x.shape
    N = wt.shape[0]
    tm, tn = _tile(T, 512), N
    norm = g is not None

    def body(*refs):
        if norm:
            x_ref, g_ref, w_ref, o_ref, h_ref = refs
        else:
            x_ref, w_ref, o_ref, h_ref = refs

        @pl.when(pl.program_id(1) == 0)
        def _():
            xv = x_ref[...].astype(F32)
            if norm:
                xv = xv * _rstd(xv) * g_ref[...]
            h_ref[...] = xv.astype(BF16)

        o_ref[...] = _dot(h_ref[...], w_ref[...], NT).astype(out_dtype)

    row = pl.BlockSpec((tm, K), lambda i, j: (i, 0))
    wsp = pl.BlockSpec((tn, K), lambda i, j: (j, 0))
    osp = pl.BlockSpec((tm, tn), lambda i, j: (i, j))
    if norm:
        return pl.pallas_call(
            body, name=name, grid=(T // tm, N // tn),
            in_specs=[row, pl.BlockSpec((1, K), lambda i, j: (0, 0)), wsp],
            out_specs=[osp, row],
            out_shape=[jax.ShapeDtypeStruct((T, N), out_dtype), jax.ShapeDtypeStruct((T, K), BF16)],
            compiler_params=_cp("parallel", "arbitrary"),
        )(x, g, wt)
    return _pcall(
        body, name=name, grid=(T // tm, N // tn), dep=dep, args=(x, wt),
        in_specs=[row, wsp], out_specs=osp,
        out_shape=jax.ShapeDtypeStruct((T, N), out_dtype),
        scratch_shapes=[pltpu.VMEM((tm, K), BF16)],
        compiler_params=_cp("parallel", "arbitrary"),
    )


def _mm_nn_res(act, w, resid, name):
    T, K = act.shape
    D = w.shape[1]
    tm = _tile(T, 512)

    def body(a_ref, w_ref, r_ref, o_ref):
        o_ref[...] = r_ref[...] + _dot(a_ref[...].astype(BF16), w_ref[...], NN)

    return pl.pallas_call(
        body, name=name, grid=(T // tm,),
        in_specs=[pl.BlockSpec((tm, K), lambda i: (i, 0)), pl.BlockSpec((K, D), lambda i: (0, 0)),
                  pl.BlockSpec((tm, D), lambda i: (i, 0))],
        out_specs=pl.BlockSpec((tm, D), lambda i: (i, 0)),
        out_shape=jax.ShapeDtypeStruct((T, D), F32),
        compiler_params=_cp("parallel"),
    )(act, w, resid)


def _mm_nn_rmsbwd(acts, w, x, g, gprev, name, dep=None):
    T = acts[0].shape[0]
    ks = [a.shape[1] for a in acts]
    K, D = w.shape
    assert sum(ks) == K
    tm = _tile(T, 512)
    na = len(acts)

    def body(*refs):
        a_refs = refs[:na]
        w_ref, x_ref, g_ref, gp_ref, o_ref, dg_ref = refs[na:]

        @pl.when(pl.program_id(0) == 0)
        def _():
            dg_ref[...] = jnp.zeros_like(dg_ref)

        dh, off = None, 0
        for a_ref, k in zip(a_refs, ks):
            part = _dot(a_ref[...].astype(BF16), w_ref[off:off + k, :], NN)
            dh = part if dh is None else dh + part
            off += k
        dx, dg = _rms_bwd(x_ref[...], g_ref[...], dh)
        o_ref[...] = gp_ref[...] + dx
        dg_ref[...] += dg

    row = pl.BlockSpec((tm, D), lambda i: (i, 0))
    vec = pl.BlockSpec((1, D), lambda i: (0, 0))
    return _pcall(
        body, name=name, grid=(T // tm,), dep=dep, args=(*acts, w, x, g, gprev),
        in_specs=[pl.BlockSpec((tm, k), lambda i: (i, 0)) for k in ks]
        + [pl.BlockSpec((K, D), lambda i: (0, 0)), row, vec, row],
        out_specs=[row, vec],
        out_shape=[jax.ShapeDtypeStruct((T, D), F32), jax.ShapeDtypeStruct((1, D), F32)],
        compiler_params=_cp("arbitrary"),
    )


def _loss_head(x, g, target):
    T, D = x.shape
    tm = _tile(T, 512)

    def body(x_ref, g_ref, t_ref, dx_ref, loss_ref, dg_ref):
        @pl.when(pl.program_id(0) == 0)
        def _():
            loss_ref[...] = jnp.zeros_like(loss_ref)
            dg_ref[...] = jnp.zeros_like(dg_ref)

        xv = x_ref[...]
        gv = g_ref[...]
        e = xv * _rstd(xv) * gv - t_ref[...]
        per_tok = jnp.sum(e * e, axis=-1, keepdims=True) * (1.0 / D)
        loss_ref[...] += 0.5 * jnp.sum(per_tok, axis=0, keepdims=True)
        dx, dg = _rms_bwd(xv, gv, e * (1.0 / D))
        dx_ref[...] = dx
        dg_ref[...] += dg

    row = pl.BlockSpec((tm, D), lambda i: (i, 0))
    vec = pl.BlockSpec((1, D), lambda i: (0, 0))
    return pl.pallas_call(
        body, name="loss_head", grid=(T // tm,),
        in_specs=[row, vec, row],
        out_specs=[row, pl.BlockSpec((1, LANES), lambda i: (0, 0)), vec],
        out_shape=[jax.ShapeDtypeStruct((T, D), F32), jax.ShapeDtypeStruct((1, LANES), F32),
                   jax.ShapeDtypeStruct((1, D), F32)],
        compiler_params=_cp("arbitrary"),
    )(x, g, target)


def _log_gates(z):
    ls = jnp.minimum(z, 0.0) - jnp.log(1.0 + jnp.exp(-jnp.abs(z)))
    return ls, ls - z


def _cumsum_mm(v, u2):
    hi = v.astype(BF16)
    lo = (v - hi.astype(F32)).astype(BF16)
    return _dot(jnp.concatenate([hi, lo], axis=1), u2, NN)


def _half_rowsum(v):
    n = v.shape[0]
    s0 = jnp.sum(v[:, :QB], axis=1, keepdims=True)
    s1 = jnp.sum(v[:, QB:], axis=1, keepdims=True)
    return jnp.concatenate([jnp.broadcast_to(s0, (n, QB)), jnp.broadcast_to(s1, (n, QB))], axis=1)


def _stack_heads(src_ref, dst_ref, n_blk):
    m0 = lax.broadcasted_iota(jnp.int32, (1, LANES), 1) < HEAD_DIM

    def fill(c, carry):
        blk = src_ref[pl.ds(pl.multiple_of(c * QB, QB), QB), :]
        zero = jnp.zeros_like(blk)
        dst_ref[c, 0:QB, :] = jnp.where(m0, blk, zero)
        dst_ref[c, QB:2 * QB, :] = jnp.where(m0, zero, blk)
        return carry

    lax.fori_loop(0, n_blk, fill, 0)


def _diag_mask(tq, j):
    n = tq - j * QB
    row = lax.broadcasted_iota(jnp.int32, (n, 2 * QB), 0)
    col = lax.broadcasted_iota(jnp.int32, (n, 2 * QB), 1)
    return (col & (QB - 1)) < row


def _tri_blockdiag(upper):
    r = lax.broadcasted_iota(jnp.int32, (2 * QB, 2 * QB), 0)
    c = lax.broadcasted_iota(jnp.int32, (2 * QB, 2 * QB), 1)
    same = (r // QB) == (c // QB)
    u = (same & ((r > c) if upper else (r < c))).astype(BF16)
    return jnp.concatenate([u, u], axis=0)


def _attn_tiles(T, n_seq):
    S = T // n_seq
    tq = ATT_TQ if S % ATT_TQ == 0 else QB
    return S, tq, tq // QB, S // tq, S // QB


def _attn_fwd(qkv, n_seq):
    T = qkv.shape[0]
    S, tq, r, n_q, n_k = _attn_tiles(T, n_seq)
    n_p = D_MODEL // LANES
    u_suffix = _tri_blockdiag(True)

    def body(q_ref, k_ref, v_ref, u_ref, o_ref, tot_ref, kk_ref, vv_ref, lr_s, acc_s):
        qi = pl.program_id(2)

        @pl.when(qi == 0)
        def _():
            _stack_heads(k_ref, kk_ref, n_k)
            _stack_heads(v_ref, vv_ref, n_k)

        u = u_ref[...]
        lr_s[...] = jnp.zeros_like(lr_s)
        acc_s[...] = jnp.zeros_like(acc_s)

        def step(kj, q, mask, lr, acc):
            ls, lk = _log_gates(_dot(q, kk_ref[kj], NT))
            if mask is not None:
                lk = jnp.where(mask, lk, 0.0)
            a = jnp.exp(ls + _cumsum_mm(lk, u) + lr)
            if mask is not None:
                a = jnp.where(mask, a, 0.0)
            return lr + _half_rowsum(lk), acc + _dot(a.astype(BF16), vv_ref[kj], NN)

        for j in reversed(range(r)):
            rows = slice(j * QB, tq)
            lr, acc = step(qi * r + j, q_ref[rows, :] * ATT_SCALE, _diag_mask(tq, j), lr_s[rows, :], acc_s[rows, :])
            lr_s[rows, :] = lr
            acc_s[rows, :] = acc

        q = q_ref[...] * ATT_SCALE

        def off(it, carry):
            lr, acc = lr_s[...], acc_s[...]
            for j in range(r):
                lr, acc = step((qi - it) * r - 1 - j, q, None, lr, acc)
            lr_s[...] = lr
            acc_s[...] = acc
            return carry

        lax.fori_loop(0, qi, off, 0)
        o_ref[...] = acc_s[...].astype(BF16)
        tot_ref[...] = lr_s[...]

    return pl.pallas_call(
        body, name="attn_fwd", grid=(n_seq, n_p, n_q),
        in_specs=[pl.BlockSpec((tq, LANES), lambda b, p, qi: (b * n_q + qi, p)),
                  pl.BlockSpec((S, LANES), lambda b, p, qi: (b, n_p + p)),
                  pl.BlockSpec((S, LANES), lambda b, p, qi: (b, 2 * n_p + p)),
                  pl.BlockSpec((4 * QB, 2 * QB), lambda b, p, qi: (0, 0))],
        out_specs=[pl.BlockSpec((tq, LANES), lambda b, p, qi: (b * n_q + qi, p)),
                   pl.BlockSpec((tq, 2 * QB), lambda b, p, qi: (b * n_q + qi, p))],
        out_shape=[jax.ShapeDtypeStruct((T, D_MODEL), BF16), jax.ShapeDtypeStruct((T, 2 * D_MODEL), F32)],
        scratch_shapes=[pltpu.VMEM((n_k, 2 * QB, LANES), BF16), pltpu.VMEM((n_k, 2 * QB, LANES), BF16),
                        pltpu.VMEM((tq, 2 * QB), F32), pltpu.VMEM((tq, LANES), F32)],
        compiler_params=_cp("parallel", "parallel", "arbitrary"),
    )(qkv, qkv, qkv, u_suffix)


def _attn_bwd(qkv, do, tot, n_seq):
    T = qkv.shape[0]
    S, tq, r, n_q, n_k = _attn_tiles(T, n_seq)
    n_p = D_MODEL // LANES
    u_prefix = _tri_blockdiag(False)

    def body(q_ref, k_ref, v_ref, do_ref, tot_ref, u_ref, dq_ref, dk_out, dv_out,
             kk_ref, vv_ref, cl_s, cg_s, dq_s, dk_ref, dv_ref):
        qi = pl.program_id(2)

        @pl.when(qi == 0)
        def _():
            _stack_heads(k_ref, kk_ref, n_k)
            _stack_heads(v_ref, vv_ref, n_k)
            dk_ref[...] = jnp.zeros_like(dk_ref)
            dv_ref[...] = jnp.zeros_like(dv_ref)

        u = u_ref[...]
        m0 = lax.broadcasted_iota(jnp.int32, (1, LANES), 1) < HEAD_DIM
        cl_s[...] = tot_ref[...]
        cg_s[...] = jnp.zeros_like(cg_s)
        dq_s[...] = jnp.zeros_like(dq_s)

        def step(kj, q, dov, mask, rest, cg, dq):
            kk = kk_ref[kj]
            ls, lk = _log_gates(_dot(q, kk, NT))
            if mask is not None:
                lk = jnp.where(mask, lk, 0.0)
            a = jnp.exp(ls + (rest - (_cumsum_mm(lk, u) + lk)))
            if mask is not None:
                a = jnp.where(mask, a, 0.0)
            g = a * _dot(dov, vv_ref[kj], NT)
            dz = g - (g + _dot(g.astype(BF16), u[:2 * QB], NN) + cg) * jnp.exp(ls)
            if mask is not None:
                dz = jnp.where(mask, dz, 0.0)
            dz = dz.astype(BF16)
            rows = pl.ds(pl.multiple_of(kj * QB, QB), QB)
            dvt = _dot(a.astype(BF16), dov, TN)
            dv_ref[rows, :] += jnp.where(m0, dvt[:QB], dvt[QB:])
            dkt = _dot(dz, q, TN)
            dk_ref[rows, :] += jnp.where(m0, dkt[:QB], dkt[QB:])
            return rest - _half_rowsum(lk), cg + _half_rowsum(g), dq + _dot(dz, kk, NN)

        q = q_ref[...] * ATT_SCALE
        dov = do_ref[...]

        def off(it, carry):
            cl, cg, dq = cl_s[...], cg_s[...], dq_s[...]
            for j in range(r):
                cl, cg, dq = step(it * r + j, q, dov, None, cl, cg, dq)
            cl_s[...] = cl
            cg_s[...] = cg
            dq_s[...] = dq
            return carry

        lax.fori_loop(0, qi, off, 0)

        for j in range(r):
            rows = slice(j * QB, tq)
            cl, cg, dq = step(qi * r + j, q_ref[rows, :] * ATT_SCALE, do_ref[rows, :],
                              _diag_mask(tq, j), cl_s[rows, :], cg_s[rows, :], dq_s[rows, :])
            cl_s[rows, :] = cl
            cg_s[rows, :] = cg
            dq_s[rows, :] = dq
        dq_ref[...] = (dq_s[...] * ATT_SCALE).astype(BF16)

        @pl.when(qi == n_q - 1)
        def _():
            dk_out[...] = dk_ref[...].astype(BF16)
            dv_out[...] = dv_ref[...].astype(BF16)

    qspec = pl.BlockSpec((tq, LANES), lambda b, p, qi: (b * n_q + qi, p))
    seq = lambda off: pl.BlockSpec((S, LANES), lambda b, p, qi: (b, off + p))
    return pl.pallas_call(
        body, name="attn_bwd", grid=(n_seq, n_p, n_q),
        in_specs=[qspec, seq(n_p), seq(2 * n_p), qspec,
                  pl.BlockSpec((tq, 2 * QB), lambda b, p, qi: (b * n_q + qi, p)),
                  pl.BlockSpec((4 * QB, 2 * QB), lambda b, p, qi: (0, 0))],
        out_specs=[qspec, seq(0), seq(0)],
        out_shape=[jax.ShapeDtypeStruct((T, D_MODEL), BF16)] * 3,
        scratch_shapes=[pltpu.VMEM((n_k, 2 * QB, LANES), BF16), pltpu.VMEM((n_k, 2 * QB, LANES), BF16),
                        pltpu.VMEM((tq, 2 * QB), F32), pltpu.VMEM((tq, 2 * QB), F32), pltpu.VMEM((tq, LANES), F32),
                        pltpu.VMEM((S, LANES), F32), pltpu.VMEM((S, LANES), F32)],
        compiler_params=_cp("parallel", "parallel", "arbitrary"),
    )(qkv, qkv, qkv, do, tot, u_prefix)


def _shifted_copies(sh_ref):
    rows = sh_ref.shape[1] - SUBLANES
    for s in range(1, SUBLANES):
        sh_ref[s, 0:rows, :] = sh_ref[0, s:s + rows, :]


def _shifted(sh_ref, start, n):
    s = start % SUBLANES
    return sh_ref[s, start - s:start - s + n, :]


def _glu_with_halo(av_ref, ag_ref, avh_ref, agh_ref, a0_s, first, ts):
    hal = avh_ref[...] * _sigmoid(agh_ref[...])
    a0_s[0, 0:HALO, :] = jnp.where(first, 0.0, hal)
    a0_s[0, HALO:HALO + ts, :] = av_ref[...] * _sigmoid(ag_ref[...])
    _shifted_copies(a0_s)


def _mix_specs(ts, n_r, with_left):
    blk = lambda c: pl.BlockSpec((ts, CA), lambda b, r: (b * n_r + r, c))
    per = ts // HALO
    left = lambda c: pl.BlockSpec((HALO, CA), lambda b, r: (jnp.maximum((b * n_r + r) * per - 1, 0), c))
    return blk, (left if with_left else None)


def _mix_fwd(z, conv_w, conv_b, ln_a_g, ln_a_b, ln_v_g, ln_v_b, ws, bias2d, n_seq):
    T = z.shape[0]
    S = T // n_seq
    ts = _tile(S, 512)
    n_r = S // ts
    shift = HALO - (CONV_WIDTH - 1)

    def body(av_ref, ag_ref, avh_ref, agh_ref, u_ref, v_ref, cw_ref, cb_ref, lag_ref, lab_ref,
             lvg_ref, lvb_ref, ws_ref, bias_ref, cat_ref, a1_ref, a0_s):
        _glu_with_halo(av_ref, ag_ref, avh_ref, agh_ref, a0_s, pl.program_id(1) == 0, ts)
        for rb in range(ts // CONV_ROWS):
            base = rb * CONV_ROWS
            acc = jnp.broadcast_to(cb_ref[...], (CONV_ROWS, CA))
            for k in range(CONV_WIDTH):
                acc = acc + cw_ref[k:k + 1, :] * _shifted(a0_s, base + shift + k, CONV_ROWS)
            a1_ref[base:base + CONV_ROWS, :] = acc
        y, _, _ = _ln_fwd(a1_ref[...], lag_ref[...], lab_ref[...])
        cat_ref[:, 0:CA] = (y * _sigmoid(y)).astype(BF16)
        for gi in range(GB):
            sl = slice(gi * DB, (gi + 1) * DB)
            v1, _, _ = _ln_fwd(v_ref[:, sl], lvg_ref[:, sl], lvb_ref[:, sl])
            v1 = v1.astype(BF16)
            for c in range(ts // CHUNK):
                rs = slice(c * CHUNK, (c + 1) * CHUNK)
                v2 = _dot(ws_ref[gi], v1[rs], NN) + bias_ref[:, sl]
                cat_ref[rs, CA + gi * DB:CA + (gi + 1) * DB] = (u_ref[rs, sl] * v2).astype(BF16)

    blk, left = _mix_specs(ts, n_r, True)
    vec = pl.BlockSpec((1, CA), lambda b, r: (0, 0))
    return pl.pallas_call(
        body, name="mix_fwd", grid=(n_seq, n_r),
        in_specs=[blk(0), blk(1), left(0), left(1), blk(2), blk(3),
                  pl.BlockSpec((CONV_WIDTH, CA), lambda b, r: (0, 0)), vec, vec, vec, vec, vec,
                  pl.BlockSpec((GB, CHUNK, CHUNK), lambda b, r: (0, 0, 0)),
                  pl.BlockSpec((CHUNK, CB), lambda b, r: (0, 0))],
        out_specs=[pl.BlockSpec((ts, CA + CB), lambda b, r: (b * n_r + r, 0)), blk(0)],
        out_shape=[jax.ShapeDtypeStruct((T, CA + CB), BF16), jax.ShapeDtypeStruct((T, CA), F32)],
        scratch_shapes=[pltpu.VMEM((SUBLANES, HALO + ts, CA), F32)],
        compiler_params=_cp("parallel", "parallel"),
    )(z, z, z, z, z, z, conv_w, conv_b, ln_a_g, ln_a_b, ln_v_g, ln_v_b, ws, bias2d)


def _mix_bwd_rows(dcat, z, a1, ln_a_g, ln_a_b, ln_v_g, ln_v_b, ws, ws_t, bias2d, n_seq):
    T = z.shape[0]
    S = T // n_seq
    ts = _tile(S, 512)
    n_r = S // ts

    def body(dc_ref, u_ref, v_ref, a1_ref, lag_ref, lab_ref, lvg_ref, lvb_ref, ws_ref, wst_ref, bias_ref,
             da1_ref, dz_ref, dlag_ref, dlab_ref, dlvg_ref, dlvb_ref, dws_ref, dsb_ref, dv1_s, dbias_s):
        first = (pl.program_id(0) == 0) & (pl.program_id(1) == 0)
        last = (pl.program_id(0) == n_seq - 1) & (pl.program_id(1) == n_r - 1)

        @pl.when(first)
        def _():
            for ref in (dlag_ref, dlab_ref, dlvg_ref, dlvb_ref, dws_ref, dbias_s):
                ref[...] = jnp.zeros_like(ref)

        lag = lag_ref[...]
        y, xh, r = _ln_fwd(a1_ref[...], lag, lab_ref[...])
        sig = _sigmoid(y)
        dy = dc_ref[:, 0:CA] * (sig * (1.0 + y * (1.0 - sig)))
        dlag_ref[...] += jnp.sum(dy * xh, axis=0, keepdims=True)
        dlab_ref[...] += jnp.sum(dy, axis=0, keepdims=True)
        da1_ref[...] = _ln_bwd(dy, xh, r, lag)

        tril = (lax.broadcasted_iota(jnp.int32, (CHUNK, CHUNK), 0)
                >= lax.broadcasted_iota(jnp.int32, (CHUNK, CHUNK), 1))
        for gi in range(GB):
            sl = slice(gi * DB, (gi + 1) * DB)
            lvg = lvg_ref[:, sl]
            v1, vh, vr = _ln_fwd(v_ref[:, sl], lvg, lvb_ref[:, sl])
            v1 = v1.astype(BF16)
            for c in range(ts // CHUNK):
                rs = slice(c * CHUNK, (c + 1) * CHUNK)
                v2 = _dot(ws_ref[gi], v1[rs], NN) + bias_ref[:, sl]
                dbo = dc_ref[rs, CA + gi * DB:CA + (gi + 1) * DB]
                dz_ref[rs, sl] = (dbo * v2).astype(BF16)
                dv2 = dbo * u_ref[rs, sl]
                dbias_s[:, sl] += dv2
                dv2b = dv2.astype(BF16)
                dws_ref[gi] += jnp.where(tril, _dot(dv2b, v1[rs], NT), 0.0)
                dv1_s[rs, :] = _dot(wst_ref[gi], dv2b, NN)
            dv1 = dv1_s[...]
            dlvg_ref[:, sl] += jnp.sum(dv1 * vh, axis=0, keepdims=True)
            dlvb_ref[:, sl] += jnp.sum(dv1, axis=0, keepdims=True)
            dz_ref[:, CB + gi * DB:CB + (gi + 1) * DB] = _ln_bwd(dv1, vh, vr, lvg).astype(BF16)

        @pl.when(last)
        def _():
            col = lax.broadcasted_iota(jnp.int32, (CHUNK, GB), 1)
            out = jnp.zeros((CHUNK, GB), F32)
            for gi in range(GB):
                s = jnp.sum(dbias_s[:, gi * DB:(gi + 1) * DB], axis=1, keepdims=True)
                out = out + jnp.where(col == gi, s, 0.0)
            dsb_ref[...] = out

    blk, _ = _mix_specs(ts, n_r, False)
    vec = pl.BlockSpec((1, CA), lambda b, r: (0, 0))
    mat = pl.BlockSpec((GB, CHUNK, CHUNK), lambda b, r: (0, 0, 0))
    wide = pl.BlockSpec((ts, CA + CB), lambda b, r: (b * n_r + r, 0))
    return pl.pallas_call(
        body, name="mix_bwd_rows", grid=(n_seq, n_r),
        in_specs=[wide, blk(2), blk(3), blk(0), vec, vec, vec, vec, mat, mat,
                  pl.BlockSpec((CHUNK, CB), lambda b, r: (0, 0))],
        out_specs=[blk(0), wide, vec, vec, vec, vec, mat, pl.BlockSpec((CHUNK, GB), lambda b, r: (0, 0))],
        out_shape=[jax.ShapeDtypeStruct((T, CA), F32), jax.ShapeDtypeStruct((T, 2 * CB), BF16)]
        + [jax.ShapeDtypeStruct((1, CA), F32)] * 4
        + [jax.ShapeDtypeStruct((GB, CHUNK, CHUNK), F32), jax.ShapeDtypeStruct((CHUNK, GB), F32)],
        scratch_shapes=[pltpu.VMEM((ts, DB), F32), pltpu.VMEM((CHUNK, CB), F32)],
        compiler_params=_cp("arbitrary", "arbitrary"),
    )(dcat, z, z, a1, ln_a_g, ln_a_b, ln_v_g, ln_v_b, ws, ws_t, bias2d)


def _mix_bwd_conv(da1, z, conv_w, n_seq):
    T = z.shape[0]
    S = T // n_seq
    ts = _tile(S, 512)
    n_r = S // ts
    per = ts // HALO
    shift = HALO - (CONV_WIDTH - 1)
    fold = CONV_ROWS // 8

    def body(d_ref, dh_ref, av_ref, ag_ref, avh_ref, agh_ref, cw_ref,
             dz_ref, dcw_ref, dcb_ref, a0_s, d1_s, da0_s, dw8_s):
        first = (pl.program_id(0) == 0) & (pl.program_id(1) == 0)
        last = (pl.program_id(0) == n_seq - 1) & (pl.program_id(1) == n_r - 1)

        @pl.when(first)
        def _():
            dw8_s[...] = jnp.zeros_like(dw8_s)
            dcb_ref[...] = jnp.zeros_like(dcb_ref)

        _glu_with_halo(av_ref, ag_ref, avh_ref, agh_ref, a0_s, pl.program_id(1) == 0, ts)
        d1_s[0, 0:ts, :] = d_ref[...]
        d1_s[0, ts:ts + HALO, :] = jnp.where(pl.program_id(1) == n_r - 1, 0.0, dh_ref[...])
        _shifted_copies(d1_s)
        dcb_ref[...] += jnp.sum(d_ref[...], axis=0, keepdims=True)
        for rb in range(ts // CONV_ROWS):
            base = rb * CONV_ROWS
            dcur = d1_s[0, base:base + CONV_ROWS, :]
            acc = jnp.zeros((CONV_ROWS, CA), F32)
            for k in range(CONV_WIDTH):
                back = CONV_WIDTH - 1 - k
                acc = acc + cw_ref[k:k + 1, :] * _shifted(d1_s, base + back, CONV_ROWS)
                prod = dcur * _shifted(a0_s, base + shift + k, CONV_ROWS)
                part = prod[0:8]
                for f in range(1, fold):
                    part = part + prod[8 * f:8 * f + 8]
                dw8_s[k] += part
            da0_s[base:base + CONV_ROWS, :] = acc
        da0 = da0_s[...]
        sig = _sigmoid(ag_ref[...])
        dz_ref[:, 0:CA] = (da0 * sig).astype(BF16)
        dz_ref[:, CA:2 * CA] = (da0 * av_ref[...] * sig * (1.0 - sig)).astype(BF16)

        @pl.when(last)
        def _():
            for k in range(CONV_WIDTH):
                dcw_ref[k:k + 1, :] = jnp.sum(dw8_s[k], axis=0, keepdims=True)

    blk, left = _mix_specs(ts, n_r, True)
    n_halo_blocks = T // HALO
    right = pl.BlockSpec((HALO, CA), lambda b, r: (jnp.minimum((b * n_r + r + 1) * per, n_halo_blocks - 1), 0))
    return pl.pallas_call(
        body, name="mix_bwd_conv", grid=(n_seq, n_r),
        in_specs=[blk(0), right, blk(0), blk(1), left(0), left(1),
                  pl.BlockSpec((CONV_WIDTH, CA), lambda b, r: (0, 0))],
        out_specs=[pl.BlockSpec((ts, 2 * CA), lambda b, r: (b * n_r + r, 0)),
                   pl.BlockSpec((CONV_WIDTH, CA), lambda b, r: (0, 0)), pl.BlockSpec((1, CA), lambda b, r: (0, 0))],
        out_shape=[jax.ShapeDtypeStruct((T, 2 * CA), BF16), jax.ShapeDtypeStruct((CONV_WIDTH, CA), F32),
                   jax.ShapeDtypeStruct((1, CA), F32)],
        scratch_shapes=[pltpu.VMEM((SUBLANES, HALO + ts, CA), F32), pltpu.VMEM((SUBLANES, ts + HALO, CA), F32),
                        pltpu.VMEM((ts, CA), F32), pltpu.VMEM((CONV_WIDTH, 8, CA), F32)],
        compiler_params=_cp("arbitrary", "arbitrary"),
    )(da1, da1, z, z, z, z, conv_w)


def _row_tile(R, want):
    t = min(R, want)
    t -= t % 8
    while t > 8 and R % t:
        t -= 8
    return t if t >= 8 and R % t == 0 else R


def _adam_step(w, g, m, v):
    nm = ADAM_B1 * m + (1.0 - ADAM_B1) * g
    nv = ADAM_B2 * v + (1.0 - ADAM_B2) * (g * g)
    m_hat = nm / (1.0 - ADAM_B1 ** ADAM_STEP)
    v_hat = nv / (1.0 - ADAM_B2 ** ADAM_STEP)
    return -ADAM_LR * (m_hat / (jnp.sqrt(v_hat) + ADAM_EPS) + ADAM_WD * w), nm, nv


def _adamw_parts(parts, w, m, v, name):
    L, n, C = w.shape
    assert len(parts) == L
    tr = _row_tile(n, 192)
    n_i = n // tr

    def body(*refs):
        p_refs = refs[:L]
        w_ref, m_ref, v_ref, g_ref, d_ref, nm_ref, nv_ref = refs[L:]
        for k in range(L):
            @pl.when(pl.program_id(0) == k)
            def _(k=k):
                acc = p_refs[k][0]
                for s in range(1, N_DEV):
                    acc = acc + p_refs[k][s]
                g_ref[...] = acc

        d_ref[...], nm_ref[...], nv_ref[...] = _adam_step(w_ref[...], g_ref[...], m_ref[...], v_ref[...])

    def part_spec(k):
        return pl.BlockSpec((N_DEV, tr, C),
                            lambda l, i: (0, jnp.where(l == k, i, jnp.where(l < k, 0, n_i - 1)), 0))

    blk = pl.BlockSpec((None, tr, C), lambda l, i: (l, i, 0))
    return pl.pallas_call(
        body, name=name, grid=(L, n_i),
        in_specs=[part_spec(k) for k in range(L)] + [blk] * 3, out_specs=[blk] * 4,
        out_shape=[jax.ShapeDtypeStruct((L, n, C), F32)] * 4,
        compiler_params=_cp("arbitrary", "arbitrary"),
    )(*parts, w, m, v)


def _adamw(w, g, m, v, name):
    R, C = w.shape
    tr = _row_tile(R, 256)

    def body(w_ref, g_ref, m_ref, v_ref, d_ref, nm_ref, nv_ref):
        d_ref[...], nm_ref[...], nv_ref[...] = _adam_step(w_ref[...], g_ref[...], m_ref[...], v_ref[...])

    blk = pl.BlockSpec((tr, C), lambda i: (i, 0))
    return pl.pallas_call(
        body, name=name, grid=(R // tr,),
        in_specs=[blk] * 4, out_specs=[blk] * 3,
        out_shape=[jax.ShapeDtypeStruct((R, C), F32)] * 3,
        compiler_params=_cp("parallel"),
    )(w, g, m, v)


def _me():
    return lax.axis_index("x"), lax.axis_index("y"), lax.axis_index("c")


def _block_rows(ref, dev, n):
    start = (4 * dev[0] + 2 * dev[1] + dev[2]) * n
    if len(ref.shape) == 2:
        return ref.at[pl.ds(start, n), :]
    return ref.at[:, pl.ds(start, n), :]


def _all_gather(shards):
    na = len(shards)
    ns = [s.shape[-2] for s in shards]

    def body(*refs):
        ins, outs = refs[:na], refs[na:2 * na]
        send_sems, recv_sems, local_sems = refs[2 * na:]
        x, y, c = _me()
        me, sibling = (x, y, c), (x, y, 1 - c)
        chips = [(1 - x, y), (x, 1 - y), (1 - x, 1 - y)]

        def copy(a, k, block, to, src=None):
            dst = _block_rows(outs[a], block, ns[a])
            return pltpu.make_async_remote_copy(
                src_ref=dst if src is None else src, dst_ref=dst,
                send_sem=send_sems.at[a, k], recv_sem=recv_sems.at[a, k], device_id=to, device_id_type=MESH)

        mine = [pltpu.make_async_copy(ins[a], _block_rows(outs[a], me, ns[a]), local_sems.at[a]) for a in range(na)]
        for cp in mine:
            cp.start()
        first = []
        for a in range(na):
            first.append(copy(a, 0, me, sibling, src=ins[a]))
            first += [copy(a, 1 + j, me, (*chip, c), src=ins[a]) for j, chip in enumerate(chips)]
        for cp in first:
            cp.start()
        passed = []
        for j, chip in enumerate(chips):
            for a in range(na):
                copy(a, 1 + j, (*chip, c), me).wait_recv()
                fwd = copy(a, 4 + j, (*chip, c), sibling)
                fwd.start()
                passed.append(fwd)
        for a in range(na):
            copy(a, 0, sibling, me).wait_recv()
            for j, chip in enumerate(chips):
                copy(a, 4 + j, (*chip, 1 - c), me).wait_recv()
        for cp in first + passed:
            cp.wait_send()
        for cp in mine:
            cp.wait()

    out_shape = [jax.ShapeDtypeStruct(s.shape[:-2] + (N_DEV * s.shape[-2], s.shape[-1]), s.dtype) for s in shards]
    return pl.pallas_call(
        body, name="weights_all_gather",
        in_specs=[ANY] * na, out_specs=[ANY] * na, out_shape=out_shape,
        scratch_shapes=[pltpu.SemaphoreType.DMA((na, 7)), pltpu.SemaphoreType.DMA((na, 7)),
                        pltpu.SemaphoreType.DMA((na,))],
    )(*shards)


def _split_copies(gather, srcs, lands, send_sems, recv_sems, ns):
    x, y, c = _me()
    me = (x, y, c)
    my_slot = 4 * x + 2 * y + c
    copies = []
    for mask in range(1, N_DEV):
        peer = (x ^ (mask >> 2), y ^ ((mask >> 1) & 1), c ^ (mask & 1))
        for a in range(len(srcs)):
            if gather:
                src, dst = srcs[a], _block_rows(lands[a], me, ns[a])
            else:
                src, dst = _block_rows(srcs[a], peer, ns[a]), lands[a].at[my_slot]
            sem = a * (N_DEV - 1) + mask - 1
            copies.append(pltpu.make_async_remote_copy(
                src_ref=src, dst_ref=dst, send_sem=send_sems.at[sem], recv_sem=recv_sems.at[sem],
                device_id=peer, device_id_type=MESH))
    return copies


HBM_SPEC = pl.BlockSpec(memory_space=pltpu.HBM)
SEM_SPEC = pl.BlockSpec(memory_space=pltpu.SEMAPHORE)


def _split_start(gather, srcs, name, dep=None):
    na = len(srcs)
    x, y, c = _me()
    mine = 4 * x + 2 * y + c
    if gather:
        ns = [s.shape[-2] for s in srcs]
        lands = [lax.dynamic_update_slice(
            lax.empty(s.shape[:-2] + (N_DEV * s.shape[-2], s.shape[-1]), s.dtype), s,
            (0,) * (s.ndim - 2) + (mine * s.shape[-2], 0)) for s in srcs]
    else:
        ns = [s.shape[-2] // N_DEV for s in srcs]
        lands = [lax.dynamic_update_slice(
            lax.empty((N_DEV, n, s.shape[-1]), s.dtype),
            lax.dynamic_slice(s, (mine * n, 0), (n, s.shape[-1]))[None], (mine, 0, 0)) for s, n in zip(srcs, ns)]
    n_in = 2 * na + (dep is not None)

    def body(*refs):
        send_sems, recv_sems = refs[n_in], refs[n_in + 1]
        for cp in _split_copies(gather, refs[:na], refs[na:2 * na], send_sems, recv_sems, ns):
            cp.start()
        refs[-1][...] = jnp.zeros_like(refs[-1])

    hbm = lambda a: pltpu.with_memory_space_constraint(a, pltpu.HBM)
    args = [hbm(a) for a in srcs] + [hbm(a) for a in lands] + ([dep] if dep is not None else [])
    out = pl.pallas_call(
        body, name=name,
        in_specs=[HBM_SPEC] * (2 * na) + ([ANY] if dep is not None else []),
        out_specs=[SEM_SPEC, SEM_SPEC] + [HBM_SPEC] * (2 * na) + [pl.BlockSpec(memory_space=pltpu.VMEM)],
        out_shape=[pltpu.SemaphoreType.DMA((na * (N_DEV - 1),)), pltpu.SemaphoreType.DMA((na * (N_DEV - 1),))]
        + [pltpu.HBM(a.shape, a.dtype) for a in srcs + lands] + [jax.ShapeDtypeStruct((8, LANES), F32)],
        input_output_aliases={i: 2 + i for i in range(2 * na)},
        compiler_params=pltpu.CompilerParams(has_side_effects=pltpu.SideEffectType.DATAFLOW_SIDE_EFFECTING),
    )(*args)
    return (gather, ns, out[0], out[1], list(out[2:2 + na]), list(out[2 + na:2 + 2 * na])), out[-1]


def _split_wait(handle, after, name):
    gather, ns, send, recv, srcs, lands = handle
    na = len(srcs)

    def body(*refs):
        send_sems, recv_sems = refs[2 * na], refs[2 * na + 1]
        for cp in _split_copies(gather, refs[:na], refs[na:2 * na], send_sems, recv_sems, ns):
            cp.wait_send()
            cp.wait_recv()

    out = pl.pallas_call(
        body, name=name,
        in_specs=[HBM_SPEC] * (2 * na) + [SEM_SPEC, SEM_SPEC, ANY],
        out_specs=[HBM_SPEC] * (2 * na),
        out_shape=[pltpu.HBM(a.shape, a.dtype) for a in srcs + lands],
        input_output_aliases={i: i for i in range(2 * na)},
        compiler_params=pltpu.CompilerParams(has_side_effects=pltpu.SideEffectType.DATAFLOW_SIDE_EFFECTING),
    )(*srcs, *lands, send, recv, after)
    return list(out[:na]), list(out[na:])


def _small_all_reduce(buf):
    R = buf.shape[0]

    def body(b_ref, o_ref, recv_ref, send_sems, recv_sems):
        x, y, c = _me()
        my_slot = 4 * x + 2 * y + c
        recv_ref[my_slot] = b_ref[...]
        copies = []
        for mask in range(1, N_DEV):
            peer = (x ^ (mask >> 2), y ^ ((mask >> 1) & 1), c ^ (mask & 1))
            copies.append(pltpu.make_async_remote_copy(
                src_ref=b_ref, dst_ref=recv_ref.at[my_slot],
                send_sem=send_sems.at[mask - 1], recv_sem=recv_sems.at[mask - 1],
                device_id=peer, device_id_type=MESH))
        for cp in copies:
            cp.start()
        for cp in copies:
            cp.wait()
        acc = recv_ref[0]
        for k in range(1, N_DEV):
            acc = acc + recv_ref[k]
        o_ref[...] = acc

    return pl.pallas_call(
        body, name="small_all_reduce",
        in_specs=[pl.BlockSpec(memory_space=pltpu.VMEM)], out_specs=pl.BlockSpec(memory_space=pltpu.VMEM),
        out_shape=jax.ShapeDtypeStruct((R, LANES), F32),
        scratch_shapes=[pltpu.VMEM((N_DEV, R, LANES), F32), pltpu.SemaphoreType.DMA((7,)),
                        pltpu.SemaphoreType.DMA((7,))],
        compiler_params=pltpu.CompilerParams(vmem_limit_bytes=VMEM_LIMIT),
    )(buf)


def _pack(arrays):
    flat = jnp.concatenate([a.reshape(-1) for a in arrays])
    pad = (-flat.shape[0]) % (8 * LANES)
    return jnp.pad(flat, (0, pad)).reshape(-1, LANES)


def _unpack(buf, shapes):
    flat = buf.reshape(-1)
    out, off = [], 0
    for s in shapes:
        n = 1
        for d in s:
            n *= d
        out.append(flat[off:off + n].reshape(s))
        off += n
    return out


def _ffn_index(layer, second):
    return (2 * layer + second) * 3


def kernel(x, g_ffn1, w_ffn1_gate, w_ffn1_up, w_ffn1_down, g_mix, w_in_ab, conv_w, conv_b, ln_a_g, ln_a_b, ln_v_g, ln_v_b, sp_w, sp_b, w_out_ab, w_qkv, w_o, g_ffn2, w_ffn2_gate, w_ffn2_up, w_ffn2_down, g_final, loss_target, m_g_ffn1, m_w_ffn1_gate, m_w_ffn1_up, m_w_ffn1_down, m_g_mix, m_w_in_ab, m_conv_w, m_conv_b, m_ln_a_g, m_ln_a_b, m_ln_v_g, m_ln_v_b, m_sp_w, m_sp_b, m_w_out_ab, m_w_qkv, m_w_o, m_g_ffn2, m_w_ffn2_gate, m_w_ffn2_up, m_w_ffn2_down, m_g_final, v_g_ffn1, v_w_ffn1_gate, v_w_ffn1_up, v_w_ffn1_down, v_g_mix, v_w_in_ab, v_conv_w, v_conv_b, v_ln_a_g, v_ln_a_b, v_ln_v_g, v_ln_v_b, v_sp_w, v_sp_b, v_w_out_ab, v_w_qkv, v_w_o, v_g_ffn2, v_w_ffn2_gate, v_w_ffn2_up, v_w_ffn2_down, v_g_final):
    n_seq, S, D = x.shape
    T = n_seq * S
    depth = g_ffn1.shape[0]
    assert depth == 2 and D == D_MODEL
    my_block = 4 * lax.axis_index("x") + 2 * lax.axis_index("y") + lax.axis_index("c")

    ffn_parts = []
    for l in range(depth):
        for gate, up, down in ((w_ffn1_gate, w_ffn1_up, w_ffn1_down), (w_ffn2_gate, w_ffn2_up, w_ffn2_down)):
            ffn_parts += [gate[l].T, up[l].T, down[l]]
    ffn_shard = lambda k: jnp.stack(ffn_parts[3 * k:3 * k + 3]).astype(BF16)
    conv_w_pad = jnp.zeros((HALO, conv_w.shape[2]), F32).at[:CONV_WIDTH].set(conv_w[0]).T
    w_ffn = [None] * (2 * depth)
    w_ffn[0], conv_w_t = _all_gather([ffn_shard(0), conv_w_pad])
    conv_w_full = conv_w_t.T[:CONV_WIDTH]
    shards_b = [w_out_ab[0].astype(BF16), ffn_shard(1)]
    shards_d = [w_qkv[0].T.astype(BF16), w_o[0].astype(BF16), ffn_shard(3)]
    gather_a, token = _split_start(True, [w_in_ab[0].T.astype(BF16)], "gather_a_start", dep=conv_w_t)
    gather_b, token = _split_start(True, shards_b, "gather_b_start", dep=token)
    gather_c, token = _split_start(True, [ffn_shard(2)], "gather_c_start", dep=token)
    gather_d, token = _split_start(True, shards_d, "gather_d_start", dep=token)

    def gathered(handle, after, name):
        return _split_wait(handle, after, name)[1]

    row = lambda a: a.reshape(1, -1)
    tril = jnp.tril(jnp.ones((CHUNK, CHUNK), dtype=bool))
    ws = jnp.where(tril[None], sp_w[0], 0.0).astype(BF16)
    ws_t = jnp.swapaxes(ws, 1, 2)
    bias2d = jnp.repeat(sp_b[0].T, DB, axis=1)
    conv_b2, lag, lab = row(conv_b[0]), row(ln_a_g[0]), row(ln_a_b[0])
    lvg, lvb = row(ln_v_g[0]), row(ln_v_b[0])

    x0 = x.reshape(T, D)
    target = loss_target.reshape(T, D)
    saved = []
    xc = x0
    for l in range(depth):
        xa, a1, b1, h1 = _ffn_fwd(xc, row(g_ffn1[l]), w_ffn[2 * l], 0, f"ffn1_fwd_{l}", dep=token)
        if l % 2 == 0:
            w_in_t, = gathered(gather_a, xa, "gather_a_wait")
            z, hm = _mm_nt(xa, w_in_t, row(g_mix[l]), F32, "mix_in_proj")
            cat, conv_out = _mix_fwd(z, conv_w_full, conv_b2, lag, lab, lvg, lvb, ws, bias2d, n_seq)
            w_out, w_ffn[1] = gathered(gather_b, cat, "gather_b_wait")
            xb = _mm_nn_res(cat, w_out, xa, "mix_out_proj")
            mixer = (z, hm, cat, conv_out)
        else:
            w_qkv_t, w_o_full, w_ffn[3] = gathered(gather_d, xa, "gather_d_wait")
            qkv, hm = _mm_nt(xa, w_qkv_t, row(g_mix[l]), BF16, "qkv_proj")
            o, tot = _attn_fwd(qkv, n_seq)
            xb = _mm_nn_res(o, w_o_full, xa, "attn_out_proj")
            mixer = (qkv, hm, o, tot)
        xn, a2, b2, h2 = _ffn_fwd(xb, row(g_ffn2[l]), w_ffn[2 * l + 1], 0, f"ffn2_fwd_{l}")
        saved.append((xc, a1, b1, h1, xa, mixer, xb, a2, b2, h2))
        xc = xn
        if l == 0:
            w_ffn[2], = gathered(gather_c, xc, "gather_c_wait")

    g, loss_part, dg_final = _loss_head(xc, row(g_final), target)

    dg_ffn1, dg_ffn2, dg_mix = [None] * depth, [None] * depth, [None] * depth
    exchanges = {}
    token = None

    def ffn_back(g, xin, gvec, a, b, h, k, tag, token):
        g, dg, da, db, s, gh = _ffn_bwd(g, xin, gvec, a, b, w_ffn[k], 0, f"ffn{tag}_bwd", dep=token)
        if k == 0:
            return g, dg, (da, db, s, gh, h)
        dws = [_mm_tn(da, h, f"dw_gate{tag}"), _mm_tn(db, h, f"dw_up{tag}"), _mm_tn(s, gh, f"dw_down{tag}")]
        exchanges[f"ffn{k}"], token = _split_start(False, dws, f"exchange_ffn{tag}_start")
        return g, dg, token

    for l in reversed(range(depth)):
        xin, a1, b1, h1, xa, mixer, xb, a2, b2, h2 = saved[l]
        g, dg_ffn2[l], token = ffn_back(g, xb, row(g_ffn2[l]), a2, b2, h2, 2 * l + 1, f"2_{l}", token)
        if l % 2 == 0:
            z, hm, cat, conv_out = mixer
            dcat = _mm_nt(g, w_out, None, F32, "mix_out_bwd", dep=token)
            d_w_out = _mm_tn(cat, g, "dw_out")
            (da1, dz_uv, d_lag, d_lab, d_lvg, d_lvb, d_ws, d_sb) = _mix_bwd_rows(
                dcat, z, conv_out, lag, lab, lvg, lvb, ws, ws_t, bias2d, n_seq)
            dz_a, d_cw, d_cb = _mix_bwd_conv(da1, z, conv_w_full, n_seq)
            d_w_in_t = jnp.concatenate([_mm_tn(dz_a, hm, "dw_in_a"), _mm_tn(dz_uv, hm, "dw_in_uv")])
            exchanges["mix"], token = _split_start(False, [d_w_out, d_w_in_t], "exchange_mix_start")
            g, dg_mix[l] = _mm_nn_rmsbwd([dz_a, dz_uv], w_in_t, xa, row(g_mix[l]), g, "mix_in_bwd", dep=token)
        else:
            qkv, hm, o, tot = mixer
            do = _mm_nt(g, w_o_full, None, BF16, "attn_out_bwd", dep=token)
            d_w_o = _mm_tn(o, g, "dw_o")
            dq, dk, dv = _attn_bwd(qkv, do, tot, n_seq)
            d_w_qkv_t = jnp.concatenate([_mm_tn(dq, hm, "dw_q"), _mm_tn(dk, hm, "dw_k"), _mm_tn(dv, hm, "dw_v")])
            exchanges["attn"], token = _split_start(False, [d_w_o, d_w_qkv_t], "exchange_attn_start")
            g, dg_mix[l] = _mm_nn_rmsbwd([dq, dk, dv], w_qkv_t, xa, row(g_mix[l]), g, "qkv_bwd", dep=token)
        g, dg_ffn1[l], token = ffn_back(g, xin, row(g_ffn1[l]), a1, b1, h1, 2 * l, f"1_{l}", token)
    grad_x = g.reshape(n_seq, S, D)

    small = [jnp.concatenate(dg_ffn1), jnp.concatenate(dg_mix), d_cw, d_cb, d_lag, d_lab, d_lvg, d_lvb,
             jnp.where(tril[None], d_ws, 0.0), d_sb.T, jnp.concatenate(dg_ffn2), dg_final, loss_part[:, :1]]
    small_shapes = [(depth, D), (depth, D), (CONV_WIDTH, CA), (1, CA), (1, CA), (1, CA), (1, GB, DB), (1, GB, DB),
                    (1, GB, CHUNK, CHUNK), (1, GB, CHUNK), (depth, D), (D,), ()]
    small_sum = _small_all_reduce(_pack(small))
    red = _unpack(small_sum, small_shapes)
    (gr_g_ffn1, gr_g_mix, gr_cw_full, gr_cb, gr_lag, gr_lab, gr_lvg, gr_lvb, gr_sp_w, gr_sp_b,
     gr_g_ffn2, gr_g_final, loss) = red
    n_cw = conv_w.shape[2]
    gr_cw = lax.dynamic_slice(gr_cw_full, (0, my_block * n_cw), (CONV_WIDTH, n_cw))[None]

    da, db, s, gh, h = token
    token = small_sum
    for which, lhs, rhs in ((2, s, gh), (1, db, h), (0, da, h)):
        dw = _mm_tn(lhs, rhs, f"dw_ffn0_{which}", dep=token)
        exchanges[f"ffn0_{which}"], token = _split_start(False, [dw], f"exchange_ffn0_{which}_start")

    def landed(key, after):
        return _split_wait(exchanges[key], after, f"exchange_{key}_wait")[1]

    parts_ffn = [None] * (6 * depth)
    for k in range(1, 2 * depth):
        parts_ffn[3 * k:3 * k + 3] = landed(f"ffn{k}", token)
    parts_out, parts_in = landed("mix", token)
    parts_o, parts_qkv = landed("attn", token)

    grads = {
        "g_ffn1": gr_g_ffn1, "g_mix": gr_g_mix, "conv_w": gr_cw, "conv_b": gr_cb, "ln_a_g": gr_lag,
        "ln_a_b": gr_lab, "ln_v_g": gr_lvg, "ln_v_b": gr_lvb, "sp_w": gr_sp_w, "sp_b": gr_sp_b,
        "g_ffn2": gr_g_ffn2, "g_final": gr_g_final,
    }
    weights = dict(g_ffn1=g_ffn1, w_ffn1_gate=w_ffn1_gate, w_ffn1_up=w_ffn1_up, w_ffn1_down=w_ffn1_down, g_mix=g_mix,
                   w_in_ab=w_in_ab, conv_w=conv_w, conv_b=conv_b, ln_a_g=ln_a_g, ln_a_b=ln_a_b, ln_v_g=ln_v_g,
                   ln_v_b=ln_v_b, sp_w=sp_w, sp_b=sp_b, w_out_ab=w_out_ab, w_qkv=w_qkv, w_o=w_o, g_ffn2=g_ffn2,
                   w_ffn2_gate=w_ffn2_gate, w_ffn2_up=w_ffn2_up, w_ffn2_down=w_ffn2_down, g_final=g_final)
    m_in = dict(g_ffn1=m_g_ffn1, w_ffn1_gate=m_w_ffn1_gate, w_ffn1_up=m_w_ffn1_up, w_ffn1_down=m_w_ffn1_down,
                g_mix=m_g_mix, w_in_ab=m_w_in_ab, conv_w=m_conv_w, conv_b=m_conv_b, ln_a_g=m_ln_a_g, ln_a_b=m_ln_a_b,
                ln_v_g=m_ln_v_g, ln_v_b=m_ln_v_b, sp_w=m_sp_w, sp_b=m_sp_b, w_out_ab=m_w_out_ab, w_qkv=m_w_qkv,
                w_o=m_w_o, g_ffn2=m_g_ffn2, w_ffn2_gate=m_w_ffn2_gate, w_ffn2_up=m_w_ffn2_up,
                w_ffn2_down=m_w_ffn2_down, g_final=m_g_final)
    v_in = dict(g_ffn1=v_g_ffn1, w_ffn1_gate=v_w_ffn1_gate, w_ffn1_up=v_w_ffn1_up, w_ffn1_down=v_w_ffn1_down,
                g_mix=v_g_mix, w_in_ab=v_w_in_ab, conv_w=v_conv_w, conv_b=v_conv_b, ln_a_g=v_ln_a_g, ln_a_b=v_ln_a_b,
                ln_v_g=v_ln_v_g, ln_v_b=v_ln_v_b, sp_w=v_sp_w, sp_b=v_sp_b, w_out_ab=v_w_out_ab, w_qkv=v_w_qkv,
                w_o=v_w_o, g_ffn2=v_g_ffn2, w_ffn2_gate=v_w_ffn2_gate, w_ffn2_up=v_w_ffn2_up,
                w_ffn2_down=v_w_ffn2_down, g_final=v_g_final)
    names = list(weights)
    grads = {n: grads[n].reshape(weights[n].shape) for n in grads}

    delta, new_m, new_v = {}, {}, {}

    def adamw_big(n, parts):
        if weights[n].shape[-1] == D:
            view = back = lambda a: a
        else:
            view = back = lambda a: jnp.swapaxes(a, 1, 2)
        out = _adamw_parts(parts, view(weights[n]), view(m_in[n]), view(v_in[n]), f"adamw_{n}")
        grads[n], delta[n], new_m[n], new_v[n] = [back(a) for a in out]

    adamw_big("w_in_ab", [parts_in])
    adamw_big("w_out_ab", [parts_out])
    adamw_big("w_qkv", [parts_qkv])
    adamw_big("w_o", [parts_o])
    kinds = ("gate", "up", "down")
    for which, kind in enumerate(kinds):
        adamw_big(f"w_ffn2_{kind}", [parts_ffn[_ffn_index(l, 1) + which] for l in range(depth)])
    big = [n for n in names if n.startswith("w_")]
    after = jnp.concatenate([delta[n].reshape(-1)[:1] for n in big if n in delta]).reshape(1, -1)
    for which in (2, 1, 0):
        parts_ffn[which], = landed(f"ffn0_{which}", after)
    for which, kind in enumerate(kinds):
        adamw_big(f"w_ffn1_{kind}", [parts_ffn[_ffn_index(l, 0) + which] for l in range(depth)])
    little = [n for n in names if n not in big]
    shapes = [weights[n].shape for n in little]
    d, nm, nv = _adamw(_pack([weights[n] for n in little]), _pack([grads[n] for n in little]),
                       _pack([m_in[n] for n in little]), _pack([v_in[n] for n in little]), "adamw_small")
    for n, dd, mm, vv in zip(little, _unpack(d, shapes), _unpack(nm, shapes), _unpack(nv, shapes)):
        delta[n], new_m[n], new_v[n] = dd, mm, vv

    return (loss, grad_x, *[grads[n] for n in names], *[delta[n] for n in names],
            *[new_m[n] for n in names], *[new_v[n] for n in names])
```

```python
import functools

import jax
import jax.numpy as jnp
from jax import lax
from jax.experimental import pallas as pl
from jax.experimental.pallas import tpu as pltpu

F32 = jnp.float32
BF16 = jnp.bfloat16

D_MODEL = 1024
CA = 512
CB = 512
GB = 4
DB = 128
CHUNK = 128
CONV_WIDTH = 31
N_HEADS = 16
HEAD_DIM = 64
EPS = 1e-6
N_DEV = 8
LANES = 128
SUBLANES = 8
QB = 128
ATT_TQ = 1024
FFN_TN = 1408
HALO = 32
CONV_ROWS = 32
ATT_SCALE = HEAD_DIM ** -0.5

ADAM_LR = 0.001
ADAM_B1 = 0.9
ADAM_B2 = 0.999
ADAM_EPS = 1e-08
ADAM_WD = 0.01
ADAM_STEP = 10

NT = (((1,), (1,)), ((), ()))
NN = (((1,), (0,)), ((), ()))
TN = (((0,), (0,)), ((), ()))
MESH = pl.DeviceIdType.MESH
ANY = pl.BlockSpec(memory_space=pl.ANY)
VMEM_LIMIT = 60 * 1024 * 1024


def _dot(a, b, dims):
    return lax.dot_general(a, b, dims, preferred_element_type=F32)


def _cp(*sem):
    return pltpu.CompilerParams(dimension_semantics=sem, vmem_limit_bytes=VMEM_LIMIT)


def _pcall(body, *, in_specs, args, dep=None, **kw):
    if dep is not None:
        n_in = len(in_specs)
        inner = body

        def body(*refs):
            inner(*refs[:n_in], *refs[n_in + 1:])

        in_specs = list(in_specs) + [ANY]
        args = tuple(args) + (dep,)
    return pl.pallas_call(body, in_specs=list(in_specs), **kw)(*args)


def _tile(n, want):
    if n <= want:
        return n
    t = want - want % LANES
    while t > LANES and n % t:
        t -= LANES
    assert n % t == 0, (n, want)
    return t


def _sigmoid(x):
    return 0.5 * jnp.tanh(0.5 * x) + 0.5


def _rstd(x):
    return lax.rsqrt(jnp.mean(x * x, axis=-1, keepdims=True) + EPS)


def _rms_bwd(x, g, dh):
    r = _rstd(x)
    u = dh * g
    dx = r * (u - x * (r * r) * jnp.mean(u * x, axis=-1, keepdims=True))
    dg = jnp.sum(dh * x * r, axis=0, keepdims=True)
    return dx, dg


def _ln_fwd(x, g, b):
    mu = jnp.mean(x, axis=-1, keepdims=True)
    xc = x - mu
    r = lax.rsqrt(jnp.mean(xc * xc, axis=-1, keepdims=True) + EPS)
    xh = xc * r
    return xh * g + b, xh, r


def _ln_bwd(dy, xh, r, g):
    dxh = dy * g
    return r * (dxh - jnp.mean(dxh, axis=-1, keepdims=True)
                - xh * jnp.mean(dxh * xh, axis=-1, keepdims=True))


def _ffn_fwd(x, g, wall, base, name, dep=None):
    T, D = x.shape
    F = wall.shape[1]
    tm, tn = _tile(T, 512), _tile(F, FFN_TN)
    n_j = F // tn

    def body(x_ref, g_ref, wg_ref, wu_ref, wd_ref, xo_ref, a_ref, b_ref, h_ref, acc_ref):
        j = pl.program_id(1)

        @pl.when(j == 0)
        def _():
            xv = x_ref[...]
            h_ref[...] = (xv * _rstd(xv) * g_ref[...]).astype(BF16)
            acc_ref[...] = jnp.zeros_like(acc_ref)

        h = h_ref[...]
        a = _dot(h, wg_ref[...], NT)
        b = _dot(h, wu_ref[...], NT)
        a_ref[...] = a.astype(BF16)
        b_ref[...] = b.astype(BF16)
        s = (a * _sigmoid(a) * b).astype(BF16)
        acc_ref[...] += _dot(s, wd_ref[...], NN)

        @pl.when(j == n_j - 1)
        def _():
            xo_ref[...] = x_ref[...] + 0.5 * acc_ref[...]

    wspec = lambda k: pl.BlockSpec((None, tn, D), lambda i, j: (base + k, j, 0))
    return _pcall(
        body, name=name, grid=(T // tm, n_j), dep=dep, args=(x, g, wall, wall, wall),
        in_specs=[pl.BlockSpec((tm, D), lambda i, j: (i, 0)), pl.BlockSpec((1, D), lambda i, j: (0, 0)),
                  wspec(0), wspec(1), wspec(2)],
        out_specs=[pl.BlockSpec((tm, D), lambda i, j: (i, 0)), pl.BlockSpec((tm, tn), lambda i, j: (i, j)),
                   pl.BlockSpec((tm, tn), lambda i, j: (i, j)), pl.BlockSpec((tm, D), lambda i, j: (i, 0))],
        out_shape=[jax.ShapeDtypeStruct((T, D), F32), jax.ShapeDtypeStruct((T, F), BF16),
                   jax.ShapeDtypeStruct((T, F), BF16), jax.ShapeDtypeStruct((T, D), BF16)],
        scratch_shapes=[pltpu.VMEM((tm, D), F32)],
        compiler_params=_cp("parallel", "arbitrary"),
    )


def _ffn_bwd(go, x, g, a, b, wall, base, name, dep=None):
    T, D = x.shape
    F = wall.shape[1]
    tm, tn = _tile(T, 512), _tile(F, FFN_TN)
    n_j = F // tn

    def body(go_ref, x_ref, g_ref, a_ref, b_ref, wg_ref, wu_ref, wd_ref,
             gx_ref, dg_ref, da_ref, db_ref, s_ref, gh_ref, acc_ref):
        i, j = pl.program_id(0), pl.program_id(1)

        @pl.when(j == 0)
        def _():
            gh_ref[...] = (0.5 * go_ref[...]).astype(BF16)
            acc_ref[...] = jnp.zeros_like(acc_ref)

        @pl.when((i == 0) & (j == 0))
        def _():
            dg_ref[...] = jnp.zeros_like(dg_ref)

        ds = _dot(gh_ref[...], wd_ref[...], NT)
        av = a_ref[...].astype(F32)
        bv = b_ref[...].astype(F32)
        sig = _sigmoid(av)
        sl = av * sig
        dab = ((ds * bv) * (sig + sl * (1.0 - sig))).astype(BF16)
        dbb = (ds * sl).astype(BF16)
        s_ref[...] = (sl * bv).astype(BF16)
        da_ref[...] = dab
        db_ref[...] = dbb
        acc_ref[...] += _dot(dab, wg_ref[...], NN) + _dot(dbb, wu_ref[...], NN)

        @pl.when(j == n_j - 1)
        def _():
            dx, dg = _rms_bwd(x_ref[...], g_ref[...], acc_ref[...])
            gx_ref[...] = go_ref[...] + dx
            dg_ref[...] += dg

    wspec = lambda k: pl.BlockSpec((None, tn, D), lambda i, j: (base + k, j, 0))
    row = pl.BlockSpec((tm, D), lambda i, j: (i, 0))
    hid = pl.BlockSpec((tm, tn), lambda i, j: (i, j))
    vec = pl.BlockSpec((1, D), lambda i, j: (0, 0))
    return _pcall(
        body, name=name, grid=(T // tm, n_j), dep=dep, args=(go, x, g, a, b, wall, wall, wall),
        in_specs=[row, row, vec, hid, hid, wspec(0), wspec(1), wspec(2)],
        out_specs=[row, vec, hid, hid, hid, row],
        out_shape=[jax.ShapeDtypeStruct((T, D), F32), jax.ShapeDtypeStruct((1, D), F32),
                   jax.ShapeDtypeStruct((T, F), BF16), jax.ShapeDtypeStruct((T, F), BF16),
                   jax.ShapeDtypeStruct((T, F), BF16), jax.ShapeDtypeStruct((T, D), BF16)],
        scratch_shapes=[pltpu.VMEM((tm, D), F32)],
        compiler_params=_cp("arbitrary", "arbitrary"),
    )


def _mm_tn(a, b, name, dep=None, out_dtype=F32):
    T, M = a.shape
    N = b.shape[1]
    tmm, tk = _tile(M, 1536), _tile(T, 2048)
    n_k = T // tk
    narrow = out_dtype != F32

    def body(a_ref, b_ref, o_ref, *scratch):
        acc_ref = scratch[0] if narrow else o_ref

        @pl.when(pl.program_id(1) == 0)
        def _():
            acc_ref[...] = jnp.zeros_like(acc_ref)

        acc_ref[...] += _dot(a_ref[...].astype(BF16), b_ref[...].astype(BF16), TN)
        if narrow:
            @pl.when(pl.program_id(1) == n_k - 1)
            def _():
                o_ref[...] = acc_ref[...].astype(out_dtype)

    return _pcall(
        body, name=name, grid=(M // tmm, n_k), dep=dep, args=(a, b),
        in_specs=[pl.BlockSpec((tk, tmm), lambda m, k: (k, m)), pl.BlockSpec((tk, N), lambda m, k: (k, 0))],
        out_specs=pl.BlockSpec((tmm, N), lambda m, k: (m, 0)),
        out_shape=jax.ShapeDtypeStruct((M, N), out_dtype),
        scratch_shapes=[pltpu.VMEM((tmm, N), F32)] if narrow else [],
        compiler_params=_cp("parallel", "arbitrary"),
    )


def _mm_nt(x, wt, g, out_dtype, name, dep=None):
    T, K = x.shape
    N = wt.shape[0]
    tm, tn = _tile(T, 512), N
    norm = g is not None

    def body(*refs):
        if norm:
            x_ref, g_ref, w_ref, o_ref, h_ref = refs
        else:
            x_ref, w_ref, o_ref, h_ref = refs

        @pl.when(pl.program_id(1) == 0)
        def _():
            xv = x_ref[...].astype(F32)
            if norm:
                xv = xv * _rstd(xv) * g_ref[...]
            h_ref[...] = xv.astype(BF16)

        o_ref[...] = _dot(h_ref[...], w_ref[...], NT).astype(out_dtype)

    row = pl.BlockSpec((tm, K), lambda i, j: (i, 0))
    wsp = pl.BlockSpec((tn, K), lambda i, j: (j, 0))
    osp = pl.BlockSpec((tm, tn), lambda i, j: (i, j))
    if norm:
        return pl.pallas_call(
            body, name=name, grid=(T // tm, N // tn),
            in_specs=[row, pl.BlockSpec((1, K), lambda i, j: (0, 0)), wsp],
            out_specs=[osp, row],
            out_shape=[jax.ShapeDtypeStruct((T, N), out_dtype), jax.ShapeDtypeStruct((T, K), BF16)],
            compiler_params=_cp("parallel", "arbitrary"),
        )(x, g, wt)
    return _pcall(
        body, name=name, grid=(T // tm, N // tn), dep=dep, args=(x, wt),
        in_specs=[row, wsp], out_specs=osp,
        out_shape=jax.ShapeDtypeStruct((T, N), out_dtype),
        scratch_shapes=[pltpu.VMEM((tm, K), BF16)],
        compiler_params=_cp("parallel", "arbitrary"),
    )


def _mm_nn_res(act, w, resid, name):
    T, K = act.shape
    D = w.shape[1]
    tm = _tile(T, 512)

    def body(a_ref, w_ref, r_ref, o_ref):
        o_ref[...] = r_ref[...] + _dot(a_ref[...].astype(BF16), w_ref[...], NN)

    return pl.pallas_call(
        body, name=name, grid=(T // tm,),
        in_specs=[pl.BlockSpec((tm, K), lambda i: (i, 0)), pl.BlockSpec((K, D), lambda i: (0, 0)),
                  pl.BlockSpec((tm, D), lambda i: (i, 0))],
        out_specs=pl.BlockSpec((tm, D), lambda i: (i, 0)),
        out_shape=jax.ShapeDtypeStruct((T, D), F32),
        compiler_params=_cp("parallel"),
    )(act, w, resid)


def _mm_nn_rmsbwd(acts, w, x, g, gprev, name, dep=None):
    T = acts[0].shape[0]
    ks = [a.shape[1] for a in acts]
    K, D = w.shape
    assert sum(ks) == K
    tm = _tile(T, 512)
    na = len(acts)

    def body(*refs):
        a_refs = refs[:na]
        w_ref, x_ref, g_ref, gp_ref, o_ref, dg_ref = refs[na:]

        @pl.when(pl.program_id(0) == 0)
        def _():
            dg_ref[...] = jnp.zeros_like(dg_ref)

        dh, off = None, 0
        for a_ref, k in zip(a_refs, ks):
            part = _dot(a_ref[...].astype(BF16), w_ref[off:off + k, :], NN)
            dh = part if dh is None else dh + part
            off += k
        dx, dg = _rms_bwd(x_ref[...], g_ref[...], dh)
        o_ref[...] = gp_ref[...] + dx
        dg_ref[...] += dg

    row = pl.BlockSpec((tm, D), lambda i: (i, 0))
    vec = pl.BlockSpec((1, D), lambda i: (0, 0))
    return _pcall(
        body, name=name, grid=(T // tm,), dep=dep, args=(*acts, w, x, g, gprev),
        in_specs=[pl.BlockSpec((tm, k), lambda i: (i, 0)) for k in ks]
        + [pl.BlockSpec((K, D), lambda i: (0, 0)), row, vec, row],
        out_specs=[row, vec],
        out_shape=[jax.ShapeDtypeStruct((T, D), F32), jax.ShapeDtypeStruct((1, D), F32)],
        compiler_params=_cp("arbitrary"),
    )


def _loss_head(x, g, target):
    T, D = x.shape
    tm = _tile(T, 512)

    def body(x_ref, g_ref, t_ref, dx_ref, loss_ref, dg_ref):
        @pl.when(pl.program_id(0) == 0)
        def _():
            loss_ref[...] = jnp.zeros_like(loss_ref)
            dg_ref[...] = jnp.zeros_like(dg_ref)

        xv = x_ref[...]
        gv = g_ref[...]
        e = xv * _rstd(xv) * gv - t_ref[...]
        per_tok = jnp.sum(e * e, axis=-1, keepdims=True) * (1.0 / D)
        loss_ref[...] += 0.5 * jnp.sum(per_tok, axis=0, keepdims=True)
        dx, dg = _rms_bwd(xv, gv, e * (1.0 / D))
        dx_ref[...] = dx
        dg_ref[...] += dg

    row = pl.BlockSpec((tm, D), lambda i: (i, 0))
    vec = pl.BlockSpec((1, D), lambda i: (0, 0))
    return pl.pallas_call(
        body, name="loss_head", grid=(T // tm,),
        in_specs=[row, vec, row],
        out_specs=[row, pl.BlockSpec((1, LANES), lambda i: (0, 0)), vec],
        out_shape=[jax.ShapeDtypeStruct((T, D), F32), jax.ShapeDtypeStruct((1, LANES), F32),
                   jax.ShapeDtypeStruct((1, D), F32)],
        compiler_params=_cp("arbitrary"),
    )(x, g, target)


def _log_gates(z):
    ls = jnp.minimum(z, 0.0) - jnp.log(1.0 + jnp.exp(-jnp.abs(z)))
    return ls, ls - z


def _cumsum_mm(v, u2):
    hi = v.astype(BF16)
    lo = (v - hi.astype(F32)).astype(BF16)
    return _dot(jnp.concatenate([hi, lo], axis=1), u2, NN)


def _half_rowsum(v):
    n = v.shape[0]
    s0 = jnp.sum(v[:, :QB], axis=1, keepdims=True)
    s1 = jnp.sum(v[:, QB:], axis=1, keepdims=True)
    return jnp.concatenate([jnp.broadcast_to(s0, (n, QB)), jnp.broadcast_to(s1, (n, QB))], axis=1)


def _stack_heads(src_ref, dst_ref, n_blk):
    m0 = lax.broadcasted_iota(jnp.int32, (1, LANES), 1) < HEAD_DIM

    def fill(c, carry):
        blk = src_ref[pl.ds(pl.multiple_of(c * QB, QB), QB), :]
        zero = jnp.zeros_like(blk)
        dst_ref[c, 0:QB, :] = jnp.where(m0, blk, zero)
        dst_ref[c, QB:2 * QB, :] = jnp.where(m0, zero, blk)
        return carry

    lax.fori_loop(0, n_blk, fill, 0)


def _diag_mask(tq, j):
    n = tq - j * QB
    row = lax.broadcasted_iota(jnp.int32, (n, 2 * QB), 0)
    col = lax.broadcasted_iota(jnp.int32, (n, 2 * QB), 1)
    return (col & (QB - 1)) < row


def _tri_blockdiag(upper):
    r = lax.broadcasted_iota(jnp.int32, (2 * QB, 2 * QB), 0)
    c = lax.broadcasted_iota(jnp.int32, (2 * QB, 2 * QB), 1)
    same = (r // QB) == (c // QB)
    u = (same & ((r > c) if upper else (r < c))).astype(BF16)
    return jnp.concatenate([u, u], axis=0)


def _attn_tiles(T, n_seq):
    S = T // n_seq
    tq = ATT_TQ if S % ATT_TQ == 0 else QB
    return S, tq, tq // QB, S // tq, S // QB


def _attn_fwd(qkv, n_seq):
    T = qkv.shape[0]
    S, tq, r, n_q, n_k = _attn_tiles(T, n_seq)
    n_p = D_MODEL // LANES
    u_suffix = _tri_blockdiag(True)

    def body(q_ref, k_ref, v_ref, u_ref, o_ref, tot_ref, kk_ref, vv_ref, lr_s, acc_s):
        qi = pl.program_id(2)

        @pl.when(qi == 0)
        def _():
            _stack_heads(k_ref, kk_ref, n_k)
            _stack_heads(v_ref, vv_ref, n_k)

        u = u_ref[...]
        lr_s[...] = jnp.zeros_like(lr_s)
        acc_s[...] = jnp.zeros_like(acc_s)

        def step(kj, q, mask, lr, acc):
            ls, lk = _log_gates(_dot(q, kk_ref[kj], NT))
            if mask is not None:
                lk = jnp.where(mask, lk, 0.0)
            a = jnp.exp(ls + _cumsum_mm(lk, u) + lr)
            if mask is not None:
                a = jnp.where(mask, a, 0.0)
            return lr + _half_rowsum(lk), acc + _dot(a.astype(BF16), vv_ref[kj], NN)

        for j in reversed(range(r)):
            rows = slice(j * QB, tq)
            lr, acc = step(qi * r + j, q_ref[rows, :] * ATT_SCALE, _diag_mask(tq, j), lr_s[rows, :], acc_s[rows, :])
            lr_s[rows, :] = lr
            acc_s[rows, :] = acc

        q = q_ref[...] * ATT_SCALE

        def off(it, carry):
            lr, acc = lr_s[...], acc_s[...]
            for j in range(r):
                lr, acc = step((qi - it) * r - 1 - j, q, None, lr, acc)
            lr_s[...] = lr
            acc_s[...] = acc
            return carry

        lax.fori_loop(0, qi, off, 0)
        o_ref[...] = acc_s[...].astype(BF16)
        tot_ref[...] = lr_s[...]

    return pl.pallas_call(
        body, name="attn_fwd", grid=(n_seq, n_p, n_q),
        in_specs=[pl.BlockSpec((tq, LANES), lambda b, p, qi: (b * n_q + qi, p)),
                  pl.BlockSpec((S, LANES), lambda b, p, qi: (b, n_p + p)),
                  pl.BlockSpec((S, LANES), lambda b, p, qi: (b, 2 * n_p + p)),
                  pl.BlockSpec((4 * QB, 2 * QB), lambda b, p, qi: (0, 0))],
        out_specs=[pl.BlockSpec((tq, LANES), lambda b, p, qi: (b * n_q + qi, p)),
                   pl.BlockSpec((tq, 2 * QB), lambda b, p, qi: (b * n_q + qi, p))],
        out_shape=[jax.ShapeDtypeStruct((T, D_MODEL), BF16), jax.ShapeDtypeStruct((T, 2 * D_MODEL), F32)],
        scratch_shapes=[pltpu.VMEM((n_k, 2 * QB, LANES), BF16), pltpu.VMEM((n_k, 2 * QB, LANES), BF16),
                        pltpu.VMEM((tq, 2 * QB), F32), pltpu.VMEM((tq, LANES), F32)],
        compiler_params=_cp("parallel", "parallel", "arbitrary"),
    )(qkv, qkv, qkv, u_suffix)


def _attn_bwd(qkv, do, tot, n_seq):
    T = qkv.shape[0]
    S, tq, r, n_q, n_k = _attn_tiles(T, n_seq)
    n_p = D_MODEL // LANES
    u_prefix = _tri_blockdiag(False)

    def body(q_ref, k_ref, v_ref, do_ref, tot_ref, u_ref, dq_ref, dk_out, dv_out,
             kk_ref, vv_ref, cl_s, cg_s, dq_s, dk_ref, dv_ref):
        qi = pl.program_id(2)

        @pl.when(qi == 0)
        def _():
            _stack_heads(k_ref, kk_ref, n_k)
            _stack_heads(v_ref, vv_ref, n_k)
            dk_ref[...] = jnp.zeros_like(dk_ref)
            dv_ref[...] = jnp.zeros_like(dv_ref)

        u = u_ref[...]
        m0 = lax.broadcasted_iota(jnp.int32, (1, LANES), 1) < HEAD_DIM
        cl_s[...] = tot_ref[...]
        cg_s[...] = jnp.zeros_like(cg_s)
        dq_s[...] = jnp.zeros_like(dq_s)

        def step(kj, q, dov, mask, rest, cg, dq):
            kk = kk_ref[kj]
            ls, lk = _log_gates(_dot(q, kk, NT))
            if mask is not None:
                lk = jnp.where(mask, lk, 0.0)
            a = jnp.exp(ls + (rest - (_cumsum_mm(lk, u) + lk)))
            if mask is not None:
                a = jnp.where(mask, a, 0.0)
            g = a * _dot(dov, vv_ref[kj], NT)
            dz = g - (g + _dot(g.astype(BF16), u[:2 * QB], NN) + cg) * jnp.exp(ls)
            if mask is not None:
                dz = jnp.where(mask, dz, 0.0)
            dz = dz.astype(BF16)
            rows = pl.ds(pl.multiple_of(kj * QB, QB), QB)
            dvt = _dot(a.astype(BF16), dov, TN)
            dv_ref[rows, :] += jnp.where(m0, dvt[:QB], dvt[QB:])
            dkt = _dot(dz, q, TN)
            dk_ref[rows, :] += jnp.where(m0, dkt[:QB], dkt[QB:])
            return rest - _half_rowsum(lk), cg + _half_rowsum(g), dq + _dot(dz, kk, NN)

        q = q_ref[...] * ATT_SCALE
        dov = do_ref[...]

        def off(it, carry):
            cl, cg, dq = cl_s[...], cg_s[...], dq_s[...]
            for j in range(r):
                cl, cg, dq = step(it * r + j, q, dov, None, cl, cg, dq)
            cl_s[...] = cl
            cg_s[...] = cg
            dq_s[...] = dq
            return carry

        lax.fori_loop(0, qi, off, 0)

        for j in range(r):
            rows = slice(j * QB, tq)
            cl, cg, dq = step(qi * r + j, q_ref[rows, :] * ATT_SCALE, do_ref[rows, :],
                              _diag_mask(tq, j), cl_s[rows, :], cg_s[rows, :], dq_s[rows, :])
            cl_s[rows, :] = cl
            cg_s[rows, :] = cg
            dq_s[rows, :] = dq
        dq_ref[...] = (dq_s[...] * ATT_SCALE).astype(BF16)

        @pl.when(qi == n_q - 1)
        def _():
            dk_out[...] = dk_ref[...].astype(BF16)
            dv_out[...] = dv_ref[...].astype(BF16)

    qspec = pl.BlockSpec((tq, LANES), lambda b, p, qi: (b * n_q + qi, p))
    seq = lambda off: pl.BlockSpec((S, LANES), lambda b, p, qi: (b, off + p))
    return pl.pallas_call(
        body, name="attn_bwd", grid=(n_seq, n_p, n_q),
        in_specs=[qspec, seq(n_p), seq(2 * n_p), qspec,
                  pl.BlockSpec((tq, 2 * QB), lambda b, p, qi: (b * n_q + qi, p)),
                  pl.BlockSpec((4 * QB, 2 * QB), lambda b, p, qi: (0, 0))],
        out_specs=[qspec, seq(0), seq(0)],
        out_shape=[jax.ShapeDtypeStruct((T, D_MODEL), BF16)] * 3,
        scratch_shapes=[pltpu.VMEM((n_k, 2 * QB, LANES), BF16), pltpu.VMEM((n_k, 2 * QB, LANES), BF16),
                        pltpu.VMEM((tq, 2 * QB), F32), pltpu.VMEM((tq, 2 * QB), F32), pltpu.VMEM((tq, LANES), F32),
                        pltpu.VMEM((S, LANES), F32), pltpu.VMEM((S, LANES), F32)],
        compiler_params=_cp("parallel", "parallel", "arbitrary"),
    )(qkv, qkv, qkv, do, tot, u_prefix)


def _shifted_copies(sh_ref):
    rows = sh_ref.shape[1] - SUBLANES
    for s in range(1, SUBLANES):
        sh_ref[s, 0:rows, :] = sh_ref[0, s:s + rows, :]


def _shifted(sh_ref, start, n):
    s = start % SUBLANES
    return sh_ref[s, start - s:start - s + n, :]


def _glu_with_halo(av_ref, ag_ref, avh_ref, agh_ref, a0_s, first, ts):
    hal = avh_ref[...] * _sigmoid(agh_ref[...])
    a0_s[0, 0:HALO, :] = jnp.where(first, 0.0, hal)
    a0_s[0, HALO:HALO + ts, :] = av_ref[...] * _sigmoid(ag_ref[...])
    _shifted_copies(a0_s)


def _mix_specs(ts, n_r, with_left):
    blk = lambda c: pl.BlockSpec((ts, CA), lambda b, r: (b * n_r + r, c))
    per = ts // HALO
    left = lambda c: pl.BlockSpec((HALO, CA), lambda b, r: (jnp.maximum((b * n_r + r) * per - 1, 0), c))
    return blk, (left if with_left else None)


def _mix_fwd(z, conv_w, conv_b, ln_a_g, ln_a_b, ln_v_g, ln_v_b, ws, bias2d, n_seq):
    T = z.shape[0]
    S = T // n_seq
    ts = _tile(S, 512)
    n_r = S // ts
    shift = HALO - (CONV_WIDTH - 1)

    def body(av_ref, ag_ref, avh_ref, agh_ref, u_ref, v_ref, cw_ref, cb_ref, lag_ref, lab_ref,
             lvg_ref, lvb_ref, ws_ref, bias_ref, cat_ref, a1_ref, a0_s):
        _glu_with_halo(av_ref, ag_ref, avh_ref, agh_ref, a0_s, pl.program_id(1) == 0, ts)
        for rb in range(ts // CONV_ROWS):
            base = rb * CONV_ROWS
            acc = jnp.broadcast_to(cb_ref[...], (CONV_ROWS, CA))
            for k in range(CONV_WIDTH):
                acc = acc + cw_ref[k:k + 1, :] * _shifted(a0_s, base + shift + k, CONV_ROWS)
            a1_ref[base:base + CONV_ROWS, :] = acc
        y, _, _ = _ln_fwd(a1_ref[...], lag_ref[...], lab_ref[...])
        cat_ref[:, 0:CA] = (y * _sigmoid(y)).astype(BF16)
        for gi in range(GB):
            sl = slice(gi * DB, (gi + 1) * DB)
            v1, _, _ = _ln_fwd(v_ref[:, sl], lvg_ref[:, sl], lvb_ref[:, sl])
            v1 = v1.astype(BF16)
            for c in range(ts // CHUNK):
                rs = slice(c * CHUNK, (c + 1) * CHUNK)
                v2 = _dot(ws_ref[gi], v1[rs], NN) + bias_ref[:, sl]
                cat_ref[rs, CA + gi * DB:CA + (gi + 1) * DB] = (u_ref[rs, sl] * v2).astype(BF16)

    blk, left = _mix_specs(ts, n_r, True)
    vec = pl.BlockSpec((1, CA), lambda b, r: (0, 0))
    return pl.pallas_call(
        body, name="mix_fwd", grid=(n_seq, n_r),
        in_specs=[blk(0), blk(1), left(0), left(1), blk(2), blk(3),
                  pl.BlockSpec((CONV_WIDTH, CA), lambda b, r: (0, 0)), vec, vec, vec, vec, vec,
                  pl.BlockSpec((GB, CHUNK, CHUNK), lambda b, r: (0, 0, 0)),
                  pl.BlockSpec((CHUNK, CB), lambda b, r: (0, 0))],
        out_specs=[pl.BlockSpec((ts, CA + CB), lambda b, r: (b * n_r + r, 0)), blk(0)],
        out_shape=[jax.ShapeDtypeStruct((T, CA + CB), BF16), jax.ShapeDtypeStruct((T, CA), F32)],
        scratch_shapes=[pltpu.VMEM((SUBLANES, HALO + ts, CA), F32)],
        compiler_params=_cp("parallel", "parallel"),
    )(z, z, z, z, z, z, conv_w, conv_b, ln_a_g, ln_a_b, ln_v_g, ln_v_b, ws, bias2d)


def _mix_bwd_rows(dcat, z, a1, ln_a_g, ln_a_b, ln_v_g, ln_v_b, ws, ws_t, bias2d, n_seq):
    T = z.shape[0]
    S = T // n_seq
    ts = _tile(S, 512)
    n_r = S // ts

    def body(dc_ref, u_ref, v_ref, a1_ref, lag_ref, lab_ref, lvg_ref, lvb_ref, ws_ref, wst_ref, bias_ref,
             da1_ref, dz_ref, dlag_ref, dlab_ref, dlvg_ref, dlvb_ref, dws_ref, dsb_ref, dv1_s, dbias_s):
        first = (pl.program_id(0) == 0) & (pl.program_id(1) == 0)
        last = (pl.program_id(0) == n_seq - 1) & (pl.program_id(1) == n_r - 1)

        @pl.when(first)
        def _():
            for ref in (dlag_ref, dlab_ref, dlvg_ref, dlvb_ref, dws_ref, dbias_s):
                ref[...] = jnp.zeros_like(ref)

        lag = lag_ref[...]
        y, xh, r = _ln_fwd(a1_ref[...], lag, lab_ref[...])
        sig = _sigmoid(y)
        dy = dc_ref[:, 0:CA] * (sig * (1.0 + y * (1.0 - sig)))
        dlag_ref[...] += jnp.sum(dy * xh, axis=0, keepdims=True)
        dlab_ref[...] += jnp.sum(dy, axis=0, keepdims=True)
        da1_ref[...] = _ln_bwd(dy, xh, r, lag)

        tril = (lax.broadcasted_iota(jnp.int32, (CHUNK, CHUNK), 0)
                >= lax.broadcasted_iota(jnp.int32, (CHUNK, CHUNK), 1))
        for gi in range(GB):
            sl = slice(gi * DB, (gi + 1) * DB)
            lvg = lvg_ref[:, sl]
            v1, vh, vr = _ln_fwd(v_ref[:, sl], lvg, lvb_ref[:, sl])
            v1 = v1.astype(BF16)
            for c in range(ts // CHUNK):
                rs = slice(c * CHUNK, (c + 1) * CHUNK)
                v2 = _dot(ws_ref[gi], v1[rs], NN) + bias_ref[:, sl]
                dbo = dc_ref[rs, CA + gi * DB:CA + (gi + 1) * DB]
                dz_ref[rs, sl] = (dbo * v2).astype(BF16)
                dv2 = dbo * u_ref[rs, sl]
                dbias_s[:, sl] += dv2
                dv2b = dv2.astype(BF16)
                dws_ref[gi] += jnp.where(tril, _dot(dv2b, v1[rs], NT), 0.0)
                dv1_s[rs, :] = _dot(wst_ref[gi], dv2b, NN)
            dv1 = dv1_s[...]
            dlvg_ref[:, sl] += jnp.sum(dv1 * vh, axis=0, keepdims=True)
            dlvb_ref[:, sl] += jnp.sum(dv1, axis=0, keepdims=True)
            dz_ref[:, CB + gi * DB:CB + (gi + 1) * DB] = _ln_bwd(dv1, vh, vr, lvg).astype(BF16)

        @pl.when(last)
        def _():
            col = lax.broadcasted_iota(jnp.int32, (CHUNK, GB), 1)
            out = jnp.zeros((CHUNK, GB), F32)
            for gi in range(GB):
                s = jnp.sum(dbias_s[:, gi * DB:(gi + 1) * DB], axis=1, keepdims=True)
                out = out + jnp.where(col == gi, s, 0.0)
            dsb_ref[...] = out

    blk, _ = _mix_specs(ts, n_r, False)
    vec = pl.BlockSpec((1, CA), lambda b, r: (0, 0))
    mat = pl.BlockSpec((GB, CHUNK, CHUNK), lambda b, r: (0, 0, 0))
    wide = pl.BlockSpec((ts, CA + CB), lambda b, r: (b * n_r + r, 0))
    return pl.pallas_call(
        body, name="mix_bwd_rows", grid=(n_seq, n_r),
        in_specs=[wide, blk(2), blk(3), blk(0), vec, vec, vec, vec, mat, mat,
                  pl.BlockSpec((CHUNK, CB), lambda b, r: (0, 0))],
        out_specs=[blk(0), wide, vec, vec, vec, vec, mat, pl.BlockSpec((CHUNK, GB), lambda b, r: (0, 0))],
        out_shape=[jax.ShapeDtypeStruct((T, CA), F32), jax.ShapeDtypeStruct((T, 2 * CB), BF16)]
        + [jax.ShapeDtypeStruct((1, CA), F32)] * 4
        + [jax.ShapeDtypeStruct((GB, CHUNK, CHUNK), F32), jax.ShapeDtypeStruct((CHUNK, GB), F32)],
        scratch_shapes=[pltpu.VMEM((ts, DB), F32), pltpu.VMEM((CHUNK, CB), F32)],
        compiler_params=_cp("arbitrary", "arbitrary"),
    )(dcat, z, z, a1, ln_a_g, ln_a_b, ln_v_g, ln_v_b, ws, ws_t, bias2d)


def _mix_bwd_conv(da1, z, conv_w, n_seq):
    T = z.shape[0]
    S = T // n_seq
    ts = _tile(S, 512)
    n_r = S // ts
    per = ts // HALO
    shift = HALO - (CONV_WIDTH - 1)
    fold = CONV_ROWS // 8

    def body(d_ref, dh_ref, av_ref, ag_ref, avh_ref, agh_ref, cw_ref,
             dz_ref, dcw_ref, dcb_ref, a0_s, d1_s, da0_s, dw8_s):
        first = (pl.program_id(0) == 0) & (pl.program_id(1) == 0)
        last = (pl.program_id(0) == n_seq - 1) & (pl.program_id(1) == n_r - 1)

        @pl.when(first)
        def _():
            dw8_s[...] = jnp.zeros_like(dw8_s)
            dcb_ref[...] = jnp.zeros_like(dcb_ref)

        _glu_with_halo(av_ref, ag_ref, avh_ref, agh_ref, a0_s, pl.program_id(1) == 0, ts)
        d1_s[0, 0:ts, :] = d_ref[...]
        d1_s[0, ts:ts + HALO, :] = jnp.where(pl.program_id(1) == n_r - 1, 0.0, dh_ref[...])
        _shifted_copies(d1_s)
        dcb_ref[...] += jnp.sum(d_ref[...], axis=0, keepdims=True)
        for rb in range(ts // CONV_ROWS):
            base = rb * CONV_ROWS
            dcur = d1_s[0, base:base + CONV_ROWS, :]
            acc = jnp.zeros((CONV_ROWS, CA), F32)
            for k in range(CONV_WIDTH):
                back = CONV_WIDTH - 1 - k
                acc = acc + cw_ref[k:k + 1, :] * _shifted(d1_s, base + back, CONV_ROWS)
                prod = dcur * _shifted(a0_s, base + shift + k, CONV_ROWS)
                part = prod[0:8]
                for f in range(1, fold):
                    part = part + prod[8 * f:8 * f + 8]
                dw8_s[k] += part
            da0_s[base:base + CONV_ROWS, :] = acc
        da0 = da0_s[...]
        sig = _sigmoid(ag_ref[...])
        dz_ref[:, 0:CA] = (da0 * sig).astype(BF16)
        dz_ref[:, CA:2 * CA] = (da0 * av_ref[...] * sig * (1.0 - sig)).astype(BF16)

        @pl.when(last)
        def _():
            for k in range(CONV_WIDTH):
                dcw_ref[k:k + 1, :] = jnp.sum(dw8_s[k], axis=0, keepdims=True)

    blk, left = _mix_specs(ts, n_r, True)
    n_halo_blocks = T // HALO
    right = pl.BlockSpec((HALO, CA), lambda b, r: (jnp.minimum((b * n_r + r + 1) * per, n_halo_blocks - 1), 0))
    return pl.pallas_call(
        body, name="mix_bwd_conv", grid=(n_seq, n_r),
        in_specs=[blk(0), right, blk(0), blk(1), left(0), left(1),
                  pl.BlockSpec((CONV_WIDTH, CA), lambda b, r: (0, 0))],
        out_specs=[pl.BlockSpec((ts, 2 * CA), lambda b, r: (b * n_r + r, 0)),
                   pl.BlockSpec((CONV_WIDTH, CA), lambda b, r: (0, 0)), pl.BlockSpec((1, CA), lambda b, r: (0, 0))],
        out_shape=[jax.ShapeDtypeStruct((T, 2 * CA), BF16), jax.ShapeDtypeStruct((CONV_WIDTH, CA), F32),
                   jax.ShapeDtypeStruct((1, CA), F32)],
        scratch_shapes=[pltpu.VMEM((SUBLANES, HALO + ts, CA), F32), pltpu.VMEM((SUBLANES, ts + HALO, CA), F32),
                        pltpu.VMEM((ts, CA), F32), pltpu.VMEM((CONV_WIDTH, 8, CA), F32)],
        compiler_params=_cp("arbitrary", "arbitrary"),
    )(da1, da1, z, z, z, z, conv_w)


def _row_tile(R, want):
    t = min(R, want)
    t -= t % 8
    while t > 8 and R % t:
        t -= 8
    return t if t >= 8 and R % t == 0 else R


def _adam_step(w, g, m, v):
    nm = ADAM_B1 * m + (1.0 - ADAM_B1) * g
    nv = ADAM_B2 * v + (1.0 - ADAM_B2) * (g * g)
    m_hat = nm / (1.0 - ADAM_B1 ** ADAM_STEP)
    v_hat = nv / (1.0 - ADAM_B2 ** ADAM_STEP)
    return -ADAM_LR * (m_hat / (jnp.sqrt(v_hat) + ADAM_EPS) + ADAM_WD * w), nm, nv


def _adamw_parts(parts, w, m, v, name):
    L, n, C = w.shape
    assert len(parts) == L
    tr = _row_tile(n, 192)
    n_i = n // tr

    def body(*refs):
        p_refs = refs[:L]
        w_ref, m_ref, v_ref, g_ref, d_ref, nm_ref, nv_ref = refs[L:]
        for k in range(L):
            @pl.when(pl.program_id(0) == k)
            def _(k=k):
                acc = p_refs[k][0].astype(F32)
                for s in range(1, N_DEV):
                    acc = acc + p_refs[k][s].astype(F32)
                g_ref[...] = acc

        d_ref[...], nm_ref[...], nv_ref[...] = _adam_step(w_ref[...], g_ref[...], m_ref[...], v_ref[...])

    def part_spec(k):
        return pl.BlockSpec((N_DEV, tr, C),
                            lambda l, i: (0, jnp.where(l == k, i, jnp.where(l < k, 0, n_i - 1)), 0))

    blk = pl.BlockSpec((None, tr, C), lambda l, i: (l, i, 0))
    return pl.pallas_call(
        body, name=name, grid=(L, n_i),
        in_specs=[part_spec(k) for k in range(L)] + [blk] * 3, out_specs=[blk] * 4,
        out_shape=[jax.ShapeDtypeStruct((L, n, C), F32)] * 4,
        compiler_params=_cp("arbitrary", "arbitrary"),
    )(*parts, w, m, v)


def _adamw(w, g, m, v, name):
    R, C = w.shape
    tr = _row_tile(R, 256)

    def body(w_ref, g_ref, m_ref, v_ref, d_ref, nm_ref, nv_ref):
        d_ref[...], nm_ref[...], nv_ref[...] = _adam_step(w_ref[...], g_ref[...], m_ref[...], v_ref[...])

    blk = pl.BlockSpec((tr, C), lambda i: (i, 0))
    return pl.pallas_call(
        body, name=name, grid=(R // tr,),
        in_specs=[blk] * 4, out_specs=[blk] * 3,
        out_shape=[jax.ShapeDtypeStruct((R, C), F32)] * 3,
        compiler_params=_cp("parallel"),
    )(w, g, m, v)


def _me():
    return lax.axis_index("x"), lax.axis_index("y"), lax.axis_index("c")


def _block_rows(ref, dev, n):
    start = (4 * dev[0] + 2 * dev[1] + dev[2]) * n
    if len(ref.shape) == 2:
        return ref.at[pl.ds(start, n), :]
    return ref.at[:, pl.ds(start, n), :]


def _all_gather(shards):
    na = len(shards)
    ns = [s.shape[-2] for s in shards]

    def body(*refs):
        ins, outs = refs[:na], refs[na:2 * na]
        send_sems, recv_sems, local_sems = refs[2 * na:]
        x, y, c = _me()
        me, sibling = (x, y, c), (x, y, 1 - c)
        chips = [(1 - x, y), (x, 1 - y), (1 - x, 1 - y)]

        def copy(a, k, block, to, src=None):
            dst = _block_rows(outs[a], block, ns[a])
            return pltpu.make_async_remote_copy(
                src_ref=dst if src is None else src, dst_ref=dst,
                send_sem=send_sems.at[a, k], recv_sem=recv_sems.at[a, k], device_id=to, device_id_type=MESH)

        mine = [pltpu.make_async_copy(ins[a], _block_rows(outs[a], me, ns[a]), local_sems.at[a]) for a in range(na)]
        for cp in mine:
            cp.start()
        first = []
        for a in range(na):
            first.append(copy(a, 0, me, sibling, src=ins[a]))
            first += [copy(a, 1 + j, me, (*chip, c), src=ins[a]) for j, chip in enumerate(chips)]
        for cp in first:
            cp.start()
        passed = []
        for j, chip in enumerate(chips):
            for a in range(na):
                copy(a, 1 + j, (*chip, c), me).wait_recv()
                fwd = copy(a, 4 + j, (*chip, c), sibling)
                fwd.start()
                passed.append(fwd)
        for a in range(na):
            copy(a, 0, sibling, me).wait_recv()
            for j, chip in enumerate(chips):
                copy(a, 4 + j, (*chip, 1 - c), me).wait_recv()
        for cp in first + passed:
            cp.wait_send()
        for cp in mine:
            cp.wait()

    out_shape = [jax.ShapeDtypeStruct(s.shape[:-2] + (N_DEV * s.shape[-2], s.shape[-1]), s.dtype) for s in shards]
    return pl.pallas_call(
        body, name="weights_all_gather",
        in_specs=[ANY] * na, out_specs=[ANY] * na, out_shape=out_shape,
        scratch_shapes=[pltpu.SemaphoreType.DMA((na, 7)), pltpu.SemaphoreType.DMA((na, 7)),
                        pltpu.SemaphoreType.DMA((na,))],
    )(*shards)


def _split_copies(gather, srcs, lands, send_sems, recv_sems, ns):
    x, y, c = _me()
    me = (x, y, c)
    my_slot = 4 * x + 2 * y + c
    copies = []
    for mask in range(1, N_DEV):
        peer = (x ^ (mask >> 2), y ^ ((mask >> 1) & 1), c ^ (mask & 1))
        for a in range(len(srcs)):
            if gather:
                src, dst = srcs[a], _block_rows(lands[a], me, ns[a])
            else:
                src, dst = _block_rows(srcs[a], peer, ns[a]), lands[a].at[my_slot]
            sem = a * (N_DEV - 1) + mask - 1
            copies.append(pltpu.make_async_remote_copy(
                src_ref=src, dst_ref=dst, send_sem=send_sems.at[sem], recv_sem=recv_sems.at[sem],
                device_id=peer, device_id_type=MESH))
    return copies


HBM_SPEC = pl.BlockSpec(memory_space=pltpu.HBM)
SEM_SPEC = pl.BlockSpec(memory_space=pltpu.SEMAPHORE)


def _split_start(gather, srcs, name, dep=None):
    na = len(srcs)
    x, y, c = _me()
    mine = 4 * x + 2 * y + c
    if gather:
        ns = [s.shape[-2] for s in srcs]
        lands = [lax.dynamic_update_slice(
            lax.empty(s.shape[:-2] + (N_DEV * s.shape[-2], s.shape[-1]), s.dtype), s,
            (0,) * (s.ndim - 2) + (mine * s.shape[-2], 0)) for s in srcs]
    else:
        ns = [s.shape[-2] // N_DEV for s in srcs]
        lands = [lax.dynamic_update_slice(
            lax.empty((N_DEV, n, s.shape[-1]), s.dtype),
            lax.dynamic_slice(s, (mine * n, 0), (n, s.shape[-1]))[None], (mine, 0, 0)) for s, n in zip(srcs, ns)]
    n_in = 2 * na + (dep is not None)

    def body(*refs):
        send_sems, recv_sems = refs[n_in], refs[n_in + 1]
        for cp in _split_copies(gather, refs[:na], refs[na:2 * na], send_sems, recv_sems, ns):
            cp.start()
        refs[-1][...] = jnp.zeros_like(refs[-1])

    hbm = lambda a: pltpu.with_memory_space_constraint(a, pltpu.HBM)
    args = [hbm(a) for a in srcs] + [hbm(a) for a in lands] + ([dep] if dep is not None else [])
    out = pl.pallas_call(
        body, name=name,
        in_specs=[HBM_SPEC] * (2 * na) + ([ANY] if dep is not None else []),
        out_specs=[SEM_SPEC, SEM_SPEC] + [HBM_SPEC] * (2 * na) + [pl.BlockSpec(memory_space=pltpu.VMEM)],
        out_shape=[pltpu.SemaphoreType.DMA((na * (N_DEV - 1),)), pltpu.SemaphoreType.DMA((na * (N_DEV - 1),))]
        + [pltpu.HBM(a.shape, a.dtype) for a in srcs + lands] + [jax.ShapeDtypeStruct((8, LANES), F32)],
        input_output_aliases={i: 2 + i for i in range(2 * na)},
        compiler_params=pltpu.CompilerParams(has_side_effects=pltpu.SideEffectType.DATAFLOW_SIDE_EFFECTING),
    )(*args)
    return (gather, ns, out[0], out[1], list(out[2:2 + na]), list(out[2 + na:2 + 2 * na])), out[-1]


def _split_wait(handle, after, name):
    gather, ns, send, recv, srcs, lands = handle
    na = len(srcs)

    def body(*refs):
        send_sems, recv_sems = refs[2 * na], refs[2 * na + 1]
        for cp in _split_copies(gather, refs[:na], refs[na:2 * na], send_sems, recv_sems, ns):
            cp.wait_send()
            cp.wait_recv()

    out = pl.pallas_call(
        body, name=name,
        in_specs=[HBM_SPEC] * (2 * na) + [SEM_SPEC, SEM_SPEC, ANY],
        out_specs=[HBM_SPEC] * (2 * na),
        out_shape=[pltpu.HBM(a.shape, a.dtype) for a in srcs + lands],
        input_output_aliases={i: i for i in range(2 * na)},
        compiler_params=pltpu.CompilerParams(has_side_effects=pltpu.SideEffectType.DATAFLOW_SIDE_EFFECTING),
    )(*srcs, *lands, send, recv, after)
    return list(out[:na]), list(out[na:])


def _small_all_reduce(buf):
    R = buf.shape[0]

    def body(b_ref, o_ref, recv_ref, send_sems, recv_sems):
        x, y, c = _me()
        my_slot = 4 * x + 2 * y + c
        recv_ref[my_slot] = b_ref[...]
        copies = []
        for mask in range(1, N_DEV):
            peer = (x ^ (mask >> 2), y ^ ((mask >> 1) & 1), c ^ (mask & 1))
            copies.append(pltpu.make_async_remote_copy(
                src_ref=b_ref, dst_ref=recv_ref.at[my_slot],
                send_sem=send_sems.at[mask - 1], recv_sem=recv_sems.at[mask - 1],
                device_id=peer, device_id_type=MESH))
        for cp in copies:
            cp.start()
        for cp in copies:
            cp.wait()
        acc = recv_ref[0]
        for k in range(1, N_DEV):
            acc = acc + recv_ref[k]
        o_ref[...] = acc

    return pl.pallas_call(
        body, name="small_all_reduce",
        in_specs=[pl.BlockSpec(memory_space=pltpu.VMEM)], out_specs=pl.BlockSpec(memory_space=pltpu.VMEM),
        out_shape=jax.ShapeDtypeStruct((R, LANES), F32),
        scratch_shapes=[pltpu.VMEM((N_DEV, R, LANES), F32), pltpu.SemaphoreType.DMA((7,)),
                        pltpu.SemaphoreType.DMA((7,))],
        compiler_params=pltpu.CompilerParams(vmem_limit_bytes=VMEM_LIMIT),
    )(buf)


def _pack(arrays):
    flat = jnp.concatenate([a.reshape(-1) for a in arrays])
    pad = (-flat.shape[0]) % (8 * LANES)
    return jnp.pad(flat, (0, pad)).reshape(-1, LANES)


def _unpack(buf, shapes):
    flat = buf.reshape(-1)
    out, off = [], 0
    for s in shapes:
        n = 1
        for d in s:
            n *= d
        out.append(flat[off:off + n].reshape(s))
        off += n
    return out


def _ffn_index(layer, second):
    return (2 * layer + second) * 3


def kernel(x, g_ffn1, w_ffn1_gate, w_ffn1_up, w_ffn1_down, g_mix, w_in_ab, conv_w, conv_b, ln_a_g, ln_a_b, ln_v_g, ln_v_b, sp_w, sp_b, w_out_ab, w_qkv, w_o, g_ffn2, w_ffn2_gate, w_ffn2_up, w_ffn2_down, g_final, loss_target, m_g_ffn1, m_w_ffn1_gate, m_w_ffn1_up, m_w_ffn1_down, m_g_mix, m_w_in_ab, m_conv_w, m_conv_b, m_ln_a_g, m_ln_a_b, m_ln_v_g, m_ln_v_b, m_sp_w, m_sp_b, m_w_out_ab, m_w_qkv, m_w_o, m_g_ffn2, m_w_ffn2_gate, m_w_ffn2_up, m_w_ffn2_down, m_g_final, v_g_ffn1, v_w_ffn1_gate, v_w_ffn1_up, v_w_ffn1_down, v_g_mix, v_w_in_ab, v_conv_w, v_conv_b, v_ln_a_g, v_ln_a_b, v_ln_v_g, v_ln_v_b, v_sp_w, v_sp_b, v_w_out_ab, v_w_qkv, v_w_o, v_g_ffn2, v_w_ffn2_gate, v_w_ffn2_up, v_w_ffn2_down, v_g_final):
    n_seq, S, D = x.shape
    T = n_seq * S
    depth = g_ffn1.shape[0]
    assert depth == 2 and D == D_MODEL
    my_block = 4 * lax.axis_index("x") + 2 * lax.axis_index("y") + lax.axis_index("c")

    ffn_parts = []
    for l in range(depth):
        for gate, up, down in ((w_ffn1_gate, w_ffn1_up, w_ffn1_down), (w_ffn2_gate, w_ffn2_up, w_ffn2_down)):
            ffn_parts += [gate[l].T, up[l].T, down[l]]
    ffn_shard = lambda k: jnp.stack(ffn_parts[3 * k:3 * k + 3]).astype(BF16)
    conv_w_pad = jnp.zeros((HALO, conv_w.shape[2]), F32).at[:CONV_WIDTH].set(conv_w[0]).T
    w_ffn = [None] * (2 * depth)
    w_ffn[0], conv_w_t = _all_gather([ffn_shard(0), conv_w_pad])
    conv_w_full = conv_w_t.T[:CONV_WIDTH]
    shards_b = [w_out_ab[0].astype(BF16), ffn_shard(1)]
    shards_d = [w_qkv[0].T.astype(BF16), w_o[0].astype(BF16), ffn_shard(3)]
    gather_a, token = _split_start(True, [w_in_ab[0].T.astype(BF16)], "gather_a_start", dep=conv_w_t)
    gather_b, token = _split_start(True, shards_b, "gather_b_start", dep=token)
    gather_c, token = _split_start(True, [ffn_shard(2)], "gather_c_start", dep=token)
    gather_d, token = _split_start(True, shards_d, "gather_d_start", dep=token)

    def gathered(handle, after, name):
        return _split_wait(handle, after, name)[1]

    row = lambda a: a.reshape(1, -1)
    tril = jnp.tril(jnp.ones((CHUNK, CHUNK), dtype=bool))
    ws = jnp.where(tril[None], sp_w[0], 0.0).astype(BF16)
    ws_t = jnp.swapaxes(ws, 1, 2)
    bias2d = jnp.repeat(sp_b[0].T, DB, axis=1)
    conv_b2, lag, lab = row(conv_b[0]), row(ln_a_g[0]), row(ln_a_b[0])
    lvg, lvb = row(ln_v_g[0]), row(ln_v_b[0])

    x0 = x.reshape(T, D)
    target = loss_target.reshape(T, D)
    saved = []
    xc = x0
    for l in range(depth):
        xa, a1, b1, h1 = _ffn_fwd(xc, row(g_ffn1[l]), w_ffn[2 * l], 0, f"ffn1_fwd_{l}", dep=token)
        if l % 2 == 0:
            w_in_t, = gathered(gather_a, xa, "gather_a_wait")
            z, hm = _mm_nt(xa, w_in_t, row(g_mix[l]), F32, "mix_in_proj")
            cat, conv_out = _mix_fwd(z, conv_w_full, conv_b2, lag, lab, lvg, lvb, ws, bias2d, n_seq)
            w_out, w_ffn[1] = gathered(gather_b, cat, "gather_b_wait")
            xb = _mm_nn_res(cat, w_out, xa, "mix_out_proj")
            mixer = (z, hm, cat, conv_out)
        else:
            w_qkv_t, w_o_full, w_ffn[3] = gathered(gather_d, xa, "gather_d_wait")
            qkv, hm = _mm_nt(xa, w_qkv_t, row(g_mix[l]), BF16, "qkv_proj")
            o, tot = _attn_fwd(qkv, n_seq)
            xb = _mm_nn_res(o, w_o_full, xa, "attn_out_proj")
            mixer = (qkv, hm, o, tot)
        xn, a2, b2, h2 = _ffn_fwd(xb, row(g_ffn2[l]), w_ffn[2 * l + 1], 0, f"ffn2_fwd_{l}")
        saved.append((xc, a1, b1, h1, xa, mixer, xb, a2, b2, h2))
        xc = xn
        if l == 0:
            w_ffn[2], = gathered(gather_c, xc, "gather_c_wait")

    g, loss_part, dg_final = _loss_head(xc, row(g_final), target)

    dg_ffn1, dg_ffn2, dg_mix = [None] * depth, [None] * depth, [None] * depth
    exchanges = {}
    token = None

    def ffn_back(g, xin, gvec, a, b, h, k, tag, token):
        g, dg, da, db, s, gh = _ffn_bwd(g, xin, gvec, a, b, w_ffn[k], 0, f"ffn{tag}_bwd", dep=token)
        if k == 0:
            return g, dg, (da, db, s, gh, h)
        dws = [_mm_tn(da, h, f"dw_gate{tag}"), _mm_tn(db, h, f"dw_up{tag}"), _mm_tn(s, gh, f"dw_down{tag}")]
        exchanges[f"ffn{k}"], token = _split_start(False, dws, f"exchange_ffn{tag}_start")
        return g, dg, token

    for l in reversed(range(depth)):
        xin, a1, b1, h1, xa, mixer, xb, a2, b2, h2 = saved[l]
        g, dg_ffn2[l], token = ffn_back(g, xb, row(g_ffn2[l]), a2, b2, h2, 2 * l + 1, f"2_{l}", token)
        if l % 2 == 0:
            z, hm, cat, conv_out = mixer
            dcat = _mm_nt(g, w_out, None, F32, "mix_out_bwd", dep=token)
            d_w_out = _mm_tn(cat, g, "dw_out")
            (da1, dz_uv, d_lag, d_lab, d_lvg, d_lvb, d_ws, d_sb) = _mix_bwd_rows(
                dcat, z, conv_out, lag, lab, lvg, lvb, ws, ws_t, bias2d, n_seq)
            dz_a, d_cw, d_cb = _mix_bwd_conv(da1, z, conv_w_full, n_seq)
            d_w_in_t = jnp.concatenate([_mm_tn(dz_a, hm, "dw_in_a"), _mm_tn(dz_uv, hm, "dw_in_uv")])
            exchanges["mix"], token = _split_start(False, [d_w_out, d_w_in_t], "exchange_mix_start")
            g, dg_mix[l] = _mm_nn_rmsbwd([dz_a, dz_uv], w_in_t, xa, row(g_mix[l]), g, "mix_in_bwd", dep=token)
        else:
            qkv, hm, o, tot = mixer
            do = _mm_nt(g, w_o_full, None, BF16, "attn_out_bwd", dep=token)
            d_w_o = _mm_tn(o, g, "dw_o")
            dq, dk, dv = _attn_bwd(qkv, do, tot, n_seq)
            d_w_qkv_t = jnp.concatenate([_mm_tn(dq, hm, "dw_q"), _mm_tn(dk, hm, "dw_k"), _mm_tn(dv, hm, "dw_v")])
            exchanges["attn"], token = _split_start(False, [d_w_o, d_w_qkv_t], "exchange_attn_start")
            g, dg_mix[l] = _mm_nn_rmsbwd([dq, dk, dv], w_qkv_t, xa, row(g_mix[l]), g, "qkv_bwd", dep=token)
        g, dg_ffn1[l], token = ffn_back(g, xin, row(g_ffn1[l]), a1, b1, h1, 2 * l, f"1_{l}", token)
    grad_x = g.reshape(n_seq, S, D)

    small = [jnp.concatenate(dg_ffn1), jnp.concatenate(dg_mix), d_cw, d_cb, d_lag, d_lab, d_lvg, d_lvb,
             jnp.where(tril[None], d_ws, 0.0), d_sb.T, jnp.concatenate(dg_ffn2), dg_final, loss_part[:, :1]]
    small_shapes = [(depth, D), (depth, D), (CONV_WIDTH, CA), (1, CA), (1, CA), (1, CA), (1, GB, DB), (1, GB, DB),
                    (1, GB, CHUNK, CHUNK), (1, GB, CHUNK), (depth, D), (D,), ()]
    small_sum = _small_all_reduce(_pack(small))
    red = _unpack(small_sum, small_shapes)
    (gr_g_ffn1, gr_g_mix, gr_cw_full, gr_cb, gr_lag, gr_lab, gr_lvg, gr_lvb, gr_sp_w, gr_sp_b,
     gr_g_ffn2, gr_g_final, loss) = red
    n_cw = conv_w.shape[2]
    gr_cw = lax.dynamic_slice(gr_cw_full, (0, my_block * n_cw), (CONV_WIDTH, n_cw))[None]

    da, db, s, gh, h = token
    token = small_sum
    for which, lhs, rhs in ((2, s, gh), (1, db, h), (0, da, h)):
        dw = _mm_tn(lhs, rhs, f"dw_ffn0_{which}", dep=token, out_dtype=BF16)
        exchanges[f"ffn0_{which}"], token = _split_start(False, [dw], f"exchange_ffn0_{which}_start")

    def landed(key, after):
        return _split_wait(exchanges[key], after, f"exchange_{key}_wait")[1]

    parts_ffn = [None] * (6 * depth)
    for k in range(1, 2 * depth):
        parts_ffn[3 * k:3 * k + 3] = landed(f"ffn{k}", token)
    parts_out, parts_in = landed("mix", token)
    parts_o, parts_qkv = landed("attn", token)

    grads = {
        "g_ffn1": gr_g_ffn1, "g_mix": gr_g_mix, "conv_w": gr_cw, "conv_b": gr_cb, "ln_a_g": gr_lag,
        "ln_a_b": gr_lab, "ln_v_g": gr_lvg, "ln_v_b": gr_lvb, "sp_w": gr_sp_w, "sp_b": gr_sp_b,
        "g_ffn2": gr_g_ffn2, "g_final": gr_g_final,
    }
    weights = dict(g_ffn1=g_ffn1, w_ffn1_gate=w_ffn1_gate, w_ffn1_up=w_ffn1_up, w_ffn1_down=w_ffn1_down, g_mix=g_mix,
                   w_in_ab=w_in_ab, conv_w=conv_w, conv_b=conv_b, ln_a_g=ln_a_g, ln_a_b=ln_a_b, ln_v_g=ln_v_g,
                   ln_v_b=ln_v_b, sp_w=sp_w, sp_b=sp_b, w_out_ab=w_out_ab, w_qkv=w_qkv, w_o=w_o, g_ffn2=g_ffn2,
                   w_ffn2_gate=w_ffn2_gate, w_ffn2_up=w_ffn2_up, w_ffn2_down=w_ffn2_down, g_final=g_final)
    m_in = dict(g_ffn1=m_g_ffn1, w_ffn1_gate=m_w_ffn1_gate, w_ffn1_up=m_w_ffn1_up, w_ffn1_down=m_w_ffn1_down,
                g_mix=m_g_mix, w_in_ab=m_w_in_ab, conv_w=m_conv_w, conv_b=m_conv_b, ln_a_g=m_ln_a_g, ln_a_b=m_ln_a_b,
                ln_v_g=m_ln_v_g, ln_v_b=m_ln_v_b, sp_w=m_sp_w, sp_b=m_sp_b, w_out_ab=m_w_out_ab, w_qkv=m_w_qkv,
                w_o=m_w_o, g_ffn2=m_g_ffn2, w_ffn2_gate=m_w_ffn2_gate, w_ffn2_up=m_w_ffn2_up,
                w_ffn2_down=m_w_ffn2_down, g_final=m_g_final)
    v_in = dict(g_ffn1=v_g_ffn1, w_ffn1_gate=v_w_ffn1_gate, w_ffn1_up=v_w_ffn1_up, w_ffn1_down=v_w_ffn1_down,
                g_mix=v_g_mix, w_in_ab=v_w_in_ab, conv_w=v_conv_w, conv_b=v_conv_b, ln_a_g=v_ln_a_g, ln_a_b=v_ln_a_b,
                ln_v_g=v_ln_v_g, ln_v_b=v_ln_v_b, sp_w=v_sp_w, sp_b=v_sp_b, w_out_ab=v_w_out_ab, w_qkv=v_w_qkv,
                w_o=v_w_o, g_ffn2=v_g_ffn2, w_ffn2_gate=v_w_ffn2_gate, w_ffn2_up=v_w_ffn2_up,
                w_ffn2_down=v_w_ffn2_down, g_final=v_g_final)
    names = list(weights)
    grads = {n: grads[n].reshape(weights[n].shape) for n in grads}

    delta, new_m, new_v = {}, {}, {}

    def adamw_big(n, parts):
        if weights[n].shape[-1] == D:
            view = back = lambda a: a
        else:
            view = back = lambda a: jnp.swapaxes(a, 1, 2)
        out = _adamw_parts(parts, view(weights[n]), view(m_in[n]), view(v_in[n]), f"adamw_{n}")
        grads[n], delta[n], new_m[n], new_v[n] = [back(a) for a in out]

    adamw_big("w_in_ab", [parts_in])
    adamw_big("w_out_ab", [parts_out])
    adamw_big("w_qkv", [parts_qkv])
    adamw_big("w_o", [parts_o])
    kinds = ("gate", "up", "down")
    for which, kind in enumerate(kinds):
        adamw_big(f"w_ffn2_{kind}", [parts_ffn[_ffn_index(l, 1) + which] for l in range(depth)])
    big = [n for n in names if n.startswith("w_")]
    after = jnp.concatenate([delta[n].reshape(-1)[:1] for n in big if n in delta]).reshape(1, -1)
    for which in (2, 1, 0):
        parts_ffn[which], = landed(f"ffn0_{which}", after)
    for which, kind in enumerate(kinds):
        adamw_big(f"w_ffn1_{kind}", [parts_ffn[_ffn_index(l, 0) + which] for l in range(depth)])
    little = [n for n in names if n not in big]
    shapes = [weights[n].shape for n in little]
    d, nm, nv = _adamw(_pack([weights[n] for n in little]), _pack([grads[n] for n in little]),
                       _pack([m_in[n] for n in little]), _pack([v_in[n] for n in little]), "adamw_small")
    for n, dd, mm, vv in zip(little, _unpack(d, shapes), _unpack(nm, shapes), _unpack(nv, shapes)):
        delta[n], new_m[n], new_v[n] = dd, mm, vv

    return (loss, grad_x, *[grads[n] for n in names], *[delta[n] for n in names],
            *[new_m[n] for n in names], *[new_v[n] for n in names])
```

```python
import functools

import jax
import jax.numpy as jnp
from jax import lax
from jax.experimental import pallas as pl
from jax.experimental.pallas import tpu as pltpu

F32 = jnp.float32
BF16 = jnp.bfloat16

D_MODEL = 1024
CA = 512
CB = 512
GB = 4
DB = 128
CHUNK = 128
CONV_WIDTH = 31
N_HEADS = 16
HEAD_DIM = 64
EPS = 1e-6
N_DEV = 8
LANES = 128
SUBLANES = 8
QB = 128
ATT_TQ = 1024
FFN_TN = 1408
HALO = 32
CONV_ROWS = 32
ATT_SCALE = HEAD_DIM ** -0.5

ADAM_LR = 0.001
ADAM_B1 = 0.9
ADAM_B2 = 0.999
ADAM_EPS = 1e-08
ADAM_WD = 0.01
ADAM_STEP = 10

NT = (((1,), (1,)), ((), ()))
NN = (((1,), (0,)), ((), ()))
TN = (((0,), (0,)), ((), ()))
MESH = pl.DeviceIdType.MESH
ANY = pl.BlockSpec(memory_space=pl.ANY)
VMEM_LIMIT = 60 * 1024 * 1024


def _dot(a, b, dims):
    return lax.dot_general(a, b, dims, preferred_element_type=F32)


def _cp(*sem):
    return pltpu.CompilerParams(dimension_semantics=sem, vmem_limit_bytes=VMEM_LIMIT)


def _pcall(body, *, in_specs, args, dep=None, **kw):
    if dep is not None:
        n_in = len(in_specs)
        inner = body

        def body(*refs):
            inner(*refs[:n_in], *refs[n_in + 1:])

        in_specs = list(in_specs) + [ANY]
        args = tuple(args) + (dep,)
    return pl.pallas_call(body, in_specs=list(in_specs), **kw)(*args)


def _tile(n, want):
    if n <= want:
        return n
    t = want - want % LANES
    while t > LANES and n % t:
        t -= LANES
    assert n % t == 0, (n, want)
    return t


def _sigmoid(x):
    return 0.5 * jnp.tanh(0.5 * x) + 0.5


def _rstd(x):
    return lax.rsqrt(jnp.mean(x * x, axis=-1, keepdims=True) + EPS)


def _rms_bwd(x, g, dh):
    r = _rstd(x)
    u = dh * g
    dx = r * (u - x * (r * r) * jnp.mean(u * x, axis=-1, keepdims=True))
    dg = jnp.sum(dh * x * r, axis=0, keepdims=True)
    return dx, dg


def _ln_fwd(x, g, b):
    mu = jnp.mean(x, axis=-1, keepdims=True)
    xc = x - mu
    r = lax.rsqrt(jnp.mean(xc * xc, axis=-1, keepdims=True) + EPS)
    xh = xc * r
    return xh * g + b, xh, r


def _ln_bwd(dy, xh, r, g):
    dxh = dy * g
    return r * (dxh - jnp.mean(dxh, axis=-1, keepdims=True)
                - xh * jnp.mean(dxh * xh, axis=-1, keepdims=True))


def _ffn_fwd(x, g, wall, base, name, dep=None):
    T, D = x.shape
    F = wall.shape[1]
    tm, tn = _tile(T, 512), _tile(F, FFN_TN)
    n_j = F // tn

    def body(x_ref, g_ref, wg_ref, wu_ref, wd_ref, xo_ref, a_ref, b_ref, h_ref, acc_ref):
        j = pl.program_id(1)

        @pl.when(j == 0)
        def _():
            xv = x_ref[...]
            h_ref[...] = (xv * _rstd(xv) * g_ref[...]).astype(BF16)
            acc_ref[...] = jnp.zeros_like(acc_ref)

        h = h_ref[...]
        a = _dot(h, wg_ref[...], NT)
        b = _dot(h, wu_ref[...], NT)
        a_ref[...] = a.astype(BF16)
        b_ref[...] = b.astype(BF16)
        s = (a * _sigmoid(a) * b).astype(BF16)
        acc_ref[...] += _dot(s, wd_ref[...], NN)

        @pl.when(j == n_j - 1)
        def _():
            xo_ref[...] = x_ref[...] + 0.5 * acc_ref[...]

    wspec = lambda k: pl.BlockSpec((None, tn, D), lambda i, j: (base + k, j, 0))
    return _pcall(
        body, name=name, grid=(T // tm, n_j), dep=dep, args=(x, g, wall, wall, wall),
        in_specs=[pl.BlockSpec((tm, D), lambda i, j: (i, 0)), pl.BlockSpec((1, D), lambda i, j: (0, 0)),
                  wspec(0), wspec(1), wspec(2)],
        out_specs=[pl.BlockSpec((tm, D), lambda i, j: (i, 0)), pl.BlockSpec((tm, tn), lambda i, j: (i, j)),
                   pl.BlockSpec((tm, tn), lambda i, j: (i, j)), pl.BlockSpec((tm, D), lambda i, j: (i, 0))],
        out_shape=[jax.ShapeDtypeStruct((T, D), F32), jax.ShapeDtypeStruct((T, F), BF16),
                   jax.ShapeDtypeStruct((T, F), BF16), jax.ShapeDtypeStruct((T, D), BF16)],
        scratch_shapes=[pltpu.VMEM((tm, D), F32)],
        compiler_params=_cp("parallel", "arbitrary"),
    )


def _ffn_bwd(go, x, g, a, b, wall, base, name, dep=None):
    T, D = x.shape
    F = wall.shape[1]
    tm, tn = _tile(T, 512), _tile(F, FFN_TN)
    n_j = F // tn

    def body(go_ref, x_ref, g_ref, a_ref, b_ref, wg_ref, wu_ref, wd_ref,
             gx_ref, dg_ref, da_ref, db_ref, s_ref, gh_ref, acc_ref):
        i, j = pl.program_id(0), pl.program_id(1)

        @pl.when(j == 0)
        def _():
            gh_ref[...] = (0.5 * go_ref[...]).astype(BF16)
            acc_ref[...] = jnp.zeros_like(acc_ref)

        @pl.when((i == 0) & (j == 0))
        def _():
            dg_ref[...] = jnp.zeros_like(dg_ref)

        ds = _dot(gh_ref[...], wd_ref[...], NT)
        av = a_ref[...].astype(F32)
        bv = b_ref[...].astype(F32)
        sig = _sigmoid(av)
        sl = av * sig
        dab = ((ds * bv) * (sig + sl * (1.0 - sig))).astype(BF16)
        dbb = (ds * sl).astype(BF16)
        s_ref[...] = (sl * bv).astype(BF16)
        da_ref[...] = dab
        db_ref[...] = dbb
        acc_ref[...] += _dot(dab, wg_ref[...], NN) + _dot(dbb, wu_ref[...], NN)

        @pl.when(j == n_j - 1)
        def _():
            dx, dg = _rms_bwd(x_ref[...], g_ref[...], acc_ref[...])
            gx_ref[...] = go_ref[...] + dx
            dg_ref[...] += dg

    wspec = lambda k: pl.BlockSpec((None, tn, D), lambda i, j: (base + k, j, 0))
    row = pl.BlockSpec((tm, D), lambda i, j: (i, 0))
    hid = pl.BlockSpec((tm, tn), lambda i, j: (i, j))
    vec = pl.BlockSpec((1, D), lambda i, j: (0, 0))
    return _pcall(
        body, name=name, grid=(T // tm, n_j), dep=dep, args=(go, x, g, a, b, wall, wall, wall),
        in_specs=[row, row, vec, hid, hid, wspec(0), wspec(1), wspec(2)],
        out_specs=[row, vec, hid, hid, hid, row],
        out_shape=[jax.ShapeDtypeStruct((T, D), F32), jax.ShapeDtypeStruct((1, D), F32),
                   jax.ShapeDtypeStruct((T, F), BF16), jax.ShapeDtypeStruct((T, F), BF16),
                   jax.ShapeDtypeStruct((T, F), BF16), jax.ShapeDtypeStruct((T, D), BF16)],
        scratch_shapes=[pltpu.VMEM((tm, D), F32)],
        compiler_params=_cp("arbitrary", "arbitrary"),
    )


def _mm_tn(a, b, name, dep=None, out_dtype=F32):
    T, M = a.shape
    N = b.shape[1]
    tmm, tk = _tile(M, 1536), _tile(T, 2048)
    n_k = T // tk
    narrow = out_dtype != F32

    def body(a_ref, b_ref, o_ref, *scratch):
        acc_ref = scratch[0] if narrow else o_ref

        @pl.when(pl.program_id(1) == 0)
        def _():
            acc_ref[...] = jnp.zeros_like(acc_ref)

        acc_ref[...] += _dot(a_ref[...].astype(BF16), b_ref[...].astype(BF16), TN)
        if narrow:
            @pl.when(pl.program_id(1) == n_k - 1)
            def _():
                o_ref[...] = acc_ref[...].astype(out_dtype)

    return _pcall(
        body, name=name, grid=(M // tmm, n_k), dep=dep, args=(a, b),
        in_specs=[pl.BlockSpec((tk, tmm), lambda m, k: (k, m)), pl.BlockSpec((tk, N), lambda m, k: (k, 0))],
        out_specs=pl.BlockSpec((tmm, N), lambda m, k: (m, 0)),
        out_shape=jax.ShapeDtypeStruct((M, N), out_dtype),
        scratch_shapes=[pltpu.VMEM((tmm, N), F32)] if narrow else [],
        compiler_params=_cp("parallel", "arbitrary"),
    )


def _mm_nt(x, wt, g, out_dtype, name, dep=None):
    T, K = x.shape
    N = wt.shape[0]
    tm, tn = _tile(T, 512), N
    norm = g is not None

    def body(*refs):
        if norm:
            x_ref, g_ref, w_ref, o_ref, h_ref = refs
        else:
            x_ref, w_ref, o_ref, h_ref = refs

        @pl.when(pl.program_id(1) == 0)
        def _():
            xv = x_ref[...].astype(F32)
            if norm:
                xv = xv * _rstd(xv) * g_ref[...]
            h_ref[...] = xv.astype(BF16)

        o_ref[...] = _dot(h_ref[...], w_ref[...], NT).astype(out_dtype)

    row = pl.BlockSpec((tm, K), lambda i, j: (i, 0))
    wsp = pl.BlockSpec((tn, K), lambda i, j: (j, 0))
    osp = pl.BlockSpec((tm, tn), lambda i, j: (i, j))
    if norm:
        return pl.pallas_call(
            body, name=name, grid=(T // tm, N // tn),
            in_specs=[row, pl.BlockSpec((1, K), lambda i, j: (0, 0)), wsp],
            out_specs=[osp, row],
            out_shape=[jax.ShapeDtypeStruct((T, N), out_dtype), jax.ShapeDtypeStruct((T, K), BF16)],
            compiler_params=_cp("parallel", "arbitrary"),
        )(x, g, wt)
    return _pcall(
        body, name=name, grid=(T // tm, N // tn), dep=dep, args=(x, wt),
        in_specs=[row, wsp], out_specs=osp,
        out_shape=jax.ShapeDtypeStruct((T, N), out_dtype),
        scratch_shapes=[pltpu.VMEM((tm, K), BF16)],
        compiler_params=_cp("parallel", "arbitrary"),
    )


def _mm_nn_res(act, w, resid, name):
    T, K = act.shape
    D = w.shape[1]
    tm = _tile(T, 512)

    def body(a_ref, w_ref, r_ref, o_ref):
        o_ref[...] = r_ref[...] + _dot(a_ref[...].astype(BF16), w_ref[...], NN)

    return pl.pallas_call(
        body, name=name, grid=(T // tm,),
        in_specs=[pl.BlockSpec((tm, K), lambda i: (i, 0)), pl.BlockSpec((K, D), lambda i: (0, 0)),
                  pl.BlockSpec((tm, D), lambda i: (i, 0))],
        out_specs=pl.BlockSpec((tm, D), lambda i: (i, 0)),
        out_shape=jax.ShapeDtypeStruct((T, D), F32),
        compiler_params=_cp("parallel"),
    )(act, w, resid)


def _mm_nn_rmsbwd(acts, w, x, g, gprev, name, dep=None):
    T = acts[0].shape[0]
    ks = [a.shape[1] for a in acts]
    K, D = w.shape
    assert sum(ks) == K
    tm = _tile(T, 512)
    na = len(acts)

    def body(*refs):
        a_refs = refs[:na]
        w_ref, x_ref, g_ref, gp_ref, o_ref, dg_ref = refs[na:]

        @pl.when(pl.program_id(0) == 0)
        def _():
            dg_ref[...] = jnp.zeros_like(dg_ref)

        dh, off = None, 0
        for a_ref, k in zip(a_refs, ks):
            part = _dot(a_ref[...].astype(BF16), w_ref[off:off + k, :], NN)
            dh = part if dh is None else dh + part
            off += k
        dx, dg = _rms_bwd(x_ref[...], g_ref[...], dh)
        o_ref[...] = gp_ref[...] + dx
        dg_ref[...] += dg

    row = pl.BlockSpec((tm, D), lambda i: (i, 0))
    vec = pl.BlockSpec((1, D), lambda i: (0, 0))
    return _pcall(
        body, name=name, grid=(T // tm,), dep=dep, args=(*acts, w, x, g, gprev),
        in_specs=[pl.BlockSpec((tm, k), lambda i: (i, 0)) for k in ks]
        + [pl.BlockSpec((K, D), lambda i: (0, 0)), row, vec, row],
        out_specs=[row, vec],
        out_shape=[jax.ShapeDtypeStruct((T, D), F32), jax.ShapeDtypeStruct((1, D), F32)],
        compiler_params=_cp("arbitrary"),
    )


def _loss_head(x, g, target):
    T, D = x.shape
    tm = _tile(T, 512)

    def body(x_ref, g_ref, t_ref, dx_ref, loss_ref, dg_ref):
        @pl.when(pl.program_id(0) == 0)
        def _():
            loss_ref[...] = jnp.zeros_like(loss_ref)
            dg_ref[...] = jnp.zeros_like(dg_ref)

        xv = x_ref[...]
        gv = g_ref[...]
        e = xv * _rstd(xv) * gv - t_ref[...]
        per_tok = jnp.sum(e * e, axis=-1, keepdims=True) * (1.0 / D)
        loss_ref[...] += 0.5 * jnp.sum(per_tok, axis=0, keepdims=True)
        dx, dg = _rms_bwd(xv, gv, e * (1.0 / D))
        dx_ref[...] = dx
        dg_ref[...] += dg

    row = pl.BlockSpec((tm, D), lambda i: (i, 0))
    vec = pl.BlockSpec((1, D), lambda i: (0, 0))
    return pl.pallas_call(
        body, name="loss_head", grid=(T // tm,),
        in_specs=[row, vec, row],
        out_specs=[row, pl.BlockSpec((1, LANES), lambda i: (0, 0)), vec],
        out_shape=[jax.ShapeDtypeStruct((T, D), F32), jax.ShapeDtypeStruct((1, LANES), F32),
                   jax.ShapeDtypeStruct((1, D), F32)],
        compiler_params=_cp("arbitrary"),
    )(x, g, target)


def _log_gates(z):
    ls = jnp.minimum(z, 0.0) - jnp.log(1.0 + jnp.exp(-jnp.abs(z)))
    return ls, ls - z


def _cumsum_mm(v, u2):
    hi = v.astype(BF16)
    lo = (v - hi.astype(F32)).astype(BF16)
    return _dot(jnp.concatenate([hi, lo], axis=1), u2, NN)


def _half_rowsum(v):
    n = v.shape[0]
    s0 = jnp.sum(v[:, :QB], axis=1, keepdims=True)
    s1 = jnp.sum(v[:, QB:], axis=1, keepdims=True)
    return jnp.concatenate([jnp.broadcast_to(s0, (n, QB)), jnp.broadcast_to(s1, (n, QB))], axis=1)


def _stack_heads(src_ref, dst_ref, n_blk):
    m0 = lax.broadcasted_iota(jnp.int32, (1, LANES), 1) < HEAD_DIM

    def fill(c, carry):
        blk = src_ref[pl.ds(pl.multiple_of(c * QB, QB), QB), :]
        zero = jnp.zeros_like(blk)
        dst_ref[c, 0:QB, :] = jnp.where(m0, blk, zero)
        dst_ref[c, QB:2 * QB, :] = jnp.where(m0, zero, blk)
        return carry

    lax.fori_loop(0, n_blk, fill, 0)


def _diag_mask(tq, j):
    n = tq - j * QB
    row = lax.broadcasted_iota(jnp.int32, (n, 2 * QB), 0)
    col = lax.broadcasted_iota(jnp.int32, (n, 2 * QB), 1)
    return (col & (QB - 1)) < row


def _tri_blockdiag(upper):
    r = lax.broadcasted_iota(jnp.int32, (2 * QB, 2 * QB), 0)
    c = lax.broadcasted_iota(jnp.int32, (2 * QB, 2 * QB), 1)
    same = (r // QB) == (c // QB)
    u = (same & ((r > c) if upper else (r < c))).astype(BF16)
    return jnp.concatenate([u, u], axis=0)


def _attn_tiles(T, n_seq):
    S = T // n_seq
    tq = ATT_TQ if S % ATT_TQ == 0 else 2 * QB
    assert S % tq == 0
    return S, tq, tq // QB, S // tq, S // QB


def _attn_fwd(qkv, n_seq):
    T = qkv.shape[0]
    S, tq, r, n_q, n_k = _attn_tiles(T, n_seq)
    n_p = D_MODEL // LANES
    u_suffix = _tri_blockdiag(True)

    def body(q_ref, k_ref, v_ref, u_ref, o_ref, a_hbm, kk_ref, vv_ref, lr_s, acc_s, a_stage, sems):
        qi = pl.program_id(2)
        group = (pl.program_id(0) * n_p + pl.program_id(1)) * n_q + qi

        @pl.when(qi == 0)
        def _():
            _stack_heads(k_ref, kk_ref, n_k)
            _stack_heads(v_ref, vv_ref, n_k)

        u = u_ref[...]
        lr_s[...] = jnp.zeros_like(lr_s)
        acc_s[...] = jnp.zeros_like(acc_s)
        a_stage[0] = jnp.zeros_like(a_stage[0])

        def saves(first_kj, half):
            return [pltpu.make_async_copy(a_stage.at[half, j], a_hbm.at[group, first_kj - j], sems.at[half])
                    for j in range(r)]

        def step(kj, half, j, rows, q, mask, lr, acc):
            ls, lk = _log_gates(_dot(q, kk_ref[kj], NT))
            if mask is not None:
                lk = jnp.where(mask, lk, 0.0)
            a = jnp.exp(ls + _cumsum_mm(lk, u) + lr)
            if mask is not None:
                a = jnp.where(mask, a, 0.0)
            a = a.astype(BF16)
            a_stage[half, j, rows, :] = a
            return lr + _half_rowsum(lk), acc + _dot(a, vv_ref[kj], NN)

        last = (qi + 1) * r - 1
        for n in range(r):
            rows = slice((r - 1 - n) * QB, tq)
            lr, acc = step(last - n, 0, n, rows, q_ref[rows, :] * ATT_SCALE, _diag_mask(tq, r - 1 - n),
                           lr_s[rows, :], acc_s[rows, :])
            lr_s[rows, :] = lr
            acc_s[rows, :] = acc
        for cp in saves(last, 0):
            cp.start()

        q = q_ref[...] * ATT_SCALE

        def off(it, carry):
            half = (it + 1) % 2
            first = (qi - it) * r - 1

            @pl.when(it >= 1)
            def _():
                for cp in saves(first, half):
                    cp.wait()

            lr, acc = lr_s[...], acc_s[...]
            for j in range(r):
                lr, acc = step(first - j, half, j, slice(0, tq), q, None, lr, acc)
            lr_s[...] = lr
            acc_s[...] = acc
            for cp in saves(first, half):
                cp.start()
            return carry

        lax.fori_loop(0, qi, off, 0)
        for cp in saves(last, 0):
            cp.wait()

        @pl.when(qi >= 1)
        def _():
            for cp in saves(last, 1):
                cp.wait()

        o_ref[...] = acc_s[...].astype(BF16)

    return pl.pallas_call(
        body, name="attn_fwd", grid=(n_seq, n_p, n_q),
        in_specs=[pl.BlockSpec((tq, LANES), lambda b, p, qi: (b * n_q + qi, p)),
                  pl.BlockSpec((S, LANES), lambda b, p, qi: (b, n_p + p)),
                  pl.BlockSpec((S, LANES), lambda b, p, qi: (b, 2 * n_p + p)),
                  pl.BlockSpec((4 * QB, 2 * QB), lambda b, p, qi: (0, 0))],
        out_specs=[pl.BlockSpec((tq, LANES), lambda b, p, qi: (b * n_q + qi, p)), ANY],
        out_shape=[jax.ShapeDtypeStruct((T, D_MODEL), BF16),
                   jax.ShapeDtypeStruct((n_seq * n_p * n_q, n_k, tq, 2 * QB), BF16)],
        scratch_shapes=[pltpu.VMEM((n_k, 2 * QB, LANES), BF16), pltpu.VMEM((n_k, 2 * QB, LANES), BF16),
                        pltpu.VMEM((tq, 2 * QB), F32), pltpu.VMEM((tq, LANES), F32),
                        pltpu.VMEM((2, r, tq, 2 * QB), BF16), pltpu.SemaphoreType.DMA((2,))],
        compiler_params=_cp("parallel", "parallel", "arbitrary"),
    )(qkv, qkv, qkv, u_suffix)


def _attn_bwd(qkv, do, a_saved, n_seq):
    T = qkv.shape[0]
    S, tq, r, n_q, n_k = _attn_tiles(T, n_seq)
    n_p = D_MODEL // LANES
    u_prefix = _tri_blockdiag(False)[:2 * QB]

    def body(q_ref, k_ref, v_ref, do_ref, u_ref, a_hbm, dq_ref, dk_out, dv_out,
             kk_ref, vv_ref, cg_s, dq_s, dk_ref, dv_ref, a_stage, sems):
        qi = pl.program_id(2)
        group = (pl.program_id(0) * n_p + pl.program_id(1)) * n_q + qi

        def fetches(g, half):
            return [pltpu.make_async_copy(a_hbm.at[group, g * r + j], a_stage.at[half, j], sems.at[half])
                    for j in range(r)]

        for cp in fetches(0, 0):
            cp.start()

        @pl.when(qi == 0)
        def _():
            _stack_heads(k_ref, kk_ref, n_k)
            _stack_heads(v_ref, vv_ref, n_k)
            dk_ref[...] = jnp.zeros_like(dk_ref)
            dv_ref[...] = jnp.zeros_like(dv_ref)

        u = u_ref[...]
        m0 = lax.broadcasted_iota(jnp.int32, (1, LANES), 1) < HEAD_DIM
        cg_s[...] = jnp.zeros_like(cg_s)
        dq_s[...] = jnp.zeros_like(dq_s)

        def step(kj, half, j, rows, q, dov, mask, cg, dq):
            kk = kk_ref[kj]
            beta = _sigmoid(_dot(q, kk, NT))
            a = a_stage[half, j, rows, :]
            g = a.astype(F32) * _dot(dov, vv_ref[kj], NT)
            dz = g - (g + _dot(g.astype(BF16), u, NN) + cg) * beta
            if mask is not None:
                dz = jnp.where(mask, dz, 0.0)
            dz = dz.astype(BF16)
            keys = pl.ds(pl.multiple_of(kj * QB, QB), QB)
            dvt = _dot(a, dov, TN)
            dv_ref[keys, :] += jnp.where(m0, dvt[:QB], dvt[QB:])
            dkt = _dot(dz, q, TN)
            dk_ref[keys, :] += jnp.where(m0, dkt[:QB], dkt[QB:])
            return cg + _half_rowsum(g), dq + _dot(dz, kk, NN)

        q = q_ref[...] * ATT_SCALE
        dov = do_ref[...]

        def off(it, carry):
            half = it % 2
            for cp in fetches(it, half):
                cp.wait()
            for cp in fetches(it + 1, 1 - half):
                cp.start()
            cg, dq = cg_s[...], dq_s[...]
            for j in range(r):
                cg, dq = step(it * r + j, half, j, slice(0, tq), q, dov, None, cg, dq)
            cg_s[...] = cg
            dq_s[...] = dq
            return carry

        lax.fori_loop(0, qi, off, 0)

        for cp in fetches(qi, qi % 2):
            cp.wait()
        for j in range(r):
            rows = slice(j * QB, tq)
            cg, dq = step(qi * r + j, qi % 2, j, rows, q_ref[rows, :] * ATT_SCALE, do_ref[rows, :],
                          _diag_mask(tq, j), cg_s[rows, :], dq_s[rows, :])
            cg_s[rows, :] = cg
            dq_s[rows, :] = dq
        dq_ref[...] = (dq_s[...] * ATT_SCALE).astype(BF16)

        @pl.when(qi == n_q - 1)
        def _():
            dk_out[...] = dk_ref[...].astype(BF16)
            dv_out[...] = dv_ref[...].astype(BF16)

    qspec = pl.BlockSpec((tq, LANES), lambda b, p, qi: (b * n_q + qi, p))
    seq = lambda off: pl.BlockSpec((S, LANES), lambda b, p, qi: (b, off + p))
    return pl.pallas_call(
        body, name="attn_bwd", grid=(n_seq, n_p, n_q),
        in_specs=[qspec, seq(n_p), seq(2 * n_p), qspec,
                  pl.BlockSpec((2 * QB, 2 * QB), lambda b, p, qi: (0, 0)), ANY],
        out_specs=[qspec, seq(0), seq(0)],
        out_shape=[jax.ShapeDtypeStruct((T, D_MODEL), BF16)] * 3,
        scratch_shapes=[pltpu.VMEM((n_k, 2 * QB, LANES), BF16), pltpu.VMEM((n_k, 2 * QB, LANES), BF16),
                        pltpu.VMEM((tq, 2 * QB), F32), pltpu.VMEM((tq, LANES), F32),
                        pltpu.VMEM((S, LANES), F32), pltpu.VMEM((S, LANES), F32),
                        pltpu.VMEM((2, r, tq, 2 * QB), BF16), pltpu.SemaphoreType.DMA((2,))],
        compiler_params=_cp("parallel", "parallel", "arbitrary"),
    )(qkv, qkv, qkv, do, u_prefix, a_saved)


def _shifted_copies(sh_ref):
    rows = sh_ref.shape[1] - SUBLANES
    for s in range(1, SUBLANES):
        sh_ref[s, 0:rows, :] = sh_ref[0, s:s + rows, :]


def _shifted(sh_ref, start, n):
    s = start % SUBLANES
    return sh_ref[s, start - s:start - s + n, :]


def _glu_with_halo(av_ref, ag_ref, avh_ref, agh_ref, a0_s, first, ts):
    hal = avh_ref[...] * _sigmoid(agh_ref[...])
    a0_s[0, 0:HALO, :] = jnp.where(first, 0.0, hal)
    a0_s[0, HALO:HALO + ts, :] = av_ref[...] * _sigmoid(ag_ref[...])
    _shifted_copies(a0_s)


def _mix_specs(ts, n_r, with_left):
    blk = lambda c: pl.BlockSpec((ts, CA), lambda b, r: (b * n_r + r, c))
    per = ts // HALO
    left = lambda c: pl.BlockSpec((HALO, CA), lambda b, r: (jnp.maximum((b * n_r + r) * per - 1, 0), c))
    return blk, (left if with_left else None)


def _mix_fwd(z, conv_w, conv_b, ln_a_g, ln_a_b, ln_v_g, ln_v_b, ws, bias2d, n_seq):
    T = z.shape[0]
    S = T // n_seq
    ts = _tile(S, 512)
    n_r = S // ts
    shift = HALO - (CONV_WIDTH - 1)

    def body(av_ref, ag_ref, avh_ref, agh_ref, u_ref, v_ref, cw_ref, cb_ref, lag_ref, lab_ref,
             lvg_ref, lvb_ref, ws_ref, bias_ref, cat_ref, a1_ref, a0_s):
        _glu_with_halo(av_ref, ag_ref, avh_ref, agh_ref, a0_s, pl.program_id(1) == 0, ts)
        for rb in range(ts // CONV_ROWS):
            base = rb * CONV_ROWS
            acc = jnp.broadcast_to(cb_ref[...], (CONV_ROWS, CA))
            for k in range(CONV_WIDTH):
                acc = acc + cw_ref[k:k + 1, :] * _shifted(a0_s, base + shift + k, CONV_ROWS)
            a1_ref[base:base + CONV_ROWS, :] = acc
        y, _, _ = _ln_fwd(a1_ref[...], lag_ref[...], lab_ref[...])
        cat_ref[:, 0:CA] = (y * _sigmoid(y)).astype(BF16)
        for gi in range(GB):
            sl = slice(gi * DB, (gi + 1) * DB)
            v1, _, _ = _ln_fwd(v_ref[:, sl], lvg_ref[:, sl], lvb_ref[:, sl])
            v1 = v1.astype(BF16)
            for c in range(ts // CHUNK):
                rs = slice(c * CHUNK, (c + 1) * CHUNK)
                v2 = _dot(ws_ref[gi], v1[rs], NN) + bias_ref[:, sl]
                cat_ref[rs, CA + gi * DB:CA + (gi + 1) * DB] = (u_ref[rs, sl] * v2).astype(BF16)

    blk, left = _mix_specs(ts, n_r, True)
    vec = pl.BlockSpec((1, CA), lambda b, r: (0, 0))
    return pl.pallas_call(
        body, name="mix_fwd", grid=(n_seq, n_r),
        in_specs=[blk(0), blk(1), left(0), left(1), blk(2), blk(3),
                  pl.BlockSpec((CONV_WIDTH, CA), lambda b, r: (0, 0)), vec, vec, vec, vec, vec,
                  pl.BlockSpec((GB, CHUNK, CHUNK), lambda b, r: (0, 0, 0)),
                  pl.BlockSpec((CHUNK, CB), lambda b, r: (0, 0))],
        out_specs=[pl.BlockSpec((ts, CA + CB), lambda b, r: (b * n_r + r, 0)), blk(0)],
        out_shape=[jax.ShapeDtypeStruct((T, CA + CB), BF16), jax.ShapeDtypeStruct((T, CA), F32)],
        scratch_shapes=[pltpu.VMEM((SUBLANES, HALO + ts, CA), F32)],
        compiler_params=_cp("parallel", "parallel"),
    )(z, z, z, z, z, z, conv_w, conv_b, ln_a_g, ln_a_b, ln_v_g, ln_v_b, ws, bias2d)


def _mix_bwd_rows(dcat, z, a1, ln_a_g, ln_a_b, ln_v_g, ln_v_b, ws, ws_t, bias2d, n_seq):
    T = z.shape[0]
    S = T // n_seq
    ts = _tile(S, 512)
    n_r = S // ts

    def body(dc_ref, u_ref, v_ref, a1_ref, lag_ref, lab_ref, lvg_ref, lvb_ref, ws_ref, wst_ref, bias_ref,
             da1_ref, dz_ref, dlag_ref, dlab_ref, dlvg_ref, dlvb_ref, dws_ref, dsb_ref, dv1_s, dbias_s):
        first = (pl.program_id(0) == 0) & (pl.program_id(1) == 0)
        last = (pl.program_id(0) == n_seq - 1) & (pl.program_id(1) == n_r - 1)

        @pl.when(first)
        def _():
            for ref in (dlag_ref, dlab_ref, dlvg_ref, dlvb_ref, dws_ref, dbias_s):
                ref[...] = jnp.zeros_like(ref)

        lag = lag_ref[...]
        y, xh, r = _ln_fwd(a1_ref[...], lag, lab_ref[...])
        sig = _sigmoid(y)
        dy = dc_ref[:, 0:CA] * (sig * (1.0 + y * (1.0 - sig)))
        dlag_ref[...] += jnp.sum(dy * xh, axis=0, keepdims=True)
        dlab_ref[...] += jnp.sum(dy, axis=0, keepdims=True)
        da1_ref[...] = _ln_bwd(dy, xh, r, lag)

        tril = (lax.broadcasted_iota(jnp.int32, (CHUNK, CHUNK), 0)
                >= lax.broadcasted_iota(jnp.int32, (CHUNK, CHUNK), 1))
        for gi in range(GB):
            sl = slice(gi * DB, (gi + 1) * DB)
            lvg = lvg_ref[:, sl]
            v1, vh, vr = _ln_fwd(v_ref[:, sl], lvg, lvb_ref[:, sl])
            v1 = v1.astype(BF16)
            for c in range(ts // CHUNK):
                rs = slice(c * CHUNK, (c + 1) * CHUNK)
                v2 = _dot(ws_ref[gi], v1[rs], NN) + bias_ref[:, sl]
                dbo = dc_ref[rs, CA + gi * DB:CA + (gi + 1) * DB]
                dz_ref[rs, sl] = (dbo * v2).astype(BF16)
                dv2 = dbo * u_ref[rs, sl]
                dbias_s[:, sl] += dv2
                dv2b = dv2.astype(BF16)
                dws_ref[gi] += jnp.where(tril, _dot(dv2b, v1[rs], NT), 0.0)
                dv1_s[rs, :] = _dot(wst_ref[gi], dv2b, NN)
            dv1 = dv1_s[...]
            dlvg_ref[:, sl] += jnp.sum(dv1 * vh, axis=0, keepdims=True)
            dlvb_ref[:, sl] += jnp.sum(dv1, axis=0, keepdims=True)
            dz_ref[:, CB + gi * DB:CB + (gi + 1) * DB] = _ln_bwd(dv1, vh, vr, lvg).astype(BF16)

        @pl.when(last)
        def _():
            col = lax.broadcasted_iota(jnp.int32, (CHUNK, GB), 1)
            out = jnp.zeros((CHUNK, GB), F32)
            for gi in range(GB):
                s = jnp.sum(dbias_s[:, gi * DB:(gi + 1) * DB], axis=1, keepdims=True)
                out = out + jnp.where(col == gi, s, 0.0)
            dsb_ref[...] = out

    blk, _ = _mix_specs(ts, n_r, False)
    vec = pl.BlockSpec((1, CA), lambda b, r: (0, 0))
    mat = pl.BlockSpec((GB, CHUNK, CHUNK), lambda b, r: (0, 0, 0))
    wide = pl.BlockSpec((ts, CA + CB), lambda b, r: (b * n_r + r, 0))
    return pl.pallas_call(
        body, name="mix_bwd_rows", grid=(n_seq, n_r),
        in_specs=[wide, blk(2), blk(3), blk(0), vec, vec, vec, vec, mat, mat,
                  pl.BlockSpec((CHUNK, CB), lambda b, r: (0, 0))],
        out_specs=[blk(0), wide, vec, vec, vec, vec, mat, pl.BlockSpec((CHUNK, GB), lambda b, r: (0, 0))],
        out_shape=[jax.ShapeDtypeStruct((T, CA), F32), jax.ShapeDtypeStruct((T, 2 * CB), BF16)]
        + [jax.ShapeDtypeStruct((1, CA), F32)] * 4
        + [jax.ShapeDtypeStruct((GB, CHUNK, CHUNK), F32), jax.ShapeDtypeStruct((CHUNK, GB), F32)],
        scratch_shapes=[pltpu.VMEM((ts, DB), F32), pltpu.VMEM((CHUNK, CB), F32)],
        compiler_params=_cp("arbitrary", "arbitrary"),
    )(dcat, z, z, a1, ln_a_g, ln_a_b, ln_v_g, ln_v_b, ws, ws_t, bias2d)


def _mix_bwd_conv(da1, z, conv_w, n_seq):
    T = z.shape[0]
    S = T // n_seq
    ts = _tile(S, 512)
    n_r = S // ts
    per = ts // HALO
    shift = HALO - (CONV_WIDTH - 1)
    fold = CONV_ROWS // 8

    def body(d_ref, dh_ref, av_ref, ag_ref, avh_ref, agh_ref, cw_ref,
             dz_ref, dcw_ref, dcb_ref, a0_s, d1_s, da0_s, dw8_s):
        first = (pl.program_id(0) == 0) & (pl.program_id(1) == 0)
        last = (pl.program_id(0) == n_seq - 1) & (pl.program_id(1) == n_r - 1)

        @pl.when(first)
        def _():
            dw8_s[...] = jnp.zeros_like(dw8_s)
            dcb_ref[...] = jnp.zeros_like(dcb_ref)

        _glu_with_halo(av_ref, ag_ref, avh_ref, agh_ref, a0_s, pl.program_id(1) == 0, ts)
        d1_s[0, 0:ts, :] = d_ref[...]
        d1_s[0, ts:ts + HALO, :] = jnp.where(pl.program_id(1) == n_r - 1, 0.0, dh_ref[...])
        _shifted_copies(d1_s)
        dcb_ref[...] += jnp.sum(d_ref[...], axis=0, keepdims=True)
        for rb in range(ts // CONV_ROWS):
            base = rb * CONV_ROWS
            dcur = d1_s[0, base:base + CONV_ROWS, :]
            acc = jnp.zeros((CONV_ROWS, CA), F32)
            for k in range(CONV_WIDTH):
                back = CONV_WIDTH - 1 - k
                acc = acc + cw_ref[k:k + 1, :] * _shifted(d1_s, base + back, CONV_ROWS)
                prod = dcur * _shifted(a0_s, base + shift + k, CONV_ROWS)
                part = prod[0:8]
                for f in range(1, fold):
                    part = part + prod[8 * f:8 * f + 8]
                dw8_s[k] += part
            da0_s[base:base + CONV_ROWS, :] = acc
        da0 = da0_s[...]
        sig = _sigmoid(ag_ref[...])
        dz_ref[:, 0:CA] = (da0 * sig).astype(BF16)
        dz_ref[:, CA:2 * CA] = (da0 * av_ref[...] * sig * (1.0 - sig)).astype(BF16)

        @pl.when(last)
        def _():
            for k in range(CONV_WIDTH):
                dcw_ref[k:k + 1, :] = jnp.sum(dw8_s[k], axis=0, keepdims=True)

    blk, left = _mix_specs(ts, n_r, True)
    n_halo_blocks = T // HALO
    right = pl.BlockSpec((HALO, CA), lambda b, r: (jnp.minimum((b * n_r + r + 1) * per, n_halo_blocks - 1), 0))
    return pl.pallas_call(
        body, name="mix_bwd_conv", grid=(n_seq, n_r),
        in_specs=[blk(0), right, blk(0), blk(1), left(0), left(1),
                  pl.BlockSpec((CONV_WIDTH, CA), lambda b, r: (0, 0))],
        out_specs=[pl.BlockSpec((ts, 2 * CA), lambda b, r: (b * n_r + r, 0)),
                   pl.BlockSpec((CONV_WIDTH, CA), lambda b, r: (0, 0)), pl.BlockSpec((1, CA), lambda b, r: (0, 0))],
        out_shape=[jax.ShapeDtypeStruct((T, 2 * CA), BF16), jax.ShapeDtypeStruct((CONV_WIDTH, CA), F32),
                   jax.ShapeDtypeStruct((1, CA), F32)],
        scratch_shapes=[pltpu.VMEM((SUBLANES, HALO + ts, CA), F32), pltpu.VMEM((SUBLANES, ts + HALO, CA), F32),
                        pltpu.VMEM((ts, CA), F32), pltpu.VMEM((CONV_WIDTH, 8, CA), F32)],
        compiler_params=_cp("arbitrary", "arbitrary"),
    )(da1, da1, z, z, z, z, conv_w)


def _row_tile(R, want):
    t = min(R, want)
    t -= t % 8
    while t > 8 and R % t:
        t -= 8
    return t if t >= 8 and R % t == 0 else R


def _adam_step(w, g, m, v):
    nm = ADAM_B1 * m + (1.0 - ADAM_B1) * g
    nv = ADAM_B2 * v + (1.0 - ADAM_B2) * (g * g)
    m_hat = nm / (1.0 - ADAM_B1 ** ADAM_STEP)
    v_hat = nv / (1.0 - ADAM_B2 ** ADAM_STEP)
    return -ADAM_LR * (m_hat / (jnp.sqrt(v_hat) + ADAM_EPS) + ADAM_WD * w), nm, nv


def _adamw_parts(parts, w, m, v, name):
    L, n, C = w.shape
    assert len(parts) == L
    tr = _row_tile(n, 192)
    n_i = n // tr

    def body(*refs):
        p_refs = refs[:L]
        w_ref, m_ref, v_ref, g_ref, d_ref, nm_ref, nv_ref = refs[L:]
        for k in range(L):
            @pl.when(pl.program_id(0) == k)
            def _(k=k):
                acc = p_refs[k][0].astype(F32)
                for s in range(1, N_DEV):
                    acc = acc + p_refs[k][s].astype(F32)
                g_ref[...] = acc

        d_ref[...], nm_ref[...], nv_ref[...] = _adam_step(w_ref[...], g_ref[...], m_ref[...], v_ref[...])

    def part_spec(k):
        return pl.BlockSpec((N_DEV, tr, C),
                            lambda l, i: (0, jnp.where(l == k, i, jnp.where(l < k, 0, n_i - 1)), 0))

    blk = pl.BlockSpec((None, tr, C), lambda l, i: (l, i, 0))
    return pl.pallas_call(
        body, name=name, grid=(L, n_i),
        in_specs=[part_spec(k) for k in range(L)] + [blk] * 3, out_specs=[blk] * 4,
        out_shape=[jax.ShapeDtypeStruct((L, n, C), F32)] * 4,
        compiler_params=_cp("arbitrary", "arbitrary"),
    )(*parts, w, m, v)


def _adamw(w, g, m, v, name):
    R, C = w.shape
    tr = _row_tile(R, 256)

    def body(w_ref, g_ref, m_ref, v_ref, d_ref, nm_ref, nv_ref):
        d_ref[...], nm_ref[...], nv_ref[...] = _adam_step(w_ref[...], g_ref[...], m_ref[...], v_ref[...])

    blk = pl.BlockSpec((tr, C), lambda i: (i, 0))
    return pl.pallas_call(
        body, name=name, grid=(R // tr,),
        in_specs=[blk] * 4, out_specs=[blk] * 3,
        out_shape=[jax.ShapeDtypeStruct((R, C), F32)] * 3,
        compiler_params=_cp("parallel"),
    )(w, g, m, v)


def _me():
    return lax.axis_index("x"), lax.axis_index("y"), lax.axis_index("c")


def _block_rows(ref, dev, n):
    start = (4 * dev[0] + 2 * dev[1] + dev[2]) * n
    if len(ref.shape) == 2:
        return ref.at[pl.ds(start, n), :]
    return ref.at[:, pl.ds(start, n), :]


def _all_gather(shards):
    na = len(shards)
    ns = [s.shape[-2] for s in shards]

    def body(*refs):
        ins, outs = refs[:na], refs[na:2 * na]
        send_sems, recv_sems, local_sems = refs[2 * na:]
        x, y, c = _me()
        me, sibling = (x, y, c), (x, y, 1 - c)
        chips = [(1 - x, y), (x, 1 - y), (1 - x, 1 - y)]

        def copy(a, k, block, to, src=None):
            dst = _block_rows(outs[a], block, ns[a])
            return pltpu.make_async_remote_copy(
                src_ref=dst if src is None else src, dst_ref=dst,
                send_sem=send_sems.at[a, k], recv_sem=recv_sems.at[a, k], device_id=to, device_id_type=MESH)

        mine = [pltpu.make_async_copy(ins[a], _block_rows(outs[a], me, ns[a]), local_sems.at[a]) for a in range(na)]
        for cp in mine:
            cp.start()
        first = []
        for a in range(na):
            first.append(copy(a, 0, me, sibling, src=ins[a]))
            first += [copy(a, 1 + j, me, (*chip, c), src=ins[a]) for j, chip in enumerate(chips)]
        for cp in first:
            cp.start()
        passed = []
        for j, chip in enumerate(chips):
            for a in range(na):
                copy(a, 1 + j, (*chip, c), me).wait_recv()
                fwd = copy(a, 4 + j, (*chip, c), sibling)
                fwd.start()
                passed.append(fwd)
        for a in range(na):
            copy(a, 0, sibling, me).wait_recv()
            for j, chip in enumerate(chips):
                copy(a, 4 + j, (*chip, 1 - c), me).wait_recv()
        for cp in first + passed:
            cp.wait_send()
        for cp in mine:
            cp.wait()

    out_shape = [jax.ShapeDtypeStruct(s.shape[:-2] + (N_DEV * s.shape[-2], s.shape[-1]), s.dtype) for s in shards]
    return pl.pallas_call(
        body, name="weights_all_gather",
        in_specs=[ANY] * na, out_specs=[ANY] * na, out_shape=out_shape,
        scratch_shapes=[pltpu.SemaphoreType.DMA((na, 7)), pltpu.SemaphoreType.DMA((na, 7)),
                        pltpu.SemaphoreType.DMA((na,))],
    )(*shards)


def _split_copies(gather, srcs, lands, send_sems, recv_sems, ns):
    x, y, c = _me()
    me = (x, y, c)
    my_slot = 4 * x + 2 * y + c
    copies = []
    for mask in range(1, N_DEV):
        peer = (x ^ (mask >> 2), y ^ ((mask >> 1) & 1), c ^ (mask & 1))
        for a in range(len(srcs)):
            if gather:
                src, dst = srcs[a], _block_rows(lands[a], me, ns[a])
            else:
                src, dst = _block_rows(srcs[a], peer, ns[a]), lands[a].at[my_slot]
            sem = a * (N_DEV - 1) + mask - 1
            copies.append(pltpu.make_async_remote_copy(
                src_ref=src, dst_ref=dst, send_sem=send_sems.at[sem], recv_sem=recv_sems.at[sem],
                device_id=peer, device_id_type=MESH))
    return copies


HBM_SPEC = pl.BlockSpec(memory_space=pltpu.HBM)
SEM_SPEC = pl.BlockSpec(memory_space=pltpu.SEMAPHORE)


def _split_start(gather, srcs, name, dep=None):
    na = len(srcs)
    x, y, c = _me()
    mine = 4 * x + 2 * y + c
    if gather:
        ns = [s.shape[-2] for s in srcs]
        lands = [lax.dynamic_update_slice(
            lax.empty(s.shape[:-2] + (N_DEV * s.shape[-2], s.shape[-1]), s.dtype), s,
            (0,) * (s.ndim - 2) + (mine * s.shape[-2], 0)) for s in srcs]
    else:
        ns = [s.shape[-2] // N_DEV for s in srcs]
        lands = [lax.dynamic_update_slice(
            lax.empty((N_DEV, n, s.shape[-1]), s.dtype),
            lax.dynamic_slice(s, (mine * n, 0), (n, s.shape[-1]))[None], (mine, 0, 0)) for s, n in zip(srcs, ns)]
    n_in = 2 * na + (dep is not None)

    def body(*refs):
        send_sems, recv_sems = refs[n_in], refs[n_in + 1]
        for cp in _split_copies(gather, refs[:na], refs[na:2 * na], send_sems, recv_sems, ns):
            cp.start()
        refs[-1][...] = jnp.zeros_like(refs[-1])

    hbm = lambda a: pltpu.with_memory_space_constraint(a, pltpu.HBM)
    args = [hbm(a) for a in srcs] + [hbm(a) for a in lands] + ([dep] if dep is not None else [])
    out = pl.pallas_call(
        body, name=name,
        in_specs=[HBM_SPEC] * (2 * na) + ([ANY] if dep is not None else []),
        out_specs=[SEM_SPEC, SEM_SPEC] + [HBM_SPEC] * (2 * na) + [pl.BlockSpec(memory_space=pltpu.VMEM)],
        out_shape=[pltpu.SemaphoreType.DMA((na * (N_DEV - 1),)), pltpu.SemaphoreType.DMA((na * (N_DEV - 1),))]
        + [pltpu.HBM(a.shape, a.dtype) for a in srcs + lands] + [jax.ShapeDtypeStruct((8, LANES), F32)],
        input_output_aliases={i: 2 + i for i in range(2 * na)},
        compiler_params=pltpu.CompilerParams(has_side_effects=pltpu.SideEffectType.DATAFLOW_SIDE_EFFECTING),
    )(*args)
    return (gather, ns, out[0], out[1], list(out[2:2 + na]), list(out[2 + na:2 + 2 * na])), out[-1]


def _split_wait(handle, after, name):
    gather, ns, send, recv, srcs, lands = handle
    na = len(srcs)

    def body(*refs):
        send_sems, recv_sems = refs[2 * na], refs[2 * na + 1]
        for cp in _split_copies(gather, refs[:na], refs[na:2 * na], send_sems, recv_sems, ns):
            cp.wait_send()
            cp.wait_recv()

    out = pl.pallas_call(
        body, name=name,
        in_specs=[HBM_SPEC] * (2 * na) + [SEM_SPEC, SEM_SPEC, ANY],
        out_specs=[HBM_SPEC] * (2 * na),
        out_shape=[pltpu.HBM(a.shape, a.dtype) for a in srcs + lands],
        input_output_aliases={i: i for i in range(2 * na)},
        compiler_params=pltpu.CompilerParams(has_side_effects=pltpu.SideEffectType.DATAFLOW_SIDE_EFFECTING),
    )(*srcs, *lands, send, recv, after)
    return list(out[:na]), list(out[na:])


def _small_all_reduce(buf):
    R = buf.shape[0]

    def body(b_ref, o_ref, recv_ref, send_sems, recv_sems):
        x, y, c = _me()
        my_slot = 4 * x + 2 * y + c
        recv_ref[my_slot] = b_ref[...]
        copies = []
        for mask in range(1, N_DEV):
            peer = (x ^ (mask >> 2), y ^ ((mask >> 1) & 1), c ^ (mask & 1))
            copies.append(pltpu.make_async_remote_copy(
                src_ref=b_ref, dst_ref=recv_ref.at[my_slot],
                send_sem=send_sems.at[mask - 1], recv_sem=recv_sems.at[mask - 1],
                device_id=peer, device_id_type=MESH))
        for cp in copies:
            cp.start()
        for cp in copies:
            cp.wait()
        acc = recv_ref[0]
        for k in range(1, N_DEV):
            acc = acc + recv_ref[k]
        o_ref[...] = acc

    return pl.pallas_call(
        body, name="small_all_reduce",
        in_specs=[pl.BlockSpec(memory_space=pltpu.VMEM)], out_specs=pl.BlockSpec(memory_space=pltpu.VMEM),
        out_shape=jax.ShapeDtypeStruct((R, LANES), F32),
        scratch_shapes=[pltpu.VMEM((N_DEV, R, LANES), F32), pltpu.SemaphoreType.DMA((7,)),
                        pltpu.SemaphoreType.DMA((7,))],
        compiler_params=pltpu.CompilerParams(vmem_limit_bytes=VMEM_LIMIT),
    )(buf)


def _pack(arrays):
    flat = jnp.concatenate([a.reshape(-1) for a in arrays])
    pad = (-flat.shape[0]) % (8 * LANES)
    return jnp.pad(flat, (0, pad)).reshape(-1, LANES)


def _unpack(buf, shapes):
    flat = buf.reshape(-1)
    out, off = [], 0
    for s in shapes:
        n = 1
        for d in s:
            n *= d
        out.append(flat[off:off + n].reshape(s))
        off += n
    return out


def _ffn_index(layer, second):
    return (2 * layer + second) * 3


def kernel(x, g_ffn1, w_ffn1_gate, w_ffn1_up, w_ffn1_down, g_mix, w_in_ab, conv_w, conv_b, ln_a_g, ln_a_b, ln_v_g, ln_v_b, sp_w, sp_b, w_out_ab, w_qkv, w_o, g_ffn2, w_ffn2_gate, w_ffn2_up, w_ffn2_down, g_final, loss_target, m_g_ffn1, m_w_ffn1_gate, m_w_ffn1_up, m_w_ffn1_down, m_g_mix, m_w_in_ab, m_conv_w, m_conv_b, m_ln_a_g, m_ln_a_b, m_ln_v_g, m_ln_v_b, m_sp_w, m_sp_b, m_w_out_ab, m_w_qkv, m_w_o, m_g_ffn2, m_w_ffn2_gate, m_w_ffn2_up, m_w_ffn2_down, m_g_final, v_g_ffn1, v_w_ffn1_gate, v_w_ffn1_up, v_w_ffn1_down, v_g_mix, v_w_in_ab, v_conv_w, v_conv_b, v_ln_a_g, v_ln_a_b, v_ln_v_g, v_ln_v_b, v_sp_w, v_sp_b, v_w_out_ab, v_w_qkv, v_w_o, v_g_ffn2, v_w_ffn2_gate, v_w_ffn2_up, v_w_ffn2_down, v_g_final):
    n_seq, S, D = x.shape
    T = n_seq * S
    depth = g_ffn1.shape[0]
    assert depth == 2 and D == D_MODEL
    my_block = 4 * lax.axis_index("x") + 2 * lax.axis_index("y") + lax.axis_index("c")

    ffn_parts = []
    for l in range(depth):
        for gate, up, down in ((w_ffn1_gate, w_ffn1_up, w_ffn1_down), (w_ffn2_gate, w_ffn2_up, w_ffn2_down)):
            ffn_parts += [gate[l].T, up[l].T, down[l]]
    ffn_shard = lambda k: jnp.stack(ffn_parts[3 * k:3 * k + 3]).astype(BF16)
    conv_w_pad = jnp.zeros((HALO, conv_w.shape[2]), F32).at[:CONV_WIDTH].set(conv_w[0]).T
    w_ffn = [None] * (2 * depth)
    w_ffn[0], conv_w_t = _all_gather([ffn_shard(0), conv_w_pad])
    conv_w_full = conv_w_t.T[:CONV_WIDTH]
    shards_b = [w_out_ab[0].astype(BF16), ffn_shard(1)]
    shards_d = [w_qkv[0].T.astype(BF16), w_o[0].astype(BF16), ffn_shard(3)]
    gather_a, token = _split_start(True, [w_in_ab[0].T.astype(BF16)], "gather_a_start", dep=conv_w_t)
    gather_b, token = _split_start(True, shards_b, "gather_b_start", dep=token)
    gather_c, token = _split_start(True, [ffn_shard(2)], "gather_c_start", dep=token)
    gather_d, token = _split_start(True, shards_d, "gather_d_start", dep=token)

    def gathered(handle, after, name):
        return _split_wait(handle, after, name)[1]

    row = lambda a: a.reshape(1, -1)
    tril = jnp.tril(jnp.ones((CHUNK, CHUNK), dtype=bool))
    ws = jnp.where(tril[None], sp_w[0], 0.0).astype(BF16)
    ws_t = jnp.swapaxes(ws, 1, 2)
    bias2d = jnp.repeat(sp_b[0].T, DB, axis=1)
    conv_b2, lag, lab = row(conv_b[0]), row(ln_a_g[0]), row(ln_a_b[0])
    lvg, lvb = row(ln_v_g[0]), row(ln_v_b[0])

    x0 = x.reshape(T, D)
    target = loss_target.reshape(T, D)
    saved = []
    xc = x0
    for l in range(depth):
        xa, a1, b1, h1 = _ffn_fwd(xc, row(g_ffn1[l]), w_ffn[2 * l], 0, f"ffn1_fwd_{l}", dep=token)
        if l % 2 == 0:
            w_in_t, = gathered(gather_a, xa, "gather_a_wait")
            z, hm = _mm_nt(xa, w_in_t, row(g_mix[l]), F32, "mix_in_proj")
            cat, conv_out = _mix_fwd(z, conv_w_full, conv_b2, lag, lab, lvg, lvb, ws, bias2d, n_seq)
            w_out, w_ffn[1] = gathered(gather_b, cat, "gather_b_wait")
            xb = _mm_nn_res(cat, w_out, xa, "mix_out_proj")
            mixer = (z, hm, cat, conv_out)
        else:
            w_qkv_t, w_o_full, w_ffn[3] = gathered(gather_d, xa, "gather_d_wait")
            qkv, hm = _mm_nt(xa, w_qkv_t, row(g_mix[l]), BF16, "qkv_proj")
            o, att = _attn_fwd(qkv, n_seq)
            xb = _mm_nn_res(o, w_o_full, xa, "attn_out_proj")
            mixer = (qkv, hm, o, att)
        xn, a2, b2, h2 = _ffn_fwd(xb, row(g_ffn2[l]), w_ffn[2 * l + 1], 0, f"ffn2_fwd_{l}")
        saved.append((xc, a1, b1, h1, xa, mixer, xb, a2, b2, h2))
        xc = xn
        if l == 0:
            w_ffn[2], = gathered(gather_c, xc, "gather_c_wait")

    g, loss_part, dg_final = _loss_head(xc, row(g_final), target)

    dg_ffn1, dg_ffn2, dg_mix = [None] * depth, [None] * depth, [None] * depth
    exchanges = {}
    token = None

    def ffn_back(g, xin, gvec, a, b, h, k, tag, token):
        g, dg, da, db, s, gh = _ffn_bwd(g, xin, gvec, a, b, w_ffn[k], 0, f"ffn{tag}_bwd", dep=token)
        if k == 0:
            return g, dg, (da, db, s, gh, h)
        dws = [_mm_tn(da, h, f"dw_gate{tag}"), _mm_tn(db, h, f"dw_up{tag}"), _mm_tn(s, gh, f"dw_down{tag}")]
        exchanges[f"ffn{k}"], token = _split_start(False, dws, f"exchange_ffn{tag}_start")
        return g, dg, token

    for l in reversed(range(depth)):
        xin, a1, b1, h1, xa, mixer, xb, a2, b2, h2 = saved[l]
        g, dg_ffn2[l], token = ffn_back(g, xb, row(g_ffn2[l]), a2, b2, h2, 2 * l + 1, f"2_{l}", token)
        if l % 2 == 0:
            z, hm, cat, conv_out = mixer
            dcat = _mm_nt(g, w_out, None, F32, "mix_out_bwd", dep=token)
            d_w_out = _mm_tn(cat, g, "dw_out")
            (da1, dz_uv, d_lag, d_lab, d_lvg, d_lvb, d_ws, d_sb) = _mix_bwd_rows(
                dcat, z, conv_out, lag, lab, lvg, lvb, ws, ws_t, bias2d, n_seq)
            dz_a, d_cw, d_cb = _mix_bwd_conv(da1, z, conv_w_full, n_seq)
            d_w_in_t = jnp.concatenate([_mm_tn(dz_a, hm, "dw_in_a"), _mm_tn(dz_uv, hm, "dw_in_uv")])
            exchanges["mix"], token = _split_start(False, [d_w_out, d_w_in_t], "exchange_mix_start")
            g, dg_mix[l] = _mm_nn_rmsbwd([dz_a, dz_uv], w_in_t, xa, row(g_mix[l]), g, "mix_in_bwd", dep=token)
        else:
            qkv, hm, o, att = mixer
            do = _mm_nt(g, w_o_full, None, BF16, "attn_out_bwd", dep=token)
            d_w_o = _mm_tn(o, g, "dw_o")
            dq, dk, dv = _attn_bwd(qkv, do, att, n_seq)
            d_w_qkv_t = jnp.concatenate([_mm_tn(dq, hm, "dw_q"), _mm_tn(dk, hm, "dw_k"), _mm_tn(dv, hm, "dw_v")])
            exchanges["attn"], token = _split_start(False, [d_w_o, d_w_qkv_t], "exchange_attn_start")
            g, dg_mix[l] = _mm_nn_rmsbwd([dq, dk, dv], w_qkv_t, xa, row(g_mix[l]), g, "qkv_bwd", dep=token)
        g, dg_ffn1[l], token = ffn_back(g, xin, row(g_ffn1[l]), a1, b1, h1, 2 * l, f"1_{l}", token)
    grad_x = g.reshape(n_seq, S, D)

    small = [jnp.concatenate(dg_ffn1), jnp.concatenate(dg_mix), d_cw, d_cb, d_lag, d_lab, d_lvg, d_lvb,
             jnp.where(tril[None], d_ws, 0.0), d_sb.T, jnp.concatenate(dg_ffn2), dg_final, loss_part[:, :1]]
    small_shapes = [(depth, D), (depth, D), (CONV_WIDTH, CA), (1, CA), (1, CA), (1, CA), (1, GB, DB), (1, GB, DB),
                    (1, GB, CHUNK, CHUNK), (1, GB, CHUNK), (depth, D), (D,), ()]
    small_sum = _small_all_reduce(_pack(small))
    red = _unpack(small_sum, small_shapes)
    (gr_g_ffn1, gr_g_mix, gr_cw_full, gr_cb, gr_lag, gr_lab, gr_lvg, gr_lvb, gr_sp_w, gr_sp_b,
     gr_g_ffn2, gr_g_final, loss) = red
    n_cw = conv_w.shape[2]
    gr_cw = lax.dynamic_slice(gr_cw_full, (0, my_block * n_cw), (CONV_WIDTH, n_cw))[None]

    da, db, s, gh, h = token
    token = small_sum
    for which, lhs, rhs in ((2, s, gh), (1, db, h), (0, da, h)):
        dw = _mm_tn(lhs, rhs, f"dw_ffn0_{which}", dep=token, out_dtype=BF16)
        exchanges[f"ffn0_{which}"], token = _split_start(False, [dw], f"exchange_ffn0_{which}_start")

    def landed(key, after):
        return _split_wait(exchanges[key], after, f"exchange_{key}_wait")[1]

    parts_ffn = [None] * (6 * depth)
    for k in range(1, 2 * depth):
        parts_ffn[3 * k:3 * k + 3] = landed(f"ffn{k}", token)
    parts_out, parts_in = landed("mix", token)
    parts_o, parts_qkv = landed("attn", token)

    grads = {
        "g_ffn1": gr_g_ffn1, "g_mix": gr_g_mix, "conv_w": gr_cw, "conv_b": gr_cb, "ln_a_g": gr_lag,
        "ln_a_b": gr_lab, "ln_v_g": gr_lvg, "ln_v_b": gr_lvb, "sp_w": gr_sp_w, "sp_b": gr_sp_b,
        "g_ffn2": gr_g_ffn2, "g_final": gr_g_final,
    }
    weights = dict(g_ffn1=g_ffn1, w_ffn1_gate=w_ffn1_gate, w_ffn1_up=w_ffn1_up, w_ffn1_down=w_ffn1_down, g_mix=g_mix,
                   w_in_ab=w_in_ab, conv_w=conv_w, conv_b=conv_b, ln_a_g=ln_a_g, ln_a_b=ln_a_b, ln_v_g=ln_v_g,
                   ln_v_b=ln_v_b, sp_w=sp_w, sp_b=sp_b, w_out_ab=w_out_ab, w_qkv=w_qkv, w_o=w_o, g_ffn2=g_ffn2,
                   w_ffn2_gate=w_ffn2_gate, w_ffn2_up=w_ffn2_up, w_ffn2_down=w_ffn2_down, g_final=g_final)
    m_in = dict(g_ffn1=m_g_ffn1, w_ffn1_gate=m_w_ffn1_gate, w_ffn1_up=m_w_ffn1_up, w_ffn1_down=m_w_ffn1_down,
                g_mix=m_g_mix, w_in_ab=m_w_in_ab, conv_w=m_conv_w, conv_b=m_conv_b, ln_a_g=m_ln_a_g, ln_a_b=m_ln_a_b,
                ln_v_g=m_ln_v_g, ln_v_b=m_ln_v_b, sp_w=m_sp_w, sp_b=m_sp_b, w_out_ab=m_w_out_ab, w_qkv=m_w_qkv,
                w_o=m_w_o, g_ffn2=m_g_ffn2, w_ffn2_gate=m_w_ffn2_gate, w_ffn2_up=m_w_ffn2_up,
                w_ffn2_down=m_w_ffn2_down, g_final=m_g_final)
    v_in = dict(g_ffn1=v_g_ffn1, w_ffn1_gate=v_w_ffn1_gate, w_ffn1_up=v_w_ffn1_up, w_ffn1_down=v_w_ffn1_down,
                g_mix=v_g_mix, w_in_ab=v_w_in_ab, conv_w=v_conv_w, conv_b=v_conv_b, ln_a_g=v_ln_a_g, ln_a_b=v_ln_a_b,
                ln_v_g=v_ln_v_g, ln_v_b=v_ln_v_b, sp_w=v_sp_w, sp_b=v_sp_b, w_out_ab=v_w_out_ab, w_qkv=v_w_qkv,
                w_o=v_w_o, g_ffn2=v_g_ffn2, w_ffn2_gate=v_w_ffn2_gate, w_ffn2_up=v_w_ffn2_up,
                w_ffn2_down=v_w_ffn2_down, g_final=v_g_final)
    names = list(weights)
    grads = {n: grads[n].reshape(weights[n].shape) for n in grads}

    delta, new_m, new_v = {}, {}, {}

    def adamw_big(n, parts):
        if weights[n].shape[-1] == D:
            view = back = lambda a: a
        else:
            view = back = lambda a: jnp.swapaxes(a, 1, 2)
        out = _adamw_parts(parts, view(weights[n]), view(m_in[n]), view(v_in[n]), f"adamw_{n}")
        grads[n], delta[n], new_m[n], new_v[n] = [back(a) for a in out]

    adamw_big("w_in_ab", [parts_in])
    adamw_big("w_out_ab", [parts_out])
    adamw_big("w_qkv", [parts_qkv])
    adamw_big("w_o", [parts_o])
    kinds = ("gate", "up", "down")
    for which, kind in enumerate(kinds):
        adamw_big(f"w_ffn2_{kind}", [parts_ffn[_ffn_index(l, 1) + which] for l in range(depth)])
    big = [n for n in names if n.startswith("w_")]
    after = jnp.concatenate([delta[n].reshape(-1)[:1] for n in big if n in delta]).reshape(1, -1)
    for which in (2, 1, 0):
        parts_ffn[which], = landed(f"ffn0_{which}", after)
    for which, kind in enumerate(kinds):
        adamw_big(f"w_ffn1_{kind}", [parts_ffn[_ffn_index(l, 0) + which] for l in range(depth)])
    little = [n for n in names if n not in big]
    shapes = [weights[n].shape for n in little]
    d, nm, nv = _adamw(_pack([weights[n] for n in little]), _pack([grads[n] for n in little]),
                       _pack([m_in[n] for n in little]), _pack([v_in[n] for n in little]), "adamw_small")
    for n, dd, mm, vv in zip(little, _unpack(d, shapes), _unpack(nm, shapes), _unpack(nv, shapes)):
        delta[n], new_m[n], new_v[n] = dd, mm, vv

    return (loss, grad_x, *[grads[n] for n in names], *[delta[n] for n in names],
            *[new_m[n] for n in names], *[new_v[n] for n in names])
```

```python
import functools

import jax
import jax.numpy as jnp
from jax import lax
from jax.experimental import pallas as pl
from jax.experimental.pallas import tpu as pltpu

F32 = jnp.float32
BF16 = jnp.bfloat16

D_MODEL = 1024
CA = 512
CB = 512
GB = 4
DB = 128
CHUNK = 128
CONV_WIDTH = 31
N_HEADS = 16
HEAD_DIM = 64
EPS = 1e-6
N_DEV = 8
LANES = 128
SUBLANES = 8
QB = 128
ATT_TQ = 1024
FFN_TN = 2816
FFN_TM = 256
HALO = 32
CONV_ROWS = 32
ATT_SCALE = HEAD_DIM ** -0.5

ADAM_LR = 0.001
ADAM_B1 = 0.9
ADAM_B2 = 0.999
ADAM_EPS = 1e-08
ADAM_WD = 0.01
ADAM_STEP = 10

NT = (((1,), (1,)), ((), ()))
NN = (((1,), (0,)), ((), ()))
TN = (((0,), (0,)), ((), ()))
MESH = pl.DeviceIdType.MESH
ANY = pl.BlockSpec(memory_space=pl.ANY)
VMEM_LIMIT = 60 * 1024 * 1024


def _dot(a, b, dims):
    return lax.dot_general(a, b, dims, preferred_element_type=F32)


def _cp(*sem):
    return pltpu.CompilerParams(dimension_semantics=sem, vmem_limit_bytes=VMEM_LIMIT)


def _pcall(body, *, in_specs, args, dep=None, **kw):
    if dep is not None:
        n_in = len(in_specs)
        inner = body

        def body(*refs):
            inner(*refs[:n_in], *refs[n_in + 1:])

        in_specs = list(in_specs) + [ANY]
        args = tuple(args) + (dep,)
    return pl.pallas_call(body, in_specs=list(in_specs), **kw)(*args)


def _tile(n, want):
    if n <= want:
        return n
    t = want - want % LANES
    while t > LANES and n % t:
        t -= LANES
    assert n % t == 0, (n, want)
    return t


def _sigmoid(x):
    return 0.5 * jnp.tanh(0.5 * x) + 0.5


def _rstd(x):
    return lax.rsqrt(jnp.mean(x * x, axis=-1, keepdims=True) + EPS)


def _rms_bwd(x, g, dh):
    r = _rstd(x)
    u = dh * g
    dx = r * (u - x * (r * r) * jnp.mean(u * x, axis=-1, keepdims=True))
    dg = jnp.sum(dh * x * r, axis=0, keepdims=True)
    return dx, dg


def _ln_fwd(x, g, b):
    mu = jnp.mean(x, axis=-1, keepdims=True)
    xc = x - mu
    r = lax.rsqrt(jnp.mean(xc * xc, axis=-1, keepdims=True) + EPS)
    xh = xc * r
    return xh * g + b, xh, r


def _ln_bwd(dy, xh, r, g):
    dxh = dy * g
    return r * (dxh - jnp.mean(dxh, axis=-1, keepdims=True)
                - xh * jnp.mean(dxh * xh, axis=-1, keepdims=True))


def _ffn_fwd(x, g, wall, base, name, dep=None):
    T, D = x.shape
    F = wall.shape[1]
    tm, tn = _tile(T, FFN_TM), _tile(F, FFN_TN)
    n_j = F // tn

    def body(x_ref, g_ref, wg_ref, wu_ref, wd_ref, xo_ref, a_ref, b_ref, h_ref, acc_ref):
        j = pl.program_id(1)

        @pl.when(j == 0)
        def _():
            xv = x_ref[...]
            h_ref[...] = (xv * _rstd(xv) * g_ref[...]).astype(BF16)
            acc_ref[...] = jnp.zeros_like(acc_ref)

        h = h_ref[...]
        a = _dot(h, wg_ref[...], NT)
        b = _dot(h, wu_ref[...], NT)
        a_ref[...] = a.astype(BF16)
        b_ref[...] = b.astype(BF16)
        s = (a * _sigmoid(a) * b).astype(BF16)
        acc_ref[...] += _dot(s, wd_ref[...], NN)

        @pl.when(j == n_j - 1)
        def _():
            xo_ref[...] = x_ref[...] + 0.5 * acc_ref[...]

    single = pl.Buffered(1) if n_j == 1 else None
    wspec = lambda k: pl.BlockSpec((None, tn, D), lambda i, j: (base + k, j, 0), pipeline_mode=single)
    return _pcall(
        body, name=name, grid=(T // tm, n_j), dep=dep, args=(x, g, wall, wall, wall),
        in_specs=[pl.BlockSpec((tm, D), lambda i, j: (i, 0)), pl.BlockSpec((1, D), lambda i, j: (0, 0)),
                  wspec(0), wspec(1), wspec(2)],
        out_specs=[pl.BlockSpec((tm, D), lambda i, j: (i, 0)), pl.BlockSpec((tm, tn), lambda i, j: (i, j)),
                   pl.BlockSpec((tm, tn), lambda i, j: (i, j)), pl.BlockSpec((tm, D), lambda i, j: (i, 0))],
        out_shape=[jax.ShapeDtypeStruct((T, D), F32), jax.ShapeDtypeStruct((T, F), BF16),
                   jax.ShapeDtypeStruct((T, F), BF16), jax.ShapeDtypeStruct((T, D), BF16)],
        scratch_shapes=[pltpu.VMEM((tm, D), F32)],
        compiler_params=_cp("parallel", "arbitrary"),
    )


def _ffn_bwd(go, x, g, a, b, wall, base, name, dep=None):
    T, D = x.shape
    F = wall.shape[1]
    tm, tn = _tile(T, FFN_TM), _tile(F, FFN_TN)
    n_j = F // tn

    def body(go_ref, x_ref, g_ref, a_ref, b_ref, wg_ref, wu_ref, wd_ref,
             gx_ref, dg_ref, da_ref, db_ref, s_ref, gh_ref, acc_ref):
        i, j = pl.program_id(0), pl.program_id(1)

        @pl.when(j == 0)
        def _():
            gh_ref[...] = (0.5 * go_ref[...]).astype(BF16)
            acc_ref[...] = jnp.zeros_like(acc_ref)

        @pl.when((i == 0) & (j == 0))
        def _():
            dg_ref[...] = jnp.zeros_like(dg_ref)

        ds = _dot(gh_ref[...], wd_ref[...], NT)
        av = a_ref[...].astype(F32)
        bv = b_ref[...].astype(F32)
        sig = _sigmoid(av)
        sl = av * sig
        dab = ((ds * bv) * (sig + sl * (1.0 - sig))).astype(BF16)
        dbb = (ds * sl).astype(BF16)
        s_ref[...] = (sl * bv).astype(BF16)
        da_ref[...] = dab
        db_ref[...] = dbb
        acc_ref[...] += _dot(dab, wg_ref[...], NN) + _dot(dbb, wu_ref[...], NN)

        @pl.when(j == n_j - 1)
        def _():
            dx, dg = _rms_bwd(x_ref[...], g_ref[...], acc_ref[...])
            gx_ref[...] = go_ref[...] + dx
            dg_ref[...] += dg

    single = pl.Buffered(1) if n_j == 1 else None
    wspec = lambda k: pl.BlockSpec((None, tn, D), lambda i, j: (base + k, j, 0), pipeline_mode=single)
    row = pl.BlockSpec((tm, D), lambda i, j: (i, 0))
    hid = pl.BlockSpec((tm, tn), lambda i, j: (i, j))
    vec = pl.BlockSpec((1, D), lambda i, j: (0, 0))
    return _pcall(
        body, name=name, grid=(T // tm, n_j), dep=dep, args=(go, x, g, a, b, wall, wall, wall),
        in_specs=[row, row, vec, hid, hid, wspec(0), wspec(1), wspec(2)],
        out_specs=[row, vec, hid, hid, hid, row],
        out_shape=[jax.ShapeDtypeStruct((T, D), F32), jax.ShapeDtypeStruct((1, D), F32),
                   jax.ShapeDtypeStruct((T, F), BF16), jax.ShapeDtypeStruct((T, F), BF16),
                   jax.ShapeDtypeStruct((T, F), BF16), jax.ShapeDtypeStruct((T, D), BF16)],
        scratch_shapes=[pltpu.VMEM((tm, D), F32)],
        compiler_params=_cp("arbitrary", "arbitrary"),
    )


def _mm_tn(a, b, name, dep=None, out_dtype=F32):
    T, M = a.shape
    N = b.shape[1]
    tmm, tk = _tile(M, 1536), _tile(T, 2048)
    n_k = T // tk
    narrow = out_dtype != F32

    def body(a_ref, b_ref, o_ref, *scratch):
        acc_ref = scratch[0] if narrow else o_ref

        @pl.when(pl.program_id(1) == 0)
        def _():
            acc_ref[...] = jnp.zeros_like(acc_ref)

        acc_ref[...] += _dot(a_ref[...].astype(BF16), b_ref[...].astype(BF16), TN)
        if narrow:
            @pl.when(pl.program_id(1) == n_k - 1)
            def _():
                o_ref[...] = acc_ref[...].astype(out_dtype)

    return _pcall(
        body, name=name, grid=(M // tmm, n_k), dep=dep, args=(a, b),
        in_specs=[pl.BlockSpec((tk, tmm), lambda m, k: (k, m)), pl.BlockSpec((tk, N), lambda m, k: (k, 0))],
        out_specs=pl.BlockSpec((tmm, N), lambda m, k: (m, 0)),
        out_shape=jax.ShapeDtypeStruct((M, N), out_dtype),
        scratch_shapes=[pltpu.VMEM((tmm, N), F32)] if narrow else [],
        compiler_params=_cp("parallel", "arbitrary"),
    )


def _mm_nt(x, wt, g, out_dtype, name, dep=None):
    T, K = x.shape
    N = wt.shape[0]
    tm, tn = _tile(T, 512), N
    norm = g is not None

    def body(*refs):
        if norm:
            x_ref, g_ref, w_ref, o_ref, h_ref = refs
        else:
            x_ref, w_ref, o_ref, h_ref = refs

        @pl.when(pl.program_id(1) == 0)
        def _():
            xv = x_ref[...].astype(F32)
            if norm:
                xv = xv * _rstd(xv) * g_ref[...]
            h_ref[...] = xv.astype(BF16)

        o_ref[...] = _dot(h_ref[...], w_ref[...], NT).astype(out_dtype)

    row = pl.BlockSpec((tm, K), lambda i, j: (i, 0))
    wsp = pl.BlockSpec((tn, K), lambda i, j: (j, 0))
    osp = pl.BlockSpec((tm, tn), lambda i, j: (i, j))
    if norm:
        return pl.pallas_call(
            body, name=name, grid=(T // tm, N // tn),
            in_specs=[row, pl.BlockSpec((1, K), lambda i, j: (0, 0)), wsp],
            out_specs=[osp, row],
            out_shape=[jax.ShapeDtypeStruct((T, N), out_dtype), jax.ShapeDtypeStruct((T, K), BF16)],
            compiler_params=_cp("parallel", "arbitrary"),
        )(x, g, wt)
    return _pcall(
        body, name=name, grid=(T // tm, N // tn), dep=dep, args=(x, wt),
        in_specs=[row, wsp], out_specs=osp,
        out_shape=jax.ShapeDtypeStruct((T, N), out_dtype),
        scratch_shapes=[pltpu.VMEM((tm, K), BF16)],
        compiler_params=_cp("parallel", "arbitrary"),
    )


def _mm_nn_res(act, w, resid, name):
    T, K = act.shape
    D = w.shape[1]
    tm = _tile(T, 512)

    def body(a_ref, w_ref, r_ref, o_ref):
        o_ref[...] = r_ref[...] + _dot(a_ref[...].astype(BF16), w_ref[...], NN)

    return pl.pallas_call(
        body, name=name, grid=(T // tm,),
        in_specs=[pl.BlockSpec((tm, K), lambda i: (i, 0)), pl.BlockSpec((K, D), lambda i: (0, 0)),
                  pl.BlockSpec((tm, D), lambda i: (i, 0))],
        out_specs=pl.BlockSpec((tm, D), lambda i: (i, 0)),
        out_shape=jax.ShapeDtypeStruct((T, D), F32),
        compiler_params=_cp("parallel"),
    )(act, w, resid)


def _mm_nn_rmsbwd(acts, w, x, g, gprev, name, dep=None):
    T = acts[0].shape[0]
    ks = [a.shape[1] for a in acts]
    K, D = w.shape
    assert sum(ks) == K
    tm = _tile(T, 512)
    na = len(acts)

    def body(*refs):
        a_refs = refs[:na]
        w_ref, x_ref, g_ref, gp_ref, o_ref, dg_ref = refs[na:]

        @pl.when(pl.program_id(0) == 0)
        def _():
            dg_ref[...] = jnp.zeros_like(dg_ref)

        dh, off = None, 0
        for a_ref, k in zip(a_refs, ks):
            part = _dot(a_ref[...].astype(BF16), w_ref[off:off + k, :], NN)
            dh = part if dh is None else dh + part
            off += k
        dx, dg = _rms_bwd(x_ref[...], g_ref[...], dh)
        o_ref[...] = gp_ref[...] + dx
        dg_ref[...] += dg

    row = pl.BlockSpec((tm, D), lambda i: (i, 0))
    vec = pl.BlockSpec((1, D), lambda i: (0, 0))
    return _pcall(
        body, name=name, grid=(T // tm,), dep=dep, args=(*acts, w, x, g, gprev),
        in_specs=[pl.BlockSpec((tm, k), lambda i: (i, 0)) for k in ks]
        + [pl.BlockSpec((K, D), lambda i: (0, 0)), row, vec, row],
        out_specs=[row, vec],
        out_shape=[jax.ShapeDtypeStruct((T, D), F32), jax.ShapeDtypeStruct((1, D), F32)],
        compiler_params=_cp("arbitrary"),
    )


def _loss_head(x, g, target):
    T, D = x.shape
    tm = _tile(T, 512)

    def body(x_ref, g_ref, t_ref, dx_ref, loss_ref, dg_ref):
        @pl.when(pl.program_id(0) == 0)
        def _():
            loss_ref[...] = jnp.zeros_like(loss_ref)
            dg_ref[...] = jnp.zeros_like(dg_ref)

        xv = x_ref[...]
        gv = g_ref[...]
        e = xv * _rstd(xv) * gv - t_ref[...]
        per_tok = jnp.sum(e * e, axis=-1, keepdims=True) * (1.0 / D)
        loss_ref[...] += 0.5 * jnp.sum(per_tok, axis=0, keepdims=True)
        dx, dg = _rms_bwd(xv, gv, e * (1.0 / D))
        dx_ref[...] = dx
        dg_ref[...] += dg

    row = pl.BlockSpec((tm, D), lambda i: (i, 0))
    vec = pl.BlockSpec((1, D), lambda i: (0, 0))
    return pl.pallas_call(
        body, name="loss_head", grid=(T // tm,),
        in_specs=[row, vec, row],
        out_specs=[row, pl.BlockSpec((1, LANES), lambda i: (0, 0)), vec],
        out_shape=[jax.ShapeDtypeStruct((T, D), F32), jax.ShapeDtypeStruct((1, LANES), F32),
                   jax.ShapeDtypeStruct((1, D), F32)],
        compiler_params=_cp("arbitrary"),
    )(x, g, target)


def _log_gates(z):
    ls = jnp.minimum(z, 0.0) - jnp.log(1.0 + jnp.exp(-jnp.abs(z)))
    return ls, ls - z


def _cumsum_mm(v, u2):
    hi = v.astype(BF16)
    lo = (v - hi.astype(F32)).astype(BF16)
    return _dot(jnp.concatenate([hi, lo], axis=1), u2, NN)


def _half_rowsum(v):
    n = v.shape[0]
    s0 = jnp.sum(v[:, :QB], axis=1, keepdims=True)
    s1 = jnp.sum(v[:, QB:], axis=1, keepdims=True)
    return jnp.concatenate([jnp.broadcast_to(s0, (n, QB)), jnp.broadcast_to(s1, (n, QB))], axis=1)


def _stack_heads(src_ref, dst_ref, n_blk):
    m0 = lax.broadcasted_iota(jnp.int32, (1, LANES), 1) < HEAD_DIM

    def fill(c, carry):
        blk = src_ref[pl.ds(pl.multiple_of(c * QB, QB), QB), :]
        zero = jnp.zeros_like(blk)
        dst_ref[c, 0:QB, :] = jnp.where(m0, blk, zero)
        dst_ref[c, QB:2 * QB, :] = jnp.where(m0, zero, blk)
        return carry

    lax.fori_loop(0, n_blk, fill, 0)


def _diag_mask(tq, j):
    n = tq - j * QB
    row = lax.broadcasted_iota(jnp.int32, (n, 2 * QB), 0)
    col = lax.broadcasted_iota(jnp.int32, (n, 2 * QB), 1)
    return (col & (QB - 1)) < row


def _tri_blockdiag(upper):
    r = lax.broadcasted_iota(jnp.int32, (2 * QB, 2 * QB), 0)
    c = lax.broadcasted_iota(jnp.int32, (2 * QB, 2 * QB), 1)
    same = (r // QB) == (c // QB)
    u = (same & ((r > c) if upper else (r < c))).astype(BF16)
    return jnp.concatenate([u, u], axis=0)


def _attn_tiles(T, n_seq):
    S = T // n_seq
    tq = ATT_TQ if S % ATT_TQ == 0 else 2 * QB
    assert S % tq == 0
    return S, tq, tq // QB, S // tq, S // QB


def _attn_fwd(qkv, n_seq):
    T = qkv.shape[0]
    S, tq, r, n_q, n_k = _attn_tiles(T, n_seq)
    n_p = D_MODEL // LANES
    u_suffix = _tri_blockdiag(True)

    def body(q_ref, k_ref, v_ref, u_ref, o_ref, a_hbm, kk_ref, vv_ref, lr_s, acc_s, a_stage, sems):
        qi = pl.program_id(2)
        group = (pl.program_id(0) * n_p + pl.program_id(1)) * n_q + qi

        @pl.when(qi == 0)
        def _():
            _stack_heads(k_ref, kk_ref, n_k)
            _stack_heads(v_ref, vv_ref, n_k)

        u = u_ref[...]
        lr_s[...] = jnp.zeros_like(lr_s)
        acc_s[...] = jnp.zeros_like(acc_s)
        a_stage[0] = jnp.zeros_like(a_stage[0])

        def saves(first_kj, half):
            return [pltpu.make_async_copy(a_stage.at[half, j], a_hbm.at[group, first_kj - j], sems.at[half])
                    for j in range(r)]

        def step(kj, half, j, rows, q, mask, lr, acc):
            ls, lk = _log_gates(_dot(q, kk_ref[kj], NT))
            if mask is not None:
                lk = jnp.where(mask, lk, 0.0)
            a = jnp.exp(ls + _cumsum_mm(lk, u) + lr)
            if mask is not None:
                a = jnp.where(mask, a, 0.0)
            a = a.astype(BF16)
            a_stage[half, j, rows, :] = a
            return lr + _half_rowsum(lk), acc + _dot(a, vv_ref[kj], NN)

        last = (qi + 1) * r - 1
        for n in range(r):
            rows = slice((r - 1 - n) * QB, tq)
            lr, acc = step(last - n, 0, n, rows, q_ref[rows, :] * ATT_SCALE, _diag_mask(tq, r - 1 - n),
                           lr_s[rows, :], acc_s[rows, :])
            lr_s[rows, :] = lr
            acc_s[rows, :] = acc
        for cp in saves(last, 0):
            cp.start()

        q = q_ref[...] * ATT_SCALE

        def off(it, carry):
            half = (it + 1) % 2
            first = (qi - it) * r - 1

            @pl.when(it >= 1)
            def _():
                for cp in saves(first, half):
                    cp.wait()

            lr, acc = lr_s[...], acc_s[...]
            for j in range(r):
                lr, acc = step(first - j, half, j, slice(0, tq), q, None, lr, acc)
            lr_s[...] = lr
            acc_s[...] = acc
            for cp in saves(first, half):
                cp.start()
            return carry

        lax.fori_loop(0, qi, off, 0)
        for cp in saves(last, 0):
            cp.wait()

        @pl.when(qi >= 1)
        def _():
            for cp in saves(last, 1):
                cp.wait()

        o_ref[...] = acc_s[...].astype(BF16)

    return pl.pallas_call(
        body, name="attn_fwd", grid=(n_seq, n_p, n_q),
        in_specs=[pl.BlockSpec((tq, LANES), lambda b, p, qi: (b * n_q + qi, p)),
                  pl.BlockSpec((S, LANES), lambda b, p, qi: (b, n_p + p)),
                  pl.BlockSpec((S, LANES), lambda b, p, qi: (b, 2 * n_p + p)),
                  pl.BlockSpec((4 * QB, 2 * QB), lambda b, p, qi: (0, 0))],
        out_specs=[pl.BlockSpec((tq, LANES), lambda b, p, qi: (b * n_q + qi, p)), ANY],
        out_shape=[jax.ShapeDtypeStruct((T, D_MODEL), BF16),
                   jax.ShapeDtypeStruct((n_seq * n_p * n_q, n_k, tq, 2 * QB), BF16)],
        scratch_shapes=[pltpu.VMEM((n_k, 2 * QB, LANES), BF16), pltpu.VMEM((n_k, 2 * QB, LANES), BF16),
                        pltpu.VMEM((tq, 2 * QB), F32), pltpu.VMEM((tq, LANES), F32),
                        pltpu.VMEM((2, r, tq, 2 * QB), BF16), pltpu.SemaphoreType.DMA((2,))],
        compiler_params=_cp("parallel", "parallel", "arbitrary"),
    )(qkv, qkv, qkv, u_suffix)


def _attn_bwd(qkv, do, a_saved, n_seq):
    T = qkv.shape[0]
    S, tq, r, n_q, n_k = _attn_tiles(T, n_seq)
    n_p = D_MODEL // LANES
    u_prefix = _tri_blockdiag(False)[:2 * QB]

    def body(q_ref, k_ref, v_ref, do_ref, u_ref, a_hbm, dq_ref, dk_out, dv_out,
             kk_ref, vv_ref, cg_s, dq_s, dk_ref, dv_ref, a_stage, sems):
        qi = pl.program_id(2)
        group = (pl.program_id(0) * n_p + pl.program_id(1)) * n_q + qi

        def fetches(g, half):
            return [pltpu.make_async_copy(a_hbm.at[group, g * r + j], a_stage.at[half, j], sems.at[half])
                    for j in range(r)]

        for cp in fetches(0, 0):
            cp.start()

        @pl.when(qi == 0)
        def _():
            _stack_heads(k_ref, kk_ref, n_k)
            _stack_heads(v_ref, vv_ref, n_k)
            dk_ref[...] = jnp.zeros_like(dk_ref)
            dv_ref[...] = jnp.zeros_like(dv_ref)

        u = u_ref[...]
        m0 = lax.broadcasted_iota(jnp.int32, (1, LANES), 1) < HEAD_DIM
        cg_s[...] = jnp.zeros_like(cg_s)
        dq_s[...] = jnp.zeros_like(dq_s)

        def step(kj, half, j, rows, q, dov, mask, cg, dq):
            kk = kk_ref[kj]
            beta = _sigmoid(_dot(q, kk, NT))
            a = a_stage[half, j, rows, :]
            g = a.astype(F32) * _dot(dov, vv_ref[kj], NT)
            dz = g - (g + _dot(g.astype(BF16), u, NN) + cg) * beta
            if mask is not None:
                dz = jnp.where(mask, dz, 0.0)
            dz = dz.astype(BF16)
            keys = pl.ds(pl.multiple_of(kj * QB, QB), QB)
            dvt = _dot(a, dov, TN)
            dv_ref[keys, :] += jnp.where(m0, dvt[:QB], dvt[QB:])
            dkt = _dot(dz, q, TN)
            dk_ref[keys, :] += jnp.where(m0, dkt[:QB], dkt[QB:])
            return cg + _half_rowsum(g), dq + _dot(dz, kk, NN)

        q = q_ref[...] * ATT_SCALE
        dov = do_ref[...]

        def off(it, carry):
            half = it % 2
            for cp in fetches(it, half):
                cp.wait()
            for cp in fetches(it + 1, 1 - half):
                cp.start()
            cg, dq = cg_s[...], dq_s[...]
            for j in range(r):
                cg, dq = step(it * r + j, half, j, slice(0, tq), q, dov, None, cg, dq)
            cg_s[...] = cg
            dq_s[...] = dq
            return carry

        lax.fori_loop(0, qi, off, 0)

        for cp in fetches(qi, qi % 2):
            cp.wait()
        for j in range(r):
            rows = slice(j * QB, tq)
            cg, dq = step(qi * r + j, qi % 2, j, rows, q_ref[rows, :] * ATT_SCALE, do_ref[rows, :],
                          _diag_mask(tq, j), cg_s[rows, :], dq_s[rows, :])
            cg_s[rows, :] = cg
            dq_s[rows, :] = dq
        dq_ref[...] = (dq_s[...] * ATT_SCALE).astype(BF16)

        @pl.when(qi == n_q - 1)
        def _():
            dk_out[...] = dk_ref[...].astype(BF16)
            dv_out[...] = dv_ref[...].astype(BF16)

    qspec = pl.BlockSpec((tq, LANES), lambda b, p, qi: (b * n_q + qi, p))
    seq = lambda off: pl.BlockSpec((S, LANES), lambda b, p, qi: (b, off + p))
    return pl.pallas_call(
        body, name="attn_bwd", grid=(n_seq, n_p, n_q),
        in_specs=[qspec, seq(n_p), seq(2 * n_p), qspec,
                  pl.BlockSpec((2 * QB, 2 * QB), lambda b, p, qi: (0, 0)), ANY],
        out_specs=[qspec, seq(0), seq(0)],
        out_shape=[jax.ShapeDtypeStruct((T, D_MODEL), BF16)] * 3,
        scratch_shapes=[pltpu.VMEM((n_k, 2 * QB, LANES), BF16), pltpu.VMEM((n_k, 2 * QB, LANES), BF16),
                        pltpu.VMEM((tq, 2 * QB), F32), pltpu.VMEM((tq, LANES), F32),
                        pltpu.VMEM((S, LANES), F32), pltpu.VMEM((S, LANES), F32),
                        pltpu.VMEM((2, r, tq, 2 * QB), BF16), pltpu.SemaphoreType.DMA((2,))],
        compiler_params=_cp("parallel", "parallel", "arbitrary"),
    )(qkv, qkv, qkv, do, u_prefix, a_saved)


def _shifted_copies(sh_ref):
    rows = sh_ref.shape[1] - SUBLANES
    for s in range(1, SUBLANES):
        sh_ref[s, 0:rows, :] = sh_ref[0, s:s + rows, :]


def _shifted(sh_ref, start, n):
    s = start % SUBLANES
    return sh_ref[s, start - s:start - s + n, :]


def _glu_with_halo(av_ref, ag_ref, avh_ref, agh_ref, a0_s, first, ts):
    hal = avh_ref[...] * _sigmoid(agh_ref[...])
    a0_s[0, 0:HALO, :] = jnp.where(first, 0.0, hal)
    a0_s[0, HALO:HALO + ts, :] = av_ref[...] * _sigmoid(ag_ref[...])
    _shifted_copies(a0_s)


def _mix_specs(ts, n_r, with_left):
    blk = lambda c: pl.BlockSpec((ts, CA), lambda b, r: (b * n_r + r, c))
    per = ts // HALO
    left = lambda c: pl.BlockSpec((HALO, CA), lambda b, r: (jnp.maximum((b * n_r + r) * per - 1, 0), c))
    return blk, (left if with_left else None)


def _mix_fwd(z, conv_w, conv_b, ln_a_g, ln_a_b, ln_v_g, ln_v_b, ws, bias2d, n_seq):
    T = z.shape[0]
    S = T // n_seq
    ts = _tile(S, 512)
    n_r = S // ts
    shift = HALO - (CONV_WIDTH - 1)

    def body(av_ref, ag_ref, avh_ref, agh_ref, u_ref, v_ref, cw_ref, cb_ref, lag_ref, lab_ref,
             lvg_ref, lvb_ref, ws_ref, bias_ref, cat_ref, a1_ref, a0_s):
        _glu_with_halo(av_ref, ag_ref, avh_ref, agh_ref, a0_s, pl.program_id(1) == 0, ts)
        for rb in range(ts // CONV_ROWS):
            base = rb * CONV_ROWS
            acc = jnp.broadcast_to(cb_ref[...], (CONV_ROWS, CA))
            for k in range(CONV_WIDTH):
                acc = acc + cw_ref[k:k + 1, :] * _shifted(a0_s, base + shift + k, CONV_ROWS)
            a1_ref[base:base + CONV_ROWS, :] = acc
        y, _, _ = _ln_fwd(a1_ref[...], lag_ref[...], lab_ref[...])
        cat_ref[:, 0:CA] = (y * _sigmoid(y)).astype(BF16)
        for gi in range(GB):
            sl = slice(gi * DB, (gi + 1) * DB)
            v1, _, _ = _ln_fwd(v_ref[:, sl], lvg_ref[:, sl], lvb_ref[:, sl])
            v1 = v1.astype(BF16)
            for c in range(ts // CHUNK):
                rs = slice(c * CHUNK, (c + 1) * CHUNK)
                v2 = _dot(ws_ref[gi], v1[rs], NN) + bias_ref[:, sl]
                cat_ref[rs, CA + gi * DB:CA + (gi + 1) * DB] = (u_ref[rs, sl] * v2).astype(BF16)

    blk, left = _mix_specs(ts, n_r, True)
    vec = pl.BlockSpec((1, CA), lambda b, r: (0, 0))
    return pl.pallas_call(
        body, name="mix_fwd", grid=(n_seq, n_r),
        in_specs=[blk(0), blk(1), left(0), left(1), blk(2), blk(3),
                  pl.BlockSpec((CONV_WIDTH, CA), lambda b, r: (0, 0)), vec, vec, vec, vec, vec,
                  pl.BlockSpec((GB, CHUNK, CHUNK), lambda b, r: (0, 0, 0)),
                  pl.BlockSpec((CHUNK, CB), lambda b, r: (0, 0))],
        out_specs=[pl.BlockSpec((ts, CA + CB), lambda b, r: (b * n_r + r, 0)), blk(0)],
        out_shape=[jax.ShapeDtypeStruct((T, CA + CB), BF16), jax.ShapeDtypeStruct((T, CA), F32)],
        scratch_shapes=[pltpu.VMEM((SUBLANES, HALO + ts, CA), F32)],
        compiler_params=_cp("parallel", "parallel"),
    )(z, z, z, z, z, z, conv_w, conv_b, ln_a_g, ln_a_b, ln_v_g, ln_v_b, ws, bias2d)


def _mix_bwd_rows(dcat, z, a1, ln_a_g, ln_a_b, ln_v_g, ln_v_b, ws, ws_t, bias2d, n_seq):
    T = z.shape[0]
    S = T // n_seq
    ts = _tile(S, 512)
    n_r = S // ts

    def body(dc_ref, u_ref, v_ref, a1_ref, lag_ref, lab_ref, lvg_ref, lvb_ref, ws_ref, wst_ref, bias_ref,
             da1_ref, dz_ref, dlag_ref, dlab_ref, dlvg_ref, dlvb_ref, dws_ref, dsb_ref, dv1_s, dbias_s):
        first = (pl.program_id(0) == 0) & (pl.program_id(1) == 0)
        last = (pl.program_id(0) == n_seq - 1) & (pl.program_id(1) == n_r - 1)

        @pl.when(first)
        def _():
            for ref in (dlag_ref, dlab_ref, dlvg_ref, dlvb_ref, dws_ref, dbias_s):
                ref[...] = jnp.zeros_like(ref)

        lag = lag_ref[...]
        y, xh, r = _ln_fwd(a1_ref[...], lag, lab_ref[...])
        sig = _sigmoid(y)
        dy = dc_ref[:, 0:CA] * (sig * (1.0 + y * (1.0 - sig)))
        dlag_ref[...] += jnp.sum(dy * xh, axis=0, keepdims=True)
        dlab_ref[...] += jnp.sum(dy, axis=0, keepdims=True)
        da1_ref[...] = _ln_bwd(dy, xh, r, lag)

        tril = (lax.broadcasted_iota(jnp.int32, (CHUNK, CHUNK), 0)
                >= lax.broadcasted_iota(jnp.int32, (CHUNK, CHUNK), 1))
        for gi in range(GB):
            sl = slice(gi * DB, (gi + 1) * DB)
            lvg = lvg_ref[:, sl]
            v1, vh, vr = _ln_fwd(v_ref[:, sl], lvg, lvb_ref[:, sl])
            v1 = v1.astype(BF16)
            for c in range(ts // CHUNK):
                rs = slice(c * CHUNK, (c + 1) * CHUNK)
                v2 = _dot(ws_ref[gi], v1[rs], NN) + bias_ref[:, sl]
                dbo = dc_ref[rs, CA + gi * DB:CA + (gi + 1) * DB]
                dz_ref[rs, sl] = (dbo * v2).astype(BF16)
                dv2 = dbo * u_ref[rs, sl]
                dbias_s[:, sl] += dv2
                dv2b = dv2.astype(BF16)
                dws_ref[gi] += jnp.where(tril, _dot(dv2b, v1[rs], NT), 0.0)
                dv1_s[rs, :] = _dot(wst_ref[gi], dv2b, NN)
            dv1 = dv1_s[...]
            dlvg_ref[:, sl] += jnp.sum(dv1 * vh, axis=0, keepdims=True)
            dlvb_ref[:, sl] += jnp.sum(dv1, axis=0, keepdims=True)
            dz_ref[:, CB + gi * DB:CB + (gi + 1) * DB] = _ln_bwd(dv1, vh, vr, lvg).astype(BF16)

        @pl.when(last)
        def _():
            col = lax.broadcasted_iota(jnp.int32, (CHUNK, GB), 1)
            out = jnp.zeros((CHUNK, GB), F32)
            for gi in range(GB):
                s = jnp.sum(dbias_s[:, gi * DB:(gi + 1) * DB], axis=1, keepdims=True)
                out = out + jnp.where(col == gi, s, 0.0)
            dsb_ref[...] = out

    blk, _ = _mix_specs(ts, n_r, False)
    vec = pl.BlockSpec((1, CA), lambda b, r: (0, 0))
    mat = pl.BlockSpec((GB, CHUNK, CHUNK), lambda b, r: (0, 0, 0))
    wide = pl.BlockSpec((ts, CA + CB), lambda b, r: (b * n_r + r, 0))
    return pl.pallas_call(
        body, name="mix_bwd_rows", grid=(n_seq, n_r),
        in_specs=[wide, blk(2), blk(3), blk(0), vec, vec, vec, vec, mat, mat,
                  pl.BlockSpec((CHUNK, CB), lambda b, r: (0, 0))],
        out_specs=[blk(0), wide, vec, vec, vec, vec, mat, pl.BlockSpec((CHUNK, GB), lambda b, r: (0, 0))],
        out_shape=[jax.ShapeDtypeStruct((T, CA), F32), jax.ShapeDtypeStruct((T, 2 * CB), BF16)]
        + [jax.ShapeDtypeStruct((1, CA), F32)] * 4
        + [jax.ShapeDtypeStruct((GB, CHUNK, CHUNK), F32), jax.ShapeDtypeStruct((CHUNK, GB), F32)],
        scratch_shapes=[pltpu.VMEM((ts, DB), F32), pltpu.VMEM((CHUNK, CB), F32)],
        compiler_params=_cp("arbitrary", "arbitrary"),
    )(dcat, z, z, a1, ln_a_g, ln_a_b, ln_v_g, ln_v_b, ws, ws_t, bias2d)


def _mix_bwd_conv(da1, z, conv_w, n_seq):
    T = z.shape[0]
    S = T // n_seq
    ts = _tile(S, 512)
    n_r = S // ts
    per = ts // HALO
    shift = HALO - (CONV_WIDTH - 1)
    fold = CONV_ROWS // 8

    def body(d_ref, dh_ref, av_ref, ag_ref, avh_ref, agh_ref, cw_ref,
             dz_ref, dcw_ref, dcb_ref, a0_s, d1_s, da0_s, dw8_s):
        first = (pl.program_id(0) == 0) & (pl.program_id(1) == 0)
        last = (pl.program_id(0) == n_seq - 1) & (pl.program_id(1) == n_r - 1)

        @pl.when(first)
        def _():
            dw8_s[...] = jnp.zeros_like(dw8_s)
            dcb_ref[...] = jnp.zeros_like(dcb_ref)

        _glu_with_halo(av_ref, ag_ref, avh_ref, agh_ref, a0_s, pl.program_id(1) == 0, ts)
        d1_s[0, 0:ts, :] = d_ref[...]
        d1_s[0, ts:ts + HALO, :] = jnp.where(pl.program_id(1) == n_r - 1, 0.0, dh_ref[...])
        _shifted_copies(d1_s)
        dcb_ref[...] += jnp.sum(d_ref[...], axis=0, keepdims=True)
        for rb in range(ts // CONV_ROWS):
            base = rb * CONV_ROWS
            dcur = d1_s[0, base:base + CONV_ROWS, :]
            acc = jnp.zeros((CONV_ROWS, CA), F32)
            for k in range(CONV_WIDTH):
                back = CONV_WIDTH - 1 - k
                acc = acc + cw_ref[k:k + 1, :] * _shifted(d1_s, base + back, CONV_ROWS)
                prod = dcur * _shifted(a0_s, base + shift + k, CONV_ROWS)
                part = prod[0:8]
                for f in range(1, fold):
                    part = part + prod[8 * f:8 * f + 8]
                dw8_s[k] += part
            da0_s[base:base + CONV_ROWS, :] = acc
        da0 = da0_s[...]
        sig = _sigmoid(ag_ref[...])
        dz_ref[:, 0:CA] = (da0 * sig).astype(BF16)
        dz_ref[:, CA:2 * CA] = (da0 * av_ref[...] * sig * (1.0 - sig)).astype(BF16)

        @pl.when(last)
        def _():
            for k in range(CONV_WIDTH):
                dcw_ref[k:k + 1, :] = jnp.sum(dw8_s[k], axis=0, keepdims=True)

    blk, left = _mix_specs(ts, n_r, True)
    n_halo_blocks = T // HALO
    right = pl.BlockSpec((HALO, CA), lambda b, r: (jnp.minimum((b * n_r + r + 1) * per, n_halo_blocks - 1), 0))
    return pl.pallas_call(
        body, name="mix_bwd_conv", grid=(n_seq, n_r),
        in_specs=[blk(0), right, blk(0), blk(1), left(0), left(1),
                  pl.BlockSpec((CONV_WIDTH, CA), lambda b, r: (0, 0))],
        out_specs=[pl.BlockSpec((ts, 2 * CA), lambda b, r: (b * n_r + r, 0)),
                   pl.BlockSpec((CONV_WIDTH, CA), lambda b, r: (0, 0)), pl.BlockSpec((1, CA), lambda b, r: (0, 0))],
        out_shape=[jax.ShapeDtypeStruct((T, 2 * CA), BF16), jax.ShapeDtypeStruct((CONV_WIDTH, CA), F32),
                   jax.ShapeDtypeStruct((1, CA), F32)],
        scratch_shapes=[pltpu.VMEM((SUBLANES, HALO + ts, CA), F32), pltpu.VMEM((SUBLANES, ts + HALO, CA), F32),
                        pltpu.VMEM((ts, CA), F32), pltpu.VMEM((CONV_WIDTH, 8, CA), F32)],
        compiler_params=_cp("arbitrary", "arbitrary"),
    )(da1, da1, z, z, z, z, conv_w)


def _row_tile(R, want):
    t = min(R, want)
    t -= t % 8
    while t > 8 and R % t:
        t -= 8
    return t if t >= 8 and R % t == 0 else R


def _adam_step(w, g, m, v):
    nm = ADAM_B1 * m + (1.0 - ADAM_B1) * g
    nv = ADAM_B2 * v + (1.0 - ADAM_B2) * (g * g)
    m_hat = nm / (1.0 - ADAM_B1 ** ADAM_STEP)
    v_hat = nv / (1.0 - ADAM_B2 ** ADAM_STEP)
    return -ADAM_LR * (m_hat / (jnp.sqrt(v_hat) + ADAM_EPS) + ADAM_WD * w), nm, nv


def _adamw_parts(parts, w, m, v, name):
    L, n, C = w.shape
    assert len(parts) == L
    tr = _row_tile(n, 192)
    n_i = n // tr

    def body(*refs):
        p_refs = refs[:L]
        w_ref, m_ref, v_ref, g_ref, d_ref, nm_ref, nv_ref = refs[L:]
        for k in range(L):
            @pl.when(pl.program_id(0) == k)
            def _(k=k):
                acc = p_refs[k][0].astype(F32)
                for s in range(1, N_DEV):
                    acc = acc + p_refs[k][s].astype(F32)
                g_ref[...] = acc

        d_ref[...], nm_ref[...], nv_ref[...] = _adam_step(w_ref[...], g_ref[...], m_ref[...], v_ref[...])

    def part_spec(k):
        return pl.BlockSpec((N_DEV, tr, C),
                            lambda l, i: (0, jnp.where(l == k, i, jnp.where(l < k, 0, n_i - 1)), 0))

    blk = pl.BlockSpec((None, tr, C), lambda l, i: (l, i, 0))
    return pl.pallas_call(
        body, name=name, grid=(L, n_i),
        in_specs=[part_spec(k) for k in range(L)] + [blk] * 3, out_specs=[blk] * 4,
        out_shape=[jax.ShapeDtypeStruct((L, n, C), F32)] * 4,
        compiler_params=_cp("arbitrary", "arbitrary"),
    )(*parts, w, m, v)


def _adamw(w, g, m, v, name):
    R, C = w.shape
    tr = _row_tile(R, 256)

    def body(w_ref, g_ref, m_ref, v_ref, d_ref, nm_ref, nv_ref):
        d_ref[...], nm_ref[...], nv_ref[...] = _adam_step(w_ref[...], g_ref[...], m_ref[...], v_ref[...])

    blk = pl.BlockSpec((tr, C), lambda i: (i, 0))
    return pl.pallas_call(
        body, name=name, grid=(R // tr,),
        in_specs=[blk] * 4, out_specs=[blk] * 3,
        out_shape=[jax.ShapeDtypeStruct((R, C), F32)] * 3,
        compiler_params=_cp("parallel"),
    )(w, g, m, v)


def _me():
    return lax.axis_index("x"), lax.axis_index("y"), lax.axis_index("c")


def _block_rows(ref, dev, n):
    start = (4 * dev[0] + 2 * dev[1] + dev[2]) * n
    if len(ref.shape) == 2:
        return ref.at[pl.ds(start, n), :]
    return ref.at[:, pl.ds(start, n), :]


def _all_gather(shards):
    na = len(shards)
    ns = [s.shape[-2] for s in shards]

    def body(*refs):
        ins, outs = refs[:na], refs[na:2 * na]
        send_sems, recv_sems, local_sems = refs[2 * na:]
        x, y, c = _me()
        me, sibling = (x, y, c), (x, y, 1 - c)
        chips = [(1 - x, y), (x, 1 - y), (1 - x, 1 - y)]

        def copy(a, k, block, to, src=None):
            dst = _block_rows(outs[a], block, ns[a])
            return pltpu.make_async_remote_copy(
                src_ref=dst if src is None else src, dst_ref=dst,
                send_sem=send_sems.at[a, k], recv_sem=recv_sems.at[a, k], device_id=to, device_id_type=MESH)

        mine = [pltpu.make_async_copy(ins[a], _block_rows(outs[a], me, ns[a]), local_sems.at[a]) for a in range(na)]
        for cp in mine:
            cp.start()
        first = []
        for a in range(na):
            first.append(copy(a, 0, me, sibling, src=ins[a]))
            first += [copy(a, 1 + j, me, (*chip, c), src=ins[a]) for j, chip in enumerate(chips)]
        for cp in first:
            cp.start()
        passed = []
        for j, chip in enumerate(chips):
            for a in range(na):
                copy(a, 1 + j, (*chip, c), me).wait_recv()
                fwd = copy(a, 4 + j, (*chip, c), sibling)
                fwd.start()
                passed.append(fwd)
        for a in range(na):
            copy(a, 0, sibling, me).wait_recv()
            for j, chip in enumerate(chips):
                copy(a, 4 + j, (*chip, 1 - c), me).wait_recv()
        for cp in first + passed:
            cp.wait_send()
        for cp in mine:
            cp.wait()

    out_shape = [jax.ShapeDtypeStruct(s.shape[:-2] + (N_DEV * s.shape[-2], s.shape[-1]), s.dtype) for s in shards]
    return pl.pallas_call(
        body, name="weights_all_gather",
        in_specs=[ANY] * na, out_specs=[ANY] * na, out_shape=out_shape,
        scratch_shapes=[pltpu.SemaphoreType.DMA((na, 7)), pltpu.SemaphoreType.DMA((na, 7)),
                        pltpu.SemaphoreType.DMA((na,))],
    )(*shards)


def _split_copies(gather, srcs, lands, send_sems, recv_sems, ns):
    x, y, c = _me()
    me = (x, y, c)
    my_slot = 4 * x + 2 * y + c
    copies = []
    for mask in range(1, N_DEV):
        peer = (x ^ (mask >> 2), y ^ ((mask >> 1) & 1), c ^ (mask & 1))
        for a in range(len(srcs)):
            if gather:
                src, dst = srcs[a], _block_rows(lands[a], me, ns[a])
            else:
                src, dst = _block_rows(srcs[a], peer, ns[a]), lands[a].at[my_slot]
            sem = a * (N_DEV - 1) + mask - 1
            copies.append(pltpu.make_async_remote_copy(
                src_ref=src, dst_ref=dst, send_sem=send_sems.at[sem], recv_sem=recv_sems.at[sem],
                device_id=peer, device_id_type=MESH))
    return copies


HBM_SPEC = pl.BlockSpec(memory_space=pltpu.HBM)
SEM_SPEC = pl.BlockSpec(memory_space=pltpu.SEMAPHORE)


def _split_start(gather, srcs, name, dep=None):
    na = len(srcs)
    x, y, c = _me()
    mine = 4 * x + 2 * y + c
    if gather:
        ns = [s.shape[-2] for s in srcs]
        lands = [lax.dynamic_update_slice(
            lax.empty(s.shape[:-2] + (N_DEV * s.shape[-2], s.shape[-1]), s.dtype), s,
            (0,) * (s.ndim - 2) + (mine * s.shape[-2], 0)) for s in srcs]
    else:
        ns = [s.shape[-2] // N_DEV for s in srcs]
        lands = [lax.dynamic_update_slice(
            lax.empty((N_DEV, n, s.shape[-1]), s.dtype),
            lax.dynamic_slice(s, (mine * n, 0), (n, s.shape[-1]))[None], (mine, 0, 0)) for s, n in zip(srcs, ns)]
    n_in = 2 * na + (dep is not None)

    def body(*refs):
        send_sems, recv_sems = refs[n_in], refs[n_in + 1]
        for cp in _split_copies(gather, refs[:na], refs[na:2 * na], send_sems, recv_sems, ns):
            cp.start()
        refs[-1][...] = jnp.zeros_like(refs[-1])

    hbm = lambda a: pltpu.with_memory_space_constraint(a, pltpu.HBM)
    args = [hbm(a) for a in srcs] + [hbm(a) for a in lands] + ([dep] if dep is not None else [])
    out = pl.pallas_call(
        body, name=name,
        in_specs=[HBM_SPEC] * (2 * na) + ([ANY] if dep is not None else []),
        out_specs=[SEM_SPEC, SEM_SPEC] + [HBM_SPEC] * (2 * na) + [pl.BlockSpec(memory_space=pltpu.VMEM)],
        out_shape=[pltpu.SemaphoreType.DMA((na * (N_DEV - 1),)), pltpu.SemaphoreType.DMA((na * (N_DEV - 1),))]
        + [pltpu.HBM(a.shape, a.dtype) for a in srcs + lands] + [jax.ShapeDtypeStruct((8, LANES), F32)],
        input_output_aliases={i: 2 + i for i in range(2 * na)},
        compiler_params=pltpu.CompilerParams(has_side_effects=pltpu.SideEffectType.DATAFLOW_SIDE_EFFECTING),
    )(*args)
    return (gather, ns, out[0], out[1], list(out[2:2 + na]), list(out[2 + na:2 + 2 * na])), out[-1]


def _split_wait(handle, after, name):
    gather, ns, send, recv, srcs, lands = handle
    na = len(srcs)

    def body(*refs):
        send_sems, recv_sems = refs[2 * na], refs[2 * na + 1]
        for cp in _split_copies(gather, refs[:na], refs[na:2 * na], send_sems, recv_sems, ns):
            cp.wait_send()
            cp.wait_recv()

    out = pl.pallas_call(
        body, name=name,
        in_specs=[HBM_SPEC] * (2 * na) + [SEM_SPEC, SEM_SPEC, ANY],
        out_specs=[HBM_SPEC] * (2 * na),
        out_shape=[pltpu.HBM(a.shape, a.dtype) for a in srcs + lands],
        input_output_aliases={i: i for i in range(2 * na)},
        compiler_params=pltpu.CompilerParams(has_side_effects=pltpu.SideEffectType.DATAFLOW_SIDE_EFFECTING),
    )(*srcs, *lands, send, recv, after)
    return list(out[:na]), list(out[na:])


def _small_all_reduce(buf):
    R = buf.shape[0]

    def body(b_ref, o_ref, recv_ref, send_sems, recv_sems):
        x, y, c = _me()
        my_slot = 4 * x + 2 * y + c
        recv_ref[my_slot] = b_ref[...]
        copies = []
        for mask in range(1, N_DEV):
            peer = (x ^ (mask >> 2), y ^ ((mask >> 1) & 1), c ^ (mask & 1))
            copies.append(pltpu.make_async_remote_copy(
                src_ref=b_ref, dst_ref=recv_ref.at[my_slot],
                send_sem=send_sems.at[mask - 1], recv_sem=recv_sems.at[mask - 1],
                device_id=peer, device_id_type=MESH))
        for cp in copies:
            cp.start()
        for cp in copies:
            cp.wait()
        acc = recv_ref[0]
        for k in range(1, N_DEV):
            acc = acc + recv_ref[k]
        o_ref[...] = acc

    return pl.pallas_call(
        body, name="small_all_reduce",
        in_specs=[pl.BlockSpec(memory_space=pltpu.VMEM)], out_specs=pl.BlockSpec(memory_space=pltpu.VMEM),
        out_shape=jax.ShapeDtypeStruct((R, LANES), F32),
        scratch_shapes=[pltpu.VMEM((N_DEV, R, LANES), F32), pltpu.SemaphoreType.DMA((7,)),
                        pltpu.SemaphoreType.DMA((7,))],
        compiler_params=pltpu.CompilerParams(vmem_limit_bytes=VMEM_LIMIT),
    )(buf)


def _pack(arrays):
    flat = jnp.concatenate([a.reshape(-1) for a in arrays])
    pad = (-flat.shape[0]) % (8 * LANES)
    return jnp.pad(flat, (0, pad)).reshape(-1, LANES)


def _unpack(buf, shapes):
    flat = buf.reshape(-1)
    out, off = [], 0
    for s in shapes:
        n = 1
        for d in s:
            n *= d
        out.append(flat[off:off + n].reshape(s))
        off += n
    return out


def _ffn_index(layer, second):
    return (2 * layer + second) * 3


def kernel(x, g_ffn1, w_ffn1_gate, w_ffn1_up, w_ffn1_down, g_mix, w_in_ab, conv_w, conv_b, ln_a_g, ln_a_b, ln_v_g, ln_v_b, sp_w, sp_b, w_out_ab, w_qkv, w_o, g_ffn2, w_ffn2_gate, w_ffn2_up, w_ffn2_down, g_final, loss_target, m_g_ffn1, m_w_ffn1_gate, m_w_ffn1_up, m_w_ffn1_down, m_g_mix, m_w_in_ab, m_conv_w, m_conv_b, m_ln_a_g, m_ln_a_b, m_ln_v_g, m_ln_v_b, m_sp_w, m_sp_b, m_w_out_ab, m_w_qkv, m_w_o, m_g_ffn2, m_w_ffn2_gate, m_w_ffn2_up, m_w_ffn2_down, m_g_final, v_g_ffn1, v_w_ffn1_gate, v_w_ffn1_up, v_w_ffn1_down, v_g_mix, v_w_in_ab, v_conv_w, v_conv_b, v_ln_a_g, v_ln_a_b, v_ln_v_g, v_ln_v_b, v_sp_w, v_sp_b, v_w_out_ab, v_w_qkv, v_w_o, v_g_ffn2, v_w_ffn2_gate, v_w_ffn2_up, v_w_ffn2_down, v_g_final):
    n_seq, S, D = x.shape
    T = n_seq * S
    depth = g_ffn1.shape[0]
    assert depth == 2 and D == D_MODEL
    my_block = 4 * lax.axis_index("x") + 2 * lax.axis_index("y") + lax.axis_index("c")

    ffn_parts = []
    for l in range(depth):
        for gate, up, down in ((w_ffn1_gate, w_ffn1_up, w_ffn1_down), (w_ffn2_gate, w_ffn2_up, w_ffn2_down)):
            ffn_parts += [gate[l].T, up[l].T, down[l]]
    ffn_shard = lambda k: jnp.stack(ffn_parts[3 * k:3 * k + 3]).astype(BF16)
    conv_w_pad = jnp.zeros((HALO, conv_w.shape[2]), F32).at[:CONV_WIDTH].set(conv_w[0]).T
    w_ffn = [None] * (2 * depth)
    w_ffn[0], conv_w_t = _all_gather([ffn_shard(0), conv_w_pad])
    conv_w_full = conv_w_t.T[:CONV_WIDTH]
    shards_b = [w_out_ab[0].astype(BF16), ffn_shard(1)]
    shards_d = [w_qkv[0].T.astype(BF16), w_o[0].astype(BF16), ffn_shard(3)]
    gather_a, token = _split_start(True, [w_in_ab[0].T.astype(BF16)], "gather_a_start", dep=conv_w_t)
    gather_b, token = _split_start(True, shards_b, "gather_b_start", dep=token)
    gather_c, token = _split_start(True, [ffn_shard(2)], "gather_c_start", dep=token)
    gather_d, token = _split_start(True, shards_d, "gather_d_start", dep=token)

    def gathered(handle, after, name):
        return _split_wait(handle, after, name)[1]

    row = lambda a: a.reshape(1, -1)
    tril = jnp.tril(jnp.ones((CHUNK, CHUNK), dtype=bool))
    ws = jnp.where(tril[None], sp_w[0], 0.0).astype(BF16)
    ws_t = jnp.swapaxes(ws, 1, 2)
    bias2d = jnp.repeat(sp_b[0].T, DB, axis=1)
    conv_b2, lag, lab = row(conv_b[0]), row(ln_a_g[0]), row(ln_a_b[0])
    lvg, lvb = row(ln_v_g[0]), row(ln_v_b[0])

    x0 = x.reshape(T, D)
    target = loss_target.reshape(T, D)
    saved = []
    xc = x0
    for l in range(depth):
        xa, a1, b1, h1 = _ffn_fwd(xc, row(g_ffn1[l]), w_ffn[2 * l], 0, f"ffn1_fwd_{l}", dep=token)
        if l % 2 == 0:
            w_in_t, = gathered(gather_a, xa, "gather_a_wait")
            z, hm = _mm_nt(xa, w_in_t, row(g_mix[l]), F32, "mix_in_proj")
            cat, conv_out = _mix_fwd(z, conv_w_full, conv_b2, lag, lab, lvg, lvb, ws, bias2d, n_seq)
            w_out, w_ffn[1] = gathered(gather_b, cat, "gather_b_wait")
            xb = _mm_nn_res(cat, w_out, xa, "mix_out_proj")
            mixer = (z, hm, cat, conv_out)
        else:
            w_qkv_t, w_o_full, w_ffn[3] = gathered(gather_d, xa, "gather_d_wait")
            qkv, hm = _mm_nt(xa, w_qkv_t, row(g_mix[l]), BF16, "qkv_proj")
            o, att = _attn_fwd(qkv, n_seq)
            xb = _mm_nn_res(o, w_o_full, xa, "attn_out_proj")
            mixer = (qkv, hm, o, att)
        xn, a2, b2, h2 = _ffn_fwd(xb, row(g_ffn2[l]), w_ffn[2 * l + 1], 0, f"ffn2_fwd_{l}")
        saved.append((xc, a1, b1, h1, xa, mixer, xb, a2, b2, h2))
        xc = xn
        if l == 0:
            w_ffn[2], = gathered(gather_c, xc, "gather_c_wait")

    g, loss_part, dg_final = _loss_head(xc, row(g_final), target)

    dg_ffn1, dg_ffn2, dg_mix = [None] * depth, [None] * depth, [None] * depth
    exchanges = {}
    token = None

    def ffn_back(g, xin, gvec, a, b, h, k, tag, token):
        g, dg, da, db, s, gh = _ffn_bwd(g, xin, gvec, a, b, w_ffn[k], 0, f"ffn{tag}_bwd", dep=token)
        if k == 0:
            return g, dg, (da, db, s, gh, h)
        dws = [_mm_tn(da, h, f"dw_gate{tag}"), _mm_tn(db, h, f"dw_up{tag}"), _mm_tn(s, gh, f"dw_down{tag}")]
        exchanges[f"ffn{k}"], token = _split_start(False, dws, f"exchange_ffn{tag}_start")
        return g, dg, token

    for l in reversed(range(depth)):
        xin, a1, b1, h1, xa, mixer, xb, a2, b2, h2 = saved[l]
        g, dg_ffn2[l], token = ffn_back(g, xb, row(g_ffn2[l]), a2, b2, h2, 2 * l + 1, f"2_{l}", token)
        if l % 2 == 0:
            z, hm, cat, conv_out = mixer
            dcat = _mm_nt(g, w_out, None, F32, "mix_out_bwd", dep=token)
            d_w_out = _mm_tn(cat, g, "dw_out")
            (da1, dz_uv, d_lag, d_lab, d_lvg, d_lvb, d_ws, d_sb) = _mix_bwd_rows(
                dcat, z, conv_out, lag, lab, lvg, lvb, ws, ws_t, bias2d, n_seq)
            dz_a, d_cw, d_cb = _mix_bwd_conv(da1, z, conv_w_full, n_seq)
            d_w_in_t = jnp.concatenate([_mm_tn(dz_a, hm, "dw_in_a"), _mm_tn(dz_uv, hm, "dw_in_uv")])
            exchanges["mix"], token = _split_start(False, [d_w_out, d_w_in_t], "exchange_mix_start")
            g, dg_mix[l] = _mm_nn_rmsbwd([dz_a, dz_uv], w_in_t, xa, row(g_mix[l]), g, "mix_in_bwd", dep=token)
        else:
            qkv, hm, o, att = mixer
            do = _mm_nt(g, w_o_full, None, BF16, "attn_out_bwd", dep=token)
            d_w_o = _mm_tn(o, g, "dw_o")
            dq, dk, dv = _attn_bwd(qkv, do, att, n_seq)
            d_w_qkv_t = jnp.concatenate([_mm_tn(dq, hm, "dw_q"), _mm_tn(dk, hm, "dw_k"), _mm_tn(dv, hm, "dw_v")])
            exchanges["attn"], token = _split_start(False, [d_w_o, d_w_qkv_t], "exchange_attn_start")
            g, dg_mix[l] = _mm_nn_rmsbwd([dq, dk, dv], w_qkv_t, xa, row(g_mix[l]), g, "qkv_bwd", dep=token)
        g, dg_ffn1[l], token = ffn_back(g, xin, row(g_ffn1[l]), a1, b1, h1, 2 * l, f"1_{l}", token)
    grad_x = g.reshape(n_seq, S, D)

    small = [jnp.concatenate(dg_ffn1), jnp.concatenate(dg_mix), d_cw, d_cb, d_lag, d_lab, d_lvg, d_lvb,
             jnp.where(tril[None], d_ws, 0.0), d_sb.T, jnp.concatenate(dg_ffn2), dg_final, loss_part[:, :1]]
    small_shapes = [(depth, D), (depth, D), (CONV_WIDTH, CA), (1, CA), (1, CA), (1, CA), (1, GB, DB), (1, GB, DB),
                    (1, GB, CHUNK, CHUNK), (1, GB, CHUNK), (depth, D), (D,), ()]
    small_sum = _small_all_reduce(_pack(small))
    red = _unpack(small_sum, small_shapes)
    (gr_g_ffn1, gr_g_mix, gr_cw_full, gr_cb, gr_lag, gr_lab, gr_lvg, gr_lvb, gr_sp_w, gr_sp_b,
     gr_g_ffn2, gr_g_final, loss) = red
    n_cw = conv_w.shape[2]
    gr_cw = lax.dynamic_slice(gr_cw_full, (0, my_block * n_cw), (CONV_WIDTH, n_cw))[None]

    da, db, s, gh, h = token
    token = small_sum
    for which, lhs, rhs in ((2, s, gh), (1, db, h), (0, da, h)):
        dw = _mm_tn(lhs, rhs, f"dw_ffn0_{which}", dep=token, out_dtype=BF16)
        exchanges[f"ffn0_{which}"], token = _split_start(False, [dw], f"exchange_ffn0_{which}_start")

    def landed(key, after):
        return _split_wait(exchanges[key], after, f"exchange_{key}_wait")[1]

    parts_ffn = [None] * (6 * depth)
    for k in range(1, 2 * depth):
        parts_ffn[3 * k:3 * k + 3] = landed(f"ffn{k}", token)
    parts_out, parts_in = landed("mix", token)
    parts_o, parts_qkv = landed("attn", token)

    grads = {
        "g_ffn1": gr_g_ffn1, "g_mix": gr_g_mix, "conv_w": gr_cw, "conv_b": gr_cb, "ln_a_g": gr_lag,
        "ln_a_b": gr_lab, "ln_v_g": gr_lvg, "ln_v_b": gr_lvb, "sp_w": gr_sp_w, "sp_b": gr_sp_b,
        "g_ffn2": gr_g_ffn2, "g_final": gr_g_final,
    }
    weights = dict(g_ffn1=g_ffn1, w_ffn1_gate=w_ffn1_gate, w_ffn1_up=w_ffn1_up, w_ffn1_down=w_ffn1_down, g_mix=g_mix,
                   w_in_ab=w_in_ab, conv_w=conv_w, conv_b=conv_b, ln_a_g=ln_a_g, ln_a_b=ln_a_b, ln_v_g=ln_v_g,
                   ln_v_b=ln_v_b, sp_w=sp_w, sp_b=sp_b, w_out_ab=w_out_ab, w_qkv=w_qkv, w_o=w_o, g_ffn2=g_ffn2,
                   w_ffn2_gate=w_ffn2_gate, w_ffn2_up=w_ffn2_up, w_ffn2_down=w_ffn2_down, g_final=g_final)
    m_in = dict(g_ffn1=m_g_ffn1, w_ffn1_gate=m_w_ffn1_gate, w_ffn1_up=m_w_ffn1_up, w_ffn1_down=m_w_ffn1_down,
                g_mix=m_g_mix, w_in_ab=m_w_in_ab, conv_w=m_conv_w, conv_b=m_conv_b, ln_a_g=m_ln_a_g, ln_a_b=m_ln_a_b,
                ln_v_g=m_ln_v_g, ln_v_b=m_ln_v_b, sp_w=m_sp_w, sp_b=m_sp_b, w_out_ab=m_w_out_ab, w_qkv=m_w_qkv,
                w_o=m_w_o, g_ffn2=m_g_ffn2, w_ffn2_gate=m_w_ffn2_gate, w_ffn2_up=m_w_ffn2_up,
                w_ffn2_down=m_w_ffn2_down, g_final=m_g_final)
    v_in = dict(g_ffn1=v_g_ffn1, w_ffn1_gate=v_w_ffn1_gate, w_ffn1_up=v_w_ffn1_up, w_ffn1_down=v_w_ffn1_down,
                g_mix=v_g_mix, w_in_ab=v_w_in_ab, conv_w=v_conv_w, conv_b=v_conv_b, ln_a_g=v_ln_a_g, ln_a_b=v_ln_a_b,
                ln_v_g=v_ln_v_g, ln_v_b=v_ln_v_b, sp_w=v_sp_w, sp_b=v_sp_b, w_out_ab=v_w_out_ab, w_qkv=v_w_qkv,
                w_o=v_w_o, g_ffn2=v_g_ffn2, w_ffn2_gate=v_w_ffn2_gate, w_ffn2_up=v_w_ffn2_up,
                w_ffn2_down=v_w_ffn2_down, g_final=v_g_final)
    names = list(weights)
    grads = {n: grads[n].reshape(weights[n].shape) for n in grads}

    delta, new_m, new_v = {}, {}, {}

    def adamw_big(n, parts):
        if weights[n].shape[-1] == D:
            view = back = lambda a: a
        else:
            view = back = lambda a: jnp.swapaxes(a, 1, 2)
        out = _adamw_parts(parts, view(weights[n]), view(m_in[n]), view(v_in[n]), f"adamw_{n}")
        grads[n], delta[n], new_m[n], new_v[n] = [back(a) for a in out]

    adamw_big("w_in_ab", [parts_in])
    adamw_big("w_out_ab", [parts_out])
    adamw_big("w_qkv", [parts_qkv])
    adamw_big("w_o", [parts_o])
    kinds = ("gate", "up", "down")
    for which, kind in enumerate(kinds):
        adamw_big(f"w_ffn2_{kind}", [parts_ffn[_ffn_index(l, 1) + which] for l in range(depth)])
    big = [n for n in names if n.startswith("w_")]
    after = jnp.concatenate([delta[n].reshape(-1)[:1] for n in big if n in delta]).reshape(1, -1)
    for which in (2, 1, 0):
        parts_ffn[which], = landed(f"ffn0_{which}", after)
    for which, kind in enumerate(kinds):
        adamw_big(f"w_ffn1_{kind}", [parts_ffn[_ffn_index(l, 0) + which] for l in range(depth)])
    little = [n for n in names if n not in big]
    shapes = [weights[n].shape for n in little]
    d, nm, nv = _adamw(_pack([weights[n] for n in little]), _pack([grads[n] for n in little]),
                       _pack([m_in[n] for n in little]), _pack([v_in[n] for n in little]), "adamw_small")
    for n, dd, mm, vv in zip(little, _unpack(d, shapes), _unpack(nm, shapes), _unpack(nv, shapes)):
        delta[n], new_m[n], new_v[n] = dd, mm, vv

    return (loss, grad_x, *[grads[n] for n in names], *[delta[n] for n in names],
            *[new_m[n] for n in names], *[new_v[n] for n in names])
```

```python
import functools

import jax
import jax.numpy as jnp
from jax import lax
from jax.experimental import pallas as pl
from jax.experimental.pallas import tpu as pltpu

F32 = jnp.float32
BF16 = jnp.bfloat16

D_MODEL = 1024
CA = 512
CB = 512
GB = 4
DB = 128
CHUNK = 128
CONV_WIDTH = 31
N_HEADS = 16
HEAD_DIM = 64
EPS = 1e-6
N_DEV = 8
LANES = 128
SUBLANES = 8
QB = 128
ATT_TQ = 1024
FFN_TN = 2816
FFN_TM = 256
HALO = 32
CONV_ROWS = 32
ATT_SCALE = HEAD_DIM ** -0.5

ADAM_LR = 0.001
ADAM_B1 = 0.9
ADAM_B2 = 0.999
ADAM_EPS = 1e-08
ADAM_WD = 0.01
ADAM_STEP = 10

NT = (((1,), (1,)), ((), ()))
NN = (((1,), (0,)), ((), ()))
TN = (((0,), (0,)), ((), ()))
MESH = pl.DeviceIdType.MESH
ANY = pl.BlockSpec(memory_space=pl.ANY)
VMEM_LIMIT = 60 * 1024 * 1024


def _dot(a, b, dims):
    return lax.dot_general(a, b, dims, preferred_element_type=F32)


def _cp(*sem):
    return pltpu.CompilerParams(dimension_semantics=sem, vmem_limit_bytes=VMEM_LIMIT)


def _pcall(body, *, in_specs, args, dep=None, **kw):
    if dep is not None:
        n_in = len(in_specs)
        inner = body

        def body(*refs):
            inner(*refs[:n_in], *refs[n_in + 1:])

        in_specs = list(in_specs) + [ANY]
        args = tuple(args) + (dep,)
    return pl.pallas_call(body, in_specs=list(in_specs), **kw)(*args)


def _tile(n, want):
    if n <= want:
        return n
    t = want - want % LANES
    while t > LANES and n % t:
        t -= LANES
    assert n % t == 0, (n, want)
    return t


def _sigmoid(x):
    return 0.5 * jnp.tanh(0.5 * x) + 0.5


def _rstd(x):
    return lax.rsqrt(jnp.mean(x * x, axis=-1, keepdims=True) + EPS)


def _rms_bwd(x, g, dh):
    r = _rstd(x)
    u = dh * g
    dx = r * (u - x * (r * r) * jnp.mean(u * x, axis=-1, keepdims=True))
    dg = jnp.sum(dh * x * r, axis=0, keepdims=True)
    return dx, dg


def _ln_fwd(x, g, b):
    mu = jnp.mean(x, axis=-1, keepdims=True)
    xc = x - mu
    r = lax.rsqrt(jnp.mean(xc * xc, axis=-1, keepdims=True) + EPS)
    xh = xc * r
    return xh * g + b, xh, r


def _ln_bwd(dy, xh, r, g):
    dxh = dy * g
    return r * (dxh - jnp.mean(dxh, axis=-1, keepdims=True)
                - xh * jnp.mean(dxh * xh, axis=-1, keepdims=True))


def _ffn_fwd(x, g, wall, base, name, dep=None):
    T, D = x.shape
    F = wall.shape[1]
    tm, tn = _tile(T, FFN_TM), _tile(F, FFN_TN)
    n_j = F // tn

    def body(x_ref, g_ref, wg_ref, wu_ref, wd_ref, xo_ref, a_ref, b_ref, h_ref, acc_ref):
        j = pl.program_id(1)

        @pl.when(j == 0)
        def _():
            xv = x_ref[...]
            h_ref[...] = (xv * _rstd(xv) * g_ref[...]).astype(BF16)
            acc_ref[...] = jnp.zeros_like(acc_ref)

        h = h_ref[...]
        a = _dot(h, wg_ref[...], NT)
        b = _dot(h, wu_ref[...], NT)
        a_ref[...] = a.astype(BF16)
        b_ref[...] = b.astype(BF16)
        s = (a * _sigmoid(a) * b).astype(BF16)
        acc_ref[...] += _dot(s, wd_ref[...], NN)

        @pl.when(j == n_j - 1)
        def _():
            xo_ref[...] = x_ref[...] + 0.5 * acc_ref[...]

    single = pl.Buffered(1) if n_j == 1 else None
    wspec = lambda k: pl.BlockSpec((None, tn, D), lambda i, j: (base + k, j, 0), pipeline_mode=single)
    return _pcall(
        body, name=name, grid=(T // tm, n_j), dep=dep, args=(x, g, wall, wall, wall),
        in_specs=[pl.BlockSpec((tm, D), lambda i, j: (i, 0)), pl.BlockSpec((1, D), lambda i, j: (0, 0)),
                  wspec(0), wspec(1), wspec(2)],
        out_specs=[pl.BlockSpec((tm, D), lambda i, j: (i, 0)), pl.BlockSpec((tm, tn), lambda i, j: (i, j)),
                   pl.BlockSpec((tm, tn), lambda i, j: (i, j)), pl.BlockSpec((tm, D), lambda i, j: (i, 0))],
        out_shape=[jax.ShapeDtypeStruct((T, D), F32), jax.ShapeDtypeStruct((T, F), BF16),
                   jax.ShapeDtypeStruct((T, F), BF16), jax.ShapeDtypeStruct((T, D), BF16)],
        scratch_shapes=[pltpu.VMEM((tm, D), F32)],
        compiler_params=_cp("parallel", "arbitrary"),
    )


def _ffn_bwd(go, x, g, a, b, wall, base, name, dep=None):
    T, D = x.shape
    F = wall.shape[1]
    tm, tn = _tile(T, FFN_TM), _tile(F, FFN_TN)
    n_j = F // tn

    def body(go_ref, x_ref, g_ref, a_ref, b_ref, wg_ref, wu_ref, wd_ref,
             gx_ref, dg_ref, da_ref, db_ref, s_ref, gh_ref, acc_ref):
        i, j = pl.program_id(0), pl.program_id(1)

        @pl.when(j == 0)
        def _():
            gh_ref[...] = (0.5 * go_ref[...]).astype(BF16)
            acc_ref[...] = jnp.zeros_like(acc_ref)

        @pl.when((i == 0) & (j == 0))
        def _():
            dg_ref[...] = jnp.zeros_like(dg_ref)

        ds = _dot(gh_ref[...], wd_ref[...], NT)
        av = a_ref[...].astype(F32)
        bv = b_ref[...].astype(F32)
        sig = _sigmoid(av)
        sl = av * sig
        dab = ((ds * bv) * (sig + sl * (1.0 - sig))).astype(BF16)
        dbb = (ds * sl).astype(BF16)
        s_ref[...] = (sl * bv).astype(BF16)
        da_ref[...] = dab
        db_ref[...] = dbb
        acc_ref[...] += _dot(dab, wg_ref[...], NN) + _dot(dbb, wu_ref[...], NN)

        @pl.when(j == n_j - 1)
        def _():
            dx, dg = _rms_bwd(x_ref[...], g_ref[...], acc_ref[...])
            gx_ref[...] = go_ref[...] + dx
            dg_ref[...] += dg

    single = pl.Buffered(1) if n_j == 1 else None
    wspec = lambda k: pl.BlockSpec((None, tn, D), lambda i, j: (base + k, j, 0), pipeline_mode=single)
    row = pl.BlockSpec((tm, D), lambda i, j: (i, 0))
    hid = pl.BlockSpec((tm, tn), lambda i, j: (i, j))
    vec = pl.BlockSpec((1, D), lambda i, j: (0, 0))
    return _pcall(
        body, name=name, grid=(T // tm, n_j), dep=dep, args=(go, x, g, a, b, wall, wall, wall),
        in_specs=[row, row, vec, hid, hid, wspec(0), wspec(1), wspec(2)],
        out_specs=[row, vec, hid, hid, hid, row],
        out_shape=[jax.ShapeDtypeStruct((T, D), F32), jax.ShapeDtypeStruct((1, D), F32),
                   jax.ShapeDtypeStruct((T, F), BF16), jax.ShapeDtypeStruct((T, F), BF16),
                   jax.ShapeDtypeStruct((T, F), BF16), jax.ShapeDtypeStruct((T, D), BF16)],
        scratch_shapes=[pltpu.VMEM((tm, D), F32)],
        compiler_params=_cp("arbitrary", "arbitrary"),
    )


def _mm_tn(a, b, name, dep=None, out_dtype=F32):
    T, M = a.shape
    N = b.shape[1]
    tmm, tk = _tile(M, 1536), _tile(T, 2048)
    n_k = T // tk
    narrow = out_dtype != F32

    def body(a_ref, b_ref, o_ref, *scratch):
        acc_ref = scratch[0] if narrow else o_ref

        @pl.when(pl.program_id(1) == 0)
        def _():
            acc_ref[...] = jnp.zeros_like(acc_ref)

        acc_ref[...] += _dot(a_ref[...].astype(BF16), b_ref[...].astype(BF16), TN)
        if narrow:
            @pl.when(pl.program_id(1) == n_k - 1)
            def _():
                o_ref[...] = acc_ref[...].astype(out_dtype)

    return _pcall(
        body, name=name, grid=(M // tmm, n_k), dep=dep, args=(a, b),
        in_specs=[pl.BlockSpec((tk, tmm), lambda m, k: (k, m)), pl.BlockSpec((tk, N), lambda m, k: (k, 0))],
        out_specs=pl.BlockSpec((tmm, N), lambda m, k: (m, 0)),
        out_shape=jax.ShapeDtypeStruct((M, N), out_dtype),
        scratch_shapes=[pltpu.VMEM((tmm, N), F32)] if narrow else [],
        compiler_params=_cp("parallel", "arbitrary"),
    )


def _mm_nt(x, wt, g, out_dtype, name, dep=None):
    T, K = x.shape
    N = wt.shape[0]
    tm, tn = _tile(T, 512), N
    norm = g is not None

    def body(*refs):
        if norm:
            x_ref, g_ref, w_ref, o_ref, h_ref = refs
        else:
            x_ref, w_ref, o_ref, h_ref = refs

        @pl.when(pl.program_id(1) == 0)
        def _():
            xv = x_ref[...].astype(F32)
            if norm:
                xv = xv * _rstd(xv) * g_ref[...]
            h_ref[...] = xv.astype(BF16)

        o_ref[...] = _dot(h_ref[...], w_ref[...], NT).astype(out_dtype)

    row = pl.BlockSpec((tm, K), lambda i, j: (i, 0))
    wsp = pl.BlockSpec((tn, K), lambda i, j: (j, 0))
    osp = pl.BlockSpec((tm, tn), lambda i, j: (i, j))
    if norm:
        return pl.pallas_call(
            body, name=name, grid=(T // tm, N // tn),
            in_specs=[row, pl.BlockSpec((1, K), lambda i, j: (0, 0)), wsp],
            out_specs=[osp, row],
            out_shape=[jax.ShapeDtypeStruct((T, N), out_dtype), jax.ShapeDtypeStruct((T, K), BF16)],
            compiler_params=_cp("parallel", "arbitrary"),
        )(x, g, wt)
    return _pcall(
        body, name=name, grid=(T // tm, N // tn), dep=dep, args=(x, wt),
        in_specs=[row, wsp], out_specs=osp,
        out_shape=jax.ShapeDtypeStruct((T, N), out_dtype),
        scratch_shapes=[pltpu.VMEM((tm, K), BF16)],
        compiler_params=_cp("parallel", "arbitrary"),
    )


def _mm_nn_res(act, w, resid, name):
    T, K = act.shape
    D = w.shape[1]
    tm = _tile(T, 512)

    def body(a_ref, w_ref, r_ref, o_ref):
        o_ref[...] = r_ref[...] + _dot(a_ref[...].astype(BF16), w_ref[...], NN)

    return pl.pallas_call(
        body, name=name, grid=(T // tm,),
        in_specs=[pl.BlockSpec((tm, K), lambda i: (i, 0)), pl.BlockSpec((K, D), lambda i: (0, 0)),
                  pl.BlockSpec((tm, D), lambda i: (i, 0))],
        out_specs=pl.BlockSpec((tm, D), lambda i: (i, 0)),
        out_shape=jax.ShapeDtypeStruct((T, D), F32),
        compiler_params=_cp("parallel"),
    )(act, w, resid)


def _mm_nn_rmsbwd(acts, w, x, g, gprev, name, dep=None):
    T = acts[0].shape[0]
    ks = [a.shape[1] for a in acts]
    K, D = w.shape
    assert sum(ks) == K
    tm = _tile(T, 512)
    na = len(acts)

    def body(*refs):
        a_refs = refs[:na]
        w_ref, x_ref, g_ref, gp_ref, o_ref, dg_ref = refs[na:]

        @pl.when(pl.program_id(0) == 0)
        def _():
            dg_ref[...] = jnp.zeros_like(dg_ref)

        dh, off = None, 0
        for a_ref, k in zip(a_refs, ks):
            part = _dot(a_ref[...].astype(BF16), w_ref[off:off + k, :], NN)
            dh = part if dh is None else dh + part
            off += k
        dx, dg = _rms_bwd(x_ref[...], g_ref[...], dh)
        o_ref[...] = gp_ref[...] + dx
        dg_ref[...] += dg

    row = pl.BlockSpec((tm, D), lambda i: (i, 0))
    vec = pl.BlockSpec((1, D), lambda i: (0, 0))
    return _pcall(
        body, name=name, grid=(T // tm,), dep=dep, args=(*acts, w, x, g, gprev),
        in_specs=[pl.BlockSpec((tm, k), lambda i: (i, 0)) for k in ks]
        + [pl.BlockSpec((K, D), lambda i: (0, 0)), row, vec, row],
        out_specs=[row, vec],
        out_shape=[jax.ShapeDtypeStruct((T, D), F32), jax.ShapeDtypeStruct((1, D), F32)],
        compiler_params=_cp("arbitrary"),
    )


def _loss_head(x, g, target):
    T, D = x.shape
    tm = _tile(T, 512)

    def body(x_ref, g_ref, t_ref, dx_ref, loss_ref, dg_ref):
        @pl.when(pl.program_id(0) == 0)
        def _():
            loss_ref[...] = jnp.zeros_like(loss_ref)
            dg_ref[...] = jnp.zeros_like(dg_ref)

        xv = x_ref[...]
        gv = g_ref[...]
        e = xv * _rstd(xv) * gv - t_ref[...]
        per_tok = jnp.sum(e * e, axis=-1, keepdims=True) * (1.0 / D)
        loss_ref[...] += 0.5 * jnp.sum(per_tok, axis=0, keepdims=True)
        dx, dg = _rms_bwd(xv, gv, e * (1.0 / D))
        dx_ref[...] = dx
        dg_ref[...] += dg

    row = pl.BlockSpec((tm, D), lambda i: (i, 0))
    vec = pl.BlockSpec((1, D), lambda i: (0, 0))
    return pl.pallas_call(
        body, name="loss_head", grid=(T // tm,),
        in_specs=[row, vec, row],
        out_specs=[row, pl.BlockSpec((1, LANES), lambda i: (0, 0)), vec],
        out_shape=[jax.ShapeDtypeStruct((T, D), F32), jax.ShapeDtypeStruct((1, LANES), F32),
                   jax.ShapeDtypeStruct((1, D), F32)],
        compiler_params=_cp("arbitrary"),
    )(x, g, target)


def _log_gates(z):
    ls = jnp.minimum(z, 0.0) - jnp.log(1.0 + jnp.exp(-jnp.abs(z)))
    return ls, ls - z


def _cumsum_mm(v, u2):
    hi = v.astype(BF16)
    lo = (v - hi.astype(F32)).astype(BF16)
    return _dot(jnp.concatenate([hi, lo], axis=1), u2, NN)


def _half_rowsum(v):
    n = v.shape[0]
    s0 = jnp.sum(v[:, :QB], axis=1, keepdims=True)
    s1 = jnp.sum(v[:, QB:], axis=1, keepdims=True)
    return jnp.concatenate([jnp.broadcast_to(s0, (n, QB)), jnp.broadcast_to(s1, (n, QB))], axis=1)


def _stack_heads(src_ref, dst_ref, n_blk):
    m0 = lax.broadcasted_iota(jnp.int32, (1, LANES), 1) < HEAD_DIM

    def fill(c, carry):
        blk = src_ref[pl.ds(pl.multiple_of(c * QB, QB), QB), :]
        zero = jnp.zeros_like(blk)
        dst_ref[c, 0:QB, :] = jnp.where(m0, blk, zero)
        dst_ref[c, QB:2 * QB, :] = jnp.where(m0, zero, blk)
        return carry

    lax.fori_loop(0, n_blk, fill, 0)


def _diag_mask(tq, j):
    n = tq - j * QB
    row = lax.broadcasted_iota(jnp.int32, (n, 2 * QB), 0)
    col = lax.broadcasted_iota(jnp.int32, (n, 2 * QB), 1)
    return (col & (QB - 1)) < row


def _tri_blockdiag(upper):
    r = lax.broadcasted_iota(jnp.int32, (2 * QB, 2 * QB), 0)
    c = lax.broadcasted_iota(jnp.int32, (2 * QB, 2 * QB), 1)
    same = (r // QB) == (c // QB)
    u = (same & ((r > c) if upper else (r < c))).astype(BF16)
    return jnp.concatenate([u, u], axis=0)


def _attn_tiles(T, n_seq):
    S = T // n_seq
    tq = ATT_TQ if S % ATT_TQ == 0 else 2 * QB
    assert S % tq == 0
    return S, tq, tq // QB, S // tq, S // QB


def _attn_fwd(qkv, n_seq):
    T = qkv.shape[0]
    S, tq, r, n_q, n_k = _attn_tiles(T, n_seq)
    n_p = D_MODEL // LANES
    u_suffix = _tri_blockdiag(True)

    def body(q_ref, k_ref, v_ref, u_ref, o_ref, a_hbm, kk_ref, vv_ref, lr_s, acc_s, a_stage, sems):
        qi = pl.program_id(2)
        group = (pl.program_id(0) * n_p + pl.program_id(1)) * n_q + qi

        @pl.when(qi == 0)
        def _():
            _stack_heads(k_ref, kk_ref, n_k)
            _stack_heads(v_ref, vv_ref, n_k)

        u = u_ref[...]
        lr_s[...] = jnp.zeros_like(lr_s)
        acc_s[...] = jnp.zeros_like(acc_s)
        base = ((qi * (qi + 1)) // 2) % 2

        def saves(first_kj, half):
            return [pltpu.make_async_copy(a_stage.at[half, j], a_hbm.at[group, first_kj - j], sems.at[half])
                    for j in range(r)]

        @pl.when(qi >= 2)
        def _():
            for cp in saves(0, base):
                cp.wait()

        a_stage[base] = jnp.zeros_like(a_stage[0])

        def step(kj, half, j, rows, q, mask, lr, acc):
            ls, lk = _log_gates(_dot(q, kk_ref[kj], NT))
            if mask is not None:
                lk = jnp.where(mask, lk, 0.0)
            a = jnp.exp(ls + _cumsum_mm(lk, u) + lr)
            if mask is not None:
                a = jnp.where(mask, a, 0.0)
            a = a.astype(BF16)
            a_stage[half, j, rows, :] = a
            return lr + _half_rowsum(lk), acc + _dot(a, vv_ref[kj], NN)

        last = (qi + 1) * r - 1
        for n in range(r):
            rows = slice((r - 1 - n) * QB, tq)
            lr, acc = step(last - n, base, n, rows, q_ref[rows, :] * ATT_SCALE, _diag_mask(tq, r - 1 - n),
                           lr_s[rows, :], acc_s[rows, :])
            lr_s[rows, :] = lr
            acc_s[rows, :] = acc
        for cp in saves(last, base):
            cp.start()

        q = q_ref[...] * ATT_SCALE

        def off(it, carry):
            half = (base + it + 1) % 2
            first = (qi - it) * r - 1
            for cp in saves(first, half):
                cp.wait()
            lr, acc = lr_s[...], acc_s[...]
            for j in range(r):
                lr, acc = step(first - j, half, j, slice(0, tq), q, None, lr, acc)
            lr_s[...] = lr
            acc_s[...] = acc
            for cp in saves(first, half):
                cp.start()
            return carry

        lax.fori_loop(0, qi, off, 0)

        @pl.when(qi == n_q - 1)
        def _():
            for cp in saves(0, (base + qi) % 2):
                cp.wait()
            if n_q > 1:
                for cp in saves(0, (base + qi + 1) % 2):
                    cp.wait()

        o_ref[...] = acc_s[...].astype(BF16)

    return pl.pallas_call(
        body, name="attn_fwd", grid=(n_seq, n_p, n_q),
        in_specs=[pl.BlockSpec((tq, LANES), lambda b, p, qi: (b * n_q + qi, p)),
                  pl.BlockSpec((S, LANES), lambda b, p, qi: (b, n_p + p)),
                  pl.BlockSpec((S, LANES), lambda b, p, qi: (b, 2 * n_p + p)),
                  pl.BlockSpec((4 * QB, 2 * QB), lambda b, p, qi: (0, 0))],
        out_specs=[pl.BlockSpec((tq, LANES), lambda b, p, qi: (b * n_q + qi, p)), ANY],
        out_shape=[jax.ShapeDtypeStruct((T, D_MODEL), BF16),
                   jax.ShapeDtypeStruct((n_seq * n_p * n_q, n_k, tq, 2 * QB), BF16)],
        scratch_shapes=[pltpu.VMEM((n_k, 2 * QB, LANES), BF16), pltpu.VMEM((n_k, 2 * QB, LANES), BF16),
                        pltpu.VMEM((tq, 2 * QB), F32), pltpu.VMEM((tq, LANES), F32),
                        pltpu.VMEM((2, r, tq, 2 * QB), BF16), pltpu.SemaphoreType.DMA((2,))],
        compiler_params=_cp("parallel", "parallel", "arbitrary"),
    )(qkv, qkv, qkv, u_suffix)


def _attn_bwd(qkv, do, a_saved, n_seq):
    T = qkv.shape[0]
    S, tq, r, n_q, n_k = _attn_tiles(T, n_seq)
    n_p = D_MODEL // LANES
    u_prefix = _tri_blockdiag(False)[:2 * QB]

    def body(q_ref, k_ref, v_ref, do_ref, u_ref, a_hbm, dq_ref, dk_out, dv_out,
             kk_ref, vv_ref, cg_s, dq_s, dk_ref, dv_ref, a_stage, sems):
        qi = pl.program_id(2)
        group = (pl.program_id(0) * n_p + pl.program_id(1)) * n_q + qi

        base = ((qi * (qi + 1)) // 2) % 2

        def fetches(grp, g, half):
            return [pltpu.make_async_copy(a_hbm.at[grp, g * r + j], a_stage.at[half, j], sems.at[half])
                    for j in range(r)]

        @pl.when(qi == 0)
        def _():
            for cp in fetches(group, 0, 0):
                cp.start()

        @pl.when(qi == 0)
        def _():
            _stack_heads(k_ref, kk_ref, n_k)
            _stack_heads(v_ref, vv_ref, n_k)
            dk_ref[...] = jnp.zeros_like(dk_ref)
            dv_ref[...] = jnp.zeros_like(dv_ref)

        u = u_ref[...]
        m0 = lax.broadcasted_iota(jnp.int32, (1, LANES), 1) < HEAD_DIM
        cg_s[...] = jnp.zeros_like(cg_s)
        dq_s[...] = jnp.zeros_like(dq_s)

        def step(kj, half, j, rows, q, dov, mask, cg, dq):
            kk = kk_ref[kj]
            beta = _sigmoid(_dot(q, kk, NT))
            a = a_stage[half, j, rows, :]
            g = a.astype(F32) * _dot(dov, vv_ref[kj], NT)
            dz = g - (g + _dot(g.astype(BF16), u, NN) + cg) * beta
            if mask is not None:
                dz = jnp.where(mask, dz, 0.0)
            dz = dz.astype(BF16)
            keys = pl.ds(pl.multiple_of(kj * QB, QB), QB)
            dvt = _dot(a, dov, TN)
            dv_ref[keys, :] += jnp.where(m0, dvt[:QB], dvt[QB:])
            dkt = _dot(dz, q, TN)
            dk_ref[keys, :] += jnp.where(m0, dkt[:QB], dkt[QB:])
            return cg + _half_rowsum(g), dq + _dot(dz, kk, NN)

        q = q_ref[...] * ATT_SCALE
        dov = do_ref[...]

        def off(it, carry):
            half = (base + it) % 2
            for cp in fetches(group, it, half):
                cp.wait()
            for cp in fetches(group, it + 1, 1 - half):
                cp.start()
            cg, dq = cg_s[...], dq_s[...]
            for j in range(r):
                cg, dq = step(it * r + j, half, j, slice(0, tq), q, dov, None, cg, dq)
            cg_s[...] = cg
            dq_s[...] = dq
            return carry

        lax.fori_loop(0, qi, off, 0)

        half = (base + qi) % 2
        for cp in fetches(group, qi, half):
            cp.wait()

        @pl.when(qi < n_q - 1)
        def _():
            for cp in fetches(group + 1, 0, 1 - half):
                cp.start()

        for j in range(r):
            rows = slice(j * QB, tq)
            cg, dq = step(qi * r + j, half, j, rows, q_ref[rows, :] * ATT_SCALE, do_ref[rows, :],
                          _diag_mask(tq, j), cg_s[rows, :], dq_s[rows, :])
            cg_s[rows, :] = cg
            dq_s[rows, :] = dq
        dq_ref[...] = (dq_s[...] * ATT_SCALE).astype(BF16)

        @pl.when(qi == n_q - 1)
        def _():
            dk_out[...] = dk_ref[...].astype(BF16)
            dv_out[...] = dv_ref[...].astype(BF16)

    qspec = pl.BlockSpec((tq, LANES), lambda b, p, qi: (b * n_q + qi, p))
    seq = lambda off: pl.BlockSpec((S, LANES), lambda b, p, qi: (b, off + p))
    return pl.pallas_call(
        body, name="attn_bwd", grid=(n_seq, n_p, n_q),
        in_specs=[qspec, seq(n_p), seq(2 * n_p), qspec,
                  pl.BlockSpec((2 * QB, 2 * QB), lambda b, p, qi: (0, 0)), ANY],
        out_specs=[qspec, seq(0), seq(0)],
        out_shape=[jax.ShapeDtypeStruct((T, D_MODEL), BF16)] * 3,
        scratch_shapes=[pltpu.VMEM((n_k, 2 * QB, LANES), BF16), pltpu.VMEM((n_k, 2 * QB, LANES), BF16),
                        pltpu.VMEM((tq, 2 * QB), F32), pltpu.VMEM((tq, LANES), F32),
                        pltpu.VMEM((S, LANES), F32), pltpu.VMEM((S, LANES), F32),
                        pltpu.VMEM((2, r, tq, 2 * QB), BF16), pltpu.SemaphoreType.DMA((2,))],
        compiler_params=_cp("parallel", "parallel", "arbitrary"),
    )(qkv, qkv, qkv, do, u_prefix, a_saved)


def _shifted_copies(sh_ref):
    rows = sh_ref.shape[1] - SUBLANES
    for s in range(1, SUBLANES):
        sh_ref[s, 0:rows, :] = sh_ref[0, s:s + rows, :]


def _shifted(sh_ref, start, n):
    s = start % SUBLANES
    return sh_ref[s, start - s:start - s + n, :]


def _glu_with_halo(av_ref, ag_ref, avh_ref, agh_ref, a0_s, first, ts):
    hal = avh_ref[...] * _sigmoid(agh_ref[...])
    a0_s[0, 0:HALO, :] = jnp.where(first, 0.0, hal)
    a0_s[0, HALO:HALO + ts, :] = av_ref[...] * _sigmoid(ag_ref[...])
    _shifted_copies(a0_s)


def _mix_specs(ts, n_r, with_left):
    blk = lambda c: pl.BlockSpec((ts, CA), lambda b, r: (b * n_r + r, c))
    per = ts // HALO
    left = lambda c: pl.BlockSpec((HALO, CA), lambda b, r: (jnp.maximum((b * n_r + r) * per - 1, 0), c))
    return blk, (left if with_left else None)


def _mix_fwd(z, conv_w, conv_b, ln_a_g, ln_a_b, ln_v_g, ln_v_b, ws, bias2d, n_seq):
    T = z.shape[0]
    S = T // n_seq
    ts = _tile(S, 512)
    n_r = S // ts
    shift = HALO - (CONV_WIDTH - 1)

    def body(av_ref, ag_ref, avh_ref, agh_ref, u_ref, v_ref, cw_ref, cb_ref, lag_ref, lab_ref,
             lvg_ref, lvb_ref, ws_ref, bias_ref, cat_ref, a1_ref, a0_s):
        _glu_with_halo(av_ref, ag_ref, avh_ref, agh_ref, a0_s, pl.program_id(1) == 0, ts)
        for rb in range(ts // CONV_ROWS):
            base = rb * CONV_ROWS
            acc = jnp.broadcast_to(cb_ref[...], (CONV_ROWS, CA))
            for k in range(CONV_WIDTH):
                acc = acc + cw_ref[k:k + 1, :] * _shifted(a0_s, base + shift + k, CONV_ROWS)
            a1_ref[base:base + CONV_ROWS, :] = acc
        y, _, _ = _ln_fwd(a1_ref[...], lag_ref[...], lab_ref[...])
        cat_ref[:, 0:CA] = (y * _sigmoid(y)).astype(BF16)
        for gi in range(GB):
            sl = slice(gi * DB, (gi + 1) * DB)
            v1, _, _ = _ln_fwd(v_ref[:, sl], lvg_ref[:, sl], lvb_ref[:, sl])
            v1 = v1.astype(BF16)
            for c in range(ts // CHUNK):
                rs = slice(c * CHUNK, (c + 1) * CHUNK)
                v2 = _dot(ws_ref[gi], v1[rs], NN) + bias_ref[:, sl]
                cat_ref[rs, CA + gi * DB:CA + (gi + 1) * DB] = (u_ref[rs, sl] * v2).astype(BF16)

    blk, left = _mix_specs(ts, n_r, True)
    vec = pl.BlockSpec((1, CA), lambda b, r: (0, 0))
    return pl.pallas_call(
        body, name="mix_fwd", grid=(n_seq, n_r),
        in_specs=[blk(0), blk(1), left(0), left(1), blk(2), blk(3),
                  pl.BlockSpec((CONV_WIDTH, CA), lambda b, r: (0, 0)), vec, vec, vec, vec, vec,
                  pl.BlockSpec((GB, CHUNK, CHUNK), lambda b, r: (0, 0, 0)),
                  pl.BlockSpec((CHUNK, CB), lambda b, r: (0, 0))],
        out_specs=[pl.BlockSpec((ts, CA + CB), lambda b, r: (b * n_r + r, 0)), blk(0)],
        out_shape=[jax.ShapeDtypeStruct((T, CA + CB), BF16), jax.ShapeDtypeStruct((T, CA), F32)],
        scratch_shapes=[pltpu.VMEM((SUBLANES, HALO + ts, CA), F32)],
        compiler_params=_cp("parallel", "parallel"),
    )(z, z, z, z, z, z, conv_w, conv_b, ln_a_g, ln_a_b, ln_v_g, ln_v_b, ws, bias2d)


def _mix_bwd_rows(dcat, z, a1, ln_a_g, ln_a_b, ln_v_g, ln_v_b, ws, ws_t, bias2d, n_seq):
    T = z.shape[0]
    S = T // n_seq
    ts = _tile(S, 512)
    n_r = S // ts

    def body(dc_ref, u_ref, v_ref, a1_ref, lag_ref, lab_ref, lvg_ref, lvb_ref, ws_ref, wst_ref, bias_ref,
             da1_ref, dz_ref, dlag_ref, dlab_ref, dlvg_ref, dlvb_ref, dws_ref, dsb_ref, dv1_s, dbias_s):
        first = (pl.program_id(0) == 0) & (pl.program_id(1) == 0)
        last = (pl.program_id(0) == n_seq - 1) & (pl.program_id(1) == n_r - 1)

        @pl.when(first)
        def _():
            for ref in (dlag_ref, dlab_ref, dlvg_ref, dlvb_ref, dws_ref, dbias_s):
                ref[...] = jnp.zeros_like(ref)

        lag = lag_ref[...]
        y, xh, r = _ln_fwd(a1_ref[...], lag, lab_ref[...])
        sig = _sigmoid(y)
        dy = dc_ref[:, 0:CA] * (sig * (1.0 + y * (1.0 - sig)))
        dlag_ref[...] += jnp.sum(dy * xh, axis=0, keepdims=True)
        dlab_ref[...] += jnp.sum(dy, axis=0, keepdims=True)
        da1_ref[...] = _ln_bwd(dy, xh, r, lag)

        tril = (lax.broadcasted_iota(jnp.int32, (CHUNK, CHUNK), 0)
                >= lax.broadcasted_iota(jnp.int32, (CHUNK, CHUNK), 1))
        for gi in range(GB):
            sl = slice(gi * DB, (gi + 1) * DB)
            lvg = lvg_ref[:, sl]
            v1, vh, vr = _ln_fwd(v_ref[:, sl], lvg, lvb_ref[:, sl])
            v1 = v1.astype(BF16)
            for c in range(ts // CHUNK):
                rs = slice(c * CHUNK, (c + 1) * CHUNK)
                v2 = _dot(ws_ref[gi], v1[rs], NN) + bias_ref[:, sl]
                dbo = dc_ref[rs, CA + gi * DB:CA + (gi + 1) * DB]
                dz_ref[rs, sl] = (dbo * v2).astype(BF16)
                dv2 = dbo * u_ref[rs, sl]
                dbias_s[:, sl] += dv2
                dv2b = dv2.astype(BF16)
                dws_ref[gi] += jnp.where(tril, _dot(dv2b, v1[rs], NT), 0.0)
                dv1_s[rs, :] = _dot(wst_ref[gi], dv2b, NN)
            dv1 = dv1_s[...]
            dlvg_ref[:, sl] += jnp.sum(dv1 * vh, axis=0, keepdims=True)
            dlvb_ref[:, sl] += jnp.sum(dv1, axis=0, keepdims=True)
            dz_ref[:, CB + gi * DB:CB + (gi + 1) * DB] = _ln_bwd(dv1, vh, vr, lvg).astype(BF16)

        @pl.when(last)
        def _():
            col = lax.broadcasted_iota(jnp.int32, (CHUNK, GB), 1)
            out = jnp.zeros((CHUNK, GB), F32)
            for gi in range(GB):
                s = jnp.sum(dbias_s[:, gi * DB:(gi + 1) * DB], axis=1, keepdims=True)
                out = out + jnp.where(col == gi, s, 0.0)
            dsb_ref[...] = out

    blk, _ = _mix_specs(ts, n_r, False)
    vec = pl.BlockSpec((1, CA), lambda b, r: (0, 0))
    mat = pl.BlockSpec((GB, CHUNK, CHUNK), lambda b, r: (0, 0, 0))
    wide = pl.BlockSpec((ts, CA + CB), lambda b, r: (b * n_r + r, 0))
    return pl.pallas_call(
        body, name="mix_bwd_rows", grid=(n_seq, n_r),
        in_specs=[wide, blk(2), blk(3), blk(0), vec, vec, vec, vec, mat, mat,
                  pl.BlockSpec((CHUNK, CB), lambda b, r: (0, 0))],
        out_specs=[blk(0), wide, vec, vec, vec, vec, mat, pl.BlockSpec((CHUNK, GB), lambda b, r: (0, 0))],
        out_shape=[jax.ShapeDtypeStruct((T, CA), F32), jax.ShapeDtypeStruct((T, 2 * CB), BF16)]
        + [jax.ShapeDtypeStruct((1, CA), F32)] * 4
        + [jax.ShapeDtypeStruct((GB, CHUNK, CHUNK), F32), jax.ShapeDtypeStruct((CHUNK, GB), F32)],
        scratch_shapes=[pltpu.VMEM((ts, DB), F32), pltpu.VMEM((CHUNK, CB), F32)],
        compiler_params=_cp("arbitrary", "arbitrary"),
    )(dcat, z, z, a1, ln_a_g, ln_a_b, ln_v_g, ln_v_b, ws, ws_t, bias2d)


def _mix_bwd_conv(da1, z, conv_w, n_seq):
    T = z.shape[0]
    S = T // n_seq
    ts = _tile(S, 512)
    n_r = S // ts
    per = ts // HALO
    shift = HALO - (CONV_WIDTH - 1)
    fold = CONV_ROWS // 8

    def body(d_ref, dh_ref, av_ref, ag_ref, avh_ref, agh_ref, cw_ref,
             dz_ref, dcw_ref, dcb_ref, a0_s, d1_s, da0_s, dw8_s):
        first = (pl.program_id(0) == 0) & (pl.program_id(1) == 0)
        last = (pl.program_id(0) == n_seq - 1) & (pl.program_id(1) == n_r - 1)

        @pl.when(first)
        def _():
            dw8_s[...] = jnp.zeros_like(dw8_s)
            dcb_ref[...] = jnp.zeros_like(dcb_ref)

        _glu_with_halo(av_ref, ag_ref, avh_ref, agh_ref, a0_s, pl.program_id(1) == 0, ts)
        d1_s[0, 0:ts, :] = d_ref[...]
        d1_s[0, ts:ts + HALO, :] = jnp.where(pl.program_id(1) == n_r - 1, 0.0, dh_ref[...])
        _shifted_copies(d1_s)
        dcb_ref[...] += jnp.sum(d_ref[...], axis=0, keepdims=True)
        for rb in range(ts // CONV_ROWS):
            base = rb * CONV_ROWS
            dcur = d1_s[0, base:base + CONV_ROWS, :]
            acc = jnp.zeros((CONV_ROWS, CA), F32)
            for k in range(CONV_WIDTH):
                back = CONV_WIDTH - 1 - k
                acc = acc + cw_ref[k:k + 1, :] * _shifted(d1_s, base + back, CONV_ROWS)
                prod = dcur * _shifted(a0_s, base + shift + k, CONV_ROWS)
                part = prod[0:8]
                for f in range(1, fold):
                    part = part + prod[8 * f:8 * f + 8]
                dw8_s[k] += part
            da0_s[base:base + CONV_ROWS, :] = acc
        da0 = da0_s[...]
        sig = _sigmoid(ag_ref[...])
        dz_ref[:, 0:CA] = (da0 * sig).astype(BF16)
        dz_ref[:, CA:2 * CA] = (da0 * av_ref[...] * sig * (1.0 - sig)).astype(BF16)

        @pl.when(last)
        def _():
            for k in range(CONV_WIDTH):
                dcw_ref[k:k + 1, :] = jnp.sum(dw8_s[k], axis=0, keepdims=True)

    blk, left = _mix_specs(ts, n_r, True)
    n_halo_blocks = T // HALO
    right = pl.BlockSpec((HALO, CA), lambda b, r: (jnp.minimum((b * n_r + r + 1) * per, n_halo_blocks - 1), 0))
    return pl.pallas_call(
        body, name="mix_bwd_conv", grid=(n_seq, n_r),
        in_specs=[blk(0), right, blk(0), blk(1), left(0), left(1),
                  pl.BlockSpec((CONV_WIDTH, CA), lambda b, r: (0, 0))],
        out_specs=[pl.BlockSpec((ts, 2 * CA), lambda b, r: (b * n_r + r, 0)),
                   pl.BlockSpec((CONV_WIDTH, CA), lambda b, r: (0, 0)), pl.BlockSpec((1, CA), lambda b, r: (0, 0))],
        out_shape=[jax.ShapeDtypeStruct((T, 2 * CA), BF16), jax.ShapeDtypeStruct((CONV_WIDTH, CA), F32),
                   jax.ShapeDtypeStruct((1, CA), F32)],
        scratch_shapes=[pltpu.VMEM((SUBLANES, HALO + ts, CA), F32), pltpu.VMEM((SUBLANES, ts + HALO, CA), F32),
                        pltpu.VMEM((ts, CA), F32), pltpu.VMEM((CONV_WIDTH, 8, CA), F32)],
        compiler_params=_cp("arbitrary", "arbitrary"),
    )(da1, da1, z, z, z, z, conv_w)


def _row_tile(R, want):
    t = min(R, want)
    t -= t % 8
    while t > 8 and R % t:
        t -= 8
    return t if t >= 8 and R % t == 0 else R


def _adam_step(w, g, m, v):
    nm = ADAM_B1 * m + (1.0 - ADAM_B1) * g
    nv = ADAM_B2 * v + (1.0 - ADAM_B2) * (g * g)
    m_hat = nm / (1.0 - ADAM_B1 ** ADAM_STEP)
    v_hat = nv / (1.0 - ADAM_B2 ** ADAM_STEP)
    return -ADAM_LR * (m_hat / (jnp.sqrt(v_hat) + ADAM_EPS) + ADAM_WD * w), nm, nv


def _adamw_parts(parts, w, m, v, name):
    L, n, C = w.shape
    assert len(parts) == L
    tr = _row_tile(n, 192)
    n_i = n // tr

    def body(*refs):
        p_refs = refs[:L]
        w_ref, m_ref, v_ref, g_ref, d_ref, nm_ref, nv_ref = refs[L:]
        for k in range(L):
            @pl.when(pl.program_id(0) == k)
            def _(k=k):
                acc = p_refs[k][0].astype(F32)
                for s in range(1, N_DEV):
                    acc = acc + p_refs[k][s].astype(F32)
                g_ref[...] = acc

        d_ref[...], nm_ref[...], nv_ref[...] = _adam_step(w_ref[...], g_ref[...], m_ref[...], v_ref[...])

    def part_spec(k):
        return pl.BlockSpec((N_DEV, tr, C),
                            lambda l, i: (0, jnp.where(l == k, i, jnp.where(l < k, 0, n_i - 1)), 0))

    blk = pl.BlockSpec((None, tr, C), lambda l, i: (l, i, 0))
    return pl.pallas_call(
        body, name=name, grid=(L, n_i),
        in_specs=[part_spec(k) for k in range(L)] + [blk] * 3, out_specs=[blk] * 4,
        out_shape=[jax.ShapeDtypeStruct((L, n, C), F32)] * 4,
        compiler_params=_cp("arbitrary", "arbitrary"),
    )(*parts, w, m, v)


def _adamw(w, g, m, v, name):
    R, C = w.shape
    tr = _row_tile(R, 256)

    def body(w_ref, g_ref, m_ref, v_ref, d_ref, nm_ref, nv_ref):
        d_ref[...], nm_ref[...], nv_ref[...] = _adam_step(w_ref[...], g_ref[...], m_ref[...], v_ref[...])

    blk = pl.BlockSpec((tr, C), lambda i: (i, 0))
    return pl.pallas_call(
        body, name=name, grid=(R // tr,),
        in_specs=[blk] * 4, out_specs=[blk] * 3,
        out_shape=[jax.ShapeDtypeStruct((R, C), F32)] * 3,
        compiler_params=_cp("parallel"),
    )(w, g, m, v)


def _me():
    return lax.axis_index("x"), lax.axis_index("y"), lax.axis_index("c")


def _block_rows(ref, dev, n):
    start = (4 * dev[0] + 2 * dev[1] + dev[2]) * n
    if len(ref.shape) == 2:
        return ref.at[pl.ds(start, n), :]
    return ref.at[:, pl.ds(start, n), :]


def _all_gather(shards):
    na = len(shards)
    ns = [s.shape[-2] for s in shards]

    def body(*refs):
        ins, outs = refs[:na], refs[na:2 * na]
        send_sems, recv_sems, local_sems = refs[2 * na:]
        x, y, c = _me()
        me, sibling = (x, y, c), (x, y, 1 - c)
        chips = [(1 - x, y), (x, 1 - y), (1 - x, 1 - y)]

        def copy(a, k, block, to, src=None):
            dst = _block_rows(outs[a], block, ns[a])
            return pltpu.make_async_remote_copy(
                src_ref=dst if src is None else src, dst_ref=dst,
                send_sem=send_sems.at[a, k], recv_sem=recv_sems.at[a, k], device_id=to, device_id_type=MESH)

        mine = [pltpu.make_async_copy(ins[a], _block_rows(outs[a], me, ns[a]), local_sems.at[a]) for a in range(na)]
        for cp in mine:
            cp.start()
        first = []
        for a in range(na):
            first.append(copy(a, 0, me, sibling, src=ins[a]))
            first += [copy(a, 1 + j, me, (*chip, c), src=ins[a]) for j, chip in enumerate(chips)]
        for cp in first:
            cp.start()
        passed = []
        for j, chip in enumerate(chips):
            for a in range(na):
                copy(a, 1 + j, (*chip, c), me).wait_recv()
                fwd = copy(a, 4 + j, (*chip, c), sibling)
                fwd.start()
                passed.append(fwd)
        for a in range(na):
            copy(a, 0, sibling, me).wait_recv()
            for j, chip in enumerate(chips):
                copy(a, 4 + j, (*chip, 1 - c), me).wait_recv()
        for cp in first + passed:
            cp.wait_send()
        for cp in mine:
            cp.wait()

    out_shape = [jax.ShapeDtypeStruct(s.shape[:-2] + (N_DEV * s.shape[-2], s.shape[-1]), s.dtype) for s in shards]
    return pl.pallas_call(
        body, name="weights_all_gather",
        in_specs=[ANY] * na, out_specs=[ANY] * na, out_shape=out_shape,
        scratch_shapes=[pltpu.SemaphoreType.DMA((na, 7)), pltpu.SemaphoreType.DMA((na, 7)),
                        pltpu.SemaphoreType.DMA((na,))],
    )(*shards)


def _split_copies(gather, srcs, lands, send_sems, recv_sems, ns):
    x, y, c = _me()
    me = (x, y, c)
    my_slot = 4 * x + 2 * y + c
    copies = []
    for mask in range(1, N_DEV):
        peer = (x ^ (mask >> 2), y ^ ((mask >> 1) & 1), c ^ (mask & 1))
        for a in range(len(srcs)):
            if gather:
                src, dst = srcs[a], _block_rows(lands[a], me, ns[a])
            else:
                src, dst = _block_rows(srcs[a], peer, ns[a]), lands[a].at[my_slot]
            sem = a * (N_DEV - 1) + mask - 1
            copies.append(pltpu.make_async_remote_copy(
                src_ref=src, dst_ref=dst, send_sem=send_sems.at[sem], recv_sem=recv_sems.at[sem],
                device_id=peer, device_id_type=MESH))
    return copies


HBM_SPEC = pl.BlockSpec(memory_space=pltpu.HBM)
SEM_SPEC = pl.BlockSpec(memory_space=pltpu.SEMAPHORE)


def _split_start(gather, srcs, name, dep=None):
    na = len(srcs)
    x, y, c = _me()
    mine = 4 * x + 2 * y + c
    if gather:
        ns = [s.shape[-2] for s in srcs]
        lands = [lax.dynamic_update_slice(
            lax.empty(s.shape[:-2] + (N_DEV * s.shape[-2], s.shape[-1]), s.dtype), s,
            (0,) * (s.ndim - 2) + (mine * s.shape[-2], 0)) for s in srcs]
    else:
        ns = [s.shape[-2] // N_DEV for s in srcs]
        lands = [lax.dynamic_update_slice(
            lax.empty((N_DEV, n, s.shape[-1]), s.dtype),
            lax.dynamic_slice(s, (mine * n, 0), (n, s.shape[-1]))[None], (mine, 0, 0)) for s, n in zip(srcs, ns)]
    n_in = 2 * na + (dep is not None)

    def body(*refs):
        send_sems, recv_sems = refs[n_in], refs[n_in + 1]
        for cp in _split_copies(gather, refs[:na], refs[na:2 * na], send_sems, recv_sems, ns):
            cp.start()
        refs[-1][...] = jnp.zeros_like(refs[-1])

    hbm = lambda a: pltpu.with_memory_space_constraint(a, pltpu.HBM)
    args = [hbm(a) for a in srcs] + [hbm(a) for a in lands] + ([dep] if dep is not None else [])
    out = pl.pallas_call(
        body, name=name,
        in_specs=[HBM_SPEC] * (2 * na) + ([ANY] if dep is not None else []),
        out_specs=[SEM_SPEC, SEM_SPEC] + [HBM_SPEC] * (2 * na) + [pl.BlockSpec(memory_space=pltpu.VMEM)],
        out_shape=[pltpu.SemaphoreType.DMA((na * (N_DEV - 1),)), pltpu.SemaphoreType.DMA((na * (N_DEV - 1),))]
        + [pltpu.HBM(a.shape, a.dtype) for a in srcs + lands] + [jax.ShapeDtypeStruct((8, LANES), F32)],
        input_output_aliases={i: 2 + i for i in range(2 * na)},
        compiler_params=pltpu.CompilerParams(has_side_effects=pltpu.SideEffectType.DATAFLOW_SIDE_EFFECTING),
    )(*args)
    return (gather, ns, out[0], out[1], list(out[2:2 + na]), list(out[2 + na:2 + 2 * na])), out[-1]


def _split_wait(handle, after, name):
    gather, ns, send, recv, srcs, lands = handle
    na = len(srcs)

    def body(*refs):
        send_sems, recv_sems = refs[2 * na], refs[2 * na + 1]
        for cp in _split_copies(gather, refs[:na], refs[na:2 * na], send_sems, recv_sems, ns):
            cp.wait_send()
            cp.wait_recv()

    out = pl.pallas_call(
        body, name=name,
        in_specs=[HBM_SPEC] * (2 * na) + [SEM_SPEC, SEM_SPEC, ANY],
        out_specs=[HBM_SPEC] * (2 * na),
        out_shape=[pltpu.HBM(a.shape, a.dtype) for a in srcs + lands],
        input_output_aliases={i: i for i in range(2 * na)},
        compiler_params=pltpu.CompilerParams(has_side_effects=pltpu.SideEffectType.DATAFLOW_SIDE_EFFECTING),
    )(*srcs, *lands, send, recv, after)
    return list(out[:na]), list(out[na:])


def _small_all_reduce(buf):
    R = buf.shape[0]

    def body(b_ref, o_ref, recv_ref, send_sems, recv_sems):
        x, y, c = _me()
        my_slot = 4 * x + 2 * y + c
        recv_ref[my_slot] = b_ref[...]
        copies = []
        for mask in range(1, N_DEV):
            peer = (x ^ (mask >> 2), y ^ ((mask >> 1) & 1), c ^ (mask & 1))
            copies.append(pltpu.make_async_remote_copy(
                src_ref=b_ref, dst_ref=recv_ref.at[my_slot],
                send_sem=send_sems.at[mask - 1], recv_sem=recv_sems.at[mask - 1],
                device_id=peer, device_id_type=MESH))
        for cp in copies:
            cp.start()
        for cp in copies:
            cp.wait()
        acc = recv_ref[0]
        for k in range(1, N_DEV):
            acc = acc + recv_ref[k]
        o_ref[...] = acc

    return pl.pallas_call(
        body, name="small_all_reduce",
        in_specs=[pl.BlockSpec(memory_space=pltpu.VMEM)], out_specs=pl.BlockSpec(memory_space=pltpu.VMEM),
        out_shape=jax.ShapeDtypeStruct((R, LANES), F32),
        scratch_shapes=[pltpu.VMEM((N_DEV, R, LANES), F32), pltpu.SemaphoreType.DMA((7,)),
                        pltpu.SemaphoreType.DMA((7,))],
        compiler_params=pltpu.CompilerParams(vmem_limit_bytes=VMEM_LIMIT),
    )(buf)


def _pack(arrays):
    flat = jnp.concatenate([a.reshape(-1) for a in arrays])
    pad = (-flat.shape[0]) % (8 * LANES)
    return jnp.pad(flat, (0, pad)).reshape(-1, LANES)


def _unpack(buf, shapes):
    flat = buf.reshape(-1)
    out, off = [], 0
    for s in shapes:
        n = 1
        for d in s:
            n *= d
        out.append(flat[off:off + n].reshape(s))
        off += n
    return out


def _ffn_index(layer, second):
    return (2 * layer + second) * 3


def kernel(x, g_ffn1, w_ffn1_gate, w_ffn1_up, w_ffn1_down, g_mix, w_in_ab, conv_w, conv_b, ln_a_g, ln_a_b, ln_v_g, ln_v_b, sp_w, sp_b, w_out_ab, w_qkv, w_o, g_ffn2, w_ffn2_gate, w_ffn2_up, w_ffn2_down, g_final, loss_target, m_g_ffn1, m_w_ffn1_gate, m_w_ffn1_up, m_w_ffn1_down, m_g_mix, m_w_in_ab, m_conv_w, m_conv_b, m_ln_a_g, m_ln_a_b, m_ln_v_g, m_ln_v_b, m_sp_w, m_sp_b, m_w_out_ab, m_w_qkv, m_w_o, m_g_ffn2, m_w_ffn2_gate, m_w_ffn2_up, m_w_ffn2_down, m_g_final, v_g_ffn1, v_w_ffn1_gate, v_w_ffn1_up, v_w_ffn1_down, v_g_mix, v_w_in_ab, v_conv_w, v_conv_b, v_ln_a_g, v_ln_a_b, v_ln_v_g, v_ln_v_b, v_sp_w, v_sp_b, v_w_out_ab, v_w_qkv, v_w_o, v_g_ffn2, v_w_ffn2_gate, v_w_ffn2_up, v_w_ffn2_down, v_g_final):
    n_seq, S, D = x.shape
    T = n_seq * S
    depth = g_ffn1.shape[0]
    assert depth == 2 and D == D_MODEL
    my_block = 4 * lax.axis_index("x") + 2 * lax.axis_index("y") + lax.axis_index("c")

    ffn_parts = []
    for l in range(depth):
        for gate, up, down in ((w_ffn1_gate, w_ffn1_up, w_ffn1_down), (w_ffn2_gate, w_ffn2_up, w_ffn2_down)):
            ffn_parts += [gate[l].T, up[l].T, down[l]]
    ffn_shard = lambda k: jnp.stack(ffn_parts[3 * k:3 * k + 3]).astype(BF16)
    conv_w_pad = jnp.zeros((HALO, conv_w.shape[2]), F32).at[:CONV_WIDTH].set(conv_w[0]).T
    w_ffn = [None] * (2 * depth)
    w_ffn[0], conv_w_t = _all_gather([ffn_shard(0), conv_w_pad])
    conv_w_full = conv_w_t.T[:CONV_WIDTH]
    shards_b = [w_out_ab[0].astype(BF16), ffn_shard(1)]
    shards_d = [w_qkv[0].T.astype(BF16), w_o[0].astype(BF16), ffn_shard(3)]
    gather_a, token = _split_start(True, [w_in_ab[0].T.astype(BF16)], "gather_a_start", dep=conv_w_t)
    gather_b, token = _split_start(True, shards_b, "gather_b_start", dep=token)
    gather_c, token = _split_start(True, [ffn_shard(2)], "gather_c_start", dep=token)
    gather_d, token = _split_start(True, shards_d, "gather_d_start", dep=token)

    def gathered(handle, after, name):
        return _split_wait(handle, after, name)[1]

    row = lambda a: a.reshape(1, -1)
    tril = jnp.tril(jnp.ones((CHUNK, CHUNK), dtype=bool))
    ws = jnp.where(tril[None], sp_w[0], 0.0).astype(BF16)
    ws_t = jnp.swapaxes(ws, 1, 2)
    bias2d = jnp.repeat(sp_b[0].T, DB, axis=1)
    conv_b2, lag, lab = row(conv_b[0]), row(ln_a_g[0]), row(ln_a_b[0])
    lvg, lvb = row(ln_v_g[0]), row(ln_v_b[0])

    x0 = x.reshape(T, D)
    target = loss_target.reshape(T, D)
    saved = []
    xc = x0
    for l in range(depth):
        xa, a1, b1, h1 = _ffn_fwd(xc, row(g_ffn1[l]), w_ffn[2 * l], 0, f"ffn1_fwd_{l}", dep=token)
        if l % 2 == 0:
            w_in_t, = gathered(gather_a, xa, "gather_a_wait")
            z, hm = _mm_nt(xa, w_in_t, row(g_mix[l]), F32, "mix_in_proj")
            cat, conv_out = _mix_fwd(z, conv_w_full, conv_b2, lag, lab, lvg, lvb, ws, bias2d, n_seq)
            w_out, w_ffn[1] = gathered(gather_b, cat, "gather_b_wait")
            xb = _mm_nn_res(cat, w_out, xa, "mix_out_proj")
            mixer = (z, hm, cat, conv_out)
        else:
            w_qkv_t, w_o_full, w_ffn[3] = gathered(gather_d, xa, "gather_d_wait")
            qkv, hm = _mm_nt(xa, w_qkv_t, row(g_mix[l]), BF16, "qkv_proj")
            o, att = _attn_fwd(qkv, n_seq)
            xb = _mm_nn_res(o, w_o_full, xa, "attn_out_proj")
            mixer = (qkv, hm, o, att)
        xn, a2, b2, h2 = _ffn_fwd(xb, row(g_ffn2[l]), w_ffn[2 * l + 1], 0, f"ffn2_fwd_{l}")
        saved.append((xc, a1, b1, h1, xa, mixer, xb, a2, b2, h2))
        xc = xn
        if l == 0:
            w_ffn[2], = gathered(gather_c, xc, "gather_c_wait")

    g, loss_part, dg_final = _loss_head(xc, row(g_final), target)

    dg_ffn1, dg_ffn2, dg_mix = [None] * depth, [None] * depth, [None] * depth
    exchanges = {}
    token = None

    def ffn_back(g, xin, gvec, a, b, h, k, tag, token):
        g, dg, da, db, s, gh = _ffn_bwd(g, xin, gvec, a, b, w_ffn[k], 0, f"ffn{tag}_bwd", dep=token)
        if k == 0:
            return g, dg, (da, db, s, gh, h)
        dws = [_mm_tn(da, h, f"dw_gate{tag}"), _mm_tn(db, h, f"dw_up{tag}"), _mm_tn(s, gh, f"dw_down{tag}")]
        exchanges[f"ffn{k}"], token = _split_start(False, dws, f"exchange_ffn{tag}_start")
        return g, dg, token

    for l in reversed(range(depth)):
        xin, a1, b1, h1, xa, mixer, xb, a2, b2, h2 = saved[l]
        g, dg_ffn2[l], token = ffn_back(g, xb, row(g_ffn2[l]), a2, b2, h2, 2 * l + 1, f"2_{l}", token)
        if l % 2 == 0:
            z, hm, cat, conv_out = mixer
            dcat = _mm_nt(g, w_out, None, F32, "mix_out_bwd", dep=token)
            d_w_out = _mm_tn(cat, g, "dw_out")
            (da1, dz_uv, d_lag, d_lab, d_lvg, d_lvb, d_ws, d_sb) = _mix_bwd_rows(
                dcat, z, conv_out, lag, lab, lvg, lvb, ws, ws_t, bias2d, n_seq)
            dz_a, d_cw, d_cb = _mix_bwd_conv(da1, z, conv_w_full, n_seq)
            d_w_in_t = jnp.concatenate([_mm_tn(dz_a, hm, "dw_in_a"), _mm_tn(dz_uv, hm, "dw_in_uv")])
            exchanges["mix"], token = _split_start(False, [d_w_out, d_w_in_t], "exchange_mix_start")
            g, dg_mix[l] = _mm_nn_rmsbwd([dz_a, dz_uv], w_in_t, xa, row(g_mix[l]), g, "mix_in_bwd", dep=token)
        else:
            qkv, hm, o, att = mixer
            do = _mm_nt(g, w_o_full, None, BF16, "attn_out_bwd", dep=token)
            d_w_o = _mm_tn(o, g, "dw_o")
            dq, dk, dv = _attn_bwd(qkv, do, att, n_seq)
            d_w_qkv_t = jnp.concatenate([_mm_tn(dq, hm, "dw_q"), _mm_tn(dk, hm, "dw_k"), _mm_tn(dv, hm, "dw_v")])
            exchanges["attn"], token = _split_start(False, [d_w_o, d_w_qkv_t], "exchange_attn_start")
            g, dg_mix[l] = _mm_nn_rmsbwd([dq, dk, dv], w_qkv_t, xa, row(g_mix[l]), g, "qkv_bwd", dep=token)
        g, dg_ffn1[l], token = ffn_back(g, xin, row(g_ffn1[l]), a1, b1, h1, 2 * l, f"1_{l}", token)
    grad_x = g.reshape(n_seq, S, D)

    small = [jnp.concatenate(dg_ffn1), jnp.concatenate(dg_mix), d_cw, d_cb, d_lag, d_lab, d_lvg, d_lvb,
             jnp.where(tril[None], d_ws, 0.0), d_sb.T, jnp.concatenate(dg_ffn2), dg_final, loss_part[:, :1]]
    small_shapes = [(depth, D), (depth, D), (CONV_WIDTH, CA), (1, CA), (1, CA), (1, CA), (1, GB, DB), (1, GB, DB),
                    (1, GB, CHUNK, CHUNK), (1, GB, CHUNK), (depth, D), (D,), ()]
    small_sum = _small_all_reduce(_pack(small))
    red = _unpack(small_sum, small_shapes)
    (gr_g_ffn1, gr_g_mix, gr_cw_full, gr_cb, gr_lag, gr_lab, gr_lvg, gr_lvb, gr_sp_w, gr_sp_b,
     gr_g_ffn2, gr_g_final, loss) = red
    n_cw = conv_w.shape[2]
    gr_cw = lax.dynamic_slice(gr_cw_full, (0, my_block * n_cw), (CONV_WIDTH, n_cw))[None]

    da, db, s, gh, h = token
    token = small_sum
    for which, lhs, rhs in ((2, s, gh), (1, db, h), (0, da, h)):
        dw = _mm_tn(lhs, rhs, f"dw_ffn0_{which}", dep=token, out_dtype=BF16)
        exchanges[f"ffn0_{which}"], token = _split_start(False, [dw], f"exchange_ffn0_{which}_start")

    def landed(key, after):
        return _split_wait(exchanges[key], after, f"exchange_{key}_wait")[1]

    parts_ffn = [None] * (6 * depth)
    for k in range(1, 2 * depth):
        parts_ffn[3 * k:3 * k + 3] = landed(f"ffn{k}", token)
    parts_out, parts_in = landed("mix", token)
    parts_o, parts_qkv = landed("attn", token)

    grads = {
        "g_ffn1": gr_g_ffn1, "g_mix": gr_g_mix, "conv_w": gr_cw, "conv_b": gr_cb, "ln_a_g": gr_lag,
        "ln_a_b": gr_lab, "ln_v_g": gr_lvg, "ln_v_b": gr_lvb, "sp_w": gr_sp_w, "sp_b": gr_sp_b,
        "g_ffn2": gr_g_ffn2, "g_final": gr_g_final,
    }
    weights = dict(g_ffn1=g_ffn1, w_ffn1_gate=w_ffn1_gate, w_ffn1_up=w_ffn1_up, w_ffn1_down=w_ffn1_down, g_mix=g_mix,
                   w_in_ab=w_in_ab, conv_w=conv_w, conv_b=conv_b, ln_a_g=ln_a_g, ln_a_b=ln_a_b, ln_v_g=ln_v_g,
                   ln_v_b=ln_v_b, sp_w=sp_w, sp_b=sp_b, w_out_ab=w_out_ab, w_qkv=w_qkv, w_o=w_o, g_ffn2=g_ffn2,
                   w_ffn2_gate=w_ffn2_gate, w_ffn2_up=w_ffn2_up, w_ffn2_down=w_ffn2_down, g_final=g_final)
    m_in = dict(g_ffn1=m_g_ffn1, w_ffn1_gate=m_w_ffn1_gate, w_ffn1_up=m_w_ffn1_up, w_ffn1_down=m_w_ffn1_down,
                g_mix=m_g_mix, w_in_ab=m_w_in_ab, conv_w=m_conv_w, conv_b=m_conv_b, ln_a_g=m_ln_a_g, ln_a_b=m_ln_a_b,
                ln_v_g=m_ln_v_g, ln_v_b=m_ln_v_b, sp_w=m_sp_w, sp_b=m_sp_b, w_out_ab=m_w_out_ab, w_qkv=m_w_qkv,
                w_o=m_w_o, g_ffn2=m_g_ffn2, w_ffn2_gate=m_w_ffn2_gate, w_ffn2_up=m_w_ffn2_up,
                w_ffn2_down=m_w_ffn2_down, g_final=m_g_final)
    v_in = dict(g_ffn1=v_g_ffn1, w_ffn1_gate=v_w_ffn1_gate, w_ffn1_up=v_w_ffn1_up, w_ffn1_down=v_w_ffn1_down,
                g_mix=v_g_mix, w_in_ab=v_w_in_ab, conv_w=v_conv_w, conv_b=v_conv_b, ln_a_g=v_ln_a_g, ln_a_b=v_ln_a_b,
                ln_v_g=v_ln_v_g, ln_v_b=v_ln_v_b, sp_w=v_sp_w, sp_b=v_sp_b, w_out_ab=v_w_out_ab, w_qkv=v_w_qkv,
                w_o=v_w_o, g_ffn2=v_g_ffn2, w_ffn2_gate=v_w_ffn2_gate, w_ffn2_up=v_w_ffn2_up,
                w_ffn2_down=v_w_ffn2_down, g_final=v_g_final)
    names = list(weights)
    grads = {n: grads[n].reshape(weights[n].shape) for n in grads}

    delta, new_m, new_v = {}, {}, {}

    def adamw_big(n, parts):
        if weights[n].shape[-1] == D:
            view = back = lambda a: a
        else:
            view = back = lambda a: jnp.swapaxes(a, 1, 2)
        out = _adamw_parts(parts, view(weights[n]), view(m_in[n]), view(v_in[n]), f"adamw_{n}")
        grads[n], delta[n], new_m[n], new_v[n] = [back(a) for a in out]

    adamw_big("w_in_ab", [parts_in])
    adamw_big("w_out_ab", [parts_out])
    adamw_big("w_qkv", [parts_qkv])
    adamw_big("w_o", [parts_o])
    kinds = ("gate", "up", "down")
    for which, kind in enumerate(kinds):
        adamw_big(f"w_ffn2_{kind}", [parts_ffn[_ffn_index(l, 1) + which] for l in range(depth)])
    big = [n for n in names if n.startswith("w_")]
    after = jnp.concatenate([delta[n].reshape(-1)[:1] for n in big if n in delta]).reshape(1, -1)
    for which in (2, 1, 0):
        parts_ffn[which], = landed(f"ffn0_{which}", after)
    for which, kind in enumerate(kinds):
        adamw_big(f"w_ffn1_{kind}", [parts_ffn[_ffn_index(l, 0) + which] for l in range(depth)])
    little = [n for n in names if n not in big]
    shapes = [weights[n].shape for n in little]
    d, nm, nv = _adamw(_pack([weights[n] for n in little]), _pack([grads[n] for n in little]),
                       _pack([m_in[n] for n in little]), _pack([v_in[n] for n in little]), "adamw_small")
    for n, dd, mm, vv in zip(little, _unpack(d, shapes), _unpack(nm, shapes), _unpack(nv, shapes)):
        delta[n], new_m[n], new_v[n] = dd, mm, vv

    return (loss, grad_x, *[grads[n] for n in names], *[delta[n] for n in names],
            *[new_m[n] for n in names], *[new_v[n] for n in names])
```

```python
import functools

import jax
import jax.numpy as jnp
from jax import lax
from jax.experimental import pallas as pl
from jax.experimental.pallas import tpu as pltpu

F32 = jnp.float32
BF16 = jnp.bfloat16

D_MODEL = 1024
CA = 512
CB = 512
GB = 4
DB = 128
CHUNK = 128
CONV_WIDTH = 31
N_HEADS = 16
HEAD_DIM = 64
EPS = 1e-6
N_DEV = 8
LANES = 128
SUBLANES = 8
QB = 128
ATT_TQ = 1024
FFN_TN = 2816
FFN_TM = 256
HALO = 32
CONV_ROWS = 32
ATT_SCALE = HEAD_DIM ** -0.5

ADAM_LR = 0.001
ADAM_B1 = 0.9
ADAM_B2 = 0.999
ADAM_EPS = 1e-08
ADAM_WD = 0.01
ADAM_STEP = 10

NT = (((1,), (1,)), ((), ()))
NN = (((1,), (0,)), ((), ()))
TN = (((0,), (0,)), ((), ()))
MESH = pl.DeviceIdType.MESH
ANY = pl.BlockSpec(memory_space=pl.ANY)
VMEM_LIMIT = 60 * 1024 * 1024


def _dot(a, b, dims):
    return lax.dot_general(a, b, dims, preferred_element_type=F32)


def _cp(*sem):
    return pltpu.CompilerParams(dimension_semantics=sem, vmem_limit_bytes=VMEM_LIMIT)


def _pcall(body, *, in_specs, args, dep=None, **kw):
    if dep is not None:
        n_in = len(in_specs)
        inner = body

        def body(*refs):
            inner(*refs[:n_in], *refs[n_in + 1:])

        in_specs = list(in_specs) + [ANY]
        args = tuple(args) + (dep,)
    return pl.pallas_call(body, in_specs=list(in_specs), **kw)(*args)


def _tile(n, want):
    if n <= want:
        return n
    t = want - want % LANES
    while t > LANES and n % t:
        t -= LANES
    assert n % t == 0, (n, want)
    return t


def _sigmoid(x):
    return 0.5 * jnp.tanh(0.5 * x) + 0.5


def _rstd(x):
    return lax.rsqrt(jnp.mean(x * x, axis=-1, keepdims=True) + EPS)


def _rms_bwd(x, g, dh):
    r = _rstd(x)
    u = dh * g
    dx = r * (u - x * (r * r) * jnp.mean(u * x, axis=-1, keepdims=True))
    dg = jnp.sum(dh * x * r, axis=0, keepdims=True)
    return dx, dg


def _ln_fwd(x, g, b):
    mu = jnp.mean(x, axis=-1, keepdims=True)
    xc = x - mu
    r = lax.rsqrt(jnp.mean(xc * xc, axis=-1, keepdims=True) + EPS)
    xh = xc * r
    return xh * g + b, xh, r


def _ln_bwd(dy, xh, r, g):
    dxh = dy * g
    return r * (dxh - jnp.mean(dxh, axis=-1, keepdims=True)
                - xh * jnp.mean(dxh * xh, axis=-1, keepdims=True))


def _ffn_fwd(x, g, wall, base, name, dep=None):
    T, D = x.shape
    F = wall.shape[1]
    tm, tn = _tile(T, FFN_TM), _tile(F, FFN_TN)
    n_j = F // tn

    def body(x_ref, g_ref, wg_ref, wu_ref, wd_ref, xo_ref, a_ref, b_ref, h_ref, acc_ref):
        j = pl.program_id(1)

        @pl.when(j == 0)
        def _():
            xv = x_ref[...]
            h_ref[...] = (xv * _rstd(xv) * g_ref[...]).astype(BF16)
            acc_ref[...] = jnp.zeros_like(acc_ref)

        h = h_ref[...]
        a = _dot(h, wg_ref[...], NT)
        b = _dot(h, wu_ref[...], NT)
        a_ref[...] = a.astype(BF16)
        b_ref[...] = b.astype(BF16)
        s = (a * _sigmoid(a) * b).astype(BF16)
        acc_ref[...] += _dot(s, wd_ref[...], NN)

        @pl.when(j == n_j - 1)
        def _():
            xo_ref[...] = x_ref[...] + 0.5 * acc_ref[...]

    single = pl.Buffered(1) if n_j == 1 else None
    wspec = lambda k: pl.BlockSpec((None, tn, D), lambda i, j: (base + k, j, 0), pipeline_mode=single)
    return _pcall(
        body, name=name, grid=(T // tm, n_j), dep=dep, args=(x, g, wall, wall, wall),
        in_specs=[pl.BlockSpec((tm, D), lambda i, j: (i, 0)), pl.BlockSpec((1, D), lambda i, j: (0, 0)),
                  wspec(0), wspec(1), wspec(2)],
        out_specs=[pl.BlockSpec((tm, D), lambda i, j: (i, 0)), pl.BlockSpec((tm, tn), lambda i, j: (i, j)),
                   pl.BlockSpec((tm, tn), lambda i, j: (i, j)), pl.BlockSpec((tm, D), lambda i, j: (i, 0))],
        out_shape=[jax.ShapeDtypeStruct((T, D), F32), jax.ShapeDtypeStruct((T, F), BF16),
                   jax.ShapeDtypeStruct((T, F), BF16), jax.ShapeDtypeStruct((T, D), BF16)],
        scratch_shapes=[pltpu.VMEM((tm, D), F32)],
        compiler_params=_cp("parallel", "arbitrary"),
    )


def _ffn_bwd(go, x, g, a, b, wall, base, name, dep=None):
    T, D = x.shape
    F = wall.shape[1]
    tm, tn = _tile(T, FFN_TM), _tile(F, FFN_TN)
    n_j = F // tn

    def body(go_ref, x_ref, g_ref, a_ref, b_ref, wg_ref, wu_ref, wd_ref,
             gx_ref, dg_ref, da_ref, db_ref, s_ref, gh_ref, acc_ref):
        i, j = pl.program_id(0), pl.program_id(1)

        @pl.when(j == 0)
        def _():
            gh_ref[...] = (0.5 * go_ref[...]).astype(BF16)
            acc_ref[...] = jnp.zeros_like(acc_ref)

        @pl.when((i == 0) & (j == 0))
        def _():
            dg_ref[...] = jnp.zeros_like(dg_ref)

        ds = _dot(gh_ref[...], wd_ref[...], NT)
        av = a_ref[...].astype(F32)
        bv = b_ref[...].astype(F32)
        sig = _sigmoid(av)
        sl = av * sig
        dab = ((ds * bv) * (sig + sl * (1.0 - sig))).astype(BF16)
        dbb = (ds * sl).astype(BF16)
        s_ref[...] = (sl * bv).astype(BF16)
        da_ref[...] = dab
        db_ref[...] = dbb
        acc_ref[...] += _dot(dab, wg_ref[...], NN) + _dot(dbb, wu_ref[...], NN)

        @pl.when(j == n_j - 1)
        def _():
            dx, dg = _rms_bwd(x_ref[...], g_ref[...], acc_ref[...])
            gx_ref[...] = go_ref[...] + dx
            dg_ref[...] += dg

    single = pl.Buffered(1) if n_j == 1 else None
    wspec = lambda k: pl.BlockSpec((None, tn, D), lambda i, j: (base + k, j, 0), pipeline_mode=single)
    row = pl.BlockSpec((tm, D), lambda i, j: (i, 0))
    hid = pl.BlockSpec((tm, tn), lambda i, j: (i, j))
    vec = pl.BlockSpec((1, D), lambda i, j: (0, 0))
    return _pcall(
        body, name=name, grid=(T // tm, n_j), dep=dep, args=(go, x, g, a, b, wall, wall, wall),
        in_specs=[row, row, vec, hid, hid, wspec(0), wspec(1), wspec(2)],
        out_specs=[row, vec, hid, hid, hid, row],
        out_shape=[jax.ShapeDtypeStruct((T, D), F32), jax.ShapeDtypeStruct((1, D), F32),
                   jax.ShapeDtypeStruct((T, F), BF16), jax.ShapeDtypeStruct((T, F), BF16),
                   jax.ShapeDtypeStruct((T, F), BF16), jax.ShapeDtypeStruct((T, D), BF16)],
        scratch_shapes=[pltpu.VMEM((tm, D), F32)],
        compiler_params=_cp("arbitrary", "arbitrary"),
    )


def _mm_tn(a, b, name, dep=None, out_dtype=F32):
    T, M = a.shape
    N = b.shape[1]
    tmm, tk = _tile(M, 1536), _tile(T, 2048)
    n_k = T // tk
    narrow = out_dtype != F32

    def body(a_ref, b_ref, o_ref, *scratch):
        acc_ref = scratch[0] if narrow else o_ref

        @pl.when(pl.program_id(1) == 0)
        def _():
            acc_ref[...] = jnp.zeros_like(acc_ref)

        acc_ref[...] += _dot(a_ref[...].astype(BF16), b_ref[...].astype(BF16), TN)
        if narrow:
            @pl.when(pl.program_id(1) == n_k - 1)
            def _():
                o_ref[...] = acc_ref[...].astype(out_dtype)

    return _pcall(
        body, name=name, grid=(M // tmm, n_k), dep=dep, args=(a, b),
        in_specs=[pl.BlockSpec((tk, tmm), lambda m, k: (k, m)), pl.BlockSpec((tk, N), lambda m, k: (k, 0))],
        out_specs=pl.BlockSpec((tmm, N), lambda m, k: (m, 0)),
        out_shape=jax.ShapeDtypeStruct((M, N), out_dtype),
        scratch_shapes=[pltpu.VMEM((tmm, N), F32)] if narrow else [],
        compiler_params=_cp("parallel", "arbitrary"),
    )


def _mm_nt(x, wt, g, out_dtype, name, dep=None):
    T, K = x.shape
    N = wt.shape[0]
    tm, tn = _tile(T, 512), N
    norm = g is not None

    def body(*refs):
        if norm:
            x_ref, g_ref, w_ref, o_ref, h_ref = refs
        else:
            x_ref, w_ref, o_ref, h_ref = refs

        @pl.when(pl.program_id(1) == 0)
        def _():
            xv = x_ref[...].astype(F32)
            if norm:
                xv = xv * _rstd(xv) * g_ref[...]
            h_ref[...] = xv.astype(BF16)

        o_ref[...] = _dot(h_ref[...], w_ref[...], NT).astype(out_dtype)

    row = pl.BlockSpec((tm, K), lambda i, j: (i, 0))
    wsp = pl.BlockSpec((tn, K), lambda i, j: (j, 0))
    osp = pl.BlockSpec((tm, tn), lambda i, j: (i, j))
    if norm:
        return pl.pallas_call(
            body, name=name, grid=(T // tm, N // tn),
            in_specs=[row, pl.BlockSpec((1, K), lambda i, j: (0, 0)), wsp],
            out_specs=[osp, row],
            out_shape=[jax.ShapeDtypeStruct((T, N), out_dtype), jax.ShapeDtypeStruct((T, K), BF16)],
            compiler_params=_cp("parallel", "arbitrary"),
        )(x, g, wt)
    return _pcall(
        body, name=name, grid=(T // tm, N // tn), dep=dep, args=(x, wt),
        in_specs=[row, wsp], out_specs=osp,
        out_shape=jax.ShapeDtypeStruct((T, N), out_dtype),
        scratch_shapes=[pltpu.VMEM((tm, K), BF16)],
        compiler_params=_cp("parallel", "arbitrary"),
    )


def _mm_nn_res(act, w, resid, name):
    T, K = act.shape
    D = w.shape[1]
    tm = _tile(T, 512)

    def body(a_ref, w_ref, r_ref, o_ref):
        o_ref[...] = r_ref[...] + _dot(a_ref[...].astype(BF16), w_ref[...], NN)

    return pl.pallas_call(
        body, name=name, grid=(T // tm,),
        in_specs=[pl.BlockSpec((tm, K), lambda i: (i, 0)), pl.BlockSpec((K, D), lambda i: (0, 0)),
                  pl.BlockSpec((tm, D), lambda i: (i, 0))],
        out_specs=pl.BlockSpec((tm, D), lambda i: (i, 0)),
        out_shape=jax.ShapeDtypeStruct((T, D), F32),
        compiler_params=_cp("parallel"),
    )(act, w, resid)


def _mm_nn_rmsbwd(acts, w, x, g, gprev, name, dep=None):
    T = acts[0].shape[0]
    ks = [a.shape[1] for a in acts]
    K, D = w.shape
    assert sum(ks) == K
    tm = _tile(T, 512)
    na = len(acts)

    def body(*refs):
        a_refs = refs[:na]
        w_ref, x_ref, g_ref, gp_ref, o_ref, dg_ref = refs[na:]

        @pl.when(pl.program_id(0) == 0)
        def _():
            dg_ref[...] = jnp.zeros_like(dg_ref)

        dh, off = None, 0
        for a_ref, k in zip(a_refs, ks):
            part = _dot(a_ref[...].astype(BF16), w_ref[off:off + k, :], NN)
            dh = part if dh is None else dh + part
            off += k
        dx, dg = _rms_bwd(x_ref[...], g_ref[...], dh)
        o_ref[...] = gp_ref[...] + dx
        dg_ref[...] += dg

    row = pl.BlockSpec((tm, D), lambda i: (i, 0))
    vec = pl.BlockSpec((1, D), lambda i: (0, 0))
    return _pcall(
        body, name=name, grid=(T // tm,), dep=dep, args=(*acts, w, x, g, gprev),
        in_specs=[pl.BlockSpec((tm, k), lambda i: (i, 0)) for k in ks]
        + [pl.BlockSpec((K, D), lambda i: (0, 0)), row, vec, row],
        out_specs=[row, vec],
        out_shape=[jax.ShapeDtypeStruct((T, D), F32), jax.ShapeDtypeStruct((1, D), F32)],
        compiler_params=_cp("arbitrary"),
    )


def _loss_head(x, g, target):
    T, D = x.shape
    tm = _tile(T, 512)

    def body(x_ref, g_ref, t_ref, dx_ref, loss_ref, dg_ref):
        @pl.when(pl.program_id(0) == 0)
        def _():
            loss_ref[...] = jnp.zeros_like(loss_ref)
            dg_ref[...] = jnp.zeros_like(dg_ref)

        xv = x_ref[...]
        gv = g_ref[...]
        e = xv * _rstd(xv) * gv - t_ref[...]
        per_tok = jnp.sum(e * e, axis=-1, keepdims=True) * (1.0 / D)
        loss_ref[...] += 0.5 * jnp.sum(per_tok, axis=0, keepdims=True)
        dx, dg = _rms_bwd(xv, gv, e * (1.0 / D))
        dx_ref[...] = dx
        dg_ref[...] += dg

    row = pl.BlockSpec((tm, D), lambda i: (i, 0))
    vec = pl.BlockSpec((1, D), lambda i: (0, 0))
    return pl.pallas_call(
        body, name="loss_head", grid=(T // tm,),
        in_specs=[row, vec, row],
        out_specs=[row, pl.BlockSpec((1, LANES), lambda i: (0, 0)), vec],
        out_shape=[jax.ShapeDtypeStruct((T, D), F32), jax.ShapeDtypeStruct((1, LANES), F32),
                   jax.ShapeDtypeStruct((1, D), F32)],
        compiler_params=_cp("arbitrary"),
    )(x, g, target)


def _log_gates(z):
    ls = jnp.minimum(z, 0.0) - jnp.log(1.0 + jnp.exp(-jnp.abs(z)))
    return ls, ls - z


def _cumsum_mm(v, u2):
    hi = v.astype(BF16)
    lo = (v - hi.astype(F32)).astype(BF16)
    return _dot(jnp.concatenate([hi, lo], axis=1), u2, NN)


def _half_rowsum(v):
    n = v.shape[0]
    s0 = jnp.sum(v[:, :QB], axis=1, keepdims=True)
    s1 = jnp.sum(v[:, QB:], axis=1, keepdims=True)
    return jnp.concatenate([jnp.broadcast_to(s0, (n, QB)), jnp.broadcast_to(s1, (n, QB))], axis=1)


def _stack_heads(src_ref, dst_ref, n_blk):
    m0 = lax.broadcasted_iota(jnp.int32, (1, LANES), 1) < HEAD_DIM

    def fill(c, carry):
        blk = src_ref[pl.ds(pl.multiple_of(c * QB, QB), QB), :]
        zero = jnp.zeros_like(blk)
        dst_ref[c, 0:QB, :] = jnp.where(m0, blk, zero)
        dst_ref[c, QB:2 * QB, :] = jnp.where(m0, zero, blk)
        return carry

    lax.fori_loop(0, n_blk, fill, 0)


def _diag_mask(tq, j):
    n = tq - j * QB
    row = lax.broadcasted_iota(jnp.int32, (n, 2 * QB), 0)
    col = lax.broadcasted_iota(jnp.int32, (n, 2 * QB), 1)
    return (col & (QB - 1)) < row


def _tri_blockdiag(upper):
    r = lax.broadcasted_iota(jnp.int32, (2 * QB, 2 * QB), 0)
    c = lax.broadcasted_iota(jnp.int32, (2 * QB, 2 * QB), 1)
    same = (r // QB) == (c // QB)
    u = (same & ((r > c) if upper else (r < c))).astype(BF16)
    return jnp.concatenate([u, u], axis=0)


def _attn_tiles(T, n_seq):
    S = T // n_seq
    tq = ATT_TQ if S % ATT_TQ == 0 else 2 * QB
    assert S % tq == 0
    return S, tq, tq // QB, S // tq, S // QB


def _attn_fwd(qkv, n_seq):
    T = qkv.shape[0]
    S, tq, r, n_q, n_k = _attn_tiles(T, n_seq)
    n_p = D_MODEL // LANES
    n_steps = n_seq * n_p * n_q
    u_suffix = _tri_blockdiag(True)

    def body(q_ref, k_ref, v_ref, u_ref, o_ref, a_hbm, kk_ref, vv_ref, lr_s, acc_s, a_stage, sems):
        qi = pl.program_id(2)
        group = (pl.program_id(0) * n_p + pl.program_id(1)) * n_q + qi

        @pl.when(qi == 0)
        def _():
            _stack_heads(k_ref, kk_ref, n_k)
            _stack_heads(v_ref, vv_ref, n_k)

        u = u_ref[...]
        lr_s[...] = jnp.zeros_like(lr_s)
        acc_s[...] = jnp.zeros_like(acc_s)
        base = ((group // n_q) * (n_q * (n_q + 1) // 2) + (qi * (qi + 1)) // 2) % 2

        def saves(first_kj, half):
            return [pltpu.make_async_copy(a_stage.at[half, j], a_hbm.at[group, first_kj - j], sems.at[half])
                    for j in range(r)]

        @pl.when(group >= 2)
        def _():
            for cp in saves(0, base):
                cp.wait()

        a_stage[base] = jnp.zeros_like(a_stage[0])

        def step(kj, half, j, rows, q, mask, lr, acc):
            ls, lk = _log_gates(_dot(q, kk_ref[kj], NT))
            if mask is not None:
                lk = jnp.where(mask, lk, 0.0)
            a = jnp.exp(ls + _cumsum_mm(lk, u) + lr)
            if mask is not None:
                a = jnp.where(mask, a, 0.0)
            a = a.astype(BF16)
            a_stage[half, j, rows, :] = a
            return lr + _half_rowsum(lk), acc + _dot(a, vv_ref[kj], NN)

        last = (qi + 1) * r - 1
        for n in range(r):
            rows = slice((r - 1 - n) * QB, tq)
            lr, acc = step(last - n, base, n, rows, q_ref[rows, :] * ATT_SCALE, _diag_mask(tq, r - 1 - n),
                           lr_s[rows, :], acc_s[rows, :])
            lr_s[rows, :] = lr
            acc_s[rows, :] = acc
        for cp in saves(last, base):
            cp.start()

        q = q_ref[...] * ATT_SCALE

        def off(it, carry):
            half = (base + it + 1) % 2
            first = (qi - it) * r - 1
            for cp in saves(first, half):
                cp.wait()
            lr, acc = lr_s[...], acc_s[...]
            for j in range(r):
                lr, acc = step(first - j, half, j, slice(0, tq), q, None, lr, acc)
            lr_s[...] = lr
            acc_s[...] = acc
            for cp in saves(first, half):
                cp.start()
            return carry

        lax.fori_loop(0, qi, off, 0)

        @pl.when(group == n_steps - 1)
        def _():
            for cp in saves(0, (base + qi) % 2):
                cp.wait()
            if n_steps * n_q > 1:
                for cp in saves(0, (base + qi + 1) % 2):
                    cp.wait()

        o_ref[...] = acc_s[...].astype(BF16)

    return pl.pallas_call(
        body, name="attn_fwd", grid=(n_seq, n_p, n_q),
        in_specs=[pl.BlockSpec((tq, LANES), lambda b, p, qi: (b * n_q + qi, p)),
                  pl.BlockSpec((S, LANES), lambda b, p, qi: (b, n_p + p)),
                  pl.BlockSpec((S, LANES), lambda b, p, qi: (b, 2 * n_p + p)),
                  pl.BlockSpec((4 * QB, 2 * QB), lambda b, p, qi: (0, 0))],
        out_specs=[pl.BlockSpec((tq, LANES), lambda b, p, qi: (b * n_q + qi, p)), ANY],
        out_shape=[jax.ShapeDtypeStruct((T, D_MODEL), BF16),
                   jax.ShapeDtypeStruct((n_seq * n_p * n_q, n_k, tq, 2 * QB), BF16)],
        scratch_shapes=[pltpu.VMEM((n_k, 2 * QB, LANES), BF16), pltpu.VMEM((n_k, 2 * QB, LANES), BF16),
                        pltpu.VMEM((tq, 2 * QB), F32), pltpu.VMEM((tq, LANES), F32),
                        pltpu.VMEM((2, r, tq, 2 * QB), BF16), pltpu.SemaphoreType.DMA((2,))],
        compiler_params=_cp("arbitrary", "arbitrary", "arbitrary"),
    )(qkv, qkv, qkv, u_suffix)


def _attn_bwd(qkv, do, a_saved, n_seq):
    T = qkv.shape[0]
    S, tq, r, n_q, n_k = _attn_tiles(T, n_seq)
    n_p = D_MODEL // LANES
    n_steps = n_seq * n_p * n_q
    u_prefix = _tri_blockdiag(False)[:2 * QB]

    def body(q_ref, k_ref, v_ref, do_ref, u_ref, a_hbm, dq_ref, dk_out, dv_out,
             kk_ref, vv_ref, cg_s, dq_s, dk_ref, dv_ref, a_stage, sems):
        qi = pl.program_id(2)
        group = (pl.program_id(0) * n_p + pl.program_id(1)) * n_q + qi

        base = ((group // n_q) * (n_q * (n_q + 1) // 2) + (qi * (qi + 1)) // 2) % 2

        def fetches(grp, g, half):
            return [pltpu.make_async_copy(a_hbm.at[grp, g * r + j], a_stage.at[half, j], sems.at[half])
                    for j in range(r)]

        @pl.when(group == 0)
        def _():
            for cp in fetches(group, 0, 0):
                cp.start()

        @pl.when(qi == 0)
        def _():
            _stack_heads(k_ref, kk_ref, n_k)
            _stack_heads(v_ref, vv_ref, n_k)
            dk_ref[...] = jnp.zeros_like(dk_ref)
            dv_ref[...] = jnp.zeros_like(dv_ref)

        u = u_ref[...]
        m0 = lax.broadcasted_iota(jnp.int32, (1, LANES), 1) < HEAD_DIM
        cg_s[...] = jnp.zeros_like(cg_s)
        dq_s[...] = jnp.zeros_like(dq_s)

        def step(kj, half, j, rows, q, dov, mask, cg, dq):
            kk = kk_ref[kj]
            beta = _sigmoid(_dot(q, kk, NT))
            a = a_stage[half, j, rows, :]
            g = a.astype(F32) * _dot(dov, vv_ref[kj], NT)
            dz = g - (g + _dot(g.astype(BF16), u, NN) + cg) * beta
            if mask is not None:
                dz = jnp.where(mask, dz, 0.0)
            dz = dz.astype(BF16)
            keys = pl.ds(pl.multiple_of(kj * QB, QB), QB)
            dvt = _dot(a, dov, TN)
            dv_ref[keys, :] += jnp.where(m0, dvt[:QB], dvt[QB:])
            dkt = _dot(dz, q, TN)
            dk_ref[keys, :] += jnp.where(m0, dkt[:QB], dkt[QB:])
            return cg + _half_rowsum(g), dq + _dot(dz, kk, NN)

        q = q_ref[...] * ATT_SCALE
        dov = do_ref[...]

        def off(it, carry):
            half = (base + it) % 2
            for cp in fetches(group, it, half):
                cp.wait()
            for cp in fetches(group, it + 1, 1 - half):
                cp.start()
            cg, dq = cg_s[...], dq_s[...]
            for j in range(r):
                cg, dq = step(it * r + j, half, j, slice(0, tq), q, dov, None, cg, dq)
            cg_s[...] = cg
            dq_s[...] = dq
            return carry

        lax.fori_loop(0, qi, off, 0)

        half = (base + qi) % 2
        for cp in fetches(group, qi, half):
            cp.wait()

        @pl.when(group < n_steps - 1)
        def _():
            for cp in fetches(group + 1, 0, 1 - half):
                cp.start()

        for j in range(r):
            rows = slice(j * QB, tq)
            cg, dq = step(qi * r + j, half, j, rows, q_ref[rows, :] * ATT_SCALE, do_ref[rows, :],
                          _diag_mask(tq, j), cg_s[rows, :], dq_s[rows, :])
            cg_s[rows, :] = cg
            dq_s[rows, :] = dq
        dq_ref[...] = (dq_s[...] * ATT_SCALE).astype(BF16)

        @pl.when(qi == n_q - 1)
        def _():
            dk_out[...] = dk_ref[...].astype(BF16)
            dv_out[...] = dv_ref[...].astype(BF16)

    qspec = pl.BlockSpec((tq, LANES), lambda b, p, qi: (b * n_q + qi, p))
    seq = lambda off: pl.BlockSpec((S, LANES), lambda b, p, qi: (b, off + p))
    return pl.pallas_call(
        body, name="attn_bwd", grid=(n_seq, n_p, n_q),
        in_specs=[qspec, seq(n_p), seq(2 * n_p), qspec,
                  pl.BlockSpec((2 * QB, 2 * QB), lambda b, p, qi: (0, 0)), ANY],
        out_specs=[qspec, seq(0), seq(0)],
        out_shape=[jax.ShapeDtypeStruct((T, D_MODEL), BF16)] * 3,
        scratch_shapes=[pltpu.VMEM((n_k, 2 * QB, LANES), BF16), pltpu.VMEM((n_k, 2 * QB, LANES), BF16),
                        pltpu.VMEM((tq, 2 * QB), F32), pltpu.VMEM((tq, LANES), F32),
                        pltpu.VMEM((S, LANES), F32), pltpu.VMEM((S, LANES), F32),
                        pltpu.VMEM((2, r, tq, 2 * QB), BF16), pltpu.SemaphoreType.DMA((2,))],
        compiler_params=_cp("arbitrary", "arbitrary", "arbitrary"),
    )(qkv, qkv, qkv, do, u_prefix, a_saved)


def _shifted_copies(sh_ref):
    rows = sh_ref.shape[1] - SUBLANES
    for s in range(1, SUBLANES):
        sh_ref[s, 0:rows, :] = sh_ref[0, s:s + rows, :]


def _shifted(sh_ref, start, n):
    s = start % SUBLANES
    return sh_ref[s, start - s:start - s + n, :]


def _glu_with_halo(av_ref, ag_ref, avh_ref, agh_ref, a0_s, first, ts):
    hal = avh_ref[...] * _sigmoid(agh_ref[...])
    a0_s[0, 0:HALO, :] = jnp.where(first, 0.0, hal)
    a0_s[0, HALO:HALO + ts, :] = av_ref[...] * _sigmoid(ag_ref[...])
    _shifted_copies(a0_s)


def _mix_specs(ts, n_r, with_left):
    blk = lambda c: pl.BlockSpec((ts, CA), lambda b, r: (b * n_r + r, c))
    per = ts // HALO
    left = lambda c: pl.BlockSpec((HALO, CA), lambda b, r: (jnp.maximum((b * n_r + r) * per - 1, 0), c))
    return blk, (left if with_left else None)


def _mix_fwd(z, conv_w, conv_b, ln_a_g, ln_a_b, ln_v_g, ln_v_b, ws, bias2d, n_seq):
    T = z.shape[0]
    S = T // n_seq
    ts = _tile(S, 512)
    n_r = S // ts
    shift = HALO - (CONV_WIDTH - 1)

    def body(av_ref, ag_ref, avh_ref, agh_ref, u_ref, v_ref, cw_ref, cb_ref, lag_ref, lab_ref,
             lvg_ref, lvb_ref, ws_ref, bias_ref, cat_ref, a1_ref, a0_s):
        _glu_with_halo(av_ref, ag_ref, avh_ref, agh_ref, a0_s, pl.program_id(1) == 0, ts)
        for rb in range(ts // CONV_ROWS):
            base = rb * CONV_ROWS
            acc = jnp.broadcast_to(cb_ref[...], (CONV_ROWS, CA))
            for k in range(CONV_WIDTH):
                acc = acc + cw_ref[k:k + 1, :] * _shifted(a0_s, base + shift + k, CONV_ROWS)
            a1_ref[base:base + CONV_ROWS, :] = acc
        y, _, _ = _ln_fwd(a1_ref[...], lag_ref[...], lab_ref[...])
        cat_ref[:, 0:CA] = (y * _sigmoid(y)).astype(BF16)
        for gi in range(GB):
            sl = slice(gi * DB, (gi + 1) * DB)
            v1, _, _ = _ln_fwd(v_ref[:, sl], lvg_ref[:, sl], lvb_ref[:, sl])
            v1 = v1.astype(BF16)
            for c in range(ts // CHUNK):
                rs = slice(c * CHUNK, (c + 1) * CHUNK)
                v2 = _dot(ws_ref[gi], v1[rs], NN) + bias_ref[:, sl]
                cat_ref[rs, CA + gi * DB:CA + (gi + 1) * DB] = (u_ref[rs, sl] * v2).astype(BF16)

    blk, left = _mix_specs(ts, n_r, True)
    vec = pl.BlockSpec((1, CA), lambda b, r: (0, 0))
    return pl.pallas_call(
        body, name="mix_fwd", grid=(n_seq, n_r),
        in_specs=[blk(0), blk(1), left(0), left(1), blk(2), blk(3),
                  pl.BlockSpec((CONV_WIDTH, CA), lambda b, r: (0, 0)), vec, vec, vec, vec, vec,
                  pl.BlockSpec((GB, CHUNK, CHUNK), lambda b, r: (0, 0, 0)),
                  pl.BlockSpec((CHUNK, CB), lambda b, r: (0, 0))],
        out_specs=[pl.BlockSpec((ts, CA + CB), lambda b, r: (b * n_r + r, 0)), blk(0)],
        out_shape=[jax.ShapeDtypeStruct((T, CA + CB), BF16), jax.ShapeDtypeStruct((T, CA), F32)],
        scratch_shapes=[pltpu.VMEM((SUBLANES, HALO + ts, CA), F32)],
        compiler_params=_cp("parallel", "parallel"),
    )(z, z, z, z, z, z, conv_w, conv_b, ln_a_g, ln_a_b, ln_v_g, ln_v_b, ws, bias2d)


def _mix_bwd_rows(dcat, z, a1, ln_a_g, ln_a_b, ln_v_g, ln_v_b, ws, ws_t, bias2d, n_seq):
    T = z.shape[0]
    S = T // n_seq
    ts = _tile(S, 512)
    n_r = S // ts

    def body(dc_ref, u_ref, v_ref, a1_ref, lag_ref, lab_ref, lvg_ref, lvb_ref, ws_ref, wst_ref, bias_ref,
             da1_ref, dz_ref, dlag_ref, dlab_ref, dlvg_ref, dlvb_ref, dws_ref, dsb_ref, dv1_s, dbias_s):
        first = (pl.program_id(0) == 0) & (pl.program_id(1) == 0)
        last = (pl.program_id(0) == n_seq - 1) & (pl.program_id(1) == n_r - 1)

        @pl.when(first)
        def _():
            for ref in (dlag_ref, dlab_ref, dlvg_ref, dlvb_ref, dws_ref, dbias_s):
                ref[...] = jnp.zeros_like(ref)

        lag = lag_ref[...]
        y, xh, r = _ln_fwd(a1_ref[...], lag, lab_ref[...])
        sig = _sigmoid(y)
        dy = dc_ref[:, 0:CA] * (sig * (1.0 + y * (1.0 - sig)))
        dlag_ref[...] += jnp.sum(dy * xh, axis=0, keepdims=True)
        dlab_ref[...] += jnp.sum(dy, axis=0, keepdims=True)
        da1_ref[...] = _ln_bwd(dy, xh, r, lag)

        tril = (lax.broadcasted_iota(jnp.int32, (CHUNK, CHUNK), 0)
                >= lax.broadcasted_iota(jnp.int32, (CHUNK, CHUNK), 1))
        for gi in range(GB):
            sl = slice(gi * DB, (gi + 1) * DB)
            lvg = lvg_ref[:, sl]
            v1, vh, vr = _ln_fwd(v_ref[:, sl], lvg, lvb_ref[:, sl])
            v1 = v1.astype(BF16)
            for c in range(ts // CHUNK):
                rs = slice(c * CHUNK, (c + 1) * CHUNK)
                v2 = _dot(ws_ref[gi], v1[rs], NN) + bias_ref[:, sl]
                dbo = dc_ref[rs, CA + gi * DB:CA + (gi + 1) * DB]
                dz_ref[rs, sl] = (dbo * v2).astype(BF16)
                dv2 = dbo * u_ref[rs, sl]
                dbias_s[:, sl] += dv2
                dv2b = dv2.astype(BF16)
                dws_ref[gi] += jnp.where(tril, _dot(dv2b, v1[rs], NT), 0.0)
                dv1_s[rs, :] = _dot(wst_ref[gi], dv2b, NN)
            dv1 = dv1_s[...]
            dlvg_ref[:, sl] += jnp.sum(dv1 * vh, axis=0, keepdims=True)
            dlvb_ref[:, sl] += jnp.sum(dv1, axis=0, keepdims=True)
            dz_ref[:, CB + gi * DB:CB + (gi + 1) * DB] = _ln_bwd(dv1, vh, vr, lvg).astype(BF16)

        @pl.when(last)
        def _():
            col = lax.broadcasted_iota(jnp.int32, (CHUNK, GB), 1)
            out = jnp.zeros((CHUNK, GB), F32)
            for gi in range(GB):
                s = jnp.sum(dbias_s[:, gi * DB:(gi + 1) * DB], axis=1, keepdims=True)
                out = out + jnp.where(col == gi, s, 0.0)
            dsb_ref[...] = out

    blk, _ = _mix_specs(ts, n_r, False)
    vec = pl.BlockSpec((1, CA), lambda b, r: (0, 0))
    mat = pl.BlockSpec((GB, CHUNK, CHUNK), lambda b, r: (0, 0, 0))
    wide = pl.BlockSpec((ts, CA + CB), lambda b, r: (b * n_r + r, 0))
    return pl.pallas_call(
        body, name="mix_bwd_rows", grid=(n_seq, n_r),
        in_specs=[wide, blk(2), blk(3), blk(0), vec, vec, vec, vec, mat, mat,
                  pl.BlockSpec((CHUNK, CB), lambda b, r: (0, 0))],
        out_specs=[blk(0), wide, vec, vec, vec, vec, mat, pl.BlockSpec((CHUNK, GB), lambda b, r: (0, 0))],
        out_shape=[jax.ShapeDtypeStruct((T, CA), F32), jax.ShapeDtypeStruct((T, 2 * CB), BF16)]
        + [jax.ShapeDtypeStruct((1, CA), F32)] * 4
        + [jax.ShapeDtypeStruct((GB, CHUNK, CHUNK), F32), jax.ShapeDtypeStruct((CHUNK, GB), F32)],
        scratch_shapes=[pltpu.VMEM((ts, DB), F32), pltpu.VMEM((CHUNK, CB), F32)],
        compiler_params=_cp("arbitrary", "arbitrary"),
    )(dcat, z, z, a1, ln_a_g, ln_a_b, ln_v_g, ln_v_b, ws, ws_t, bias2d)


def _mix_bwd_conv(da1, z, conv_w, n_seq):
    T = z.shape[0]
    S = T // n_seq
    ts = _tile(S, 512)
    n_r = S // ts
    per = ts // HALO
    shift = HALO - (CONV_WIDTH - 1)
    fold = CONV_ROWS // 8

    def body(d_ref, dh_ref, av_ref, ag_ref, avh_ref, agh_ref, cw_ref,
             dz_ref, dcw_ref, dcb_ref, a0_s, d1_s, da0_s, dw8_s):
        first = (pl.program_id(0) == 0) & (pl.program_id(1) == 0)
        last = (pl.program_id(0) == n_seq - 1) & (pl.program_id(1) == n_r - 1)

        @pl.when(first)
        def _():
            dw8_s[...] = jnp.zeros_like(dw8_s)
            dcb_ref[...] = jnp.zeros_like(dcb_ref)

        _glu_with_halo(av_ref, ag_ref, avh_ref, agh_ref, a0_s, pl.program_id(1) == 0, ts)
        d1_s[0, 0:ts, :] = d_ref[...]
        d1_s[0, ts:ts + HALO, :] = jnp.where(pl.program_id(1) == n_r - 1, 0.0, dh_ref[...])
        _shifted_copies(d1_s)
        dcb_ref[...] += jnp.sum(d_ref[...], axis=0, keepdims=True)
        for rb in range(ts // CONV_ROWS):
            base = rb * CONV_ROWS
            dcur = d1_s[0, base:base + CONV_ROWS, :]
            acc = jnp.zeros((CONV_ROWS, CA), F32)
            for k in range(CONV_WIDTH):
                back = CONV_WIDTH - 1 - k
                acc = acc + cw_ref[k:k + 1, :] * _shifted(d1_s, base + back, CONV_ROWS)
                prod = dcur * _shifted(a0_s, base + shift + k, CONV_ROWS)
                part = prod[0:8]
                for f in range(1, fold):
                    part = part + prod[8 * f:8 * f + 8]
                dw8_s[k] += part
            da0_s[base:base + CONV_ROWS, :] = acc
        da0 = da0_s[...]
        sig = _sigmoid(ag_ref[...])
        dz_ref[:, 0:CA] = (da0 * sig).astype(BF16)
        dz_ref[:, CA:2 * CA] = (da0 * av_ref[...] * sig * (1.0 - sig)).astype(BF16)

        @pl.when(last)
        def _():
            for k in range(CONV_WIDTH):
                dcw_ref[k:k + 1, :] = jnp.sum(dw8_s[k], axis=0, keepdims=True)

    blk, left = _mix_specs(ts, n_r, True)
    n_halo_blocks = T // HALO
    right = pl.BlockSpec((HALO, CA), lambda b, r: (jnp.minimum((b * n_r + r + 1) * per, n_halo_blocks - 1), 0))
    return pl.pallas_call(
        body, name="mix_bwd_conv", grid=(n_seq, n_r),
        in_specs=[blk(0), right, blk(0), blk(1), left(0), left(1),
                  pl.BlockSpec((CONV_WIDTH, CA), lambda b, r: (0, 0))],
        out_specs=[pl.BlockSpec((ts, 2 * CA), lambda b, r: (b * n_r + r, 0)),
                   pl.BlockSpec((CONV_WIDTH, CA), lambda b, r: (0, 0)), pl.BlockSpec((1, CA), lambda b, r: (0, 0))],
        out_shape=[jax.ShapeDtypeStruct((T, 2 * CA), BF16), jax.ShapeDtypeStruct((CONV_WIDTH, CA), F32),
                   jax.ShapeDtypeStruct((1, CA), F32)],
        scratch_shapes=[pltpu.VMEM((SUBLANES, HALO + ts, CA), F32), pltpu.VMEM((SUBLANES, ts + HALO, CA), F32),
                        pltpu.VMEM((ts, CA), F32), pltpu.VMEM((CONV_WIDTH, 8, CA), F32)],
        compiler_params=_cp("arbitrary", "arbitrary"),
    )(da1, da1, z, z, z, z, conv_w)


def _row_tile(R, want):
    t = min(R, want)
    t -= t % 8
    while t > 8 and R % t:
        t -= 8
    return t if t >= 8 and R % t == 0 else R


def _adam_step(w, g, m, v):
    nm = ADAM_B1 * m + (1.0 - ADAM_B1) * g
    nv = ADAM_B2 * v + (1.0 - ADAM_B2) * (g * g)
    m_hat = nm / (1.0 - ADAM_B1 ** ADAM_STEP)
    v_hat = nv / (1.0 - ADAM_B2 ** ADAM_STEP)
    return -ADAM_LR * (m_hat / (jnp.sqrt(v_hat) + ADAM_EPS) + ADAM_WD * w), nm, nv


def _adamw_parts(parts, w, m, v, name):
    L, n, C = w.shape
    assert len(parts) == L
    tr = _row_tile(n, 192)
    n_i = n // tr

    def body(*refs):
        p_refs = refs[:L]
        w_ref, m_ref, v_ref, g_ref, d_ref, nm_ref, nv_ref = refs[L:]
        for k in range(L):
            @pl.when(pl.program_id(0) == k)
            def _(k=k):
                acc = p_refs[k][0].astype(F32)
                for s in range(1, N_DEV):
                    acc = acc + p_refs[k][s].astype(F32)
                g_ref[...] = acc

        d_ref[...], nm_ref[...], nv_ref[...] = _adam_step(w_ref[...], g_ref[...], m_ref[...], v_ref[...])

    def part_spec(k):
        return pl.BlockSpec((N_DEV, tr, C),
                            lambda l, i: (0, jnp.where(l == k, i, jnp.where(l < k, 0, n_i - 1)), 0))

    blk = pl.BlockSpec((None, tr, C), lambda l, i: (l, i, 0))
    return pl.pallas_call(
        body, name=name, grid=(L, n_i),
        in_specs=[part_spec(k) for k in range(L)] + [blk] * 3, out_specs=[blk] * 4,
        out_shape=[jax.ShapeDtypeStruct((L, n, C), F32)] * 4,
        compiler_params=_cp("arbitrary", "arbitrary"),
    )(*parts, w, m, v)


def _adamw(w, g, m, v, name):
    R, C = w.shape
    tr = _row_tile(R, 256)

    def body(w_ref, g_ref, m_ref, v_ref, d_ref, nm_ref, nv_ref):
        d_ref[...], nm_ref[...], nv_ref[...] = _adam_step(w_ref[...], g_ref[...], m_ref[...], v_ref[...])

    blk = pl.BlockSpec((tr, C), lambda i: (i, 0))
    return pl.pallas_call(
        body, name=name, grid=(R // tr,),
        in_specs=[blk] * 4, out_specs=[blk] * 3,
        out_shape=[jax.ShapeDtypeStruct((R, C), F32)] * 3,
        compiler_params=_cp("parallel"),
    )(w, g, m, v)


def _me():
    return lax.axis_index("x"), lax.axis_index("y"), lax.axis_index("c")


def _block_rows(ref, dev, n):
    start = (4 * dev[0] + 2 * dev[1] + dev[2]) * n
    if len(ref.shape) == 2:
        return ref.at[pl.ds(start, n), :]
    return ref.at[:, pl.ds(start, n), :]


def _all_gather(shards):
    na = len(shards)
    ns = [s.shape[-2] for s in shards]

    def body(*refs):
        ins, outs = refs[:na], refs[na:2 * na]
        send_sems, recv_sems, local_sems = refs[2 * na:]
        x, y, c = _me()
        me, sibling = (x, y, c), (x, y, 1 - c)
        chips = [(1 - x, y), (x, 1 - y), (1 - x, 1 - y)]

        def copy(a, k, block, to, src=None):
            dst = _block_rows(outs[a], block, ns[a])
            return pltpu.make_async_remote_copy(
                src_ref=dst if src is None else src, dst_ref=dst,
                send_sem=send_sems.at[a, k], recv_sem=recv_sems.at[a, k], device_id=to, device_id_type=MESH)

        mine = [pltpu.make_async_copy(ins[a], _block_rows(outs[a], me, ns[a]), local_sems.at[a]) for a in range(na)]
        for cp in mine:
            cp.start()
        first = []
        for a in range(na):
            first.append(copy(a, 0, me, sibling, src=ins[a]))
            first += [copy(a, 1 + j, me, (*chip, c), src=ins[a]) for j, chip in enumerate(chips)]
        for cp in first:
            cp.start()
        passed = []
        for j, chip in enumerate(chips):
            for a in range(na):
                copy(a, 1 + j, (*chip, c), me).wait_recv()
                fwd = copy(a, 4 + j, (*chip, c), sibling)
                fwd.start()
                passed.append(fwd)
        for a in range(na):
            copy(a, 0, sibling, me).wait_recv()
            for j, chip in enumerate(chips):
                copy(a, 4 + j, (*chip, 1 - c), me).wait_recv()
        for cp in first + passed:
            cp.wait_send()
        for cp in mine:
            cp.wait()

    out_shape = [jax.ShapeDtypeStruct(s.shape[:-2] + (N_DEV * s.shape[-2], s.shape[-1]), s.dtype) for s in shards]
    return pl.pallas_call(
        body, name="weights_all_gather",
        in_specs=[ANY] * na, out_specs=[ANY] * na, out_shape=out_shape,
        scratch_shapes=[pltpu.SemaphoreType.DMA((na, 7)), pltpu.SemaphoreType.DMA((na, 7)),
                        pltpu.SemaphoreType.DMA((na,))],
    )(*shards)


def _split_copies(gather, srcs, lands, send_sems, recv_sems, ns):
    x, y, c = _me()
    me = (x, y, c)
    my_slot = 4 * x + 2 * y + c
    copies = []
    for mask in range(1, N_DEV):
        peer = (x ^ (mask >> 2), y ^ ((mask >> 1) & 1), c ^ (mask & 1))
        for a in range(len(srcs)):
            if gather:
                src, dst = srcs[a], _block_rows(lands[a], me, ns[a])
            else:
                src, dst = _block_rows(srcs[a], peer, ns[a]), lands[a].at[my_slot]
            sem = a * (N_DEV - 1) + mask - 1
            copies.append(pltpu.make_async_remote_copy(
                src_ref=src, dst_ref=dst, send_sem=send_sems.at[sem], recv_sem=recv_sems.at[sem],
                device_id=peer, device_id_type=MESH))
    return copies


HBM_SPEC = pl.BlockSpec(memory_space=pltpu.HBM)
SEM_SPEC = pl.BlockSpec(memory_space=pltpu.SEMAPHORE)


def _split_start(gather, srcs, name, dep=None):
    na = len(srcs)
    x, y, c = _me()
    mine = 4 * x + 2 * y + c
    if gather:
        ns = [s.shape[-2] for s in srcs]
        lands = [lax.dynamic_update_slice(
            lax.empty(s.shape[:-2] + (N_DEV * s.shape[-2], s.shape[-1]), s.dtype), s,
            (0,) * (s.ndim - 2) + (mine * s.shape[-2], 0)) for s in srcs]
    else:
        ns = [s.shape[-2] // N_DEV for s in srcs]
        lands = [lax.dynamic_update_slice(
            lax.empty((N_DEV, n, s.shape[-1]), s.dtype),
            lax.dynamic_slice(s, (mine * n, 0), (n, s.shape[-1]))[None], (mine, 0, 0)) for s, n in zip(srcs, ns)]
    n_in = 2 * na + (dep is not None)

    def body(*refs):
        send_sems, recv_sems = refs[n_in], refs[n_in + 1]
        for cp in _split_copies(gather, refs[:na], refs[na:2 * na], send_sems, recv_sems, ns):
            cp.start()
        refs[-1][...] = jnp.zeros_like(refs[-1])

    hbm = lambda a: pltpu.with_memory_space_constraint(a, pltpu.HBM)
    args = [hbm(a) for a in srcs] + [hbm(a) for a in lands] + ([dep] if dep is not None else [])
    out = pl.pallas_call(
        body, name=name,
        in_specs=[HBM_SPEC] * (2 * na) + ([ANY] if dep is not None else []),
        out_specs=[SEM_SPEC, SEM_SPEC] + [HBM_SPEC] * (2 * na) + [pl.BlockSpec(memory_space=pltpu.VMEM)],
        out_shape=[pltpu.SemaphoreType.DMA((na * (N_DEV - 1),)), pltpu.SemaphoreType.DMA((na * (N_DEV - 1),))]
        + [pltpu.HBM(a.shape, a.dtype) for a in srcs + lands] + [jax.ShapeDtypeStruct((8, LANES), F32)],
        input_output_aliases={i: 2 + i for i in range(2 * na)},
        compiler_params=pltpu.CompilerParams(has_side_effects=pltpu.SideEffectType.DATAFLOW_SIDE_EFFECTING),
    )(*args)
    return (gather, ns, out[0], out[1], list(out[2:2 + na]), list(out[2 + na:2 + 2 * na])), out[-1]


def _split_wait(handle, after, name):
    gather, ns, send, recv, srcs, lands = handle
    na = len(srcs)

    def body(*refs):
        send_sems, recv_sems = refs[2 * na], refs[2 * na + 1]
        for cp in _split_copies(gather, refs[:na], refs[na:2 * na], send_sems, recv_sems, ns):
            cp.wait_send()
            cp.wait_recv()

    out = pl.pallas_call(
        body, name=name,
        in_specs=[HBM_SPEC] * (2 * na) + [SEM_SPEC, SEM_SPEC, ANY],
        out_specs=[HBM_SPEC] * (2 * na),
        out_shape=[pltpu.HBM(a.shape, a.dtype) for a in srcs + lands],
        input_output_aliases={i: i for i in range(2 * na)},
        compiler_params=pltpu.CompilerParams(has_side_effects=pltpu.SideEffectType.DATAFLOW_SIDE_EFFECTING),
    )(*srcs, *lands, send, recv, after)
    return list(out[:na]), list(out[na:])


def _small_all_reduce(buf):
    R = buf.shape[0]

    def body(b_ref, o_ref, recv_ref, send_sems, recv_sems):
        x, y, c = _me()
        my_slot = 4 * x + 2 * y + c
        recv_ref[my_slot] = b_ref[...]
        copies = []
        for mask in range(1, N_DEV):
            peer = (x ^ (mask >> 2), y ^ ((mask >> 1) & 1), c ^ (mask & 1))
            copies.append(pltpu.make_async_remote_copy(
                src_ref=b_ref, dst_ref=recv_ref.at[my_slot],
                send_sem=send_sems.at[mask - 1], recv_sem=recv_sems.at[mask - 1],
                device_id=peer, device_id_type=MESH))
        for cp in copies:
            cp.start()
        for cp in copies:
            cp.wait()
        acc = recv_ref[0]
        for k in range(1, N_DEV):
            acc = acc + recv_ref[k]
        o_ref[...] = acc

    return pl.pallas_call(
        body, name="small_all_reduce",
        in_specs=[pl.BlockSpec(memory_space=pltpu.VMEM)], out_specs=pl.BlockSpec(memory_space=pltpu.VMEM),
        out_shape=jax.ShapeDtypeStruct((R, LANES), F32),
        scratch_shapes=[pltpu.VMEM((N_DEV, R, LANES), F32), pltpu.SemaphoreType.DMA((7,)),
                        pltpu.SemaphoreType.DMA((7,))],
        compiler_params=pltpu.CompilerParams(vmem_limit_bytes=VMEM_LIMIT),
    )(buf)


def _pack(arrays):
    flat = jnp.concatenate([a.reshape(-1) for a in arrays])
    pad = (-flat.shape[0]) % (8 * LANES)
    return jnp.pad(flat, (0, pad)).reshape(-1, LANES)


def _unpack(buf, shapes):
    flat = buf.reshape(-1)
    out, off = [], 0
    for s in shapes:
        n = 1
        for d in s:
            n *= d
        out.append(flat[off:off + n].reshape(s))
        off += n
    return out


def _ffn_index(layer, second):
    return (2 * layer + second) * 3


def kernel(x, g_ffn1, w_ffn1_gate, w_ffn1_up, w_ffn1_down, g_mix, w_in_ab, conv_w, conv_b, ln_a_g, ln_a_b, ln_v_g, ln_v_b, sp_w, sp_b, w_out_ab, w_qkv, w_o, g_ffn2, w_ffn2_gate, w_ffn2_up, w_ffn2_down, g_final, loss_target, m_g_ffn1, m_w_ffn1_gate, m_w_ffn1_up, m_w_ffn1_down, m_g_mix, m_w_in_ab, m_conv_w, m_conv_b, m_ln_a_g, m_ln_a_b, m_ln_v_g, m_ln_v_b, m_sp_w, m_sp_b, m_w_out_ab, m_w_qkv, m_w_o, m_g_ffn2, m_w_ffn2_gate, m_w_ffn2_up, m_w_ffn2_down, m_g_final, v_g_ffn1, v_w_ffn1_gate, v_w_ffn1_up, v_w_ffn1_down, v_g_mix, v_w_in_ab, v_conv_w, v_conv_b, v_ln_a_g, v_ln_a_b, v_ln_v_g, v_ln_v_b, v_sp_w, v_sp_b, v_w_out_ab, v_w_qkv, v_w_o, v_g_ffn2, v_w_ffn2_gate, v_w_ffn2_up, v_w_ffn2_down, v_g_final):
    n_seq, S, D = x.shape
    T = n_seq * S
    depth = g_ffn1.shape[0]
    assert depth == 2 and D == D_MODEL
    my_block = 4 * lax.axis_index("x") + 2 * lax.axis_index("y") + lax.axis_index("c")

    ffn_parts = []
    for l in range(depth):
        for gate, up, down in ((w_ffn1_gate, w_ffn1_up, w_ffn1_down), (w_ffn2_gate, w_ffn2_up, w_ffn2_down)):
            ffn_parts += [gate[l].T, up[l].T, down[l]]
    ffn_shard = lambda k: jnp.stack(ffn_parts[3 * k:3 * k + 3]).astype(BF16)
    conv_w_pad = jnp.zeros((HALO, conv_w.shape[2]), F32).at[:CONV_WIDTH].set(conv_w[0]).T
    w_ffn = [None] * (2 * depth)
    w_ffn[0], conv_w_t = _all_gather([ffn_shard(0), conv_w_pad])
    conv_w_full = conv_w_t.T[:CONV_WIDTH]
    shards_b = [w_out_ab[0].astype(BF16), ffn_shard(1)]
    shards_d = [w_qkv[0].T.astype(BF16), w_o[0].astype(BF16), ffn_shard(3)]
    gather_a, token = _split_start(True, [w_in_ab[0].T.astype(BF16)], "gather_a_start", dep=conv_w_t)
    gather_b, token = _split_start(True, shards_b, "gather_b_start", dep=token)
    gather_c, token = _split_start(True, [ffn_shard(2)], "gather_c_start", dep=token)
    gather_d, token = _split_start(True, shards_d, "gather_d_start", dep=token)

    def gathered(handle, after, name):
        return _split_wait(handle, after, name)[1]

    row = lambda a: a.reshape(1, -1)
    tril = jnp.tril(jnp.ones((CHUNK, CHUNK), dtype=bool))
    ws = jnp.where(tril[None], sp_w[0], 0.0).astype(BF16)
    ws_t = jnp.swapaxes(ws, 1, 2)
    bias2d = jnp.repeat(sp_b[0].T, DB, axis=1)
    conv_b2, lag, lab = row(conv_b[0]), row(ln_a_g[0]), row(ln_a_b[0])
    lvg, lvb = row(ln_v_g[0]), row(ln_v_b[0])

    x0 = x.reshape(T, D)
    target = loss_target.reshape(T, D)
    saved = []
    xc = x0
    for l in range(depth):
        xa, a1, b1, h1 = _ffn_fwd(xc, row(g_ffn1[l]), w_ffn[2 * l], 0, f"ffn1_fwd_{l}", dep=token)
        if l % 2 == 0:
            w_in_t, = gathered(gather_a, xa, "gather_a_wait")
            z, hm = _mm_nt(xa, w_in_t, row(g_mix[l]), F32, "mix_in_proj")
            cat, conv_out = _mix_fwd(z, conv_w_full, conv_b2, lag, lab, lvg, lvb, ws, bias2d, n_seq)
            w_out, w_ffn[1] = gathered(gather_b, cat, "gather_b_wait")
            xb = _mm_nn_res(cat, w_out, xa, "mix_out_proj")
            mixer = (z, hm, cat, conv_out)
        else:
            w_qkv_t, w_o_full, w_ffn[3] = gathered(gather_d, xa, "gather_d_wait")
            qkv, hm = _mm_nt(xa, w_qkv_t, row(g_mix[l]), BF16, "qkv_proj")
            o, att = _attn_fwd(qkv, n_seq)
            xb = _mm_nn_res(o, w_o_full, xa, "attn_out_proj")
            mixer = (qkv, hm, o, att)
        xn, a2, b2, h2 = _ffn_fwd(xb, row(g_ffn2[l]), w_ffn[2 * l + 1], 0, f"ffn2_fwd_{l}")
        saved.append((xc, a1, b1, h1, xa, mixer, xb, a2, b2, h2))
        xc = xn
        if l == 0:
            w_ffn[2], = gathered(gather_c, xc, "gather_c_wait")

    g, loss_part, dg_final = _loss_head(xc, row(g_final), target)

    dg_ffn1, dg_ffn2, dg_mix = [None] * depth, [None] * depth, [None] * depth
    exchanges = {}
    token = None

    def ffn_back(g, xin, gvec, a, b, h, k, tag, token):
        g, dg, da, db, s, gh = _ffn_bwd(g, xin, gvec, a, b, w_ffn[k], 0, f"ffn{tag}_bwd", dep=token)
        if k == 0:
            return g, dg, (da, db, s, gh, h)
        dws = [_mm_tn(da, h, f"dw_gate{tag}"), _mm_tn(db, h, f"dw_up{tag}"), _mm_tn(s, gh, f"dw_down{tag}")]
        exchanges[f"ffn{k}"], token = _split_start(False, dws, f"exchange_ffn{tag}_start")
        return g, dg, token

    for l in reversed(range(depth)):
        xin, a1, b1, h1, xa, mixer, xb, a2, b2, h2 = saved[l]
        g, dg_ffn2[l], token = ffn_back(g, xb, row(g_ffn2[l]), a2, b2, h2, 2 * l + 1, f"2_{l}", token)
        if l % 2 == 0:
            z, hm, cat, conv_out = mixer
            dcat = _mm_nt(g, w_out, None, F32, "mix_out_bwd", dep=token)
            d_w_out = _mm_tn(cat, g, "dw_out")
            (da1, dz_uv, d_lag, d_lab, d_lvg, d_lvb, d_ws, d_sb) = _mix_bwd_rows(
                dcat, z, conv_out, lag, lab, lvg, lvb, ws, ws_t, bias2d, n_seq)
            dz_a, d_cw, d_cb = _mix_bwd_conv(da1, z, conv_w_full, n_seq)
            d_w_in_t = jnp.concatenate([_mm_tn(dz_a, hm, "dw_in_a"), _mm_tn(dz_uv, hm, "dw_in_uv")])
            exchanges["mix"], token = _split_start(False, [d_w_out, d_w_in_t], "exchange_mix_start")
            g, dg_mix[l] = _mm_nn_rmsbwd([dz_a, dz_uv], w_in_t, xa, row(g_mix[l]), g, "mix_in_bwd", dep=token)
        else:
            qkv, hm, o, att = mixer
            do = _mm_nt(g, w_o_full, None, BF16, "attn_out_bwd", dep=token)
            d_w_o = _mm_tn(o, g, "dw_o")
            dq, dk, dv = _attn_bwd(qkv, do, att, n_seq)
            d_w_qkv_t = jnp.concatenate([_mm_tn(dq, hm, "dw_q"), _mm_tn(dk, hm, "dw_k"), _mm_tn(dv, hm, "dw_v")])
            exchanges["attn"], token = _split_start(False, [d_w_o, d_w_qkv_t], "exchange_attn_start")
            g, dg_mix[l] = _mm_nn_rmsbwd([dq, dk, dv], w_qkv_t, xa, row(g_mix[l]), g, "qkv_bwd", dep=token)
        g, dg_ffn1[l], token = ffn_back(g, xin, row(g_ffn1[l]), a1, b1, h1, 2 * l, f"1_{l}", token)
    grad_x = g.reshape(n_seq, S, D)

    small = [jnp.concatenate(dg_ffn1), jnp.concatenate(dg_mix), d_cw, d_cb, d_lag, d_lab, d_lvg, d_lvb,
             jnp.where(tril[None], d_ws, 0.0), d_sb.T, jnp.concatenate(dg_ffn2), dg_final, loss_part[:, :1]]
    small_shapes = [(depth, D), (depth, D), (CONV_WIDTH, CA), (1, CA), (1, CA), (1, CA), (1, GB, DB), (1, GB, DB),
                    (1, GB, CHUNK, CHUNK), (1, GB, CHUNK), (depth, D), (D,), ()]
    small_sum = _small_all_reduce(_pack(small))
    red = _unpack(small_sum, small_shapes)
    (gr_g_ffn1, gr_g_mix, gr_cw_full, gr_cb, gr_lag, gr_lab, gr_lvg, gr_lvb, gr_sp_w, gr_sp_b,
     gr_g_ffn2, gr_g_final, loss) = red
    n_cw = conv_w.shape[2]
    gr_cw = lax.dynamic_slice(gr_cw_full, (0, my_block * n_cw), (CONV_WIDTH, n_cw))[None]

    da, db, s, gh, h = token
    token = small_sum
    for which, lhs, rhs in ((2, s, gh), (1, db, h), (0, da, h)):
        dw = _mm_tn(lhs, rhs, f"dw_ffn0_{which}", dep=token, out_dtype=BF16)
        exchanges[f"ffn0_{which}"], token = _split_start(False, [dw], f"exchange_ffn0_{which}_start")

    def landed(key, after):
        return _split_wait(exchanges[key], after, f"exchange_{key}_wait")[1]

    parts_ffn = [None] * (6 * depth)
    for k in range(1, 2 * depth):
        parts_ffn[3 * k:3 * k + 3] = landed(f"ffn{k}", token)
    parts_out, parts_in = landed("mix", token)
    parts_o, parts_qkv = landed("attn", token)

    grads = {
        "g_ffn1": gr_g_ffn1, "g_mix": gr_g_mix, "conv_w": gr_cw, "conv_b": gr_cb, "ln_a_g": gr_lag,
        "ln_a_b": gr_lab, "ln_v_g": gr_lvg, "ln_v_b": gr_lvb, "sp_w": gr_sp_w, "sp_b": gr_sp_b,
        "g_ffn2": gr_g_ffn2, "g_final": gr_g_final,
    }
    weights = dict(g_ffn1=g_ffn1, w_ffn1_gate=w_ffn1_gate, w_ffn1_up=w_ffn1_up, w_ffn1_down=w_ffn1_down, g_mix=g_mix,
                   w_in_ab=w_in_ab, conv_w=conv_w, conv_b=conv_b, ln_a_g=ln_a_g, ln_a_b=ln_a_b, ln_v_g=ln_v_g,
                   ln_v_b=ln_v_b, sp_w=sp_w, sp_b=sp_b, w_out_ab=w_out_ab, w_qkv=w_qkv, w_o=w_o, g_ffn2=g_ffn2,
                   w_ffn2_gate=w_ffn2_gate, w_ffn2_up=w_ffn2_up, w_ffn2_down=w_ffn2_down, g_final=g_final)
    m_in = dict(g_ffn1=m_g_ffn1, w_ffn1_gate=m_w_ffn1_gate, w_ffn1_up=m_w_ffn1_up, w_ffn1_down=m_w_ffn1_down,
                g_mix=m_g_mix, w_in_ab=m_w_in_ab, conv_w=m_conv_w, conv_b=m_conv_b, ln_a_g=m_ln_a_g, ln_a_b=m_ln_a_b,
                ln_v_g=m_ln_v_g, ln_v_b=m_ln_v_b, sp_w=m_sp_w, sp_b=m_sp_b, w_out_ab=m_w_out_ab, w_qkv=m_w_qkv,
                w_o=m_w_o, g_ffn2=m_g_ffn2, w_ffn2_gate=m_w_ffn2_gate, w_ffn2_up=m_w_ffn2_up,
                w_ffn2_down=m_w_ffn2_down, g_final=m_g_final)
    v_in = dict(g_ffn1=v_g_ffn1, w_ffn1_gate=v_w_ffn1_gate, w_ffn1_up=v_w_ffn1_up, w_ffn1_down=v_w_ffn1_down,
                g_mix=v_g_mix, w_in_ab=v_w_in_ab, conv_w=v_conv_w, conv_b=v_conv_b, ln_a_g=v_ln_a_g, ln_a_b=v_ln_a_b,
                ln_v_g=v_ln_v_g, ln_v_b=v_ln_v_b, sp_w=v_sp_w, sp_b=v_sp_b, w_out_ab=v_w_out_ab, w_qkv=v_w_qkv,
                w_o=v_w_o, g_ffn2=v_g_ffn2, w_ffn2_gate=v_w_ffn2_gate, w_ffn2_up=v_w_ffn2_up,
                w_ffn2_down=v_w_ffn2_down, g_final=v_g_final)
    names = list(weights)
    grads = {n: grads[n].reshape(weights[n].shape) for n in grads}

    delta, new_m, new_v = {}, {}, {}

    def adamw_big(n, parts):
        if weights[n].shape[-1] == D:
            view = back = lambda a: a
        else:
            view = back = lambda a: jnp.swapaxes(a, 1, 2)
        out = _adamw_parts(parts, view(weights[n]), view(m_in[n]), view(v_in[n]), f"adamw_{n}")
        grads[n], delta[n], new_m[n], new_v[n] = [back(a) for a in out]

    adamw_big("w_in_ab", [parts_in])
    adamw_big("w_out_ab", [parts_out])
    adamw_big("w_qkv", [parts_qkv])
    adamw_big("w_o", [parts_o])
    kinds = ("gate", "up", "down")
    for which, kind in enumerate(kinds):
        adamw_big(f"w_ffn2_{kind}", [parts_ffn[_ffn_index(l, 1) + which] for l in range(depth)])
    big = [n for n in names if n.startswith("w_")]
    after = jnp.concatenate([delta[n].reshape(-1)[:1] for n in big if n in delta]).reshape(1, -1)
    for which in (2, 1, 0):
        parts_ffn[which], = landed(f"ffn0_{which}", after)
    for which, kind in enumerate(kinds):
        adamw_big(f"w_ffn1_{kind}", [parts_ffn[_ffn_index(l, 0) + which] for l in range(depth)])
    little = [n for n in names if n not in big]
    shapes = [weights[n].shape for n in little]
    d, nm, nv = _adamw(_pack([weights[n] for n in little]), _pack([grads[n] for n in little]),
                       _pack([m_in[n] for n in little]), _pack([v_in[n] for n in little]), "adamw_small")
    for n, dd, mm, vv in zip(little, _unpack(d, shapes), _unpack(nm, shapes), _unpack(nv, shapes)):
        delta[n], new_m[n], new_v[n] = dd, mm, vv

    return (loss, grad_x, *[grads[n] for n in names], *[delta[n] for n in names],
            *[new_m[n] for n in names], *[new_v[n] for n in names])
```

```python
import functools

import jax
import jax.numpy as jnp
from jax import lax
from jax.experimental import pallas as pl
from jax.experimental.pallas import tpu as pltpu

F32 = jnp.float32
BF16 = jnp.bfloat16

D_MODEL = 1024
CA = 512
CB = 512
GB = 4
DB = 128
CHUNK = 128
CONV_WIDTH = 31
N_HEADS = 16
HEAD_DIM = 64
EPS = 1e-6
N_DEV = 8
LANES = 128
SUBLANES = 8
QB = 128
ATT_TQ = 1024
FFN_TN = 2816
FFN_TM = 256
HALO = 32
CONV_ROWS = 32
ATT_SCALE = HEAD_DIM ** -0.5

ADAM_LR = 0.001
ADAM_B1 = 0.9
ADAM_B2 = 0.999
ADAM_EPS = 1e-08
ADAM_WD = 0.01
ADAM_STEP = 10

NT = (((1,), (1,)), ((), ()))
NN = (((1,), (0,)), ((), ()))
TN = (((0,), (0,)), ((), ()))
MESH = pl.DeviceIdType.MESH
ANY = pl.BlockSpec(memory_space=pl.ANY)
VMEM_LIMIT = 60 * 1024 * 1024


def _dot(a, b, dims):
    return lax.dot_general(a, b, dims, preferred_element_type=F32)


def _cp(*sem):
    return pltpu.CompilerParams(dimension_semantics=sem, vmem_limit_bytes=VMEM_LIMIT)


def _pcall(body, *, in_specs, args, dep=None, **kw):
    if dep is not None:
        n_in = len(in_specs)
        inner = body

        def body(*refs):
            inner(*refs[:n_in], *refs[n_in + 1:])

        in_specs = list(in_specs) + [ANY]
        args = tuple(args) + (dep,)
    return pl.pallas_call(body, in_specs=list(in_specs), **kw)(*args)


def _tile(n, want):
    if n <= want:
        return n
    t = want - want % LANES
    while t > LANES and n % t:
        t -= LANES
    assert n % t == 0, (n, want)
    return t


def _sigmoid(x):
    return 0.5 * jnp.tanh(0.5 * x) + 0.5


def _rstd(x):
    return lax.rsqrt(jnp.mean(x * x, axis=-1, keepdims=True) + EPS)


def _rms_bwd(x, g, dh):
    r = _rstd(x)
    u = dh * g
    dx = r * (u - x * (r * r) * jnp.mean(u * x, axis=-1, keepdims=True))
    dg = jnp.sum(dh * x * r, axis=0, keepdims=True)
    return dx, dg


def _ln_fwd(x, g, b):
    mu = jnp.mean(x, axis=-1, keepdims=True)
    xc = x - mu
    r = lax.rsqrt(jnp.mean(xc * xc, axis=-1, keepdims=True) + EPS)
    xh = xc * r
    return xh * g + b, xh, r


def _ln_bwd(dy, xh, r, g):
    dxh = dy * g
    return r * (dxh - jnp.mean(dxh, axis=-1, keepdims=True)
                - xh * jnp.mean(dxh * xh, axis=-1, keepdims=True))


def _ffn_fwd(x, g, wall, base, name, dep=None):
    T, D = x.shape
    F = wall.shape[1]
    tm, tn = _tile(T, FFN_TM), _tile(F, FFN_TN)
    n_j = F // tn

    def body(x_ref, g_ref, wg_ref, wu_ref, wd_ref, xo_ref, a_ref, b_ref, h_ref, acc_ref):
        j = pl.program_id(1)

        @pl.when(j == 0)
        def _():
            xv = x_ref[...]
            h_ref[...] = (xv * _rstd(xv) * g_ref[...]).astype(BF16)
            acc_ref[...] = jnp.zeros_like(acc_ref)

        h = h_ref[...]
        a = _dot(h, wg_ref[...], NT)
        b = _dot(h, wu_ref[...], NT)
        a_ref[...] = a.astype(BF16)
        b_ref[...] = b.astype(BF16)
        s = (a * _sigmoid(a) * b).astype(BF16)
        acc_ref[...] += _dot(s, wd_ref[...], NN)

        @pl.when(j == n_j - 1)
        def _():
            xo_ref[...] = x_ref[...] + 0.5 * acc_ref[...]

    single = pl.Buffered(1) if n_j == 1 else None
    wspec = lambda k: pl.BlockSpec((None, tn, D), lambda i, j: (base + k, j, 0), pipeline_mode=single)
    return _pcall(
        body, name=name, grid=(T // tm, n_j), dep=dep, args=(x, g, wall, wall, wall),
        in_specs=[pl.BlockSpec((tm, D), lambda i, j: (i, 0)), pl.BlockSpec((1, D), lambda i, j: (0, 0)),
                  wspec(0), wspec(1), wspec(2)],
        out_specs=[pl.BlockSpec((tm, D), lambda i, j: (i, 0)), pl.BlockSpec((tm, tn), lambda i, j: (i, j)),
                   pl.BlockSpec((tm, tn), lambda i, j: (i, j)), pl.BlockSpec((tm, D), lambda i, j: (i, 0))],
        out_shape=[jax.ShapeDtypeStruct((T, D), F32), jax.ShapeDtypeStruct((T, F), BF16),
                   jax.ShapeDtypeStruct((T, F), BF16), jax.ShapeDtypeStruct((T, D), BF16)],
        scratch_shapes=[pltpu.VMEM((tm, D), F32)],
        compiler_params=_cp("parallel", "arbitrary"),
    )


def _ffn_bwd(go, x, g, a, b, wall, base, name, dep=None):
    T, D = x.shape
    F = wall.shape[1]
    tm, tn = _tile(T, FFN_TM), _tile(F, FFN_TN)
    n_j = F // tn

    def body(go_ref, x_ref, g_ref, a_ref, b_ref, wg_ref, wu_ref, wd_ref,
             gx_ref, dg_ref, da_ref, db_ref, s_ref, gh_ref, acc_ref):
        i, j = pl.program_id(0), pl.program_id(1)

        @pl.when(j == 0)
        def _():
            gh_ref[...] = (0.5 * go_ref[...]).astype(BF16)
            acc_ref[...] = jnp.zeros_like(acc_ref)

        @pl.when((i == 0) & (j == 0))
        def _():
            dg_ref[...] = jnp.zeros_like(dg_ref)

        ds = _dot(gh_ref[...], wd_ref[...], NT)
        av = a_ref[...].astype(F32)
        bv = b_ref[...].astype(F32)
        sig = _sigmoid(av)
        sl = av * sig
        dab = ((ds * bv) * (sig + sl * (1.0 - sig))).astype(BF16)
        dbb = (ds * sl).astype(BF16)
        s_ref[...] = (sl * bv).astype(BF16)
        da_ref[...] = dab
        db_ref[...] = dbb
        acc_ref[...] += _dot(dab, wg_ref[...], NN) + _dot(dbb, wu_ref[...], NN)

        @pl.when(j == n_j - 1)
        def _():
            dx, dg = _rms_bwd(x_ref[...], g_ref[...], acc_ref[...])
            gx_ref[...] = go_ref[...] + dx
            dg_ref[...] += dg

    single = pl.Buffered(1) if n_j == 1 else None
    wspec = lambda k: pl.BlockSpec((None, tn, D), lambda i, j: (base + k, j, 0), pipeline_mode=single)
    row = pl.BlockSpec((tm, D), lambda i, j: (i, 0))
    hid = pl.BlockSpec((tm, tn), lambda i, j: (i, j))
    vec = pl.BlockSpec((1, D), lambda i, j: (0, 0))
    return _pcall(
        body, name=name, grid=(T // tm, n_j), dep=dep, args=(go, x, g, a, b, wall, wall, wall),
        in_specs=[row, row, vec, hid, hid, wspec(0), wspec(1), wspec(2)],
        out_specs=[row, vec, hid, hid, hid, row],
        out_shape=[jax.ShapeDtypeStruct((T, D), F32), jax.ShapeDtypeStruct((1, D), F32),
                   jax.ShapeDtypeStruct((T, F), BF16), jax.ShapeDtypeStruct((T, F), BF16),
                   jax.ShapeDtypeStruct((T, F), BF16), jax.ShapeDtypeStruct((T, D), BF16)],
        scratch_shapes=[pltpu.VMEM((tm, D), F32)],
        compiler_params=_cp("arbitrary", "arbitrary"),
    )


def _mm_tn(a, b, name, dep=None, out_dtype=F32):
    T, M = a.shape
    N = b.shape[1]
    tmm, tk = _tile(M, 1536), _tile(T, 2048)
    n_k = T // tk
    narrow = out_dtype != F32

    def body(a_ref, b_ref, o_ref, *scratch):
        acc_ref = scratch[0] if narrow else o_ref

        @pl.when(pl.program_id(1) == 0)
        def _():
            acc_ref[...] = jnp.zeros_like(acc_ref)

        acc_ref[...] += _dot(a_ref[...].astype(BF16), b_ref[...].astype(BF16), TN)
        if narrow:
            @pl.when(pl.program_id(1) == n_k - 1)
            def _():
                o_ref[...] = acc_ref[...].astype(out_dtype)

    return _pcall(
        body, name=name, grid=(M // tmm, n_k), dep=dep, args=(a, b),
        in_specs=[pl.BlockSpec((tk, tmm), lambda m, k: (k, m)), pl.BlockSpec((tk, N), lambda m, k: (k, 0))],
        out_specs=pl.BlockSpec((tmm, N), lambda m, k: (m, 0)),
        out_shape=jax.ShapeDtypeStruct((M, N), out_dtype),
        scratch_shapes=[pltpu.VMEM((tmm, N), F32)] if narrow else [],
        compiler_params=_cp("parallel", "arbitrary"),
    )


def _mm_nt(x, wt, g, out_dtype, name, dep=None):
    T, K = x.shape
    N = wt.shape[0]
    tm, tn = _tile(T, 512), N
    norm = g is not None

    def body(*refs):
        if norm:
            x_ref, g_ref, w_ref, o_ref, h_ref = refs
        else:
            x_ref, w_ref, o_ref, h_ref = refs

        @pl.when(pl.program_id(1) == 0)
        def _():
            xv = x_ref[...].astype(F32)
            if norm:
                xv = xv * _rstd(xv) * g_ref[...]
            h_ref[...] = xv.astype(BF16)

        o_ref[...] = _dot(h_ref[...], w_ref[...], NT).astype(out_dtype)

    row = pl.BlockSpec((tm, K), lambda i, j: (i, 0))
    wsp = pl.BlockSpec((tn, K), lambda i, j: (j, 0))
    osp = pl.BlockSpec((tm, tn), lambda i, j: (i, j))
    if norm:
        return pl.pallas_call(
            body, name=name, grid=(T // tm, N // tn),
            in_specs=[row, pl.BlockSpec((1, K), lambda i, j: (0, 0)), wsp],
            out_specs=[osp, row],
            out_shape=[jax.ShapeDtypeStruct((T, N), out_dtype), jax.ShapeDtypeStruct((T, K), BF16)],
            compiler_params=_cp("parallel", "arbitrary"),
        )(x, g, wt)
    return _pcall(
        body, name=name, grid=(T // tm, N // tn), dep=dep, args=(x, wt),
        in_specs=[row, wsp], out_specs=osp,
        out_shape=jax.ShapeDtypeStruct((T, N), out_dtype),
        scratch_shapes=[pltpu.VMEM((tm, K), BF16)],
        compiler_params=_cp("parallel", "arbitrary"),
    )


def _mm_nn_res(act, w, resid, name):
    T, K = act.shape
    D = w.shape[1]
    tm = _tile(T, 512)

    def body(a_ref, w_ref, r_ref, o_ref):
        o_ref[...] = r_ref[...] + _dot(a_ref[...].astype(BF16), w_ref[...], NN)

    return pl.pallas_call(
        body, name=name, grid=(T // tm,),
        in_specs=[pl.BlockSpec((tm, K), lambda i: (i, 0)), pl.BlockSpec((K, D), lambda i: (0, 0)),
                  pl.BlockSpec((tm, D), lambda i: (i, 0))],
        out_specs=pl.BlockSpec((tm, D), lambda i: (i, 0)),
        out_shape=jax.ShapeDtypeStruct((T, D), F32),
        compiler_params=_cp("parallel"),
    )(act, w, resid)


def _mm_nn_rmsbwd(acts, w, x, g, gprev, name, dep=None):
    T = acts[0].shape[0]
    ks = [a.shape[1] for a in acts]
    K, D = w.shape
    assert sum(ks) == K
    tm = _tile(T, 512)
    na = len(acts)

    def body(*refs):
        a_refs = refs[:na]
        w_ref, x_ref, g_ref, gp_ref, o_ref, dg_ref = refs[na:]

        @pl.when(pl.program_id(0) == 0)
        def _():
            dg_ref[...] = jnp.zeros_like(dg_ref)

        dh, off = None, 0
        for a_ref, k in zip(a_refs, ks):
            part = _dot(a_ref[...].astype(BF16), w_ref[off:off + k, :], NN)
            dh = part if dh is None else dh + part
            off += k
        dx, dg = _rms_bwd(x_ref[...], g_ref[...], dh)
        o_ref[...] = gp_ref[...] + dx
        dg_ref[...] += dg

    row = pl.BlockSpec((tm, D), lambda i: (i, 0))
    vec = pl.BlockSpec((1, D), lambda i: (0, 0))
    return _pcall(
        body, name=name, grid=(T // tm,), dep=dep, args=(*acts, w, x, g, gprev),
        in_specs=[pl.BlockSpec((tm, k), lambda i: (i, 0)) for k in ks]
        + [pl.BlockSpec((K, D), lambda i: (0, 0)), row, vec, row],
        out_specs=[row, vec],
        out_shape=[jax.ShapeDtypeStruct((T, D), F32), jax.ShapeDtypeStruct((1, D), F32)],
        compiler_params=_cp("arbitrary"),
    )


def _loss_head(x, g, target):
    T, D = x.shape
    tm = _tile(T, 512)

    def body(x_ref, g_ref, t_ref, dx_ref, loss_ref, dg_ref):
        @pl.when(pl.program_id(0) == 0)
        def _():
            loss_ref[...] = jnp.zeros_like(loss_ref)
            dg_ref[...] = jnp.zeros_like(dg_ref)

        xv = x_ref[...]
        gv = g_ref[...]
        e = xv * _rstd(xv) * gv - t_ref[...]
        per_tok = jnp.sum(e * e, axis=-1, keepdims=True) * (1.0 / D)
        loss_ref[...] += 0.5 * jnp.sum(per_tok, axis=0, keepdims=True)
        dx, dg = _rms_bwd(xv, gv, e * (1.0 / D))
        dx_ref[...] = dx
        dg_ref[...] += dg

    row = pl.BlockSpec((tm, D), lambda i: (i, 0))
    vec = pl.BlockSpec((1, D), lambda i: (0, 0))
    return pl.pallas_call(
        body, name="loss_head", grid=(T // tm,),
        in_specs=[row, vec, row],
        out_specs=[row, pl.BlockSpec((1, LANES), lambda i: (0, 0)), vec],
        out_shape=[jax.ShapeDtypeStruct((T, D), F32), jax.ShapeDtypeStruct((1, LANES), F32),
                   jax.ShapeDtypeStruct((1, D), F32)],
        compiler_params=_cp("arbitrary"),
    )(x, g, target)


def _log_gates(z):
    neg_abs = lax.bitcast_convert_type(lax.bitcast_convert_type(z, jnp.uint32) | jnp.uint32(0x80000000), F32)
    ls = jnp.minimum(z, 0.0) - jnp.log(1.0 + jnp.exp(neg_abs))
    return ls, ls - z


def _cumsum_mm(v, u2):
    hi = v.astype(BF16)
    lo = (v - hi.astype(F32)).astype(BF16)
    return _dot(jnp.concatenate([hi, lo], axis=1), u2, NN)


def _half_rowsum(v):
    n = v.shape[0]
    s0 = jnp.sum(v[:, :QB], axis=1, keepdims=True)
    s1 = jnp.sum(v[:, QB:], axis=1, keepdims=True)
    return jnp.concatenate([jnp.broadcast_to(s0, (n, QB)), jnp.broadcast_to(s1, (n, QB))], axis=1)


def _stack_heads(src_ref, dst_ref, n_blk):
    m0 = lax.broadcasted_iota(jnp.int32, (1, LANES), 1) < HEAD_DIM

    def fill(c, carry):
        blk = src_ref[pl.ds(pl.multiple_of(c * QB, QB), QB), :]
        zero = jnp.zeros_like(blk)
        dst_ref[c, 0:QB, :] = jnp.where(m0, blk, zero)
        dst_ref[c, QB:2 * QB, :] = jnp.where(m0, zero, blk)
        return carry

    lax.fori_loop(0, n_blk, fill, 0)


def _diag_mask(tq, j):
    n = tq - j * QB
    row = lax.broadcasted_iota(jnp.int32, (n, 2 * QB), 0)
    col = lax.broadcasted_iota(jnp.int32, (n, 2 * QB), 1)
    return (col & (QB - 1)) < row


def _tri_blockdiag(upper):
    r = lax.broadcasted_iota(jnp.int32, (2 * QB, 2 * QB), 0)
    c = lax.broadcasted_iota(jnp.int32, (2 * QB, 2 * QB), 1)
    same = (r // QB) == (c // QB)
    u = (same & ((r > c) if upper else (r < c))).astype(BF16)
    return jnp.concatenate([u, u], axis=0)


def _attn_tiles(T, n_seq):
    S = T // n_seq
    tq = ATT_TQ if S % ATT_TQ == 0 else 2 * QB
    assert S % tq == 0
    return S, tq, tq // QB, S // tq, S // QB


def _attn_fwd(qkv, n_seq):
    T = qkv.shape[0]
    S, tq, r, n_q, n_k = _attn_tiles(T, n_seq)
    n_p = D_MODEL // LANES
    n_steps = n_seq * n_p * n_q
    u_suffix = _tri_blockdiag(True)

    def body(q_ref, k_ref, v_ref, u_ref, o_ref, a_hbm, kk_ref, vv_ref, lr_s, acc_s, a_stage, sems):
        qi = pl.program_id(2)
        group = (pl.program_id(0) * n_p + pl.program_id(1)) * n_q + qi

        @pl.when(qi == 0)
        def _():
            _stack_heads(k_ref, kk_ref, n_k)
            _stack_heads(v_ref, vv_ref, n_k)

        u = u_ref[...]
        lr_s[...] = jnp.zeros_like(lr_s)
        acc_s[...] = jnp.zeros_like(acc_s)
        base = ((group // n_q) * (n_q * (n_q + 1) // 2) + (qi * (qi + 1)) // 2) % 2

        def saves(first_kj, half):
            return [pltpu.make_async_copy(a_stage.at[half, j], a_hbm.at[group, first_kj - j], sems.at[half])
                    for j in range(r)]

        @pl.when(group >= 2)
        def _():
            for cp in saves(0, base):
                cp.wait()

        a_stage[base] = jnp.zeros_like(a_stage[0])

        def step(kj, half, j, rows, q, mask, lr, acc):
            ls, lk = _log_gates(_dot(q, kk_ref[kj], NT))
            if mask is not None:
                lk = jnp.where(mask, lk, 0.0)
            a = jnp.exp(ls + _cumsum_mm(lk, u) + lr)
            if mask is not None:
                a = jnp.where(mask, a, 0.0)
            a = a.astype(BF16)
            a_stage[half, j, rows, :] = a
            return lr + _half_rowsum(lk), acc + _dot(a, vv_ref[kj], NN)

        last = (qi + 1) * r - 1
        for n in range(r):
            rows = slice((r - 1 - n) * QB, tq)
            lr, acc = step(last - n, base, n, rows, q_ref[rows, :] * ATT_SCALE, _diag_mask(tq, r - 1 - n),
                           lr_s[rows, :], acc_s[rows, :])
            lr_s[rows, :] = lr
            acc_s[rows, :] = acc
        for cp in saves(last, base):
            cp.start()

        q = q_ref[...] * ATT_SCALE

        def off(it, carry):
            half = (base + it + 1) % 2
            first = (qi - it) * r - 1
            for cp in saves(first, half):
                cp.wait()
            lr, acc = lr_s[...], acc_s[...]
            for j in range(r):
                lr, acc = step(first - j, half, j, slice(0, tq), q, None, lr, acc)
            lr_s[...] = lr
            acc_s[...] = acc
            for cp in saves(first, half):
                cp.start()
            return carry

        lax.fori_loop(0, qi, off, 0)

        @pl.when(group == n_steps - 1)
        def _():
            for cp in saves(0, (base + qi) % 2):
                cp.wait()
            if n_steps * n_q > 1:
                for cp in saves(0, (base + qi + 1) % 2):
                    cp.wait()

        o_ref[...] = acc_s[...].astype(BF16)

    return pl.pallas_call(
        body, name="attn_fwd", grid=(n_seq, n_p, n_q),
        in_specs=[pl.BlockSpec((tq, LANES), lambda b, p, qi: (b * n_q + qi, p)),
                  pl.BlockSpec((S, LANES), lambda b, p, qi: (b, n_p + p)),
                  pl.BlockSpec((S, LANES), lambda b, p, qi: (b, 2 * n_p + p)),
                  pl.BlockSpec((4 * QB, 2 * QB), lambda b, p, qi: (0, 0))],
        out_specs=[pl.BlockSpec((tq, LANES), lambda b, p, qi: (b * n_q + qi, p)), ANY],
        out_shape=[jax.ShapeDtypeStruct((T, D_MODEL), BF16),
                   jax.ShapeDtypeStruct((n_seq * n_p * n_q, n_k, tq, 2 * QB), BF16)],
        scratch_shapes=[pltpu.VMEM((n_k, 2 * QB, LANES), BF16), pltpu.VMEM((n_k, 2 * QB, LANES), BF16),
                        pltpu.VMEM((tq, 2 * QB), F32), pltpu.VMEM((tq, LANES), F32),
                        pltpu.VMEM((2, r, tq, 2 * QB), BF16), pltpu.SemaphoreType.DMA((2,))],
        compiler_params=_cp("arbitrary", "arbitrary", "arbitrary"),
    )(qkv, qkv, qkv, u_suffix)


def _attn_bwd(qkv, do, a_saved, n_seq):
    T = qkv.shape[0]
    S, tq, r, n_q, n_k = _attn_tiles(T, n_seq)
    n_p = D_MODEL // LANES
    n_steps = n_seq * n_p * n_q
    u_prefix = _tri_blockdiag(False)[:2 * QB]

    def body(q_ref, k_ref, v_ref, do_ref, u_ref, a_hbm, dq_ref, dk_out, dv_out,
             kk_ref, vv_ref, cg_s, dq_s, dk_ref, dv_ref, a_stage, sems):
        qi = pl.program_id(2)
        group = (pl.program_id(0) * n_p + pl.program_id(1)) * n_q + qi

        base = ((group // n_q) * (n_q * (n_q + 1) // 2) + (qi * (qi + 1)) // 2) % 2

        def fetches(grp, g, half):
            return [pltpu.make_async_copy(a_hbm.at[grp, g * r + j], a_stage.at[half, j], sems.at[half])
                    for j in range(r)]

        @pl.when(group == 0)
        def _():
            for cp in fetches(group, 0, 0):
                cp.start()

        @pl.when(qi == 0)
        def _():
            _stack_heads(k_ref, kk_ref, n_k)
            _stack_heads(v_ref, vv_ref, n_k)
            dk_ref[...] = jnp.zeros_like(dk_ref)
            dv_ref[...] = jnp.zeros_like(dv_ref)

        u = u_ref[...]
        m0 = lax.broadcasted_iota(jnp.int32, (1, LANES), 1) < HEAD_DIM
        cg_s[...] = jnp.zeros_like(cg_s)
        dq_s[...] = jnp.zeros_like(dq_s)

        def step(kj, half, j, rows, q, dov, mask, cg, dq):
            kk = kk_ref[kj]
            beta = _sigmoid(_dot(q, kk, NT))
            a = a_stage[half, j, rows, :]
            g = a.astype(F32) * _dot(dov, vv_ref[kj], NT)
            dz = g - (g + _dot(g.astype(BF16), u, NN) + cg) * beta
            if mask is not None:
                dz = jnp.where(mask, dz, 0.0)
            dz = dz.astype(BF16)
            keys = pl.ds(pl.multiple_of(kj * QB, QB), QB)
            dvt = _dot(a, dov, TN)
            dv_ref[keys, :] += jnp.where(m0, dvt[:QB], dvt[QB:])
            dkt = _dot(dz, q, TN)
            dk_ref[keys, :] += jnp.where(m0, dkt[:QB], dkt[QB:])
            return cg + _half_rowsum(g), dq + _dot(dz, kk, NN)

        q = q_ref[...] * ATT_SCALE
        dov = do_ref[...]

        def off(it, carry):
            half = (base + it) % 2
            for cp in fetches(group, it, half):
                cp.wait()
            for cp in fetches(group, it + 1, 1 - half):
                cp.start()
            cg, dq = cg_s[...], dq_s[...]
            for j in range(r):
                cg, dq = step(it * r + j, half, j, slice(0, tq), q, dov, None, cg, dq)
            cg_s[...] = cg
            dq_s[...] = dq
            return carry

        lax.fori_loop(0, qi, off, 0)

        half = (base + qi) % 2
        for cp in fetches(group, qi, half):
            cp.wait()

        @pl.when(group < n_steps - 1)
        def _():
            for cp in fetches(group + 1, 0, 1 - half):
                cp.start()

        for j in range(r):
            rows = slice(j * QB, tq)
            cg, dq = step(qi * r + j, half, j, rows, q_ref[rows, :] * ATT_SCALE, do_ref[rows, :],
                          _diag_mask(tq, j), cg_s[rows, :], dq_s[rows, :])
            cg_s[rows, :] = cg
            dq_s[rows, :] = dq
        dq_ref[...] = (dq_s[...] * ATT_SCALE).astype(BF16)

        @pl.when(qi == n_q - 1)
        def _():
            dk_out[...] = dk_ref[...].astype(BF16)
            dv_out[...] = dv_ref[...].astype(BF16)

    qspec = pl.BlockSpec((tq, LANES), lambda b, p, qi: (b * n_q + qi, p))
    seq = lambda off: pl.BlockSpec((S, LANES), lambda b, p, qi: (b, off + p))
    return pl.pallas_call(
        body, name="attn_bwd", grid=(n_seq, n_p, n_q),
        in_specs=[qspec, seq(n_p), seq(2 * n_p), qspec,
                  pl.BlockSpec((2 * QB, 2 * QB), lambda b, p, qi: (0, 0)), ANY],
        out_specs=[qspec, seq(0), seq(0)],
        out_shape=[jax.ShapeDtypeStruct((T, D_MODEL), BF16)] * 3,
        scratch_shapes=[pltpu.VMEM((n_k, 2 * QB, LANES), BF16), pltpu.VMEM((n_k, 2 * QB, LANES), BF16),
                        pltpu.VMEM((tq, 2 * QB), F32), pltpu.VMEM((tq, LANES), F32),
                        pltpu.VMEM((S, LANES), F32), pltpu.VMEM((S, LANES), F32),
                        pltpu.VMEM((2, r, tq, 2 * QB), BF16), pltpu.SemaphoreType.DMA((2,))],
        compiler_params=_cp("arbitrary", "arbitrary", "arbitrary"),
    )(qkv, qkv, qkv, do, u_prefix, a_saved)


def _shifted_copies(sh_ref):
    rows = sh_ref.shape[1] - SUBLANES
    for s in range(1, SUBLANES):
        sh_ref[s, 0:rows, :] = sh_ref[0, s:s + rows, :]


def _shifted(sh_ref, start, n):
    s = start % SUBLANES
    return sh_ref[s, start - s:start - s + n, :]


def _glu_with_halo(av_ref, ag_ref, avh_ref, agh_ref, a0_s, first, ts):
    hal = avh_ref[...] * _sigmoid(agh_ref[...])
    a0_s[0, 0:HALO, :] = jnp.where(first, 0.0, hal)
    a0_s[0, HALO:HALO + ts, :] = av_ref[...] * _sigmoid(ag_ref[...])
    _shifted_copies(a0_s)


def _mix_specs(ts, n_r, with_left):
    blk = lambda c: pl.BlockSpec((ts, CA), lambda b, r: (b * n_r + r, c))
    per = ts // HALO
    left = lambda c: pl.BlockSpec((HALO, CA), lambda b, r: (jnp.maximum((b * n_r + r) * per - 1, 0), c))
    return blk, (left if with_left else None)


def _mix_fwd(z, conv_w, conv_b, ln_a_g, ln_a_b, ln_v_g, ln_v_b, ws, bias2d, n_seq):
    T = z.shape[0]
    S = T // n_seq
    ts = _tile(S, 512)
    n_r = S // ts
    shift = HALO - (CONV_WIDTH - 1)

    def body(av_ref, ag_ref, avh_ref, agh_ref, u_ref, v_ref, cw_ref, cb_ref, lag_ref, lab_ref,
             lvg_ref, lvb_ref, ws_ref, bias_ref, cat_ref, a1_ref, a0_s):
        _glu_with_halo(av_ref, ag_ref, avh_ref, agh_ref, a0_s, pl.program_id(1) == 0, ts)
        for rb in range(ts // CONV_ROWS):
            base = rb * CONV_ROWS
            acc = jnp.broadcast_to(cb_ref[...], (CONV_ROWS, CA))
            for k in range(CONV_WIDTH):
                acc = acc + cw_ref[k:k + 1, :] * _shifted(a0_s, base + shift + k, CONV_ROWS)
            a1_ref[base:base + CONV_ROWS, :] = acc
        y, _, _ = _ln_fwd(a1_ref[...], lag_ref[...], lab_ref[...])
        cat_ref[:, 0:CA] = (y * _sigmoid(y)).astype(BF16)
        for gi in range(GB):
            sl = slice(gi * DB, (gi + 1) * DB)
            v1, _, _ = _ln_fwd(v_ref[:, sl], lvg_ref[:, sl], lvb_ref[:, sl])
            v1 = v1.astype(BF16)
            for c in range(ts // CHUNK):
                rs = slice(c * CHUNK, (c + 1) * CHUNK)
                v2 = _dot(ws_ref[gi], v1[rs], NN) + bias_ref[:, sl]
                cat_ref[rs, CA + gi * DB:CA + (gi + 1) * DB] = (u_ref[rs, sl] * v2).astype(BF16)

    blk, left = _mix_specs(ts, n_r, True)
    vec = pl.BlockSpec((1, CA), lambda b, r: (0, 0))
    return pl.pallas_call(
        body, name="mix_fwd", grid=(n_seq, n_r),
        in_specs=[blk(0), blk(1), left(0), left(1), blk(2), blk(3),
                  pl.BlockSpec((CONV_WIDTH, CA), lambda b, r: (0, 0)), vec, vec, vec, vec, vec,
                  pl.BlockSpec((GB, CHUNK, CHUNK), lambda b, r: (0, 0, 0)),
                  pl.BlockSpec((CHUNK, CB), lambda b, r: (0, 0))],
        out_specs=[pl.BlockSpec((ts, CA + CB), lambda b, r: (b * n_r + r, 0)), blk(0)],
        out_shape=[jax.ShapeDtypeStruct((T, CA + CB), BF16), jax.ShapeDtypeStruct((T, CA), F32)],
        scratch_shapes=[pltpu.VMEM((SUBLANES, HALO + ts, CA), F32)],
        compiler_params=_cp("parallel", "parallel"),
    )(z, z, z, z, z, z, conv_w, conv_b, ln_a_g, ln_a_b, ln_v_g, ln_v_b, ws, bias2d)


def _mix_bwd_rows(dcat, z, a1, ln_a_g, ln_a_b, ln_v_g, ln_v_b, ws, ws_t, bias2d, n_seq):
    T = z.shape[0]
    S = T // n_seq
    ts = _tile(S, 512)
    n_r = S // ts

    def body(dc_ref, u_ref, v_ref, a1_ref, lag_ref, lab_ref, lvg_ref, lvb_ref, ws_ref, wst_ref, bias_ref,
             da1_ref, dz_ref, dlag_ref, dlab_ref, dlvg_ref, dlvb_ref, dws_ref, dsb_ref, dv1_s, dbias_s):
        first = (pl.program_id(0) == 0) & (pl.program_id(1) == 0)
        last = (pl.program_id(0) == n_seq - 1) & (pl.program_id(1) == n_r - 1)

        @pl.when(first)
        def _():
            for ref in (dlag_ref, dlab_ref, dlvg_ref, dlvb_ref, dws_ref, dbias_s):
                ref[...] = jnp.zeros_like(ref)

        lag = lag_ref[...]
        y, xh, r = _ln_fwd(a1_ref[...], lag, lab_ref[...])
        sig = _sigmoid(y)
        dy = dc_ref[:, 0:CA] * (sig * (1.0 + y * (1.0 - sig)))
        dlag_ref[...] += jnp.sum(dy * xh, axis=0, keepdims=True)
        dlab_ref[...] += jnp.sum(dy, axis=0, keepdims=True)
        da1_ref[...] = _ln_bwd(dy, xh, r, lag)

        tril = (lax.broadcasted_iota(jnp.int32, (CHUNK, CHUNK), 0)
                >= lax.broadcasted_iota(jnp.int32, (CHUNK, CHUNK), 1))
        for gi in range(GB):
            sl = slice(gi * DB, (gi + 1) * DB)
            lvg = lvg_ref[:, sl]
            v1, vh, vr = _ln_fwd(v_ref[:, sl], lvg, lvb_ref[:, sl])
            v1 = v1.astype(BF16)
            for c in range(ts // CHUNK):
                rs = slice(c * CHUNK, (c + 1) * CHUNK)
                v2 = _dot(ws_ref[gi], v1[rs], NN) + bias_ref[:, sl]
                dbo = dc_ref[rs, CA + gi * DB:CA + (gi + 1) * DB]
                dz_ref[rs, sl] = (dbo * v2).astype(BF16)
                dv2 = dbo * u_ref[rs, sl]
                dbias_s[:, sl] += dv2
                dv2b = dv2.astype(BF16)
                dws_ref[gi] += jnp.where(tril, _dot(dv2b, v1[rs], NT), 0.0)
                dv1_s[rs, :] = _dot(wst_ref[gi], dv2b, NN)
            dv1 = dv1_s[...]
            dlvg_ref[:, sl] += jnp.sum(dv1 * vh, axis=0, keepdims=True)
            dlvb_ref[:, sl] += jnp.sum(dv1, axis=0, keepdims=True)
            dz_ref[:, CB + gi * DB:CB + (gi + 1) * DB] = _ln_bwd(dv1, vh, vr, lvg).astype(BF16)

        @pl.when(last)
        def _():
            col = lax.broadcasted_iota(jnp.int32, (CHUNK, GB), 1)
            out = jnp.zeros((CHUNK, GB), F32)
            for gi in range(GB):
                s = jnp.sum(dbias_s[:, gi * DB:(gi + 1) * DB], axis=1, keepdims=True)
                out = out + jnp.where(col == gi, s, 0.0)
            dsb_ref[...] = out

    blk, _ = _mix_specs(ts, n_r, False)
    vec = pl.BlockSpec((1, CA), lambda b, r: (0, 0))
    mat = pl.BlockSpec((GB, CHUNK, CHUNK), lambda b, r: (0, 0, 0))
    wide = pl.BlockSpec((ts, CA + CB), lambda b, r: (b * n_r + r, 0))
    return pl.pallas_call(
        body, name="mix_bwd_rows", grid=(n_seq, n_r),
        in_specs=[wide, blk(2), blk(3), blk(0), vec, vec, vec, vec, mat, mat,
                  pl.BlockSpec((CHUNK, CB), lambda b, r: (0, 0))],
        out_specs=[blk(0), wide, vec, vec, vec, vec, mat, pl.BlockSpec((CHUNK, GB), lambda b, r: (0, 0))],
        out_shape=[jax.ShapeDtypeStruct((T, CA), F32), jax.ShapeDtypeStruct((T, 2 * CB), BF16)]
        + [jax.ShapeDtypeStruct((1, CA), F32)] * 4
        + [jax.ShapeDtypeStruct((GB, CHUNK, CHUNK), F32), jax.ShapeDtypeStruct((CHUNK, GB), F32)],
        scratch_shapes=[pltpu.VMEM((ts, DB), F32), pltpu.VMEM((CHUNK, CB), F32)],
        compiler_params=_cp("arbitrary", "arbitrary"),
    )(dcat, z, z, a1, ln_a_g, ln_a_b, ln_v_g, ln_v_b, ws, ws_t, bias2d)


def _mix_bwd_conv(da1, z, conv_w, n_seq):
    T = z.shape[0]
    S = T // n_seq
    ts = _tile(S, 512)
    n_r = S // ts
    per = ts // HALO
    shift = HALO - (CONV_WIDTH - 1)
    fold = CONV_ROWS // 8

    def body(d_ref, dh_ref, av_ref, ag_ref, avh_ref, agh_ref, cw_ref,
             dz_ref, dcw_ref, dcb_ref, a0_s, d1_s, da0_s, dw8_s):
        first = (pl.program_id(0) == 0) & (pl.program_id(1) == 0)
        last = (pl.program_id(0) == n_seq - 1) & (pl.program_id(1) == n_r - 1)

        @pl.when(first)
        def _():
            dw8_s[...] = jnp.zeros_like(dw8_s)
            dcb_ref[...] = jnp.zeros_like(dcb_ref)

        _glu_with_halo(av_ref, ag_ref, avh_ref, agh_ref, a0_s, pl.program_id(1) == 0, ts)
        d1_s[0, 0:ts, :] = d_ref[...]
        d1_s[0, ts:ts + HALO, :] = jnp.where(pl.program_id(1) == n_r - 1, 0.0, dh_ref[...])
        _shifted_copies(d1_s)
        dcb_ref[...] += jnp.sum(d_ref[...], axis=0, keepdims=True)
        for rb in range(ts // CONV_ROWS):
            base = rb * CONV_ROWS
            dcur = d1_s[0, base:base + CONV_ROWS, :]
            acc = jnp.zeros((CONV_ROWS, CA), F32)
            for k in range(CONV_WIDTH):
                back = CONV_WIDTH - 1 - k
                acc = acc + cw_ref[k:k + 1, :] * _shifted(d1_s, base + back, CONV_ROWS)
                prod = dcur * _shifted(a0_s, base + shift + k, CONV_ROWS)
                part = prod[0:8]
                for f in range(1, fold):
                    part = part + prod[8 * f:8 * f + 8]
                dw8_s[k] += part
            da0_s[base:base + CONV_ROWS, :] = acc
        da0 = da0_s[...]
        sig = _sigmoid(ag_ref[...])
        dz_ref[:, 0:CA] = (da0 * sig).astype(BF16)
        dz_ref[:, CA:2 * CA] = (da0 * av_ref[...] * sig * (1.0 - sig)).astype(BF16)

        @pl.when(last)
        def _():
            for k in range(CONV_WIDTH):
                dcw_ref[k:k + 1, :] = jnp.sum(dw8_s[k], axis=0, keepdims=True)

    blk, left = _mix_specs(ts, n_r, True)
    n_halo_blocks = T // HALO
    right = pl.BlockSpec((HALO, CA), lambda b, r: (jnp.minimum((b * n_r + r + 1) * per, n_halo_blocks - 1), 0))
    return pl.pallas_call(
        body, name="mix_bwd_conv", grid=(n_seq, n_r),
        in_specs=[blk(0), right, blk(0), blk(1), left(0), left(1),
                  pl.BlockSpec((CONV_WIDTH, CA), lambda b, r: (0, 0))],
        out_specs=[pl.BlockSpec((ts, 2 * CA), lambda b, r: (b * n_r + r, 0)),
                   pl.BlockSpec((CONV_WIDTH, CA), lambda b, r: (0, 0)), pl.BlockSpec((1, CA), lambda b, r: (0, 0))],
        out_shape=[jax.ShapeDtypeStruct((T, 2 * CA), BF16), jax.ShapeDtypeStruct((CONV_WIDTH, CA), F32),
                   jax.ShapeDtypeStruct((1, CA), F32)],
        scratch_shapes=[pltpu.VMEM((SUBLANES, HALO + ts, CA), F32), pltpu.VMEM((SUBLANES, ts + HALO, CA), F32),
                        pltpu.VMEM((ts, CA), F32), pltpu.VMEM((CONV_WIDTH, 8, CA), F32)],
        compiler_params=_cp("arbitrary", "arbitrary"),
    )(da1, da1, z, z, z, z, conv_w)


def _row_tile(R, want):
    t = min(R, want)
    t -= t % 8
    while t > 8 and R % t:
        t -= 8
    return t if t >= 8 and R % t == 0 else R


def _adam_step(w, g, m, v):
    nm = ADAM_B1 * m + (1.0 - ADAM_B1) * g
    nv = ADAM_B2 * v + (1.0 - ADAM_B2) * (g * g)
    m_hat = nm / (1.0 - ADAM_B1 ** ADAM_STEP)
    v_hat = nv / (1.0 - ADAM_B2 ** ADAM_STEP)
    return -ADAM_LR * (m_hat / (jnp.sqrt(v_hat) + ADAM_EPS) + ADAM_WD * w), nm, nv


def _adamw_parts(parts, w, m, v, name):
    L, n, C = w.shape
    assert len(parts) == L
    tr = _row_tile(n, 192)
    n_i = n // tr

    def body(*refs):
        p_refs = refs[:L]
        w_ref, m_ref, v_ref, g_ref, d_ref, nm_ref, nv_ref = refs[L:]
        for k in range(L):
            @pl.when(pl.program_id(0) == k)
            def _(k=k):
                acc = p_refs[k][0].astype(F32)
                for s in range(1, N_DEV):
                    acc = acc + p_refs[k][s].astype(F32)
                g_ref[...] = acc

        d_ref[...], nm_ref[...], nv_ref[...] = _adam_step(w_ref[...], g_ref[...], m_ref[...], v_ref[...])

    def part_spec(k):
        return pl.BlockSpec((N_DEV, tr, C),
                            lambda l, i: (0, jnp.where(l == k, i, jnp.where(l < k, 0, n_i - 1)), 0))

    blk = pl.BlockSpec((None, tr, C), lambda l, i: (l, i, 0))
    return pl.pallas_call(
        body, name=name, grid=(L, n_i),
        in_specs=[part_spec(k) for k in range(L)] + [blk] * 3, out_specs=[blk] * 4,
        out_shape=[jax.ShapeDtypeStruct((L, n, C), F32)] * 4,
        compiler_params=_cp("arbitrary", "arbitrary"),
    )(*parts, w, m, v)


def _adamw(w, g, m, v, name):
    R, C = w.shape
    tr = _row_tile(R, 256)

    def body(w_ref, g_ref, m_ref, v_ref, d_ref, nm_ref, nv_ref):
        d_ref[...], nm_ref[...], nv_ref[...] = _adam_step(w_ref[...], g_ref[...], m_ref[...], v_ref[...])

    blk = pl.BlockSpec((tr, C), lambda i: (i, 0))
    return pl.pallas_call(
        body, name=name, grid=(R // tr,),
        in_specs=[blk] * 4, out_specs=[blk] * 3,
        out_shape=[jax.ShapeDtypeStruct((R, C), F32)] * 3,
        compiler_params=_cp("parallel"),
    )(w, g, m, v)


def _me():
    return lax.axis_index("x"), lax.axis_index("y"), lax.axis_index("c")


def _block_rows(ref, dev, n):
    start = (4 * dev[0] + 2 * dev[1] + dev[2]) * n
    if len(ref.shape) == 2:
        return ref.at[pl.ds(start, n), :]
    return ref.at[:, pl.ds(start, n), :]


def _all_gather(shards):
    na = len(shards)
    ns = [s.shape[-2] for s in shards]

    def body(*refs):
        ins, outs = refs[:na], refs[na:2 * na]
        send_sems, recv_sems, local_sems = refs[2 * na:]
        x, y, c = _me()
        me, sibling = (x, y, c), (x, y, 1 - c)
        chips = [(1 - x, y), (x, 1 - y), (1 - x, 1 - y)]

        def copy(a, k, block, to, src=None):
            dst = _block_rows(outs[a], block, ns[a])
            return pltpu.make_async_remote_copy(
                src_ref=dst if src is None else src, dst_ref=dst,
                send_sem=send_sems.at[a, k], recv_sem=recv_sems.at[a, k], device_id=to, device_id_type=MESH)

        mine = [pltpu.make_async_copy(ins[a], _block_rows(outs[a], me, ns[a]), local_sems.at[a]) for a in range(na)]
        for cp in mine:
            cp.start()
        first = []
        for a in range(na):
            first.append(copy(a, 0, me, sibling, src=ins[a]))
            first += [copy(a, 1 + j, me, (*chip, c), src=ins[a]) for j, chip in enumerate(chips)]
        for cp in first:
            cp.start()
        passed = []
        for j, chip in enumerate(chips):
            for a in range(na):
                copy(a, 1 + j, (*chip, c), me).wait_recv()
                fwd = copy(a, 4 + j, (*chip, c), sibling)
                fwd.start()
                passed.append(fwd)
        for a in range(na):
            copy(a, 0, sibling, me).wait_recv()
            for j, chip in enumerate(chips):
                copy(a, 4 + j, (*chip, 1 - c), me).wait_recv()
        for cp in first + passed:
            cp.wait_send()
        for cp in mine:
            cp.wait()

    out_shape = [jax.ShapeDtypeStruct(s.shape[:-2] + (N_DEV * s.shape[-2], s.shape[-1]), s.dtype) for s in shards]
    return pl.pallas_call(
        body, name="weights_all_gather",
        in_specs=[ANY] * na, out_specs=[ANY] * na, out_shape=out_shape,
        scratch_shapes=[pltpu.SemaphoreType.DMA((na, 7)), pltpu.SemaphoreType.DMA((na, 7)),
                        pltpu.SemaphoreType.DMA((na,))],
    )(*shards)


def _split_copies(gather, srcs, lands, send_sems, recv_sems, ns):
    x, y, c = _me()
    me = (x, y, c)
    my_slot = 4 * x + 2 * y + c
    copies = []
    for mask in range(1, N_DEV):
        peer = (x ^ (mask >> 2), y ^ ((mask >> 1) & 1), c ^ (mask & 1))
        for a in range(len(srcs)):
            if gather:
                src, dst = srcs[a], _block_rows(lands[a], me, ns[a])
            else:
                src, dst = _block_rows(srcs[a], peer, ns[a]), lands[a].at[my_slot]
            sem = a * (N_DEV - 1) + mask - 1
            copies.append(pltpu.make_async_remote_copy(
                src_ref=src, dst_ref=dst, send_sem=send_sems.at[sem], recv_sem=recv_sems.at[sem],
                device_id=peer, device_id_type=MESH))
    return copies


HBM_SPEC = pl.BlockSpec(memory_space=pltpu.HBM)
SEM_SPEC = pl.BlockSpec(memory_space=pltpu.SEMAPHORE)


def _split_start(gather, srcs, name, dep=None):
    na = len(srcs)
    x, y, c = _me()
    mine = 4 * x + 2 * y + c
    if gather:
        ns = [s.shape[-2] for s in srcs]
        lands = [lax.dynamic_update_slice(
            lax.empty(s.shape[:-2] + (N_DEV * s.shape[-2], s.shape[-1]), s.dtype), s,
            (0,) * (s.ndim - 2) + (mine * s.shape[-2], 0)) for s in srcs]
    else:
        ns = [s.shape[-2] // N_DEV for s in srcs]
        lands = [lax.dynamic_update_slice(
            lax.empty((N_DEV, n, s.shape[-1]), s.dtype),
            lax.dynamic_slice(s, (mine * n, 0), (n, s.shape[-1]))[None], (mine, 0, 0)) for s, n in zip(srcs, ns)]
    n_in = 2 * na + (dep is not None)

    def body(*refs):
        send_sems, recv_sems = refs[n_in], refs[n_in + 1]
        for cp in _split_copies(gather, refs[:na], refs[na:2 * na], send_sems, recv_sems, ns):
            cp.start()
        refs[-1][...] = jnp.zeros_like(refs[-1])

    hbm = lambda a: pltpu.with_memory_space_constraint(a, pltpu.HBM)
    args = [hbm(a) for a in srcs] + [hbm(a) for a in lands] + ([dep] if dep is not None else [])
    out = pl.pallas_call(
        body, name=name,
        in_specs=[HBM_SPEC] * (2 * na) + ([ANY] if dep is not None else []),
        out_specs=[SEM_SPEC, SEM_SPEC] + [HBM_SPEC] * (2 * na) + [pl.BlockSpec(memory_space=pltpu.VMEM)],
        out_shape=[pltpu.SemaphoreType.DMA((na * (N_DEV - 1),)), pltpu.SemaphoreType.DMA((na * (N_DEV - 1),))]
        + [pltpu.HBM(a.shape, a.dtype) for a in srcs + lands] + [jax.ShapeDtypeStruct((8, LANES), F32)],
        input_output_aliases={i: 2 + i for i in range(2 * na)},
        compiler_params=pltpu.CompilerParams(has_side_effects=pltpu.SideEffectType.DATAFLOW_SIDE_EFFECTING),
    )(*args)
    return (gather, ns, out[0], out[1], list(out[2:2 + na]), list(out[2 + na:2 + 2 * na])), out[-1]


def _split_wait(handle, after, name):
    gather, ns, send, recv, srcs, lands = handle
    na = len(srcs)

    def body(*refs):
        send_sems, recv_sems = refs[2 * na], refs[2 * na + 1]
        for cp in _split_copies(gather, refs[:na], refs[na:2 * na], send_sems, recv_sems, ns):
            cp.wait_send()
            cp.wait_recv()

    out = pl.pallas_call(
        body, name=name,
        in_specs=[HBM_SPEC] * (2 * na) + [SEM_SPEC, SEM_SPEC, ANY],
        out_specs=[HBM_SPEC] * (2 * na),
        out_shape=[pltpu.HBM(a.shape, a.dtype) for a in srcs + lands],
        input_output_aliases={i: i for i in range(2 * na)},
        compiler_params=pltpu.CompilerParams(has_side_effects=pltpu.SideEffectType.DATAFLOW_SIDE_EFFECTING),
    )(*srcs, *lands, send, recv, after)
    return list(out[:na]), list(out[na:])


def _small_all_reduce(buf):
    R = buf.shape[0]

    def body(b_ref, o_ref, recv_ref, send_sems, recv_sems):
        x, y, c = _me()
        my_slot = 4 * x + 2 * y + c
        recv_ref[my_slot] = b_ref[...]
        copies = []
        for mask in range(1, N_DEV):
            peer = (x ^ (mask >> 2), y ^ ((mask >> 1) & 1), c ^ (mask & 1))
            copies.append(pltpu.make_async_remote_copy(
                src_ref=b_ref, dst_ref=recv_ref.at[my_slot],
                send_sem=send_sems.at[mask - 1], recv_sem=recv_sems.at[mask - 1],
                device_id=peer, device_id_type=MESH))
        for cp in copies:
            cp.start()
        for cp in copies:
            cp.wait()
        acc = recv_ref[0]
        for k in range(1, N_DEV):
            acc = acc + recv_ref[k]
        o_ref[...] = acc

    return pl.pallas_call(
        body, name="small_all_reduce",
        in_specs=[pl.BlockSpec(memory_space=pltpu.VMEM)], out_specs=pl.BlockSpec(memory_space=pltpu.VMEM),
        out_shape=jax.ShapeDtypeStruct((R, LANES), F32),
        scratch_shapes=[pltpu.VMEM((N_DEV, R, LANES), F32), pltpu.SemaphoreType.DMA((7,)),
                        pltpu.SemaphoreType.DMA((7,))],
        compiler_params=pltpu.CompilerParams(vmem_limit_bytes=VMEM_LIMIT),
    )(buf)


def _pack(arrays):
    flat = jnp.concatenate([a.reshape(-1) for a in arrays])
    pad = (-flat.shape[0]) % (8 * LANES)
    return jnp.pad(flat, (0, pad)).reshape(-1, LANES)


def _unpack(buf, shapes):
    flat = buf.reshape(-1)
    out, off = [], 0
    for s in shapes:
        n = 1
        for d in s:
            n *= d
        out.append(flat[off:off + n].reshape(s))
        off += n
    return out


def _ffn_index(layer, second):
    return (2 * layer + second) * 3


def kernel(x, g_ffn1, w_ffn1_gate, w_ffn1_up, w_ffn1_down, g_mix, w_in_ab, conv_w, conv_b, ln_a_g, ln_a_b, ln_v_g, ln_v_b, sp_w, sp_b, w_out_ab, w_qkv, w_o, g_ffn2, w_ffn2_gate, w_ffn2_up, w_ffn2_down, g_final, loss_target, m_g_ffn1, m_w_ffn1_gate, m_w_ffn1_up, m_w_ffn1_down, m_g_mix, m_w_in_ab, m_conv_w, m_conv_b, m_ln_a_g, m_ln_a_b, m_ln_v_g, m_ln_v_b, m_sp_w, m_sp_b, m_w_out_ab, m_w_qkv, m_w_o, m_g_ffn2, m_w_ffn2_gate, m_w_ffn2_up, m_w_ffn2_down, m_g_final, v_g_ffn1, v_w_ffn1_gate, v_w_ffn1_up, v_w_ffn1_down, v_g_mix, v_w_in_ab, v_conv_w, v_conv_b, v_ln_a_g, v_ln_a_b, v_ln_v_g, v_ln_v_b, v_sp_w, v_sp_b, v_w_out_ab, v_w_qkv, v_w_o, v_g_ffn2, v_w_ffn2_gate, v_w_ffn2_up, v_w_ffn2_down, v_g_final):
    n_seq, S, D = x.shape
    T = n_seq * S
    depth = g_ffn1.shape[0]
    assert depth == 2 and D == D_MODEL
    my_block = 4 * lax.axis_index("x") + 2 * lax.axis_index("y") + lax.axis_index("c")

    ffn_parts = []
    for l in range(depth):
        for gate, up, down in ((w_ffn1_gate, w_ffn1_up, w_ffn1_down), (w_ffn2_gate, w_ffn2_up, w_ffn2_down)):
            ffn_parts += [gate[l].T, up[l].T, down[l]]
    ffn_shard = lambda k: jnp.stack(ffn_parts[3 * k:3 * k + 3]).astype(BF16)
    conv_w_pad = jnp.zeros((HALO, conv_w.shape[2]), F32).at[:CONV_WIDTH].set(conv_w[0]).T
    w_ffn = [None] * (2 * depth)
    w_ffn[0], conv_w_t = _all_gather([ffn_shard(0), conv_w_pad])
    conv_w_full = conv_w_t.T[:CONV_WIDTH]
    shards_b = [w_out_ab[0].astype(BF16), ffn_shard(1)]
    shards_d = [w_qkv[0].T.astype(BF16), w_o[0].astype(BF16), ffn_shard(3)]
    gather_a, token = _split_start(True, [w_in_ab[0].T.astype(BF16)], "gather_a_start", dep=conv_w_t)
    gather_b, token = _split_start(True, shards_b, "gather_b_start", dep=token)
    gather_c, token = _split_start(True, [ffn_shard(2)], "gather_c_start", dep=token)
    gather_d, token = _split_start(True, shards_d, "gather_d_start", dep=token)

    def gathered(handle, after, name):
        return _split_wait(handle, after, name)[1]

    row = lambda a: a.reshape(1, -1)
    tril = jnp.tril(jnp.ones((CHUNK, CHUNK), dtype=bool))
    ws = jnp.where(tril[None], sp_w[0], 0.0).astype(BF16)
    ws_t = jnp.swapaxes(ws, 1, 2)
    bias2d = jnp.repeat(sp_b[0].T, DB, axis=1)
    conv_b2, lag, lab = row(conv_b[0]), row(ln_a_g[0]), row(ln_a_b[0])
    lvg, lvb = row(ln_v_g[0]), row(ln_v_b[0])

    x0 = x.reshape(T, D)
    target = loss_target.reshape(T, D)
    saved = []
    xc = x0
    for l in range(depth):
        xa, a1, b1, h1 = _ffn_fwd(xc, row(g_ffn1[l]), w_ffn[2 * l], 0, f"ffn1_fwd_{l}", dep=token)
        if l % 2 == 0:
            w_in_t, = gathered(gather_a, xa, "gather_a_wait")
            z, hm = _mm_nt(xa, w_in_t, row(g_mix[l]), F32, "mix_in_proj")
            cat, conv_out = _mix_fwd(z, conv_w_full, conv_b2, lag, lab, lvg, lvb, ws, bias2d, n_seq)
            w_out, w_ffn[1] = gathered(gather_b, cat, "gather_b_wait")
            xb = _mm_nn_res(cat, w_out, xa, "mix_out_proj")
            mixer = (z, hm, cat, conv_out)
        else:
            w_qkv_t, w_o_full, w_ffn[3] = gathered(gather_d, xa, "gather_d_wait")
            qkv, hm = _mm_nt(xa, w_qkv_t, row(g_mix[l]), BF16, "qkv_proj")
            o, att = _attn_fwd(qkv, n_seq)
            xb = _mm_nn_res(o, w_o_full, xa, "attn_out_proj")
            mixer = (qkv, hm, o, att)
        xn, a2, b2, h2 = _ffn_fwd(xb, row(g_ffn2[l]), w_ffn[2 * l + 1], 0, f"ffn2_fwd_{l}")
        saved.append((xc, a1, b1, h1, xa, mixer, xb, a2, b2, h2))
        xc = xn
        if l == 0:
            w_ffn[2], = gathered(gather_c, xc, "gather_c_wait")

    g, loss_part, dg_final = _loss_head(xc, row(g_final), target)

    dg_ffn1, dg_ffn2, dg_mix = [None] * depth, [None] * depth, [None] * depth
    exchanges = {}
    token = None

    def ffn_back(g, xin, gvec, a, b, h, k, tag, token):
        g, dg, da, db, s, gh = _ffn_bwd(g, xin, gvec, a, b, w_ffn[k], 0, f"ffn{tag}_bwd", dep=token)
        if k == 0:
            return g, dg, (da, db, s, gh, h)
        dws = [_mm_tn(da, h, f"dw_gate{tag}"), _mm_tn(db, h, f"dw_up{tag}"), _mm_tn(s, gh, f"dw_down{tag}")]
        exchanges[f"ffn{k}"], token = _split_start(False, dws, f"exchange_ffn{tag}_start")
        return g, dg, token

    for l in reversed(range(depth)):
        xin, a1, b1, h1, xa, mixer, xb, a2, b2, h2 = saved[l]
        g, dg_ffn2[l], token = ffn_back(g, xb, row(g_ffn2[l]), a2, b2, h2, 2 * l + 1, f"2_{l}", token)
        if l % 2 == 0:
            z, hm, cat, conv_out = mixer
            dcat = _mm_nt(g, w_out, None, F32, "mix_out_bwd", dep=token)
            d_w_out = _mm_tn(cat, g, "dw_out")
            (da1, dz_uv, d_lag, d_lab, d_lvg, d_lvb, d_ws, d_sb) = _mix_bwd_rows(
                dcat, z, conv_out, lag, lab, lvg, lvb, ws, ws_t, bias2d, n_seq)
            dz_a, d_cw, d_cb = _mix_bwd_conv(da1, z, conv_w_full, n_seq)
            d_w_in_t = jnp.concatenate([_mm_tn(dz_a, hm, "dw_in_a"), _mm_tn(dz_uv, hm, "dw_in_uv")])
            exchanges["mix"], token = _split_start(False, [d_w_out, d_w_in_t], "exchange_mix_start")
            g, dg_mix[l] = _mm_nn_rmsbwd([dz_a, dz_uv], w_in_t, xa, row(g_mix[l]), g, "mix_in_bwd", dep=token)
        else:
            qkv, hm, o, att = mixer
            do = _mm_nt(g, w_o_full, None, BF16, "attn_out_bwd", dep=token)
            d_w_o = _mm_tn(o, g, "dw_o")
            dq, dk, dv = _attn_bwd(qkv, do, att, n_seq)
            d_w_qkv_t = jnp.concatenate([_mm_tn(dq, hm, "dw_q"), _mm_tn(dk, hm, "dw_k"), _mm_tn(dv, hm, "dw_v")])
            exchanges["attn"], token = _split_start(False, [d_w_o, d_w_qkv_t], "exchange_attn_start")
            g, dg_mix[l] = _mm_nn_rmsbwd([dq, dk, dv], w_qkv_t, xa, row(g_mix[l]), g, "qkv_bwd", dep=token)
        g, dg_ffn1[l], token = ffn_back(g, xin, row(g_ffn1[l]), a1, b1, h1, 2 * l, f"1_{l}", token)
    grad_x = g.reshape(n_seq, S, D)

    small = [jnp.concatenate(dg_ffn1), jnp.concatenate(dg_mix), d_cw, d_cb, d_lag, d_lab, d_lvg, d_lvb,
             jnp.where(tril[None], d_ws, 0.0), d_sb.T, jnp.concatenate(dg_ffn2), dg_final, loss_part[:, :1]]
    small_shapes = [(depth, D), (depth, D), (CONV_WIDTH, CA), (1, CA), (1, CA), (1, CA), (1, GB, DB), (1, GB, DB),
                    (1, GB, CHUNK, CHUNK), (1, GB, CHUNK), (depth, D), (D,), ()]
    small_sum = _small_all_reduce(_pack(small))
    red = _unpack(small_sum, small_shapes)
    (gr_g_ffn1, gr_g_mix, gr_cw_full, gr_cb, gr_lag, gr_lab, gr_lvg, gr_lvb, gr_sp_w, gr_sp_b,
     gr_g_ffn2, gr_g_final, loss) = red
    n_cw = conv_w.shape[2]
    gr_cw = lax.dynamic_slice(gr_cw_full, (0, my_block * n_cw), (CONV_WIDTH, n_cw))[None]

    da, db, s, gh, h = token
    token = small_sum
    for which, lhs, rhs in ((2, s, gh), (1, db, h), (0, da, h)):
        dw = _mm_tn(lhs, rhs, f"dw_ffn0_{which}", dep=token, out_dtype=BF16)
        exchanges[f"ffn0_{which}"], token = _split_start(False, [dw], f"exchange_ffn0_{which}_start")

    def landed(key, after):
        return _split_wait(exchanges[key], after, f"exchange_{key}_wait")[1]

    parts_ffn = [None] * (6 * depth)
    for k in range(1, 2 * depth):
        parts_ffn[3 * k:3 * k + 3] = landed(f"ffn{k}", token)
    parts_out, parts_in = landed("mix", token)
    parts_o, parts_qkv = landed("attn", token)

    grads = {
        "g_ffn1": gr_g_ffn1, "g_mix": gr_g_mix, "conv_w": gr_cw, "conv_b": gr_cb, "ln_a_g": gr_lag,
        "ln_a_b": gr_lab, "ln_v_g": gr_lvg, "ln_v_b": gr_lvb, "sp_w": gr_sp_w, "sp_b": gr_sp_b,
        "g_ffn2": gr_g_ffn2, "g_final": gr_g_final,
    }
    weights = dict(g_ffn1=g_ffn1, w_ffn1_gate=w_ffn1_gate, w_ffn1_up=w_ffn1_up, w_ffn1_down=w_ffn1_down, g_mix=g_mix,
                   w_in_ab=w_in_ab, conv_w=conv_w, conv_b=conv_b, ln_a_g=ln_a_g, ln_a_b=ln_a_b, ln_v_g=ln_v_g,
                   ln_v_b=ln_v_b, sp_w=sp_w, sp_b=sp_b, w_out_ab=w_out_ab, w_qkv=w_qkv, w_o=w_o, g_ffn2=g_ffn2,
                   w_ffn2_gate=w_ffn2_gate, w_ffn2_up=w_ffn2_up, w_ffn2_down=w_ffn2_down, g_final=g_final)
    m_in = dict(g_ffn1=m_g_ffn1, w_ffn1_gate=m_w_ffn1_gate, w_ffn1_up=m_w_ffn1_up, w_ffn1_down=m_w_ffn1_down,
                g_mix=m_g_mix, w_in_ab=m_w_in_ab, conv_w=m_conv_w, conv_b=m_conv_b, ln_a_g=m_ln_a_g, ln_a_b=m_ln_a_b,
                ln_v_g=m_ln_v_g, ln_v_b=m_ln_v_b, sp_w=m_sp_w, sp_b=m_sp_b, w_out_ab=m_w_out_ab, w_qkv=m_w_qkv,
                w_o=m_w_o, g_ffn2=m_g_ffn2, w_ffn2_gate=m_w_ffn2_gate, w_ffn2_up=m_w_ffn2_up,
                w_ffn2_down=m_w_ffn2_down, g_final=m_g_final)
    v_in = dict(g_ffn1=v_g_ffn1, w_ffn1_gate=v_w_ffn1_gate, w_ffn1_up=v_w_ffn1_up, w_ffn1_down=v_w_ffn1_down,
                g_mix=v_g_mix, w_in_ab=v_w_in_ab, conv_w=v_conv_w, conv_b=v_conv_b, ln_a_g=v_ln_a_g, ln_a_b=v_ln_a_b,
                ln_v_g=v_ln_v_g, ln_v_b=v_ln_v_b, sp_w=v_sp_w, sp_b=v_sp_b, w_out_ab=v_w_out_ab, w_qkv=v_w_qkv,
                w_o=v_w_o, g_ffn2=v_g_ffn2, w_ffn2_gate=v_w_ffn2_gate, w_ffn2_up=v_w_ffn2_up,
                w_ffn2_down=v_w_ffn2_down, g_final=v_g_final)
    names = list(weights)
    grads = {n: grads[n].reshape(weights[n].shape) for n in grads}

    delta, new_m, new_v = {}, {}, {}

    def adamw_big(n, parts):
        if weights[n].shape[-1] == D:
            view = back = lambda a: a
        else:
            view = back = lambda a: jnp.swapaxes(a, 1, 2)
        out = _adamw_parts(parts, view(weights[n]), view(m_in[n]), view(v_in[n]), f"adamw_{n}")
        grads[n], delta[n], new_m[n], new_v[n] = [back(a) for a in out]

    adamw_big("w_in_ab", [parts_in])
    adamw_big("w_out_ab", [parts_out])
    adamw_big("w_qkv", [parts_qkv])
    adamw_big("w_o", [parts_o])
    kinds = ("gate", "up", "down")
    for which, kind in enumerate(kinds):
        adamw_big(f"w_ffn2_{kind}", [parts_ffn[_ffn_index(l, 1) + which] for l in range(depth)])
    big = [n for n in names if n.startswith("w_")]
    after = jnp.concatenate([delta[n].reshape(-1)[:1] for n in big if n in delta]).reshape(1, -1)
    for which in (2, 1, 0):
        parts_ffn[which], = landed(f"ffn0_{which}", after)
    for which, kind in enumerate(kinds):
        adamw_big(f"w_ffn1_{kind}", [parts_ffn[_ffn_index(l, 0) + which] for l in range(depth)])
    little = [n for n in names if n not in big]
    shapes = [weights[n].shape for n in little]
    d, nm, nv = _adamw(_pack([weights[n] for n in little]), _pack([grads[n] for n in little]),
                       _pack([m_in[n] for n in little]), _pack([v_in[n] for n in little]), "adamw_small")
    for n, dd, mm, vv in zip(little, _unpack(d, shapes), _unpack(nm, shapes), _unpack(nv, shapes)):
        delta[n], new_m[n], new_v[n] = dd, mm, vv

    return (loss, grad_x, *[grads[n] for n in names], *[delta[n] for n in names],
            *[new_m[n] for n in names], *[new_v[n] for n in names])
```

```python
import functools

import jax
import jax.numpy as jnp
from jax import lax
from jax.experimental import pallas as pl
from jax.experimental.pallas import tpu as pltpu

F32 = jnp.float32
BF16 = jnp.bfloat16

D_MODEL = 1024
CA = 512
CB = 512
GB = 4
DB = 128
CHUNK = 128
CONV_WIDTH = 31
N_HEADS = 16
HEAD_DIM = 64
EPS = 1e-6
N_DEV = 8
LANES = 128
SUBLANES = 8
QB = 128
ATT_TQ = 1024
FFN_TN = 2816
FFN_TM = 256
HALO = 32
CONV_ROWS = 32
ATT_SCALE = HEAD_DIM ** -0.5

ADAM_LR = 0.001
ADAM_B1 = 0.9
ADAM_B2 = 0.999
ADAM_EPS = 1e-08
ADAM_WD = 0.01
ADAM_STEP = 10

NT = (((1,), (1,)), ((), ()))
NN = (((1,), (0,)), ((), ()))
TN = (((0,), (0,)), ((), ()))
MESH = pl.DeviceIdType.MESH
ANY = pl.BlockSpec(memory_space=pl.ANY)
VMEM_LIMIT = 60 * 1024 * 1024


def _dot(a, b, dims):
    return lax.dot_general(a, b, dims, preferred_element_type=F32)


def _cp(*sem):
    return pltpu.CompilerParams(dimension_semantics=sem, vmem_limit_bytes=VMEM_LIMIT)


def _pcall(body, *, in_specs, args, dep=None, **kw):
    if dep is not None:
        n_in = len(in_specs)
        inner = body

        def body(*refs):
            inner(*refs[:n_in], *refs[n_in + 1:])

        in_specs = list(in_specs) + [ANY]
        args = tuple(args) + (dep,)
    return pl.pallas_call(body, in_specs=list(in_specs), **kw)(*args)


def _tile(n, want):
    if n <= want:
        return n
    t = want - want % LANES
    while t > LANES and n % t:
        t -= LANES
    assert n % t == 0, (n, want)
    return t


def _sigmoid(x):
    return 0.5 * jnp.tanh(0.5 * x) + 0.5


def _rstd(x):
    return lax.rsqrt(jnp.mean(x * x, axis=-1, keepdims=True) + EPS)


def _rms_bwd(x, g, dh):
    r = _rstd(x)
    u = dh * g
    dx = r * (u - x * (r * r) * jnp.mean(u * x, axis=-1, keepdims=True))
    dg = jnp.sum(dh * x * r, axis=0, keepdims=True)
    return dx, dg


def _ln_fwd(x, g, b):
    mu = jnp.mean(x, axis=-1, keepdims=True)
    xc = x - mu
    r = lax.rsqrt(jnp.mean(xc * xc, axis=-1, keepdims=True) + EPS)
    xh = xc * r
    return xh * g + b, xh, r


def _ln_bwd(dy, xh, r, g):
    dxh = dy * g
    return r * (dxh - jnp.mean(dxh, axis=-1, keepdims=True)
                - xh * jnp.mean(dxh * xh, axis=-1, keepdims=True))


def _ffn_fwd(x, g, wall, base, name, dep=None):
    T, D = x.shape
    F = wall.shape[1]
    tm, tn = _tile(T, FFN_TM), _tile(F, FFN_TN)
    n_j = F // tn

    def body(x_ref, g_ref, wg_ref, wu_ref, wd_ref, xo_ref, a_ref, b_ref, h_ref, acc_ref):
        j = pl.program_id(1)

        @pl.when(j == 0)
        def _():
            xv = x_ref[...]
            h_ref[...] = (xv * _rstd(xv) * g_ref[...]).astype(BF16)
            acc_ref[...] = jnp.zeros_like(acc_ref)

        h = h_ref[...]
        a = _dot(h, wg_ref[...], NT)
        b = _dot(h, wu_ref[...], NT)
        a_ref[...] = a.astype(BF16)
        b_ref[...] = b.astype(BF16)
        s = (a * _sigmoid(a) * b).astype(BF16)
        acc_ref[...] += _dot(s, wd_ref[...], NN)

        @pl.when(j == n_j - 1)
        def _():
            xo_ref[...] = x_ref[...] + 0.5 * acc_ref[...]

    single = pl.Buffered(1) if n_j == 1 else None
    wspec = lambda k: pl.BlockSpec((None, tn, D), lambda i, j: (base + k, j, 0), pipeline_mode=single)
    return _pcall(
        body, name=name, grid=(T // tm, n_j), dep=dep, args=(x, g, wall, wall, wall),
        in_specs=[pl.BlockSpec((tm, D), lambda i, j: (i, 0)), pl.BlockSpec((1, D), lambda i, j: (0, 0)),
                  wspec(0), wspec(1), wspec(2)],
        out_specs=[pl.BlockSpec((tm, D), lambda i, j: (i, 0)), pl.BlockSpec((tm, tn), lambda i, j: (i, j)),
                   pl.BlockSpec((tm, tn), lambda i, j: (i, j)), pl.BlockSpec((tm, D), lambda i, j: (i, 0))],
        out_shape=[jax.ShapeDtypeStruct((T, D), F32), jax.ShapeDtypeStruct((T, F), BF16),
                   jax.ShapeDtypeStruct((T, F), BF16), jax.ShapeDtypeStruct((T, D), BF16)],
        scratch_shapes=[pltpu.VMEM((tm, D), F32)],
        compiler_params=_cp("parallel", "arbitrary"),
    )


def _ffn_bwd(go, x, g, a, b, wall, base, name, dep=None):
    T, D = x.shape
    F = wall.shape[1]
    tm, tn = _tile(T, FFN_TM), _tile(F, FFN_TN)
    n_j = F // tn

    def body(go_ref, x_ref, g_ref, a_ref, b_ref, wg_ref, wu_ref, wd_ref,
             gx_ref, dg_ref, da_ref, db_ref, s_ref, gh_ref, acc_ref):
        i, j = pl.program_id(0), pl.program_id(1)

        @pl.when(j == 0)
        def _():
            gh_ref[...] = (0.5 * go_ref[...]).astype(BF16)
            acc_ref[...] = jnp.zeros_like(acc_ref)

        @pl.when((i == 0) & (j == 0))
        def _():
            dg_ref[...] = jnp.zeros_like(dg_ref)

        ds = _dot(gh_ref[...], wd_ref[...], NT)
        av = a_ref[...].astype(F32)
        bv = b_ref[...].astype(F32)
        sig = _sigmoid(av)
        sl = av * sig
        dab = ((ds * bv) * (sig + sl * (1.0 - sig))).astype(BF16)
        dbb = (ds * sl).astype(BF16)
        s_ref[...] = (sl * bv).astype(BF16)
        da_ref[...] = dab
        db_ref[...] = dbb
        acc_ref[...] += _dot(dab, wg_ref[...], NN) + _dot(dbb, wu_ref[...], NN)

        @pl.when(j == n_j - 1)
        def _():
            dx, dg = _rms_bwd(x_ref[...], g_ref[...], acc_ref[...])
            gx_ref[...] = go_ref[...] + dx
            dg_ref[...] += dg

    single = pl.Buffered(1) if n_j == 1 else None
    wspec = lambda k: pl.BlockSpec((None, tn, D), lambda i, j: (base + k, j, 0), pipeline_mode=single)
    row = pl.BlockSpec((tm, D), lambda i, j: (i, 0))
    hid = pl.BlockSpec((tm, tn), lambda i, j: (i, j))
    vec = pl.BlockSpec((1, D), lambda i, j: (0, 0))
    return _pcall(
        body, name=name, grid=(T // tm, n_j), dep=dep, args=(go, x, g, a, b, wall, wall, wall),
        in_specs=[row, row, vec, hid, hid, wspec(0), wspec(1), wspec(2)],
        out_specs=[row, vec, hid, hid, hid, row],
        out_shape=[jax.ShapeDtypeStruct((T, D), F32), jax.ShapeDtypeStruct((1, D), F32),
                   jax.ShapeDtypeStruct((T, F), BF16), jax.ShapeDtypeStruct((T, F), BF16),
                   jax.ShapeDtypeStruct((T, F), BF16), jax.ShapeDtypeStruct((T, D), BF16)],
        scratch_shapes=[pltpu.VMEM((tm, D), F32)],
        compiler_params=_cp("arbitrary", "arbitrary"),
    )


def _mm_tn(a, b, name, dep=None, out_dtype=F32):
    T, M = a.shape
    N = b.shape[1]
    tmm, tk = _tile(M, 1536), _tile(T, 2048)
    n_k = T // tk
    narrow = out_dtype != F32

    def body(a_ref, b_ref, o_ref, *scratch):
        acc_ref = scratch[0] if narrow else o_ref

        @pl.when(pl.program_id(1) == 0)
        def _():
            acc_ref[...] = jnp.zeros_like(acc_ref)

        acc_ref[...] += _dot(a_ref[...].astype(BF16), b_ref[...].astype(BF16), TN)
        if narrow:
            @pl.when(pl.program_id(1) == n_k - 1)
            def _():
                o_ref[...] = acc_ref[...].astype(out_dtype)

    return _pcall(
        body, name=name, grid=(M // tmm, n_k), dep=dep, args=(a, b),
        in_specs=[pl.BlockSpec((tk, tmm), lambda m, k: (k, m)), pl.BlockSpec((tk, N), lambda m, k: (k, 0))],
        out_specs=pl.BlockSpec((tmm, N), lambda m, k: (m, 0)),
        out_shape=jax.ShapeDtypeStruct((M, N), out_dtype),
        scratch_shapes=[pltpu.VMEM((tmm, N), F32)] if narrow else [],
        compiler_params=_cp("parallel", "arbitrary"),
    )


def _mm_nt(x, wt, g, out_dtype, name, dep=None):
    T, K = x.shape
    N = wt.shape[0]
    tm, tn = _tile(T, 512), N
    norm = g is not None

    def body(*refs):
        if norm:
            x_ref, g_ref, w_ref, o_ref, h_ref = refs
        else:
            x_ref, w_ref, o_ref, h_ref = refs

        @pl.when(pl.program_id(1) == 0)
        def _():
            xv = x_ref[...].astype(F32)
            if norm:
                xv = xv * _rstd(xv) * g_ref[...]
            h_ref[...] = xv.astype(BF16)

        o_ref[...] = _dot(h_ref[...], w_ref[...], NT).astype(out_dtype)

    row = pl.BlockSpec((tm, K), lambda i, j: (i, 0))
    wsp = pl.BlockSpec((tn, K), lambda i, j: (j, 0))
    osp = pl.BlockSpec((tm, tn), lambda i, j: (i, j))
    if norm:
        return pl.pallas_call(
            body, name=name, grid=(T // tm, N // tn),
            in_specs=[row, pl.BlockSpec((1, K), lambda i, j: (0, 0)), wsp],
            out_specs=[osp, row],
            out_shape=[jax.ShapeDtypeStruct((T, N), out_dtype), jax.ShapeDtypeStruct((T, K), BF16)],
            compiler_params=_cp("parallel", "arbitrary"),
        )(x, g, wt)
    return _pcall(
        body, name=name, grid=(T // tm, N // tn), dep=dep, args=(x, wt),
        in_specs=[row, wsp], out_specs=osp,
        out_shape=jax.ShapeDtypeStruct((T, N), out_dtype),
        scratch_shapes=[pltpu.VMEM((tm, K), BF16)],
        compiler_params=_cp("parallel", "arbitrary"),
    )


def _mm_nn_res(act, w, resid, name):
    T, K = act.shape
    D = w.shape[1]
    tm = _tile(T, 512)

    def body(a_ref, w_ref, r_ref, o_ref):
        o_ref[...] = r_ref[...] + _dot(a_ref[...].astype(BF16), w_ref[...], NN)

    return pl.pallas_call(
        body, name=name, grid=(T // tm,),
        in_specs=[pl.BlockSpec((tm, K), lambda i: (i, 0)), pl.BlockSpec((K, D), lambda i: (0, 0)),
                  pl.BlockSpec((tm, D), lambda i: (i, 0))],
        out_specs=pl.BlockSpec((tm, D), lambda i: (i, 0)),
        out_shape=jax.ShapeDtypeStruct((T, D), F32),
        compiler_params=_cp("parallel"),
    )(act, w, resid)


def _mm_nn_rmsbwd(acts, w, x, g, gprev, name, dep=None):
    T = acts[0].shape[0]
    ks = [a.shape[1] for a in acts]
    K, D = w.shape
    assert sum(ks) == K
    tm = _tile(T, 512)
    na = len(acts)

    def body(*refs):
        a_refs = refs[:na]
        w_ref, x_ref, g_ref, gp_ref, o_ref, dg_ref = refs[na:]

        @pl.when(pl.program_id(0) == 0)
        def _():
            dg_ref[...] = jnp.zeros_like(dg_ref)

        dh, off = None, 0
        for a_ref, k in zip(a_refs, ks):
            part = _dot(a_ref[...].astype(BF16), w_ref[off:off + k, :], NN)
            dh = part if dh is None else dh + part
            off += k
        dx, dg = _rms_bwd(x_ref[...], g_ref[...], dh)
        o_ref[...] = gp_ref[...] + dx
        dg_ref[...] += dg

    row = pl.BlockSpec((tm, D), lambda i: (i, 0))
    vec = pl.BlockSpec((1, D), lambda i: (0, 0))
    return _pcall(
        body, name=name, grid=(T // tm,), dep=dep, args=(*acts, w, x, g, gprev),
        in_specs=[pl.BlockSpec((tm, k), lambda i: (i, 0)) for k in ks]
        + [pl.BlockSpec((K, D), lambda i: (0, 0)), row, vec, row],
        out_specs=[row, vec],
        out_shape=[jax.ShapeDtypeStruct((T, D), F32), jax.ShapeDtypeStruct((1, D), F32)],
        compiler_params=_cp("arbitrary"),
    )


def _loss_head(x, g, target):
    T, D = x.shape
    tm = _tile(T, 512)

    def body(x_ref, g_ref, t_ref, dx_ref, loss_ref, dg_ref):
        @pl.when(pl.program_id(0) == 0)
        def _():
            loss_ref[...] = jnp.zeros_like(loss_ref)
            dg_ref[...] = jnp.zeros_like(dg_ref)

        xv = x_ref[...]
        gv = g_ref[...]
        e = xv * _rstd(xv) * gv - t_ref[...]
        per_tok = jnp.sum(e * e, axis=-1, keepdims=True) * (1.0 / D)
        loss_ref[...] += 0.5 * jnp.sum(per_tok, axis=0, keepdims=True)
        dx, dg = _rms_bwd(xv, gv, e * (1.0 / D))
        dx_ref[...] = dx
        dg_ref[...] += dg

    row = pl.BlockSpec((tm, D), lambda i: (i, 0))
    vec = pl.BlockSpec((1, D), lambda i: (0, 0))
    return pl.pallas_call(
        body, name="loss_head", grid=(T // tm,),
        in_specs=[row, vec, row],
        out_specs=[row, pl.BlockSpec((1, LANES), lambda i: (0, 0)), vec],
        out_shape=[jax.ShapeDtypeStruct((T, D), F32), jax.ShapeDtypeStruct((1, LANES), F32),
                   jax.ShapeDtypeStruct((1, D), F32)],
        compiler_params=_cp("arbitrary"),
    )(x, g, target)


def _log_gates(z):
    neg_abs = lax.bitcast_convert_type(lax.bitcast_convert_type(z, jnp.uint32) | jnp.uint32(0x80000000), F32)
    ls = jnp.minimum(z, 0.0) - jnp.log(1.0 + jnp.exp(neg_abs))
    return ls, ls - z


def _cumsum_mm(v, u2):
    hi = v.astype(BF16)
    lo = (v - hi.astype(F32)).astype(BF16)
    return _dot(jnp.concatenate([hi, lo], axis=1), u2, NN)


def _half_rowsum(v):
    n = v.shape[0]
    s0 = jnp.sum(v[:, :QB], axis=1, keepdims=True)
    s1 = jnp.sum(v[:, QB:], axis=1, keepdims=True)
    return jnp.concatenate([jnp.broadcast_to(s0, (n, QB)), jnp.broadcast_to(s1, (n, QB))], axis=1)


def _stack_heads(src_ref, dst_ref, n_blk):
    m0 = lax.broadcasted_iota(jnp.int32, (1, LANES), 1) < HEAD_DIM

    def fill(c, carry):
        blk = src_ref[pl.ds(pl.multiple_of(c * QB, QB), QB), :]
        zero = jnp.zeros_like(blk)
        dst_ref[c, 0:QB, :] = jnp.where(m0, blk, zero)
        dst_ref[c, QB:2 * QB, :] = jnp.where(m0, zero, blk)
        return carry

    lax.fori_loop(0, n_blk, fill, 0)


def _diag_mask(tq, j):
    n = tq - j * QB
    row = lax.broadcasted_iota(jnp.int32, (n, 2 * QB), 0)
    col = lax.broadcasted_iota(jnp.int32, (n, 2 * QB), 1)
    return (col & (QB - 1)) < row


def _tri_blockdiag(upper):
    r = lax.broadcasted_iota(jnp.int32, (2 * QB, 2 * QB), 0)
    c = lax.broadcasted_iota(jnp.int32, (2 * QB, 2 * QB), 1)
    same = (r // QB) == (c // QB)
    u = (same & ((r > c) if upper else (r < c))).astype(BF16)
    return jnp.concatenate([u, u], axis=0)


def _attn_tiles(T, n_seq):
    S = T // n_seq
    tq = ATT_TQ if S % ATT_TQ == 0 else 2 * QB
    assert S % tq == 0
    return S, tq, tq // QB, S // tq, S // QB


def _attn_fwd(qkv, n_seq):
    T = qkv.shape[0]
    S, tq, r, n_q, n_k = _attn_tiles(T, n_seq)
    n_p = D_MODEL // LANES
    n_steps = n_seq * n_p * n_q
    u_suffix = _tri_blockdiag(True)

    def body(q_ref, k_ref, v_ref, u_ref, o_ref, a_hbm, kk_ref, vv_ref, lr_s, acc_s, a_stage, sems):
        qi = pl.program_id(2)
        group = (pl.program_id(0) * n_p + pl.program_id(1)) * n_q + qi

        @pl.when(qi == 0)
        def _():
            _stack_heads(k_ref, kk_ref, n_k)
            _stack_heads(v_ref, vv_ref, n_k)

        u = u_ref[...]
        lr_s[...] = jnp.zeros_like(lr_s)
        acc_s[...] = jnp.zeros_like(acc_s)
        base = ((group // n_q) * (n_q * (n_q + 1) // 2) + (qi * (qi + 1)) // 2) % 2

        def saves(first_kj, half):
            return [pltpu.make_async_copy(a_stage.at[half, j], a_hbm.at[group, first_kj - j], sems.at[half])
                    for j in range(r)]

        @pl.when(group >= 2)
        def _():
            for cp in saves(0, base):
                cp.wait()

        a_stage[base] = jnp.zeros_like(a_stage[0])

        def step(kj, half, j, rows, q, mask, lr, acc):
            ls, lk = _log_gates(_dot(q, kk_ref[kj], NT))
            if mask is not None:
                lk = jnp.where(mask, lk, 0.0)
            a = jnp.exp(ls + _cumsum_mm(lk, u) + lr)
            if mask is not None:
                a = jnp.where(mask, a, 0.0)
            a = a.astype(BF16)
            a_stage[half, j, rows, :] = a
            return lr + _half_rowsum(lk), acc + _dot(a, vv_ref[kj], NN)

        last = (qi + 1) * r - 1
        for n in range(r):
            rows = slice((r - 1 - n) * QB, tq)
            lr, acc = step(last - n, base, n, rows, q_ref[rows, :] * ATT_SCALE, _diag_mask(tq, r - 1 - n),
                           lr_s[rows, :], acc_s[rows, :])
            lr_s[rows, :] = lr
            acc_s[rows, :] = acc
        for cp in saves(last, base):
            cp.start()

        q = q_ref[...] * ATT_SCALE

        def off(it, carry):
            half = (base + it + 1) % 2
            first = (qi - it) * r - 1
            for cp in saves(first, half):
                cp.wait()
            lr, acc = lr_s[...], acc_s[...]
            for j in range(r):
                lr, acc = step(first - j, half, j, slice(0, tq), q, None, lr, acc)
            lr_s[...] = lr
            acc_s[...] = acc
            for cp in saves(first, half):
                cp.start()
            return carry

        lax.fori_loop(0, qi, off, 0)

        @pl.when(group == n_steps - 1)
        def _():
            for cp in saves(0, (base + qi) % 2):
                cp.wait()
            if n_steps * n_q > 1:
                for cp in saves(0, (base + qi + 1) % 2):
                    cp.wait()

        o_ref[...] = acc_s[...].astype(BF16)

    return pl.pallas_call(
        body, name="attn_fwd", grid=(n_seq, n_p, n_q),
        in_specs=[pl.BlockSpec((tq, LANES), lambda b, p, qi: (b * n_q + qi, p)),
                  pl.BlockSpec((S, LANES), lambda b, p, qi: (b, n_p + p)),
                  pl.BlockSpec((S, LANES), lambda b, p, qi: (b, 2 * n_p + p)),
                  pl.BlockSpec((4 * QB, 2 * QB), lambda b, p, qi: (0, 0))],
        out_specs=[pl.BlockSpec((tq, LANES), lambda b, p, qi: (b * n_q + qi, p)), ANY],
        out_shape=[jax.ShapeDtypeStruct((T, D_MODEL), BF16),
                   jax.ShapeDtypeStruct((n_seq * n_p * n_q, n_k, tq, 2 * QB), BF16)],
        scratch_shapes=[pltpu.VMEM((n_k, 2 * QB, LANES), BF16), pltpu.VMEM((n_k, 2 * QB, LANES), BF16),
                        pltpu.VMEM((tq, 2 * QB), F32), pltpu.VMEM((tq, LANES), F32),
                        pltpu.VMEM((2, r, tq, 2 * QB), BF16), pltpu.SemaphoreType.DMA((2,))],
        compiler_params=_cp("arbitrary", "arbitrary", "arbitrary"),
    )(qkv, qkv, qkv, u_suffix)


def _attn_bwd(qkv, do, a_saved, n_seq):
    T = qkv.shape[0]
    S, tq, r, n_q, n_k = _attn_tiles(T, n_seq)
    n_p = D_MODEL // LANES
    n_steps = n_seq * n_p * n_q
    u_prefix = _tri_blockdiag(False)[:2 * QB]

    def body(q_ref, k_ref, v_ref, do_ref, u_ref, a_hbm, dq_ref, dk_out, dv_out,
             kk_ref, vv_ref, cg_s, dq_s, dk_ref, dv_ref, a_stage, sems):
        qi = pl.program_id(2)
        group = (pl.program_id(0) * n_p + pl.program_id(1)) * n_q + qi

        base = ((group // n_q) * (n_q * (n_q + 1) // 2) + (qi * (qi + 1)) // 2) % 2

        def fetches(grp, g, half):
            return [pltpu.make_async_copy(a_hbm.at[grp, g * r + j], a_stage.at[half, j], sems.at[half])
                    for j in range(r)]

        @pl.when(group == 0)
        def _():
            for cp in fetches(group, 0, 0):
                cp.start()

        @pl.when(qi == 0)
        def _():
            _stack_heads(k_ref, kk_ref, n_k)
            _stack_heads(v_ref, vv_ref, n_k)
            dk_ref[...] = jnp.zeros_like(dk_ref)
            dv_ref[...] = jnp.zeros_like(dv_ref)

        u = u_ref[...]
        m0 = lax.broadcasted_iota(jnp.int32, (1, LANES), 1) < HEAD_DIM
        cg_s[...] = jnp.zeros_like(cg_s)
        dq_s[...] = jnp.zeros_like(dq_s)

        def step(kj, half, j, rows, q, dov, mask, cg, dq):
            kk = kk_ref[kj]
            beta = _sigmoid(_dot(q, kk, NT))
            a = a_stage[half, j, rows, :]
            g = a.astype(F32) * _dot(dov, vv_ref[kj], NT)
            dz = g - (g + _dot(g.astype(BF16), u, NN) + cg) * beta
            if mask is not None:
                dz = jnp.where(mask, dz, 0.0)
            dz = dz.astype(BF16)
            keys = pl.ds(pl.multiple_of(kj * QB, QB), QB)
            dvt = _dot(a, dov, TN)
            dv_ref[keys, :] += jnp.where(m0, dvt[:QB], dvt[QB:])
            dkt = _dot(dz, q, TN)
            dk_ref[keys, :] += jnp.where(m0, dkt[:QB], dkt[QB:])
            return cg + _half_rowsum(g), dq + _dot(dz, kk, NN)

        q = q_ref[...] * ATT_SCALE
        dov = do_ref[...]

        def off(it, carry):
            half = (base + it) % 2
            for cp in fetches(group, it, half):
                cp.wait()
            for cp in fetches(group, it + 1, 1 - half):
                cp.start()
            cg, dq = cg_s[...], dq_s[...]
            for j in range(r):
                cg, dq = step(it * r + j, half, j, slice(0, tq), q, dov, None, cg, dq)
            cg_s[...] = cg
            dq_s[...] = dq
            return carry

        lax.fori_loop(0, qi, off, 0)

        half = (base + qi) % 2
        for cp in fetches(group, qi, half):
            cp.wait()

        @pl.when(group < n_steps - 1)
        def _():
            for cp in fetches(group + 1, 0, 1 - half):
                cp.start()

        for j in range(r):
            rows = slice(j * QB, tq)
            cg, dq = step(qi * r + j, half, j, rows, q_ref[rows, :] * ATT_SCALE, do_ref[rows, :],
                          _diag_mask(tq, j), cg_s[rows, :], dq_s[rows, :])
            cg_s[rows, :] = cg
            dq_s[rows, :] = dq
        dq_ref[...] = (dq_s[...] * ATT_SCALE).astype(BF16)

        @pl.when(qi == n_q - 1)
        def _():
            dk_out[...] = dk_ref[...].astype(BF16)
            dv_out[...] = dv_ref[...].astype(BF16)

    qspec = pl.BlockSpec((tq, LANES), lambda b, p, qi: (b * n_q + qi, p))
    seq = lambda off: pl.BlockSpec((S, LANES), lambda b, p, qi: (b, off + p))
    return pl.pallas_call(
        body, name="attn_bwd", grid=(n_seq, n_p, n_q),
        in_specs=[qspec, seq(n_p), seq(2 * n_p), qspec,
                  pl.BlockSpec((2 * QB, 2 * QB), lambda b, p, qi: (0, 0)), ANY],
        out_specs=[qspec, seq(0), seq(0)],
        out_shape=[jax.ShapeDtypeStruct((T, D_MODEL), BF16)] * 3,
        scratch_shapes=[pltpu.VMEM((n_k, 2 * QB, LANES), BF16), pltpu.VMEM((n_k, 2 * QB, LANES), BF16),
                        pltpu.VMEM((tq, 2 * QB), F32), pltpu.VMEM((tq, LANES), F32),
                        pltpu.VMEM((S, LANES), F32), pltpu.VMEM((S, LANES), F32),
                        pltpu.VMEM((2, r, tq, 2 * QB), BF16), pltpu.SemaphoreType.DMA((2,))],
        compiler_params=_cp("arbitrary", "arbitrary", "arbitrary"),
    )(qkv, qkv, qkv, do, u_prefix, a_saved)


def _shifted_copies(sh_ref):
    rows = sh_ref.shape[1] - SUBLANES
    for s in range(1, SUBLANES):
        sh_ref[s, 0:rows, :] = sh_ref[0, s:s + rows, :]


def _shifted(sh_ref, start, n):
    s = start % SUBLANES
    return sh_ref[s, start - s:start - s + n, :]


def _glu_with_halo(av_ref, ag_ref, avh_ref, agh_ref, a0_s, first, ts):
    hal = avh_ref[...] * _sigmoid(agh_ref[...])
    a0_s[0, 0:HALO, :] = jnp.where(first, 0.0, hal)
    a0_s[0, HALO:HALO + ts, :] = av_ref[...] * _sigmoid(ag_ref[...])
    _shifted_copies(a0_s)


def _mix_specs(ts, n_r, with_left):
    blk = lambda c: pl.BlockSpec((ts, CA), lambda b, r: (b * n_r + r, c))
    per = ts // HALO
    left = lambda c: pl.BlockSpec((HALO, CA), lambda b, r: (jnp.maximum((b * n_r + r) * per - 1, 0), c))
    return blk, (left if with_left else None)


def _mix_fwd(z, conv_w, conv_b, ln_a_g, ln_a_b, ln_v_g, ln_v_b, ws, bias2d, n_seq):
    T = z.shape[0]
    S = T // n_seq
    ts = _tile(S, 512)
    n_r = S // ts
    shift = HALO - (CONV_WIDTH - 1)

    def body(av_ref, ag_ref, avh_ref, agh_ref, u_ref, v_ref, cw_ref, cb_ref, lag_ref, lab_ref,
             lvg_ref, lvb_ref, ws_ref, bias_ref, cat_ref, a1_ref, a0_s):
        _glu_with_halo(av_ref, ag_ref, avh_ref, agh_ref, a0_s, pl.program_id(1) == 0, ts)
        for rb in range(ts // CONV_ROWS):
            base = rb * CONV_ROWS
            acc = jnp.broadcast_to(cb_ref[...], (CONV_ROWS, CA))
            for k in range(CONV_WIDTH):
                acc = acc + cw_ref[k:k + 1, :] * _shifted(a0_s, base + shift + k, CONV_ROWS)
            a1_ref[base:base + CONV_ROWS, :] = acc
        y, _, _ = _ln_fwd(a1_ref[...], lag_ref[...], lab_ref[...])
        cat_ref[:, 0:CA] = (y * _sigmoid(y)).astype(BF16)
        for gi in range(GB):
            sl = slice(gi * DB, (gi + 1) * DB)
            v1, _, _ = _ln_fwd(v_ref[:, sl], lvg_ref[:, sl], lvb_ref[:, sl])
            v1 = v1.astype(BF16)
            for c in range(ts // CHUNK):
                rs = slice(c * CHUNK, (c + 1) * CHUNK)
                v2 = _dot(ws_ref[gi], v1[rs], NN) + bias_ref[:, sl]
                cat_ref[rs, CA + gi * DB:CA + (gi + 1) * DB] = (u_ref[rs, sl] * v2).astype(BF16)

    blk, left = _mix_specs(ts, n_r, True)
    vec = pl.BlockSpec((1, CA), lambda b, r: (0, 0))
    return pl.pallas_call(
        body, name="mix_fwd", grid=(n_seq, n_r),
        in_specs=[blk(0), blk(1), left(0), left(1), blk(2), blk(3),
                  pl.BlockSpec((CONV_WIDTH, CA), lambda b, r: (0, 0)), vec, vec, vec, vec, vec,
                  pl.BlockSpec((GB, CHUNK, CHUNK), lambda b, r: (0, 0, 0)),
                  pl.BlockSpec((CHUNK, CB), lambda b, r: (0, 0))],
        out_specs=[pl.BlockSpec((ts, CA + CB), lambda b, r: (b * n_r + r, 0)), blk(0)],
        out_shape=[jax.ShapeDtypeStruct((T, CA + CB), BF16), jax.ShapeDtypeStruct((T, CA), F32)],
        scratch_shapes=[pltpu.VMEM((SUBLANES, HALO + ts, CA), F32)],
        compiler_params=_cp("parallel", "parallel"),
    )(z, z, z, z, z, z, conv_w, conv_b, ln_a_g, ln_a_b, ln_v_g, ln_v_b, ws, bias2d)


def _mix_bwd_rows(dcat, z, a1, ln_a_g, ln_a_b, ln_v_g, ln_v_b, ws, ws_t, bias2d, n_seq):
    T = z.shape[0]
    S = T // n_seq
    ts = _tile(S, 512)
    n_r = S // ts

    def body(dc_ref, u_ref, v_ref, a1_ref, lag_ref, lab_ref, lvg_ref, lvb_ref, ws_ref, wst_ref, bias_ref,
             da1_ref, dz_ref, dlag_ref, dlab_ref, dlvg_ref, dlvb_ref, dws_ref, dsb_ref, dv1_s, dbias_s):
        first = (pl.program_id(0) == 0) & (pl.program_id(1) == 0)
        last = (pl.program_id(0) == n_seq - 1) & (pl.program_id(1) == n_r - 1)

        @pl.when(first)
        def _():
            for ref in (dlag_ref, dlab_ref, dlvg_ref, dlvb_ref, dws_ref, dbias_s):
                ref[...] = jnp.zeros_like(ref)

        lag = lag_ref[...]
        y, xh, r = _ln_fwd(a1_ref[...], lag, lab_ref[...])
        sig = _sigmoid(y)
        dy = dc_ref[:, 0:CA] * (sig * (1.0 + y * (1.0 - sig)))
        dlag_ref[...] += jnp.sum(dy * xh, axis=0, keepdims=True)
        dlab_ref[...] += jnp.sum(dy, axis=0, keepdims=True)
        da1_ref[...] = _ln_bwd(dy, xh, r, lag)

        tril = (lax.broadcasted_iota(jnp.int32, (CHUNK, CHUNK), 0)
                >= lax.broadcasted_iota(jnp.int32, (CHUNK, CHUNK), 1))
        for gi in range(GB):
            sl = slice(gi * DB, (gi + 1) * DB)
            lvg = lvg_ref[:, sl]
            v1, vh, vr = _ln_fwd(v_ref[:, sl], lvg, lvb_ref[:, sl])
            v1 = v1.astype(BF16)
            for c in range(ts // CHUNK):
                rs = slice(c * CHUNK, (c + 1) * CHUNK)
                v2 = _dot(ws_ref[gi], v1[rs], NN) + bias_ref[:, sl]
                dbo = dc_ref[rs, CA + gi * DB:CA + (gi + 1) * DB]
                dz_ref[rs, sl] = (dbo * v2).astype(BF16)
                dv2 = dbo * u_ref[rs, sl]
                dbias_s[:, sl] += dv2
                dv2b = dv2.astype(BF16)
                dws_ref[gi] += jnp.where(tril, _dot(dv2b, v1[rs], NT), 0.0)
                dv1_s[rs, :] = _dot(wst_ref[gi], dv2b, NN)
            dv1 = dv1_s[...]
            dlvg_ref[:, sl] += jnp.sum(dv1 * vh, axis=0, keepdims=True)
            dlvb_ref[:, sl] += jnp.sum(dv1, axis=0, keepdims=True)
            dz_ref[:, CB + gi * DB:CB + (gi + 1) * DB] = _ln_bwd(dv1, vh, vr, lvg).astype(BF16)

        @pl.when(last)
        def _():
            col = lax.broadcasted_iota(jnp.int32, (CHUNK, GB), 1)
            out = jnp.zeros((CHUNK, GB), F32)
            for gi in range(GB):
                s = jnp.sum(dbias_s[:, gi * DB:(gi + 1) * DB], axis=1, keepdims=True)
                out = out + jnp.where(col == gi, s, 0.0)
            dsb_ref[...] = out

    blk, _ = _mix_specs(ts, n_r, False)
    vec = pl.BlockSpec((1, CA), lambda b, r: (0, 0))
    mat = pl.BlockSpec((GB, CHUNK, CHUNK), lambda b, r: (0, 0, 0))
    wide = pl.BlockSpec((ts, CA + CB), lambda b, r: (b * n_r + r, 0))
    return pl.pallas_call(
        body, name="mix_bwd_rows", grid=(n_seq, n_r),
        in_specs=[wide, blk(2), blk(3), blk(0), vec, vec, vec, vec, mat, mat,
                  pl.BlockSpec((CHUNK, CB), lambda b, r: (0, 0))],
        out_specs=[blk(0), wide, vec, vec, vec, vec, mat, pl.BlockSpec((CHUNK, GB), lambda b, r: (0, 0))],
        out_shape=[jax.ShapeDtypeStruct((T, CA), F32), jax.ShapeDtypeStruct((T, 2 * CB), BF16)]
        + [jax.ShapeDtypeStruct((1, CA), F32)] * 4
        + [jax.ShapeDtypeStruct((GB, CHUNK, CHUNK), F32), jax.ShapeDtypeStruct((CHUNK, GB), F32)],
        scratch_shapes=[pltpu.VMEM((ts, DB), F32), pltpu.VMEM((CHUNK, CB), F32)],
        compiler_params=_cp("arbitrary", "arbitrary"),
    )(dcat, z, z, a1, ln_a_g, ln_a_b, ln_v_g, ln_v_b, ws, ws_t, bias2d)


def _mix_bwd_conv(da1, z, conv_w, n_seq):
    T = z.shape[0]
    S = T // n_seq
    ts = _tile(S, 512)
    n_r = S // ts
    per = ts // HALO
    shift = HALO - (CONV_WIDTH - 1)
    fold = CONV_ROWS // 8

    def body(d_ref, dh_ref, av_ref, ag_ref, avh_ref, agh_ref, cw_ref,
             dz_ref, dcw_ref, dcb_ref, a0_s, d1_s, da0_s, dw8_s):
        first = (pl.program_id(0) == 0) & (pl.program_id(1) == 0)
        last = (pl.program_id(0) == n_seq - 1) & (pl.program_id(1) == n_r - 1)

        @pl.when(first)
        def _():
            dw8_s[...] = jnp.zeros_like(dw8_s)
            dcb_ref[...] = jnp.zeros_like(dcb_ref)

        _glu_with_halo(av_ref, ag_ref, avh_ref, agh_ref, a0_s, pl.program_id(1) == 0, ts)
        d1_s[0, 0:ts, :] = d_ref[...]
        d1_s[0, ts:ts + HALO, :] = jnp.where(pl.program_id(1) == n_r - 1, 0.0, dh_ref[...])
        _shifted_copies(d1_s)
        dcb_ref[...] += jnp.sum(d_ref[...], axis=0, keepdims=True)
        for rb in range(ts // CONV_ROWS):
            base = rb * CONV_ROWS
            dcur = d1_s[0, base:base + CONV_ROWS, :]
            acc = jnp.zeros((CONV_ROWS, CA), F32)
            for k in range(CONV_WIDTH):
                back = CONV_WIDTH - 1 - k
                acc = acc + cw_ref[k:k + 1, :] * _shifted(d1_s, base + back, CONV_ROWS)
                prod = dcur * _shifted(a0_s, base + shift + k, CONV_ROWS)
                part = prod[0:8]
                for f in range(1, fold):
                    part = part + prod[8 * f:8 * f + 8]
                dw8_s[k] += part
            da0_s[base:base + CONV_ROWS, :] = acc
        da0 = da0_s[...]
        sig = _sigmoid(ag_ref[...])
        dz_ref[:, 0:CA] = (da0 * sig).astype(BF16)
        dz_ref[:, CA:2 * CA] = (da0 * av_ref[...] * sig * (1.0 - sig)).astype(BF16)

        @pl.when(last)
        def _():
            for k in range(CONV_WIDTH):
                dcw_ref[k:k + 1, :] = jnp.sum(dw8_s[k], axis=0, keepdims=True)

    blk, left = _mix_specs(ts, n_r, True)
    n_halo_blocks = T // HALO
    right = pl.BlockSpec((HALO, CA), lambda b, r: (jnp.minimum((b * n_r + r + 1) * per, n_halo_blocks - 1), 0))
    return pl.pallas_call(
        body, name="mix_bwd_conv", grid=(n_seq, n_r),
        in_specs=[blk(0), right, blk(0), blk(1), left(0), left(1),
                  pl.BlockSpec((CONV_WIDTH, CA), lambda b, r: (0, 0))],
        out_specs=[pl.BlockSpec((ts, 2 * CA), lambda b, r: (b * n_r + r, 0)),
                   pl.BlockSpec((CONV_WIDTH, CA), lambda b, r: (0, 0)), pl.BlockSpec((1, CA), lambda b, r: (0, 0))],
        out_shape=[jax.ShapeDtypeStruct((T, 2 * CA), BF16), jax.ShapeDtypeStruct((CONV_WIDTH, CA), F32),
                   jax.ShapeDtypeStruct((1, CA), F32)],
        scratch_shapes=[pltpu.VMEM((SUBLANES, HALO + ts, CA), F32), pltpu.VMEM((SUBLANES, ts + HALO, CA), F32),
                        pltpu.VMEM((ts, CA), F32), pltpu.VMEM((CONV_WIDTH, 8, CA), F32)],
        compiler_params=_cp("arbitrary", "arbitrary"),
    )(da1, da1, z, z, z, z, conv_w)


def _row_tile(R, want):
    t = min(R, want)
    t -= t % 8
    while t > 8 and R % t:
        t -= 8
    return t if t >= 8 and R % t == 0 else R


def _adam_step(w, g, m, v):
    nm = ADAM_B1 * m + (1.0 - ADAM_B1) * g
    nv = ADAM_B2 * v + (1.0 - ADAM_B2) * (g * g)
    m_hat = nm / (1.0 - ADAM_B1 ** ADAM_STEP)
    v_hat = nv / (1.0 - ADAM_B2 ** ADAM_STEP)
    return -ADAM_LR * (m_hat / (jnp.sqrt(v_hat) + ADAM_EPS) + ADAM_WD * w), nm, nv


def _adamw_parts(parts, w, m, v, name):
    L, n, C = w.shape
    assert len(parts) == L
    tr = _row_tile(n, 192)
    n_i = n // tr

    def body(*refs):
        p_refs = refs[:L]
        w_ref, m_ref, v_ref, g_ref, d_ref, nm_ref, nv_ref = refs[L:]
        for k in range(L):
            @pl.when(pl.program_id(0) == k)
            def _(k=k):
                acc = p_refs[k][0].astype(F32)
                for s in range(1, N_DEV):
                    acc = acc + p_refs[k][s].astype(F32)
                g_ref[...] = acc

        d_ref[...], nm_ref[...], nv_ref[...] = _adam_step(w_ref[...], g_ref[...], m_ref[...], v_ref[...])

    def part_spec(k):
        return pl.BlockSpec((N_DEV, tr, C),
                            lambda l, i: (0, jnp.where(l == k, i, jnp.where(l < k, 0, n_i - 1)), 0))

    blk = pl.BlockSpec((None, tr, C), lambda l, i: (l, i, 0))
    return pl.pallas_call(
        body, name=name, grid=(L, n_i),
        in_specs=[part_spec(k) for k in range(L)] + [blk] * 3, out_specs=[blk] * 4,
        out_shape=[jax.ShapeDtypeStruct((L, n, C), F32)] * 4,
        compiler_params=_cp("arbitrary", "arbitrary"),
    )(*parts, w, m, v)


def _adamw(w, g, m, v, name):
    R, C = w.shape
    tr = _row_tile(R, 256)

    def body(w_ref, g_ref, m_ref, v_ref, d_ref, nm_ref, nv_ref):
        d_ref[...], nm_ref[...], nv_ref[...] = _adam_step(w_ref[...], g_ref[...], m_ref[...], v_ref[...])

    blk = pl.BlockSpec((tr, C), lambda i: (i, 0))
    return pl.pallas_call(
        body, name=name, grid=(R // tr,),
        in_specs=[blk] * 4, out_specs=[blk] * 3,
        out_shape=[jax.ShapeDtypeStruct((R, C), F32)] * 3,
        compiler_params=_cp("parallel"),
    )(w, g, m, v)


def _me():
    return lax.axis_index("x"), lax.axis_index("y"), lax.axis_index("c")


def _block_rows(ref, dev, n):
    start = (4 * dev[0] + 2 * dev[1] + dev[2]) * n
    if len(ref.shape) == 2:
        return ref.at[pl.ds(start, n), :]
    return ref.at[:, pl.ds(start, n), :]


def _all_gather(shards):
    na = len(shards)
    ns = [s.shape[-2] for s in shards]

    def body(*refs):
        ins, outs = refs[:na], refs[na:2 * na]
        send_sems, recv_sems, local_sems = refs[2 * na:]
        x, y, c = _me()
        me, sibling = (x, y, c), (x, y, 1 - c)
        chips = [(1 - x, y), (x, 1 - y), (1 - x, 1 - y)]

        def copy(a, k, block, to, src=None):
            dst = _block_rows(outs[a], block, ns[a])
            return pltpu.make_async_remote_copy(
                src_ref=dst if src is None else src, dst_ref=dst,
                send_sem=send_sems.at[a, k], recv_sem=recv_sems.at[a, k], device_id=to, device_id_type=MESH)

        mine = [pltpu.make_async_copy(ins[a], _block_rows(outs[a], me, ns[a]), local_sems.at[a]) for a in range(na)]
        for cp in mine:
            cp.start()
        first = []
        for a in range(na):
            first.append(copy(a, 0, me, sibling, src=ins[a]))
            first += [copy(a, 1 + j, me, (*chip, c), src=ins[a]) for j, chip in enumerate(chips)]
        for cp in first:
            cp.start()
        passed = []
        for j, chip in enumerate(chips):
            for a in range(na):
                copy(a, 1 + j, (*chip, c), me).wait_recv()
                fwd = copy(a, 4 + j, (*chip, c), sibling)
                fwd.start()
                passed.append(fwd)
        for a in range(na):
            copy(a, 0, sibling, me).wait_recv()
            for j, chip in enumerate(chips):
                copy(a, 4 + j, (*chip, 1 - c), me).wait_recv()
        for cp in first + passed:
            cp.wait_send()
        for cp in mine:
            cp.wait()

    out_shape = [jax.ShapeDtypeStruct(s.shape[:-2] + (N_DEV * s.shape[-2], s.shape[-1]), s.dtype) for s in shards]
    return pl.pallas_call(
        body, name="weights_all_gather",
        in_specs=[ANY] * na, out_specs=[ANY] * na, out_shape=out_shape,
        scratch_shapes=[pltpu.SemaphoreType.DMA((na, 7)), pltpu.SemaphoreType.DMA((na, 7)),
                        pltpu.SemaphoreType.DMA((na,))],
    )(*shards)


def _split_copies(gather, srcs, lands, send_sems, recv_sems, ns):
    x, y, c = _me()
    me = (x, y, c)
    my_slot = 4 * x + 2 * y + c
    copies = []
    for mask in range(1, N_DEV):
        peer = (x ^ (mask >> 2), y ^ ((mask >> 1) & 1), c ^ (mask & 1))
        for a in range(len(srcs)):
            if gather:
                src, dst = srcs[a], _block_rows(lands[a], me, ns[a])
            else:
                src, dst = _block_rows(srcs[a], peer, ns[a]), lands[a].at[my_slot]
            sem = a * (N_DEV - 1) + mask - 1
            copies.append(pltpu.make_async_remote_copy(
                src_ref=src, dst_ref=dst, send_sem=send_sems.at[sem], recv_sem=recv_sems.at[sem],
                device_id=peer, device_id_type=MESH))
    return copies


HBM_SPEC = pl.BlockSpec(memory_space=pltpu.HBM)
SEM_SPEC = pl.BlockSpec(memory_space=pltpu.SEMAPHORE)


def _split_start(gather, srcs, name, dep=None):
    na = len(srcs)
    x, y, c = _me()
    mine = 4 * x + 2 * y + c
    if gather:
        ns = [s.shape[-2] for s in srcs]
        lands = [lax.dynamic_update_slice(
            lax.empty(s.shape[:-2] + (N_DEV * s.shape[-2], s.shape[-1]), s.dtype), s,
            (0,) * (s.ndim - 2) + (mine * s.shape[-2], 0)) for s in srcs]
    else:
        ns = [s.shape[-2] // N_DEV for s in srcs]
        lands = [lax.dynamic_update_slice(
            lax.empty((N_DEV, n, s.shape[-1]), s.dtype),
            lax.dynamic_slice(s, (mine * n, 0), (n, s.shape[-1]))[None], (mine, 0, 0)) for s, n in zip(srcs, ns)]
    n_in = 2 * na + (dep is not None)

    def body(*refs):
        send_sems, recv_sems = refs[n_in], refs[n_in + 1]
        for cp in _split_copies(gather, refs[:na], refs[na:2 * na], send_sems, recv_sems, ns):
            cp.start()
        refs[-1][...] = jnp.zeros_like(refs[-1])

    hbm = lambda a: pltpu.with_memory_space_constraint(a, pltpu.HBM)
    args = [hbm(a) for a in srcs] + [hbm(a) for a in lands] + ([dep] if dep is not None else [])
    out = pl.pallas_call(
        body, name=name,
        in_specs=[HBM_SPEC] * (2 * na) + ([ANY] if dep is not None else []),
        out_specs=[SEM_SPEC, SEM_SPEC] + [HBM_SPEC] * (2 * na) + [pl.BlockSpec(memory_space=pltpu.VMEM)],
        out_shape=[pltpu.SemaphoreType.DMA((na * (N_DEV - 1),)), pltpu.SemaphoreType.DMA((na * (N_DEV - 1),))]
        + [pltpu.HBM(a.shape, a.dtype) for a in srcs + lands] + [jax.ShapeDtypeStruct((8, LANES), F32)],
        input_output_aliases={i: 2 + i for i in range(2 * na)},
        compiler_params=pltpu.CompilerParams(has_side_effects=pltpu.SideEffectType.DATAFLOW_SIDE_EFFECTING),
    )(*args)
    return (gather, ns, out[0], out[1], list(out[2:2 + na]), list(out[2 + na:2 + 2 * na])), out[-1]


def _split_wait(handle, after, name):
    gather, ns, send, recv, srcs, lands = handle
    na = len(srcs)

    def body(*refs):
        send_sems, recv_sems = refs[2 * na], refs[2 * na + 1]
        for cp in _split_copies(gather, refs[:na], refs[na:2 * na], send_sems, recv_sems, ns):
            cp.wait_send()
            cp.wait_recv()

    out = pl.pallas_call(
        body, name=name,
        in_specs=[HBM_SPEC] * (2 * na) + [SEM_SPEC, SEM_SPEC, ANY],
        out_specs=[HBM_SPEC] * (2 * na),
        out_shape=[pltpu.HBM(a.shape, a.dtype) for a in srcs + lands],
        input_output_aliases={i: i for i in range(2 * na)},
        compiler_params=pltpu.CompilerParams(has_side_effects=pltpu.SideEffectType.DATAFLOW_SIDE_EFFECTING),
    )(*srcs, *lands, send, recv, after)
    return list(out[:na]), list(out[na:])


def _sum_blocks(parts):
    n, R, C = parts.shape

    def body(p_ref, o_ref):
        acc = p_ref[0]
        for k in range(1, n):
            acc = acc + p_ref[k]
        o_ref[...] = acc

    return pl.pallas_call(
        body, name="small_sum",
        in_specs=[pl.BlockSpec(memory_space=pltpu.VMEM)], out_specs=pl.BlockSpec(memory_space=pltpu.VMEM),
        out_shape=jax.ShapeDtypeStruct((R, C), F32),
        compiler_params=pltpu.CompilerParams(vmem_limit_bytes=VMEM_LIMIT),
    )(parts)


def _pack(arrays):
    flat = jnp.concatenate([a.reshape(-1) for a in arrays])
    pad = (-flat.shape[0]) % (8 * LANES)
    return jnp.pad(flat, (0, pad)).reshape(-1, LANES)


def _unpack(buf, shapes):
    flat = buf.reshape(-1)
    out, off = [], 0
    for s in shapes:
        n = 1
        for d in s:
            n *= d
        out.append(flat[off:off + n].reshape(s))
        off += n
    return out


def _ffn_index(layer, second):
    return (2 * layer + second) * 3


def kernel(x, g_ffn1, w_ffn1_gate, w_ffn1_up, w_ffn1_down, g_mix, w_in_ab, conv_w, conv_b, ln_a_g, ln_a_b, ln_v_g, ln_v_b, sp_w, sp_b, w_out_ab, w_qkv, w_o, g_ffn2, w_ffn2_gate, w_ffn2_up, w_ffn2_down, g_final, loss_target, m_g_ffn1, m_w_ffn1_gate, m_w_ffn1_up, m_w_ffn1_down, m_g_mix, m_w_in_ab, m_conv_w, m_conv_b, m_ln_a_g, m_ln_a_b, m_ln_v_g, m_ln_v_b, m_sp_w, m_sp_b, m_w_out_ab, m_w_qkv, m_w_o, m_g_ffn2, m_w_ffn2_gate, m_w_ffn2_up, m_w_ffn2_down, m_g_final, v_g_ffn1, v_w_ffn1_gate, v_w_ffn1_up, v_w_ffn1_down, v_g_mix, v_w_in_ab, v_conv_w, v_conv_b, v_ln_a_g, v_ln_a_b, v_ln_v_g, v_ln_v_b, v_sp_w, v_sp_b, v_w_out_ab, v_w_qkv, v_w_o, v_g_ffn2, v_w_ffn2_gate, v_w_ffn2_up, v_w_ffn2_down, v_g_final):
    n_seq, S, D = x.shape
    T = n_seq * S
    depth = g_ffn1.shape[0]
    assert depth == 2 and D == D_MODEL
    my_block = 4 * lax.axis_index("x") + 2 * lax.axis_index("y") + lax.axis_index("c")

    ffn_parts = []
    for l in range(depth):
        for gate, up, down in ((w_ffn1_gate, w_ffn1_up, w_ffn1_down), (w_ffn2_gate, w_ffn2_up, w_ffn2_down)):
            ffn_parts += [gate[l].T, up[l].T, down[l]]
    ffn_shard = lambda k: jnp.stack(ffn_parts[3 * k:3 * k + 3]).astype(BF16)
    conv_w_pad = jnp.zeros((HALO, conv_w.shape[2]), F32).at[:CONV_WIDTH].set(conv_w[0]).T
    w_ffn = [None] * (2 * depth)
    w_ffn[0], conv_w_t = _all_gather([ffn_shard(0), conv_w_pad])
    conv_w_full = conv_w_t.T[:CONV_WIDTH]
    shards_b = [w_out_ab[0].astype(BF16), ffn_shard(1)]
    shards_d = [w_qkv[0].T.astype(BF16), w_o[0].astype(BF16), ffn_shard(3)]
    gather_a, token = _split_start(True, [w_in_ab[0].T.astype(BF16)], "gather_a_start", dep=conv_w_t)
    gather_b, token = _split_start(True, shards_b, "gather_b_start", dep=token)
    gather_c, token = _split_start(True, [ffn_shard(2)], "gather_c_start", dep=token)
    gather_d, token = _split_start(True, shards_d, "gather_d_start", dep=token)

    def gathered(handle, after, name):
        return _split_wait(handle, after, name)[1]

    row = lambda a: a.reshape(1, -1)
    tril = jnp.tril(jnp.ones((CHUNK, CHUNK), dtype=bool))
    ws = jnp.where(tril[None], sp_w[0], 0.0).astype(BF16)
    ws_t = jnp.swapaxes(ws, 1, 2)
    bias2d = jnp.repeat(sp_b[0].T, DB, axis=1)
    conv_b2, lag, lab = row(conv_b[0]), row(ln_a_g[0]), row(ln_a_b[0])
    lvg, lvb = row(ln_v_g[0]), row(ln_v_b[0])

    x0 = x.reshape(T, D)
    target = loss_target.reshape(T, D)
    saved = []
    xc = x0
    for l in range(depth):
        xa, a1, b1, h1 = _ffn_fwd(xc, row(g_ffn1[l]), w_ffn[2 * l], 0, f"ffn1_fwd_{l}", dep=token)
        if l % 2 == 0:
            w_in_t, = gathered(gather_a, xa, "gather_a_wait")
            z, hm = _mm_nt(xa, w_in_t, row(g_mix[l]), F32, "mix_in_proj")
            cat, conv_out = _mix_fwd(z, conv_w_full, conv_b2, lag, lab, lvg, lvb, ws, bias2d, n_seq)
            w_out, w_ffn[1] = gathered(gather_b, cat, "gather_b_wait")
            xb = _mm_nn_res(cat, w_out, xa, "mix_out_proj")
            mixer = (z, hm, cat, conv_out)
        else:
            w_qkv_t, w_o_full, w_ffn[3] = gathered(gather_d, xa, "gather_d_wait")
            qkv, hm = _mm_nt(xa, w_qkv_t, row(g_mix[l]), BF16, "qkv_proj")
            o, att = _attn_fwd(qkv, n_seq)
            xb = _mm_nn_res(o, w_o_full, xa, "attn_out_proj")
            mixer = (qkv, hm, o, att)
        xn, a2, b2, h2 = _ffn_fwd(xb, row(g_ffn2[l]), w_ffn[2 * l + 1], 0, f"ffn2_fwd_{l}")
        saved.append((xc, a1, b1, h1, xa, mixer, xb, a2, b2, h2))
        xc = xn
        if l == 0:
            w_ffn[2], = gathered(gather_c, xc, "gather_c_wait")

    g, loss_part, dg_final = _loss_head(xc, row(g_final), target)

    dg_ffn1, dg_ffn2, dg_mix = [None] * depth, [None] * depth, [None] * depth
    exchanges = {}
    token = None

    def ffn_back(g, xin, gvec, a, b, h, k, tag, token):
        g, dg, da, db, s, gh = _ffn_bwd(g, xin, gvec, a, b, w_ffn[k], 0, f"ffn{tag}_bwd", dep=token)
        if k == 0:
            return g, dg, (da, db, s, gh, h)
        dws = [_mm_tn(da, h, f"dw_gate{tag}"), _mm_tn(db, h, f"dw_up{tag}"), _mm_tn(s, gh, f"dw_down{tag}")]
        exchanges[f"ffn{k}"], token = _split_start(False, dws, f"exchange_ffn{tag}_start")
        return g, dg, token

    for l in reversed(range(depth)):
        xin, a1, b1, h1, xa, mixer, xb, a2, b2, h2 = saved[l]
        g, dg_ffn2[l], token = ffn_back(g, xb, row(g_ffn2[l]), a2, b2, h2, 2 * l + 1, f"2_{l}", token)
        if l % 2 == 0:
            z, hm, cat, conv_out = mixer
            dcat = _mm_nt(g, w_out, None, F32, "mix_out_bwd", dep=token)
            d_w_out = _mm_tn(cat, g, "dw_out")
            (da1, dz_uv, d_lag, d_lab, d_lvg, d_lvb, d_ws, d_sb) = _mix_bwd_rows(
                dcat, z, conv_out, lag, lab, lvg, lvb, ws, ws_t, bias2d, n_seq)
            dz_a, d_cw, d_cb = _mix_bwd_conv(da1, z, conv_w_full, n_seq)
            d_w_in_t = jnp.concatenate([_mm_tn(dz_a, hm, "dw_in_a"), _mm_tn(dz_uv, hm, "dw_in_uv")])
            exchanges["mix"], token = _split_start(False, [d_w_out, d_w_in_t], "exchange_mix_start")
            g, dg_mix[l] = _mm_nn_rmsbwd([dz_a, dz_uv], w_in_t, xa, row(g_mix[l]), g, "mix_in_bwd", dep=token)
        else:
            qkv, hm, o, att = mixer
            do = _mm_nt(g, w_o_full, None, BF16, "attn_out_bwd", dep=token)
            d_w_o = _mm_tn(o, g, "dw_o")
            dq, dk, dv = _attn_bwd(qkv, do, att, n_seq)
            d_w_qkv_t = jnp.concatenate([_mm_tn(dq, hm, "dw_q"), _mm_tn(dk, hm, "dw_k"), _mm_tn(dv, hm, "dw_v")])
            exchanges["attn"], token = _split_start(False, [d_w_o, d_w_qkv_t], "exchange_attn_start")
            g, dg_mix[l] = _mm_nn_rmsbwd([dq, dk, dv], w_qkv_t, xa, row(g_mix[l]), g, "qkv_bwd", dep=token)
        g, dg_ffn1[l], token = ffn_back(g, xin, row(g_ffn1[l]), a1, b1, h1, 2 * l, f"1_{l}", token)
    grad_x = g.reshape(n_seq, S, D)

    small = [jnp.concatenate(dg_ffn1), jnp.concatenate(dg_mix), d_cw, d_cb, d_lag, d_lab, d_lvg, d_lvb,
             jnp.where(tril[None], d_ws, 0.0), d_sb.T, jnp.concatenate(dg_ffn2), dg_final, loss_part[:, :1]]
    small_shapes = [(depth, D), (depth, D), (CONV_WIDTH, CA), (1, CA), (1, CA), (1, CA), (1, GB, DB), (1, GB, DB),
                    (1, GB, CHUNK, CHUNK), (1, GB, CHUNK), (depth, D), (D,), ()]
    da, db, s, gh, h = token
    small_gather, token = _split_start(True, [_pack(small)], "small_gather_start")

    for which, lhs, rhs in ((2, s, gh), (1, db, h), (0, da, h)):
        dw = _mm_tn(lhs, rhs, f"dw_ffn0_{which}", dep=token, out_dtype=BF16)
        exchanges[f"ffn0_{which}"], token = _split_start(False, [dw], f"exchange_ffn0_{which}_start")

    small_all, = _split_wait(small_gather, token, "small_gather_wait")[1]
    red = _unpack(_sum_blocks(small_all.reshape(N_DEV, -1, LANES)), small_shapes)
    (gr_g_ffn1, gr_g_mix, gr_cw_full, gr_cb, gr_lag, gr_lab, gr_lvg, gr_lvb, gr_sp_w, gr_sp_b,
     gr_g_ffn2, gr_g_final, loss) = red
    n_cw = conv_w.shape[2]
    gr_cw = lax.dynamic_slice(gr_cw_full, (0, my_block * n_cw), (CONV_WIDTH, n_cw))[None]

    def landed(key, after):
        return _split_wait(exchanges[key], after, f"exchange_{key}_wait")[1]

    parts_ffn = [None] * (6 * depth)
    for k in range(1, 2 * depth):
        parts_ffn[3 * k:3 * k + 3] = landed(f"ffn{k}", token)
    parts_out, parts_in = landed("mix", token)
    parts_o, parts_qkv = landed("attn", token)

    grads = {
        "g_ffn1": gr_g_ffn1, "g_mix": gr_g_mix, "conv_w": gr_cw, "conv_b": gr_cb, "ln_a_g": gr_lag,
        "ln_a_b": gr_lab, "ln_v_g": gr_lvg, "ln_v_b": gr_lvb, "sp_w": gr_sp_w, "sp_b": gr_sp_b,
        "g_ffn2": gr_g_ffn2, "g_final": gr_g_final,
    }
    weights = dict(g_ffn1=g_ffn1, w_ffn1_gate=w_ffn1_gate, w_ffn1_up=w_ffn1_up, w_ffn1_down=w_ffn1_down, g_mix=g_mix,
                   w_in_ab=w_in_ab, conv_w=conv_w, conv_b=conv_b, ln_a_g=ln_a_g, ln_a_b=ln_a_b, ln_v_g=ln_v_g,
                   ln_v_b=ln_v_b, sp_w=sp_w, sp_b=sp_b, w_out_ab=w_out_ab, w_qkv=w_qkv, w_o=w_o, g_ffn2=g_ffn2,
                   w_ffn2_gate=w_ffn2_gate, w_ffn2_up=w_ffn2_up, w_ffn2_down=w_ffn2_down, g_final=g_final)
    m_in = dict(g_ffn1=m_g_ffn1, w_ffn1_gate=m_w_ffn1_gate, w_ffn1_up=m_w_ffn1_up, w_ffn1_down=m_w_ffn1_down,
                g_mix=m_g_mix, w_in_ab=m_w_in_ab, conv_w=m_conv_w, conv_b=m_conv_b, ln_a_g=m_ln_a_g, ln_a_b=m_ln_a_b,
                ln_v_g=m_ln_v_g, ln_v_b=m_ln_v_b, sp_w=m_sp_w, sp_b=m_sp_b, w_out_ab=m_w_out_ab, w_qkv=m_w_qkv,
                w_o=m_w_o, g_ffn2=m_g_ffn2, w_ffn2_gate=m_w_ffn2_gate, w_ffn2_up=m_w_ffn2_up,
                w_ffn2_down=m_w_ffn2_down, g_final=m_g_final)
    v_in = dict(g_ffn1=v_g_ffn1, w_ffn1_gate=v_w_ffn1_gate, w_ffn1_up=v_w_ffn1_up, w_ffn1_down=v_w_ffn1_down,
                g_mix=v_g_mix, w_in_ab=v_w_in_ab, conv_w=v_conv_w, conv_b=v_conv_b, ln_a_g=v_ln_a_g, ln_a_b=v_ln_a_b,
                ln_v_g=v_ln_v_g, ln_v_b=v_ln_v_b, sp_w=v_sp_w, sp_b=v_sp_b, w_out_ab=v_w_out_ab, w_qkv=v_w_qkv,
                w_o=v_w_o, g_ffn2=v_g_ffn2, w_ffn2_gate=v_w_ffn2_gate, w_ffn2_up=v_w_ffn2_up,
                w_ffn2_down=v_w_ffn2_down, g_final=v_g_final)
    names = list(weights)
    grads = {n: grads[n].reshape(weights[n].shape) for n in grads}

    delta, new_m, new_v = {}, {}, {}

    def adamw_big(n, parts):
        if weights[n].shape[-1] == D:
            view = back = lambda a: a
        else:
            view = back = lambda a: jnp.swapaxes(a, 1, 2)
        out = _adamw_parts(parts, view(weights[n]), view(m_in[n]), view(v_in[n]), f"adamw_{n}")
        grads[n], delta[n], new_m[n], new_v[n] = [back(a) for a in out]

    adamw_big("w_in_ab", [parts_in])
    adamw_big("w_out_ab", [parts_out])
    adamw_big("w_qkv", [parts_qkv])
    adamw_big("w_o", [parts_o])
    kinds = ("gate", "up", "down")
    for which, kind in enumerate(kinds):
        adamw_big(f"w_ffn2_{kind}", [parts_ffn[_ffn_index(l, 1) + which] for l in range(depth)])
    big = [n for n in names if n.startswith("w_")]
    after = jnp.concatenate([delta[n].reshape(-1)[:1] for n in big if n in delta]).reshape(1, -1)
    for which in (2, 1, 0):
        parts_ffn[which], = landed(f"ffn0_{which}", after)
    for which, kind in enumerate(kinds):
        adamw_big(f"w_ffn1_{kind}", [parts_ffn[_ffn_index(l, 0) + which] for l in range(depth)])
    little = [n for n in names if n not in big]
    shapes = [weights[n].shape for n in little]
    d, nm, nv = _adamw(_pack([weights[n] for n in little]), _pack([grads[n] for n in little]),
                       _pack([m_in[n] for n in little]), _pack([v_in[n] for n in little]), "adamw_small")
    for n, dd, mm, vv in zip(little, _unpack(d, shapes), _unpack(nm, shapes), _unpack(nv, shapes)):
        delta[n], new_m[n], new_v[n] = dd, mm, vv

    return (loss, grad_x, *[grads[n] for n in names], *[delta[n] for n in names],
            *[new_m[n] for n in names], *[new_v[n] for n in names])
```

```python
import jax
import jax.numpy as jnp
from jax import lax
from jax.experimental import pallas as pl
from jax.experimental.pallas import tpu as pltpu

F32 = jnp.float32
BF16 = jnp.bfloat16

D_MODEL = 1024
CA = 512
CB = 512
GB = 4
DB = 128
CHUNK = 128
CONV_WIDTH = 31
N_HEADS = 16
HEAD_DIM = 64
EPS = 1e-6
N_DEV = 8
LANES = 128
SUBLANES = 8
QB = 128
ATT_TQ = 1024
FFN_TN = 2816
FFN_TM = 256
HALO = 32
CONV_ROWS = 32
ATT_SCALE = HEAD_DIM ** -0.5

ADAM_LR = 0.001
ADAM_B1 = 0.9
ADAM_B2 = 0.999
ADAM_EPS = 1e-08
ADAM_WD = 0.01
ADAM_STEP = 10

NT = (((1,), (1,)), ((), ()))
NN = (((1,), (0,)), ((), ()))
TN = (((0,), (0,)), ((), ()))
MESH = pl.DeviceIdType.MESH
ANY = pl.BlockSpec(memory_space=pl.ANY)
VMEM_LIMIT = 60 * 1024 * 1024


def _dot(a, b, dims):
    return lax.dot_general(a, b, dims, preferred_element_type=F32)


def _cp(*sem):
    return pltpu.CompilerParams(dimension_semantics=sem, vmem_limit_bytes=VMEM_LIMIT)


def _pcall(body, *, in_specs, args, dep=None, **kw):
    if dep is not None:
        n_in = len(in_specs)
        inner = body

        def body(*refs):
            inner(*refs[:n_in], *refs[n_in + 1:])

        in_specs = list(in_specs) + [ANY]
        args = tuple(args) + (dep,)
    return pl.pallas_call(body, in_specs=list(in_specs), **kw)(*args)


def _tile(n, want):
    if n <= want:
        return n
    t = want - want % LANES
    while t > LANES and n % t:
        t -= LANES
    assert n % t == 0, (n, want)
    return t


def _sigmoid(x):
    return 0.5 * jnp.tanh(0.5 * x) + 0.5


def _rstd(x):
    return lax.rsqrt(jnp.mean(x * x, axis=-1, keepdims=True) + EPS)


def _rms_bwd(x, g, dh):
    r = _rstd(x)
    u = dh * g
    dx = r * (u - x * (r * r) * jnp.mean(u * x, axis=-1, keepdims=True))
    dg = jnp.sum(dh * x * r, axis=0, keepdims=True)
    return dx, dg


def _ln_fwd(x, g, b):
    mu = jnp.mean(x, axis=-1, keepdims=True)
    xc = x - mu
    r = lax.rsqrt(jnp.mean(xc * xc, axis=-1, keepdims=True) + EPS)
    xh = xc * r
    return xh * g + b, xh, r


def _ln_bwd(dy, xh, r, g):
    dxh = dy * g
    return r * (dxh - jnp.mean(dxh, axis=-1, keepdims=True)
                - xh * jnp.mean(dxh * xh, axis=-1, keepdims=True))


def _ffn_fwd(x, g, wall, base, name, dep=None):
    T, D = x.shape
    F = wall.shape[1]
    tm, tn = _tile(T, FFN_TM), _tile(F, FFN_TN)
    n_j = F // tn

    def body(x_ref, g_ref, wg_ref, wu_ref, wd_ref, xo_ref, a_ref, b_ref, h_ref, acc_ref):
        j = pl.program_id(1)

        @pl.when(j == 0)
        def _():
            xv = x_ref[...]
            h_ref[...] = (xv * _rstd(xv) * g_ref[...]).astype(BF16)
            acc_ref[...] = jnp.zeros_like(acc_ref)

        h = h_ref[...]
        a = _dot(h, wg_ref[...], NT)
        b = _dot(h, wu_ref[...], NT)
        a_ref[...] = a.astype(BF16)
        b_ref[...] = b.astype(BF16)
        s = (a * _sigmoid(a) * b).astype(BF16)
        acc_ref[...] += _dot(s, wd_ref[...], NN)

        @pl.when(j == n_j - 1)
        def _():
            xo_ref[...] = x_ref[...] + 0.5 * acc_ref[...]

    single = pl.Buffered(1) if n_j == 1 else None
    wspec = lambda k: pl.BlockSpec((None, tn, D), lambda i, j: (base + k, j, 0), pipeline_mode=single)
    return _pcall(
        body, name=name, grid=(T // tm, n_j), dep=dep, args=(x, g, wall, wall, wall),
        in_specs=[pl.BlockSpec((tm, D), lambda i, j: (i, 0)), pl.BlockSpec((1, D), lambda i, j: (0, 0)),
                  wspec(0), wspec(1), wspec(2)],
        out_specs=[pl.BlockSpec((tm, D), lambda i, j: (i, 0)), pl.BlockSpec((tm, tn), lambda i, j: (i, j)),
                   pl.BlockSpec((tm, tn), lambda i, j: (i, j)), pl.BlockSpec((tm, D), lambda i, j: (i, 0))],
        out_shape=[jax.ShapeDtypeStruct((T, D), F32), jax.ShapeDtypeStruct((T, F), BF16),
                   jax.ShapeDtypeStruct((T, F), BF16), jax.ShapeDtypeStruct((T, D), BF16)],
        scratch_shapes=[pltpu.VMEM((tm, D), F32)],
        compiler_params=_cp("parallel", "arbitrary"),
    )


def _ffn_bwd(go, x, g, a, b, wall, base, name, dep=None):
    T, D = x.shape
    F = wall.shape[1]
    tm, tn = _tile(T, FFN_TM), _tile(F, FFN_TN)
    n_j = F // tn

    def body(go_ref, x_ref, g_ref, a_ref, b_ref, wg_ref, wu_ref, wd_ref,
             gx_ref, dg_ref, da_ref, db_ref, s_ref, gh_ref, acc_ref):
        i, j = pl.program_id(0), pl.program_id(1)

        @pl.when(j == 0)
        def _():
            gh_ref[...] = (0.5 * go_ref[...]).astype(BF16)
            acc_ref[...] = jnp.zeros_like(acc_ref)

        @pl.when((i == 0) & (j == 0))
        def _():
            dg_ref[...] = jnp.zeros_like(dg_ref)

        ds = _dot(gh_ref[...], wd_ref[...], NT)
        av = a_ref[...].astype(F32)
        bv = b_ref[...].astype(F32)
        sig = _sigmoid(av)
        sl = av * sig
        dab = ((ds * bv) * (sig + sl * (1.0 - sig))).astype(BF16)
        dbb = (ds * sl).astype(BF16)
        s_ref[...] = (sl * bv).astype(BF16)
        da_ref[...] = dab
        db_ref[...] = dbb
        acc_ref[...] += _dot(dab, wg_ref[...], NN) + _dot(dbb, wu_ref[...], NN)

        @pl.when(j == n_j - 1)
        def _():
            dx, dg = _rms_bwd(x_ref[...], g_ref[...], acc_ref[...])
            gx_ref[...] = go_ref[...] + dx
            dg_ref[...] += dg

    single = pl.Buffered(1) if n_j == 1 else None
    wspec = lambda k: pl.BlockSpec((None, tn, D), lambda i, j: (base + k, j, 0), pipeline_mode=single)
    row = pl.BlockSpec((tm, D), lambda i, j: (i, 0))
    hid = pl.BlockSpec((tm, tn), lambda i, j: (i, j))
    vec = pl.BlockSpec((1, D), lambda i, j: (0, 0))
    return _pcall(
        body, name=name, grid=(T // tm, n_j), dep=dep, args=(go, x, g, a, b, wall, wall, wall),
        in_specs=[row, row, vec, hid, hid, wspec(0), wspec(1), wspec(2)],
        out_specs=[row, vec, hid, hid, hid, row],
        out_shape=[jax.ShapeDtypeStruct((T, D), F32), jax.ShapeDtypeStruct((1, D), F32),
                   jax.ShapeDtypeStruct((T, F), BF16), jax.ShapeDtypeStruct((T, F), BF16),
                   jax.ShapeDtypeStruct((T, F), BF16), jax.ShapeDtypeStruct((T, D), BF16)],
        scratch_shapes=[pltpu.VMEM((tm, D), F32)],
        compiler_params=_cp("arbitrary", "arbitrary"),
    )


def _mm_tn(a, b, name, dep=None, out_dtype=F32):
    T, M = a.shape
    N = b.shape[1]
    tmm, tk = _tile(M, 1536), _tile(T, 2048)
    n_k = T // tk
    narrow = out_dtype != F32

    def body(a_ref, b_ref, o_ref, *scratch):
        acc_ref = scratch[0] if narrow else o_ref

        @pl.when(pl.program_id(1) == 0)
        def _():
            acc_ref[...] = jnp.zeros_like(acc_ref)

        acc_ref[...] += _dot(a_ref[...].astype(BF16), b_ref[...].astype(BF16), TN)
        if narrow:
            @pl.when(pl.program_id(1) == n_k - 1)
            def _():
                o_ref[...] = acc_ref[...].astype(out_dtype)

    return _pcall(
        body, name=name, grid=(M // tmm, n_k), dep=dep, args=(a, b),
        in_specs=[pl.BlockSpec((tk, tmm), lambda m, k: (k, m)), pl.BlockSpec((tk, N), lambda m, k: (k, 0))],
        out_specs=pl.BlockSpec((tmm, N), lambda m, k: (m, 0)),
        out_shape=jax.ShapeDtypeStruct((M, N), out_dtype),
        scratch_shapes=[pltpu.VMEM((tmm, N), F32)] if narrow else [],
        compiler_params=_cp("parallel", "arbitrary"),
    )


def _mm_nt(x, wt, g, out_dtype, name, dep=None):
    T, K = x.shape
    N = wt.shape[0]
    tm, tn = _tile(T, 512), N
    norm = g is not None

    def body(*refs):
        if norm:
            x_ref, g_ref, w_ref, o_ref, h_ref = refs
        else:
            x_ref, w_ref, o_ref, h_ref = refs

        @pl.when(pl.program_id(1) == 0)
        def _():
            xv = x_ref[...].astype(F32)
            if norm:
                xv = xv * _rstd(xv) * g_ref[...]
            h_ref[...] = xv.astype(BF16)

        o_ref[...] = _dot(h_ref[...], w_ref[...], NT).astype(out_dtype)

    row = pl.BlockSpec((tm, K), lambda i, j: (i, 0))
    wsp = pl.BlockSpec((tn, K), lambda i, j: (j, 0))
    osp = pl.BlockSpec((tm, tn), lambda i, j: (i, j))
    if norm:
        return pl.pallas_call(
            body, name=name, grid=(T // tm, N // tn),
            in_specs=[row, pl.BlockSpec((1, K), lambda i, j: (0, 0)), wsp],
            out_specs=[osp, row],
            out_shape=[jax.ShapeDtypeStruct((T, N), out_dtype), jax.ShapeDtypeStruct((T, K), BF16)],
            compiler_params=_cp("parallel", "arbitrary"),
        )(x, g, wt)
    return _pcall(
        body, name=name, grid=(T // tm, N // tn), dep=dep, args=(x, wt),
        in_specs=[row, wsp], out_specs=osp,
        out_shape=jax.ShapeDtypeStruct((T, N), out_dtype),
        scratch_shapes=[pltpu.VMEM((tm, K), BF16)],
        compiler_params=_cp("parallel", "arbitrary"),
    )


def _mm_nn_res(act, w, resid, name):
    T, K = act.shape
    D = w.shape[1]
    tm = _tile(T, 512)

    def body(a_ref, w_ref, r_ref, o_ref):
        o_ref[...] = r_ref[...] + _dot(a_ref[...].astype(BF16), w_ref[...], NN)

    return pl.pallas_call(
        body, name=name, grid=(T // tm,),
        in_specs=[pl.BlockSpec((tm, K), lambda i: (i, 0)), pl.BlockSpec((K, D), lambda i: (0, 0)),
                  pl.BlockSpec((tm, D), lambda i: (i, 0))],
        out_specs=pl.BlockSpec((tm, D), lambda i: (i, 0)),
        out_shape=jax.ShapeDtypeStruct((T, D), F32),
        compiler_params=_cp("parallel"),
    )(act, w, resid)


def _mm_nn_rmsbwd(acts, w, x, g, gprev, name, dep=None):
    T = acts[0].shape[0]
    ks = [a.shape[1] for a in acts]
    K, D = w.shape
    assert sum(ks) == K
    tm = _tile(T, 512)
    na = len(acts)

    def body(*refs):
        a_refs = refs[:na]
        w_ref, x_ref, g_ref, gp_ref, o_ref, dg_ref = refs[na:]

        @pl.when(pl.program_id(0) == 0)
        def _():
            dg_ref[...] = jnp.zeros_like(dg_ref)

        dh, off = None, 0
        for a_ref, k in zip(a_refs, ks):
            part = _dot(a_ref[...].astype(BF16), w_ref[off:off + k, :], NN)
            dh = part if dh is None else dh + part
            off += k
        dx, dg = _rms_bwd(x_ref[...], g_ref[...], dh)
        o_ref[...] = gp_ref[...] + dx
        dg_ref[...] += dg

    row = pl.BlockSpec((tm, D), lambda i: (i, 0))
    vec = pl.BlockSpec((1, D), lambda i: (0, 0))
    return _pcall(
        body, name=name, grid=(T // tm,), dep=dep, args=(*acts, w, x, g, gprev),
        in_specs=[pl.BlockSpec((tm, k), lambda i: (i, 0)) for k in ks]
        + [pl.BlockSpec((K, D), lambda i: (0, 0)), row, vec, row],
        out_specs=[row, vec],
        out_shape=[jax.ShapeDtypeStruct((T, D), F32), jax.ShapeDtypeStruct((1, D), F32)],
        compiler_params=_cp("arbitrary"),
    )


def _loss_head(x, g, target):
    T, D = x.shape
    tm = _tile(T, 512)

    def body(x_ref, g_ref, t_ref, dx_ref, loss_ref, dg_ref):
        @pl.when(pl.program_id(0) == 0)
        def _():
            loss_ref[...] = jnp.zeros_like(loss_ref)
            dg_ref[...] = jnp.zeros_like(dg_ref)

        xv = x_ref[...]
        gv = g_ref[...]
        e = xv * _rstd(xv) * gv - t_ref[...]
        per_tok = jnp.sum(e * e, axis=-1, keepdims=True) * (1.0 / D)
        loss_ref[...] += 0.5 * jnp.sum(per_tok, axis=0, keepdims=True)
        dx, dg = _rms_bwd(xv, gv, e * (1.0 / D))
        dx_ref[...] = dx
        dg_ref[...] += dg

    row = pl.BlockSpec((tm, D), lambda i: (i, 0))
    vec = pl.BlockSpec((1, D), lambda i: (0, 0))
    return pl.pallas_call(
        body, name="loss_head", grid=(T // tm,),
        in_specs=[row, vec, row],
        out_specs=[row, pl.BlockSpec((1, LANES), lambda i: (0, 0)), vec],
        out_shape=[jax.ShapeDtypeStruct((T, D), F32), jax.ShapeDtypeStruct((1, LANES), F32),
                   jax.ShapeDtypeStruct((1, D), F32)],
        compiler_params=_cp("arbitrary"),
    )(x, g, target)


def _log_gates(z):
    neg_abs = lax.bitcast_convert_type(lax.bitcast_convert_type(z, jnp.uint32) | jnp.uint32(0x80000000), F32)
    ls = jnp.minimum(z, 0.0) - jnp.log(1.0 + jnp.exp(neg_abs))
    return ls, ls - z


def _cumsum_mm(v, u2):
    hi = v.astype(BF16)
    lo = (v - hi.astype(F32)).astype(BF16)
    return _dot(jnp.concatenate([hi, lo], axis=1), u2, NN)


def _half_rowsum(v):
    n = v.shape[0]
    s0 = jnp.sum(v[:, :QB], axis=1, keepdims=True)
    s1 = jnp.sum(v[:, QB:], axis=1, keepdims=True)
    return jnp.concatenate([jnp.broadcast_to(s0, (n, QB)), jnp.broadcast_to(s1, (n, QB))], axis=1)


def _stack_heads(src_ref, dst_ref, n_blk):
    m0 = lax.broadcasted_iota(jnp.int32, (1, LANES), 1) < HEAD_DIM

    def fill(c, carry):
        blk = src_ref[pl.ds(pl.multiple_of(c * QB, QB), QB), :]
        zero = jnp.zeros_like(blk)
        dst_ref[c, 0:QB, :] = jnp.where(m0, blk, zero)
        dst_ref[c, QB:2 * QB, :] = jnp.where(m0, zero, blk)
        return carry

    lax.fori_loop(0, n_blk, fill, 0)


def _diag_mask(tq, j):
    n = tq - j * QB
    row = lax.broadcasted_iota(jnp.int32, (n, 2 * QB), 0)
    col = lax.broadcasted_iota(jnp.int32, (n, 2 * QB), 1)
    return (col & (QB - 1)) < row


def _tri_blockdiag(upper):
    r = lax.broadcasted_iota(jnp.int32, (2 * QB, 2 * QB), 0)
    c = lax.broadcasted_iota(jnp.int32, (2 * QB, 2 * QB), 1)
    same = (r // QB) == (c // QB)
    u = (same & ((r > c) if upper else (r < c))).astype(BF16)
    return jnp.concatenate([u, u], axis=0)


def _attn_tiles(T, n_seq):
    S = T // n_seq
    tq = ATT_TQ if S % ATT_TQ == 0 else 2 * QB
    assert S % tq == 0
    return S, tq, tq // QB, S // tq, S // QB


def _attn_fwd(qkv, n_seq):
    T = qkv.shape[0]
    S, tq, r, n_q, n_k = _attn_tiles(T, n_seq)
    n_p = D_MODEL // LANES
    n_steps = n_seq * n_p * n_q
    u_suffix = _tri_blockdiag(True)

    def body(q_ref, k_ref, v_ref, u_ref, o_ref, a_hbm, kk_ref, vv_ref, lr_s, acc_s, a_stage, sems):
        qi = pl.program_id(2)
        group = (pl.program_id(0) * n_p + pl.program_id(1)) * n_q + qi

        @pl.when(qi == 0)
        def _():
            _stack_heads(k_ref, kk_ref, n_k)
            _stack_heads(v_ref, vv_ref, n_k)

        u = u_ref[...]
        lr_s[...] = jnp.zeros_like(lr_s)
        acc_s[...] = jnp.zeros_like(acc_s)
        base = ((group // n_q) * (n_q * (n_q + 1) // 2) + (qi * (qi + 1)) // 2) % 2

        def saves(first_kj, half):
            return [pltpu.make_async_copy(a_stage.at[half, j], a_hbm.at[group, first_kj - j], sems.at[half])
                    for j in range(r)]

        @pl.when(group >= 2)
        def _():
            for cp in saves(0, base):
                cp.wait()

        a_stage[base] = jnp.zeros_like(a_stage[0])

        def step(kj, half, j, rows, q, mask, lr, acc):
            ls, lk = _log_gates(_dot(q, kk_ref[kj], NT))
            if mask is not None:
                lk = jnp.where(mask, lk, 0.0)
            a = jnp.exp(ls + _cumsum_mm(lk, u) + lr)
            if mask is not None:
                a = jnp.where(mask, a, 0.0)
            a = a.astype(BF16)
            a_stage[half, j, rows, :] = a
            return lr + _half_rowsum(lk), acc + _dot(a, vv_ref[kj], NN)

        last = (qi + 1) * r - 1
        for n in range(r):
            rows = slice((r - 1 - n) * QB, tq)
            lr, acc = step(last - n, base, n, rows, q_ref[rows, :] * ATT_SCALE, _diag_mask(tq, r - 1 - n),
                           lr_s[rows, :], acc_s[rows, :])
            lr_s[rows, :] = lr
            acc_s[rows, :] = acc
        for cp in saves(last, base):
            cp.start()

        q = q_ref[...] * ATT_SCALE

        def off(it, carry):
            half = (base + it + 1) % 2
            first = (qi - it) * r - 1
            for cp in saves(first, half):
                cp.wait()
            lr, acc = lr_s[...], acc_s[...]
            for j in range(r):
                lr, acc = step(first - j, half, j, slice(0, tq), q, None, lr, acc)
            lr_s[...] = lr
            acc_s[...] = acc
            for cp in saves(first, half):
                cp.start()
            return carry

        lax.fori_loop(0, qi, off, 0)

        @pl.when(group == n_steps - 1)
        def _():
            for cp in saves(0, (base + qi) % 2):
                cp.wait()
            if n_steps * n_q > 1:
                for cp in saves(0, (base + qi + 1) % 2):
                    cp.wait()

        o_ref[...] = acc_s[...].astype(BF16)

    return pl.pallas_call(
        body, name="attn_fwd", grid=(n_seq, n_p, n_q),
        in_specs=[pl.BlockSpec((tq, LANES), lambda b, p, qi: (b * n_q + qi, p)),
                  pl.BlockSpec((S, LANES), lambda b, p, qi: (b, n_p + p)),
                  pl.BlockSpec((S, LANES), lambda b, p, qi: (b, 2 * n_p + p)),
                  pl.BlockSpec((4 * QB, 2 * QB), lambda b, p, qi: (0, 0))],
        out_specs=[pl.BlockSpec((tq, LANES), lambda b, p, qi: (b * n_q + qi, p)), ANY],
        out_shape=[jax.ShapeDtypeStruct((T, D_MODEL), BF16),
                   jax.ShapeDtypeStruct((n_seq * n_p * n_q, n_k, tq, 2 * QB), BF16)],
        scratch_shapes=[pltpu.VMEM((n_k, 2 * QB, LANES), BF16), pltpu.VMEM((n_k, 2 * QB, LANES), BF16),
                        pltpu.VMEM((tq, 2 * QB), F32), pltpu.VMEM((tq, LANES), F32),
                        pltpu.VMEM((2, r, tq, 2 * QB), BF16), pltpu.SemaphoreType.DMA((2,))],
        compiler_params=_cp("arbitrary", "arbitrary", "arbitrary"),
    )(qkv, qkv, qkv, u_suffix)


def _attn_bwd(qkv, do, a_saved, n_seq):
    T = qkv.shape[0]
    S, tq, r, n_q, n_k = _attn_tiles(T, n_seq)
    n_p = D_MODEL // LANES
    n_steps = n_seq * n_p * n_q
    u_prefix = _tri_blockdiag(False)[:2 * QB]

    def body(q_ref, k_ref, v_ref, do_ref, u_ref, a_hbm, dq_ref, dk_out, dv_out,
             kk_ref, vv_ref, cg_s, dq_s, dk_ref, dv_ref, a_stage, sems):
        qi = pl.program_id(2)
        group = (pl.program_id(0) * n_p + pl.program_id(1)) * n_q + qi

        base = ((group // n_q) * (n_q * (n_q + 1) // 2) + (qi * (qi + 1)) // 2) % 2

        def fetches(grp, g, half):
            return [pltpu.make_async_copy(a_hbm.at[grp, g * r + j], a_stage.at[half, j], sems.at[half])
                    for j in range(r)]

        @pl.when(group == 0)
        def _():
            for cp in fetches(group, 0, 0):
                cp.start()

        @pl.when(qi == 0)
        def _():
            _stack_heads(k_ref, kk_ref, n_k)
            _stack_heads(v_ref, vv_ref, n_k)
            dk_ref[...] = jnp.zeros_like(dk_ref)
            dv_ref[...] = jnp.zeros_like(dv_ref)

        u = u_ref[...]
        cg_s[...] = jnp.zeros_like(cg_s)
        dq_s[...] = jnp.zeros_like(dq_s)

        def step(kj, half, j, rows, q, dov, mask, cg, dq):
            kk = kk_ref[kj]
            beta = _sigmoid(_dot(q, kk, NT))
            a = a_stage[half, j, rows, :]
            g = a.astype(F32) * _dot(dov, vv_ref[kj], NT)
            dz = g - (g + _dot(g.astype(BF16), u, NN) + cg) * beta
            if mask is not None:
                dz = jnp.where(mask, dz, 0.0)
            dz = dz.astype(BF16)
            keys = pl.ds(pl.multiple_of(kj * QB, QB), QB)
            dvt = _dot(dov_t[:, rows], a, NN)
            dv_ref[keys, :] += jnp.where(t0, dvt[:, :QB], dvt[:, QB:]).T
            dkt = _dot(q_t[:, rows], dz, NN)
            dk_ref[keys, :] += jnp.where(t0, dkt[:, :QB], dkt[:, QB:]).T
            return cg + _half_rowsum(g), dq + _dot(dz, kk, NN)

        q = q_ref[...] * ATT_SCALE
        dov = do_ref[...]
        q_t = q.astype(F32).T.astype(BF16)
        dov_t = dov.astype(F32).T.astype(BF16)
        t0 = lax.broadcasted_iota(jnp.int32, (LANES, 1), 0) < HEAD_DIM

        def off(it, carry):
            half = (base + it) % 2
            for cp in fetches(group, it, half):
                cp.wait()
            for cp in fetches(group, it + 1, 1 - half):
                cp.start()
            cg, dq = cg_s[...], dq_s[...]
            for j in range(r):
                cg, dq = step(it * r + j, half, j, slice(0, tq), q, dov, None, cg, dq)
            cg_s[...] = cg
            dq_s[...] = dq
            return carry

        lax.fori_loop(0, qi, off, 0)

        half = (base + qi) % 2
        for cp in fetches(group, qi, half):
            cp.wait()

        @pl.when(group < n_steps - 1)
        def _():
            for cp in fetches(group + 1, 0, 1 - half):
                cp.start()

        for j in range(r):
            rows = slice(j * QB, tq)
            cg, dq = step(qi * r + j, half, j, rows, q_ref[rows, :] * ATT_SCALE, do_ref[rows, :],
                          _diag_mask(tq, j), cg_s[rows, :], dq_s[rows, :])
            cg_s[rows, :] = cg
            dq_s[rows, :] = dq
        dq_ref[...] = (dq_s[...] * ATT_SCALE).astype(BF16)

        @pl.when(qi == n_q - 1)
        def _():
            dk_out[...] = dk_ref[...].astype(BF16)
            dv_out[...] = dv_ref[...].astype(BF16)

    qspec = pl.BlockSpec((tq, LANES), lambda b, p, qi: (b * n_q + qi, p))
    seq = lambda off: pl.BlockSpec((S, LANES), lambda b, p, qi: (b, off + p))
    return pl.pallas_call(
        body, name="attn_bwd", grid=(n_seq, n_p, n_q),
        in_specs=[qspec, seq(n_p), seq(2 * n_p), qspec,
                  pl.BlockSpec((2 * QB, 2 * QB), lambda b, p, qi: (0, 0)), ANY],
        out_specs=[qspec, seq(0), seq(0)],
        out_shape=[jax.ShapeDtypeStruct((T, D_MODEL), BF16)] * 3,
        scratch_shapes=[pltpu.VMEM((n_k, 2 * QB, LANES), BF16), pltpu.VMEM((n_k, 2 * QB, LANES), BF16),
                        pltpu.VMEM((tq, 2 * QB), F32), pltpu.VMEM((tq, LANES), F32),
                        pltpu.VMEM((S, LANES), F32), pltpu.VMEM((S, LANES), F32),
                        pltpu.VMEM((2, r, tq, 2 * QB), BF16), pltpu.SemaphoreType.DMA((2,))],
        compiler_params=_cp("arbitrary", "arbitrary", "arbitrary"),
    )(qkv, qkv, qkv, do, u_prefix, a_saved)


def _shifted_copies(sh_ref):
    rows = sh_ref.shape[1] - SUBLANES
    for s in range(1, SUBLANES):
        sh_ref[s, 0:rows, :] = sh_ref[0, s:s + rows, :]


def _shifted(sh_ref, start, n):
    s = start % SUBLANES
    return sh_ref[s, start - s:start - s + n, :]


def _glu_with_halo(av_ref, ag_ref, avh_ref, agh_ref, a0_s, first, ts):
    hal = avh_ref[...] * _sigmoid(agh_ref[...])
    a0_s[0, 0:HALO, :] = jnp.where(first, 0.0, hal)
    a0_s[0, HALO:HALO + ts, :] = av_ref[...] * _sigmoid(ag_ref[...])
    _shifted_copies(a0_s)


def _mix_specs(ts, n_r, with_left):
    blk = lambda c: pl.BlockSpec((ts, CA), lambda b, r: (b * n_r + r, c))
    per = ts // HALO
    left = lambda c: pl.BlockSpec((HALO, CA), lambda b, r: (jnp.maximum((b * n_r + r) * per - 1, 0), c))
    return blk, (left if with_left else None)


def _mix_fwd(z, conv_w, conv_b, ln_a_g, ln_a_b, ln_v_g, ln_v_b, ws, bias2d, n_seq):
    T = z.shape[0]
    S = T // n_seq
    ts = _tile(S, 512)
    n_r = S // ts
    shift = HALO - (CONV_WIDTH - 1)

    def body(av_ref, ag_ref, avh_ref, agh_ref, u_ref, v_ref, cw_ref, cb_ref, lag_ref, lab_ref,
             lvg_ref, lvb_ref, ws_ref, bias_ref, cat_ref, a1_ref, a0_s):
        _glu_with_halo(av_ref, ag_ref, avh_ref, agh_ref, a0_s, pl.program_id(1) == 0, ts)
        for rb in range(ts // CONV_ROWS):
            base = rb * CONV_ROWS
            acc = jnp.broadcast_to(cb_ref[...], (CONV_ROWS, CA))
            for k in range(CONV_WIDTH):
                acc = acc + cw_ref[k:k + 1, :] * _shifted(a0_s, base + shift + k, CONV_ROWS)
            a1_ref[base:base + CONV_ROWS, :] = acc
        y, _, _ = _ln_fwd(a1_ref[...], lag_ref[...], lab_ref[...])
        cat_ref[:, 0:CA] = (y * _sigmoid(y)).astype(BF16)
        for gi in range(GB):
            sl = slice(gi * DB, (gi + 1) * DB)
            v1, _, _ = _ln_fwd(v_ref[:, sl], lvg_ref[:, sl], lvb_ref[:, sl])
            v1 = v1.astype(BF16)
            for c in range(ts // CHUNK):
                rs = slice(c * CHUNK, (c + 1) * CHUNK)
                v2 = _dot(ws_ref[gi], v1[rs], NN) + bias_ref[:, sl]
                cat_ref[rs, CA + gi * DB:CA + (gi + 1) * DB] = (u_ref[rs, sl] * v2).astype(BF16)

    blk, left = _mix_specs(ts, n_r, True)
    vec = pl.BlockSpec((1, CA), lambda b, r: (0, 0))
    return pl.pallas_call(
        body, name="mix_fwd", grid=(n_seq, n_r),
        in_specs=[blk(0), blk(1), left(0), left(1), blk(2), blk(3),
                  pl.BlockSpec((CONV_WIDTH, CA), lambda b, r: (0, 0)), vec, vec, vec, vec, vec,
                  pl.BlockSpec((GB, CHUNK, CHUNK), lambda b, r: (0, 0, 0)),
                  pl.BlockSpec((CHUNK, CB), lambda b, r: (0, 0))],
        out_specs=[pl.BlockSpec((ts, CA + CB), lambda b, r: (b * n_r + r, 0)), blk(0)],
        out_shape=[jax.ShapeDtypeStruct((T, CA + CB), BF16), jax.ShapeDtypeStruct((T, CA), F32)],
        scratch_shapes=[pltpu.VMEM((SUBLANES, HALO + ts, CA), F32)],
        compiler_params=_cp("parallel", "parallel"),
    )(z, z, z, z, z, z, conv_w, conv_b, ln_a_g, ln_a_b, ln_v_g, ln_v_b, ws, bias2d)


def _mix_bwd_rows(dcat, z, a1, ln_a_g, ln_a_b, ln_v_g, ln_v_b, ws, ws_t, bias2d, n_seq):
    T = z.shape[0]
    S = T // n_seq
    ts = _tile(S, 512)
    n_r = S // ts

    def body(dc_ref, u_ref, v_ref, a1_ref, lag_ref, lab_ref, lvg_ref, lvb_ref, ws_ref, wst_ref, bias_ref,
             da1_ref, dz_ref, dlag_ref, dlab_ref, dlvg_ref, dlvb_ref, dws_ref, dsb_ref, dv1_s, dbias_s):
        first = (pl.program_id(0) == 0) & (pl.program_id(1) == 0)
        last = (pl.program_id(0) == n_seq - 1) & (pl.program_id(1) == n_r - 1)

        @pl.when(first)
        def _():
            for ref in (dlag_ref, dlab_ref, dlvg_ref, dlvb_ref, dws_ref, dbias_s):
                ref[...] = jnp.zeros_like(ref)

        lag = lag_ref[...]
        y, xh, r = _ln_fwd(a1_ref[...], lag, lab_ref[...])
        sig = _sigmoid(y)
        dy = dc_ref[:, 0:CA] * (sig * (1.0 + y * (1.0 - sig)))
        dlag_ref[...] += jnp.sum(dy * xh, axis=0, keepdims=True)
        dlab_ref[...] += jnp.sum(dy, axis=0, keepdims=True)
        da1_ref[...] = _ln_bwd(dy, xh, r, lag)

        tril = (lax.broadcasted_iota(jnp.int32, (CHUNK, CHUNK), 0)
                >= lax.broadcasted_iota(jnp.int32, (CHUNK, CHUNK), 1))
        for gi in range(GB):
            sl = slice(gi * DB, (gi + 1) * DB)
            lvg = lvg_ref[:, sl]
            v1, vh, vr = _ln_fwd(v_ref[:, sl], lvg, lvb_ref[:, sl])
            v1 = v1.astype(BF16)
            for c in range(ts // CHUNK):
                rs = slice(c * CHUNK, (c + 1) * CHUNK)
                v2 = _dot(ws_ref[gi], v1[rs], NN) + bias_ref[:, sl]
                dbo = dc_ref[rs, CA + gi * DB:CA + (gi + 1) * DB]
                dz_ref[rs, sl] = (dbo * v2).astype(BF16)
                dv2 = dbo * u_ref[rs, sl]
                dbias_s[:, sl] += dv2
                dv2b = dv2.astype(BF16)
                dws_ref[gi] += jnp.where(tril, _dot(dv2b, v1[rs], NT), 0.0)
                dv1_s[rs, :] = _dot(wst_ref[gi], dv2b, NN)
            dv1 = dv1_s[...]
            dlvg_ref[:, sl] += jnp.sum(dv1 * vh, axis=0, keepdims=True)
            dlvb_ref[:, sl] += jnp.sum(dv1, axis=0, keepdims=True)
            dz_ref[:, CB + gi * DB:CB + (gi + 1) * DB] = _ln_bwd(dv1, vh, vr, lvg).astype(BF16)

        @pl.when(last)
        def _():
            col = lax.broadcasted_iota(jnp.int32, (CHUNK, GB), 1)
            out = jnp.zeros((CHUNK, GB), F32)
            for gi in range(GB):
                s = jnp.sum(dbias_s[:, gi * DB:(gi + 1) * DB], axis=1, keepdims=True)
                out = out + jnp.where(col == gi, s, 0.0)
            dsb_ref[...] = out

    blk, _ = _mix_specs(ts, n_r, False)
    vec = pl.BlockSpec((1, CA), lambda b, r: (0, 0))
    mat = pl.BlockSpec((GB, CHUNK, CHUNK), lambda b, r: (0, 0, 0))
    wide = pl.BlockSpec((ts, CA + CB), lambda b, r: (b * n_r + r, 0))
    return pl.pallas_call(
        body, name="mix_bwd_rows", grid=(n_seq, n_r),
        in_specs=[wide, blk(2), blk(3), blk(0), vec, vec, vec, vec, mat, mat,
                  pl.BlockSpec((CHUNK, CB), lambda b, r: (0, 0))],
        out_specs=[blk(0), wide, vec, vec, vec, vec, mat, pl.BlockSpec((CHUNK, GB), lambda b, r: (0, 0))],
        out_shape=[jax.ShapeDtypeStruct((T, CA), F32), jax.ShapeDtypeStruct((T, 2 * CB), BF16)]
        + [jax.ShapeDtypeStruct((1, CA), F32)] * 4
        + [jax.ShapeDtypeStruct((GB, CHUNK, CHUNK), F32), jax.ShapeDtypeStruct((CHUNK, GB), F32)],
        scratch_shapes=[pltpu.VMEM((ts, DB), F32), pltpu.VMEM((CHUNK, CB), F32)],
        compiler_params=_cp("arbitrary", "arbitrary"),
    )(dcat, z, z, a1, ln_a_g, ln_a_b, ln_v_g, ln_v_b, ws, ws_t, bias2d)


def _mix_bwd_conv(da1, z, conv_w, n_seq):
    T = z.shape[0]
    S = T // n_seq
    ts = _tile(S, 512)
    n_r = S // ts
    per = ts // HALO
    shift = HALO - (CONV_WIDTH - 1)
    fold = CONV_ROWS // 8

    def body(d_ref, dh_ref, av_ref, ag_ref, avh_ref, agh_ref, cw_ref,
             dz_ref, dcw_ref, dcb_ref, a0_s, d1_s, da0_s, dw8_s):
        first = (pl.program_id(0) == 0) & (pl.program_id(1) == 0)
        last = (pl.program_id(0) == n_seq - 1) & (pl.program_id(1) == n_r - 1)

        @pl.when(first)
        def _():
            dw8_s[...] = jnp.zeros_like(dw8_s)
            dcb_ref[...] = jnp.zeros_like(dcb_ref)

        _glu_with_halo(av_ref, ag_ref, avh_ref, agh_ref, a0_s, pl.program_id(1) == 0, ts)
        d1_s[0, 0:ts, :] = d_ref[...]
        d1_s[0, ts:ts + HALO, :] = jnp.where(pl.program_id(1) == n_r - 1, 0.0, dh_ref[...])
        _shifted_copies(d1_s)
        dcb_ref[...] += jnp.sum(d_ref[...], axis=0, keepdims=True)
        for rb in range(ts // CONV_ROWS):
            base = rb * CONV_ROWS
            dcur = d1_s[0, base:base + CONV_ROWS, :]
            acc = jnp.zeros((CONV_ROWS, CA), F32)
            for k in range(CONV_WIDTH):
                back = CONV_WIDTH - 1 - k
                acc = acc + cw_ref[k:k + 1, :] * _shifted(d1_s, base + back, CONV_ROWS)
                prod = dcur * _shifted(a0_s, base + shift + k, CONV_ROWS)
                part = prod[0:8]
                for f in range(1, fold):
                    part = part + prod[8 * f:8 * f + 8]
                dw8_s[k] += part
            da0_s[base:base + CONV_ROWS, :] = acc
        da0 = da0_s[...]
        sig = _sigmoid(ag_ref[...])
        dz_ref[:, 0:CA] = (da0 * sig).astype(BF16)
        dz_ref[:, CA:2 * CA] = (da0 * av_ref[...] * sig * (1.0 - sig)).astype(BF16)

        @pl.when(last)
        def _():
            for k in range(CONV_WIDTH):
                dcw_ref[k:k + 1, :] = jnp.sum(dw8_s[k], axis=0, keepdims=True)

    blk, left = _mix_specs(ts, n_r, True)
    n_halo_blocks = T // HALO
    right = pl.BlockSpec((HALO, CA), lambda b, r: (jnp.minimum((b * n_r + r + 1) * per, n_halo_blocks - 1), 0))
    return pl.pallas_call(
        body, name="mix_bwd_conv", grid=(n_seq, n_r),
        in_specs=[blk(0), right, blk(0), blk(1), left(0), left(1),
                  pl.BlockSpec((CONV_WIDTH, CA), lambda b, r: (0, 0))],
        out_specs=[pl.BlockSpec((ts, 2 * CA), lambda b, r: (b * n_r + r, 0)),
                   pl.BlockSpec((CONV_WIDTH, CA), lambda b, r: (0, 0)), pl.BlockSpec((1, CA), lambda b, r: (0, 0))],
        out_shape=[jax.ShapeDtypeStruct((T, 2 * CA), BF16), jax.ShapeDtypeStruct((CONV_WIDTH, CA), F32),
                   jax.ShapeDtypeStruct((1, CA), F32)],
        scratch_shapes=[pltpu.VMEM((SUBLANES, HALO + ts, CA), F32), pltpu.VMEM((SUBLANES, ts + HALO, CA), F32),
                        pltpu.VMEM((ts, CA), F32), pltpu.VMEM((CONV_WIDTH, 8, CA), F32)],
        compiler_params=_cp("arbitrary", "arbitrary"),
    )(da1, da1, z, z, z, z, conv_w)


def _row_tile(R, want):
    t = min(R, want)
    t -= t % 8
    while t > 8 and R % t:
        t -= 8
    return t if t >= 8 and R % t == 0 else R


def _adam_step(w, g, m, v):
    nm = ADAM_B1 * m + (1.0 - ADAM_B1) * g
    nv = ADAM_B2 * v + (1.0 - ADAM_B2) * (g * g)
    m_hat = nm / (1.0 - ADAM_B1 ** ADAM_STEP)
    v_hat = nv / (1.0 - ADAM_B2 ** ADAM_STEP)
    return -ADAM_LR * (m_hat / (jnp.sqrt(v_hat) + ADAM_EPS) + ADAM_WD * w), nm, nv


def _adamw_parts(parts, w, m, v, name):
    L, n, C = w.shape
    assert len(parts) == L
    tr = _row_tile(n, 192)
    n_i = n // tr

    def body(*refs):
        p_refs = refs[:L]
        w_ref, m_ref, v_ref, g_ref, d_ref, nm_ref, nv_ref = refs[L:]
        for k in range(L):
            @pl.when(pl.program_id(0) == k)
            def _(k=k):
                acc = p_refs[k][0].astype(F32)
                for s in range(1, N_DEV):
                    acc = acc + p_refs[k][s].astype(F32)
                g_ref[...] = acc

        d_ref[...], nm_ref[...], nv_ref[...] = _adam_step(w_ref[...], g_ref[...], m_ref[...], v_ref[...])

    def part_spec(k):
        return pl.BlockSpec((N_DEV, tr, C),
                            lambda l, i: (0, jnp.where(l == k, i, jnp.where(l < k, 0, n_i - 1)), 0))

    blk = pl.BlockSpec((None, tr, C), lambda l, i: (l, i, 0))
    return pl.pallas_call(
        body, name=name, grid=(L, n_i),
        in_specs=[part_spec(k) for k in range(L)] + [blk] * 3, out_specs=[blk] * 4,
        out_shape=[jax.ShapeDtypeStruct((L, n, C), F32)] * 4,
        compiler_params=_cp("arbitrary", "arbitrary"),
    )(*parts, w, m, v)


def _adamw(w, g, m, v, name):
    R, C = w.shape
    tr = _row_tile(R, 256)

    def body(w_ref, g_ref, m_ref, v_ref, d_ref, nm_ref, nv_ref):
        d_ref[...], nm_ref[...], nv_ref[...] = _adam_step(w_ref[...], g_ref[...], m_ref[...], v_ref[...])

    blk = pl.BlockSpec((tr, C), lambda i: (i, 0))
    return pl.pallas_call(
        body, name=name, grid=(R // tr,),
        in_specs=[blk] * 4, out_specs=[blk] * 3,
        out_shape=[jax.ShapeDtypeStruct((R, C), F32)] * 3,
        compiler_params=_cp("parallel"),
    )(w, g, m, v)


def _me():
    return lax.axis_index("x"), lax.axis_index("y"), lax.axis_index("c")


def _block_rows(ref, dev, n):
    start = (4 * dev[0] + 2 * dev[1] + dev[2]) * n
    if len(ref.shape) == 2:
        return ref.at[pl.ds(start, n), :]
    return ref.at[:, pl.ds(start, n), :]


def _all_gather(shards):
    na = len(shards)
    ns = [s.shape[-2] for s in shards]

    def body(*refs):
        ins, outs = refs[:na], refs[na:2 * na]
        send_sems, recv_sems, local_sems = refs[2 * na:]
        x, y, c = _me()
        me, sibling = (x, y, c), (x, y, 1 - c)
        chips = [(1 - x, y), (x, 1 - y), (1 - x, 1 - y)]

        def copy(a, k, block, to, src=None):
            dst = _block_rows(outs[a], block, ns[a])
            return pltpu.make_async_remote_copy(
                src_ref=dst if src is None else src, dst_ref=dst,
                send_sem=send_sems.at[a, k], recv_sem=recv_sems.at[a, k], device_id=to, device_id_type=MESH)

        mine = [pltpu.make_async_copy(ins[a], _block_rows(outs[a], me, ns[a]), local_sems.at[a]) for a in range(na)]
        for cp in mine:
            cp.start()
        first = []
        for a in range(na):
            first.append(copy(a, 0, me, sibling, src=ins[a]))
            first += [copy(a, 1 + j, me, (*chip, c), src=ins[a]) for j, chip in enumerate(chips)]
        for cp in first:
            cp.start()
        passed = []
        for j, chip in enumerate(chips):
            for a in range(na):
                copy(a, 1 + j, (*chip, c), me).wait_recv()
                fwd = copy(a, 4 + j, (*chip, c), sibling)
                fwd.start()
                passed.append(fwd)
        for a in range(na):
            copy(a, 0, sibling, me).wait_recv()
            for j, chip in enumerate(chips):
                copy(a, 4 + j, (*chip, 1 - c), me).wait_recv()
        for cp in first + passed:
            cp.wait_send()
        for cp in mine:
            cp.wait()

    out_shape = [jax.ShapeDtypeStruct(s.shape[:-2] + (N_DEV * s.shape[-2], s.shape[-1]), s.dtype) for s in shards]
    return pl.pallas_call(
        body, name="weights_all_gather",
        in_specs=[ANY] * na, out_specs=[ANY] * na, out_shape=out_shape,
        scratch_shapes=[pltpu.SemaphoreType.DMA((na, 7)), pltpu.SemaphoreType.DMA((na, 7)),
                        pltpu.SemaphoreType.DMA((na,))],
    )(*shards)


def _split_copies(gather, srcs, lands, send_sems, recv_sems, ns):
    x, y, c = _me()
    me = (x, y, c)
    my_slot = 4 * x + 2 * y + c
    copies = []
    for mask in range(1, N_DEV):
        peer = (x ^ (mask >> 2), y ^ ((mask >> 1) & 1), c ^ (mask & 1))
        for a in range(len(srcs)):
            if gather:
                src, dst = srcs[a], _block_rows(lands[a], me, ns[a])
            else:
                src, dst = _block_rows(srcs[a], peer, ns[a]), lands[a].at[my_slot]
            sem = a * (N_DEV - 1) + mask - 1
            copies.append(pltpu.make_async_remote_copy(
                src_ref=src, dst_ref=dst, send_sem=send_sems.at[sem], recv_sem=recv_sems.at[sem],
                device_id=peer, device_id_type=MESH))
    return copies


HBM_SPEC = pl.BlockSpec(memory_space=pltpu.HBM)
SEM_SPEC = pl.BlockSpec(memory_space=pltpu.SEMAPHORE)


def _split_start(gather, srcs, name, dep=None):
    na = len(srcs)
    x, y, c = _me()
    mine = 4 * x + 2 * y + c
    if gather:
        ns = [s.shape[-2] for s in srcs]
        lands = [lax.dynamic_update_slice(
            lax.empty(s.shape[:-2] + (N_DEV * s.shape[-2], s.shape[-1]), s.dtype), s,
            (0,) * (s.ndim - 2) + (mine * s.shape[-2], 0)) for s in srcs]
    else:
        ns = [s.shape[-2] // N_DEV for s in srcs]
        lands = [lax.dynamic_update_slice(
            lax.empty((N_DEV, n, s.shape[-1]), s.dtype),
            lax.dynamic_slice(s, (mine * n, 0), (n, s.shape[-1]))[None], (mine, 0, 0)) for s, n in zip(srcs, ns)]
    n_in = 2 * na + (dep is not None)

    def body(*refs):
        send_sems, recv_sems = refs[n_in], refs[n_in + 1]
        for cp in _split_copies(gather, refs[:na], refs[na:2 * na], send_sems, recv_sems, ns):
            cp.start()
        refs[-1][...] = jnp.zeros_like(refs[-1])

    hbm = lambda a: pltpu.with_memory_space_constraint(a, pltpu.HBM)
    args = [hbm(a) for a in srcs] + [hbm(a) for a in lands] + ([dep] if dep is not None else [])
    out = pl.pallas_call(
        body, name=name,
        in_specs=[HBM_SPEC] * (2 * na) + ([ANY] if dep is not None else []),
        out_specs=[SEM_SPEC, SEM_SPEC] + [HBM_SPEC] * (2 * na) + [pl.BlockSpec(memory_space=pltpu.VMEM)],
        out_shape=[pltpu.SemaphoreType.DMA((na * (N_DEV - 1),)), pltpu.SemaphoreType.DMA((na * (N_DEV - 1),))]
        + [pltpu.HBM(a.shape, a.dtype) for a in srcs + lands] + [jax.ShapeDtypeStruct((8, LANES), F32)],
        input_output_aliases={i: 2 + i for i in range(2 * na)},
        compiler_params=pltpu.CompilerParams(has_side_effects=pltpu.SideEffectType.DATAFLOW_SIDE_EFFECTING),
    )(*args)
    return (gather, ns, out[0], out[1], list(out[2:2 + na]), list(out[2 + na:2 + 2 * na])), out[-1]


def _split_wait(handle, after, name):
    gather, ns, send, recv, srcs, lands = handle
    na = len(srcs)

    def body(*refs):
        send_sems, recv_sems = refs[2 * na], refs[2 * na + 1]
        for cp in _split_copies(gather, refs[:na], refs[na:2 * na], send_sems, recv_sems, ns):
            cp.wait_send()
            cp.wait_recv()

    out = pl.pallas_call(
        body, name=name,
        in_specs=[HBM_SPEC] * (2 * na) + [SEM_SPEC, SEM_SPEC, ANY],
        out_specs=[HBM_SPEC] * (2 * na),
        out_shape=[pltpu.HBM(a.shape, a.dtype) for a in srcs + lands],
        input_output_aliases={i: i for i in range(2 * na)},
        compiler_params=pltpu.CompilerParams(has_side_effects=pltpu.SideEffectType.DATAFLOW_SIDE_EFFECTING),
    )(*srcs, *lands, send, recv, after)
    return list(out[:na]), list(out[na:])


def _sum_blocks(parts):
    n, R, C = parts.shape

    def body(p_ref, o_ref):
        acc = p_ref[0]
        for k in range(1, n):
            acc = acc + p_ref[k]
        o_ref[...] = acc

    return pl.pallas_call(
        body, name="small_sum",
        in_specs=[pl.BlockSpec(memory_space=pltpu.VMEM)], out_specs=pl.BlockSpec(memory_space=pltpu.VMEM),
        out_shape=jax.ShapeDtypeStruct((R, C), F32),
        compiler_params=pltpu.CompilerParams(vmem_limit_bytes=VMEM_LIMIT),
    )(parts)


def _pack(arrays):
    flat = jnp.concatenate([a.reshape(-1) for a in arrays])
    pad = (-flat.shape[0]) % (8 * LANES)
    return jnp.pad(flat, (0, pad)).reshape(-1, LANES)


def _unpack(buf, shapes):
    flat = buf.reshape(-1)
    out, off = [], 0
    for s in shapes:
        n = 1
        for d in s:
            n *= d
        out.append(flat[off:off + n].reshape(s))
        off += n
    return out


def _ffn_index(layer, second):
    return (2 * layer + second) * 3


def kernel(x, g_ffn1, w_ffn1_gate, w_ffn1_up, w_ffn1_down, g_mix, w_in_ab, conv_w, conv_b, ln_a_g, ln_a_b, ln_v_g, ln_v_b, sp_w, sp_b, w_out_ab, w_qkv, w_o, g_ffn2, w_ffn2_gate, w_ffn2_up, w_ffn2_down, g_final, loss_target, m_g_ffn1, m_w_ffn1_gate, m_w_ffn1_up, m_w_ffn1_down, m_g_mix, m_w_in_ab, m_conv_w, m_conv_b, m_ln_a_g, m_ln_a_b, m_ln_v_g, m_ln_v_b, m_sp_w, m_sp_b, m_w_out_ab, m_w_qkv, m_w_o, m_g_ffn2, m_w_ffn2_gate, m_w_ffn2_up, m_w_ffn2_down, m_g_final, v_g_ffn1, v_w_ffn1_gate, v_w_ffn1_up, v_w_ffn1_down, v_g_mix, v_w_in_ab, v_conv_w, v_conv_b, v_ln_a_g, v_ln_a_b, v_ln_v_g, v_ln_v_b, v_sp_w, v_sp_b, v_w_out_ab, v_w_qkv, v_w_o, v_g_ffn2, v_w_ffn2_gate, v_w_ffn2_up, v_w_ffn2_down, v_g_final):
    n_seq, S, D = x.shape
    T = n_seq * S
    depth = g_ffn1.shape[0]
    assert depth == 2 and D == D_MODEL
    my_block = 4 * lax.axis_index("x") + 2 * lax.axis_index("y") + lax.axis_index("c")

    ffn_parts = []
    for l in range(depth):
        for gate, up, down in ((w_ffn1_gate, w_ffn1_up, w_ffn1_down), (w_ffn2_gate, w_ffn2_up, w_ffn2_down)):
            ffn_parts += [gate[l].T, up[l].T, down[l]]
    ffn_shard = lambda k: jnp.stack(ffn_parts[3 * k:3 * k + 3]).astype(BF16)
    conv_w_pad = jnp.zeros((HALO, conv_w.shape[2]), F32).at[:CONV_WIDTH].set(conv_w[0]).T
    w_ffn = [None] * (2 * depth)
    w_ffn[0], conv_w_t = _all_gather([ffn_shard(0), conv_w_pad])
    conv_w_full = conv_w_t.T[:CONV_WIDTH]
    shards_b = [w_out_ab[0].astype(BF16), ffn_shard(1)]
    shards_d = [w_qkv[0].T.astype(BF16), w_o[0].astype(BF16), ffn_shard(3)]
    gather_a, token = _split_start(True, [w_in_ab[0].T.astype(BF16)], "gather_a_start", dep=conv_w_t)
    gather_b, token = _split_start(True, shards_b, "gather_b_start", dep=token)
    gather_c, token = _split_start(True, [ffn_shard(2)], "gather_c_start", dep=token)
    gather_d, token = _split_start(True, shards_d, "gather_d_start", dep=token)

    def gathered(handle, after, name):
        return _split_wait(handle, after, name)[1]

    row = lambda a: a.reshape(1, -1)
    tril = jnp.tril(jnp.ones((CHUNK, CHUNK), dtype=bool))
    ws = jnp.where(tril[None], sp_w[0], 0.0).astype(BF16)
    ws_t = jnp.swapaxes(ws, 1, 2)
    bias2d = jnp.repeat(sp_b[0].T, DB, axis=1)
    conv_b2, lag, lab = row(conv_b[0]), row(ln_a_g[0]), row(ln_a_b[0])
    lvg, lvb = row(ln_v_g[0]), row(ln_v_b[0])

    x0 = x.reshape(T, D)
    target = loss_target.reshape(T, D)
    saved = []
    xc = x0
    for l in range(depth):
        xa, a1, b1, h1 = _ffn_fwd(xc, row(g_ffn1[l]), w_ffn[2 * l], 0, f"ffn1_fwd_{l}", dep=token)
        if l % 2 == 0:
            w_in_t, = gathered(gather_a, xa, "gather_a_wait")
            z, hm = _mm_nt(xa, w_in_t, row(g_mix[l]), F32, "mix_in_proj")
            cat, conv_out = _mix_fwd(z, conv_w_full, conv_b2, lag, lab, lvg, lvb, ws, bias2d, n_seq)
            w_out, w_ffn[1] = gathered(gather_b, cat, "gather_b_wait")
            xb = _mm_nn_res(cat, w_out, xa, "mix_out_proj")
            mixer = (z, hm, cat, conv_out)
        else:
            w_qkv_t, w_o_full, w_ffn[3] = gathered(gather_d, xa, "gather_d_wait")
            qkv, hm = _mm_nt(xa, w_qkv_t, row(g_mix[l]), BF16, "qkv_proj")
            o, att = _attn_fwd(qkv, n_seq)
            xb = _mm_nn_res(o, w_o_full, xa, "attn_out_proj")
            mixer = (qkv, hm, o, att)
        xn, a2, b2, h2 = _ffn_fwd(xb, row(g_ffn2[l]), w_ffn[2 * l + 1], 0, f"ffn2_fwd_{l}")
        saved.append((xc, a1, b1, h1, xa, mixer, xb, a2, b2, h2))
        xc = xn
        if l == 0:
            w_ffn[2], = gathered(gather_c, xc, "gather_c_wait")

    g, loss_part, dg_final = _loss_head(xc, row(g_final), target)

    dg_ffn1, dg_ffn2, dg_mix = [None] * depth, [None] * depth, [None] * depth
    exchanges = {}
    token = None

    def ffn_back(g, xin, gvec, a, b, h, k, tag, token):
        g, dg, da, db, s, gh = _ffn_bwd(g, xin, gvec, a, b, w_ffn[k], 0, f"ffn{tag}_bwd", dep=token)
        if k == 0:
            return g, dg, (da, db, s, gh, h)
        dws = [_mm_tn(da, h, f"dw_gate{tag}"), _mm_tn(db, h, f"dw_up{tag}"), _mm_tn(s, gh, f"dw_down{tag}")]
        exchanges[f"ffn{k}"], token = _split_start(False, dws, f"exchange_ffn{tag}_start")
        return g, dg, token

    for l in reversed(range(depth)):
        xin, a1, b1, h1, xa, mixer, xb, a2, b2, h2 = saved[l]
        g, dg_ffn2[l], token = ffn_back(g, xb, row(g_ffn2[l]), a2, b2, h2, 2 * l + 1, f"2_{l}", token)
        if l % 2 == 0:
            z, hm, cat, conv_out = mixer
            dcat = _mm_nt(g, w_out, None, F32, "mix_out_bwd", dep=token)
            d_w_out = _mm_tn(cat, g, "dw_out")
            (da1, dz_uv, d_lag, d_lab, d_lvg, d_lvb, d_ws, d_sb) = _mix_bwd_rows(
                dcat, z, conv_out, lag, lab, lvg, lvb, ws, ws_t, bias2d, n_seq)
            dz_a, d_cw, d_cb = _mix_bwd_conv(da1, z, conv_w_full, n_seq)
            d_w_in_t = jnp.concatenate([_mm_tn(dz_a, hm, "dw_in_a"), _mm_tn(dz_uv, hm, "dw_in_uv")])
            exchanges["mix"], token = _split_start(False, [d_w_out, d_w_in_t], "exchange_mix_start")
            g, dg_mix[l] = _mm_nn_rmsbwd([dz_a, dz_uv], w_in_t, xa, row(g_mix[l]), g, "mix_in_bwd", dep=token)
        else:
            qkv, hm, o, att = mixer
            do = _mm_nt(g, w_o_full, None, BF16, "attn_out_bwd", dep=token)
            d_w_o = _mm_tn(o, g, "dw_o")
            dq, dk, dv = _attn_bwd(qkv, do, att, n_seq)
            d_w_qkv_t = jnp.concatenate([_mm_tn(dq, hm, "dw_q"), _mm_tn(dk, hm, "dw_k"), _mm_tn(dv, hm, "dw_v")])
            exchanges["attn"], token = _split_start(False, [d_w_o, d_w_qkv_t], "exchange_attn_start")
            g, dg_mix[l] = _mm_nn_rmsbwd([dq, dk, dv], w_qkv_t, xa, row(g_mix[l]), g, "qkv_bwd", dep=token)
        g, dg_ffn1[l], token = ffn_back(g, xin, row(g_ffn1[l]), a1, b1, h1, 2 * l, f"1_{l}", token)
    grad_x = g.reshape(n_seq, S, D)

    small = [jnp.concatenate(dg_ffn1), jnp.concatenate(dg_mix), d_cw, d_cb, d_lag, d_lab, d_lvg, d_lvb,
             jnp.where(tril[None], d_ws, 0.0), d_sb.T, jnp.concatenate(dg_ffn2), dg_final, loss_part[:, :1]]
    small_shapes = [(depth, D), (depth, D), (CONV_WIDTH, CA), (1, CA), (1, CA), (1, CA), (1, GB, DB), (1, GB, DB),
                    (1, GB, CHUNK, CHUNK), (1, GB, CHUNK), (depth, D), (D,), ()]
    da, db, s, gh, h = token
    small_gather, token = _split_start(True, [_pack(small)], "small_gather_start")

    for which, lhs, rhs in ((2, s, gh), (1, db, h), (0, da, h)):
        dw = _mm_tn(lhs, rhs, f"dw_ffn0_{which}", dep=token, out_dtype=BF16)
        exchanges[f"ffn0_{which}"], token = _split_start(False, [dw], f"exchange_ffn0_{which}_start")

    small_all, = _split_wait(small_gather, token, "small_gather_wait")[1]
    red = _unpack(_sum_blocks(small_all.reshape(N_DEV, -1, LANES)), small_shapes)
    (gr_g_ffn1, gr_g_mix, gr_cw_full, gr_cb, gr_lag, gr_lab, gr_lvg, gr_lvb, gr_sp_w, gr_sp_b,
     gr_g_ffn2, gr_g_final, loss) = red
    n_cw = conv_w.shape[2]
    gr_cw = lax.dynamic_slice(gr_cw_full, (0, my_block * n_cw), (CONV_WIDTH, n_cw))[None]

    def landed(key, after):
        return _split_wait(exchanges[key], after, f"exchange_{key}_wait")[1]

    parts_ffn = [None] * (6 * depth)
    for k in range(1, 2 * depth):
        parts_ffn[3 * k:3 * k + 3] = landed(f"ffn{k}", token)
    parts_out, parts_in = landed("mix", token)
    parts_o, parts_qkv = landed("attn", token)

    grads = {
        "g_ffn1": gr_g_ffn1, "g_mix": gr_g_mix, "conv_w": gr_cw, "conv_b": gr_cb, "ln_a_g": gr_lag,
        "ln_a_b": gr_lab, "ln_v_g": gr_lvg, "ln_v_b": gr_lvb, "sp_w": gr_sp_w, "sp_b": gr_sp_b,
        "g_ffn2": gr_g_ffn2, "g_final": gr_g_final,
    }
    weights = dict(g_ffn1=g_ffn1, w_ffn1_gate=w_ffn1_gate, w_ffn1_up=w_ffn1_up, w_ffn1_down=w_ffn1_down, g_mix=g_mix,
                   w_in_ab=w_in_ab, conv_w=conv_w, conv_b=conv_b, ln_a_g=ln_a_g, ln_a_b=ln_a_b, ln_v_g=ln_v_g,
                   ln_v_b=ln_v_b, sp_w=sp_w, sp_b=sp_b, w_out_ab=w_out_ab, w_qkv=w_qkv, w_o=w_o, g_ffn2=g_ffn2,
                   w_ffn2_gate=w_ffn2_gate, w_ffn2_up=w_ffn2_up, w_ffn2_down=w_ffn2_down, g_final=g_final)
    m_in = dict(g_ffn1=m_g_ffn1, w_ffn1_gate=m_w_ffn1_gate, w_ffn1_up=m_w_ffn1_up, w_ffn1_down=m_w_ffn1_down,
                g_mix=m_g_mix, w_in_ab=m_w_in_ab, conv_w=m_conv_w, conv_b=m_conv_b, ln_a_g=m_ln_a_g, ln_a_b=m_ln_a_b,
                ln_v_g=m_ln_v_g, ln_v_b=m_ln_v_b, sp_w=m_sp_w, sp_b=m_sp_b, w_out_ab=m_w_out_ab, w_qkv=m_w_qkv,
                w_o=m_w_o, g_ffn2=m_g_ffn2, w_ffn2_gate=m_w_ffn2_gate, w_ffn2_up=m_w_ffn2_up,
                w_ffn2_down=m_w_ffn2_down, g_final=m_g_final)
    v_in = dict(g_ffn1=v_g_ffn1, w_ffn1_gate=v_w_ffn1_gate, w_ffn1_up=v_w_ffn1_up, w_ffn1_down=v_w_ffn1_down,
                g_mix=v_g_mix, w_in_ab=v_w_in_ab, conv_w=v_conv_w, conv_b=v_conv_b, ln_a_g=v_ln_a_g, ln_a_b=v_ln_a_b,
                ln_v_g=v_ln_v_g, ln_v_b=v_ln_v_b, sp_w=v_sp_w, sp_b=v_sp_b, w_out_ab=v_w_out_ab, w_qkv=v_w_qkv,
                w_o=v_w_o, g_ffn2=v_g_ffn2, w_ffn2_gate=v_w_ffn2_gate, w_ffn2_up=v_w_ffn2_up,
                w_ffn2_down=v_w_ffn2_down, g_final=v_g_final)
    names = list(weights)
    grads = {n: grads[n].reshape(weights[n].shape) for n in grads}

    delta, new_m, new_v = {}, {}, {}

    def adamw_big(n, parts):
        if weights[n].shape[-1] == D:
            view = back = lambda a: a
        else:
            view = back = lambda a: jnp.swapaxes(a, 1, 2)
        out = _adamw_parts(parts, view(weights[n]), view(m_in[n]), view(v_in[n]), f"adamw_{n}")
        grads[n], delta[n], new_m[n], new_v[n] = [back(a) for a in out]

    adamw_big("w_in_ab", [parts_in])
    adamw_big("w_out_ab", [parts_out])
    adamw_big("w_qkv", [parts_qkv])
    adamw_big("w_o", [parts_o])
    kinds = ("gate", "up", "down")
    for which, kind in enumerate(kinds):
        adamw_big(f"w_ffn2_{kind}", [parts_ffn[_ffn_index(l, 1) + which] for l in range(depth)])
    big = [n for n in names if n.startswith("w_")]
    after = jnp.concatenate([delta[n].reshape(-1)[:1] for n in big if n in delta]).reshape(1, -1)
    for which in (2, 1, 0):
        parts_ffn[which], = landed(f"ffn0_{which}", after)
    for which, kind in enumerate(kinds):
        adamw_big(f"w_ffn1_{kind}", [parts_ffn[_ffn_index(l, 0) + which] for l in range(depth)])
    little = [n for n in names if n not in big]
    shapes = [weights[n].shape for n in little]
    d, nm, nv = _adamw(_pack([weights[n] for n in little]), _pack([grads[n] for n in little]),
                       _pack([m_in[n] for n in little]), _pack([v_in[n] for n in little]), "adamw_small")
    for n, dd, mm, vv in zip(little, _unpack(d, shapes), _unpack(nm, shapes), _unpack(nv, shapes)):
        delta[n], new_m[n], new_v[n] = dd, mm, vv

    return (loss, grad_x, *[grads[n] for n in names], *[delta[n] for n in names],
            *[new_m[n] for n in names], *[new_v[n] for n in names])
```

```python
import jax
import jax.numpy as jnp
from jax import lax
from jax.experimental import pallas as pl
from jax.experimental.pallas import tpu as pltpu

F32 = jnp.float32
BF16 = jnp.bfloat16

D_MODEL = 1024
CA = 512
CB = 512
GB = 4
DB = 128
CHUNK = 128
CONV_WIDTH = 31
N_HEADS = 16
HEAD_DIM = 64
EPS = 1e-6
N_DEV = 8
LANES = 128
SUBLANES = 8
QB = 128
ATT_TQ = 1024
FFN_TN = 2816
FFN_TM = 256
HALO = 32
CONV_ROWS = 32
ATT_SCALE = HEAD_DIM ** -0.5

ADAM_LR = 0.001
ADAM_B1 = 0.9
ADAM_B2 = 0.999
ADAM_EPS = 1e-08
ADAM_WD = 0.01
ADAM_STEP = 10

NT = (((1,), (1,)), ((), ()))
NN = (((1,), (0,)), ((), ()))
TN = (((0,), (0,)), ((), ()))
MESH = pl.DeviceIdType.MESH
ANY = pl.BlockSpec(memory_space=pl.ANY)
VMEM_LIMIT = 60 * 1024 * 1024


def _dot(a, b, dims):
    return lax.dot_general(a, b, dims, preferred_element_type=F32)


def _cp(*sem):
    return pltpu.CompilerParams(dimension_semantics=sem, vmem_limit_bytes=VMEM_LIMIT)


def _pcall(body, *, in_specs, args, dep=None, **kw):
    if dep is not None:
        n_in = len(in_specs)
        inner = body

        def body(*refs):
            inner(*refs[:n_in], *refs[n_in + 1:])

        in_specs = list(in_specs) + [ANY]
        args = tuple(args) + (dep,)
    return pl.pallas_call(body, in_specs=list(in_specs), **kw)(*args)


def _tile(n, want):
    if n <= want:
        return n
    t = want - want % LANES
    while t > LANES and n % t:
        t -= LANES
    assert n % t == 0, (n, want)
    return t


def _sigmoid(x):
    return 0.5 * jnp.tanh(0.5 * x) + 0.5


def _rstd(x):
    return lax.rsqrt(jnp.mean(x * x, axis=-1, keepdims=True) + EPS)


def _rms_bwd(x, g, dh):
    r = _rstd(x)
    u = dh * g
    dx = r * (u - x * (r * r) * jnp.mean(u * x, axis=-1, keepdims=True))
    dg = jnp.sum(dh * x * r, axis=0, keepdims=True)
    return dx, dg


def _ln_fwd(x, g, b):
    mu = jnp.mean(x, axis=-1, keepdims=True)
    xc = x - mu
    r = lax.rsqrt(jnp.mean(xc * xc, axis=-1, keepdims=True) + EPS)
    xh = xc * r
    return xh * g + b, xh, r


def _ln_bwd(dy, xh, r, g):
    dxh = dy * g
    return r * (dxh - jnp.mean(dxh, axis=-1, keepdims=True)
                - xh * jnp.mean(dxh * xh, axis=-1, keepdims=True))


def _ffn_fwd(x, g, wall, base, name, dep=None):
    T, D = x.shape
    F = wall.shape[1]
    tm, tn = _tile(T, FFN_TM), _tile(F, FFN_TN)
    n_j = F // tn

    def body(x_ref, g_ref, wg_ref, wu_ref, wd_ref, xo_ref, a_ref, b_ref, h_ref, acc_ref):
        j = pl.program_id(1)

        @pl.when(j == 0)
        def _():
            xv = x_ref[...]
            h_ref[...] = (xv * _rstd(xv) * g_ref[...]).astype(BF16)
            acc_ref[...] = jnp.zeros_like(acc_ref)

        h = h_ref[...]
        a = _dot(h, wg_ref[...], NT)
        b = _dot(h, wu_ref[...], NT)
        a_ref[...] = a.astype(BF16)
        b_ref[...] = b.astype(BF16)
        s = (a * _sigmoid(a) * b).astype(BF16)
        acc_ref[...] += _dot(s, wd_ref[...], NN)

        @pl.when(j == n_j - 1)
        def _():
            xo_ref[...] = x_ref[...] + 0.5 * acc_ref[...]

    single = pl.Buffered(1) if n_j == 1 else None
    wspec = lambda k: pl.BlockSpec((None, tn, D), lambda i, j: (base + k, j, 0), pipeline_mode=single)
    return _pcall(
        body, name=name, grid=(T // tm, n_j), dep=dep, args=(x, g, wall, wall, wall),
        in_specs=[pl.BlockSpec((tm, D), lambda i, j: (i, 0)), pl.BlockSpec((1, D), lambda i, j: (0, 0)),
                  wspec(0), wspec(1), wspec(2)],
        out_specs=[pl.BlockSpec((tm, D), lambda i, j: (i, 0)), pl.BlockSpec((tm, tn), lambda i, j: (i, j)),
                   pl.BlockSpec((tm, tn), lambda i, j: (i, j)), pl.BlockSpec((tm, D), lambda i, j: (i, 0))],
        out_shape=[jax.ShapeDtypeStruct((T, D), F32), jax.ShapeDtypeStruct((T, F), BF16),
                   jax.ShapeDtypeStruct((T, F), BF16), jax.ShapeDtypeStruct((T, D), BF16)],
        scratch_shapes=[pltpu.VMEM((tm, D), F32)],
        compiler_params=_cp("parallel", "arbitrary"),
    )


def _ffn_bwd(go, x, g, a, b, wall, base, name, dep=None):
    T, D = x.shape
    F = wall.shape[1]
    tm, tn = _tile(T, FFN_TM), _tile(F, FFN_TN)
    n_j = F // tn

    def body(go_ref, x_ref, g_ref, a_ref, b_ref, wg_ref, wu_ref, wd_ref,
             gx_ref, dg_ref, da_ref, db_ref, s_ref, gh_ref, acc_ref):
        i, j = pl.program_id(0), pl.program_id(1)

        @pl.when(j == 0)
        def _():
            gh_ref[...] = (0.5 * go_ref[...]).astype(BF16)
            acc_ref[...] = jnp.zeros_like(acc_ref)

        @pl.when((i == 0) & (j == 0))
        def _():
            dg_ref[...] = jnp.zeros_like(dg_ref)

        ds = _dot(gh_ref[...], wd_ref[...], NT)
        av = a_ref[...].astype(F32)
        bv = b_ref[...].astype(F32)
        sig = _sigmoid(av)
        sl = av * sig
        dab = ((ds * bv) * (sig + sl * (1.0 - sig))).astype(BF16)
        dbb = (ds * sl).astype(BF16)
        s_ref[...] = (sl * bv).astype(BF16)
        da_ref[...] = dab
        db_ref[...] = dbb
        acc_ref[...] += _dot(dab, wg_ref[...], NN) + _dot(dbb, wu_ref[...], NN)

        @pl.when(j == n_j - 1)
        def _():
            dx, dg = _rms_bwd(x_ref[...], g_ref[...], acc_ref[...])
            gx_ref[...] = go_ref[...] + dx
            dg_ref[...] += dg

    single = pl.Buffered(1) if n_j == 1 else None
    wspec = lambda k: pl.BlockSpec((None, tn, D), lambda i, j: (base + k, j, 0), pipeline_mode=single)
    row = pl.BlockSpec((tm, D), lambda i, j: (i, 0))
    hid = pl.BlockSpec((tm, tn), lambda i, j: (i, j))
    vec = pl.BlockSpec((1, D), lambda i, j: (0, 0))
    return _pcall(
        body, name=name, grid=(T // tm, n_j), dep=dep, args=(go, x, g, a, b, wall, wall, wall),
        in_specs=[row, row, vec, hid, hid, wspec(0), wspec(1), wspec(2)],
        out_specs=[row, vec, hid, hid, hid, row],
        out_shape=[jax.ShapeDtypeStruct((T, D), F32), jax.ShapeDtypeStruct((1, D), F32),
                   jax.ShapeDtypeStruct((T, F), BF16), jax.ShapeDtypeStruct((T, F), BF16),
                   jax.ShapeDtypeStruct((T, F), BF16), jax.ShapeDtypeStruct((T, D), BF16)],
        scratch_shapes=[pltpu.VMEM((tm, D), F32)],
        compiler_params=_cp("arbitrary", "arbitrary"),
    )


def _mm_tn(a, b, name, dep=None, out_dtype=F32):
    T, M = a.shape
    N = b.shape[1]
    tmm, tk = _tile(M, 1536), _tile(T, 2048)
    n_k = T // tk
    narrow = out_dtype != F32

    def body(a_ref, b_ref, o_ref, *scratch):
        acc_ref = scratch[0] if narrow else o_ref

        @pl.when(pl.program_id(1) == 0)
        def _():
            acc_ref[...] = jnp.zeros_like(acc_ref)

        acc_ref[...] += _dot(a_ref[...].astype(BF16), b_ref[...].astype(BF16), TN)
        if narrow:
            @pl.when(pl.program_id(1) == n_k - 1)
            def _():
                o_ref[...] = acc_ref[...].astype(out_dtype)

    return _pcall(
        body, name=name, grid=(M // tmm, n_k), dep=dep, args=(a, b),
        in_specs=[pl.BlockSpec((tk, tmm), lambda m, k: (k, m)), pl.BlockSpec((tk, N), lambda m, k: (k, 0))],
        out_specs=pl.BlockSpec((tmm, N), lambda m, k: (m, 0)),
        out_shape=jax.ShapeDtypeStruct((M, N), out_dtype),
        scratch_shapes=[pltpu.VMEM((tmm, N), F32)] if narrow else [],
        compiler_params=_cp("parallel", "arbitrary"),
    )


def _mm_nt(x, wt, g, out_dtype, name, dep=None):
    T, K = x.shape
    N = wt.shape[0]
    tm, tn = _tile(T, 512), N
    norm = g is not None

    def body(*refs):
        if norm:
            x_ref, g_ref, w_ref, o_ref, h_ref = refs
        else:
            x_ref, w_ref, o_ref, h_ref = refs

        @pl.when(pl.program_id(1) == 0)
        def _():
            xv = x_ref[...].astype(F32)
            if norm:
                xv = xv * _rstd(xv) * g_ref[...]
            h_ref[...] = xv.astype(BF16)

        o_ref[...] = _dot(h_ref[...], w_ref[...], NT).astype(out_dtype)

    row = pl.BlockSpec((tm, K), lambda i, j: (i, 0))
    wsp = pl.BlockSpec((tn, K), lambda i, j: (j, 0))
    osp = pl.BlockSpec((tm, tn), lambda i, j: (i, j))
    if norm:
        return pl.pallas_call(
            body, name=name, grid=(T // tm, N // tn),
            in_specs=[row, pl.BlockSpec((1, K), lambda i, j: (0, 0)), wsp],
            out_specs=[osp, row],
            out_shape=[jax.ShapeDtypeStruct((T, N), out_dtype), jax.ShapeDtypeStruct((T, K), BF16)],
            compiler_params=_cp("parallel", "arbitrary"),
        )(x, g, wt)
    return _pcall(
        body, name=name, grid=(T // tm, N // tn), dep=dep, args=(x, wt),
        in_specs=[row, wsp], out_specs=osp,
        out_shape=jax.ShapeDtypeStruct((T, N), out_dtype),
        scratch_shapes=[pltpu.VMEM((tm, K), BF16)],
        compiler_params=_cp("parallel", "arbitrary"),
    )


def _mm_nn_res(act, w, resid, name):
    T, K = act.shape
    D = w.shape[1]
    tm = _tile(T, 512)

    def body(a_ref, w_ref, r_ref, o_ref):
        o_ref[...] = r_ref[...] + _dot(a_ref[...].astype(BF16), w_ref[...], NN)

    return pl.pallas_call(
        body, name=name, grid=(T // tm,),
        in_specs=[pl.BlockSpec((tm, K), lambda i: (i, 0)), pl.BlockSpec((K, D), lambda i: (0, 0)),
                  pl.BlockSpec((tm, D), lambda i: (i, 0))],
        out_specs=pl.BlockSpec((tm, D), lambda i: (i, 0)),
        out_shape=jax.ShapeDtypeStruct((T, D), F32),
        compiler_params=_cp("parallel"),
    )(act, w, resid)


def _mm_nn_rmsbwd(acts, w, x, g, gprev, name, dep=None):
    T = acts[0].shape[0]
    ks = [a.shape[1] for a in acts]
    K, D = w.shape
    assert sum(ks) == K
    tm = _tile(T, 512)
    na = len(acts)

    def body(*refs):
        a_refs = refs[:na]
        w_ref, x_ref, g_ref, gp_ref, o_ref, dg_ref = refs[na:]

        @pl.when(pl.program_id(0) == 0)
        def _():
            dg_ref[...] = jnp.zeros_like(dg_ref)

        dh, off = None, 0
        for a_ref, k in zip(a_refs, ks):
            part = _dot(a_ref[...].astype(BF16), w_ref[off:off + k, :], NN)
            dh = part if dh is None else dh + part
            off += k
        dx, dg = _rms_bwd(x_ref[...], g_ref[...], dh)
        o_ref[...] = gp_ref[...] + dx
        dg_ref[...] += dg

    row = pl.BlockSpec((tm, D), lambda i: (i, 0))
    vec = pl.BlockSpec((1, D), lambda i: (0, 0))
    return _pcall(
        body, name=name, grid=(T // tm,), dep=dep, args=(*acts, w, x, g, gprev),
        in_specs=[pl.BlockSpec((tm, k), lambda i: (i, 0)) for k in ks]
        + [pl.BlockSpec((K, D), lambda i: (0, 0)), row, vec, row],
        out_specs=[row, vec],
        out_shape=[jax.ShapeDtypeStruct((T, D), F32), jax.ShapeDtypeStruct((1, D), F32)],
        compiler_params=_cp("arbitrary"),
    )


def _loss_head(x, g, target):
    T, D = x.shape
    tm = _tile(T, 512)

    def body(x_ref, g_ref, t_ref, dx_ref, loss_ref, dg_ref):
        @pl.when(pl.program_id(0) == 0)
        def _():
            loss_ref[...] = jnp.zeros_like(loss_ref)
            dg_ref[...] = jnp.zeros_like(dg_ref)

        xv = x_ref[...]
        gv = g_ref[...]
        e = xv * _rstd(xv) * gv - t_ref[...]
        per_tok = jnp.sum(e * e, axis=-1, keepdims=True) * (1.0 / D)
        loss_ref[...] += 0.5 * jnp.sum(per_tok, axis=0, keepdims=True)
        dx, dg = _rms_bwd(xv, gv, e * (1.0 / D))
        dx_ref[...] = dx
        dg_ref[...] += dg

    row = pl.BlockSpec((tm, D), lambda i: (i, 0))
    vec = pl.BlockSpec((1, D), lambda i: (0, 0))
    return pl.pallas_call(
        body, name="loss_head", grid=(T // tm,),
        in_specs=[row, vec, row],
        out_specs=[row, pl.BlockSpec((1, LANES), lambda i: (0, 0)), vec],
        out_shape=[jax.ShapeDtypeStruct((T, D), F32), jax.ShapeDtypeStruct((1, LANES), F32),
                   jax.ShapeDtypeStruct((1, D), F32)],
        compiler_params=_cp("arbitrary"),
    )(x, g, target)


def _log_gates(z):
    neg_abs = lax.bitcast_convert_type(lax.bitcast_convert_type(z, jnp.uint32) | jnp.uint32(0x80000000), F32)
    ls = jnp.minimum(z, 0.0) - jnp.log(1.0 + jnp.exp(neg_abs))
    return ls, ls - z


def _cumsum_mm(v, u2):
    hi = v.astype(BF16)
    lo = (v - hi.astype(F32)).astype(BF16)
    return _dot(jnp.concatenate([hi, lo], axis=1), u2, NN)


def _half_rowsum(v):
    n = v.shape[0]
    s0 = jnp.sum(v[:, :QB], axis=1, keepdims=True)
    s1 = jnp.sum(v[:, QB:], axis=1, keepdims=True)
    return jnp.concatenate([jnp.broadcast_to(s0, (n, QB)), jnp.broadcast_to(s1, (n, QB))], axis=1)


def _stack_heads(src_ref, dst_ref, n_blk):
    m0 = lax.broadcasted_iota(jnp.int32, (1, LANES), 1) < HEAD_DIM

    def fill(c, carry):
        blk = src_ref[pl.ds(pl.multiple_of(c * QB, QB), QB), :]
        zero = jnp.zeros_like(blk)
        dst_ref[c, 0:QB, :] = jnp.where(m0, blk, zero)
        dst_ref[c, QB:2 * QB, :] = jnp.where(m0, zero, blk)
        return carry

    lax.fori_loop(0, n_blk, fill, 0)


def _diag_mask(tq, j):
    n = tq - j * QB
    row = lax.broadcasted_iota(jnp.int32, (n, 2 * QB), 0)
    col = lax.broadcasted_iota(jnp.int32, (n, 2 * QB), 1)
    return (col & (QB - 1)) < row


def _tri_blockdiag(upper):
    r = lax.broadcasted_iota(jnp.int32, (2 * QB, 2 * QB), 0)
    c = lax.broadcasted_iota(jnp.int32, (2 * QB, 2 * QB), 1)
    same = (r // QB) == (c // QB)
    u = (same & ((r > c) if upper else (r < c))).astype(BF16)
    return jnp.concatenate([u, u], axis=0)


def _attn_tiles(T, n_seq):
    S = T // n_seq
    tq = ATT_TQ if S % ATT_TQ == 0 else 2 * QB
    assert S % tq == 0
    return S, tq, tq // QB, S // tq, S // QB


def _attn_fwd(qkv, n_seq):
    T = qkv.shape[0]
    S, tq, r, n_q, n_k = _attn_tiles(T, n_seq)
    n_p = D_MODEL // LANES
    n_steps = n_seq * n_p * n_q
    u_suffix = _tri_blockdiag(True)

    def body(q_ref, k_ref, v_ref, u_ref, o_ref, a_hbm, kk_ref, vv_ref, lr_s, acc_s, a_stage, sems):
        qi = pl.program_id(2)
        group = (pl.program_id(0) * n_p + pl.program_id(1)) * n_q + qi

        @pl.when(qi == 0)
        def _():
            _stack_heads(k_ref, kk_ref, n_k)
            _stack_heads(v_ref, vv_ref, n_k)

        u = u_ref[...]
        lr_s[...] = jnp.zeros_like(lr_s)
        acc_s[...] = jnp.zeros_like(acc_s)
        base = ((group // n_q) * (n_q * (n_q + 1) // 2) + (qi * (qi + 1)) // 2) % 2

        def saves(first_kj, half):
            return [pltpu.make_async_copy(a_stage.at[half, j], a_hbm.at[group, first_kj - j], sems.at[half])
                    for j in range(r)]

        @pl.when(group >= 2)
        def _():
            for cp in saves(0, base):
                cp.wait()

        a_stage[base] = jnp.zeros_like(a_stage[0])

        def step(kj, half, j, rows, q, mask, lr, acc):
            ls, lk = _log_gates(_dot(q, kk_ref[kj], NT))
            if mask is not None:
                lk = jnp.where(mask, lk, 0.0)
            a = jnp.exp(ls + _cumsum_mm(lk, u) + lr)
            if mask is not None:
                a = jnp.where(mask, a, 0.0)
            a = a.astype(BF16)
            a_stage[half, j, rows, :] = a
            return lr + _half_rowsum(lk), acc + _dot(a, vv_ref[kj], NN)

        last = (qi + 1) * r - 1
        for n in range(r):
            rows = slice((r - 1 - n) * QB, tq)
            lr, acc = step(last - n, base, n, rows, q_ref[rows, :] * ATT_SCALE, _diag_mask(tq, r - 1 - n),
                           lr_s[rows, :], acc_s[rows, :])
            lr_s[rows, :] = lr
            acc_s[rows, :] = acc
        for cp in saves(last, base):
            cp.start()

        q = q_ref[...] * ATT_SCALE

        def off(it, carry):
            half = (base + it + 1) % 2
            first = (qi - it) * r - 1
            for cp in saves(first, half):
                cp.wait()
            lr, acc = lr_s[...], acc_s[...]
            for j in range(r):
                lr, acc = step(first - j, half, j, slice(0, tq), q, None, lr, acc)
            lr_s[...] = lr
            acc_s[...] = acc
            for cp in saves(first, half):
                cp.start()
            return carry

        lax.fori_loop(0, qi, off, 0)

        @pl.when(group == n_steps - 1)
        def _():
            for cp in saves(0, (base + qi) % 2):
                cp.wait()
            if n_steps * n_q > 1:
                for cp in saves(0, (base + qi + 1) % 2):
                    cp.wait()

        o_ref[...] = acc_s[...].astype(BF16)

    return pl.pallas_call(
        body, name="attn_fwd", grid=(n_seq, n_p, n_q),
        in_specs=[pl.BlockSpec((tq, LANES), lambda b, p, qi: (b * n_q + qi, p)),
                  pl.BlockSpec((S, LANES), lambda b, p, qi: (b, n_p + p)),
                  pl.BlockSpec((S, LANES), lambda b, p, qi: (b, 2 * n_p + p)),
                  pl.BlockSpec((4 * QB, 2 * QB), lambda b, p, qi: (0, 0))],
        out_specs=[pl.BlockSpec((tq, LANES), lambda b, p, qi: (b * n_q + qi, p)), ANY],
        out_shape=[jax.ShapeDtypeStruct((T, D_MODEL), BF16),
                   jax.ShapeDtypeStruct((n_seq * n_p * n_q, n_k, tq, 2 * QB), BF16)],
        scratch_shapes=[pltpu.VMEM((n_k, 2 * QB, LANES), BF16), pltpu.VMEM((n_k, 2 * QB, LANES), BF16),
                        pltpu.VMEM((tq, 2 * QB), F32), pltpu.VMEM((tq, LANES), F32),
                        pltpu.VMEM((2, r, tq, 2 * QB), BF16), pltpu.SemaphoreType.DMA((2,))],
        compiler_params=_cp("arbitrary", "arbitrary", "arbitrary"),
    )(qkv, qkv, qkv, u_suffix)


def _attn_bwd(qkv, do, a_saved, n_seq):
    T = qkv.shape[0]
    S, tq, r, n_q, n_k = _attn_tiles(T, n_seq)
    n_p = D_MODEL // LANES
    n_steps = n_seq * n_p * n_q
    u_prefix = _tri_blockdiag(False)[:2 * QB]

    def body(q_ref, k_ref, v_ref, do_ref, u_ref, a_hbm, dq_ref, dk_out, dv_out,
             kk_ref, vv_ref, cg_s, dq_s, dk_ref, dv_ref, a_stage, sems):
        qi = pl.program_id(2)
        group = (pl.program_id(0) * n_p + pl.program_id(1)) * n_q + qi

        base = ((group // n_q) * (n_q * (n_q + 1) // 2) + (qi * (qi + 1)) // 2) % 2

        def fetches(grp, g, half):
            return [pltpu.make_async_copy(a_hbm.at[grp, g * r + j], a_stage.at[half, j], sems.at[half])
                    for j in range(r)]

        @pl.when(group == 0)
        def _():
            for cp in fetches(group, 0, 0):
                cp.start()

        @pl.when(qi == 0)
        def _():
            _stack_heads(k_ref, kk_ref, n_k)
            _stack_heads(v_ref, vv_ref, n_k)
            dk_ref[...] = jnp.zeros_like(dk_ref)
            dv_ref[...] = jnp.zeros_like(dv_ref)

        u = u_ref[...]
        cg_s[...] = jnp.zeros_like(cg_s)
        dq_s[...] = jnp.zeros_like(dq_s)

        def step(kj, half, j, rows, q, dov, mask, cg, dq):
            kk = kk_ref[kj]
            beta = _sigmoid(_dot(q, kk, NT))
            a = a_stage[half, j, rows, :]
            g = a.astype(F32) * _dot(dov, vv_ref[kj], NT)
            dz = g - (g + _dot(g.astype(BF16), u, NN) + cg) * beta
            if mask is not None:
                dz = jnp.where(mask, dz, 0.0)
            dz = dz.astype(BF16)
            keys = pl.ds(pl.multiple_of(kj * QB, QB), QB)
            dvt = _dot(dov_t[:, rows], a, NN)
            dv_ref[keys, :] += jnp.where(t0, dvt[:, :QB], dvt[:, QB:]).T
            dkt = _dot(q_t[:, rows], dz, NN)
            dk_ref[keys, :] += jnp.where(t0, dkt[:, :QB], dkt[:, QB:]).T
            return cg + _half_rowsum(g), dq + _dot(dz, kk, NN)

        q = q_ref[...] * ATT_SCALE
        dov = do_ref[...]
        q_t = q.astype(F32).T.astype(BF16)
        dov_t = dov.astype(F32).T.astype(BF16)
        t0 = lax.broadcasted_iota(jnp.int32, (LANES, 1), 0) < HEAD_DIM

        def off(it, carry):
            half = (base + it) % 2
            for cp in fetches(group, it, half):
                cp.wait()
            for cp in fetches(group, it + 1, 1 - half):
                cp.start()
            cg, dq = cg_s[...], dq_s[...]
            for j in range(r):
                cg, dq = step(it * r + j, half, j, slice(0, tq), q, dov, None, cg, dq)
            cg_s[...] = cg
            dq_s[...] = dq
            return carry

        lax.fori_loop(0, qi, off, 0)

        half = (base + qi) % 2
        for cp in fetches(group, qi, half):
            cp.wait()

        @pl.when(group < n_steps - 1)
        def _():
            for cp in fetches(group + 1, 0, 1 - half):
                cp.start()

        for j in range(r):
            rows = slice(j * QB, tq)
            cg, dq = step(qi * r + j, half, j, rows, q_ref[rows, :] * ATT_SCALE, do_ref[rows, :],
                          _diag_mask(tq, j), cg_s[rows, :], dq_s[rows, :])
            cg_s[rows, :] = cg
            dq_s[rows, :] = dq
        dq_ref[...] = (dq_s[...] * ATT_SCALE).astype(BF16)

        @pl.when(qi == n_q - 1)
        def _():
            dk_out[...] = dk_ref[...].astype(BF16)
            dv_out[...] = dv_ref[...].astype(BF16)

    qspec = pl.BlockSpec((tq, LANES), lambda b, p, qi: (b * n_q + qi, p))
    seq = lambda off: pl.BlockSpec((S, LANES), lambda b, p, qi: (b, off + p))
    return pl.pallas_call(
        body, name="attn_bwd", grid=(n_seq, n_p, n_q),
        in_specs=[qspec, seq(n_p), seq(2 * n_p), qspec,
                  pl.BlockSpec((2 * QB, 2 * QB), lambda b, p, qi: (0, 0)), ANY],
        out_specs=[qspec, seq(0), seq(0)],
        out_shape=[jax.ShapeDtypeStruct((T, D_MODEL), BF16)] * 3,
        scratch_shapes=[pltpu.VMEM((n_k, 2 * QB, LANES), BF16), pltpu.VMEM((n_k, 2 * QB, LANES), BF16),
                        pltpu.VMEM((tq, 2 * QB), F32), pltpu.VMEM((tq, LANES), F32),
                        pltpu.VMEM((S, LANES), F32), pltpu.VMEM((S, LANES), F32),
                        pltpu.VMEM((2, r, tq, 2 * QB), BF16), pltpu.SemaphoreType.DMA((2,))],
        compiler_params=_cp("arbitrary", "arbitrary", "arbitrary"),
    )(qkv, qkv, qkv, do, u_prefix, a_saved)


def _shifted_copies(sh_ref):
    rows = sh_ref.shape[1] - SUBLANES
    for s in range(1, SUBLANES):
        sh_ref[s, 0:rows, :] = sh_ref[0, s:s + rows, :]


def _shifted(sh_ref, start, n):
    s = start % SUBLANES
    return sh_ref[s, start - s:start - s + n, :]


def _glu_with_halo(av_ref, ag_ref, avh_ref, agh_ref, a0_s, first, ts):
    hal = avh_ref[...] * _sigmoid(agh_ref[...])
    a0_s[0, 0:HALO, :] = jnp.where(first, 0.0, hal)
    a0_s[0, HALO:HALO + ts, :] = av_ref[...] * _sigmoid(ag_ref[...])
    _shifted_copies(a0_s)


def _mix_specs(ts, n_r, with_left):
    blk = lambda c: pl.BlockSpec((ts, CA), lambda b, r: (b * n_r + r, c))
    per = ts // HALO
    left = lambda c: pl.BlockSpec((HALO, CA), lambda b, r: (jnp.maximum((b * n_r + r) * per - 1, 0), c))
    return blk, (left if with_left else None)


def _mix_fwd(z, conv_w, conv_b, ln_a_g, ln_a_b, ln_v_g, ln_v_b, ws, bias2d, n_seq):
    T = z.shape[0]
    S = T // n_seq
    ts = _tile(S, 512)
    n_r = S // ts
    shift = HALO - (CONV_WIDTH - 1)

    def body(av_ref, ag_ref, avh_ref, agh_ref, u_ref, v_ref, cw_ref, cb_ref, lag_ref, lab_ref,
             lvg_ref, lvb_ref, ws_ref, bias_ref, cat_ref, a1_ref, a0_s):
        _glu_with_halo(av_ref, ag_ref, avh_ref, agh_ref, a0_s, pl.program_id(1) == 0, ts)
        for rb in range(ts // CONV_ROWS):
            base = rb * CONV_ROWS
            acc = jnp.broadcast_to(cb_ref[...], (CONV_ROWS, CA))
            for k in range(CONV_WIDTH):
                acc = acc + cw_ref[k:k + 1, :] * _shifted(a0_s, base + shift + k, CONV_ROWS)
            a1_ref[base:base + CONV_ROWS, :] = acc
        y, _, _ = _ln_fwd(a1_ref[...], lag_ref[...], lab_ref[...])
        cat_ref[:, 0:CA] = (y * _sigmoid(y)).astype(BF16)
        for gi in range(GB):
            sl = slice(gi * DB, (gi + 1) * DB)
            v1, _, _ = _ln_fwd(v_ref[:, sl], lvg_ref[:, sl], lvb_ref[:, sl])
            v1 = v1.astype(BF16)
            for c in range(ts // CHUNK):
                rs = slice(c * CHUNK, (c + 1) * CHUNK)
                v2 = _dot(ws_ref[gi], v1[rs], NN) + bias_ref[:, sl]
                cat_ref[rs, CA + gi * DB:CA + (gi + 1) * DB] = (u_ref[rs, sl] * v2).astype(BF16)

    blk, left = _mix_specs(ts, n_r, True)
    vec = pl.BlockSpec((1, CA), lambda b, r: (0, 0))
    return pl.pallas_call(
        body, name="mix_fwd", grid=(n_seq, n_r),
        in_specs=[blk(0), blk(1), left(0), left(1), blk(2), blk(3),
                  pl.BlockSpec((CONV_WIDTH, CA), lambda b, r: (0, 0)), vec, vec, vec, vec, vec,
                  pl.BlockSpec((GB, CHUNK, CHUNK), lambda b, r: (0, 0, 0)),
                  pl.BlockSpec((CHUNK, CB), lambda b, r: (0, 0))],
        out_specs=[pl.BlockSpec((ts, CA + CB), lambda b, r: (b * n_r + r, 0)), blk(0)],
        out_shape=[jax.ShapeDtypeStruct((T, CA + CB), BF16), jax.ShapeDtypeStruct((T, CA), F32)],
        scratch_shapes=[pltpu.VMEM((SUBLANES, HALO + ts, CA), F32)],
        compiler_params=_cp("parallel", "parallel"),
    )(z, z, z, z, z, z, conv_w, conv_b, ln_a_g, ln_a_b, ln_v_g, ln_v_b, ws, bias2d)


def _mix_bwd_rows(dcat, z, a1, ln_a_g, ln_a_b, ln_v_g, ln_v_b, ws, ws_t, bias2d, n_seq):
    T = z.shape[0]
    S = T // n_seq
    ts = _tile(S, 512)
    n_r = S // ts

    def body(dc_ref, u_ref, v_ref, a1_ref, lag_ref, lab_ref, lvg_ref, lvb_ref, ws_ref, wst_ref, bias_ref,
             da1_ref, dz_ref, dlag_ref, dlab_ref, dlvg_ref, dlvb_ref, dws_ref, dsb_ref, dv1_s, dbias_s):
        first = (pl.program_id(0) == 0) & (pl.program_id(1) == 0)
        last = (pl.program_id(0) == n_seq - 1) & (pl.program_id(1) == n_r - 1)

        @pl.when(first)
        def _():
            for ref in (dlag_ref, dlab_ref, dlvg_ref, dlvb_ref, dws_ref, dbias_s):
                ref[...] = jnp.zeros_like(ref)

        lag = lag_ref[...]
        y, xh, r = _ln_fwd(a1_ref[...], lag, lab_ref[...])
        sig = _sigmoid(y)
        dy = dc_ref[:, 0:CA] * (sig * (1.0 + y * (1.0 - sig)))
        dlag_ref[...] += jnp.sum(dy * xh, axis=0, keepdims=True)
        dlab_ref[...] += jnp.sum(dy, axis=0, keepdims=True)
        da1_ref[...] = _ln_bwd(dy, xh, r, lag)

        tril = (lax.broadcasted_iota(jnp.int32, (CHUNK, CHUNK), 0)
                >= lax.broadcasted_iota(jnp.int32, (CHUNK, CHUNK), 1))
        for gi in range(GB):
            sl = slice(gi * DB, (gi + 1) * DB)
            lvg = lvg_ref[:, sl]
            v1, vh, vr = _ln_fwd(v_ref[:, sl], lvg, lvb_ref[:, sl])
            v1 = v1.astype(BF16)
            for c in range(ts // CHUNK):
                rs = slice(c * CHUNK, (c + 1) * CHUNK)
                v2 = _dot(ws_ref[gi], v1[rs], NN) + bias_ref[:, sl]
                dbo = dc_ref[rs, CA + gi * DB:CA + (gi + 1) * DB]
                dz_ref[rs, sl] = (dbo * v2).astype(BF16)
                dv2 = dbo * u_ref[rs, sl]
                dbias_s[:, sl] += dv2
                dv2b = dv2.astype(BF16)
                dws_ref[gi] += jnp.where(tril, _dot(dv2b, v1[rs], NT), 0.0)
                dv1_s[rs, :] = _dot(wst_ref[gi], dv2b, NN)
            dv1 = dv1_s[...]
            dlvg_ref[:, sl] += jnp.sum(dv1 * vh, axis=0, keepdims=True)
            dlvb_ref[:, sl] += jnp.sum(dv1, axis=0, keepdims=True)
            dz_ref[:, CB + gi * DB:CB + (gi + 1) * DB] = _ln_bwd(dv1, vh, vr, lvg).astype(BF16)

        @pl.when(last)
        def _():
            col = lax.broadcasted_iota(jnp.int32, (CHUNK, GB), 1)
            out = jnp.zeros((CHUNK, GB), F32)
            for gi in range(GB):
                s = jnp.sum(dbias_s[:, gi * DB:(gi + 1) * DB], axis=1, keepdims=True)
                out = out + jnp.where(col == gi, s, 0.0)
            dsb_ref[...] = out

    blk, _ = _mix_specs(ts, n_r, False)
    vec = pl.BlockSpec((1, CA), lambda b, r: (0, 0))
    mat = pl.BlockSpec((GB, CHUNK, CHUNK), lambda b, r: (0, 0, 0))
    wide = pl.BlockSpec((ts, CA + CB), lambda b, r: (b * n_r + r, 0))
    return pl.pallas_call(
        body, name="mix_bwd_rows", grid=(n_seq, n_r),
        in_specs=[wide, blk(2), blk(3), blk(0), vec, vec, vec, vec, mat, mat,
                  pl.BlockSpec((CHUNK, CB), lambda b, r: (0, 0))],
        out_specs=[blk(0), wide, vec, vec, vec, vec, mat, pl.BlockSpec((CHUNK, GB), lambda b, r: (0, 0))],
        out_shape=[jax.ShapeDtypeStruct((T, CA), F32), jax.ShapeDtypeStruct((T, 2 * CB), BF16)]
        + [jax.ShapeDtypeStruct((1, CA), F32)] * 4
        + [jax.ShapeDtypeStruct((GB, CHUNK, CHUNK), F32), jax.ShapeDtypeStruct((CHUNK, GB), F32)],
        scratch_shapes=[pltpu.VMEM((ts, DB), F32), pltpu.VMEM((CHUNK, CB), F32)],
        compiler_params=_cp("arbitrary", "arbitrary"),
    )(dcat, z, z, a1, ln_a_g, ln_a_b, ln_v_g, ln_v_b, ws, ws_t, bias2d)


def _mix_bwd_conv(da1, z, conv_w, n_seq):
    T = z.shape[0]
    S = T // n_seq
    ts = _tile(S, 512)
    n_r = S // ts
    per = ts // HALO
    shift = HALO - (CONV_WIDTH - 1)
    fold = CONV_ROWS // 8

    def body(d_ref, dh_ref, av_ref, ag_ref, avh_ref, agh_ref, cw_ref,
             dz_ref, dcw_ref, dcb_ref, a0_s, d1_s, da0_s, dw8_s):
        first = (pl.program_id(0) == 0) & (pl.program_id(1) == 0)
        last = (pl.program_id(0) == n_seq - 1) & (pl.program_id(1) == n_r - 1)

        @pl.when(first)
        def _():
            dw8_s[...] = jnp.zeros_like(dw8_s)
            dcb_ref[...] = jnp.zeros_like(dcb_ref)

        _glu_with_halo(av_ref, ag_ref, avh_ref, agh_ref, a0_s, pl.program_id(1) == 0, ts)
        d1_s[0, 0:ts, :] = d_ref[...]
        d1_s[0, ts:ts + HALO, :] = jnp.where(pl.program_id(1) == n_r - 1, 0.0, dh_ref[...])
        _shifted_copies(d1_s)
        dcb_ref[...] += jnp.sum(d_ref[...], axis=0, keepdims=True)
        for rb in range(ts // CONV_ROWS):
            base = rb * CONV_ROWS
            dcur = d1_s[0, base:base + CONV_ROWS, :]
            acc = jnp.zeros((CONV_ROWS, CA), F32)
            for k in range(CONV_WIDTH):
                back = CONV_WIDTH - 1 - k
                acc = acc + cw_ref[k:k + 1, :] * _shifted(d1_s, base + back, CONV_ROWS)
                prod = dcur * _shifted(a0_s, base + shift + k, CONV_ROWS)
                part = prod[0:8]
                for f in range(1, fold):
                    part = part + prod[8 * f:8 * f + 8]
                dw8_s[k] += part
            da0_s[base:base + CONV_ROWS, :] = acc
        da0 = da0_s[...]
        sig = _sigmoid(ag_ref[...])
        dz_ref[:, 0:CA] = (da0 * sig).astype(BF16)
        dz_ref[:, CA:2 * CA] = (da0 * av_ref[...] * sig * (1.0 - sig)).astype(BF16)

        @pl.when(last)
        def _():
            for k in range(CONV_WIDTH):
                dcw_ref[k:k + 1, :] = jnp.sum(dw8_s[k], axis=0, keepdims=True)

    blk, left = _mix_specs(ts, n_r, True)
    n_halo_blocks = T // HALO
    right = pl.BlockSpec((HALO, CA), lambda b, r: (jnp.minimum((b * n_r + r + 1) * per, n_halo_blocks - 1), 0))
    return pl.pallas_call(
        body, name="mix_bwd_conv", grid=(n_seq, n_r),
        in_specs=[blk(0), right, blk(0), blk(1), left(0), left(1),
                  pl.BlockSpec((CONV_WIDTH, CA), lambda b, r: (0, 0))],
        out_specs=[pl.BlockSpec((ts, 2 * CA), lambda b, r: (b * n_r + r, 0)),
                   pl.BlockSpec((CONV_WIDTH, CA), lambda b, r: (0, 0)), pl.BlockSpec((1, CA), lambda b, r: (0, 0))],
        out_shape=[jax.ShapeDtypeStruct((T, 2 * CA), BF16), jax.ShapeDtypeStruct((CONV_WIDTH, CA), F32),
                   jax.ShapeDtypeStruct((1, CA), F32)],
        scratch_shapes=[pltpu.VMEM((SUBLANES, HALO + ts, CA), F32), pltpu.VMEM((SUBLANES, ts + HALO, CA), F32),
                        pltpu.VMEM((ts, CA), F32), pltpu.VMEM((CONV_WIDTH, 8, CA), F32)],
        compiler_params=_cp("arbitrary", "arbitrary"),
    )(da1, da1, z, z, z, z, conv_w)


def _row_tile(R, want):
    t = min(R, want)
    t -= t % 8
    while t > 8 and R % t:
        t -= 8
    return t if t >= 8 and R % t == 0 else R


def _adam_step(w, g, m, v):
    nm = ADAM_B1 * m + (1.0 - ADAM_B1) * g
    nv = ADAM_B2 * v + (1.0 - ADAM_B2) * (g * g)
    m_hat = nm / (1.0 - ADAM_B1 ** ADAM_STEP)
    v_hat = nv / (1.0 - ADAM_B2 ** ADAM_STEP)
    return -ADAM_LR * (m_hat / (jnp.sqrt(v_hat) + ADAM_EPS) + ADAM_WD * w), nm, nv


def _adamw_parts(parts, w, m, v, name):
    L, n, C = w.shape
    assert len(parts) == L
    tr = _row_tile(n, 192)
    n_i = n // tr

    def body(*refs):
        p_refs = refs[:L]
        w_ref, m_ref, v_ref, g_ref, d_ref, nm_ref, nv_ref = refs[L:]
        for k in range(L):
            @pl.when(pl.program_id(0) == k)
            def _(k=k):
                acc = p_refs[k][0].astype(F32)
                for s in range(1, N_DEV):
                    acc = acc + p_refs[k][s].astype(F32)
                g_ref[...] = acc

        d_ref[...], nm_ref[...], nv_ref[...] = _adam_step(w_ref[...], g_ref[...], m_ref[...], v_ref[...])

    def part_spec(k):
        return pl.BlockSpec((N_DEV, tr, C),
                            lambda l, i: (0, jnp.where(l == k, i, jnp.where(l < k, 0, n_i - 1)), 0))

    blk = pl.BlockSpec((None, tr, C), lambda l, i: (l, i, 0))
    return pl.pallas_call(
        body, name=name, grid=(L, n_i),
        in_specs=[part_spec(k) for k in range(L)] + [blk] * 3, out_specs=[blk] * 4,
        out_shape=[jax.ShapeDtypeStruct((L, n, C), F32)] * 4,
        compiler_params=_cp("arbitrary", "arbitrary"),
    )(*parts, w, m, v)


def _adamw(w, g, m, v, name):
    R, C = w.shape
    tr = _row_tile(R, 256)

    def body(w_ref, g_ref, m_ref, v_ref, d_ref, nm_ref, nv_ref):
        d_ref[...], nm_ref[...], nv_ref[...] = _adam_step(w_ref[...], g_ref[...], m_ref[...], v_ref[...])

    blk = pl.BlockSpec((tr, C), lambda i: (i, 0))
    return pl.pallas_call(
        body, name=name, grid=(R // tr,),
        in_specs=[blk] * 4, out_specs=[blk] * 3,
        out_shape=[jax.ShapeDtypeStruct((R, C), F32)] * 3,
        compiler_params=_cp("parallel"),
    )(w, g, m, v)


def _me():
    return lax.axis_index("x"), lax.axis_index("y"), lax.axis_index("c")


def _block_rows(ref, dev, n):
    start = (4 * dev[0] + 2 * dev[1] + dev[2]) * n
    if len(ref.shape) == 2:
        return ref.at[pl.ds(start, n), :]
    return ref.at[:, pl.ds(start, n), :]


def _all_gather(shards):
    na = len(shards)
    ns = [s.shape[-2] for s in shards]

    def body(*refs):
        ins, outs = refs[:na], refs[na:2 * na]
        send_sems, recv_sems, local_sems = refs[2 * na:]
        x, y, c = _me()
        me, sibling = (x, y, c), (x, y, 1 - c)
        chips = [(1 - x, y), (x, 1 - y), (1 - x, 1 - y)]

        def copy(a, k, block, to, src=None):
            dst = _block_rows(outs[a], block, ns[a])
            return pltpu.make_async_remote_copy(
                src_ref=dst if src is None else src, dst_ref=dst,
                send_sem=send_sems.at[a, k], recv_sem=recv_sems.at[a, k], device_id=to, device_id_type=MESH)

        mine = [pltpu.make_async_copy(ins[a], _block_rows(outs[a], me, ns[a]), local_sems.at[a]) for a in range(na)]
        for cp in mine:
            cp.start()
        first = []
        for a in range(na):
            first.append(copy(a, 0, me, sibling, src=ins[a]))
            first += [copy(a, 1 + j, me, (*chip, c), src=ins[a]) for j, chip in enumerate(chips)]
        for cp in first:
            cp.start()
        passed = []
        for j, chip in enumerate(chips):
            for a in range(na):
                copy(a, 1 + j, (*chip, c), me).wait_recv()
                fwd = copy(a, 4 + j, (*chip, c), sibling)
                fwd.start()
                passed.append(fwd)
        for a in range(na):
            copy(a, 0, sibling, me).wait_recv()
            for j, chip in enumerate(chips):
                copy(a, 4 + j, (*chip, 1 - c), me).wait_recv()
        for cp in first + passed:
            cp.wait_send()
        for cp in mine:
            cp.wait()

    out_shape = [jax.ShapeDtypeStruct(s.shape[:-2] + (N_DEV * s.shape[-2], s.shape[-1]), s.dtype) for s in shards]
    return pl.pallas_call(
        body, name="weights_all_gather",
        in_specs=[ANY] * na, out_specs=[ANY] * na, out_shape=out_shape,
        scratch_shapes=[pltpu.SemaphoreType.DMA((na, 7)), pltpu.SemaphoreType.DMA((na, 7)),
                        pltpu.SemaphoreType.DMA((na,))],
    )(*shards)


def _split_copies(gather, srcs, lands, send_sems, recv_sems, ns):
    x, y, c = _me()
    me = (x, y, c)
    my_slot = 4 * x + 2 * y + c
    copies = []
    for mask in range(1, N_DEV):
        peer = (x ^ (mask >> 2), y ^ ((mask >> 1) & 1), c ^ (mask & 1))
        for a in range(len(srcs)):
            if gather:
                src, dst = srcs[a], _block_rows(lands[a], me, ns[a])
            else:
                src, dst = _block_rows(srcs[a], peer, ns[a]), lands[a].at[my_slot]
            sem = a * (N_DEV - 1) + mask - 1
            copies.append(pltpu.make_async_remote_copy(
                src_ref=src, dst_ref=dst, send_sem=send_sems.at[sem], recv_sem=recv_sems.at[sem],
                device_id=peer, device_id_type=MESH))
    return copies


HBM_SPEC = pl.BlockSpec(memory_space=pltpu.HBM)
SEM_SPEC = pl.BlockSpec(memory_space=pltpu.SEMAPHORE)


def _split_start(gather, srcs, name, dep=None):
    na = len(srcs)
    if gather:
        ns = [s.shape[-2] for s in srcs]
        lands = [lax.empty(s.shape[:-2] + (N_DEV * s.shape[-2], s.shape[-1]), s.dtype) for s in srcs]
    else:
        ns = [s.shape[-2] // N_DEV for s in srcs]
        lands = [lax.empty((N_DEV, n, s.shape[-1]), s.dtype) for s, n in zip(srcs, ns)]
    n_in = 2 * na + (dep is not None)

    def body(*refs):
        send_sems, recv_sems, own_sems = refs[n_in], refs[n_in + 1], refs[-1]
        token = refs[-2]
        x, y, c = _me()
        me = (x, y, c)
        own = []
        for a in range(na):
            if gather:
                src, dst = refs[a], _block_rows(refs[na + a], me, ns[a])
            else:
                src, dst = _block_rows(refs[a], me, ns[a]), refs[na + a].at[4 * x + 2 * y + c]
            own.append(pltpu.make_async_copy(src, dst, own_sems.at[a]))
        for cp in own:
            cp.start()
        for cp in _split_copies(gather, refs[:na], refs[na:2 * na], send_sems, recv_sems, ns):
            cp.start()
        token[...] = jnp.zeros_like(token)
        for cp in own:
            cp.wait()

    hbm = lambda a: pltpu.with_memory_space_constraint(a, pltpu.HBM)
    args = [hbm(a) for a in srcs] + [hbm(a) for a in lands] + ([dep] if dep is not None else [])
    out = pl.pallas_call(
        body, name=name,
        in_specs=[HBM_SPEC] * (2 * na) + ([ANY] if dep is not None else []),
        out_specs=[SEM_SPEC, SEM_SPEC] + [HBM_SPEC] * (2 * na) + [pl.BlockSpec(memory_space=pltpu.VMEM)],
        out_shape=[pltpu.SemaphoreType.DMA((na * (N_DEV - 1),)), pltpu.SemaphoreType.DMA((na * (N_DEV - 1),))]
        + [pltpu.HBM(a.shape, a.dtype) for a in srcs + lands] + [jax.ShapeDtypeStruct((8, LANES), F32)],
        input_output_aliases={i: 2 + i for i in range(2 * na)},
        scratch_shapes=[pltpu.SemaphoreType.DMA((na,))],
        compiler_params=pltpu.CompilerParams(has_side_effects=pltpu.SideEffectType.DATAFLOW_SIDE_EFFECTING),
    )(*args)
    return (gather, ns, out[0], out[1], list(out[2:2 + na]), list(out[2 + na:2 + 2 * na])), out[-1]


def _split_wait(handle, after, name):
    gather, ns, send, recv, srcs, lands = handle
    na = len(srcs)

    def body(*refs):
        send_sems, recv_sems = refs[2 * na], refs[2 * na + 1]
        for cp in _split_copies(gather, refs[:na], refs[na:2 * na], send_sems, recv_sems, ns):
            cp.wait_send()
            cp.wait_recv()

    out = pl.pallas_call(
        body, name=name,
        in_specs=[HBM_SPEC] * (2 * na) + [SEM_SPEC, SEM_SPEC, ANY],
        out_specs=[HBM_SPEC] * (2 * na),
        out_shape=[pltpu.HBM(a.shape, a.dtype) for a in srcs + lands],
        input_output_aliases={i: i for i in range(2 * na)},
        compiler_params=pltpu.CompilerParams(has_side_effects=pltpu.SideEffectType.DATAFLOW_SIDE_EFFECTING),
    )(*srcs, *lands, send, recv, after)
    return list(out[:na]), list(out[na:])


def _sum_blocks(parts):
    n, R, C = parts.shape

    def body(p_ref, o_ref):
        acc = p_ref[0]
        for k in range(1, n):
            acc = acc + p_ref[k]
        o_ref[...] = acc

    return pl.pallas_call(
        body, name="small_sum",
        in_specs=[pl.BlockSpec(memory_space=pltpu.VMEM)], out_specs=pl.BlockSpec(memory_space=pltpu.VMEM),
        out_shape=jax.ShapeDtypeStruct((R, C), F32),
        compiler_params=pltpu.CompilerParams(vmem_limit_bytes=VMEM_LIMIT),
    )(parts)


def _pack(arrays):
    flat = jnp.concatenate([a.reshape(-1) for a in arrays])
    pad = (-flat.shape[0]) % (8 * LANES)
    return jnp.pad(flat, (0, pad)).reshape(-1, LANES)


def _unpack(buf, shapes):
    flat = buf.reshape(-1)
    out, off = [], 0
    for s in shapes:
        n = 1
        for d in s:
            n *= d
        out.append(flat[off:off + n].reshape(s))
        off += n
    return out


def _ffn_index(layer, second):
    return (2 * layer + second) * 3


def kernel(x, g_ffn1, w_ffn1_gate, w_ffn1_up, w_ffn1_down, g_mix, w_in_ab, conv_w, conv_b, ln_a_g, ln_a_b, ln_v_g, ln_v_b, sp_w, sp_b, w_out_ab, w_qkv, w_o, g_ffn2, w_ffn2_gate, w_ffn2_up, w_ffn2_down, g_final, loss_target, m_g_ffn1, m_w_ffn1_gate, m_w_ffn1_up, m_w_ffn1_down, m_g_mix, m_w_in_ab, m_conv_w, m_conv_b, m_ln_a_g, m_ln_a_b, m_ln_v_g, m_ln_v_b, m_sp_w, m_sp_b, m_w_out_ab, m_w_qkv, m_w_o, m_g_ffn2, m_w_ffn2_gate, m_w_ffn2_up, m_w_ffn2_down, m_g_final, v_g_ffn1, v_w_ffn1_gate, v_w_ffn1_up, v_w_ffn1_down, v_g_mix, v_w_in_ab, v_conv_w, v_conv_b, v_ln_a_g, v_ln_a_b, v_ln_v_g, v_ln_v_b, v_sp_w, v_sp_b, v_w_out_ab, v_w_qkv, v_w_o, v_g_ffn2, v_w_ffn2_gate, v_w_ffn2_up, v_w_ffn2_down, v_g_final):
    n_seq, S, D = x.shape
    T = n_seq * S
    depth = g_ffn1.shape[0]
    assert depth == 2 and D == D_MODEL
    my_block = 4 * lax.axis_index("x") + 2 * lax.axis_index("y") + lax.axis_index("c")

    ffn_parts = []
    for l in range(depth):
        for gate, up, down in ((w_ffn1_gate, w_ffn1_up, w_ffn1_down), (w_ffn2_gate, w_ffn2_up, w_ffn2_down)):
            ffn_parts += [gate[l].T, up[l].T, down[l]]
    ffn_shard = lambda k: jnp.stack(ffn_parts[3 * k:3 * k + 3]).astype(BF16)
    conv_w_pad = jnp.zeros((HALO, conv_w.shape[2]), F32).at[:CONV_WIDTH].set(conv_w[0]).T
    w_ffn = [None] * (2 * depth)
    w_ffn[0], conv_w_t = _all_gather([ffn_shard(0), conv_w_pad])
    conv_w_full = conv_w_t.T[:CONV_WIDTH]
    shards_b = [w_out_ab[0].astype(BF16), ffn_shard(1)]
    shards_d = [w_qkv[0].T.astype(BF16), w_o[0].astype(BF16), ffn_shard(3)]
    gather_a, token = _split_start(True, [w_in_ab[0].T.astype(BF16)], "gather_a_start", dep=conv_w_t)
    gather_b, token = _split_start(True, shards_b, "gather_b_start", dep=token)
    gather_c, token = _split_start(True, [ffn_shard(2)], "gather_c_start", dep=token)
    gather_d, token = _split_start(True, shards_d, "gather_d_start", dep=token)

    def gathered(handle, after, name):
        return _split_wait(handle, after, name)[1]

    row = lambda a: a.reshape(1, -1)
    tril = jnp.tril(jnp.ones((CHUNK, CHUNK), dtype=bool))
    ws = jnp.where(tril[None], sp_w[0], 0.0).astype(BF16)
    ws_t = jnp.swapaxes(ws, 1, 2)
    bias2d = jnp.repeat(sp_b[0].T, DB, axis=1)
    conv_b2, lag, lab = row(conv_b[0]), row(ln_a_g[0]), row(ln_a_b[0])
    lvg, lvb = row(ln_v_g[0]), row(ln_v_b[0])

    x0 = x.reshape(T, D)
    target = loss_target.reshape(T, D)
    saved = []
    xc = x0
    for l in range(depth):
        xa, a1, b1, h1 = _ffn_fwd(xc, row(g_ffn1[l]), w_ffn[2 * l], 0, f"ffn1_fwd_{l}", dep=token)
        if l % 2 == 0:
            w_in_t, = gathered(gather_a, xa, "gather_a_wait")
            z, hm = _mm_nt(xa, w_in_t, row(g_mix[l]), F32, "mix_in_proj")
            cat, conv_out = _mix_fwd(z, conv_w_full, conv_b2, lag, lab, lvg, lvb, ws, bias2d, n_seq)
            w_out, w_ffn[1] = gathered(gather_b, cat, "gather_b_wait")
            xb = _mm_nn_res(cat, w_out, xa, "mix_out_proj")
            mixer = (z, hm, cat, conv_out)
        else:
            w_qkv_t, w_o_full, w_ffn[3] = gathered(gather_d, xa, "gather_d_wait")
            qkv, hm = _mm_nt(xa, w_qkv_t, row(g_mix[l]), BF16, "qkv_proj")
            o, att = _attn_fwd(qkv, n_seq)
            xb = _mm_nn_res(o, w_o_full, xa, "attn_out_proj")
            mixer = (qkv, hm, o, att)
        xn, a2, b2, h2 = _ffn_fwd(xb, row(g_ffn2[l]), w_ffn[2 * l + 1], 0, f"ffn2_fwd_{l}")
        saved.append((xc, a1, b1, h1, xa, mixer, xb, a2, b2, h2))
        xc = xn
        if l == 0:
            w_ffn[2], = gathered(gather_c, xc, "gather_c_wait")

    g, loss_part, dg_final = _loss_head(xc, row(g_final), target)

    dg_ffn1, dg_ffn2, dg_mix = [None] * depth, [None] * depth, [None] * depth
    exchanges = {}
    token = None

    def ffn_back(g, xin, gvec, a, b, h, k, tag, token):
        g, dg, da, db, s, gh = _ffn_bwd(g, xin, gvec, a, b, w_ffn[k], 0, f"ffn{tag}_bwd", dep=token)
        if k == 0:
            return g, dg, (da, db, s, gh, h)
        dws = [_mm_tn(da, h, f"dw_gate{tag}"), _mm_tn(db, h, f"dw_up{tag}"), _mm_tn(s, gh, f"dw_down{tag}")]
        exchanges[f"ffn{k}"], token = _split_start(False, dws, f"exchange_ffn{tag}_start")
        return g, dg, token

    for l in reversed(range(depth)):
        xin, a1, b1, h1, xa, mixer, xb, a2, b2, h2 = saved[l]
        g, dg_ffn2[l], token = ffn_back(g, xb, row(g_ffn2[l]), a2, b2, h2, 2 * l + 1, f"2_{l}", token)
        if l % 2 == 0:
            z, hm, cat, conv_out = mixer
            dcat = _mm_nt(g, w_out, None, F32, "mix_out_bwd", dep=token)
            d_w_out = _mm_tn(cat, g, "dw_out")
            (da1, dz_uv, d_lag, d_lab, d_lvg, d_lvb, d_ws, d_sb) = _mix_bwd_rows(
                dcat, z, conv_out, lag, lab, lvg, lvb, ws, ws_t, bias2d, n_seq)
            dz_a, d_cw, d_cb = _mix_bwd_conv(da1, z, conv_w_full, n_seq)
            d_w_in_t = jnp.concatenate([_mm_tn(dz_a, hm, "dw_in_a"), _mm_tn(dz_uv, hm, "dw_in_uv")])
            exchanges["mix"], token = _split_start(False, [d_w_out, d_w_in_t], "exchange_mix_start")
            g, dg_mix[l] = _mm_nn_rmsbwd([dz_a, dz_uv], w_in_t, xa, row(g_mix[l]), g, "mix_in_bwd", dep=token)
        else:
            qkv, hm, o, att = mixer
            do = _mm_nt(g, w_o_full, None, BF16, "attn_out_bwd", dep=token)
            d_w_o = _mm_tn(o, g, "dw_o")
            dq, dk, dv = _attn_bwd(qkv, do, att, n_seq)
            d_w_qkv_t = jnp.concatenate([_mm_tn(dq, hm, "dw_q"), _mm_tn(dk, hm, "dw_k"), _mm_tn(dv, hm, "dw_v")])
            exchanges["attn"], token = _split_start(False, [d_w_o, d_w_qkv_t], "exchange_attn_start")
            g, dg_mix[l] = _mm_nn_rmsbwd([dq, dk, dv], w_qkv_t, xa, row(g_mix[l]), g, "qkv_bwd", dep=token)
        g, dg_ffn1[l], token = ffn_back(g, xin, row(g_ffn1[l]), a1, b1, h1, 2 * l, f"1_{l}", token)
    grad_x = g.reshape(n_seq, S, D)

    small = [jnp.concatenate(dg_ffn1), jnp.concatenate(dg_mix), d_cw, d_cb, d_lag, d_lab, d_lvg, d_lvb,
             jnp.where(tril[None], d_ws, 0.0), d_sb.T, jnp.concatenate(dg_ffn2), dg_final, loss_part[:, :1]]
    small_shapes = [(depth, D), (depth, D), (CONV_WIDTH, CA), (1, CA), (1, CA), (1, CA), (1, GB, DB), (1, GB, DB),
                    (1, GB, CHUNK, CHUNK), (1, GB, CHUNK), (depth, D), (D,), ()]
    da, db, s, gh, h = token
    small_gather, token = _split_start(True, [_pack(small)], "small_gather_start")

    for which, lhs, rhs in ((2, s, gh), (1, db, h), (0, da, h)):
        dw = _mm_tn(lhs, rhs, f"dw_ffn0_{which}", dep=token, out_dtype=BF16)
        exchanges[f"ffn0_{which}"], token = _split_start(False, [dw], f"exchange_ffn0_{which}_start")

    small_all, = _split_wait(small_gather, token, "small_gather_wait")[1]
    red = _unpack(_sum_blocks(small_all.reshape(N_DEV, -1, LANES)), small_shapes)
    (gr_g_ffn1, gr_g_mix, gr_cw_full, gr_cb, gr_lag, gr_lab, gr_lvg, gr_lvb, gr_sp_w, gr_sp_b,
     gr_g_ffn2, gr_g_final, loss) = red
    n_cw = conv_w.shape[2]
    gr_cw = lax.dynamic_slice(gr_cw_full, (0, my_block * n_cw), (CONV_WIDTH, n_cw))[None]

    def landed(key, after):
        return _split_wait(exchanges[key], after, f"exchange_{key}_wait")[1]

    parts_ffn = [None] * (6 * depth)
    for k in range(1, 2 * depth):
        parts_ffn[3 * k:3 * k + 3] = landed(f"ffn{k}", token)
    parts_out, parts_in = landed("mix", token)
    parts_o, parts_qkv = landed("attn", token)

    grads = {
        "g_ffn1": gr_g_ffn1, "g_mix": gr_g_mix, "conv_w": gr_cw, "conv_b": gr_cb, "ln_a_g": gr_lag,
        "ln_a_b": gr_lab, "ln_v_g": gr_lvg, "ln_v_b": gr_lvb, "sp_w": gr_sp_w, "sp_b": gr_sp_b,
        "g_ffn2": gr_g_ffn2, "g_final": gr_g_final,
    }
    weights = dict(g_ffn1=g_ffn1, w_ffn1_gate=w_ffn1_gate, w_ffn1_up=w_ffn1_up, w_ffn1_down=w_ffn1_down, g_mix=g_mix,
                   w_in_ab=w_in_ab, conv_w=conv_w, conv_b=conv_b, ln_a_g=ln_a_g, ln_a_b=ln_a_b, ln_v_g=ln_v_g,
                   ln_v_b=ln_v_b, sp_w=sp_w, sp_b=sp_b, w_out_ab=w_out_ab, w_qkv=w_qkv, w_o=w_o, g_ffn2=g_ffn2,
                   w_ffn2_gate=w_ffn2_gate, w_ffn2_up=w_ffn2_up, w_ffn2_down=w_ffn2_down, g_final=g_final)
    m_in = dict(g_ffn1=m_g_ffn1, w_ffn1_gate=m_w_ffn1_gate, w_ffn1_up=m_w_ffn1_up, w_ffn1_down=m_w_ffn1_down,
                g_mix=m_g_mix, w_in_ab=m_w_in_ab, conv_w=m_conv_w, conv_b=m_conv_b, ln_a_g=m_ln_a_g, ln_a_b=m_ln_a_b,
                ln_v_g=m_ln_v_g, ln_v_b=m_ln_v_b, sp_w=m_sp_w, sp_b=m_sp_b, w_out_ab=m_w_out_ab, w_qkv=m_w_qkv,
                w_o=m_w_o, g_ffn2=m_g_ffn2, w_ffn2_gate=m_w_ffn2_gate, w_ffn2_up=m_w_ffn2_up,
                w_ffn2_down=m_w_ffn2_down, g_final=m_g_final)
    v_in = dict(g_ffn1=v_g_ffn1, w_ffn1_gate=v_w_ffn1_gate, w_ffn1_up=v_w_ffn1_up, w_ffn1_down=v_w_ffn1_down,
                g_mix=v_g_mix, w_in_ab=v_w_in_ab, conv_w=v_conv_w, conv_b=v_conv_b, ln_a_g=v_ln_a_g, ln_a_b=v_ln_a_b,
                ln_v_g=v_ln_v_g, ln_v_b=v_ln_v_b, sp_w=v_sp_w, sp_b=v_sp_b, w_out_ab=v_w_out_ab, w_qkv=v_w_qkv,
                w_o=v_w_o, g_ffn2=v_g_ffn2, w_ffn2_gate=v_w_ffn2_gate, w_ffn2_up=v_w_ffn2_up,
                w_ffn2_down=v_w_ffn2_down, g_final=v_g_final)
    names = list(weights)
    grads = {n: grads[n].reshape(weights[n].shape) for n in grads}

    delta, new_m, new_v = {}, {}, {}

    def adamw_big(n, parts):
        if weights[n].shape[-1] == D:
            view = back = lambda a: a
        else:
            view = back = lambda a: jnp.swapaxes(a, 1, 2)
        out = _adamw_parts(parts, view(weights[n]), view(m_in[n]), view(v_in[n]), f"adamw_{n}")
        grads[n], delta[n], new_m[n], new_v[n] = [back(a) for a in out]

    adamw_big("w_in_ab", [parts_in])
    adamw_big("w_out_ab", [parts_out])
    adamw_big("w_qkv", [parts_qkv])
    adamw_big("w_o", [parts_o])
    kinds = ("gate", "up", "down")
    for which, kind in enumerate(kinds):
        adamw_big(f"w_ffn2_{kind}", [parts_ffn[_ffn_index(l, 1) + which] for l in range(depth)])
    big = [n for n in names if n.startswith("w_")]
    after = jnp.concatenate([delta[n].reshape(-1)[:1] for n in big if n in delta]).reshape(1, -1)
    for which in (2, 1, 0):
        parts_ffn[which], = landed(f"ffn0_{which}", after)
    for which, kind in enumerate(kinds):
        adamw_big(f"w_ffn1_{kind}", [parts_ffn[_ffn_index(l, 0) + which] for l in range(depth)])
    little = [n for n in names if n not in big]
    shapes = [weights[n].shape for n in little]
    d, nm, nv = _adamw(_pack([weights[n] for n in little]), _pack([grads[n] for n in little]),
                       _pack([m_in[n] for n in little]), _pack([v_in[n] for n in little]), "adamw_small")
    for n, dd, mm, vv in zip(little, _unpack(d, shapes), _unpack(nm, shapes), _unpack(nv, shapes)):
        delta[n], new_m[n], new_v[n] = dd, mm, vv

    return (loss, grad_x, *[grads[n] for n in names], *[delta[n] for n in names],
            *[new_m[n] for n in names], *[new_v[n] for n in names])
```

```python
import jax
import jax.numpy as jnp
from jax import lax
from jax.experimental import pallas as pl
from jax.experimental.pallas import tpu as pltpu

F32 = jnp.float32
BF16 = jnp.bfloat16

D_MODEL = 1024
CA = 512
CB = 512
GB = 4
DB = 128
CHUNK = 128
CONV_WIDTH = 31
N_HEADS = 16
HEAD_DIM = 64
EPS = 1e-6
N_DEV = 8
LANES = 128
SUBLANES = 8
QB = 128
ATT_TQ = 1024
FFN_TN = 2816
FFN_TM = 256
HALO = 32
CONV_ROWS = 32
ATT_SCALE = HEAD_DIM ** -0.5

ADAM_LR = 0.001
ADAM_B1 = 0.9
ADAM_B2 = 0.999
ADAM_EPS = 1e-08
ADAM_WD = 0.01
ADAM_STEP = 10

NT = (((1,), (1,)), ((), ()))
NN = (((1,), (0,)), ((), ()))
TN = (((0,), (0,)), ((), ()))
MESH = pl.DeviceIdType.MESH
ANY = pl.BlockSpec(memory_space=pl.ANY)
VMEM_LIMIT = 60 * 1024 * 1024


def _dot(a, b, dims):
    return lax.dot_general(a, b, dims, preferred_element_type=F32)


def _cp(*sem):
    return pltpu.CompilerParams(dimension_semantics=sem, vmem_limit_bytes=VMEM_LIMIT)


def _pcall(body, *, in_specs, args, dep=None, **kw):
    if dep is not None:
        n_in = len(in_specs)
        inner = body

        def body(*refs):
            inner(*refs[:n_in], *refs[n_in + 1:])

        in_specs = list(in_specs) + [ANY]
        args = tuple(args) + (dep,)
    return pl.pallas_call(body, in_specs=list(in_specs), **kw)(*args)


def _tile(n, want):
    if n <= want:
        return n
    t = want - want % LANES
    while t > LANES and n % t:
        t -= LANES
    assert n % t == 0, (n, want)
    return t


def _sigmoid(x):
    return 0.5 * jnp.tanh(0.5 * x) + 0.5


def _rstd(x):
    return lax.rsqrt(jnp.mean(x * x, axis=-1, keepdims=True) + EPS)


def _rms_bwd(x, g, dh):
    r = _rstd(x)
    u = dh * g
    dx = r * (u - x * (r * r) * jnp.mean(u * x, axis=-1, keepdims=True))
    dg = jnp.sum(dh * x * r, axis=0, keepdims=True)
    return dx, dg


def _ln_fwd(x, g, b):
    mu = jnp.mean(x, axis=-1, keepdims=True)
    xc = x - mu
    r = lax.rsqrt(jnp.mean(xc * xc, axis=-1, keepdims=True) + EPS)
    xh = xc * r
    return xh * g + b, xh, r


def _ln_bwd(dy, xh, r, g):
    dxh = dy * g
    return r * (dxh - jnp.mean(dxh, axis=-1, keepdims=True)
                - xh * jnp.mean(dxh * xh, axis=-1, keepdims=True))


def _ffn_fwd(x, g, wall, base, name, dep=None):
    T, D = x.shape
    F = wall.shape[1]
    tm, tn = _tile(T, FFN_TM), _tile(F, FFN_TN)
    n_j = F // tn

    def body(x_ref, g_ref, wg_ref, wu_ref, wd_ref, xo_ref, a_ref, b_ref, h_ref, acc_ref):
        j = pl.program_id(1)

        @pl.when(j == 0)
        def _():
            xv = x_ref[...]
            h_ref[...] = (xv * _rstd(xv) * g_ref[...]).astype(BF16)
            acc_ref[...] = jnp.zeros_like(acc_ref)

        h = h_ref[...]
        a = _dot(h, wg_ref[...], NT)
        b = _dot(h, wu_ref[...], NT)
        a_ref[...] = a.astype(BF16)
        b_ref[...] = b.astype(BF16)
        s = (a * _sigmoid(a) * b).astype(BF16)
        acc_ref[...] += _dot(s, wd_ref[...], NN)

        @pl.when(j == n_j - 1)
        def _():
            xo_ref[...] = x_ref[...] + 0.5 * acc_ref[...]

    single = pl.Buffered(1) if n_j == 1 else None
    wspec = lambda k: pl.BlockSpec((None, tn, D), lambda i, j: (base + k, j, 0), pipeline_mode=single)
    return _pcall(
        body, name=name, grid=(T // tm, n_j), dep=dep, args=(x, g, wall, wall, wall),
        in_specs=[pl.BlockSpec((tm, D), lambda i, j: (i, 0)), pl.BlockSpec((1, D), lambda i, j: (0, 0)),
                  wspec(0), wspec(1), wspec(2)],
        out_specs=[pl.BlockSpec((tm, D), lambda i, j: (i, 0)), pl.BlockSpec((tm, tn), lambda i, j: (i, j)),
                   pl.BlockSpec((tm, tn), lambda i, j: (i, j)), pl.BlockSpec((tm, D), lambda i, j: (i, 0))],
        out_shape=[jax.ShapeDtypeStruct((T, D), F32), jax.ShapeDtypeStruct((T, F), BF16),
                   jax.ShapeDtypeStruct((T, F), BF16), jax.ShapeDtypeStruct((T, D), BF16)],
        scratch_shapes=[pltpu.VMEM((tm, D), F32)],
        compiler_params=_cp("parallel", "arbitrary"),
    )


def _ffn_bwd(go, x, g, a, b, wall, base, name, dep=None):
    T, D = x.shape
    F = wall.shape[1]
    tm, tn = _tile(T, FFN_TM), _tile(F, FFN_TN)
    n_j = F // tn

    def body(go_ref, x_ref, g_ref, a_ref, b_ref, wg_ref, wu_ref, wd_ref,
             gx_ref, dg_ref, da_ref, db_ref, s_ref, gh_ref, acc_ref):
        i, j = pl.program_id(0), pl.program_id(1)

        @pl.when(j == 0)
        def _():
            gh_ref[...] = (0.5 * go_ref[...]).astype(BF16)
            acc_ref[...] = jnp.zeros_like(acc_ref)

        @pl.when((i == 0) & (j == 0))
        def _():
            dg_ref[...] = jnp.zeros_like(dg_ref)

        ds = _dot(gh_ref[...], wd_ref[...], NT)
        av = a_ref[...].astype(F32)
        bv = b_ref[...].astype(F32)
        sig = _sigmoid(av)
        sl = av * sig
        dab = ((ds * bv) * (sig + sl * (1.0 - sig))).astype(BF16)
        dbb = (ds * sl).astype(BF16)
        s_ref[...] = (sl * bv).astype(BF16)
        da_ref[...] = dab
        db_ref[...] = dbb
        acc_ref[...] += _dot(dab, wg_ref[...], NN) + _dot(dbb, wu_ref[...], NN)

        @pl.when(j == n_j - 1)
        def _():
            dx, dg = _rms_bwd(x_ref[...], g_ref[...], acc_ref[...])
            gx_ref[...] = go_ref[...] + dx
            dg_ref[...] += dg

    single = pl.Buffered(1) if n_j == 1 else None
    wspec = lambda k: pl.BlockSpec((None, tn, D), lambda i, j: (base + k, j, 0), pipeline_mode=single)
    row = pl.BlockSpec((tm, D), lambda i, j: (i, 0))
    hid = pl.BlockSpec((tm, tn), lambda i, j: (i, j))
    vec = pl.BlockSpec((1, D), lambda i, j: (0, 0))
    return _pcall(
        body, name=name, grid=(T // tm, n_j), dep=dep, args=(go, x, g, a, b, wall, wall, wall),
        in_specs=[row, row, vec, hid, hid, wspec(0), wspec(1), wspec(2)],
        out_specs=[row, vec, hid, hid, hid, row],
        out_shape=[jax.ShapeDtypeStruct((T, D), F32), jax.ShapeDtypeStruct((1, D), F32),
                   jax.ShapeDtypeStruct((T, F), BF16), jax.ShapeDtypeStruct((T, F), BF16),
                   jax.ShapeDtypeStruct((T, F), BF16), jax.ShapeDtypeStruct((T, D), BF16)],
        scratch_shapes=[pltpu.VMEM((tm, D), F32)],
        compiler_params=_cp("arbitrary", "arbitrary"),
    )


def _mm_tn(a, b, name, dep=None, out_dtype=F32):
    T, M = a.shape
    N = b.shape[1]
    tmm, tk = _tile(M, 1536), _tile(T, 2048)
    n_k = T // tk
    narrow = out_dtype != F32

    def body(a_ref, b_ref, o_ref, *scratch):
        acc_ref = scratch[0] if narrow else o_ref

        @pl.when(pl.program_id(1) == 0)
        def _():
            acc_ref[...] = jnp.zeros_like(acc_ref)

        acc_ref[...] += _dot(a_ref[...].astype(BF16), b_ref[...].astype(BF16), TN)
        if narrow:
            @pl.when(pl.program_id(1) == n_k - 1)
            def _():
                o_ref[...] = acc_ref[...].astype(out_dtype)

    return _pcall(
        body, name=name, grid=(M // tmm, n_k), dep=dep, args=(a, b),
        in_specs=[pl.BlockSpec((tk, tmm), lambda m, k: (k, m)), pl.BlockSpec((tk, N), lambda m, k: (k, 0))],
        out_specs=pl.BlockSpec((tmm, N), lambda m, k: (m, 0)),
        out_shape=jax.ShapeDtypeStruct((M, N), out_dtype),
        scratch_shapes=[pltpu.VMEM((tmm, N), F32)] if narrow else [],
        compiler_params=_cp("parallel", "arbitrary"),
    )


def _mm_nt(x, wt, g, out_dtype, name, dep=None):
    T, K = x.shape
    N = wt.shape[0]
    tm, tn = _tile(T, 512), N
    norm = g is not None

    def body(*refs):
        if norm:
            x_ref, g_ref, w_ref, o_ref, h_ref = refs
        else:
            x_ref, w_ref, o_ref, h_ref = refs

        @pl.when(pl.program_id(1) == 0)
        def _():
            xv = x_ref[...].astype(F32)
            if norm:
                xv = xv * _rstd(xv) * g_ref[...]
            h_ref[...] = xv.astype(BF16)

        o_ref[...] = _dot(h_ref[...], w_ref[...], NT).astype(out_dtype)

    row = pl.BlockSpec((tm, K), lambda i, j: (i, 0))
    wsp = pl.BlockSpec((tn, K), lambda i, j: (j, 0))
    osp = pl.BlockSpec((tm, tn), lambda i, j: (i, j))
    if norm:
        return pl.pallas_call(
            body, name=name, grid=(T // tm, N // tn),
            in_specs=[row, pl.BlockSpec((1, K), lambda i, j: (0, 0)), wsp],
            out_specs=[osp, row],
            out_shape=[jax.ShapeDtypeStruct((T, N), out_dtype), jax.ShapeDtypeStruct((T, K), BF16)],
            compiler_params=_cp("parallel", "arbitrary"),
        )(x, g, wt)
    return _pcall(
        body, name=name, grid=(T // tm, N // tn), dep=dep, args=(x, wt),
        in_specs=[row, wsp], out_specs=osp,
        out_shape=jax.ShapeDtypeStruct((T, N), out_dtype),
        scratch_shapes=[pltpu.VMEM((tm, K), BF16)],
        compiler_params=_cp("parallel", "arbitrary"),
    )


def _mm_nn_res(act, w, resid, name):
    T, K = act.shape
    D = w.shape[1]
    tm = _tile(T, 512)

    def body(a_ref, w_ref, r_ref, o_ref):
        o_ref[...] = r_ref[...] + _dot(a_ref[...].astype(BF16), w_ref[...], NN)

    return pl.pallas_call(
        body, name=name, grid=(T // tm,),
        in_specs=[pl.BlockSpec((tm, K), lambda i: (i, 0)), pl.BlockSpec((K, D), lambda i: (0, 0)),
                  pl.BlockSpec((tm, D), lambda i: (i, 0))],
        out_specs=pl.BlockSpec((tm, D), lambda i: (i, 0)),
        out_shape=jax.ShapeDtypeStruct((T, D), F32),
        compiler_params=_cp("parallel"),
    )(act, w, resid)


def _mm_nn_rmsbwd(acts, w, x, g, gprev, name, dep=None):
    T = acts[0].shape[0]
    ks = [a.shape[1] for a in acts]
    K, D = w.shape
    assert sum(ks) == K
    tm = _tile(T, 512)
    na = len(acts)

    def body(*refs):
        a_refs = refs[:na]
        w_ref, x_ref, g_ref, gp_ref, o_ref, dg_ref = refs[na:]

        @pl.when(pl.program_id(0) == 0)
        def _():
            dg_ref[...] = jnp.zeros_like(dg_ref)

        dh, off = None, 0
        for a_ref, k in zip(a_refs, ks):
            part = _dot(a_ref[...].astype(BF16), w_ref[off:off + k, :], NN)
            dh = part if dh is None else dh + part
            off += k
        dx, dg = _rms_bwd(x_ref[...], g_ref[...], dh)
        o_ref[...] = gp_ref[...] + dx
        dg_ref[...] += dg

    row = pl.BlockSpec((tm, D), lambda i: (i, 0))
    vec = pl.BlockSpec((1, D), lambda i: (0, 0))
    return _pcall(
        body, name=name, grid=(T // tm,), dep=dep, args=(*acts, w, x, g, gprev),
        in_specs=[pl.BlockSpec((tm, k), lambda i: (i, 0)) for k in ks]
        + [pl.BlockSpec((K, D), lambda i: (0, 0)), row, vec, row],
        out_specs=[row, vec],
        out_shape=[jax.ShapeDtypeStruct((T, D), F32), jax.ShapeDtypeStruct((1, D), F32)],
        compiler_params=_cp("arbitrary"),
    )


def _loss_head(x, g, target):
    T, D = x.shape
    tm = _tile(T, 512)

    def body(x_ref, g_ref, t_ref, dx_ref, loss_ref, dg_ref):
        @pl.when(pl.program_id(0) == 0)
        def _():
            loss_ref[...] = jnp.zeros_like(loss_ref)
            dg_ref[...] = jnp.zeros_like(dg_ref)

        xv = x_ref[...]
        gv = g_ref[...]
        e = xv * _rstd(xv) * gv - t_ref[...]
        per_tok = jnp.sum(e * e, axis=-1, keepdims=True) * (1.0 / D)
        loss_ref[...] += 0.5 * jnp.sum(per_tok, axis=0, keepdims=True)
        dx, dg = _rms_bwd(xv, gv, e * (1.0 / D))
        dx_ref[...] = dx
        dg_ref[...] += dg

    row = pl.BlockSpec((tm, D), lambda i: (i, 0))
    vec = pl.BlockSpec((1, D), lambda i: (0, 0))
    return pl.pallas_call(
        body, name="loss_head", grid=(T // tm,),
        in_specs=[row, vec, row],
        out_specs=[row, pl.BlockSpec((1, LANES), lambda i: (0, 0)), vec],
        out_shape=[jax.ShapeDtypeStruct((T, D), F32), jax.ShapeDtypeStruct((1, LANES), F32),
                   jax.ShapeDtypeStruct((1, D), F32)],
        compiler_params=_cp("arbitrary"),
    )(x, g, target)


def _log_gates(z):
    neg_abs = lax.bitcast_convert_type(lax.bitcast_convert_type(z, jnp.uint32) | jnp.uint32(0x80000000), F32)
    ls = jnp.minimum(z, 0.0) - jnp.log(1.0 + jnp.exp(neg_abs))
    return ls, ls - z


def _cumsum_mm(v, u):
    return _dot(v.astype(BF16), u, NN)


def _half_rowsum(v):
    n = v.shape[0]
    s0 = jnp.sum(v[:, :QB], axis=1, keepdims=True)
    s1 = jnp.sum(v[:, QB:], axis=1, keepdims=True)
    return jnp.concatenate([jnp.broadcast_to(s0, (n, QB)), jnp.broadcast_to(s1, (n, QB))], axis=1)


def _stack_heads(src_ref, dst_ref, n_blk):
    m0 = lax.broadcasted_iota(jnp.int32, (1, LANES), 1) < HEAD_DIM

    def fill(c, carry):
        blk = src_ref[pl.ds(pl.multiple_of(c * QB, QB), QB), :]
        zero = jnp.zeros_like(blk)
        dst_ref[c, 0:QB, :] = jnp.where(m0, blk, zero)
        dst_ref[c, QB:2 * QB, :] = jnp.where(m0, zero, blk)
        return carry

    lax.fori_loop(0, n_blk, fill, 0)


def _diag_mask(tq, j):
    n = tq - j * QB
    row = lax.broadcasted_iota(jnp.int32, (n, 2 * QB), 0)
    col = lax.broadcasted_iota(jnp.int32, (n, 2 * QB), 1)
    return (col & (QB - 1)) < row


def _tri_blockdiag(upper):
    r = lax.broadcasted_iota(jnp.int32, (2 * QB, 2 * QB), 0)
    c = lax.broadcasted_iota(jnp.int32, (2 * QB, 2 * QB), 1)
    same = (r // QB) == (c // QB)
    return (same & ((r > c) if upper else (r < c))).astype(BF16)


def _attn_tiles(T, n_seq):
    S = T // n_seq
    tq = ATT_TQ if S % ATT_TQ == 0 else 2 * QB
    assert S % tq == 0
    return S, tq, tq // QB, S // tq, S // QB


def _attn_fwd(qkv, n_seq):
    T = qkv.shape[0]
    S, tq, r, n_q, n_k = _attn_tiles(T, n_seq)
    n_p = D_MODEL // LANES
    n_steps = n_seq * n_p * n_q
    u_suffix = _tri_blockdiag(True)

    def body(q_ref, k_ref, v_ref, u_ref, o_ref, a_hbm, kk_ref, vv_ref, lr_s, acc_s, a_stage, sems):
        qi = pl.program_id(2)
        group = (pl.program_id(0) * n_p + pl.program_id(1)) * n_q + qi

        @pl.when(qi == 0)
        def _():
            _stack_heads(k_ref, kk_ref, n_k)
            _stack_heads(v_ref, vv_ref, n_k)

        u = u_ref[...]
        lr_s[...] = jnp.zeros_like(lr_s)
        acc_s[...] = jnp.zeros_like(acc_s)
        base = ((group // n_q) * (n_q * (n_q + 1) // 2) + (qi * (qi + 1)) // 2) % 2

        def saves(first_kj, half):
            return [pltpu.make_async_copy(a_stage.at[half, j], a_hbm.at[group, first_kj - j], sems.at[half])
                    for j in range(r)]

        @pl.when(group >= 2)
        def _():
            for cp in saves(0, base):
                cp.wait()

        a_stage[base] = jnp.zeros_like(a_stage[0])

        def step(kj, half, j, rows, q, mask, lr, acc):
            ls, lk = _log_gates(_dot(q, kk_ref[kj], NT))
            if mask is not None:
                lk = jnp.where(mask, lk, 0.0)
            a = jnp.exp(ls + _cumsum_mm(lk, u) + lr)
            if mask is not None:
                a = jnp.where(mask, a, 0.0)
            a = a.astype(BF16)
            a_stage[half, j, rows, :] = a
            return lr + _half_rowsum(lk), acc + _dot(a, vv_ref[kj], NN)

        last = (qi + 1) * r - 1
        for n in range(r):
            rows = slice((r - 1 - n) * QB, tq)
            lr, acc = step(last - n, base, n, rows, q_ref[rows, :] * ATT_SCALE, _diag_mask(tq, r - 1 - n),
                           lr_s[rows, :], acc_s[rows, :])
            lr_s[rows, :] = lr
            acc_s[rows, :] = acc
        for cp in saves(last, base):
            cp.start()

        q = q_ref[...] * ATT_SCALE

        def off(it, carry):
            half = (base + it + 1) % 2
            first = (qi - it) * r - 1
            for cp in saves(first, half):
                cp.wait()
            lr, acc = lr_s[...], acc_s[...]
            for j in range(r):
                lr, acc = step(first - j, half, j, slice(0, tq), q, None, lr, acc)
            lr_s[...] = lr
            acc_s[...] = acc
            for cp in saves(first, half):
                cp.start()
            return carry

        lax.fori_loop(0, qi, off, 0)

        @pl.when(group == n_steps - 1)
        def _():
            for cp in saves(0, (base + qi) % 2):
                cp.wait()
            if n_steps * n_q > 1:
                for cp in saves(0, (base + qi + 1) % 2):
                    cp.wait()

        o_ref[...] = acc_s[...].astype(BF16)

    return pl.pallas_call(
        body, name="attn_fwd", grid=(n_seq, n_p, n_q),
        in_specs=[pl.BlockSpec((tq, LANES), lambda b, p, qi: (b * n_q + qi, p)),
                  pl.BlockSpec((S, LANES), lambda b, p, qi: (b, n_p + p)),
                  pl.BlockSpec((S, LANES), lambda b, p, qi: (b, 2 * n_p + p)),
                  pl.BlockSpec((2 * QB, 2 * QB), lambda b, p, qi: (0, 0))],
        out_specs=[pl.BlockSpec((tq, LANES), lambda b, p, qi: (b * n_q + qi, p)), ANY],
        out_shape=[jax.ShapeDtypeStruct((T, D_MODEL), BF16),
                   jax.ShapeDtypeStruct((n_seq * n_p * n_q, n_k, tq, 2 * QB), BF16)],
        scratch_shapes=[pltpu.VMEM((n_k, 2 * QB, LANES), BF16), pltpu.VMEM((n_k, 2 * QB, LANES), BF16),
                        pltpu.VMEM((tq, 2 * QB), F32), pltpu.VMEM((tq, LANES), F32),
                        pltpu.VMEM((2, r, tq, 2 * QB), BF16), pltpu.SemaphoreType.DMA((2,))],
        compiler_params=_cp("arbitrary", "arbitrary", "arbitrary"),
    )(qkv, qkv, qkv, u_suffix)


def _attn_bwd(qkv, do, a_saved, n_seq):
    T = qkv.shape[0]
    S, tq, r, n_q, n_k = _attn_tiles(T, n_seq)
    n_p = D_MODEL // LANES
    n_steps = n_seq * n_p * n_q
    u_prefix = _tri_blockdiag(False)

    def body(q_ref, k_ref, v_ref, do_ref, u_ref, a_hbm, dq_ref, dk_out, dv_out,
             kk_ref, vv_ref, cg_s, dq_s, dk_ref, dv_ref, a_stage, sems):
        qi = pl.program_id(2)
        group = (pl.program_id(0) * n_p + pl.program_id(1)) * n_q + qi

        base = ((group // n_q) * (n_q * (n_q + 1) // 2) + (qi * (qi + 1)) // 2) % 2

        def fetches(grp, g, half):
            return [pltpu.make_async_copy(a_hbm.at[grp, g * r + j], a_stage.at[half, j], sems.at[half])
                    for j in range(r)]

        @pl.when(group == 0)
        def _():
            for cp in fetches(group, 0, 0):
                cp.start()

        @pl.when(qi == 0)
        def _():
            _stack_heads(k_ref, kk_ref, n_k)
            _stack_heads(v_ref, vv_ref, n_k)
            dk_ref[...] = jnp.zeros_like(dk_ref)
            dv_ref[...] = jnp.zeros_like(dv_ref)

        u = u_ref[...]
        cg_s[...] = jnp.zeros_like(cg_s)
        dq_s[...] = jnp.zeros_like(dq_s)

        def step(kj, half, j, rows, q, dov, mask, cg, dq):
            kk = kk_ref[kj]
            beta = _sigmoid(_dot(q, kk, NT))
            a = a_stage[half, j, rows, :]
            g = a.astype(F32) * _dot(dov, vv_ref[kj], NT)
            dz = g - (g + _dot(g.astype(BF16), u, NN) + cg) * beta
            if mask is not None:
                dz = jnp.where(mask, dz, 0.0)
            dz = dz.astype(BF16)
            keys = pl.ds(pl.multiple_of(kj * QB, QB), QB)
            dvt = _dot(dov_t[:, rows], a, NN)
            dv_ref[keys, :] += jnp.where(t0, dvt[:, :QB], dvt[:, QB:]).T
            dkt = _dot(q_t[:, rows], dz, NN)
            dk_ref[keys, :] += jnp.where(t0, dkt[:, :QB], dkt[:, QB:]).T
            return cg + _half_rowsum(g), dq + _dot(dz, kk, NN)

        q = q_ref[...] * ATT_SCALE
        dov = do_ref[...]
        q_t = q.astype(F32).T.astype(BF16)
        dov_t = dov.astype(F32).T.astype(BF16)
        t0 = lax.broadcasted_iota(jnp.int32, (LANES, 1), 0) < HEAD_DIM

        def off(it, carry):
            half = (base + it) % 2
            for cp in fetches(group, it, half):
                cp.wait()
            for cp in fetches(group, it + 1, 1 - half):
                cp.start()
            cg, dq = cg_s[...], dq_s[...]
            for j in range(r):
                cg, dq = step(it * r + j, half, j, slice(0, tq), q, dov, None, cg, dq)
            cg_s[...] = cg
            dq_s[...] = dq
            return carry

        lax.fori_loop(0, qi, off, 0)

        half = (base + qi) % 2
        for cp in fetches(group, qi, half):
            cp.wait()

        @pl.when(group < n_steps - 1)
        def _():
            for cp in fetches(group + 1, 0, 1 - half):
                cp.start()

        for j in range(r):
            rows = slice(j * QB, tq)
            cg, dq = step(qi * r + j, half, j, rows, q_ref[rows, :] * ATT_SCALE, do_ref[rows, :],
                          _diag_mask(tq, j), cg_s[rows, :], dq_s[rows, :])
            cg_s[rows, :] = cg
            dq_s[rows, :] = dq
        dq_ref[...] = (dq_s[...] * ATT_SCALE).astype(BF16)

        @pl.when(qi == n_q - 1)
        def _():
            dk_out[...] = dk_ref[...].astype(BF16)
            dv_out[...] = dv_ref[...].astype(BF16)

    qspec = pl.BlockSpec((tq, LANES), lambda b, p, qi: (b * n_q + qi, p))
    seq = lambda off: pl.BlockSpec((S, LANES), lambda b, p, qi: (b, off + p))
    return pl.pallas_call(
        body, name="attn_bwd", grid=(n_seq, n_p, n_q),
        in_specs=[qspec, seq(n_p), seq(2 * n_p), qspec,
                  pl.BlockSpec((2 * QB, 2 * QB), lambda b, p, qi: (0, 0)), ANY],
        out_specs=[qspec, seq(0), seq(0)],
        out_shape=[jax.ShapeDtypeStruct((T, D_MODEL), BF16)] * 3,
        scratch_shapes=[pltpu.VMEM((n_k, 2 * QB, LANES), BF16), pltpu.VMEM((n_k, 2 * QB, LANES), BF16),
                        pltpu.VMEM((tq, 2 * QB), F32), pltpu.VMEM((tq, LANES), F32),
                        pltpu.VMEM((S, LANES), F32), pltpu.VMEM((S, LANES), F32),
                        pltpu.VMEM((2, r, tq, 2 * QB), BF16), pltpu.SemaphoreType.DMA((2,))],
        compiler_params=_cp("arbitrary", "arbitrary", "arbitrary"),
    )(qkv, qkv, qkv, do, u_prefix, a_saved)


def _shifted_copies(sh_ref):
    rows = sh_ref.shape[1] - SUBLANES
    for s in range(1, SUBLANES):
        sh_ref[s, 0:rows, :] = sh_ref[0, s:s + rows, :]


def _shifted(sh_ref, start, n):
    s = start % SUBLANES
    return sh_ref[s, start - s:start - s + n, :]


def _glu_with_halo(av_ref, ag_ref, avh_ref, agh_ref, a0_s, first, ts):
    hal = avh_ref[...] * _sigmoid(agh_ref[...])
    a0_s[0, 0:HALO, :] = jnp.where(first, 0.0, hal)
    a0_s[0, HALO:HALO + ts, :] = av_ref[...] * _sigmoid(ag_ref[...])
    _shifted_copies(a0_s)


def _mix_specs(ts, n_r, with_left):
    blk = lambda c: pl.BlockSpec((ts, CA), lambda b, r: (b * n_r + r, c))
    per = ts // HALO
    left = lambda c: pl.BlockSpec((HALO, CA), lambda b, r: (jnp.maximum((b * n_r + r) * per - 1, 0), c))
    return blk, (left if with_left else None)


def _mix_fwd(z, conv_w, conv_b, ln_a_g, ln_a_b, ln_v_g, ln_v_b, ws, bias2d, n_seq):
    T = z.shape[0]
    S = T // n_seq
    ts = _tile(S, 512)
    n_r = S // ts
    shift = HALO - (CONV_WIDTH - 1)

    def body(av_ref, ag_ref, avh_ref, agh_ref, u_ref, v_ref, cw_ref, cb_ref, lag_ref, lab_ref,
             lvg_ref, lvb_ref, ws_ref, bias_ref, cat_ref, a1_ref, a0_s):
        _glu_with_halo(av_ref, ag_ref, avh_ref, agh_ref, a0_s, pl.program_id(1) == 0, ts)
        for rb in range(ts // CONV_ROWS):
            base = rb * CONV_ROWS
            acc = jnp.broadcast_to(cb_ref[...], (CONV_ROWS, CA))
            for k in range(CONV_WIDTH):
                acc = acc + cw_ref[k:k + 1, :] * _shifted(a0_s, base + shift + k, CONV_ROWS)
            a1_ref[base:base + CONV_ROWS, :] = acc
        y, _, _ = _ln_fwd(a1_ref[...], lag_ref[...], lab_ref[...])
        cat_ref[:, 0:CA] = (y * _sigmoid(y)).astype(BF16)
        for gi in range(GB):
            sl = slice(gi * DB, (gi + 1) * DB)
            v1, _, _ = _ln_fwd(v_ref[:, sl], lvg_ref[:, sl], lvb_ref[:, sl])
            v1 = v1.astype(BF16)
            for c in range(ts // CHUNK):
                rs = slice(c * CHUNK, (c + 1) * CHUNK)
                v2 = _dot(ws_ref[gi], v1[rs], NN) + bias_ref[:, sl]
                cat_ref[rs, CA + gi * DB:CA + (gi + 1) * DB] = (u_ref[rs, sl] * v2).astype(BF16)

    blk, left = _mix_specs(ts, n_r, True)
    vec = pl.BlockSpec((1, CA), lambda b, r: (0, 0))
    return pl.pallas_call(
        body, name="mix_fwd", grid=(n_seq, n_r),
        in_specs=[blk(0), blk(1), left(0), left(1), blk(2), blk(3),
                  pl.BlockSpec((CONV_WIDTH, CA), lambda b, r: (0, 0)), vec, vec, vec, vec, vec,
                  pl.BlockSpec((GB, CHUNK, CHUNK), lambda b, r: (0, 0, 0)),
                  pl.BlockSpec((CHUNK, CB), lambda b, r: (0, 0))],
        out_specs=[pl.BlockSpec((ts, CA + CB), lambda b, r: (b * n_r + r, 0)), blk(0)],
        out_shape=[jax.ShapeDtypeStruct((T, CA + CB), BF16), jax.ShapeDtypeStruct((T, CA), F32)],
        scratch_shapes=[pltpu.VMEM((SUBLANES, HALO + ts, CA), F32)],
        compiler_params=_cp("parallel", "parallel"),
    )(z, z, z, z, z, z, conv_w, conv_b, ln_a_g, ln_a_b, ln_v_g, ln_v_b, ws, bias2d)


def _mix_bwd_rows(dcat, z, a1, ln_a_g, ln_a_b, ln_v_g, ln_v_b, ws, ws_t, bias2d, n_seq):
    T = z.shape[0]
    S = T // n_seq
    ts = _tile(S, 512)
    n_r = S // ts

    def body(dc_ref, u_ref, v_ref, a1_ref, lag_ref, lab_ref, lvg_ref, lvb_ref, ws_ref, wst_ref, bias_ref,
             da1_ref, dz_ref, dlag_ref, dlab_ref, dlvg_ref, dlvb_ref, dws_ref, dsb_ref, dv1_s, dbias_s):
        first = (pl.program_id(0) == 0) & (pl.program_id(1) == 0)
        last = (pl.program_id(0) == n_seq - 1) & (pl.program_id(1) == n_r - 1)

        @pl.when(first)
        def _():
            for ref in (dlag_ref, dlab_ref, dlvg_ref, dlvb_ref, dws_ref, dbias_s):
                ref[...] = jnp.zeros_like(ref)

        lag = lag_ref[...]
        y, xh, r = _ln_fwd(a1_ref[...], lag, lab_ref[...])
        sig = _sigmoid(y)
        dy = dc_ref[:, 0:CA] * (sig * (1.0 + y * (1.0 - sig)))
        dlag_ref[...] += jnp.sum(dy * xh, axis=0, keepdims=True)
        dlab_ref[...] += jnp.sum(dy, axis=0, keepdims=True)
        da1_ref[...] = _ln_bwd(dy, xh, r, lag)

        tril = (lax.broadcasted_iota(jnp.int32, (CHUNK, CHUNK), 0)
                >= lax.broadcasted_iota(jnp.int32, (CHUNK, CHUNK), 1))
        for gi in range(GB):
            sl = slice(gi * DB, (gi + 1) * DB)
            lvg = lvg_ref[:, sl]
            v1, vh, vr = _ln_fwd(v_ref[:, sl], lvg, lvb_ref[:, sl])
            v1 = v1.astype(BF16)
            for c in range(ts // CHUNK):
                rs = slice(c * CHUNK, (c + 1) * CHUNK)
                v2 = _dot(ws_ref[gi], v1[rs], NN) + bias_ref[:, sl]
                dbo = dc_ref[rs, CA + gi * DB:CA + (gi + 1) * DB]
                dz_ref[rs, sl] = (dbo * v2).astype(BF16)
                dv2 = dbo * u_ref[rs, sl]
                dbias_s[:, sl] += dv2
                dv2b = dv2.astype(BF16)
                dws_ref[gi] += jnp.where(tril, _dot(dv2b, v1[rs], NT), 0.0)
                dv1_s[rs, :] = _dot(wst_ref[gi], dv2b, NN)
            dv1 = dv1_s[...]
            dlvg_ref[:, sl] += jnp.sum(dv1 * vh, axis=0, keepdims=True)
            dlvb_ref[:, sl] += jnp.sum(dv1, axis=0, keepdims=True)
            dz_ref[:, CB + gi * DB:CB + (gi + 1) * DB] = _ln_bwd(dv1, vh, vr, lvg).astype(BF16)

        @pl.when(last)
        def _():
            col = lax.broadcasted_iota(jnp.int32, (CHUNK, GB), 1)
            out = jnp.zeros((CHUNK, GB), F32)
            for gi in range(GB):
                s = jnp.sum(dbias_s[:, gi * DB:(gi + 1) * DB], axis=1, keepdims=True)
                out = out + jnp.where(col == gi, s, 0.0)
            dsb_ref[...] = out

    blk, _ = _mix_specs(ts, n_r, False)
    vec = pl.BlockSpec((1, CA), lambda b, r: (0, 0))
    mat = pl.BlockSpec((GB, CHUNK, CHUNK), lambda b, r: (0, 0, 0))
    wide = pl.BlockSpec((ts, CA + CB), lambda b, r: (b * n_r + r, 0))
    return pl.pallas_call(
        body, name="mix_bwd_rows", grid=(n_seq, n_r),
        in_specs=[wide, blk(2), blk(3), blk(0), vec, vec, vec, vec, mat, mat,
                  pl.BlockSpec((CHUNK, CB), lambda b, r: (0, 0))],
        out_specs=[blk(0), wide, vec, vec, vec, vec, mat, pl.BlockSpec((CHUNK, GB), lambda b, r: (0, 0))],
        out_shape=[jax.ShapeDtypeStruct((T, CA), F32), jax.ShapeDtypeStruct((T, 2 * CB), BF16)]
        + [jax.ShapeDtypeStruct((1, CA), F32)] * 4
        + [jax.ShapeDtypeStruct((GB, CHUNK, CHUNK), F32), jax.ShapeDtypeStruct((CHUNK, GB), F32)],
        scratch_shapes=[pltpu.VMEM((ts, DB), F32), pltpu.VMEM((CHUNK, CB), F32)],
        compiler_params=_cp("arbitrary", "arbitrary"),
    )(dcat, z, z, a1, ln_a_g, ln_a_b, ln_v_g, ln_v_b, ws, ws_t, bias2d)


def _mix_bwd_conv(da1, z, conv_w, n_seq):
    T = z.shape[0]
    S = T // n_seq
    ts = _tile(S, 512)
    n_r = S // ts
    per = ts // HALO
    shift = HALO - (CONV_WIDTH - 1)
    fold = CONV_ROWS // 8

    def body(d_ref, dh_ref, av_ref, ag_ref, avh_ref, agh_ref, cw_ref,
             dz_ref, dcw_ref, dcb_ref, a0_s, d1_s, da0_s, dw8_s):
        first = (pl.program_id(0) == 0) & (pl.program_id(1) == 0)
        last = (pl.program_id(0) == n_seq - 1) & (pl.program_id(1) == n_r - 1)

        @pl.when(first)
        def _():
            dw8_s[...] = jnp.zeros_like(dw8_s)
            dcb_ref[...] = jnp.zeros_like(dcb_ref)

        _glu_with_halo(av_ref, ag_ref, avh_ref, agh_ref, a0_s, pl.program_id(1) == 0, ts)
        d1_s[0, 0:ts, :] = d_ref[...]
        d1_s[0, ts:ts + HALO, :] = jnp.where(pl.program_id(1) == n_r - 1, 0.0, dh_ref[...])
        _shifted_copies(d1_s)
        dcb_ref[...] += jnp.sum(d_ref[...], axis=0, keepdims=True)
        for rb in range(ts // CONV_ROWS):
            base = rb * CONV_ROWS
            dcur = d1_s[0, base:base + CONV_ROWS, :]
            acc = jnp.zeros((CONV_ROWS, CA), F32)
            for k in range(CONV_WIDTH):
                back = CONV_WIDTH - 1 - k
                acc = acc + cw_ref[k:k + 1, :] * _shifted(d1_s, base + back, CONV_ROWS)
                prod = dcur * _shifted(a0_s, base + shift + k, CONV_ROWS)
                part = prod[0:8]
                for f in range(1, fold):
                    part = part + prod[8 * f:8 * f + 8]
                dw8_s[k] += part
            da0_s[base:base + CONV_ROWS, :] = acc
        da0 = da0_s[...]
        sig = _sigmoid(ag_ref[...])
        dz_ref[:, 0:CA] = (da0 * sig).astype(BF16)
        dz_ref[:, CA:2 * CA] = (da0 * av_ref[...] * sig * (1.0 - sig)).astype(BF16)

        @pl.when(last)
        def _():
            for k in range(CONV_WIDTH):
                dcw_ref[k:k + 1, :] = jnp.sum(dw8_s[k], axis=0, keepdims=True)

    blk, left = _mix_specs(ts, n_r, True)
    n_halo_blocks = T // HALO
    right = pl.BlockSpec((HALO, CA), lambda b, r: (jnp.minimum((b * n_r + r + 1) * per, n_halo_blocks - 1), 0))
    return pl.pallas_call(
        body, name="mix_bwd_conv", grid=(n_seq, n_r),
        in_specs=[blk(0), right, blk(0), blk(1), left(0), left(1),
                  pl.BlockSpec((CONV_WIDTH, CA), lambda b, r: (0, 0))],
        out_specs=[pl.BlockSpec((ts, 2 * CA), lambda b, r: (b * n_r + r, 0)),
                   pl.BlockSpec((CONV_WIDTH, CA), lambda b, r: (0, 0)), pl.BlockSpec((1, CA), lambda b, r: (0, 0))],
        out_shape=[jax.ShapeDtypeStruct((T, 2 * CA), BF16), jax.ShapeDtypeStruct((CONV_WIDTH, CA), F32),
                   jax.ShapeDtypeStruct((1, CA), F32)],
        scratch_shapes=[pltpu.VMEM((SUBLANES, HALO + ts, CA), F32), pltpu.VMEM((SUBLANES, ts + HALO, CA), F32),
                        pltpu.VMEM((ts, CA), F32), pltpu.VMEM((CONV_WIDTH, 8, CA), F32)],
        compiler_params=_cp("arbitrary", "arbitrary"),
    )(da1, da1, z, z, z, z, conv_w)


def _row_tile(R, want):
    t = min(R, want)
    t -= t % 8
    while t > 8 and R % t:
        t -= 8
    return t if t >= 8 and R % t == 0 else R


def _adam_step(w, g, m, v):
    nm = ADAM_B1 * m + (1.0 - ADAM_B1) * g
    nv = ADAM_B2 * v + (1.0 - ADAM_B2) * (g * g)
    m_hat = nm / (1.0 - ADAM_B1 ** ADAM_STEP)
    v_hat = nv / (1.0 - ADAM_B2 ** ADAM_STEP)
    return -ADAM_LR * (m_hat / (jnp.sqrt(v_hat) + ADAM_EPS) + ADAM_WD * w), nm, nv


def _adamw_parts(parts, w, m, v, name):
    L, n, C = w.shape
    assert len(parts) == L
    tr = _row_tile(n, 192)
    n_i = n // tr

    def body(*refs):
        p_refs = refs[:L]
        w_ref, m_ref, v_ref, g_ref, d_ref, nm_ref, nv_ref = refs[L:]
        for k in range(L):
            @pl.when(pl.program_id(0) == k)
            def _(k=k):
                acc = p_refs[k][0].astype(F32)
                for s in range(1, N_DEV):
                    acc = acc + p_refs[k][s].astype(F32)
                g_ref[...] = acc

        d_ref[...], nm_ref[...], nv_ref[...] = _adam_step(w_ref[...], g_ref[...], m_ref[...], v_ref[...])

    def part_spec(k):
        return pl.BlockSpec((N_DEV, tr, C),
                            lambda l, i: (0, jnp.where(l == k, i, jnp.where(l < k, 0, n_i - 1)), 0))

    blk = pl.BlockSpec((None, tr, C), lambda l, i: (l, i, 0))
    return pl.pallas_call(
        body, name=name, grid=(L, n_i),
        in_specs=[part_spec(k) for k in range(L)] + [blk] * 3, out_specs=[blk] * 4,
        out_shape=[jax.ShapeDtypeStruct((L, n, C), F32)] * 4,
        compiler_params=_cp("arbitrary", "arbitrary"),
    )(*parts, w, m, v)


def _adamw(w, g, m, v, name):
    R, C = w.shape
    tr = _row_tile(R, 256)

    def body(w_ref, g_ref, m_ref, v_ref, d_ref, nm_ref, nv_ref):
        d_ref[...], nm_ref[...], nv_ref[...] = _adam_step(w_ref[...], g_ref[...], m_ref[...], v_ref[...])

    blk = pl.BlockSpec((tr, C), lambda i: (i, 0))
    return pl.pallas_call(
        body, name=name, grid=(R // tr,),
        in_specs=[blk] * 4, out_specs=[blk] * 3,
        out_shape=[jax.ShapeDtypeStruct((R, C), F32)] * 3,
        compiler_params=_cp("parallel"),
    )(w, g, m, v)


def _me():
    return lax.axis_index("x"), lax.axis_index("y"), lax.axis_index("c")


def _block_rows(ref, dev, n):
    start = (4 * dev[0] + 2 * dev[1] + dev[2]) * n
    if len(ref.shape) == 2:
        return ref.at[pl.ds(start, n), :]
    return ref.at[:, pl.ds(start, n), :]


def _all_gather(shards):
    na = len(shards)
    ns = [s.shape[-2] for s in shards]

    def body(*refs):
        ins, outs = refs[:na], refs[na:2 * na]
        send_sems, recv_sems, local_sems = refs[2 * na:]
        x, y, c = _me()
        me, sibling = (x, y, c), (x, y, 1 - c)
        chips = [(1 - x, y), (x, 1 - y), (1 - x, 1 - y)]

        def copy(a, k, block, to, src=None):
            dst = _block_rows(outs[a], block, ns[a])
            return pltpu.make_async_remote_copy(
                src_ref=dst if src is None else src, dst_ref=dst,
                send_sem=send_sems.at[a, k], recv_sem=recv_sems.at[a, k], device_id=to, device_id_type=MESH)

        mine = [pltpu.make_async_copy(ins[a], _block_rows(outs[a], me, ns[a]), local_sems.at[a]) for a in range(na)]
        for cp in mine:
            cp.start()
        first = []
        for a in range(na):
            first.append(copy(a, 0, me, sibling, src=ins[a]))
            first += [copy(a, 1 + j, me, (*chip, c), src=ins[a]) for j, chip in enumerate(chips)]
        for cp in first:
            cp.start()
        passed = []
        for j, chip in enumerate(chips):
            for a in range(na):
                copy(a, 1 + j, (*chip, c), me).wait_recv()
                fwd = copy(a, 4 + j, (*chip, c), sibling)
                fwd.start()
                passed.append(fwd)
        for a in range(na):
            copy(a, 0, sibling, me).wait_recv()
            for j, chip in enumerate(chips):
                copy(a, 4 + j, (*chip, 1 - c), me).wait_recv()
        for cp in first + passed:
            cp.wait_send()
        for cp in mine:
            cp.wait()

    out_shape = [jax.ShapeDtypeStruct(s.shape[:-2] + (N_DEV * s.shape[-2], s.shape[-1]), s.dtype) for s in shards]
    return pl.pallas_call(
        body, name="weights_all_gather",
        in_specs=[ANY] * na, out_specs=[ANY] * na, out_shape=out_shape,
        scratch_shapes=[pltpu.SemaphoreType.DMA((na, 7)), pltpu.SemaphoreType.DMA((na, 7)),
                        pltpu.SemaphoreType.DMA((na,))],
    )(*shards)


def _split_copies(gather, srcs, lands, send_sems, recv_sems, ns):
    x, y, c = _me()
    me = (x, y, c)
    my_slot = 4 * x + 2 * y + c
    copies = []
    for mask in range(1, N_DEV):
        peer = (x ^ (mask >> 2), y ^ ((mask >> 1) & 1), c ^ (mask & 1))
        for a in range(len(srcs)):
            if gather:
                src, dst = srcs[a], _block_rows(lands[a], me, ns[a])
            else:
                src, dst = _block_rows(srcs[a], peer, ns[a]), lands[a].at[my_slot]
            sem = a * (N_DEV - 1) + mask - 1
            copies.append(pltpu.make_async_remote_copy(
                src_ref=src, dst_ref=dst, send_sem=send_sems.at[sem], recv_sem=recv_sems.at[sem],
                device_id=peer, device_id_type=MESH))
    return copies


HBM_SPEC = pl.BlockSpec(memory_space=pltpu.HBM)
SEM_SPEC = pl.BlockSpec(memory_space=pltpu.SEMAPHORE)


def _split_start(gather, srcs, name, dep=None):
    na = len(srcs)
    x, y, c = _me()
    mine = 4 * x + 2 * y + c
    if gather:
        ns = [s.shape[-2] for s in srcs]
        lands = [lax.dynamic_update_slice(
            lax.empty(s.shape[:-2] + (N_DEV * s.shape[-2], s.shape[-1]), s.dtype), s,
            (0,) * (s.ndim - 2) + (mine * s.shape[-2], 0)) for s in srcs]
    else:
        ns = [s.shape[-2] // N_DEV for s in srcs]
        lands = [lax.dynamic_update_slice(
            lax.empty((N_DEV, n, s.shape[-1]), s.dtype),
            lax.dynamic_slice(s, (mine * n, 0), (n, s.shape[-1]))[None], (mine, 0, 0)) for s, n in zip(srcs, ns)]
    n_in = 2 * na + (dep is not None)

    def body(*refs):
        send_sems, recv_sems = refs[n_in], refs[n_in + 1]
        for cp in _split_copies(gather, refs[:na], refs[na:2 * na], send_sems, recv_sems, ns):
            cp.start()
        refs[-1][...] = jnp.zeros_like(refs[-1])

    hbm = lambda a: pltpu.with_memory_space_constraint(a, pltpu.HBM)
    args = [hbm(a) for a in srcs] + [hbm(a) for a in lands] + ([dep] if dep is not None else [])
    out = pl.pallas_call(
        body, name=name,
        in_specs=[HBM_SPEC] * (2 * na) + ([ANY] if dep is not None else []),
        out_specs=[SEM_SPEC, SEM_SPEC] + [HBM_SPEC] * (2 * na) + [pl.BlockSpec(memory_space=pltpu.VMEM)],
        out_shape=[pltpu.SemaphoreType.DMA((na * (N_DEV - 1),)), pltpu.SemaphoreType.DMA((na * (N_DEV - 1),))]
        + [pltpu.HBM(a.shape, a.dtype) for a in srcs + lands] + [jax.ShapeDtypeStruct((8, LANES), F32)],
        input_output_aliases={i: 2 + i for i in range(2 * na)},
        compiler_params=pltpu.CompilerParams(has_side_effects=pltpu.SideEffectType.DATAFLOW_SIDE_EFFECTING),
    )(*args)
    return (gather, ns, out[0], out[1], list(out[2:2 + na]), list(out[2 + na:2 + 2 * na])), out[-1]


def _split_wait(handle, after, name):
    gather, ns, send, recv, srcs, lands = handle
    na = len(srcs)

    def body(*refs):
        send_sems, recv_sems = refs[2 * na], refs[2 * na + 1]
        for cp in _split_copies(gather, refs[:na], refs[na:2 * na], send_sems, recv_sems, ns):
            cp.wait_send()
            cp.wait_recv()

    out = pl.pallas_call(
        body, name=name,
        in_specs=[HBM_SPEC] * (2 * na) + [SEM_SPEC, SEM_SPEC, ANY],
        out_specs=[HBM_SPEC] * (2 * na),
        out_shape=[pltpu.HBM(a.shape, a.dtype) for a in srcs + lands],
        input_output_aliases={i: i for i in range(2 * na)},
        compiler_params=pltpu.CompilerParams(has_side_effects=pltpu.SideEffectType.DATAFLOW_SIDE_EFFECTING),
    )(*srcs, *lands, send, recv, after)
    return list(out[:na]), list(out[na:])


def _sum_blocks(parts):
    n, R, C = parts.shape

    def body(p_ref, o_ref):
        acc = p_ref[0]
        for k in range(1, n):
            acc = acc + p_ref[k]
        o_ref[...] = acc

    return pl.pallas_call(
        body, name="small_sum",
        in_specs=[pl.BlockSpec(memory_space=pltpu.VMEM)], out_specs=pl.BlockSpec(memory_space=pltpu.VMEM),
        out_shape=jax.ShapeDtypeStruct((R, C), F32),
        compiler_params=pltpu.CompilerParams(vmem_limit_bytes=VMEM_LIMIT),
    )(parts)


def _pack(arrays):
    flat = jnp.concatenate([a.reshape(-1) for a in arrays])
    pad = (-flat.shape[0]) % (8 * LANES)
    return jnp.pad(flat, (0, pad)).reshape(-1, LANES)


def _unpack(buf, shapes):
    flat = buf.reshape(-1)
    out, off = [], 0
    for s in shapes:
        n = 1
        for d in s:
            n *= d
        out.append(flat[off:off + n].reshape(s))
        off += n
    return out


def _ffn_index(layer, second):
    return (2 * layer + second) * 3


def kernel(x, g_ffn1, w_ffn1_gate, w_ffn1_up, w_ffn1_down, g_mix, w_in_ab, conv_w, conv_b, ln_a_g, ln_a_b, ln_v_g, ln_v_b, sp_w, sp_b, w_out_ab, w_qkv, w_o, g_ffn2, w_ffn2_gate, w_ffn2_up, w_ffn2_down, g_final, loss_target, m_g_ffn1, m_w_ffn1_gate, m_w_ffn1_up, m_w_ffn1_down, m_g_mix, m_w_in_ab, m_conv_w, m_conv_b, m_ln_a_g, m_ln_a_b, m_ln_v_g, m_ln_v_b, m_sp_w, m_sp_b, m_w_out_ab, m_w_qkv, m_w_o, m_g_ffn2, m_w_ffn2_gate, m_w_ffn2_up, m_w_ffn2_down, m_g_final, v_g_ffn1, v_w_ffn1_gate, v_w_ffn1_up, v_w_ffn1_down, v_g_mix, v_w_in_ab, v_conv_w, v_conv_b, v_ln_a_g, v_ln_a_b, v_ln_v_g, v_ln_v_b, v_sp_w, v_sp_b, v_w_out_ab, v_w_qkv, v_w_o, v_g_ffn2, v_w_ffn2_gate, v_w_ffn2_up, v_w_ffn2_down, v_g_final):
    n_seq, S, D = x.shape
    T = n_seq * S
    depth = g_ffn1.shape[0]
    assert depth == 2 and D == D_MODEL
    my_block = 4 * lax.axis_index("x") + 2 * lax.axis_index("y") + lax.axis_index("c")

    ffn_parts = []
    for l in range(depth):
        for gate, up, down in ((w_ffn1_gate, w_ffn1_up, w_ffn1_down), (w_ffn2_gate, w_ffn2_up, w_ffn2_down)):
            ffn_parts += [gate[l].T, up[l].T, down[l]]
    ffn_shard = lambda k: jnp.stack(ffn_parts[3 * k:3 * k + 3]).astype(BF16)
    conv_w_pad = jnp.zeros((HALO, conv_w.shape[2]), F32).at[:CONV_WIDTH].set(conv_w[0]).T
    w_ffn = [None] * (2 * depth)
    w_ffn[0], conv_w_t = _all_gather([ffn_shard(0), conv_w_pad])
    conv_w_full = conv_w_t.T[:CONV_WIDTH]
    shards_b = [w_out_ab[0].astype(BF16), ffn_shard(1)]
    shards_d = [w_qkv[0].T.astype(BF16), w_o[0].astype(BF16), ffn_shard(3)]
    gather_a, token = _split_start(True, [w_in_ab[0].T.astype(BF16)], "gather_a_start", dep=conv_w_t)
    gather_b, token = _split_start(True, shards_b, "gather_b_start", dep=token)
    gather_c, token = _split_start(True, [ffn_shard(2)], "gather_c_start", dep=token)
    gather_d, token = _split_start(True, shards_d, "gather_d_start", dep=token)

    def gathered(handle, after, name):
        return _split_wait(handle, after, name)[1]

    row = lambda a: a.reshape(1, -1)
    tril = jnp.tril(jnp.ones((CHUNK, CHUNK), dtype=bool))
    ws = jnp.where(tril[None], sp_w[0], 0.0).astype(BF16)
    ws_t = jnp.swapaxes(ws, 1, 2)
    bias2d = jnp.repeat(sp_b[0].T, DB, axis=1)
    conv_b2, lag, lab = row(conv_b[0]), row(ln_a_g[0]), row(ln_a_b[0])
    lvg, lvb = row(ln_v_g[0]), row(ln_v_b[0])

    x0 = x.reshape(T, D)
    target = loss_target.reshape(T, D)
    saved = []
    xc = x0
    for l in range(depth):
        xa, a1, b1, h1 = _ffn_fwd(xc, row(g_ffn1[l]), w_ffn[2 * l], 0, f"ffn1_fwd_{l}", dep=token)
        if l % 2 == 0:
            w_in_t, = gathered(gather_a, xa, "gather_a_wait")
            z, hm = _mm_nt(xa, w_in_t, row(g_mix[l]), F32, "mix_in_proj")
            cat, conv_out = _mix_fwd(z, conv_w_full, conv_b2, lag, lab, lvg, lvb, ws, bias2d, n_seq)
            w_out, w_ffn[1] = gathered(gather_b, cat, "gather_b_wait")
            xb = _mm_nn_res(cat, w_out, xa, "mix_out_proj")
            mixer = (z, hm, cat, conv_out)
        else:
            w_qkv_t, w_o_full, w_ffn[3] = gathered(gather_d, xa, "gather_d_wait")
            qkv, hm = _mm_nt(xa, w_qkv_t, row(g_mix[l]), BF16, "qkv_proj")
            o, att = _attn_fwd(qkv, n_seq)
            xb = _mm_nn_res(o, w_o_full, xa, "attn_out_proj")
            mixer = (qkv, hm, o, att)
        xn, a2, b2, h2 = _ffn_fwd(xb, row(g_ffn2[l]), w_ffn[2 * l + 1], 0, f"ffn2_fwd_{l}")
        saved.append((xc, a1, b1, h1, xa, mixer, xb, a2, b2, h2))
        xc = xn
        if l == 0:
            w_ffn[2], = gathered(gather_c, xc, "gather_c_wait")

    g, loss_part, dg_final = _loss_head(xc, row(g_final), target)

    dg_ffn1, dg_ffn2, dg_mix = [None] * depth, [None] * depth, [None] * depth
    exchanges = {}
    token = None

    def ffn_back(g, xin, gvec, a, b, h, k, tag, token):
        g, dg, da, db, s, gh = _ffn_bwd(g, xin, gvec, a, b, w_ffn[k], 0, f"ffn{tag}_bwd", dep=token)
        if k == 0:
            return g, dg, (da, db, s, gh, h)
        dws = [_mm_tn(da, h, f"dw_gate{tag}"), _mm_tn(db, h, f"dw_up{tag}"), _mm_tn(s, gh, f"dw_down{tag}")]
        exchanges[f"ffn{k}"], token = _split_start(False, dws, f"exchange_ffn{tag}_start")
        return g, dg, token

    for l in reversed(range(depth)):
        xin, a1, b1, h1, xa, mixer, xb, a2, b2, h2 = saved[l]
        g, dg_ffn2[l], token = ffn_back(g, xb, row(g_ffn2[l]), a2, b2, h2, 2 * l + 1, f"2_{l}", token)
        if l % 2 == 0:
            z, hm, cat, conv_out = mixer
            dcat = _mm_nt(g, w_out, None, F32, "mix_out_bwd", dep=token)
            d_w_out = _mm_tn(cat, g, "dw_out")
            (da1, dz_uv, d_lag, d_lab, d_lvg, d_lvb, d_ws, d_sb) = _mix_bwd_rows(
                dcat, z, conv_out, lag, lab, lvg, lvb, ws, ws_t, bias2d, n_seq)
            dz_a, d_cw, d_cb = _mix_bwd_conv(da1, z, conv_w_full, n_seq)
            d_w_in_t = jnp.concatenate([_mm_tn(dz_a, hm, "dw_in_a"), _mm_tn(dz_uv, hm, "dw_in_uv")])
            exchanges["mix"], token = _split_start(False, [d_w_out, d_w_in_t], "exchange_mix_start")
            g, dg_mix[l] = _mm_nn_rmsbwd([dz_a, dz_uv], w_in_t, xa, row(g_mix[l]), g, "mix_in_bwd", dep=token)
        else:
            qkv, hm, o, att = mixer
            do = _mm_nt(g, w_o_full, None, BF16, "attn_out_bwd", dep=token)
            d_w_o = _mm_tn(o, g, "dw_o")
            dq, dk, dv = _attn_bwd(qkv, do, att, n_seq)
            d_w_qkv_t = jnp.concatenate([_mm_tn(dq, hm, "dw_q"), _mm_tn(dk, hm, "dw_k"), _mm_tn(dv, hm, "dw_v")])
            exchanges["attn"], token = _split_start(False, [d_w_o, d_w_qkv_t], "exchange_attn_start")
            g, dg_mix[l] = _mm_nn_rmsbwd([dq, dk, dv], w_qkv_t, xa, row(g_mix[l]), g, "qkv_bwd", dep=token)
        g, dg_ffn1[l], token = ffn_back(g, xin, row(g_ffn1[l]), a1, b1, h1, 2 * l, f"1_{l}", token)
    grad_x = g.reshape(n_seq, S, D)

    small = [jnp.concatenate(dg_ffn1), jnp.concatenate(dg_mix), d_cw, d_cb, d_lag, d_lab, d_lvg, d_lvb,
             jnp.where(tril[None], d_ws, 0.0), d_sb.T, jnp.concatenate(dg_ffn2), dg_final, loss_part[:, :1]]
    small_shapes = [(depth, D), (depth, D), (CONV_WIDTH, CA), (1, CA), (1, CA), (1, CA), (1, GB, DB), (1, GB, DB),
                    (1, GB, CHUNK, CHUNK), (1, GB, CHUNK), (depth, D), (D,), ()]
    da, db, s, gh, h = token
    small_gather, token = _split_start(True, [_pack(small)], "small_gather_start")

    for which, lhs, rhs in ((2, s, gh), (1, db, h), (0, da, h)):
        dw = _mm_tn(lhs, rhs, f"dw_ffn0_{which}", dep=token, out_dtype=BF16)
        exchanges[f"ffn0_{which}"], token = _split_start(False, [dw], f"exchange_ffn0_{which}_start")

    small_all, = _split_wait(small_gather, token, "small_gather_wait")[1]
    red = _unpack(_sum_blocks(small_all.reshape(N_DEV, -1, LANES)), small_shapes)
    (gr_g_ffn1, gr_g_mix, gr_cw_full, gr_cb, gr_lag, gr_lab, gr_lvg, gr_lvb, gr_sp_w, gr_sp_b,
     gr_g_ffn2, gr_g_final, loss) = red
    n_cw = conv_w.shape[2]
    gr_cw = lax.dynamic_slice(gr_cw_full, (0, my_block * n_cw), (CONV_WIDTH, n_cw))[None]

    def landed(key, after):
        return _split_wait(exchanges[key], after, f"exchange_{key}_wait")[1]

    parts_ffn = [None] * (6 * depth)
    for k in range(1, 2 * depth):
        parts_ffn[3 * k:3 * k + 3] = landed(f"ffn{k}", token)
    parts_out, parts_in = landed("mix", token)
    parts_o, parts_qkv = landed("attn", token)

    grads = {
        "g_ffn1": gr_g_ffn1, "g_mix": gr_g_mix, "conv_w": gr_cw, "conv_b": gr_cb, "ln_a_g": gr_lag,
        "ln_a_b": gr_lab, "ln_v_g": gr_lvg, "ln_v_b": gr_lvb, "sp_w": gr_sp_w, "sp_b": gr_sp_b,
        "g_ffn2": gr_g_ffn2, "g_final": gr_g_final,
    }
    weights = dict(g_ffn1=g_ffn1, w_ffn1_gate=w_ffn1_gate, w_ffn1_up=w_ffn1_up, w_ffn1_down=w_ffn1_down, g_mix=g_mix,
                   w_in_ab=w_in_ab, conv_w=conv_w, conv_b=conv_b, ln_a_g=ln_a_g, ln_a_b=ln_a_b, ln_v_g=ln_v_g,
                   ln_v_b=ln_v_b, sp_w=sp_w, sp_b=sp_b, w_out_ab=w_out_ab, w_qkv=w_qkv, w_o=w_o, g_ffn2=g_ffn2,
                   w_ffn2_gate=w_ffn2_gate, w_ffn2_up=w_ffn2_up, w_ffn2_down=w_ffn2_down, g_final=g_final)
    m_in = dict(g_ffn1=m_g_ffn1, w_ffn1_gate=m_w_ffn1_gate, w_ffn1_up=m_w_ffn1_up, w_ffn1_down=m_w_ffn1_down,
                g_mix=m_g_mix, w_in_ab=m_w_in_ab, conv_w=m_conv_w, conv_b=m_conv_b, ln_a_g=m_ln_a_g, ln_a_b=m_ln_a_b,
                ln_v_g=m_ln_v_g, ln_v_b=m_ln_v_b, sp_w=m_sp_w, sp_b=m_sp_b, w_out_ab=m_w_out_ab, w_qkv=m_w_qkv,
                w_o=m_w_o, g_ffn2=m_g_ffn2, w_ffn2_gate=m_w_ffn2_gate, w_ffn2_up=m_w_ffn2_up,
                w_ffn2_down=m_w_ffn2_down, g_final=m_g_final)
    v_in = dict(g_ffn1=v_g_ffn1, w_ffn1_gate=v_w_ffn1_gate, w_ffn1_up=v_w_ffn1_up, w_ffn1_down=v_w_ffn1_down,
                g_mix=v_g_mix, w_in_ab=v_w_in_ab, conv_w=v_conv_w, conv_b=v_conv_b, ln_a_g=v_ln_a_g, ln_a_b=v_ln_a_b,
                ln_v_g=v_ln_v_g, ln_v_b=v_ln_v_b, sp_w=v_sp_w, sp_b=v_sp_b, w_out_ab=v_w_out_ab, w_qkv=v_w_qkv,
                w_o=v_w_o, g_ffn2=v_g_ffn2, w_ffn2_gate=v_w_ffn2_gate, w_ffn2_up=v_w_ffn2_up,
                w_ffn2_down=v_w_ffn2_down, g_final=v_g_final)
    names = list(weights)
    grads = {n: grads[n].reshape(weights[n].shape) for n in grads}

    delta, new_m, new_v = {}, {}, {}

    def adamw_big(n, parts):
        if weights[n].shape[-1] == D:
            view = back = lambda a: a
        else:
            view = back = lambda a: jnp.swapaxes(a, 1, 2)
        out = _adamw_parts(parts, view(weights[n]), view(m_in[n]), view(v_in[n]), f"adamw_{n}")
        grads[n], delta[n], new_m[n], new_v[n] = [back(a) for a in out]

    adamw_big("w_in_ab", [parts_in])
    adamw_big("w_out_ab", [parts_out])
    adamw_big("w_qkv", [parts_qkv])
    adamw_big("w_o", [parts_o])
    kinds = ("gate", "up", "down")
    for which, kind in enumerate(kinds):
        adamw_big(f"w_ffn2_{kind}", [parts_ffn[_ffn_index(l, 1) + which] for l in range(depth)])
    big = [n for n in names if n.startswith("w_")]
    after = jnp.concatenate([delta[n].reshape(-1)[:1] for n in big if n in delta]).reshape(1, -1)
    for which in (2, 1, 0):
        parts_ffn[which], = landed(f"ffn0_{which}", after)
    for which, kind in enumerate(kinds):
        adamw_big(f"w_ffn1_{kind}", [parts_ffn[_ffn_index(l, 0) + which] for l in range(depth)])
    little = [n for n in names if n not in big]
    shapes = [weights[n].shape for n in little]
    d, nm, nv = _adamw(_pack([weights[n] for n in little]), _pack([grads[n] for n in little]),
                       _pack([m_in[n] for n in little]), _pack([v_in[n] for n in little]), "adamw_small")
    for n, dd, mm, vv in zip(little, _unpack(d, shapes), _unpack(nm, shapes), _unpack(nv, shapes)):
        delta[n], new_m[n], new_v[n] = dd, mm, vv

    return (loss, grad_x, *[grads[n] for n in names], *[delta[n] for n in names],
            *[new_m[n] for n in names], *[new_v[n] for n in names])
```

```python
import jax
import jax.numpy as jnp
from jax import lax
from jax.experimental import pallas as pl
from jax.experimental.pallas import tpu as pltpu

F32 = jnp.float32
BF16 = jnp.bfloat16

D_MODEL = 1024
CA = 512
CB = 512
GB = 4
DB = 128
CHUNK = 128
CONV_WIDTH = 31
N_HEADS = 16
HEAD_DIM = 64
EPS = 1e-6
N_DEV = 8
LANES = 128
SUBLANES = 8
QB = 128
ATT_TQ = 1024
FFN_TN = 2816
FFN_TM = 256
HALO = 32
CONV_ROWS = 32
ATT_SCALE = HEAD_DIM ** -0.5

ADAM_LR = 0.001
ADAM_B1 = 0.9
ADAM_B2 = 0.999
ADAM_EPS = 1e-08
ADAM_WD = 0.01
ADAM_STEP = 10

NT = (((1,), (1,)), ((), ()))
NN = (((1,), (0,)), ((), ()))
TN = (((0,), (0,)), ((), ()))
MESH = pl.DeviceIdType.MESH
ANY = pl.BlockSpec(memory_space=pl.ANY)
VMEM_LIMIT = 60 * 1024 * 1024


def _dot(a, b, dims):
    return lax.dot_general(a, b, dims, preferred_element_type=F32)


def _cp(*sem):
    return pltpu.CompilerParams(dimension_semantics=sem, vmem_limit_bytes=VMEM_LIMIT)


def _pcall(body, *, in_specs, args, dep=None, **kw):
    if dep is not None:
        n_in = len(in_specs)
        inner = body

        def body(*refs):
            inner(*refs[:n_in], *refs[n_in + 1:])

        in_specs = list(in_specs) + [ANY]
        args = tuple(args) + (dep,)
    return pl.pallas_call(body, in_specs=list(in_specs), **kw)(*args)


def _tile(n, want):
    if n <= want:
        return n
    t = want - want % LANES
    while t > LANES and n % t:
        t -= LANES
    assert n % t == 0, (n, want)
    return t


def _sigmoid(x):
    return 0.5 * jnp.tanh(0.5 * x) + 0.5


def _rstd(x):
    return lax.rsqrt(jnp.mean(x * x, axis=-1, keepdims=True) + EPS)


def _rms_bwd(x, g, dh):
    r = _rstd(x)
    u = dh * g
    dx = r * (u - x * (r * r) * jnp.mean(u * x, axis=-1, keepdims=True))
    dg = jnp.sum(dh * x * r, axis=0, keepdims=True)
    return dx, dg


def _ln_fwd(x, g, b):
    mu = jnp.mean(x, axis=-1, keepdims=True)
    xc = x - mu
    r = lax.rsqrt(jnp.mean(xc * xc, axis=-1, keepdims=True) + EPS)
    xh = xc * r
    return xh * g + b, xh, r


def _ln_bwd(dy, xh, r, g):
    dxh = dy * g
    return r * (dxh - jnp.mean(dxh, axis=-1, keepdims=True)
                - xh * jnp.mean(dxh * xh, axis=-1, keepdims=True))


def _ffn_fwd(x, g, wall, base, name, dep=None):
    T, D = x.shape
    F = wall.shape[1]
    tm, tn = _tile(T, FFN_TM), _tile(F, FFN_TN)
    n_j = F // tn

    def body(x_ref, g_ref, wg_ref, wu_ref, wd_ref, xo_ref, a_ref, b_ref, h_ref, acc_ref):
        j = pl.program_id(1)

        @pl.when(j == 0)
        def _():
            xv = x_ref[...]
            h_ref[...] = (xv * _rstd(xv) * g_ref[...]).astype(BF16)
            acc_ref[...] = jnp.zeros_like(acc_ref)

        h = h_ref[...]
        a = _dot(h, wg_ref[...], NT)
        b = _dot(h, wu_ref[...], NT)
        a_ref[...] = a.astype(BF16)
        b_ref[...] = b.astype(BF16)
        s = (a * _sigmoid(a) * b).astype(BF16)
        acc_ref[...] += _dot(s, wd_ref[...], NN)

        @pl.when(j == n_j - 1)
        def _():
            xo_ref[...] = x_ref[...] + 0.5 * acc_ref[...]

    single = pl.Buffered(1) if n_j == 1 else None
    wspec = lambda k: pl.BlockSpec((None, tn, D), lambda i, j: (base + k, j, 0), pipeline_mode=single)
    return _pcall(
        body, name=name, grid=(T // tm, n_j), dep=dep, args=(x, g, wall, wall, wall),
        in_specs=[pl.BlockSpec((tm, D), lambda i, j: (i, 0)), pl.BlockSpec((1, D), lambda i, j: (0, 0)),
                  wspec(0), wspec(1), wspec(2)],
        out_specs=[pl.BlockSpec((tm, D), lambda i, j: (i, 0)), pl.BlockSpec((tm, tn), lambda i, j: (i, j)),
                   pl.BlockSpec((tm, tn), lambda i, j: (i, j)), pl.BlockSpec((tm, D), lambda i, j: (i, 0))],
        out_shape=[jax.ShapeDtypeStruct((T, D), F32), jax.ShapeDtypeStruct((T, F), BF16),
                   jax.ShapeDtypeStruct((T, F), BF16), jax.ShapeDtypeStruct((T, D), BF16)],
        scratch_shapes=[pltpu.VMEM((tm, D), F32)],
        compiler_params=_cp("parallel", "arbitrary"),
    )


def _ffn_bwd(go, x, g, a, b, wall, base, name, dep=None):
    T, D = x.shape
    F = wall.shape[1]
    tm, tn = _tile(T, FFN_TM), _tile(F, FFN_TN)
    n_j = F // tn

    def body(go_ref, x_ref, g_ref, a_ref, b_ref, wg_ref, wu_ref, wd_ref,
             gx_ref, dg_ref, da_ref, db_ref, s_ref, gh_ref, acc_ref):
        i, j = pl.program_id(0), pl.program_id(1)

        @pl.when(j == 0)
        def _():
            gh_ref[...] = (0.5 * go_ref[...]).astype(BF16)
            acc_ref[...] = jnp.zeros_like(acc_ref)

        @pl.when((i == 0) & (j == 0))
        def _():
            dg_ref[...] = jnp.zeros_like(dg_ref)

        ds = _dot(gh_ref[...], wd_ref[...], NT)
        av = a_ref[...].astype(F32)
        bv = b_ref[...].astype(F32)
        sig = _sigmoid(av)
        sl = av * sig
        dab = ((ds * bv) * (sig + sl * (1.0 - sig))).astype(BF16)
        dbb = (ds * sl).astype(BF16)
        s_ref[...] = (sl * bv).astype(BF16)
        da_ref[...] = dab
        db_ref[...] = dbb
        acc_ref[...] += _dot(dab, wg_ref[...], NN) + _dot(dbb, wu_ref[...], NN)

        @pl.when(j == n_j - 1)
        def _():
            dx, dg = _rms_bwd(x_ref[...], g_ref[...], acc_ref[...])
            gx_ref[...] = go_ref[...] + dx
            dg_ref[...] += dg

    single = pl.Buffered(1) if n_j == 1 else None
    wspec = lambda k: pl.BlockSpec((None, tn, D), lambda i, j: (base + k, j, 0), pipeline_mode=single)
    row = pl.BlockSpec((tm, D), lambda i, j: (i, 0))
    hid = pl.BlockSpec((tm, tn), lambda i, j: (i, j))
    vec = pl.BlockSpec((1, D), lambda i, j: (0, 0))
    return _pcall(
        body, name=name, grid=(T // tm, n_j), dep=dep, args=(go, x, g, a, b, wall, wall, wall),
        in_specs=[row, row, vec, hid, hid, wspec(0), wspec(1), wspec(2)],
        out_specs=[row, vec, hid, hid, hid, row],
        out_shape=[jax.ShapeDtypeStruct((T, D), F32), jax.ShapeDtypeStruct((1, D), F32),
                   jax.ShapeDtypeStruct((T, F), BF16), jax.ShapeDtypeStruct((T, F), BF16),
                   jax.ShapeDtypeStruct((T, F), BF16), jax.ShapeDtypeStruct((T, D), BF16)],
        scratch_shapes=[pltpu.VMEM((tm, D), F32)],
        compiler_params=_cp("arbitrary", "arbitrary"),
    )


def _mm_tn(a, b, name, dep=None, out_dtype=F32):
    T, M = a.shape
    N = b.shape[1]
    tmm, tk = _tile(M, 1536), _tile(T, 2048)
    n_k = T // tk
    narrow = out_dtype != F32

    def body(a_ref, b_ref, o_ref, *scratch):
        acc_ref = scratch[0] if narrow else o_ref

        @pl.when(pl.program_id(1) == 0)
        def _():
            acc_ref[...] = jnp.zeros_like(acc_ref)

        acc_ref[...] += _dot(a_ref[...].astype(BF16), b_ref[...].astype(BF16), TN)
        if narrow:
            @pl.when(pl.program_id(1) == n_k - 1)
            def _():
                o_ref[...] = acc_ref[...].astype(out_dtype)

    return _pcall(
        body, name=name, grid=(M // tmm, n_k), dep=dep, args=(a, b),
        in_specs=[pl.BlockSpec((tk, tmm), lambda m, k: (k, m)), pl.BlockSpec((tk, N), lambda m, k: (k, 0))],
        out_specs=pl.BlockSpec((tmm, N), lambda m, k: (m, 0)),
        out_shape=jax.ShapeDtypeStruct((M, N), out_dtype),
        scratch_shapes=[pltpu.VMEM((tmm, N), F32)] if narrow else [],
        compiler_params=_cp("parallel", "arbitrary"),
    )


def _mm_nt(x, wt, g, out_dtype, name, dep=None):
    T, K = x.shape
    N = wt.shape[0]
    tm, tn = _tile(T, 512), N
    norm = g is not None

    def body(*refs):
        if norm:
            x_ref, g_ref, w_ref, o_ref, h_ref = refs
        else:
            x_ref, w_ref, o_ref, h_ref = refs

        @pl.when(pl.program_id(1) == 0)
        def _():
            xv = x_ref[...].astype(F32)
            if norm:
                xv = xv * _rstd(xv) * g_ref[...]
            h_ref[...] = xv.astype(BF16)

        o_ref[...] = _dot(h_ref[...], w_ref[...], NT).astype(out_dtype)

    row = pl.BlockSpec((tm, K), lambda i, j: (i, 0))
    wsp = pl.BlockSpec((tn, K), lambda i, j: (j, 0))
    osp = pl.BlockSpec((tm, tn), lambda i, j: (i, j))
    if norm:
        return pl.pallas_call(
            body, name=name, grid=(T // tm, N // tn),
            in_specs=[row, pl.BlockSpec((1, K), lambda i, j: (0, 0)), wsp],
            out_specs=[osp, row],
            out_shape=[jax.ShapeDtypeStruct((T, N), out_dtype), jax.ShapeDtypeStruct((T, K), BF16)],
            compiler_params=_cp("parallel", "arbitrary"),
        )(x, g, wt)
    return _pcall(
        body, name=name, grid=(T // tm, N // tn), dep=dep, args=(x, wt),
        in_specs=[row, wsp], out_specs=osp,
        out_shape=jax.ShapeDtypeStruct((T, N), out_dtype),
        scratch_shapes=[pltpu.VMEM((tm, K), BF16)],
        compiler_params=_cp("parallel", "arbitrary"),
    )


def _mm_nn_res(act, w, resid, name):
    T, K = act.shape
    D = w.shape[1]
    tm = _tile(T, 512)

    def body(a_ref, w_ref, r_ref, o_ref):
        o_ref[...] = r_ref[...] + _dot(a_ref[...].astype(BF16), w_ref[...], NN)

    return pl.pallas_call(
        body, name=name, grid=(T // tm,),
        in_specs=[pl.BlockSpec((tm, K), lambda i: (i, 0)), pl.BlockSpec((K, D), lambda i: (0, 0)),
                  pl.BlockSpec((tm, D), lambda i: (i, 0))],
        out_specs=pl.BlockSpec((tm, D), lambda i: (i, 0)),
        out_shape=jax.ShapeDtypeStruct((T, D), F32),
        compiler_params=_cp("parallel"),
    )(act, w, resid)


def _mm_nn_rmsbwd(acts, w, x, g, gprev, name, dep=None):
    T = acts[0].shape[0]
    ks = [a.shape[1] for a in acts]
    K, D = w.shape
    assert sum(ks) == K
    tm = _tile(T, 512)
    na = len(acts)

    def body(*refs):
        a_refs = refs[:na]
        w_ref, x_ref, g_ref, gp_ref, o_ref, dg_ref = refs[na:]

        @pl.when(pl.program_id(0) == 0)
        def _():
            dg_ref[...] = jnp.zeros_like(dg_ref)

        dh, off = None, 0
        for a_ref, k in zip(a_refs, ks):
            part = _dot(a_ref[...].astype(BF16), w_ref[off:off + k, :], NN)
            dh = part if dh is None else dh + part
            off += k
        dx, dg = _rms_bwd(x_ref[...], g_ref[...], dh)
        o_ref[...] = gp_ref[...] + dx
        dg_ref[...] += dg

    row = pl.BlockSpec((tm, D), lambda i: (i, 0))
    vec = pl.BlockSpec((1, D), lambda i: (0, 0))
    return _pcall(
        body, name=name, grid=(T // tm,), dep=dep, args=(*acts, w, x, g, gprev),
        in_specs=[pl.BlockSpec((tm, k), lambda i: (i, 0)) for k in ks]
        + [pl.BlockSpec((K, D), lambda i: (0, 0)), row, vec, row],
        out_specs=[row, vec],
        out_shape=[jax.ShapeDtypeStruct((T, D), F32), jax.ShapeDtypeStruct((1, D), F32)],
        compiler_params=_cp("arbitrary"),
    )


def _loss_head(x, g, target):
    T, D = x.shape
    tm = _tile(T, 512)

    def body(x_ref, g_ref, t_ref, dx_ref, loss_ref, dg_ref):
        @pl.when(pl.program_id(0) == 0)
        def _():
            loss_ref[...] = jnp.zeros_like(loss_ref)
            dg_ref[...] = jnp.zeros_like(dg_ref)

        xv = x_ref[...]
        gv = g_ref[...]
        e = xv * _rstd(xv) * gv - t_ref[...]
        per_tok = jnp.sum(e * e, axis=-1, keepdims=True) * (1.0 / D)
        loss_ref[...] += 0.5 * jnp.sum(per_tok, axis=0, keepdims=True)
        dx, dg = _rms_bwd(xv, gv, e * (1.0 / D))
        dx_ref[...] = dx
        dg_ref[...] += dg

    row = pl.BlockSpec((tm, D), lambda i: (i, 0))
    vec = pl.BlockSpec((1, D), lambda i: (0, 0))
    return pl.pallas_call(
        body, name="loss_head", grid=(T // tm,),
        in_specs=[row, vec, row],
        out_specs=[row, pl.BlockSpec((1, LANES), lambda i: (0, 0)), vec],
        out_shape=[jax.ShapeDtypeStruct((T, D), F32), jax.ShapeDtypeStruct((1, LANES), F32),
                   jax.ShapeDtypeStruct((1, D), F32)],
        compiler_params=_cp("arbitrary"),
    )(x, g, target)


def _log_gates(z):
    neg_abs = lax.bitcast_convert_type(lax.bitcast_convert_type(z, jnp.uint32) | jnp.uint32(0x80000000), F32)
    ls = jnp.minimum(z, 0.0) - jnp.log(1.0 + jnp.exp(neg_abs))
    return ls, ls - z


def _cumsum_mm(v, u):
    return _dot(v.astype(BF16), u, NN)


def _half_rowsum(v):
    n = v.shape[0]
    s0 = jnp.sum(v[:, :QB], axis=1, keepdims=True)
    s1 = jnp.sum(v[:, QB:], axis=1, keepdims=True)
    return jnp.concatenate([jnp.broadcast_to(s0, (n, QB)), jnp.broadcast_to(s1, (n, QB))], axis=1)


def _stack_heads(src_ref, dst_ref, n_blk):
    m0 = lax.broadcasted_iota(jnp.int32, (1, LANES), 1) < HEAD_DIM

    def fill(c, carry):
        blk = src_ref[pl.ds(pl.multiple_of(c * QB, QB), QB), :]
        zero = jnp.zeros_like(blk)
        dst_ref[c, 0:QB, :] = jnp.where(m0, blk, zero)
        dst_ref[c, QB:2 * QB, :] = jnp.where(m0, zero, blk)
        return carry

    lax.fori_loop(0, n_blk, fill, 0)


def _diag_mask(tq, j):
    n = tq - j * QB
    row = lax.broadcasted_iota(jnp.int32, (n, 2 * QB), 0)
    col = lax.broadcasted_iota(jnp.int32, (n, 2 * QB), 1)
    return (col & (QB - 1)) < row


def _tri_blockdiag(upper):
    r = lax.broadcasted_iota(jnp.int32, (2 * QB, 2 * QB), 0)
    c = lax.broadcasted_iota(jnp.int32, (2 * QB, 2 * QB), 1)
    same = (r // QB) == (c // QB)
    return (same & ((r > c) if upper else (r < c))).astype(BF16)


def _attn_tiles(T, n_seq):
    S = T // n_seq
    tq = ATT_TQ if S % ATT_TQ == 0 else 2 * QB
    assert S % tq == 0
    return S, tq, tq // QB, S // tq, S // QB


def _attn_fwd(qkv, n_seq):
    T = qkv.shape[0]
    S, tq, r, n_q, n_k = _attn_tiles(T, n_seq)
    n_p = D_MODEL // LANES
    n_steps = n_seq * n_p * n_q
    u_suffix = _tri_blockdiag(True)

    def body(q_ref, k_ref, v_ref, u_ref, o_ref, a_hbm, kk_ref, vv_ref, lr_s, acc_s, a_stage, sems):
        qi = pl.program_id(2)
        group = (pl.program_id(0) * n_p + pl.program_id(1)) * n_q + qi

        @pl.when(qi == 0)
        def _():
            _stack_heads(k_ref, kk_ref, n_k)
            _stack_heads(v_ref, vv_ref, n_k)

        u = u_ref[...]
        lr_s[...] = jnp.zeros_like(lr_s)
        acc_s[...] = jnp.zeros_like(acc_s)
        base = ((group // n_q) * (n_q * (n_q + 1) // 2) + (qi * (qi + 1)) // 2) % 2

        def saves(first_kj, half):
            return [pltpu.make_async_copy(a_stage.at[half, j], a_hbm.at[group, first_kj - j], sems.at[half])
                    for j in range(r)]

        @pl.when(group >= 2)
        def _():
            for cp in saves(0, base):
                cp.wait()

        a_stage[base] = jnp.zeros_like(a_stage[0])

        def step(kj, half, j, rows, q, mask, lr, acc):
            ls, lk = _log_gates(_dot(q, kk_ref[kj], NT))
            if mask is not None:
                lk = jnp.where(mask, lk, 0.0)
            a = jnp.exp(ls + _cumsum_mm(lk, u) + lr)
            if mask is not None:
                a = jnp.where(mask, a, 0.0)
            a = a.astype(BF16)
            a_stage[half, j, rows, :] = a
            return lr + _half_rowsum(lk), acc + _dot(a, vv_ref[kj], NN)

        last = (qi + 1) * r - 1
        for n in range(r):
            rows = slice((r - 1 - n) * QB, tq)
            lr, acc = step(last - n, base, n, rows, q_ref[rows, :] * ATT_SCALE, _diag_mask(tq, r - 1 - n),
                           lr_s[rows, :], acc_s[rows, :])
            lr_s[rows, :] = lr
            acc_s[rows, :] = acc
        for cp in saves(last, base):
            cp.start()

        q = q_ref[...] * ATT_SCALE

        def off(it, carry):
            half = (base + it + 1) % 2
            first = (qi - it) * r - 1
            for cp in saves(first, half):
                cp.wait()
            lr, acc = lr_s[...], acc_s[...]
            for j in range(r):
                lr, acc = step(first - j, half, j, slice(0, tq), q, None, lr, acc)
            lr_s[...] = lr
            acc_s[...] = acc
            for cp in saves(first, half):
                cp.start()
            return carry

        lax.fori_loop(0, qi, off, 0)

        @pl.when(group == n_steps - 1)
        def _():
            for cp in saves(0, (base + qi) % 2):
                cp.wait()
            if n_steps * n_q > 1:
                for cp in saves(0, (base + qi + 1) % 2):
                    cp.wait()

        o_ref[...] = acc_s[...].astype(BF16)

    return pl.pallas_call(
        body, name="attn_fwd", grid=(n_seq, n_p, n_q),
        in_specs=[pl.BlockSpec((tq, LANES), lambda b, p, qi: (b * n_q + qi, p)),
                  pl.BlockSpec((S, LANES), lambda b, p, qi: (b, n_p + p)),
                  pl.BlockSpec((S, LANES), lambda b, p, qi: (b, 2 * n_p + p)),
                  pl.BlockSpec((2 * QB, 2 * QB), lambda b, p, qi: (0, 0))],
        out_specs=[pl.BlockSpec((tq, LANES), lambda b, p, qi: (b * n_q + qi, p)), ANY],
        out_shape=[jax.ShapeDtypeStruct((T, D_MODEL), BF16),
                   jax.ShapeDtypeStruct((n_seq * n_p * n_q, n_k, tq, 2 * QB), BF16)],
        scratch_shapes=[pltpu.VMEM((n_k, 2 * QB, LANES), BF16), pltpu.VMEM((n_k, 2 * QB, LANES), BF16),
                        pltpu.VMEM((tq, 2 * QB), F32), pltpu.VMEM((tq, LANES), F32),
                        pltpu.VMEM((2, r, tq, 2 * QB), BF16), pltpu.SemaphoreType.DMA((2,))],
        compiler_params=_cp("arbitrary", "arbitrary", "arbitrary"),
    )(qkv, qkv, qkv, u_suffix)


def _attn_bwd(qkv, do, a_saved, n_seq):
    T = qkv.shape[0]
    S, tq, r, n_q, n_k = _attn_tiles(T, n_seq)
    n_p = D_MODEL // LANES
    n_steps = n_seq * n_p * n_q
    u_prefix = _tri_blockdiag(False)

    def body(q_ref, k_ref, v_ref, do_ref, u_ref, a_hbm, dq_ref, dk_out, dv_out,
             kk_ref, vv_ref, cg_s, dq_s, dk_ref, dv_ref, a_stage, sems):
        qi = pl.program_id(2)
        group = (pl.program_id(0) * n_p + pl.program_id(1)) * n_q + qi

        base = ((group // n_q) * (n_q * (n_q + 1) // 2) + (qi * (qi + 1)) // 2) % 2

        def fetches(grp, g, half):
            return [pltpu.make_async_copy(a_hbm.at[grp, g * r + j], a_stage.at[half, j], sems.at[half])
                    for j in range(r)]

        @pl.when(group == 0)
        def _():
            for cp in fetches(group, 0, 0):
                cp.start()

        @pl.when(qi == 0)
        def _():
            _stack_heads(k_ref, kk_ref, n_k)
            _stack_heads(v_ref, vv_ref, n_k)
            dk_ref[...] = jnp.zeros_like(dk_ref)
            dv_ref[...] = jnp.zeros_like(dv_ref)

        u = u_ref[...]
        cg_s[...] = jnp.zeros_like(cg_s)
        dq_s[...] = jnp.zeros_like(dq_s)

        def step(kj, half, j, rows, q, dov, mask, cg, dq):
            kk = kk_ref[kj]
            beta = _sigmoid(_dot(q, kk, NT))
            a = a_stage[half, j, rows, :]
            g = a.astype(F32) * _dot(dov, vv_ref[kj], NT)
            dz = g - (g + _dot(g.astype(BF16), u, NN) + cg) * beta
            if mask is not None:
                dz = jnp.where(mask, dz, 0.0)
            dz = dz.astype(BF16)
            keys = pl.ds(pl.multiple_of(kj * QB, QB), QB)
            dvt = _dot(dov_t[:, rows], a, NN)
            dv_ref[keys, :] += jnp.where(t0, dvt[:, :QB], dvt[:, QB:]).T
            dkt = _dot(q_t[:, rows], dz, NN)
            dk_ref[keys, :] += jnp.where(t0, dkt[:, :QB], dkt[:, QB:]).T
            return cg + _half_rowsum(g), dq + _dot(dz, kk, NN)

        q = q_ref[...] * ATT_SCALE
        dov = do_ref[...]
        q_t = q.astype(F32).T.astype(BF16)
        dov_t = dov.astype(F32).T.astype(BF16)
        t0 = lax.broadcasted_iota(jnp.int32, (LANES, 1), 0) < HEAD_DIM

        def off(it, carry):
            half = (base + it) % 2
            for cp in fetches(group, it, half):
                cp.wait()
            for cp in fetches(group, it + 1, 1 - half):
                cp.start()
            cg, dq = cg_s[...], dq_s[...]
            for j in range(r):
                cg, dq = step(it * r + j, half, j, slice(0, tq), q, dov, None, cg, dq)
            cg_s[...] = cg
            dq_s[...] = dq
            return carry

        lax.fori_loop(0, qi, off, 0)

        half = (base + qi) % 2
        for cp in fetches(group, qi, half):
            cp.wait()

        @pl.when(group < n_steps - 1)
        def _():
            for cp in fetches(group + 1, 0, 1 - half):
                cp.start()

        for j in range(r):
            rows = slice(j * QB, tq)
            cg, dq = step(qi * r + j, half, j, rows, q_ref[rows, :] * ATT_SCALE, do_ref[rows, :],
                          _diag_mask(tq, j), cg_s[rows, :], dq_s[rows, :])
            cg_s[rows, :] = cg
            dq_s[rows, :] = dq
        dq_ref[...] = (dq_s[...] * ATT_SCALE).astype(BF16)

        @pl.when(qi == n_q - 1)
        def _():
            dk_out[...] = dk_ref[...].astype(BF16)
            dv_out[...] = dv_ref[...].astype(BF16)

    qspec = pl.BlockSpec((tq, LANES), lambda b, p, qi: (b * n_q + qi, p))
    seq = lambda off: pl.BlockSpec((S, LANES), lambda b, p, qi: (b, off + p))
    return pl.pallas_call(
        body, name="attn_bwd", grid=(n_seq, n_p, n_q),
        in_specs=[qspec, seq(n_p), seq(2 * n_p), qspec,
                  pl.BlockSpec((2 * QB, 2 * QB), lambda b, p, qi: (0, 0)), ANY],
        out_specs=[qspec, seq(0), seq(0)],
        out_shape=[jax.ShapeDtypeStruct((T, D_MODEL), BF16)] * 3,
        scratch_shapes=[pltpu.VMEM((n_k, 2 * QB, LANES), BF16), pltpu.VMEM((n_k, 2 * QB, LANES), BF16),
                        pltpu.VMEM((tq, 2 * QB), F32), pltpu.VMEM((tq, LANES), F32),
                        pltpu.VMEM((S, LANES), F32), pltpu.VMEM((S, LANES), F32),
                        pltpu.VMEM((2, r, tq, 2 * QB), BF16), pltpu.SemaphoreType.DMA((2,))],
        compiler_params=_cp("arbitrary", "arbitrary", "arbitrary"),
    )(qkv, qkv, qkv, do, u_prefix, a_saved)


def _shifted_copies(sh_ref):
    rows = sh_ref.shape[1] - SUBLANES
    for s in range(1, SUBLANES):
        sh_ref[s, 0:rows, :] = sh_ref[0, s:s + rows, :]


def _shifted(sh_ref, start, n):
    s = start % SUBLANES
    return sh_ref[s, start - s:start - s + n, :]


def _glu_with_halo(av_ref, ag_ref, avh_ref, agh_ref, a0_s, first, ts):
    hal = avh_ref[...] * _sigmoid(agh_ref[...])
    a0_s[0, 0:HALO, :] = jnp.where(first, 0.0, hal)
    a0_s[0, HALO:HALO + ts, :] = av_ref[...] * _sigmoid(ag_ref[...])
    _shifted_copies(a0_s)


def _mix_specs(ts, n_r, with_left):
    blk = lambda c: pl.BlockSpec((ts, CA), lambda b, r: (b * n_r + r, c))
    per = ts // HALO
    left = lambda c: pl.BlockSpec((HALO, CA), lambda b, r: (jnp.maximum((b * n_r + r) * per - 1, 0), c))
    return blk, (left if with_left else None)


def _mix_fwd(z, conv_w, conv_b, ln_a_g, ln_a_b, ln_v_g, ln_v_b, ws, bias2d, n_seq):
    T = z.shape[0]
    S = T // n_seq
    ts = _tile(S, 512)
    n_r = S // ts
    shift = HALO - (CONV_WIDTH - 1)

    def body(av_ref, ag_ref, avh_ref, agh_ref, u_ref, v_ref, cw_ref, cb_ref, lag_ref, lab_ref,
             lvg_ref, lvb_ref, ws_ref, bias_ref, cat_ref, a1_ref, a0_s):
        _glu_with_halo(av_ref, ag_ref, avh_ref, agh_ref, a0_s, pl.program_id(1) == 0, ts)
        for rb in range(ts // CONV_ROWS):
            base = rb * CONV_ROWS
            acc = jnp.broadcast_to(cb_ref[...], (CONV_ROWS, CA))
            for k in range(CONV_WIDTH):
                acc = acc + cw_ref[k:k + 1, :] * _shifted(a0_s, base + shift + k, CONV_ROWS)
            a1_ref[base:base + CONV_ROWS, :] = acc
        y, _, _ = _ln_fwd(a1_ref[...], lag_ref[...], lab_ref[...])
        cat_ref[:, 0:CA] = (y * _sigmoid(y)).astype(BF16)
        for gi in range(GB):
            sl = slice(gi * DB, (gi + 1) * DB)
            v1, _, _ = _ln_fwd(v_ref[:, sl], lvg_ref[:, sl], lvb_ref[:, sl])
            v1 = v1.astype(BF16)
            for c in range(ts // CHUNK):
                rs = slice(c * CHUNK, (c + 1) * CHUNK)
                v2 = _dot(ws_ref[gi], v1[rs], NN) + bias_ref[:, sl]
                cat_ref[rs, CA + gi * DB:CA + (gi + 1) * DB] = (u_ref[rs, sl] * v2).astype(BF16)

    blk, left = _mix_specs(ts, n_r, True)
    vec = pl.BlockSpec((1, CA), lambda b, r: (0, 0))
    return pl.pallas_call(
        body, name="mix_fwd", grid=(n_seq, n_r),
        in_specs=[blk(0), blk(1), left(0), left(1), blk(2), blk(3),
                  pl.BlockSpec((CONV_WIDTH, CA), lambda b, r: (0, 0)), vec, vec, vec, vec, vec,
                  pl.BlockSpec((GB, CHUNK, CHUNK), lambda b, r: (0, 0, 0)),
                  pl.BlockSpec((CHUNK, CB), lambda b, r: (0, 0))],
        out_specs=[pl.BlockSpec((ts, CA + CB), lambda b, r: (b * n_r + r, 0)), blk(0)],
        out_shape=[jax.ShapeDtypeStruct((T, CA + CB), BF16), jax.ShapeDtypeStruct((T, CA), F32)],
        scratch_shapes=[pltpu.VMEM((SUBLANES, HALO + ts, CA), F32)],
        compiler_params=_cp("parallel", "parallel"),
    )(z, z, z, z, z, z, conv_w, conv_b, ln_a_g, ln_a_b, ln_v_g, ln_v_b, ws, bias2d)


def _mix_bwd_rows(dcat, z, a1, ln_a_g, ln_a_b, ln_v_g, ln_v_b, ws, ws_t, bias2d, n_seq):
    T = z.shape[0]
    S = T // n_seq
    ts = _tile(S, 512)
    n_r = S // ts

    def body(dc_ref, u_ref, v_ref, a1_ref, lag_ref, lab_ref, lvg_ref, lvb_ref, ws_ref, wst_ref, bias_ref,
             da1_ref, dz_ref, dlag_ref, dlab_ref, dlvg_ref, dlvb_ref, dws_ref, dsb_ref, dv1_s, dbias_s):
        first = (pl.program_id(0) == 0) & (pl.program_id(1) == 0)
        last = (pl.program_id(0) == n_seq - 1) & (pl.program_id(1) == n_r - 1)

        @pl.when(first)
        def _():
            for ref in (dlag_ref, dlab_ref, dlvg_ref, dlvb_ref, dws_ref, dbias_s):
                ref[...] = jnp.zeros_like(ref)

        lag = lag_ref[...]
        y, xh, r = _ln_fwd(a1_ref[...], lag, lab_ref[...])
        sig = _sigmoid(y)
        dy = dc_ref[:, 0:CA] * (sig * (1.0 + y * (1.0 - sig)))
        dlag_ref[...] += jnp.sum(dy * xh, axis=0, keepdims=True)
        dlab_ref[...] += jnp.sum(dy, axis=0, keepdims=True)
        da1_ref[...] = _ln_bwd(dy, xh, r, lag)

        tril = (lax.broadcasted_iota(jnp.int32, (CHUNK, CHUNK), 0)
                >= lax.broadcasted_iota(jnp.int32, (CHUNK, CHUNK), 1))
        for gi in range(GB):
            sl = slice(gi * DB, (gi + 1) * DB)
            lvg = lvg_ref[:, sl]
            v1, vh, vr = _ln_fwd(v_ref[:, sl], lvg, lvb_ref[:, sl])
            v1 = v1.astype(BF16)
            for c in range(ts // CHUNK):
                rs = slice(c * CHUNK, (c + 1) * CHUNK)
                v2 = _dot(ws_ref[gi], v1[rs], NN) + bias_ref[:, sl]
                dbo = dc_ref[rs, CA + gi * DB:CA + (gi + 1) * DB]
                dz_ref[rs, sl] = (dbo * v2).astype(BF16)
                dv2 = dbo * u_ref[rs, sl]
                dbias_s[:, sl] += dv2
                dv2b = dv2.astype(BF16)
                dws_ref[gi] += jnp.where(tril, _dot(dv2b, v1[rs], NT), 0.0)
                dv1_s[rs, :] = _dot(wst_ref[gi], dv2b, NN)
            dv1 = dv1_s[...]
            dlvg_ref[:, sl] += jnp.sum(dv1 * vh, axis=0, keepdims=True)
            dlvb_ref[:, sl] += jnp.sum(dv1, axis=0, keepdims=True)
            dz_ref[:, CB + gi * DB:CB + (gi + 1) * DB] = _ln_bwd(dv1, vh, vr, lvg).astype(BF16)

        @pl.when(last)
        def _():
            col = lax.broadcasted_iota(jnp.int32, (CHUNK, GB), 1)
            out = jnp.zeros((CHUNK, GB), F32)
            for gi in range(GB):
                s = jnp.sum(dbias_s[:, gi * DB:(gi + 1) * DB], axis=1, keepdims=True)
                out = out + jnp.where(col == gi, s, 0.0)
            dsb_ref[...] = out

    blk, _ = _mix_specs(ts, n_r, False)
    vec = pl.BlockSpec((1, CA), lambda b, r: (0, 0))
    mat = pl.BlockSpec((GB, CHUNK, CHUNK), lambda b, r: (0, 0, 0))
    wide = pl.BlockSpec((ts, CA + CB), lambda b, r: (b * n_r + r, 0))
    return pl.pallas_call(
        body, name="mix_bwd_rows", grid=(n_seq, n_r),
        in_specs=[wide, blk(2), blk(3), blk(0), vec, vec, vec, vec, mat, mat,
                  pl.BlockSpec((CHUNK, CB), lambda b, r: (0, 0))],
        out_specs=[blk(0), wide, vec, vec, vec, vec, mat, pl.BlockSpec((CHUNK, GB), lambda b, r: (0, 0))],
        out_shape=[jax.ShapeDtypeStruct((T, CA), F32), jax.ShapeDtypeStruct((T, 2 * CB), BF16)]
        + [jax.ShapeDtypeStruct((1, CA), F32)] * 4
        + [jax.ShapeDtypeStruct((GB, CHUNK, CHUNK), F32), jax.ShapeDtypeStruct((CHUNK, GB), F32)],
        scratch_shapes=[pltpu.VMEM((ts, DB), F32), pltpu.VMEM((CHUNK, CB), F32)],
        compiler_params=_cp("arbitrary", "arbitrary"),
    )(dcat, z, z, a1, ln_a_g, ln_a_b, ln_v_g, ln_v_b, ws, ws_t, bias2d)


def _mix_bwd_conv(da1, z, conv_w, n_seq):
    T = z.shape[0]
    S = T // n_seq
    ts = _tile(S, 512)
    n_r = S // ts
    per = ts // HALO
    shift = HALO - (CONV_WIDTH - 1)
    fold = CONV_ROWS // 8

    def body(d_ref, dh_ref, av_ref, ag_ref, avh_ref, agh_ref, cw_ref,
             dz_ref, dcw_ref, dcb_ref, a0_s, d1_s, da0_s, dw8_s):
        first = (pl.program_id(0) == 0) & (pl.program_id(1) == 0)
        last = (pl.program_id(0) == n_seq - 1) & (pl.program_id(1) == n_r - 1)

        @pl.when(first)
        def _():
            dw8_s[...] = jnp.zeros_like(dw8_s)
            dcb_ref[...] = jnp.zeros_like(dcb_ref)

        _glu_with_halo(av_ref, ag_ref, avh_ref, agh_ref, a0_s, pl.program_id(1) == 0, ts)
        d1_s[0, 0:ts, :] = d_ref[...]
        d1_s[0, ts:ts + HALO, :] = jnp.where(pl.program_id(1) == n_r - 1, 0.0, dh_ref[...])
        _shifted_copies(d1_s)
        dcb_ref[...] += jnp.sum(d_ref[...], axis=0, keepdims=True)
        for rb in range(ts // CONV_ROWS):
            base = rb * CONV_ROWS
            dcur = d1_s[0, base:base + CONV_ROWS, :]
            acc = jnp.zeros((CONV_ROWS, CA), F32)
            for k in range(CONV_WIDTH):
                back = CONV_WIDTH - 1 - k
                acc = acc + cw_ref[k:k + 1, :] * _shifted(d1_s, base + back, CONV_ROWS)
                prod = dcur * _shifted(a0_s, base + shift + k, CONV_ROWS)
                part = prod[0:8]
                for f in range(1, fold):
                    part = part + prod[8 * f:8 * f + 8]
                dw8_s[k] += part
            da0_s[base:base + CONV_ROWS, :] = acc
        da0 = da0_s[...]
        sig = _sigmoid(ag_ref[...])
        dz_ref[:, 0:CA] = (da0 * sig).astype(BF16)
        dz_ref[:, CA:2 * CA] = (da0 * av_ref[...] * sig * (1.0 - sig)).astype(BF16)

        @pl.when(last)
        def _():
            for k in range(CONV_WIDTH):
                dcw_ref[k:k + 1, :] = jnp.sum(dw8_s[k], axis=0, keepdims=True)

    blk, left = _mix_specs(ts, n_r, True)
    n_halo_blocks = T // HALO
    right = pl.BlockSpec((HALO, CA), lambda b, r: (jnp.minimum((b * n_r + r + 1) * per, n_halo_blocks - 1), 0))
    return pl.pallas_call(
        body, name="mix_bwd_conv", grid=(n_seq, n_r),
        in_specs=[blk(0), right, blk(0), blk(1), left(0), left(1),
                  pl.BlockSpec((CONV_WIDTH, CA), lambda b, r: (0, 0))],
        out_specs=[pl.BlockSpec((ts, 2 * CA), lambda b, r: (b * n_r + r, 0)),
                   pl.BlockSpec((CONV_WIDTH, CA), lambda b, r: (0, 0)), pl.BlockSpec((1, CA), lambda b, r: (0, 0))],
        out_shape=[jax.ShapeDtypeStruct((T, 2 * CA), BF16), jax.ShapeDtypeStruct((CONV_WIDTH, CA), F32),
                   jax.ShapeDtypeStruct((1, CA), F32)],
        scratch_shapes=[pltpu.VMEM((SUBLANES, HALO + ts, CA), F32), pltpu.VMEM((SUBLANES, ts + HALO, CA), F32),
                        pltpu.VMEM((ts, CA), F32), pltpu.VMEM((CONV_WIDTH, 8, CA), F32)],
        compiler_params=_cp("arbitrary", "arbitrary"),
    )(da1, da1, z, z, z, z, conv_w)


def _row_tile(R, want):
    t = min(R, want)
    t -= t % 8
    while t > 8 and R % t:
        t -= 8
    return t if t >= 8 and R % t == 0 else R


def _adam_step(w, g, m, v):
    nm = ADAM_B1 * m + (1.0 - ADAM_B1) * g
    nv = ADAM_B2 * v + (1.0 - ADAM_B2) * (g * g)
    m_hat = nm / (1.0 - ADAM_B1 ** ADAM_STEP)
    v_hat = nv / (1.0 - ADAM_B2 ** ADAM_STEP)
    return -ADAM_LR * (m_hat / (jnp.sqrt(v_hat) + ADAM_EPS) + ADAM_WD * w), nm, nv


def _adamw_parts(parts, w, m, v, name):
    L, n, C = w.shape
    assert len(parts) == L
    tr = _row_tile(n, 192)
    n_i = n // tr

    def body(*refs):
        p_refs = refs[:L]
        w_ref, m_ref, v_ref, g_ref, d_ref, nm_ref, nv_ref = refs[L:]
        for k in range(L):
            @pl.when(pl.program_id(0) == k)
            def _(k=k):
                acc = p_refs[k][0].astype(F32)
                for s in range(1, N_DEV):
                    acc = acc + p_refs[k][s].astype(F32)
                g_ref[...] = acc

        d_ref[...], nm_ref[...], nv_ref[...] = _adam_step(w_ref[...], g_ref[...], m_ref[...], v_ref[...])

    def part_spec(k):
        return pl.BlockSpec((N_DEV, tr, C),
                            lambda l, i: (0, jnp.where(l == k, i, jnp.where(l < k, 0, n_i - 1)), 0))

    blk = pl.BlockSpec((None, tr, C), lambda l, i: (l, i, 0))
    return pl.pallas_call(
        body, name=name, grid=(L, n_i),
        in_specs=[part_spec(k) for k in range(L)] + [blk] * 3, out_specs=[blk] * 4,
        out_shape=[jax.ShapeDtypeStruct((L, n, C), F32)] * 4,
        compiler_params=_cp("arbitrary", "arbitrary"),
    )(*parts, w, m, v)


def _adamw(w, g, m, v, name):
    R, C = w.shape
    tr = _row_tile(R, 256)

    def body(w_ref, g_ref, m_ref, v_ref, d_ref, nm_ref, nv_ref):
        d_ref[...], nm_ref[...], nv_ref[...] = _adam_step(w_ref[...], g_ref[...], m_ref[...], v_ref[...])

    blk = pl.BlockSpec((tr, C), lambda i: (i, 0))
    return pl.pallas_call(
        body, name=name, grid=(R // tr,),
        in_specs=[blk] * 4, out_specs=[blk] * 3,
        out_shape=[jax.ShapeDtypeStruct((R, C), F32)] * 3,
        compiler_params=_cp("parallel"),
    )(w, g, m, v)


def _me():
    return lax.axis_index("x"), lax.axis_index("y"), lax.axis_index("c")


def _block_rows(ref, dev, n):
    start = (4 * dev[0] + 2 * dev[1] + dev[2]) * n
    if len(ref.shape) == 2:
        return ref.at[pl.ds(start, n), :]
    return ref.at[:, pl.ds(start, n), :]


def _all_gather(shards):
    na = len(shards)
    ns = [s.shape[-2] for s in shards]

    def body(*refs):
        ins, outs = refs[:na], refs[na:2 * na]
        send_sems, recv_sems, local_sems = refs[2 * na:]
        x, y, c = _me()
        me, sibling = (x, y, c), (x, y, 1 - c)
        chips = [(1 - x, y), (x, 1 - y), (1 - x, 1 - y)]

        def copy(a, k, block, to, src=None):
            dst = _block_rows(outs[a], block, ns[a])
            return pltpu.make_async_remote_copy(
                src_ref=dst if src is None else src, dst_ref=dst,
                send_sem=send_sems.at[a, k], recv_sem=recv_sems.at[a, k], device_id=to, device_id_type=MESH)

        mine = [pltpu.make_async_copy(ins[a], _block_rows(outs[a], me, ns[a]), local_sems.at[a]) for a in range(na)]
        for cp in mine:
            cp.start()
        first = []
        for a in range(na):
            first.append(copy(a, 0, me, sibling, src=ins[a]))
            first += [copy(a, 1 + j, me, (*chip, c), src=ins[a]) for j, chip in enumerate(chips)]
        for cp in first:
            cp.start()
        passed = []
        for j, chip in enumerate(chips):
            for a in range(na):
                copy(a, 1 + j, (*chip, c), me).wait_recv()
                fwd = copy(a, 4 + j, (*chip, c), sibling)
                fwd.start()
                passed.append(fwd)
        for a in range(na):
            copy(a, 0, sibling, me).wait_recv()
            for j, chip in enumerate(chips):
                copy(a, 4 + j, (*chip, 1 - c), me).wait_recv()
        for cp in first + passed:
            cp.wait_send()
        for cp in mine:
            cp.wait()

    out_shape = [jax.ShapeDtypeStruct(s.shape[:-2] + (N_DEV * s.shape[-2], s.shape[-1]), s.dtype) for s in shards]
    return pl.pallas_call(
        body, name="weights_all_gather",
        in_specs=[ANY] * na, out_specs=[ANY] * na, out_shape=out_shape,
        scratch_shapes=[pltpu.SemaphoreType.DMA((na, 7)), pltpu.SemaphoreType.DMA((na, 7)),
                        pltpu.SemaphoreType.DMA((na,))],
    )(*shards)


def _split_copies(gather, srcs, lands, send_sems, recv_sems, ns):
    x, y, c = _me()
    me = (x, y, c)
    my_slot = 4 * x + 2 * y + c
    copies = []
    for mask in range(1, N_DEV):
        peer = (x ^ (mask >> 2), y ^ ((mask >> 1) & 1), c ^ (mask & 1))
        for a in range(len(srcs)):
            if gather:
                src, dst = srcs[a], _block_rows(lands[a], me, ns[a])
            else:
                src, dst = _block_rows(srcs[a], peer, ns[a]), lands[a].at[my_slot]
            sem = a * (N_DEV - 1) + mask - 1
            copies.append(pltpu.make_async_remote_copy(
                src_ref=src, dst_ref=dst, send_sem=send_sems.at[sem], recv_sem=recv_sems.at[sem],
                device_id=peer, device_id_type=MESH))
    return copies


HBM_SPEC = pl.BlockSpec(memory_space=pltpu.HBM)
SEM_SPEC = pl.BlockSpec(memory_space=pltpu.SEMAPHORE)


def _split_start(gather, srcs, name, dep=None):
    na = len(srcs)
    x, y, c = _me()
    mine = 4 * x + 2 * y + c
    if gather:
        ns = [s.shape[-2] for s in srcs]
        lands = [lax.dynamic_update_slice(
            jnp.zeros(s.shape[:-2] + (N_DEV * s.shape[-2], s.shape[-1]), s.dtype), s,
            (0,) * (s.ndim - 2) + (mine * s.shape[-2], 0)) for s in srcs]
    else:
        ns = [s.shape[-2] // N_DEV for s in srcs]
        lands = [lax.dynamic_update_slice(
            lax.empty((N_DEV, n, s.shape[-1]), s.dtype),
            lax.dynamic_slice(s, (mine * n, 0), (n, s.shape[-1]))[None], (mine, 0, 0)) for s, n in zip(srcs, ns)]
    n_in = 2 * na + (dep is not None)

    def body(*refs):
        send_sems, recv_sems = refs[n_in], refs[n_in + 1]
        for cp in _split_copies(gather, refs[:na], refs[na:2 * na], send_sems, recv_sems, ns):
            cp.start()
        refs[-1][...] = jnp.zeros_like(refs[-1])

    hbm = lambda a: pltpu.with_memory_space_constraint(a, pltpu.HBM)
    args = [hbm(a) for a in srcs] + [hbm(a) for a in lands] + ([dep] if dep is not None else [])
    out = pl.pallas_call(
        body, name=name,
        in_specs=[HBM_SPEC] * (2 * na) + ([ANY] if dep is not None else []),
        out_specs=[SEM_SPEC, SEM_SPEC] + [HBM_SPEC] * (2 * na) + [pl.BlockSpec(memory_space=pltpu.VMEM)],
        out_shape=[pltpu.SemaphoreType.DMA((na * (N_DEV - 1),)), pltpu.SemaphoreType.DMA((na * (N_DEV - 1),))]
        + [pltpu.HBM(a.shape, a.dtype) for a in srcs + lands] + [jax.ShapeDtypeStruct((8, LANES), F32)],
        input_output_aliases={i: 2 + i for i in range(2 * na)},
        compiler_params=pltpu.CompilerParams(has_side_effects=pltpu.SideEffectType.DATAFLOW_SIDE_EFFECTING),
    )(*args)
    return (gather, ns, out[0], out[1], list(out[2:2 + na]), list(out[2 + na:2 + 2 * na])), out[-1]


def _split_wait(handle, after, name):
    gather, ns, send, recv, srcs, lands = handle
    na = len(srcs)

    def body(*refs):
        send_sems, recv_sems = refs[2 * na], refs[2 * na + 1]
        for cp in _split_copies(gather, refs[:na], refs[na:2 * na], send_sems, recv_sems, ns):
            cp.wait_send()
            cp.wait_recv()

    out = pl.pallas_call(
        body, name=name,
        in_specs=[HBM_SPEC] * (2 * na) + [SEM_SPEC, SEM_SPEC, ANY],
        out_specs=[HBM_SPEC] * (2 * na),
        out_shape=[pltpu.HBM(a.shape, a.dtype) for a in srcs + lands],
        input_output_aliases={i: i for i in range(2 * na)},
        compiler_params=pltpu.CompilerParams(has_side_effects=pltpu.SideEffectType.DATAFLOW_SIDE_EFFECTING),
    )(*srcs, *lands, send, recv, after)
    return list(out[:na]), list(out[na:])


def _sum_blocks(parts):
    n, R, C = parts.shape

    def body(p_ref, o_ref):
        acc = p_ref[0]
        for k in range(1, n):
            acc = acc + p_ref[k]
        o_ref[...] = acc

    return pl.pallas_call(
        body, name="small_sum",
        in_specs=[pl.BlockSpec(memory_space=pltpu.VMEM)], out_specs=pl.BlockSpec(memory_space=pltpu.VMEM),
        out_shape=jax.ShapeDtypeStruct((R, C), F32),
        compiler_params=pltpu.CompilerParams(vmem_limit_bytes=VMEM_LIMIT),
    )(parts)


def _pack(arrays):
    flat = jnp.concatenate([a.reshape(-1) for a in arrays])
    pad = (-flat.shape[0]) % (8 * LANES)
    return jnp.pad(flat, (0, pad)).reshape(-1, LANES)


def _unpack(buf, shapes):
    flat = buf.reshape(-1)
    out, off = [], 0
    for s in shapes:
        n = 1
        for d in s:
            n *= d
        out.append(flat[off:off + n].reshape(s))
        off += n
    return out


def _ffn_index(layer, second):
    return (2 * layer + second) * 3


def kernel(x, g_ffn1, w_ffn1_gate, w_ffn1_up, w_ffn1_down, g_mix, w_in_ab, conv_w, conv_b, ln_a_g, ln_a_b, ln_v_g, ln_v_b, sp_w, sp_b, w_out_ab, w_qkv, w_o, g_ffn2, w_ffn2_gate, w_ffn2_up, w_ffn2_down, g_final, loss_target, m_g_ffn1, m_w_ffn1_gate, m_w_ffn1_up, m_w_ffn1_down, m_g_mix, m_w_in_ab, m_conv_w, m_conv_b, m_ln_a_g, m_ln_a_b, m_ln_v_g, m_ln_v_b, m_sp_w, m_sp_b, m_w_out_ab, m_w_qkv, m_w_o, m_g_ffn2, m_w_ffn2_gate, m_w_ffn2_up, m_w_ffn2_down, m_g_final, v_g_ffn1, v_w_ffn1_gate, v_w_ffn1_up, v_w_ffn1_down, v_g_mix, v_w_in_ab, v_conv_w, v_conv_b, v_ln_a_g, v_ln_a_b, v_ln_v_g, v_ln_v_b, v_sp_w, v_sp_b, v_w_out_ab, v_w_qkv, v_w_o, v_g_ffn2, v_w_ffn2_gate, v_w_ffn2_up, v_w_ffn2_down, v_g_final):
    n_seq, S, D = x.shape
    T = n_seq * S
    depth = g_ffn1.shape[0]
    assert depth == 2 and D == D_MODEL
    my_block = 4 * lax.axis_index("x") + 2 * lax.axis_index("y") + lax.axis_index("c")

    ffn_parts = []
    for l in range(depth):
        for gate, up, down in ((w_ffn1_gate, w_ffn1_up, w_ffn1_down), (w_ffn2_gate, w_ffn2_up, w_ffn2_down)):
            ffn_parts += [gate[l].T, up[l].T, down[l]]
    ffn_shard = lambda k: jnp.stack(ffn_parts[3 * k:3 * k + 3]).astype(BF16)
    conv_w_pad = jnp.zeros((HALO, conv_w.shape[2]), F32).at[:CONV_WIDTH].set(conv_w[0]).T
    w_ffn = [None] * (2 * depth)
    w_ffn[0], conv_w_t = _all_gather([ffn_shard(0), conv_w_pad])
    conv_w_full = conv_w_t.T[:CONV_WIDTH]
    shards_b = [w_out_ab[0].astype(BF16), ffn_shard(1)]
    shards_d = [w_qkv[0].T.astype(BF16), w_o[0].astype(BF16), ffn_shard(3)]
    gather_a, token = _split_start(True, [w_in_ab[0].T.astype(BF16)], "gather_a_start", dep=conv_w_t)
    gather_b, token = _split_start(True, shards_b, "gather_b_start", dep=token)
    gather_c, token = _split_start(True, [ffn_shard(2)], "gather_c_start", dep=token)
    gather_d, token = _split_start(True, shards_d, "gather_d_start", dep=token)

    def gathered(handle, after, name):
        return _split_wait(handle, after, name)[1]

    row = lambda a: a.reshape(1, -1)
    tril = jnp.tril(jnp.ones((CHUNK, CHUNK), dtype=bool))
    ws = jnp.where(tril[None], sp_w[0], 0.0).astype(BF16)
    ws_t = jnp.swapaxes(ws, 1, 2)
    bias2d = jnp.repeat(sp_b[0].T, DB, axis=1)
    conv_b2, lag, lab = row(conv_b[0]), row(ln_a_g[0]), row(ln_a_b[0])
    lvg, lvb = row(ln_v_g[0]), row(ln_v_b[0])

    x0 = x.reshape(T, D)
    target = loss_target.reshape(T, D)
    saved = []
    xc = x0
    for l in range(depth):
        xa, a1, b1, h1 = _ffn_fwd(xc, row(g_ffn1[l]), w_ffn[2 * l], 0, f"ffn1_fwd_{l}", dep=token)
        if l % 2 == 0:
            w_in_t, = gathered(gather_a, xa, "gather_a_wait")
            z, hm = _mm_nt(xa, w_in_t, row(g_mix[l]), F32, "mix_in_proj")
            cat, conv_out = _mix_fwd(z, conv_w_full, conv_b2, lag, lab, lvg, lvb, ws, bias2d, n_seq)
            w_out, w_ffn[1] = gathered(gather_b, cat, "gather_b_wait")
            xb = _mm_nn_res(cat, w_out, xa, "mix_out_proj")
            mixer = (z, hm, cat, conv_out)
        else:
            w_qkv_t, w_o_full, w_ffn[3] = gathered(gather_d, xa, "gather_d_wait")
            qkv, hm = _mm_nt(xa, w_qkv_t, row(g_mix[l]), BF16, "qkv_proj")
            o, att = _attn_fwd(qkv, n_seq)
            xb = _mm_nn_res(o, w_o_full, xa, "attn_out_proj")
            mixer = (qkv, hm, o, att)
        xn, a2, b2, h2 = _ffn_fwd(xb, row(g_ffn2[l]), w_ffn[2 * l + 1], 0, f"ffn2_fwd_{l}")
        saved.append((xc, a1, b1, h1, xa, mixer, xb, a2, b2, h2))
        xc = xn
        if l == 0:
            w_ffn[2], = gathered(gather_c, xc, "gather_c_wait")

    g, loss_part, dg_final = _loss_head(xc, row(g_final), target)

    dg_ffn1, dg_ffn2, dg_mix = [None] * depth, [None] * depth, [None] * depth
    exchanges = {}
    token = None

    def ffn_back(g, xin, gvec, a, b, h, k, tag, token):
        g, dg, da, db, s, gh = _ffn_bwd(g, xin, gvec, a, b, w_ffn[k], 0, f"ffn{tag}_bwd", dep=token)
        if k == 0:
            return g, dg, (da, db, s, gh, h)
        dws = [_mm_tn(da, h, f"dw_gate{tag}"), _mm_tn(db, h, f"dw_up{tag}"), _mm_tn(s, gh, f"dw_down{tag}")]
        exchanges[f"ffn{k}"], token = _split_start(False, dws, f"exchange_ffn{tag}_start")
        return g, dg, token

    for l in reversed(range(depth)):
        xin, a1, b1, h1, xa, mixer, xb, a2, b2, h2 = saved[l]
        g, dg_ffn2[l], token = ffn_back(g, xb, row(g_ffn2[l]), a2, b2, h2, 2 * l + 1, f"2_{l}", token)
        if l % 2 == 0:
            z, hm, cat, conv_out = mixer
            dcat = _mm_nt(g, w_out, None, F32, "mix_out_bwd", dep=token)
            d_w_out = _mm_tn(cat, g, "dw_out")
            (da1, dz_uv, d_lag, d_lab, d_lvg, d_lvb, d_ws, d_sb) = _mix_bwd_rows(
                dcat, z, conv_out, lag, lab, lvg, lvb, ws, ws_t, bias2d, n_seq)
            dz_a, d_cw, d_cb = _mix_bwd_conv(da1, z, conv_w_full, n_seq)
            d_w_in_t = jnp.concatenate([_mm_tn(dz_a, hm, "dw_in_a"), _mm_tn(dz_uv, hm, "dw_in_uv")])
            exchanges["mix"], token = _split_start(False, [d_w_out, d_w_in_t], "exchange_mix_start")
            g, dg_mix[l] = _mm_nn_rmsbwd([dz_a, dz_uv], w_in_t, xa, row(g_mix[l]), g, "mix_in_bwd", dep=token)
        else:
            qkv, hm, o, att = mixer
            do = _mm_nt(g, w_o_full, None, BF16, "attn_out_bwd", dep=token)
            d_w_o = _mm_tn(o, g, "dw_o")
            dq, dk, dv = _attn_bwd(qkv, do, att, n_seq)
            d_w_qkv_t = jnp.concatenate([_mm_tn(dq, hm, "dw_q"), _mm_tn(dk, hm, "dw_k"), _mm_tn(dv, hm, "dw_v")])
            exchanges["attn"], token = _split_start(False, [d_w_o, d_w_qkv_t], "exchange_attn_start")
            g, dg_mix[l] = _mm_nn_rmsbwd([dq, dk, dv], w_qkv_t, xa, row(g_mix[l]), g, "qkv_bwd", dep=token)
        g, dg_ffn1[l], token = ffn_back(g, xin, row(g_ffn1[l]), a1, b1, h1, 2 * l, f"1_{l}", token)
    grad_x = g.reshape(n_seq, S, D)

    small = [jnp.concatenate(dg_ffn1), jnp.concatenate(dg_mix), d_cw, d_cb, d_lag, d_lab, d_lvg, d_lvb,
             jnp.where(tril[None], d_ws, 0.0), d_sb.T, jnp.concatenate(dg_ffn2), dg_final, loss_part[:, :1]]
    small_shapes = [(depth, D), (depth, D), (CONV_WIDTH, CA), (1, CA), (1, CA), (1, CA), (1, GB, DB), (1, GB, DB),
                    (1, GB, CHUNK, CHUNK), (1, GB, CHUNK), (depth, D), (D,), ()]
    da, db, s, gh, h = token
    small_gather, token = _split_start(True, [_pack(small)], "small_gather_start")

    for which, lhs, rhs in ((2, s, gh), (1, db, h), (0, da, h)):
        dw = _mm_tn(lhs, rhs, f"dw_ffn0_{which}", dep=token, out_dtype=BF16)
        exchanges[f"ffn0_{which}"], token = _split_start(False, [dw], f"exchange_ffn0_{which}_start")

    small_all, = _split_wait(small_gather, token, "small_gather_wait")[1]
    red = _unpack(_sum_blocks(small_all.reshape(N_DEV, -1, LANES)), small_shapes)
    (gr_g_ffn1, gr_g_mix, gr_cw_full, gr_cb, gr_lag, gr_lab, gr_lvg, gr_lvb, gr_sp_w, gr_sp_b,
     gr_g_ffn2, gr_g_final, loss) = red
    n_cw = conv_w.shape[2]
    gr_cw = lax.dynamic_slice(gr_cw_full, (0, my_block * n_cw), (CONV_WIDTH, n_cw))[None]

    def landed(key, after):
        return _split_wait(exchanges[key], after, f"exchange_{key}_wait")[1]

    parts_ffn = [None] * (6 * depth)
    for k in range(1, 2 * depth):
        parts_ffn[3 * k:3 * k + 3] = landed(f"ffn{k}", token)
    parts_out, parts_in = landed("mix", token)
    parts_o, parts_qkv = landed("attn", token)

    grads = {
        "g_ffn1": gr_g_ffn1, "g_mix": gr_g_mix, "conv_w": gr_cw, "conv_b": gr_cb, "ln_a_g": gr_lag,
        "ln_a_b": gr_lab, "ln_v_g": gr_lvg, "ln_v_b": gr_lvb, "sp_w": gr_sp_w, "sp_b": gr_sp_b,
        "g_ffn2": gr_g_ffn2, "g_final": gr_g_final,
    }
    weights = dict(g_ffn1=g_ffn1, w_ffn1_gate=w_ffn1_gate, w_ffn1_up=w_ffn1_up, w_ffn1_down=w_ffn1_down, g_mix=g_mix,
                   w_in_ab=w_in_ab, conv_w=conv_w, conv_b=conv_b, ln_a_g=ln_a_g, ln_a_b=ln_a_b, ln_v_g=ln_v_g,
                   ln_v_b=ln_v_b, sp_w=sp_w, sp_b=sp_b, w_out_ab=w_out_ab, w_qkv=w_qkv, w_o=w_o, g_ffn2=g_ffn2,
                   w_ffn2_gate=w_ffn2_gate, w_ffn2_up=w_ffn2_up, w_ffn2_down=w_ffn2_down, g_final=g_final)
    m_in = dict(g_ffn1=m_g_ffn1, w_ffn1_gate=m_w_ffn1_gate, w_ffn1_up=m_w_ffn1_up, w_ffn1_down=m_w_ffn1_down,
                g_mix=m_g_mix, w_in_ab=m_w_in_ab, conv_w=m_conv_w, conv_b=m_conv_b, ln_a_g=m_ln_a_g, ln_a_b=m_ln_a_b,
                ln_v_g=m_ln_v_g, ln_v_b=m_ln_v_b, sp_w=m_sp_w, sp_b=m_sp_b, w_out_ab=m_w_out_ab, w_qkv=m_w_qkv,
                w_o=m_w_o, g_ffn2=m_g_ffn2, w_ffn2_gate=m_w_ffn2_gate, w_ffn2_up=m_w_ffn2_up,
                w_ffn2_down=m_w_ffn2_down, g_final=m_g_final)
    v_in = dict(g_ffn1=v_g_ffn1, w_ffn1_gate=v_w_ffn1_gate, w_ffn1_up=v_w_ffn1_up, w_ffn1_down=v_w_ffn1_down,
                g_mix=v_g_mix, w_in_ab=v_w_in_ab, conv_w=v_conv_w, conv_b=v_conv_b, ln_a_g=v_ln_a_g, ln_a_b=v_ln_a_b,
                ln_v_g=v_ln_v_g, ln_v_b=v_ln_v_b, sp_w=v_sp_w, sp_b=v_sp_b, w_out_ab=v_w_out_ab, w_qkv=v_w_qkv,
                w_o=v_w_o, g_ffn2=v_g_ffn2, w_ffn2_gate=v_w_ffn2_gate, w_ffn2_up=v_w_ffn2_up,
                w_ffn2_down=v_w_ffn2_down, g_final=v_g_final)
    names = list(weights)
    grads = {n: grads[n].reshape(weights[n].shape) for n in grads}

    delta, new_m, new_v = {}, {}, {}

    def adamw_big(n, parts):
        if weights[n].shape[-1] == D:
            view = back = lambda a: a
        else:
            view = back = lambda a: jnp.swapaxes(a, 1, 2)
        out = _adamw_parts(parts, view(weights[n]), view(m_in[n]), view(v_in[n]), f"adamw_{n}")
        grads[n], delta[n], new_m[n], new_v[n] = [back(a) for a in out]

    adamw_big("w_in_ab", [parts_in])
    adamw_big("w_out_ab", [parts_out])
    adamw_big("w_qkv", [parts_qkv])
    adamw_big("w_o", [parts_o])
    kinds = ("gate", "up", "down")
    for which, kind in enumerate(kinds):
        adamw_big(f"w_ffn2_{kind}", [parts_ffn[_ffn_index(l, 1) + which] for l in range(depth)])
    big = [n for n in names if n.startswith("w_")]
    after = jnp.concatenate([delta[n].reshape(-1)[:1] for n in big if n in delta]).reshape(1, -1)
    for which in (2, 1, 0):
        parts_ffn[which], = landed(f"ffn0_{which}", after)
    for which, kind in enumerate(kinds):
        adamw_big(f"w_ffn1_{kind}", [parts_ffn[_ffn_index(l, 0) + which] for l in range(depth)])
    little = [n for n in names if n not in big]
    shapes = [weights[n].shape for n in little]
    d, nm, nv = _adamw(_pack([weights[n] for n in little]), _pack([grads[n] for n in little]),
                       _pack([m_in[n] for n in little]), _pack([v_in[n] for n in little]), "adamw_small")
    for n, dd, mm, vv in zip(little, _unpack(d, shapes), _unpack(nm, shapes), _unpack(nv, shapes)):
        delta[n], new_m[n], new_v[n] = dd, mm, vv

    return (loss, grad_x, *[grads[n] for n in names], *[delta[n] for n in names],
            *[new_m[n] for n in names], *[new_v[n] for n in names])
```

```python
import jax
import jax.numpy as jnp
from jax import lax
from jax.experimental import pallas as pl
from jax.experimental.pallas import tpu as pltpu

F32 = jnp.float32
BF16 = jnp.bfloat16

D_MODEL = 1024
CA = 512
CB = 512
GB = 4
DB = 128
CHUNK = 128
CONV_WIDTH = 31
N_HEADS = 16
HEAD_DIM = 64
EPS = 1e-6
N_DEV = 8
LANES = 128
SUBLANES = 8
QB = 128
ATT_TQ = 1024
FFN_TN = 2816
FFN_TM = 256
HALO = 32
CONV_ROWS = 32
ATT_SCALE = HEAD_DIM ** -0.5

ADAM_LR = 0.001
ADAM_B1 = 0.9
ADAM_B2 = 0.999
ADAM_EPS = 1e-08
ADAM_WD = 0.01
ADAM_STEP = 10

NT = (((1,), (1,)), ((), ()))
NN = (((1,), (0,)), ((), ()))
TN = (((0,), (0,)), ((), ()))
MESH = pl.DeviceIdType.MESH
ANY = pl.BlockSpec(memory_space=pl.ANY)
VMEM_LIMIT = 60 * 1024 * 1024


def _dot(a, b, dims):
    return lax.dot_general(a, b, dims, preferred_element_type=F32)


def _cp(*sem):
    return pltpu.CompilerParams(dimension_semantics=sem, vmem_limit_bytes=VMEM_LIMIT)


def _pcall(body, *, in_specs, args, dep=None, **kw):
    if dep is not None:
        n_in = len(in_specs)
        inner = body

        def body(*refs):
            inner(*refs[:n_in], *refs[n_in + 1:])

        in_specs = list(in_specs) + [ANY]
        args = tuple(args) + (dep,)
    return pl.pallas_call(body, in_specs=list(in_specs), **kw)(*args)


def _tile(n, want):
    if n <= want:
        return n
    t = want - want % LANES
    while t > LANES and n % t:
        t -= LANES
    assert n % t == 0, (n, want)
    return t


def _sigmoid(x):
    return 0.5 * jnp.tanh(0.5 * x) + 0.5


def _rstd(x):
    return lax.rsqrt(jnp.mean(x * x, axis=-1, keepdims=True) + EPS)


def _rms_bwd(x, g, dh):
    r = _rstd(x)
    u = dh * g
    dx = r * (u - x * (r * r) * jnp.mean(u * x, axis=-1, keepdims=True))
    dg = jnp.sum(dh * x * r, axis=0, keepdims=True)
    return dx, dg


def _ln_fwd(x, g, b):
    mu = jnp.mean(x, axis=-1, keepdims=True)
    xc = x - mu
    r = lax.rsqrt(jnp.mean(xc * xc, axis=-1, keepdims=True) + EPS)
    xh = xc * r
    return xh * g + b, xh, r


def _ln_bwd(dy, xh, r, g):
    dxh = dy * g
    return r * (dxh - jnp.mean(dxh, axis=-1, keepdims=True)
                - xh * jnp.mean(dxh * xh, axis=-1, keepdims=True))


def _ffn_fwd(x, g, wall, base, name, dep=None):
    T, D = x.shape
    F = wall.shape[1]
    tm, tn = _tile(T, FFN_TM), _tile(F, FFN_TN)
    n_j = F // tn

    def body(x_ref, g_ref, wg_ref, wu_ref, wd_ref, xo_ref, a_ref, b_ref, h_ref, acc_ref):
        j = pl.program_id(1)

        @pl.when(j == 0)
        def _():
            xv = x_ref[...]
            h_ref[...] = (xv * _rstd(xv) * g_ref[...]).astype(BF16)
            acc_ref[...] = jnp.zeros_like(acc_ref)

        h = h_ref[...]
        a = _dot(h, wg_ref[...], NT)
        b = _dot(h, wu_ref[...], NT)
        a_ref[...] = a.astype(BF16)
        b_ref[...] = b.astype(BF16)
        s = (a * _sigmoid(a) * b).astype(BF16)
        acc_ref[...] += _dot(s, wd_ref[...], NN)

        @pl.when(j == n_j - 1)
        def _():
            xo_ref[...] = x_ref[...] + 0.5 * acc_ref[...]

    single = pl.Buffered(1) if n_j == 1 else None
    wspec = lambda k: pl.BlockSpec((None, tn, D), lambda i, j: (base + k, j, 0), pipeline_mode=single)
    return _pcall(
        body, name=name, grid=(T // tm, n_j), dep=dep, args=(x, g, wall, wall, wall),
        in_specs=[pl.BlockSpec((tm, D), lambda i, j: (i, 0)), pl.BlockSpec((1, D), lambda i, j: (0, 0)),
                  wspec(0), wspec(1), wspec(2)],
        out_specs=[pl.BlockSpec((tm, D), lambda i, j: (i, 0)), pl.BlockSpec((tm, tn), lambda i, j: (i, j)),
                   pl.BlockSpec((tm, tn), lambda i, j: (i, j)), pl.BlockSpec((tm, D), lambda i, j: (i, 0))],
        out_shape=[jax.ShapeDtypeStruct((T, D), F32), jax.ShapeDtypeStruct((T, F), BF16),
                   jax.ShapeDtypeStruct((T, F), BF16), jax.ShapeDtypeStruct((T, D), BF16)],
        scratch_shapes=[pltpu.VMEM((tm, D), F32)],
        compiler_params=_cp("parallel", "arbitrary"),
    )


def _ffn_bwd(go, x, g, a, b, wall, base, name, dep=None):
    T, D = x.shape
    F = wall.shape[1]
    tm, tn = _tile(T, FFN_TM), _tile(F, FFN_TN)
    n_j = F // tn

    def body(go_ref, x_ref, g_ref, a_ref, b_ref, wg_ref, wu_ref, wd_ref,
             gx_ref, dg_ref, da_ref, db_ref, s_ref, gh_ref, acc_ref):
        i, j = pl.program_id(0), pl.program_id(1)

        @pl.when(j == 0)
        def _():
            gh_ref[...] = (0.5 * go_ref[...]).astype(BF16)
            acc_ref[...] = jnp.zeros_like(acc_ref)

        @pl.when((i == 0) & (j == 0))
        def _():
            dg_ref[...] = jnp.zeros_like(dg_ref)

        ds = _dot(gh_ref[...], wd_ref[...], NT)
        av = a_ref[...].astype(F32)
        bv = b_ref[...].astype(F32)
        sig = _sigmoid(av)
        sl = av * sig
        dab = ((ds * bv) * (sig + sl * (1.0 - sig))).astype(BF16)
        dbb = (ds * sl).astype(BF16)
        s_ref[...] = (sl * bv).astype(BF16)
        da_ref[...] = dab
        db_ref[...] = dbb
        acc_ref[...] += _dot(dab, wg_ref[...], NN) + _dot(dbb, wu_ref[...], NN)

        @pl.when(j == n_j - 1)
        def _():
            dx, dg = _rms_bwd(x_ref[...], g_ref[...], acc_ref[...])
            gx_ref[...] = go_ref[...] + dx
            dg_ref[...] += dg

    single = pl.Buffered(1) if n_j == 1 else None
    wspec = lambda k: pl.BlockSpec((None, tn, D), lambda i, j: (base + k, j, 0), pipeline_mode=single)
    row = pl.BlockSpec((tm, D), lambda i, j: (i, 0))
    hid = pl.BlockSpec((tm, tn), lambda i, j: (i, j))
    vec = pl.BlockSpec((1, D), lambda i, j: (0, 0))
    return _pcall(
        body, name=name, grid=(T // tm, n_j), dep=dep, args=(go, x, g, a, b, wall, wall, wall),
        in_specs=[row, row, vec, hid, hid, wspec(0), wspec(1), wspec(2)],
        out_specs=[row, vec, hid, hid, hid, row],
        out_shape=[jax.ShapeDtypeStruct((T, D), F32), jax.ShapeDtypeStruct((1, D), F32),
                   jax.ShapeDtypeStruct((T, F), BF16), jax.ShapeDtypeStruct((T, F), BF16),
                   jax.ShapeDtypeStruct((T, F), BF16), jax.ShapeDtypeStruct((T, D), BF16)],
        scratch_shapes=[pltpu.VMEM((tm, D), F32)],
        compiler_params=_cp("arbitrary", "arbitrary"),
    )


def _mm_tn(a, b, name, dep=None, out_dtype=F32):
    T, M = a.shape
    N = b.shape[1]
    tmm, tk = _tile(M, 1536), _tile(T, 2048)
    n_k = T // tk
    narrow = out_dtype != F32

    def body(a_ref, b_ref, o_ref, *scratch):
        acc_ref = scratch[0] if narrow else o_ref

        @pl.when(pl.program_id(1) == 0)
        def _():
            acc_ref[...] = jnp.zeros_like(acc_ref)

        acc_ref[...] += _dot(a_ref[...].astype(BF16), b_ref[...].astype(BF16), TN)
        if narrow:
            @pl.when(pl.program_id(1) == n_k - 1)
            def _():
                o_ref[...] = acc_ref[...].astype(out_dtype)

    return _pcall(
        body, name=name, grid=(M // tmm, n_k), dep=dep, args=(a, b),
        in_specs=[pl.BlockSpec((tk, tmm), lambda m, k: (k, m)), pl.BlockSpec((tk, N), lambda m, k: (k, 0))],
        out_specs=pl.BlockSpec((tmm, N), lambda m, k: (m, 0)),
        out_shape=jax.ShapeDtypeStruct((M, N), out_dtype),
        scratch_shapes=[pltpu.VMEM((tmm, N), F32)] if narrow else [],
        compiler_params=_cp("parallel", "arbitrary"),
    )


def _mm_nt(x, wt, g, out_dtype, name, dep=None):
    T, K = x.shape
    N = wt.shape[0]
    tm, tn = _tile(T, 512), N
    norm = g is not None

    def body(*refs):
        if norm:
            x_ref, g_ref, w_ref, o_ref, h_ref = refs
        else:
            x_ref, w_ref, o_ref, h_ref = refs

        @pl.when(pl.program_id(1) == 0)
        def _():
            xv = x_ref[...].astype(F32)
            if norm:
                xv = xv * _rstd(xv) * g_ref[...]
            h_ref[...] = xv.astype(BF16)

        o_ref[...] = _dot(h_ref[...], w_ref[...], NT).astype(out_dtype)

    row = pl.BlockSpec((tm, K), lambda i, j: (i, 0))
    wsp = pl.BlockSpec((tn, K), lambda i, j: (j, 0))
    osp = pl.BlockSpec((tm, tn), lambda i, j: (i, j))
    if norm:
        return pl.pallas_call(
            body, name=name, grid=(T // tm, N // tn),
            in_specs=[row, pl.BlockSpec((1, K), lambda i, j: (0, 0)), wsp],
            out_specs=[osp, row],
            out_shape=[jax.ShapeDtypeStruct((T, N), out_dtype), jax.ShapeDtypeStruct((T, K), BF16)],
            compiler_params=_cp("parallel", "arbitrary"),
        )(x, g, wt)
    return _pcall(
        body, name=name, grid=(T // tm, N // tn), dep=dep, args=(x, wt),
        in_specs=[row, wsp], out_specs=osp,
        out_shape=jax.ShapeDtypeStruct((T, N), out_dtype),
        scratch_shapes=[pltpu.VMEM((tm, K), BF16)],
        compiler_params=_cp("parallel", "arbitrary"),
    )


def _mm_nn_res(act, w, resid, name):
    T, K = act.shape
    D = w.shape[1]
    tm = _tile(T, 512)

    def body(a_ref, w_ref, r_ref, o_ref):
        o_ref[...] = r_ref[...] + _dot(a_ref[...].astype(BF16), w_ref[...], NN)

    return pl.pallas_call(
        body, name=name, grid=(T // tm,),
        in_specs=[pl.BlockSpec((tm, K), lambda i: (i, 0)), pl.BlockSpec((K, D), lambda i: (0, 0)),
                  pl.BlockSpec((tm, D), lambda i: (i, 0))],
        out_specs=pl.BlockSpec((tm, D), lambda i: (i, 0)),
        out_shape=jax.ShapeDtypeStruct((T, D), F32),
        compiler_params=_cp("parallel"),
    )(act, w, resid)


def _mm_nn_rmsbwd(acts, w, x, g, gprev, name, dep=None):
    T = acts[0].shape[0]
    ks = [a.shape[1] for a in acts]
    K, D = w.shape
    assert sum(ks) == K
    tm = _tile(T, 512)
    na = len(acts)

    def body(*refs):
        a_refs = refs[:na]
        w_ref, x_ref, g_ref, gp_ref, o_ref, dg_ref = refs[na:]

        @pl.when(pl.program_id(0) == 0)
        def _():
            dg_ref[...] = jnp.zeros_like(dg_ref)

        dh, off = None, 0
        for a_ref, k in zip(a_refs, ks):
            part = _dot(a_ref[...].astype(BF16), w_ref[off:off + k, :], NN)
            dh = part if dh is None else dh + part
            off += k
        dx, dg = _rms_bwd(x_ref[...], g_ref[...], dh)
        o_ref[...] = gp_ref[...] + dx
        dg_ref[...] += dg

    row = pl.BlockSpec((tm, D), lambda i: (i, 0))
    vec = pl.BlockSpec((1, D), lambda i: (0, 0))
    return _pcall(
        body, name=name, grid=(T // tm,), dep=dep, args=(*acts, w, x, g, gprev),
        in_specs=[pl.BlockSpec((tm, k), lambda i: (i, 0)) for k in ks]
        + [pl.BlockSpec((K, D), lambda i: (0, 0)), row, vec, row],
        out_specs=[row, vec],
        out_shape=[jax.ShapeDtypeStruct((T, D), F32), jax.ShapeDtypeStruct((1, D), F32)],
        compiler_params=_cp("arbitrary"),
    )


def _loss_head(x, g, target):
    T, D = x.shape
    tm = _tile(T, 512)

    def body(x_ref, g_ref, t_ref, dx_ref, loss_ref, dg_ref):
        @pl.when(pl.program_id(0) == 0)
        def _():
            loss_ref[...] = jnp.zeros_like(loss_ref)
            dg_ref[...] = jnp.zeros_like(dg_ref)

        xv = x_ref[...]
        gv = g_ref[...]
        e = xv * _rstd(xv) * gv - t_ref[...]
        per_tok = jnp.sum(e * e, axis=-1, keepdims=True) * (1.0 / D)
        loss_ref[...] += 0.5 * jnp.sum(per_tok, axis=0, keepdims=True)
        dx, dg = _rms_bwd(xv, gv, e * (1.0 / D))
        dx_ref[...] = dx
        dg_ref[...] += dg

    row = pl.BlockSpec((tm, D), lambda i: (i, 0))
    vec = pl.BlockSpec((1, D), lambda i: (0, 0))
    return pl.pallas_call(
        body, name="loss_head", grid=(T // tm,),
        in_specs=[row, vec, row],
        out_specs=[row, pl.BlockSpec((1, LANES), lambda i: (0, 0)), vec],
        out_shape=[jax.ShapeDtypeStruct((T, D), F32), jax.ShapeDtypeStruct((1, LANES), F32),
                   jax.ShapeDtypeStruct((1, D), F32)],
        compiler_params=_cp("arbitrary"),
    )(x, g, target)


def _log_gates(z):
    neg_abs = lax.bitcast_convert_type(lax.bitcast_convert_type(z, jnp.uint32) | jnp.uint32(0x80000000), F32)
    ls = jnp.minimum(z, 0.0) - jnp.log(1.0 + jnp.exp(neg_abs))
    return ls, ls - z


def _cumsum_mm(v, u):
    return _dot(v.astype(BF16), u, NN)


def _half_rowsum(v):
    n = v.shape[0]
    s0 = jnp.sum(v[:, :QB], axis=1, keepdims=True)
    s1 = jnp.sum(v[:, QB:], axis=1, keepdims=True)
    return jnp.concatenate([jnp.broadcast_to(s0, (n, QB)), jnp.broadcast_to(s1, (n, QB))], axis=1)


def _stack_heads(src_ref, dst_ref, n_blk):
    m0 = lax.broadcasted_iota(jnp.int32, (1, LANES), 1) < HEAD_DIM

    def fill(c, carry):
        blk = src_ref[pl.ds(pl.multiple_of(c * QB, QB), QB), :]
        zero = jnp.zeros_like(blk)
        dst_ref[c, 0:QB, :] = jnp.where(m0, blk, zero)
        dst_ref[c, QB:2 * QB, :] = jnp.where(m0, zero, blk)
        return carry

    lax.fori_loop(0, n_blk, fill, 0)


def _diag_mask(tq, j):
    n = tq - j * QB
    row = lax.broadcasted_iota(jnp.int32, (n, 2 * QB), 0)
    col = lax.broadcasted_iota(jnp.int32, (n, 2 * QB), 1)
    return (col & (QB - 1)) < row


def _tri_blockdiag(upper):
    r = lax.broadcasted_iota(jnp.int32, (2 * QB, 2 * QB), 0)
    c = lax.broadcasted_iota(jnp.int32, (2 * QB, 2 * QB), 1)
    same = (r // QB) == (c // QB)
    return (same & ((r > c) if upper else (r < c))).astype(BF16)


def _attn_tiles(T, n_seq):
    S = T // n_seq
    tq = ATT_TQ if S % ATT_TQ == 0 else 2 * QB
    assert S % tq == 0
    return S, tq, tq // QB, S // tq, S // QB


def _attn_fwd(qkv, n_seq):
    T = qkv.shape[0]
    S, tq, r, n_q, n_k = _attn_tiles(T, n_seq)
    n_p = D_MODEL // LANES
    n_steps = n_seq * n_p * n_q
    u_suffix = _tri_blockdiag(True)

    def body(q_ref, k_ref, v_ref, u_ref, o_ref, a_hbm, kk_ref, vv_ref, lr_s, acc_s, a_stage, sems):
        qi = pl.program_id(2)
        group = (pl.program_id(0) * n_p + pl.program_id(1)) * n_q + qi

        @pl.when(qi == 0)
        def _():
            _stack_heads(k_ref, kk_ref, n_k)
            _stack_heads(v_ref, vv_ref, n_k)

        u = u_ref[...]
        lr_s[...] = jnp.zeros_like(lr_s)
        acc_s[...] = jnp.zeros_like(acc_s)
        base = ((group // n_q) * (n_q * (n_q + 1) // 2) + (qi * (qi + 1)) // 2) % 2

        def saves(first_kj, half):
            return [pltpu.make_async_copy(a_stage.at[half, j], a_hbm.at[group, first_kj - j], sems.at[half])
                    for j in range(r)]

        @pl.when(group >= 2)
        def _():
            for cp in saves(0, base):
                cp.wait()

        @pl.when(group == 0)
        def _():
            a_stage[...] = jnp.zeros_like(a_stage)

        def step(kj, half, j, rows, q, mask, lr, acc):
            ls, lk = _log_gates(_dot(q, kk_ref[kj], NT))
            if mask is not None:
                lk = jnp.where(mask, lk, 0.0)
            a = jnp.exp(ls + _cumsum_mm(lk, u) + lr)
            if mask is not None:
                a = jnp.where(mask, a, 0.0)
            a = a.astype(BF16)
            a_stage[half, j, rows, :] = a
            return lr + _half_rowsum(lk), acc + _dot(a, vv_ref[kj], NN)

        last = (qi + 1) * r - 1
        for n in range(r):
            rows = slice((r - 1 - n) * QB, tq)
            lr, acc = step(last - n, base, n, rows, q_ref[rows, :] * ATT_SCALE, _diag_mask(tq, r - 1 - n),
                           lr_s[rows, :], acc_s[rows, :])
            lr_s[rows, :] = lr
            acc_s[rows, :] = acc
        for cp in saves(last, base):
            cp.start()

        q = q_ref[...] * ATT_SCALE

        def off(it, carry):
            half = (base + it + 1) % 2
            first = (qi - it) * r - 1
            for cp in saves(first, half):
                cp.wait()
            lr, acc = lr_s[...], acc_s[...]
            for j in range(r):
                lr, acc = step(first - j, half, j, slice(0, tq), q, None, lr, acc)
            lr_s[...] = lr
            acc_s[...] = acc
            for cp in saves(first, half):
                cp.start()
            return carry

        lax.fori_loop(0, qi, off, 0)

        @pl.when(group == n_steps - 1)
        def _():
            for cp in saves(0, (base + qi) % 2):
                cp.wait()
            if n_steps * n_q > 1:
                for cp in saves(0, (base + qi + 1) % 2):
                    cp.wait()

        o_ref[...] = acc_s[...].astype(BF16)

    return pl.pallas_call(
        body, name="attn_fwd", grid=(n_seq, n_p, n_q),
        in_specs=[pl.BlockSpec((tq, LANES), lambda b, p, qi: (b * n_q + qi, p)),
                  pl.BlockSpec((S, LANES), lambda b, p, qi: (b, n_p + p)),
                  pl.BlockSpec((S, LANES), lambda b, p, qi: (b, 2 * n_p + p)),
                  pl.BlockSpec((2 * QB, 2 * QB), lambda b, p, qi: (0, 0))],
        out_specs=[pl.BlockSpec((tq, LANES), lambda b, p, qi: (b * n_q + qi, p)), ANY],
        out_shape=[jax.ShapeDtypeStruct((T, D_MODEL), BF16),
                   jax.ShapeDtypeStruct((n_seq * n_p * n_q, n_k, tq, 2 * QB), BF16)],
        scratch_shapes=[pltpu.VMEM((n_k, 2 * QB, LANES), BF16), pltpu.VMEM((n_k, 2 * QB, LANES), BF16),
                        pltpu.VMEM((tq, 2 * QB), F32), pltpu.VMEM((tq, LANES), F32),
                        pltpu.VMEM((2, r, tq, 2 * QB), BF16), pltpu.SemaphoreType.DMA((2,))],
        compiler_params=_cp("arbitrary", "arbitrary", "arbitrary"),
    )(qkv, qkv, qkv, u_suffix)


def _attn_bwd(qkv, do, a_saved, n_seq):
    T = qkv.shape[0]
    S, tq, r, n_q, n_k = _attn_tiles(T, n_seq)
    n_p = D_MODEL // LANES
    n_steps = n_seq * n_p * n_q
    u_prefix = _tri_blockdiag(False)

    def body(q_ref, k_ref, v_ref, do_ref, u_ref, a_hbm, dq_ref, dk_out, dv_out,
             kk_ref, vv_ref, cg_s, dq_s, dk_ref, dv_ref, a_stage, sems):
        qi = pl.program_id(2)
        group = (pl.program_id(0) * n_p + pl.program_id(1)) * n_q + qi

        base = ((group // n_q) * (n_q * (n_q + 1) // 2) + (qi * (qi + 1)) // 2) % 2

        def fetches(grp, g, half):
            return [pltpu.make_async_copy(a_hbm.at[grp, g * r + j], a_stage.at[half, j], sems.at[half])
                    for j in range(r)]

        @pl.when(group == 0)
        def _():
            for cp in fetches(group, 0, 0):
                cp.start()

        @pl.when(qi == 0)
        def _():
            _stack_heads(k_ref, kk_ref, n_k)
            _stack_heads(v_ref, vv_ref, n_k)
            dk_ref[...] = jnp.zeros_like(dk_ref)
            dv_ref[...] = jnp.zeros_like(dv_ref)

        u = u_ref[...]
        cg_s[...] = jnp.zeros_like(cg_s)
        dq_s[...] = jnp.zeros_like(dq_s)

        def step(kj, half, j, rows, q, dov, mask, cg, dq):
            kk = kk_ref[kj]
            beta = _sigmoid(_dot(q, kk, NT))
            a = a_stage[half, j, rows, :]
            g = a.astype(F32) * _dot(dov, vv_ref[kj], NT)
            dz = g - (g + _dot(g.astype(BF16), u, NN) + cg) * beta
            if mask is not None:
                dz = jnp.where(mask, dz, 0.0)
            dz = dz.astype(BF16)
            keys = pl.ds(pl.multiple_of(kj * QB, QB), QB)
            dvt = _dot(dov_t[:, rows], a, NN)
            dv_ref[keys, :] += jnp.where(t0, dvt[:, :QB], dvt[:, QB:]).T
            dkt = _dot(q_t[:, rows], dz, NN)
            dk_ref[keys, :] += jnp.where(t0, dkt[:, :QB], dkt[:, QB:]).T
            return cg + _half_rowsum(g), dq + _dot(dz, kk, NN)

        q = q_ref[...] * ATT_SCALE
        dov = do_ref[...]
        q_t = q.astype(F32).T.astype(BF16)
        dov_t = dov.astype(F32).T.astype(BF16)
        t0 = lax.broadcasted_iota(jnp.int32, (LANES, 1), 0) < HEAD_DIM

        def off(it, carry):
            half = (base + it) % 2
            for cp in fetches(group, it, half):
                cp.wait()
            for cp in fetches(group, it + 1, 1 - half):
                cp.start()
            cg, dq = cg_s[...], dq_s[...]
            for j in range(r):
                cg, dq = step(it * r + j, half, j, slice(0, tq), q, dov, None, cg, dq)
            cg_s[...] = cg
            dq_s[...] = dq
            return carry

        lax.fori_loop(0, qi, off, 0)

        half = (base + qi) % 2
        for cp in fetches(group, qi, half):
            cp.wait()

        @pl.when(group < n_steps - 1)
        def _():
            for cp in fetches(group + 1, 0, 1 - half):
                cp.start()

        for j in range(r):
            rows = slice(j * QB, tq)
            cg, dq = step(qi * r + j, half, j, rows, q_ref[rows, :] * ATT_SCALE, do_ref[rows, :],
                          _diag_mask(tq, j), cg_s[rows, :], dq_s[rows, :])
            cg_s[rows, :] = cg
            dq_s[rows, :] = dq
        dq_ref[...] = (dq_s[...] * ATT_SCALE).astype(BF16)

        @pl.when(qi == n_q - 1)
        def _():
            dk_out[...] = dk_ref[...].astype(BF16)
            dv_out[...] = dv_ref[...].astype(BF16)

    qspec = pl.BlockSpec((tq, LANES), lambda b, p, qi: (b * n_q + qi, p))
    seq = lambda off: pl.BlockSpec((S, LANES), lambda b, p, qi: (b, off + p))
    return pl.pallas_call(
        body, name="attn_bwd", grid=(n_seq, n_p, n_q),
        in_specs=[qspec, seq(n_p), seq(2 * n_p), qspec,
                  pl.BlockSpec((2 * QB, 2 * QB), lambda b, p, qi: (0, 0)), ANY],
        out_specs=[qspec, seq(0), seq(0)],
        out_shape=[jax.ShapeDtypeStruct((T, D_MODEL), BF16)] * 3,
        scratch_shapes=[pltpu.VMEM((n_k, 2 * QB, LANES), BF16), pltpu.VMEM((n_k, 2 * QB, LANES), BF16),
                        pltpu.VMEM((tq, 2 * QB), F32), pltpu.VMEM((tq, LANES), F32),
                        pltpu.VMEM((S, LANES), F32), pltpu.VMEM((S, LANES), F32),
                        pltpu.VMEM((2, r, tq, 2 * QB), BF16), pltpu.SemaphoreType.DMA((2,))],
        compiler_params=_cp("arbitrary", "arbitrary", "arbitrary"),
    )(qkv, qkv, qkv, do, u_prefix, a_saved)


def _shifted_copies(sh_ref):
    rows = sh_ref.shape[1] - SUBLANES
    for s in range(1, SUBLANES):
        sh_ref[s, 0:rows, :] = sh_ref[0, s:s + rows, :]


def _shifted(sh_ref, start, n):
    s = start % SUBLANES
    return sh_ref[s, start - s:start - s + n, :]


def _glu_with_halo(av_ref, ag_ref, avh_ref, agh_ref, a0_s, first, ts):
    hal = avh_ref[...] * _sigmoid(agh_ref[...])
    a0_s[0, 0:HALO, :] = jnp.where(first, 0.0, hal)
    a0_s[0, HALO:HALO + ts, :] = av_ref[...] * _sigmoid(ag_ref[...])
    _shifted_copies(a0_s)


def _mix_specs(ts, n_r, with_left):
    blk = lambda c: pl.BlockSpec((ts, CA), lambda b, r: (b * n_r + r, c))
    per = ts // HALO
    left = lambda c: pl.BlockSpec((HALO, CA), lambda b, r: (jnp.maximum((b * n_r + r) * per - 1, 0), c))
    return blk, (left if with_left else None)


def _mix_fwd(z, conv_w, conv_b, ln_a_g, ln_a_b, ln_v_g, ln_v_b, ws, bias2d, n_seq):
    T = z.shape[0]
    S = T // n_seq
    ts = _tile(S, 512)
    n_r = S // ts
    shift = HALO - (CONV_WIDTH - 1)

    def body(av_ref, ag_ref, avh_ref, agh_ref, u_ref, v_ref, cw_ref, cb_ref, lag_ref, lab_ref,
             lvg_ref, lvb_ref, ws_ref, bias_ref, cat_ref, a1_ref, a0_s):
        _glu_with_halo(av_ref, ag_ref, avh_ref, agh_ref, a0_s, pl.program_id(1) == 0, ts)
        for rb in range(ts // CONV_ROWS):
            base = rb * CONV_ROWS
            acc = jnp.broadcast_to(cb_ref[...], (CONV_ROWS, CA))
            for k in range(CONV_WIDTH):
                acc = acc + cw_ref[k:k + 1, :] * _shifted(a0_s, base + shift + k, CONV_ROWS)
            a1_ref[base:base + CONV_ROWS, :] = acc
        y, _, _ = _ln_fwd(a1_ref[...], lag_ref[...], lab_ref[...])
        cat_ref[:, 0:CA] = (y * _sigmoid(y)).astype(BF16)
        for gi in range(GB):
            sl = slice(gi * DB, (gi + 1) * DB)
            v1, _, _ = _ln_fwd(v_ref[:, sl], lvg_ref[:, sl], lvb_ref[:, sl])
            v1 = v1.astype(BF16)
            for c in range(ts // CHUNK):
                rs = slice(c * CHUNK, (c + 1) * CHUNK)
                v2 = _dot(ws_ref[gi], v1[rs], NN) + bias_ref[:, sl]
                cat_ref[rs, CA + gi * DB:CA + (gi + 1) * DB] = (u_ref[rs, sl] * v2).astype(BF16)

    blk, left = _mix_specs(ts, n_r, True)
    vec = pl.BlockSpec((1, CA), lambda b, r: (0, 0))
    return pl.pallas_call(
        body, name="mix_fwd", grid=(n_seq, n_r),
        in_specs=[blk(0), blk(1), left(0), left(1), blk(2), blk(3),
                  pl.BlockSpec((CONV_WIDTH, CA), lambda b, r: (0, 0)), vec, vec, vec, vec, vec,
                  pl.BlockSpec((GB, CHUNK, CHUNK), lambda b, r: (0, 0, 0)),
                  pl.BlockSpec((CHUNK, CB), lambda b, r: (0, 0))],
        out_specs=[pl.BlockSpec((ts, CA + CB), lambda b, r: (b * n_r + r, 0)), blk(0)],
        out_shape=[jax.ShapeDtypeStruct((T, CA + CB), BF16), jax.ShapeDtypeStruct((T, CA), F32)],
        scratch_shapes=[pltpu.VMEM((SUBLANES, HALO + ts, CA), F32)],
        compiler_params=_cp("parallel", "parallel"),
    )(z, z, z, z, z, z, conv_w, conv_b, ln_a_g, ln_a_b, ln_v_g, ln_v_b, ws, bias2d)


def _mix_bwd_rows(dcat, z, a1, ln_a_g, ln_a_b, ln_v_g, ln_v_b, ws, ws_t, bias2d, n_seq):
    T = z.shape[0]
    S = T // n_seq
    ts = _tile(S, 512)
    n_r = S // ts

    def body(dc_ref, u_ref, v_ref, a1_ref, lag_ref, lab_ref, lvg_ref, lvb_ref, ws_ref, wst_ref, bias_ref,
             da1_ref, dz_ref, dlag_ref, dlab_ref, dlvg_ref, dlvb_ref, dws_ref, dsb_ref, dv1_s, dbias_s):
        first = (pl.program_id(0) == 0) & (pl.program_id(1) == 0)
        last = (pl.program_id(0) == n_seq - 1) & (pl.program_id(1) == n_r - 1)

        @pl.when(first)
        def _():
            for ref in (dlag_ref, dlab_ref, dlvg_ref, dlvb_ref, dws_ref, dbias_s):
                ref[...] = jnp.zeros_like(ref)

        lag = lag_ref[...]
        y, xh, r = _ln_fwd(a1_ref[...], lag, lab_ref[...])
        sig = _sigmoid(y)
        dy = dc_ref[:, 0:CA] * (sig * (1.0 + y * (1.0 - sig)))
        dlag_ref[...] += jnp.sum(dy * xh, axis=0, keepdims=True)
        dlab_ref[...] += jnp.sum(dy, axis=0, keepdims=True)
        da1_ref[...] = _ln_bwd(dy, xh, r, lag)

        tril = (lax.broadcasted_iota(jnp.int32, (CHUNK, CHUNK), 0)
                >= lax.broadcasted_iota(jnp.int32, (CHUNK, CHUNK), 1))
        for gi in range(GB):
            sl = slice(gi * DB, (gi + 1) * DB)
            lvg = lvg_ref[:, sl]
            v1, vh, vr = _ln_fwd(v_ref[:, sl], lvg, lvb_ref[:, sl])
            v1 = v1.astype(BF16)
            for c in range(ts // CHUNK):
                rs = slice(c * CHUNK, (c + 1) * CHUNK)
                v2 = _dot(ws_ref[gi], v1[rs], NN) + bias_ref[:, sl]
                dbo = dc_ref[rs, CA + gi * DB:CA + (gi + 1) * DB]
                dz_ref[rs, sl] = (dbo * v2).astype(BF16)
                dv2 = dbo * u_ref[rs, sl]
                dbias_s[:, sl] += dv2
                dv2b = dv2.astype(BF16)
                dws_ref[gi] += jnp.where(tril, _dot(dv2b, v1[rs], NT), 0.0)
                dv1_s[rs, :] = _dot(wst_ref[gi], dv2b, NN)
            dv1 = dv1_s[...]
            dlvg_ref[:, sl] += jnp.sum(dv1 * vh, axis=0, keepdims=True)
            dlvb_ref[:, sl] += jnp.sum(dv1, axis=0, keepdims=True)
            dz_ref[:, CB + gi * DB:CB + (gi + 1) * DB] = _ln_bwd(dv1, vh, vr, lvg).astype(BF16)

        @pl.when(last)
        def _():
            col = lax.broadcasted_iota(jnp.int32, (CHUNK, GB), 1)
            out = jnp.zeros((CHUNK, GB), F32)
            for gi in range(GB):
                s = jnp.sum(dbias_s[:, gi * DB:(gi + 1) * DB], axis=1, keepdims=True)
                out = out + jnp.where(col == gi, s, 0.0)
            dsb_ref[...] = out

    blk, _ = _mix_specs(ts, n_r, False)
    vec = pl.BlockSpec((1, CA), lambda b, r: (0, 0))
    mat = pl.BlockSpec((GB, CHUNK, CHUNK), lambda b, r: (0, 0, 0))
    wide = pl.BlockSpec((ts, CA + CB), lambda b, r: (b * n_r + r, 0))
    return pl.pallas_call(
        body, name="mix_bwd_rows", grid=(n_seq, n_r),
        in_specs=[wide, blk(2), blk(3), blk(0), vec, vec, vec, vec, mat, mat,
                  pl.BlockSpec((CHUNK, CB), lambda b, r: (0, 0))],
        out_specs=[blk(0), wide, vec, vec, vec, vec, mat, pl.BlockSpec((CHUNK, GB), lambda b, r: (0, 0))],
        out_shape=[jax.ShapeDtypeStruct((T, CA), F32), jax.ShapeDtypeStruct((T, 2 * CB), BF16)]
        + [jax.ShapeDtypeStruct((1, CA), F32)] * 4
        + [jax.ShapeDtypeStruct((GB, CHUNK, CHUNK), F32), jax.ShapeDtypeStruct((CHUNK, GB), F32)],
        scratch_shapes=[pltpu.VMEM((ts, DB), F32), pltpu.VMEM((CHUNK, CB), F32)],
        compiler_params=_cp("arbitrary", "arbitrary"),
    )(dcat, z, z, a1, ln_a_g, ln_a_b, ln_v_g, ln_v_b, ws, ws_t, bias2d)


def _mix_bwd_conv(da1, z, conv_w, n_seq):
    T = z.shape[0]
    S = T // n_seq
    ts = _tile(S, 512)
    n_r = S // ts
    per = ts // HALO
    shift = HALO - (CONV_WIDTH - 1)
    fold = CONV_ROWS // 8

    def body(d_ref, dh_ref, av_ref, ag_ref, avh_ref, agh_ref, cw_ref,
             dz_ref, dcw_ref, dcb_ref, a0_s, d1_s, da0_s, dw8_s):
        first = (pl.program_id(0) == 0) & (pl.program_id(1) == 0)
        last = (pl.program_id(0) == n_seq - 1) & (pl.program_id(1) == n_r - 1)

        @pl.when(first)
        def _():
            dw8_s[...] = jnp.zeros_like(dw8_s)
            dcb_ref[...] = jnp.zeros_like(dcb_ref)

        _glu_with_halo(av_ref, ag_ref, avh_ref, agh_ref, a0_s, pl.program_id(1) == 0, ts)
        d1_s[0, 0:ts, :] = d_ref[...]
        d1_s[0, ts:ts + HALO, :] = jnp.where(pl.program_id(1) == n_r - 1, 0.0, dh_ref[...])
        _shifted_copies(d1_s)
        dcb_ref[...] += jnp.sum(d_ref[...], axis=0, keepdims=True)
        for rb in range(ts // CONV_ROWS):
            base = rb * CONV_ROWS
            dcur = d1_s[0, base:base + CONV_ROWS, :]
            acc = jnp.zeros((CONV_ROWS, CA), F32)
            for k in range(CONV_WIDTH):
                back = CONV_WIDTH - 1 - k
                acc = acc + cw_ref[k:k + 1, :] * _shifted(d1_s, base + back, CONV_ROWS)
                prod = dcur * _shifted(a0_s, base + shift + k, CONV_ROWS)
                part = prod[0:8]
                for f in range(1, fold):
                    part = part + prod[8 * f:8 * f + 8]
                dw8_s[k] += part
            da0_s[base:base + CONV_ROWS, :] = acc
        da0 = da0_s[...]
        sig = _sigmoid(ag_ref[...])
        dz_ref[:, 0:CA] = (da0 * sig).astype(BF16)
        dz_ref[:, CA:2 * CA] = (da0 * av_ref[...] * sig * (1.0 - sig)).astype(BF16)

        @pl.when(last)
        def _():
            for k in range(CONV_WIDTH):
                dcw_ref[k:k + 1, :] = jnp.sum(dw8_s[k], axis=0, keepdims=True)

    blk, left = _mix_specs(ts, n_r, True)
    n_halo_blocks = T // HALO
    right = pl.BlockSpec((HALO, CA), lambda b, r: (jnp.minimum((b * n_r + r + 1) * per, n_halo_blocks - 1), 0))
    return pl.pallas_call(
        body, name="mix_bwd_conv", grid=(n_seq, n_r),
        in_specs=[blk(0), right, blk(0), blk(1), left(0), left(1),
                  pl.BlockSpec((CONV_WIDTH, CA), lambda b, r: (0, 0))],
        out_specs=[pl.BlockSpec((ts, 2 * CA), lambda b, r: (b * n_r + r, 0)),
                   pl.BlockSpec((CONV_WIDTH, CA), lambda b, r: (0, 0)), pl.BlockSpec((1, CA), lambda b, r: (0, 0))],
        out_shape=[jax.ShapeDtypeStruct((T, 2 * CA), BF16), jax.ShapeDtypeStruct((CONV_WIDTH, CA), F32),
                   jax.ShapeDtypeStruct((1, CA), F32)],
        scratch_shapes=[pltpu.VMEM((SUBLANES, HALO + ts, CA), F32), pltpu.VMEM((SUBLANES, ts + HALO, CA), F32),
                        pltpu.VMEM((ts, CA), F32), pltpu.VMEM((CONV_WIDTH, 8, CA), F32)],
        compiler_params=_cp("arbitrary", "arbitrary"),
    )(da1, da1, z, z, z, z, conv_w)


def _row_tile(R, want):
    t = min(R, want)
    t -= t % 8
    while t > 8 and R % t:
        t -= 8
    return t if t >= 8 and R % t == 0 else R


def _adam_step(w, g, m, v):
    nm = ADAM_B1 * m + (1.0 - ADAM_B1) * g
    nv = ADAM_B2 * v + (1.0 - ADAM_B2) * (g * g)
    m_hat = nm / (1.0 - ADAM_B1 ** ADAM_STEP)
    v_hat = nv / (1.0 - ADAM_B2 ** ADAM_STEP)
    return -ADAM_LR * (m_hat / (jnp.sqrt(v_hat) + ADAM_EPS) + ADAM_WD * w), nm, nv


def _adamw_parts(parts, w, m, v, name):
    L, n, C = w.shape
    assert len(parts) == L
    tr = _row_tile(n, 192)
    n_i = n // tr

    def body(*refs):
        p_refs = refs[:L]
        w_ref, m_ref, v_ref, g_ref, d_ref, nm_ref, nv_ref = refs[L:]
        for k in range(L):
            @pl.when(pl.program_id(0) == k)
            def _(k=k):
                acc = p_refs[k][0].astype(F32)
                for s in range(1, N_DEV):
                    acc = acc + p_refs[k][s].astype(F32)
                g_ref[...] = acc

        d_ref[...], nm_ref[...], nv_ref[...] = _adam_step(w_ref[...], g_ref[...], m_ref[...], v_ref[...])

    def part_spec(k):
        return pl.BlockSpec((N_DEV, tr, C),
                            lambda l, i: (0, jnp.where(l == k, i, jnp.where(l < k, 0, n_i - 1)), 0))

    blk = pl.BlockSpec((None, tr, C), lambda l, i: (l, i, 0))
    return pl.pallas_call(
        body, name=name, grid=(L, n_i),
        in_specs=[part_spec(k) for k in range(L)] + [blk] * 3, out_specs=[blk] * 4,
        out_shape=[jax.ShapeDtypeStruct((L, n, C), F32)] * 4,
        compiler_params=_cp("arbitrary", "arbitrary"),
    )(*parts, w, m, v)


def _adamw(w, g, m, v, name):
    R, C = w.shape
    tr = _row_tile(R, 256)

    def body(w_ref, g_ref, m_ref, v_ref, d_ref, nm_ref, nv_ref):
        d_ref[...], nm_ref[...], nv_ref[...] = _adam_step(w_ref[...], g_ref[...], m_ref[...], v_ref[...])

    blk = pl.BlockSpec((tr, C), lambda i: (i, 0))
    return pl.pallas_call(
        body, name=name, grid=(R // tr,),
        in_specs=[blk] * 4, out_specs=[blk] * 3,
        out_shape=[jax.ShapeDtypeStruct((R, C), F32)] * 3,
        compiler_params=_cp("parallel"),
    )(w, g, m, v)


def _me():
    return lax.axis_index("x"), lax.axis_index("y"), lax.axis_index("c")


def _block_rows(ref, dev, n):
    start = (4 * dev[0] + 2 * dev[1] + dev[2]) * n
    if len(ref.shape) == 2:
        return ref.at[pl.ds(start, n), :]
    return ref.at[:, pl.ds(start, n), :]


def _all_gather(shards):
    na = len(shards)
    ns = [s.shape[-2] for s in shards]

    def body(*refs):
        ins, outs = refs[:na], refs[na:2 * na]
        send_sems, recv_sems, local_sems = refs[2 * na:]
        x, y, c = _me()
        me, sibling = (x, y, c), (x, y, 1 - c)
        chips = [(1 - x, y), (x, 1 - y), (1 - x, 1 - y)]

        def copy(a, k, block, to, src=None):
            dst = _block_rows(outs[a], block, ns[a])
            return pltpu.make_async_remote_copy(
                src_ref=dst if src is None else src, dst_ref=dst,
                send_sem=send_sems.at[a, k], recv_sem=recv_sems.at[a, k], device_id=to, device_id_type=MESH)

        mine = [pltpu.make_async_copy(ins[a], _block_rows(outs[a], me, ns[a]), local_sems.at[a]) for a in range(na)]
        for cp in mine:
            cp.start()
        first = []
        for a in range(na):
            first.append(copy(a, 0, me, sibling, src=ins[a]))
            first += [copy(a, 1 + j, me, (*chip, c), src=ins[a]) for j, chip in enumerate(chips)]
        for cp in first:
            cp.start()
        passed = []
        for j, chip in enumerate(chips):
            for a in range(na):
                copy(a, 1 + j, (*chip, c), me).wait_recv()
                fwd = copy(a, 4 + j, (*chip, c), sibling)
                fwd.start()
                passed.append(fwd)
        for a in range(na):
            copy(a, 0, sibling, me).wait_recv()
            for j, chip in enumerate(chips):
                copy(a, 4 + j, (*chip, 1 - c), me).wait_recv()
        for cp in first + passed:
            cp.wait_send()
        for cp in mine:
            cp.wait()

    out_shape = [jax.ShapeDtypeStruct(s.shape[:-2] + (N_DEV * s.shape[-2], s.shape[-1]), s.dtype) for s in shards]
    return pl.pallas_call(
        body, name="weights_all_gather",
        in_specs=[ANY] * na, out_specs=[ANY] * na, out_shape=out_shape,
        scratch_shapes=[pltpu.SemaphoreType.DMA((na, 7)), pltpu.SemaphoreType.DMA((na, 7)),
                        pltpu.SemaphoreType.DMA((na,))],
    )(*shards)


def _split_copies(gather, srcs, lands, send_sems, recv_sems, ns):
    x, y, c = _me()
    me = (x, y, c)
    my_slot = 4 * x + 2 * y + c
    copies = []
    for mask in range(1, N_DEV):
        peer = (x ^ (mask >> 2), y ^ ((mask >> 1) & 1), c ^ (mask & 1))
        for a in range(len(srcs)):
            if gather:
                src, dst = srcs[a], _block_rows(lands[a], me, ns[a])
            else:
                src, dst = _block_rows(srcs[a], peer, ns[a]), lands[a].at[my_slot]
            sem = a * (N_DEV - 1) + mask - 1
            copies.append(pltpu.make_async_remote_copy(
                src_ref=src, dst_ref=dst, send_sem=send_sems.at[sem], recv_sem=recv_sems.at[sem],
                device_id=peer, device_id_type=MESH))
    return copies


HBM_SPEC = pl.BlockSpec(memory_space=pltpu.HBM)
SEM_SPEC = pl.BlockSpec(memory_space=pltpu.SEMAPHORE)


def _split_start(gather, srcs, name, dep=None):
    na = len(srcs)
    x, y, c = _me()
    mine = 4 * x + 2 * y + c
    if gather:
        ns = [s.shape[-2] for s in srcs]
        lands = [lax.dynamic_update_slice(
            lax.empty(s.shape[:-2] + (N_DEV * s.shape[-2], s.shape[-1]), s.dtype), s,
            (0,) * (s.ndim - 2) + (mine * s.shape[-2], 0)) for s in srcs]
    else:
        ns = [s.shape[-2] // N_DEV for s in srcs]
        lands = [lax.dynamic_update_slice(
            lax.empty((N_DEV, n, s.shape[-1]), s.dtype),
            lax.dynamic_slice(s, (mine * n, 0), (n, s.shape[-1]))[None], (mine, 0, 0)) for s, n in zip(srcs, ns)]
    n_in = 2 * na + (dep is not None)

    def body(*refs):
        send_sems, recv_sems = refs[n_in], refs[n_in + 1]
        for cp in _split_copies(gather, refs[:na], refs[na:2 * na], send_sems, recv_sems, ns):
            cp.start()
        refs[-1][...] = jnp.zeros_like(refs[-1])

    hbm = lambda a: pltpu.with_memory_space_constraint(a, pltpu.HBM)
    args = [hbm(a) for a in srcs] + [hbm(a) for a in lands] + ([dep] if dep is not None else [])
    out = pl.pallas_call(
        body, name=name,
        in_specs=[HBM_SPEC] * (2 * na) + ([ANY] if dep is not None else []),
        out_specs=[SEM_SPEC, SEM_SPEC] + [HBM_SPEC] * (2 * na) + [pl.BlockSpec(memory_space=pltpu.VMEM)],
        out_shape=[pltpu.SemaphoreType.DMA((na * (N_DEV - 1),)), pltpu.SemaphoreType.DMA((na * (N_DEV - 1),))]
        + [pltpu.HBM(a.shape, a.dtype) for a in srcs + lands] + [jax.ShapeDtypeStruct((8, LANES), F32)],
        input_output_aliases={i: 2 + i for i in range(2 * na)},
        compiler_params=pltpu.CompilerParams(has_side_effects=pltpu.SideEffectType.DATAFLOW_SIDE_EFFECTING),
    )(*args)
    return (gather, ns, out[0], out[1], list(out[2:2 + na]), list(out[2 + na:2 + 2 * na])), out[-1]


def _split_wait(handle, after, name):
    gather, ns, send, recv, srcs, lands = handle
    na = len(srcs)

    def body(*refs):
        send_sems, recv_sems = refs[2 * na], refs[2 * na + 1]
        for cp in _split_copies(gather, refs[:na], refs[na:2 * na], send_sems, recv_sems, ns):
            cp.wait_send()
            cp.wait_recv()

    out = pl.pallas_call(
        body, name=name,
        in_specs=[HBM_SPEC] * (2 * na) + [SEM_SPEC, SEM_SPEC, ANY],
        out_specs=[HBM_SPEC] * (2 * na),
        out_shape=[pltpu.HBM(a.shape, a.dtype) for a in srcs + lands],
        input_output_aliases={i: i for i in range(2 * na)},
        compiler_params=pltpu.CompilerParams(has_side_effects=pltpu.SideEffectType.DATAFLOW_SIDE_EFFECTING),
    )(*srcs, *lands, send, recv, after)
    return list(out[:na]), list(out[na:])


def _sum_blocks(parts):
    n, R, C = parts.shape

    def body(p_ref, o_ref):
        acc = p_ref[0]
        for k in range(1, n):
            acc = acc + p_ref[k]
        o_ref[...] = acc

    return pl.pallas_call(
        body, name="small_sum",
        in_specs=[pl.BlockSpec(memory_space=pltpu.VMEM)], out_specs=pl.BlockSpec(memory_space=pltpu.VMEM),
        out_shape=jax.ShapeDtypeStruct((R, C), F32),
        compiler_params=pltpu.CompilerParams(vmem_limit_bytes=VMEM_LIMIT),
    )(parts)


def _pack(arrays):
    flat = jnp.concatenate([a.reshape(-1) for a in arrays])
    pad = (-flat.shape[0]) % (8 * LANES)
    return jnp.pad(flat, (0, pad)).reshape(-1, LANES)


def _unpack(buf, shapes):
    flat = buf.reshape(-1)
    out, off = [], 0
    for s in shapes:
        n = 1
        for d in s:
            n *= d
        out.append(flat[off:off + n].reshape(s))
        off += n
    return out


def _ffn_index(layer, second):
    return (2 * layer + second) * 3


def kernel(x, g_ffn1, w_ffn1_gate, w_ffn1_up, w_ffn1_down, g_mix, w_in_ab, conv_w, conv_b, ln_a_g, ln_a_b, ln_v_g, ln_v_b, sp_w, sp_b, w_out_ab, w_qkv, w_o, g_ffn2, w_ffn2_gate, w_ffn2_up, w_ffn2_down, g_final, loss_target, m_g_ffn1, m_w_ffn1_gate, m_w_ffn1_up, m_w_ffn1_down, m_g_mix, m_w_in_ab, m_conv_w, m_conv_b, m_ln_a_g, m_ln_a_b, m_ln_v_g, m_ln_v_b, m_sp_w, m_sp_b, m_w_out_ab, m_w_qkv, m_w_o, m_g_ffn2, m_w_ffn2_gate, m_w_ffn2_up, m_w_ffn2_down, m_g_final, v_g_ffn1, v_w_ffn1_gate, v_w_ffn1_up, v_w_ffn1_down, v_g_mix, v_w_in_ab, v_conv_w, v_conv_b, v_ln_a_g, v_ln_a_b, v_ln_v_g, v_ln_v_b, v_sp_w, v_sp_b, v_w_out_ab, v_w_qkv, v_w_o, v_g_ffn2, v_w_ffn2_gate, v_w_ffn2_up, v_w_ffn2_down, v_g_final):
    n_seq, S, D = x.shape
    T = n_seq * S
    depth = g_ffn1.shape[0]
    assert depth == 2 and D == D_MODEL
    my_block = 4 * lax.axis_index("x") + 2 * lax.axis_index("y") + lax.axis_index("c")

    ffn_parts = []
    for l in range(depth):
        for gate, up, down in ((w_ffn1_gate, w_ffn1_up, w_ffn1_down), (w_ffn2_gate, w_ffn2_up, w_ffn2_down)):
            ffn_parts += [gate[l].T, up[l].T, down[l]]
    ffn_shard = lambda k: jnp.stack(ffn_parts[3 * k:3 * k + 3]).astype(BF16)
    conv_w_pad = jnp.zeros((HALO, conv_w.shape[2]), F32).at[:CONV_WIDTH].set(conv_w[0]).T
    w_ffn = [None] * (2 * depth)
    w_ffn[0], conv_w_t = _all_gather([ffn_shard(0), conv_w_pad])
    conv_w_full = conv_w_t.T[:CONV_WIDTH]
    shards_b = [w_out_ab[0].astype(BF16), ffn_shard(1)]
    shards_d = [w_qkv[0].T.astype(BF16), w_o[0].astype(BF16), ffn_shard(3)]
    gather_a, token = _split_start(True, [w_in_ab[0].T.astype(BF16)], "gather_a_start", dep=conv_w_t)
    gather_b, token = _split_start(True, shards_b, "gather_b_start", dep=token)
    gather_c, token = _split_start(True, [ffn_shard(2)], "gather_c_start", dep=token)
    gather_d, token = _split_start(True, shards_d, "gather_d_start", dep=token)

    def gathered(handle, after, name):
        return _split_wait(handle, after, name)[1]

    row = lambda a: a.reshape(1, -1)
    tril = jnp.tril(jnp.ones((CHUNK, CHUNK), dtype=bool))
    ws = jnp.where(tril[None], sp_w[0], 0.0).astype(BF16)
    ws_t = jnp.swapaxes(ws, 1, 2)
    bias2d = jnp.repeat(sp_b[0].T, DB, axis=1)
    conv_b2, lag, lab = row(conv_b[0]), row(ln_a_g[0]), row(ln_a_b[0])
    lvg, lvb = row(ln_v_g[0]), row(ln_v_b[0])

    x0 = x.reshape(T, D)
    target = loss_target.reshape(T, D)
    saved = []
    xc = x0
    for l in range(depth):
        xa, a1, b1, h1 = _ffn_fwd(xc, row(g_ffn1[l]), w_ffn[2 * l], 0, f"ffn1_fwd_{l}", dep=token)
        if l % 2 == 0:
            w_in_t, = gathered(gather_a, xa, "gather_a_wait")
            z, hm = _mm_nt(xa, w_in_t, row(g_mix[l]), F32, "mix_in_proj")
            cat, conv_out = _mix_fwd(z, conv_w_full, conv_b2, lag, lab, lvg, lvb, ws, bias2d, n_seq)
            w_out, w_ffn[1] = gathered(gather_b, cat, "gather_b_wait")
            xb = _mm_nn_res(cat, w_out, xa, "mix_out_proj")
            mixer = (z, hm, cat, conv_out)
        else:
            w_qkv_t, w_o_full, w_ffn[3] = gathered(gather_d, xa, "gather_d_wait")
            qkv, hm = _mm_nt(xa, w_qkv_t, row(g_mix[l]), BF16, "qkv_proj")
            o, att = _attn_fwd(qkv, n_seq)
            xb = _mm_nn_res(o, w_o_full, xa, "attn_out_proj")
            mixer = (qkv, hm, o, att)
        xn, a2, b2, h2 = _ffn_fwd(xb, row(g_ffn2[l]), w_ffn[2 * l + 1], 0, f"ffn2_fwd_{l}")
        saved.append((xc, a1, b1, h1, xa, mixer, xb, a2, b2, h2))
        xc = xn
        if l == 0:
            w_ffn[2], = gathered(gather_c, xc, "gather_c_wait")

    g, loss_part, dg_final = _loss_head(xc, row(g_final), target)

    dg_ffn1, dg_ffn2, dg_mix = [None] * depth, [None] * depth, [None] * depth
    exchanges = {}
    token = None

    def ffn_back(g, xin, gvec, a, b, h, k, tag, token):
        g, dg, da, db, s, gh = _ffn_bwd(g, xin, gvec, a, b, w_ffn[k], 0, f"ffn{tag}_bwd", dep=token)
        if k == 0:
            return g, dg, (da, db, s, gh, h)
        dws = [_mm_tn(da, h, f"dw_gate{tag}"), _mm_tn(db, h, f"dw_up{tag}"), _mm_tn(s, gh, f"dw_down{tag}")]
        exchanges[f"ffn{k}"], token = _split_start(False, dws, f"exchange_ffn{tag}_start")
        return g, dg, token

    for l in reversed(range(depth)):
        xin, a1, b1, h1, xa, mixer, xb, a2, b2, h2 = saved[l]
        g, dg_ffn2[l], token = ffn_back(g, xb, row(g_ffn2[l]), a2, b2, h2, 2 * l + 1, f"2_{l}", token)
        if l % 2 == 0:
            z, hm, cat, conv_out = mixer
            dcat = _mm_nt(g, w_out, None, F32, "mix_out_bwd", dep=token)
            d_w_out = _mm_tn(cat, g, "dw_out")
            (da1, dz_uv, d_lag, d_lab, d_lvg, d_lvb, d_ws, d_sb) = _mix_bwd_rows(
                dcat, z, conv_out, lag, lab, lvg, lvb, ws, ws_t, bias2d, n_seq)
            dz_a, d_cw, d_cb = _mix_bwd_conv(da1, z, conv_w_full, n_seq)
            d_w_in_t = jnp.concatenate([_mm_tn(dz_a, hm, "dw_in_a"), _mm_tn(dz_uv, hm, "dw_in_uv")])
            exchanges["mix"], token = _split_start(False, [d_w_out, d_w_in_t], "exchange_mix_start")
            g, dg_mix[l] = _mm_nn_rmsbwd([dz_a, dz_uv], w_in_t, xa, row(g_mix[l]), g, "mix_in_bwd", dep=token)
        else:
            qkv, hm, o, att = mixer
            do = _mm_nt(g, w_o_full, None, BF16, "attn_out_bwd", dep=token)
            d_w_o = _mm_tn(o, g, "dw_o")
            dq, dk, dv = _attn_bwd(qkv, do, att, n_seq)
            d_w_qkv_t = jnp.concatenate([_mm_tn(dq, hm, "dw_q"), _mm_tn(dk, hm, "dw_k"), _mm_tn(dv, hm, "dw_v")])
            exchanges["attn"], token = _split_start(False, [d_w_o, d_w_qkv_t], "exchange_attn_start")
            g, dg_mix[l] = _mm_nn_rmsbwd([dq, dk, dv], w_qkv_t, xa, row(g_mix[l]), g, "qkv_bwd", dep=token)
        g, dg_ffn1[l], token = ffn_back(g, xin, row(g_ffn1[l]), a1, b1, h1, 2 * l, f"1_{l}", token)
    grad_x = g.reshape(n_seq, S, D)

    small = [jnp.concatenate(dg_ffn1), jnp.concatenate(dg_mix), d_cw, d_cb, d_lag, d_lab, d_lvg, d_lvb,
             jnp.where(tril[None], d_ws, 0.0), d_sb.T, jnp.concatenate(dg_ffn2), dg_final, loss_part[:, :1]]
    small_shapes = [(depth, D), (depth, D), (CONV_WIDTH, CA), (1, CA), (1, CA), (1, CA), (1, GB, DB), (1, GB, DB),
                    (1, GB, CHUNK, CHUNK), (1, GB, CHUNK), (depth, D), (D,), ()]
    da, db, s, gh, h = token
    small_gather, token = _split_start(True, [_pack(small)], "small_gather_start")

    for which, lhs, rhs in ((2, s, gh), (1, db, h), (0, da, h)):
        dw = _mm_tn(lhs, rhs, f"dw_ffn0_{which}", dep=token, out_dtype=BF16)
        exchanges[f"ffn0_{which}"], token = _split_start(False, [dw], f"exchange_ffn0_{which}_start")

    small_all, = _split_wait(small_gather, token, "small_gather_wait")[1]
    red = _unpack(_sum_blocks(small_all.reshape(N_DEV, -1, LANES)), small_shapes)
    (gr_g_ffn1, gr_g_mix, gr_cw_full, gr_cb, gr_lag, gr_lab, gr_lvg, gr_lvb, gr_sp_w, gr_sp_b,
     gr_g_ffn2, gr_g_final, loss) = red
    n_cw = conv_w.shape[2]
    gr_cw = lax.dynamic_slice(gr_cw_full, (0, my_block * n_cw), (CONV_WIDTH, n_cw))[None]

    def landed(key, after):
        return _split_wait(exchanges[key], after, f"exchange_{key}_wait")[1]

    parts_ffn = [None] * (6 * depth)
    for k in range(1, 2 * depth):
        parts_ffn[3 * k:3 * k + 3] = landed(f"ffn{k}", token)
    parts_out, parts_in = landed("mix", token)
    parts_o, parts_qkv = landed("attn", token)

    grads = {
        "g_ffn1": gr_g_ffn1, "g_mix": gr_g_mix, "conv_w": gr_cw, "conv_b": gr_cb, "ln_a_g": gr_lag,
        "ln_a_b": gr_lab, "ln_v_g": gr_lvg, "ln_v_b": gr_lvb, "sp_w": gr_sp_w, "sp_b": gr_sp_b,
        "g_ffn2": gr_g_ffn2, "g_final": gr_g_final,
    }
    weights = dict(g_ffn1=g_ffn1, w_ffn1_gate=w_ffn1_gate, w_ffn1_up=w_ffn1_up, w_ffn1_down=w_ffn1_down, g_mix=g_mix,
                   w_in_ab=w_in_ab, conv_w=conv_w, conv_b=conv_b, ln_a_g=ln_a_g, ln_a_b=ln_a_b, ln_v_g=ln_v_g,
                   ln_v_b=ln_v_b, sp_w=sp_w, sp_b=sp_b, w_out_ab=w_out_ab, w_qkv=w_qkv, w_o=w_o, g_ffn2=g_ffn2,
                   w_ffn2_gate=w_ffn2_gate, w_ffn2_up=w_ffn2_up, w_ffn2_down=w_ffn2_down, g_final=g_final)
    m_in = dict(g_ffn1=m_g_ffn1, w_ffn1_gate=m_w_ffn1_gate, w_ffn1_up=m_w_ffn1_up, w_ffn1_down=m_w_ffn1_down,
                g_mix=m_g_mix, w_in_ab=m_w_in_ab, conv_w=m_conv_w, conv_b=m_conv_b, ln_a_g=m_ln_a_g, ln_a_b=m_ln_a_b,
                ln_v_g=m_ln_v_g, ln_v_b=m_ln_v_b, sp_w=m_sp_w, sp_b=m_sp_b, w_out_ab=m_w_out_ab, w_qkv=m_w_qkv,
                w_o=m_w_o, g_ffn2=m_g_ffn2, w_ffn2_gate=m_w_ffn2_gate, w_ffn2_up=m_w_ffn2_up,
                w_ffn2_down=m_w_ffn2_down, g_final=m_g_final)
    v_in = dict(g_ffn1=v_g_ffn1, w_ffn1_gate=v_w_ffn1_gate, w_ffn1_up=v_w_ffn1_up, w_ffn1_down=v_w_ffn1_down,
                g_mix=v_g_mix, w_in_ab=v_w_in_ab, conv_w=v_conv_w, conv_b=v_conv_b, ln_a_g=v_ln_a_g, ln_a_b=v_ln_a_b,
                ln_v_g=v_ln_v_g, ln_v_b=v_ln_v_b, sp_w=v_sp_w, sp_b=v_sp_b, w_out_ab=v_w_out_ab, w_qkv=v_w_qkv,
                w_o=v_w_o, g_ffn2=v_g_ffn2, w_ffn2_gate=v_w_ffn2_gate, w_ffn2_up=v_w_ffn2_up,
                w_ffn2_down=v_w_ffn2_down, g_final=v_g_final)
    names = list(weights)
    grads = {n: grads[n].reshape(weights[n].shape) for n in grads}

    delta, new_m, new_v = {}, {}, {}

    def adamw_big(n, parts):
        if weights[n].shape[-1] == D:
            view = back = lambda a: a
        else:
            view = back = lambda a: jnp.swapaxes(a, 1, 2)
        out = _adamw_parts(parts, view(weights[n]), view(m_in[n]), view(v_in[n]), f"adamw_{n}")
        grads[n], delta[n], new_m[n], new_v[n] = [back(a) for a in out]

    adamw_big("w_in_ab", [parts_in])
    adamw_big("w_out_ab", [parts_out])
    adamw_big("w_qkv", [parts_qkv])
    adamw_big("w_o", [parts_o])
    kinds = ("gate", "up", "down")
    for which, kind in enumerate(kinds):
        adamw_big(f"w_ffn2_{kind}", [parts_ffn[_ffn_index(l, 1) + which] for l in range(depth)])
    big = [n for n in names if n.startswith("w_")]
    after = jnp.concatenate([delta[n].reshape(-1)[:1] for n in big if n in delta]).reshape(1, -1)
    for which in (2, 1, 0):
        parts_ffn[which], = landed(f"ffn0_{which}", after)
    for which, kind in enumerate(kinds):
        adamw_big(f"w_ffn1_{kind}", [parts_ffn[_ffn_index(l, 0) + which] for l in range(depth)])
    little = [n for n in names if n not in big]
    shapes = [weights[n].shape for n in little]
    d, nm, nv = _adamw(_pack([weights[n] for n in little]), _pack([grads[n] for n in little]),
                       _pack([m_in[n] for n in little]), _pack([v_in[n] for n in little]), "adamw_small")
    for n, dd, mm, vv in zip(little, _unpack(d, shapes), _unpack(nm, shapes), _unpack(nv, shapes)):
        delta[n], new_m[n], new_v[n] = dd, mm, vv

    return (loss, grad_x, *[grads[n] for n in names], *[delta[n] for n in names],
            *[new_m[n] for n in names], *[new_v[n] for n in names])
```

```python
import jax
import jax.numpy as jnp
from jax import lax
from jax.experimental import pallas as pl
from jax.experimental.pallas import tpu as pltpu

F32 = jnp.float32
BF16 = jnp.bfloat16

D_MODEL = 1024
CA = 512
CB = 512
GB = 4
DB = 128
CHUNK = 128
CONV_WIDTH = 31
N_HEADS = 16
HEAD_DIM = 64
EPS = 1e-6
N_DEV = 8
LANES = 128
SUBLANES = 8
QB = 128
ATT_TQ = 1024
FFN_TN = 2816
FFN_TM = 256
HALO = 32
CONV_ROWS = 32
ATT_SCALE = HEAD_DIM ** -0.5

ADAM_LR = 0.001
ADAM_B1 = 0.9
ADAM_B2 = 0.999
ADAM_EPS = 1e-08
ADAM_WD = 0.01
ADAM_STEP = 10

NT = (((1,), (1,)), ((), ()))
NN = (((1,), (0,)), ((), ()))
TN = (((0,), (0,)), ((), ()))
MESH = pl.DeviceIdType.MESH
ANY = pl.BlockSpec(memory_space=pl.ANY)
VMEM_LIMIT = 60 * 1024 * 1024


def _dot(a, b, dims):
    return lax.dot_general(a, b, dims, preferred_element_type=F32)


def _cp(*sem):
    return pltpu.CompilerParams(dimension_semantics=sem, vmem_limit_bytes=VMEM_LIMIT)


def _pcall(body, *, in_specs, args, dep=None, **kw):
    if dep is not None:
        n_in = len(in_specs)
        inner = body

        def body(*refs):
            inner(*refs[:n_in], *refs[n_in + 1:])

        in_specs = list(in_specs) + [ANY]
        args = tuple(args) + (dep,)
    return pl.pallas_call(body, in_specs=list(in_specs), **kw)(*args)


def _tile(n, want):
    if n <= want:
        return n
    t = want - want % LANES
    while t > LANES and n % t:
        t -= LANES
    assert n % t == 0, (n, want)
    return t


def _sigmoid(x):
    return 0.5 * jnp.tanh(0.5 * x) + 0.5


def _rstd(x):
    return lax.rsqrt(jnp.mean(x * x, axis=-1, keepdims=True) + EPS)


def _rms_bwd(x, g, dh):
    r = _rstd(x)
    u = dh * g
    dx = r * (u - x * (r * r) * jnp.mean(u * x, axis=-1, keepdims=True))
    dg = jnp.sum(dh * x * r, axis=0, keepdims=True)
    return dx, dg


def _ln_fwd(x, g, b):
    mu = jnp.mean(x, axis=-1, keepdims=True)
    xc = x - mu
    r = lax.rsqrt(jnp.mean(xc * xc, axis=-1, keepdims=True) + EPS)
    xh = xc * r
    return xh * g + b, xh, r


def _ln_bwd(dy, xh, r, g):
    dxh = dy * g
    return r * (dxh - jnp.mean(dxh, axis=-1, keepdims=True)
                - xh * jnp.mean(dxh * xh, axis=-1, keepdims=True))


def _ffn_fwd(x, g, wall, base, name, dep=None):
    T, D = x.shape
    F = wall.shape[1]
    tm, tn = _tile(T, FFN_TM), _tile(F, FFN_TN)
    n_j = F // tn

    def body(x_ref, g_ref, wg_ref, wu_ref, wd_ref, xo_ref, a_ref, b_ref, h_ref, acc_ref):
        j = pl.program_id(1)

        @pl.when(j == 0)
        def _():
            xv = x_ref[...]
            h_ref[...] = (xv * _rstd(xv) * g_ref[...]).astype(BF16)
            acc_ref[...] = jnp.zeros_like(acc_ref)

        h = h_ref[...]
        a = _dot(h, wg_ref[...], NT)
        b = _dot(h, wu_ref[...], NT)
        a_ref[...] = a.astype(BF16)
        b_ref[...] = b.astype(BF16)
        s = (a * _sigmoid(a) * b).astype(BF16)
        acc_ref[...] += _dot(s, wd_ref[...], NN)

        @pl.when(j == n_j - 1)
        def _():
            xo_ref[...] = x_ref[...] + 0.5 * acc_ref[...]

    single = pl.Buffered(1) if n_j == 1 else None
    wspec = lambda k: pl.BlockSpec((None, tn, D), lambda i, j: (base + k, j, 0), pipeline_mode=single)
    return _pcall(
        body, name=name, grid=(T // tm, n_j), dep=dep, args=(x, g, wall, wall, wall),
        in_specs=[pl.BlockSpec((tm, D), lambda i, j: (i, 0)), pl.BlockSpec((1, D), lambda i, j: (0, 0)),
                  wspec(0), wspec(1), wspec(2)],
        out_specs=[pl.BlockSpec((tm, D), lambda i, j: (i, 0)), pl.BlockSpec((tm, tn), lambda i, j: (i, j)),
                   pl.BlockSpec((tm, tn), lambda i, j: (i, j)), pl.BlockSpec((tm, D), lambda i, j: (i, 0))],
        out_shape=[jax.ShapeDtypeStruct((T, D), F32), jax.ShapeDtypeStruct((T, F), BF16),
                   jax.ShapeDtypeStruct((T, F), BF16), jax.ShapeDtypeStruct((T, D), BF16)],
        scratch_shapes=[pltpu.VMEM((tm, D), F32)],
        compiler_params=_cp("parallel", "arbitrary"),
    )


def _proj_ffn_fwd(act, w, resid, g, wall, base, name):
    T, D = resid.shape
    K = act.shape[1]
    F = wall.shape[1]
    tm = _tile(T, FFN_TM)

    def body(act_ref, w_ref, r_ref, g_ref, wg_ref, wu_ref, wd_ref, x_ref, xo_ref, a_ref, b_ref, h_ref):
        xv = r_ref[...] + _dot(act_ref[...].astype(BF16), w_ref[...], NN)
        x_ref[...] = xv
        h = (xv * _rstd(xv) * g_ref[...]).astype(BF16)
        h_ref[...] = h
        a = _dot(h, wg_ref[...], NT)
        b = _dot(h, wu_ref[...], NT)
        a_ref[...] = a.astype(BF16)
        b_ref[...] = b.astype(BF16)
        s = (a * _sigmoid(a) * b).astype(BF16)
        xo_ref[...] = x_ref[...] + 0.5 * _dot(s, wd_ref[...], NN)

    one = pl.Buffered(1)
    wspec = lambda k: pl.BlockSpec((None, F, D), lambda i: (base + k, 0, 0), pipeline_mode=one)
    row = pl.BlockSpec((tm, D), lambda i: (i, 0))
    hid = pl.BlockSpec((tm, F), lambda i: (i, 0))
    return pl.pallas_call(
        body, name=name, grid=(T // tm,),
        in_specs=[pl.BlockSpec((tm, K), lambda i: (i, 0)), pl.BlockSpec((K, D), lambda i: (0, 0), pipeline_mode=one),
                  row, pl.BlockSpec((1, D), lambda i: (0, 0)), wspec(0), wspec(1), wspec(2)],
        out_specs=[row, row, hid, hid, row],
        out_shape=[jax.ShapeDtypeStruct((T, D), F32), jax.ShapeDtypeStruct((T, D), F32),
                   jax.ShapeDtypeStruct((T, F), BF16), jax.ShapeDtypeStruct((T, F), BF16),
                   jax.ShapeDtypeStruct((T, D), BF16)],
        compiler_params=_cp("parallel"),
    )(act, w, resid, g, wall, wall, wall)


def _ffn_bwd(go, x, g, a, b, wall, base, name, dep=None):
    T, D = x.shape
    F = wall.shape[1]
    tm, tn = _tile(T, FFN_TM), _tile(F, FFN_TN)
    n_j = F // tn

    def body(go_ref, x_ref, g_ref, a_ref, b_ref, wg_ref, wu_ref, wd_ref,
             gx_ref, dg_ref, da_ref, db_ref, s_ref, gh_ref, acc_ref):
        i, j = pl.program_id(0), pl.program_id(1)

        @pl.when(j == 0)
        def _():
            gh_ref[...] = (0.5 * go_ref[...]).astype(BF16)
            acc_ref[...] = jnp.zeros_like(acc_ref)

        @pl.when((i == 0) & (j == 0))
        def _():
            dg_ref[...] = jnp.zeros_like(dg_ref)

        ds = _dot(gh_ref[...], wd_ref[...], NT)
        av = a_ref[...].astype(F32)
        bv = b_ref[...].astype(F32)
        sig = _sigmoid(av)
        sl = av * sig
        dab = ((ds * bv) * (sig + sl * (1.0 - sig))).astype(BF16)
        dbb = (ds * sl).astype(BF16)
        s_ref[...] = (sl * bv).astype(BF16)
        da_ref[...] = dab
        db_ref[...] = dbb
        acc_ref[...] += _dot(dab, wg_ref[...], NN) + _dot(dbb, wu_ref[...], NN)

        @pl.when(j == n_j - 1)
        def _():
            dx, dg = _rms_bwd(x_ref[...], g_ref[...], acc_ref[...])
            gx_ref[...] = go_ref[...] + dx
            dg_ref[...] += dg

    single = pl.Buffered(1) if n_j == 1 else None
    wspec = lambda k: pl.BlockSpec((None, tn, D), lambda i, j: (base + k, j, 0), pipeline_mode=single)
    row = pl.BlockSpec((tm, D), lambda i, j: (i, 0))
    hid = pl.BlockSpec((tm, tn), lambda i, j: (i, j))
    vec = pl.BlockSpec((1, D), lambda i, j: (0, 0))
    return _pcall(
        body, name=name, grid=(T // tm, n_j), dep=dep, args=(go, x, g, a, b, wall, wall, wall),
        in_specs=[row, row, vec, hid, hid, wspec(0), wspec(1), wspec(2)],
        out_specs=[row, vec, hid, hid, hid, row],
        out_shape=[jax.ShapeDtypeStruct((T, D), F32), jax.ShapeDtypeStruct((1, D), F32),
                   jax.ShapeDtypeStruct((T, F), BF16), jax.ShapeDtypeStruct((T, F), BF16),
                   jax.ShapeDtypeStruct((T, F), BF16), jax.ShapeDtypeStruct((T, D), BF16)],
        scratch_shapes=[pltpu.VMEM((tm, D), F32)],
        compiler_params=_cp("arbitrary", "arbitrary"),
    )


def _mm_tn(a, b, name, dep=None, out_dtype=F32):
    T, M = a.shape
    N = b.shape[1]
    tmm, tk = _tile(M, 1536), _tile(T, 2048)
    n_k = T // tk
    narrow = out_dtype != F32

    def body(a_ref, b_ref, o_ref, *scratch):
        acc_ref = scratch[0] if narrow else o_ref

        @pl.when(pl.program_id(1) == 0)
        def _():
            acc_ref[...] = jnp.zeros_like(acc_ref)

        acc_ref[...] += _dot(a_ref[...].astype(BF16), b_ref[...].astype(BF16), TN)
        if narrow:
            @pl.when(pl.program_id(1) == n_k - 1)
            def _():
                o_ref[...] = acc_ref[...].astype(out_dtype)

    return _pcall(
        body, name=name, grid=(M // tmm, n_k), dep=dep, args=(a, b),
        in_specs=[pl.BlockSpec((tk, tmm), lambda m, k: (k, m)), pl.BlockSpec((tk, N), lambda m, k: (k, 0))],
        out_specs=pl.BlockSpec((tmm, N), lambda m, k: (m, 0)),
        out_shape=jax.ShapeDtypeStruct((M, N), out_dtype),
        scratch_shapes=[pltpu.VMEM((tmm, N), F32)] if narrow else [],
        compiler_params=_cp("parallel", "arbitrary"),
    )


def _mm_nt(x, wt, g, out_dtype, name, dep=None):
    T, K = x.shape
    N = wt.shape[0]
    tm, tn = _tile(T, 512), N
    norm = g is not None

    def body(*refs):
        if norm:
            x_ref, g_ref, w_ref, o_ref, h_ref = refs
        else:
            x_ref, w_ref, o_ref, h_ref = refs

        @pl.when(pl.program_id(1) == 0)
        def _():
            xv = x_ref[...].astype(F32)
            if norm:
                xv = xv * _rstd(xv) * g_ref[...]
            h_ref[...] = xv.astype(BF16)

        o_ref[...] = _dot(h_ref[...], w_ref[...], NT).astype(out_dtype)

    row = pl.BlockSpec((tm, K), lambda i, j: (i, 0))
    wsp = pl.BlockSpec((tn, K), lambda i, j: (j, 0))
    osp = pl.BlockSpec((tm, tn), lambda i, j: (i, j))
    if norm:
        return pl.pallas_call(
            body, name=name, grid=(T // tm, N // tn),
            in_specs=[row, pl.BlockSpec((1, K), lambda i, j: (0, 0)), wsp],
            out_specs=[osp, row],
            out_shape=[jax.ShapeDtypeStruct((T, N), out_dtype), jax.ShapeDtypeStruct((T, K), BF16)],
            compiler_params=_cp("parallel", "arbitrary"),
        )(x, g, wt)
    return _pcall(
        body, name=name, grid=(T // tm, N // tn), dep=dep, args=(x, wt),
        in_specs=[row, wsp], out_specs=osp,
        out_shape=jax.ShapeDtypeStruct((T, N), out_dtype),
        scratch_shapes=[pltpu.VMEM((tm, K), BF16)],
        compiler_params=_cp("parallel", "arbitrary"),
    )


def _mm_nn_res(act, w, resid, name):
    T, K = act.shape
    D = w.shape[1]
    tm = _tile(T, 512)

    def body(a_ref, w_ref, r_ref, o_ref):
        o_ref[...] = r_ref[...] + _dot(a_ref[...].astype(BF16), w_ref[...], NN)

    return pl.pallas_call(
        body, name=name, grid=(T // tm,),
        in_specs=[pl.BlockSpec((tm, K), lambda i: (i, 0)), pl.BlockSpec((K, D), lambda i: (0, 0)),
                  pl.BlockSpec((tm, D), lambda i: (i, 0))],
        out_specs=pl.BlockSpec((tm, D), lambda i: (i, 0)),
        out_shape=jax.ShapeDtypeStruct((T, D), F32),
        compiler_params=_cp("parallel"),
    )(act, w, resid)


def _mm_nn_rmsbwd(acts, w, x, g, gprev, name, dep=None):
    T = acts[0].shape[0]
    ks = [a.shape[1] for a in acts]
    K, D = w.shape
    assert sum(ks) == K
    tm = _tile(T, 512)
    na = len(acts)

    def body(*refs):
        a_refs = refs[:na]
        w_ref, x_ref, g_ref, gp_ref, o_ref, dg_ref = refs[na:]

        @pl.when(pl.program_id(0) == 0)
        def _():
            dg_ref[...] = jnp.zeros_like(dg_ref)

        dh, off = None, 0
        for a_ref, k in zip(a_refs, ks):
            part = _dot(a_ref[...].astype(BF16), w_ref[off:off + k, :], NN)
            dh = part if dh is None else dh + part
            off += k
        dx, dg = _rms_bwd(x_ref[...], g_ref[...], dh)
        o_ref[...] = gp_ref[...] + dx
        dg_ref[...] += dg

    row = pl.BlockSpec((tm, D), lambda i: (i, 0))
    vec = pl.BlockSpec((1, D), lambda i: (0, 0))
    return _pcall(
        body, name=name, grid=(T // tm,), dep=dep, args=(*acts, w, x, g, gprev),
        in_specs=[pl.BlockSpec((tm, k), lambda i: (i, 0)) for k in ks]
        + [pl.BlockSpec((K, D), lambda i: (0, 0)), row, vec, row],
        out_specs=[row, vec],
        out_shape=[jax.ShapeDtypeStruct((T, D), F32), jax.ShapeDtypeStruct((1, D), F32)],
        compiler_params=_cp("arbitrary"),
    )


def _loss_head(x, g, target):
    T, D = x.shape
    tm = _tile(T, 512)

    def body(x_ref, g_ref, t_ref, dx_ref, loss_ref, dg_ref):
        @pl.when(pl.program_id(0) == 0)
        def _():
            loss_ref[...] = jnp.zeros_like(loss_ref)
            dg_ref[...] = jnp.zeros_like(dg_ref)

        xv = x_ref[...]
        gv = g_ref[...]
        e = xv * _rstd(xv) * gv - t_ref[...]
        per_tok = jnp.sum(e * e, axis=-1, keepdims=True) * (1.0 / D)
        loss_ref[...] += 0.5 * jnp.sum(per_tok, axis=0, keepdims=True)
        dx, dg = _rms_bwd(xv, gv, e * (1.0 / D))
        dx_ref[...] = dx
        dg_ref[...] += dg

    row = pl.BlockSpec((tm, D), lambda i: (i, 0))
    vec = pl.BlockSpec((1, D), lambda i: (0, 0))
    return pl.pallas_call(
        body, name="loss_head", grid=(T // tm,),
        in_specs=[row, vec, row],
        out_specs=[row, pl.BlockSpec((1, LANES), lambda i: (0, 0)), vec],
        out_shape=[jax.ShapeDtypeStruct((T, D), F32), jax.ShapeDtypeStruct((1, LANES), F32),
                   jax.ShapeDtypeStruct((1, D), F32)],
        compiler_params=_cp("arbitrary"),
    )(x, g, target)


def _log_gates(z):
    neg_abs = lax.bitcast_convert_type(lax.bitcast_convert_type(z, jnp.uint32) | jnp.uint32(0x80000000), F32)
    ls = jnp.minimum(z, 0.0) - jnp.log(1.0 + jnp.exp(neg_abs))
    return ls, ls - z


def _cumsum_mm(v, u):
    return _dot(v.astype(BF16), u, NN)


def _half_rowsum(v):
    n = v.shape[0]
    s0 = jnp.sum(v[:, :QB], axis=1, keepdims=True)
    s1 = jnp.sum(v[:, QB:], axis=1, keepdims=True)
    return jnp.concatenate([jnp.broadcast_to(s0, (n, QB)), jnp.broadcast_to(s1, (n, QB))], axis=1)


def _stack_heads(src_ref, dst_ref, n_blk):
    m0 = lax.broadcasted_iota(jnp.int32, (1, LANES), 1) < HEAD_DIM

    def fill(c, carry):
        blk = src_ref[pl.ds(pl.multiple_of(c * QB, QB), QB), :]
        zero = jnp.zeros_like(blk)
        dst_ref[c, 0:QB, :] = jnp.where(m0, blk, zero)
        dst_ref[c, QB:2 * QB, :] = jnp.where(m0, zero, blk)
        return carry

    lax.fori_loop(0, n_blk, fill, 0)


def _diag_mask(tq, j):
    n = tq - j * QB
    row = lax.broadcasted_iota(jnp.int32, (n, 2 * QB), 0)
    col = lax.broadcasted_iota(jnp.int32, (n, 2 * QB), 1)
    return (col & (QB - 1)) < row


def _tri_blockdiag(upper):
    r = lax.broadcasted_iota(jnp.int32, (2 * QB, 2 * QB), 0)
    c = lax.broadcasted_iota(jnp.int32, (2 * QB, 2 * QB), 1)
    same = (r // QB) == (c // QB)
    return (same & ((r > c) if upper else (r < c))).astype(BF16)


def _attn_tiles(T, n_seq):
    S = T // n_seq
    tq = ATT_TQ if S % ATT_TQ == 0 else 2 * QB
    assert S % tq == 0
    return S, tq, tq // QB, S // tq, S // QB


def _attn_fwd(qkv, n_seq):
    T = qkv.shape[0]
    S, tq, r, n_q, n_k = _attn_tiles(T, n_seq)
    n_p = D_MODEL // LANES
    n_steps = n_seq * n_p * n_q
    u_suffix = _tri_blockdiag(True)

    def body(q_ref, k_ref, v_ref, u_ref, o_ref, a_hbm, kk_ref, vv_ref, lr_s, acc_s, a_stage, sems):
        qi = pl.program_id(2)
        group = (pl.program_id(0) * n_p + pl.program_id(1)) * n_q + qi

        @pl.when(qi == 0)
        def _():
            _stack_heads(k_ref, kk_ref, n_k)
            _stack_heads(v_ref, vv_ref, n_k)

        u = u_ref[...]
        lr_s[...] = jnp.zeros_like(lr_s)
        acc_s[...] = jnp.zeros_like(acc_s)
        base = ((group // n_q) * (n_q * (n_q + 1) // 2) + (qi * (qi + 1)) // 2) % 2

        def saves(first_kj, half):
            return [pltpu.make_async_copy(a_stage.at[half, j], a_hbm.at[group, first_kj - j], sems.at[half])
                    for j in range(r)]

        @pl.when(group >= 2)
        def _():
            for cp in saves(0, base):
                cp.wait()

        a_stage[base] = jnp.zeros_like(a_stage[0])

        def step(kj, half, j, rows, q, mask, lr, acc):
            ls, lk = _log_gates(_dot(q, kk_ref[kj], NT))
            if mask is not None:
                lk = jnp.where(mask, lk, 0.0)
            a = jnp.exp(ls + _cumsum_mm(lk, u) + lr)
            if mask is not None:
                a = jnp.where(mask, a, 0.0)
            a = a.astype(BF16)
            a_stage[half, j, rows, :] = a
            return lr + _half_rowsum(lk), acc + _dot(a, vv_ref[kj], NN)

        last = (qi + 1) * r - 1
        for n in range(r):
            rows = slice((r - 1 - n) * QB, tq)
            lr, acc = step(last - n, base, n, rows, q_ref[rows, :] * ATT_SCALE, _diag_mask(tq, r - 1 - n),
                           lr_s[rows, :], acc_s[rows, :])
            lr_s[rows, :] = lr
            acc_s[rows, :] = acc
        for cp in saves(last, base):
            cp.start()

        q = q_ref[...] * ATT_SCALE

        def off(it, carry):
            half = (base + it + 1) % 2
            first = (qi - it) * r - 1
            for cp in saves(first, half):
                cp.wait()
            lr, acc = lr_s[...], acc_s[...]
            for j in range(r):
                lr, acc = step(first - j, half, j, slice(0, tq), q, None, lr, acc)
            lr_s[...] = lr
            acc_s[...] = acc
            for cp in saves(first, half):
                cp.start()
            return carry

        lax.fori_loop(0, qi, off, 0)

        @pl.when(group == n_steps - 1)
        def _():
            for cp in saves(0, (base + qi) % 2):
                cp.wait()
            if n_steps * n_q > 1:
                for cp in saves(0, (base + qi + 1) % 2):
                    cp.wait()

        o_ref[...] = acc_s[...].astype(BF16)

    return pl.pallas_call(
        body, name="attn_fwd", grid=(n_seq, n_p, n_q),
        in_specs=[pl.BlockSpec((tq, LANES), lambda b, p, qi: (b * n_q + qi, p)),
                  pl.BlockSpec((S, LANES), lambda b, p, qi: (b, n_p + p)),
                  pl.BlockSpec((S, LANES), lambda b, p, qi: (b, 2 * n_p + p)),
                  pl.BlockSpec((2 * QB, 2 * QB), lambda b, p, qi: (0, 0))],
        out_specs=[pl.BlockSpec((tq, LANES), lambda b, p, qi: (b * n_q + qi, p)), ANY],
        out_shape=[jax.ShapeDtypeStruct((T, D_MODEL), BF16),
                   jax.ShapeDtypeStruct((n_seq * n_p * n_q, n_k, tq, 2 * QB), BF16)],
        scratch_shapes=[pltpu.VMEM((n_k, 2 * QB, LANES), BF16), pltpu.VMEM((n_k, 2 * QB, LANES), BF16),
                        pltpu.VMEM((tq, 2 * QB), F32), pltpu.VMEM((tq, LANES), F32),
                        pltpu.VMEM((2, r, tq, 2 * QB), BF16), pltpu.SemaphoreType.DMA((2,))],
        compiler_params=_cp("arbitrary", "arbitrary", "arbitrary"),
    )(qkv, qkv, qkv, u_suffix)


def _attn_bwd(qkv, do, a_saved, n_seq):
    T = qkv.shape[0]
    S, tq, r, n_q, n_k = _attn_tiles(T, n_seq)
    n_p = D_MODEL // LANES
    n_steps = n_seq * n_p * n_q
    u_prefix = _tri_blockdiag(False)

    def body(q_ref, k_ref, v_ref, do_ref, u_ref, a_hbm, dq_ref, dk_out, dv_out,
             kk_ref, vv_ref, cg_s, dq_s, dk_ref, dv_ref, a_stage, sems):
        qi = pl.program_id(2)
        group = (pl.program_id(0) * n_p + pl.program_id(1)) * n_q + qi

        base = ((group // n_q) * (n_q * (n_q + 1) // 2) + (qi * (qi + 1)) // 2) % 2

        def fetches(grp, g, half):
            return [pltpu.make_async_copy(a_hbm.at[grp, g * r + j], a_stage.at[half, j], sems.at[half])
                    for j in range(r)]

        @pl.when(group == 0)
        def _():
            for cp in fetches(group, 0, 0):
                cp.start()

        @pl.when(qi == 0)
        def _():
            _stack_heads(k_ref, kk_ref, n_k)
            _stack_heads(v_ref, vv_ref, n_k)
            dk_ref[...] = jnp.zeros_like(dk_ref)
            dv_ref[...] = jnp.zeros_like(dv_ref)

        u = u_ref[...]
        cg_s[...] = jnp.zeros_like(cg_s)
        dq_s[...] = jnp.zeros_like(dq_s)

        def step(kj, half, j, rows, q, dov, mask, cg, dq):
            kk = kk_ref[kj]
            beta = _sigmoid(_dot(q, kk, NT))
            a = a_stage[half, j, rows, :]
            g = a.astype(F32) * _dot(dov, vv_ref[kj], NT)
            dz = g - (g + _dot(g.astype(BF16), u, NN) + cg) * beta
            if mask is not None:
                dz = jnp.where(mask, dz, 0.0)
            dz = dz.astype(BF16)
            keys = pl.ds(pl.multiple_of(kj * QB, QB), QB)
            dvt = _dot(dov_t[:, rows], a, NN)
            dv_ref[keys, :] += jnp.where(t0, dvt[:, :QB], dvt[:, QB:]).T
            dkt = _dot(q_t[:, rows], dz, NN)
            dk_ref[keys, :] += jnp.where(t0, dkt[:, :QB], dkt[:, QB:]).T
            return cg + _half_rowsum(g), dq + _dot(dz, kk, NN)

        q = q_ref[...] * ATT_SCALE
        dov = do_ref[...]
        q_t = q.astype(F32).T.astype(BF16)
        dov_t = dov.astype(F32).T.astype(BF16)
        t0 = lax.broadcasted_iota(jnp.int32, (LANES, 1), 0) < HEAD_DIM

        def off(it, carry):
            half = (base + it) % 2
            for cp in fetches(group, it, half):
                cp.wait()
            for cp in fetches(group, it + 1, 1 - half):
                cp.start()
            cg, dq = cg_s[...], dq_s[...]
            for j in range(r):
                cg, dq = step(it * r + j, half, j, slice(0, tq), q, dov, None, cg, dq)
            cg_s[...] = cg
            dq_s[...] = dq
            return carry

        lax.fori_loop(0, qi, off, 0)

        half = (base + qi) % 2
        for cp in fetches(group, qi, half):
            cp.wait()

        @pl.when(group < n_steps - 1)
        def _():
            for cp in fetches(group + 1, 0, 1 - half):
                cp.start()

        for j in range(r):
            rows = slice(j * QB, tq)
            cg, dq = step(qi * r + j, half, j, rows, q_ref[rows, :] * ATT_SCALE, do_ref[rows, :],
                          _diag_mask(tq, j), cg_s[rows, :], dq_s[rows, :])
            cg_s[rows, :] = cg
            dq_s[rows, :] = dq
        dq_ref[...] = (dq_s[...] * ATT_SCALE).astype(BF16)

        @pl.when(qi == n_q - 1)
        def _():
            dk_out[...] = dk_ref[...].astype(BF16)
            dv_out[...] = dv_ref[...].astype(BF16)

    qspec = pl.BlockSpec((tq, LANES), lambda b, p, qi: (b * n_q + qi, p))
    seq = lambda off: pl.BlockSpec((S, LANES), lambda b, p, qi: (b, off + p))
    return pl.pallas_call(
        body, name="attn_bwd", grid=(n_seq, n_p, n_q),
        in_specs=[qspec, seq(n_p), seq(2 * n_p), qspec,
                  pl.BlockSpec((2 * QB, 2 * QB), lambda b, p, qi: (0, 0)), ANY],
        out_specs=[qspec, seq(0), seq(0)],
        out_shape=[jax.ShapeDtypeStruct((T, D_MODEL), BF16)] * 3,
        scratch_shapes=[pltpu.VMEM((n_k, 2 * QB, LANES), BF16), pltpu.VMEM((n_k, 2 * QB, LANES), BF16),
                        pltpu.VMEM((tq, 2 * QB), F32), pltpu.VMEM((tq, LANES), F32),
                        pltpu.VMEM((S, LANES), F32), pltpu.VMEM((S, LANES), F32),
                        pltpu.VMEM((2, r, tq, 2 * QB), BF16), pltpu.SemaphoreType.DMA((2,))],
        compiler_params=_cp("arbitrary", "arbitrary", "arbitrary"),
    )(qkv, qkv, qkv, do, u_prefix, a_saved)


def _shifted_copies(sh_ref):
    rows = sh_ref.shape[1] - SUBLANES
    for s in range(1, SUBLANES):
        sh_ref[s, 0:rows, :] = sh_ref[0, s:s + rows, :]


def _shifted(sh_ref, start, n):
    s = start % SUBLANES
    return sh_ref[s, start - s:start - s + n, :]


def _glu_with_halo(av_ref, ag_ref, avh_ref, agh_ref, a0_s, first, ts):
    hal = avh_ref[...] * _sigmoid(agh_ref[...])
    a0_s[0, 0:HALO, :] = jnp.where(first, 0.0, hal)
    a0_s[0, HALO:HALO + ts, :] = av_ref[...] * _sigmoid(ag_ref[...])
    _shifted_copies(a0_s)


def _mix_specs(ts, n_r, with_left):
    blk = lambda c: pl.BlockSpec((ts, CA), lambda b, r: (b * n_r + r, c))
    per = ts // HALO
    left = lambda c: pl.BlockSpec((HALO, CA), lambda b, r: (jnp.maximum((b * n_r + r) * per - 1, 0), c))
    return blk, (left if with_left else None)


def _mix_fwd(z, conv_w, conv_b, ln_a_g, ln_a_b, ln_v_g, ln_v_b, ws, bias2d, n_seq):
    T = z.shape[0]
    S = T // n_seq
    ts = _tile(S, 512)
    n_r = S // ts
    shift = HALO - (CONV_WIDTH - 1)

    def body(av_ref, ag_ref, avh_ref, agh_ref, u_ref, v_ref, cw_ref, cb_ref, lag_ref, lab_ref,
             lvg_ref, lvb_ref, ws_ref, bias_ref, cat_ref, a1_ref, a0_s):
        _glu_with_halo(av_ref, ag_ref, avh_ref, agh_ref, a0_s, pl.program_id(1) == 0, ts)
        for rb in range(ts // CONV_ROWS):
            base = rb * CONV_ROWS
            acc = jnp.broadcast_to(cb_ref[...], (CONV_ROWS, CA))
            for k in range(CONV_WIDTH):
                acc = acc + cw_ref[k:k + 1, :] * _shifted(a0_s, base + shift + k, CONV_ROWS)
            a1_ref[base:base + CONV_ROWS, :] = acc
        y, _, _ = _ln_fwd(a1_ref[...], lag_ref[...], lab_ref[...])
        cat_ref[:, 0:CA] = (y * _sigmoid(y)).astype(BF16)
        for gi in range(GB):
            sl = slice(gi * DB, (gi + 1) * DB)
            v1, _, _ = _ln_fwd(v_ref[:, sl], lvg_ref[:, sl], lvb_ref[:, sl])
            v1 = v1.astype(BF16)
            for c in range(ts // CHUNK):
                rs = slice(c * CHUNK, (c + 1) * CHUNK)
                v2 = _dot(ws_ref[gi], v1[rs], NN) + bias_ref[:, sl]
                cat_ref[rs, CA + gi * DB:CA + (gi + 1) * DB] = (u_ref[rs, sl] * v2).astype(BF16)

    blk, left = _mix_specs(ts, n_r, True)
    vec = pl.BlockSpec((1, CA), lambda b, r: (0, 0))
    return pl.pallas_call(
        body, name="mix_fwd", grid=(n_seq, n_r),
        in_specs=[blk(0), blk(1), left(0), left(1), blk(2), blk(3),
                  pl.BlockSpec((CONV_WIDTH, CA), lambda b, r: (0, 0)), vec, vec, vec, vec, vec,
                  pl.BlockSpec((GB, CHUNK, CHUNK), lambda b, r: (0, 0, 0)),
                  pl.BlockSpec((CHUNK, CB), lambda b, r: (0, 0))],
        out_specs=[pl.BlockSpec((ts, CA + CB), lambda b, r: (b * n_r + r, 0)), blk(0)],
        out_shape=[jax.ShapeDtypeStruct((T, CA + CB), BF16), jax.ShapeDtypeStruct((T, CA), F32)],
        scratch_shapes=[pltpu.VMEM((SUBLANES, HALO + ts, CA), F32)],
        compiler_params=_cp("parallel", "parallel"),
    )(z, z, z, z, z, z, conv_w, conv_b, ln_a_g, ln_a_b, ln_v_g, ln_v_b, ws, bias2d)


def _mix_bwd_rows(dcat, z, a1, ln_a_g, ln_a_b, ln_v_g, ln_v_b, ws, ws_t, bias2d, n_seq):
    T = z.shape[0]
    S = T // n_seq
    ts = _tile(S, 512)
    n_r = S // ts

    def body(dc_ref, u_ref, v_ref, a1_ref, lag_ref, lab_ref, lvg_ref, lvb_ref, ws_ref, wst_ref, bias_ref,
             da1_ref, dz_ref, dlag_ref, dlab_ref, dlvg_ref, dlvb_ref, dws_ref, dsb_ref, dv1_s, dbias_s):
        first = (pl.program_id(0) == 0) & (pl.program_id(1) == 0)
        last = (pl.program_id(0) == n_seq - 1) & (pl.program_id(1) == n_r - 1)

        @pl.when(first)
        def _():
            for ref in (dlag_ref, dlab_ref, dlvg_ref, dlvb_ref, dws_ref, dbias_s):
                ref[...] = jnp.zeros_like(ref)

        lag = lag_ref[...]
        y, xh, r = _ln_fwd(a1_ref[...], lag, lab_ref[...])
        sig = _sigmoid(y)
        dy = dc_ref[:, 0:CA] * (sig * (1.0 + y * (1.0 - sig)))
        dlag_ref[...] += jnp.sum(dy * xh, axis=0, keepdims=True)
        dlab_ref[...] += jnp.sum(dy, axis=0, keepdims=True)
        da1_ref[...] = _ln_bwd(dy, xh, r, lag)

        tril = (lax.broadcasted_iota(jnp.int32, (CHUNK, CHUNK), 0)
                >= lax.broadcasted_iota(jnp.int32, (CHUNK, CHUNK), 1))
        for gi in range(GB):
            sl = slice(gi * DB, (gi + 1) * DB)
            lvg = lvg_ref[:, sl]
            v1, vh, vr = _ln_fwd(v_ref[:, sl], lvg, lvb_ref[:, sl])
            v1 = v1.astype(BF16)
            for c in range(ts // CHUNK):
                rs = slice(c * CHUNK, (c + 1) * CHUNK)
                v2 = _dot(ws_ref[gi], v1[rs], NN) + bias_ref[:, sl]
                dbo = dc_ref[rs, CA + gi * DB:CA + (gi + 1) * DB]
                dz_ref[rs, sl] = (dbo * v2).astype(BF16)
                dv2 = dbo * u_ref[rs, sl]
                dbias_s[:, sl] += dv2
                dv2b = dv2.astype(BF16)
                dws_ref[gi] += jnp.where(tril, _dot(dv2b, v1[rs], NT), 0.0)
                dv1_s[rs, :] = _dot(wst_ref[gi], dv2b, NN)
            dv1 = dv1_s[...]
            dlvg_ref[:, sl] += jnp.sum(dv1 * vh, axis=0, keepdims=True)
            dlvb_ref[:, sl] += jnp.sum(dv1, axis=0, keepdims=True)
            dz_ref[:, CB + gi * DB:CB + (gi + 1) * DB] = _ln_bwd(dv1, vh, vr, lvg).astype(BF16)

        @pl.when(last)
        def _():
            col = lax.broadcasted_iota(jnp.int32, (CHUNK, GB), 1)
            out = jnp.zeros((CHUNK, GB), F32)
            for gi in range(GB):
                s = jnp.sum(dbias_s[:, gi * DB:(gi + 1) * DB], axis=1, keepdims=True)
                out = out + jnp.where(col == gi, s, 0.0)
            dsb_ref[...] = out

    blk, _ = _mix_specs(ts, n_r, False)
    vec = pl.BlockSpec((1, CA), lambda b, r: (0, 0))
    mat = pl.BlockSpec((GB, CHUNK, CHUNK), lambda b, r: (0, 0, 0))
    wide = pl.BlockSpec((ts, CA + CB), lambda b, r: (b * n_r + r, 0))
    return pl.pallas_call(
        body, name="mix_bwd_rows", grid=(n_seq, n_r),
        in_specs=[wide, blk(2), blk(3), blk(0), vec, vec, vec, vec, mat, mat,
                  pl.BlockSpec((CHUNK, CB), lambda b, r: (0, 0))],
        out_specs=[blk(0), wide, vec, vec, vec, vec, mat, pl.BlockSpec((CHUNK, GB), lambda b, r: (0, 0))],
        out_shape=[jax.ShapeDtypeStruct((T, CA), F32), jax.ShapeDtypeStruct((T, 2 * CB), BF16)]
        + [jax.ShapeDtypeStruct((1, CA), F32)] * 4
        + [jax.ShapeDtypeStruct((GB, CHUNK, CHUNK), F32), jax.ShapeDtypeStruct((CHUNK, GB), F32)],
        scratch_shapes=[pltpu.VMEM((ts, DB), F32), pltpu.VMEM((CHUNK, CB), F32)],
        compiler_params=_cp("arbitrary", "arbitrary"),
    )(dcat, z, z, a1, ln_a_g, ln_a_b, ln_v_g, ln_v_b, ws, ws_t, bias2d)


def _mix_bwd_conv(da1, z, conv_w, n_seq):
    T = z.shape[0]
    S = T // n_seq
    ts = _tile(S, 512)
    n_r = S // ts
    per = ts // HALO
    shift = HALO - (CONV_WIDTH - 1)
    fold = CONV_ROWS // 8

    def body(d_ref, dh_ref, av_ref, ag_ref, avh_ref, agh_ref, cw_ref,
             dz_ref, dcw_ref, dcb_ref, a0_s, d1_s, da0_s, dw8_s):
        first = (pl.program_id(0) == 0) & (pl.program_id(1) == 0)
        last = (pl.program_id(0) == n_seq - 1) & (pl.program_id(1) == n_r - 1)

        @pl.when(first)
        def _():
            dw8_s[...] = jnp.zeros_like(dw8_s)
            dcb_ref[...] = jnp.zeros_like(dcb_ref)

        _glu_with_halo(av_ref, ag_ref, avh_ref, agh_ref, a0_s, pl.program_id(1) == 0, ts)
        d1_s[0, 0:ts, :] = d_ref[...]
        d1_s[0, ts:ts + HALO, :] = jnp.where(pl.program_id(1) == n_r - 1, 0.0, dh_ref[...])
        _shifted_copies(d1_s)
        dcb_ref[...] += jnp.sum(d_ref[...], axis=0, keepdims=True)
        for rb in range(ts // CONV_ROWS):
            base = rb * CONV_ROWS
            dcur = d1_s[0, base:base + CONV_ROWS, :]
            acc = jnp.zeros((CONV_ROWS, CA), F32)
            for k in range(CONV_WIDTH):
                back = CONV_WIDTH - 1 - k
                acc = acc + cw_ref[k:k + 1, :] * _shifted(d1_s, base + back, CONV_ROWS)
                prod = dcur * _shifted(a0_s, base + shift + k, CONV_ROWS)
                part = prod[0:8]
                for f in range(1, fold):
                    part = part + prod[8 * f:8 * f + 8]
                dw8_s[k] += part
            da0_s[base:base + CONV_ROWS, :] = acc
        da0 = da0_s[...]
        sig = _sigmoid(ag_ref[...])
        dz_ref[:, 0:CA] = (da0 * sig).astype(BF16)
        dz_ref[:, CA:2 * CA] = (da0 * av_ref[...] * sig * (1.0 - sig)).astype(BF16)

        @pl.when(last)
        def _():
            for k in range(CONV_WIDTH):
                dcw_ref[k:k + 1, :] = jnp.sum(dw8_s[k], axis=0, keepdims=True)

    blk, left = _mix_specs(ts, n_r, True)
    n_halo_blocks = T // HALO
    right = pl.BlockSpec((HALO, CA), lambda b, r: (jnp.minimum((b * n_r + r + 1) * per, n_halo_blocks - 1), 0))
    return pl.pallas_call(
        body, name="mix_bwd_conv", grid=(n_seq, n_r),
        in_specs=[blk(0), right, blk(0), blk(1), left(0), left(1),
                  pl.BlockSpec((CONV_WIDTH, CA), lambda b, r: (0, 0))],
        out_specs=[pl.BlockSpec((ts, 2 * CA), lambda b, r: (b * n_r + r, 0)),
                   pl.BlockSpec((CONV_WIDTH, CA), lambda b, r: (0, 0)), pl.BlockSpec((1, CA), lambda b, r: (0, 0))],
        out_shape=[jax.ShapeDtypeStruct((T, 2 * CA), BF16), jax.ShapeDtypeStruct((CONV_WIDTH, CA), F32),
                   jax.ShapeDtypeStruct((1, CA), F32)],
        scratch_shapes=[pltpu.VMEM((SUBLANES, HALO + ts, CA), F32), pltpu.VMEM((SUBLANES, ts + HALO, CA), F32),
                        pltpu.VMEM((ts, CA), F32), pltpu.VMEM((CONV_WIDTH, 8, CA), F32)],
        compiler_params=_cp("arbitrary", "arbitrary"),
    )(da1, da1, z, z, z, z, conv_w)


def _row_tile(R, want):
    t = min(R, want)
    t -= t % 8
    while t > 8 and R % t:
        t -= 8
    return t if t >= 8 and R % t == 0 else R


def _adam_step(w, g, m, v):
    nm = ADAM_B1 * m + (1.0 - ADAM_B1) * g
    nv = ADAM_B2 * v + (1.0 - ADAM_B2) * (g * g)
    m_hat = nm / (1.0 - ADAM_B1 ** ADAM_STEP)
    v_hat = nv / (1.0 - ADAM_B2 ** ADAM_STEP)
    return -ADAM_LR * (m_hat / (jnp.sqrt(v_hat) + ADAM_EPS) + ADAM_WD * w), nm, nv


def _adamw_parts(parts, w, m, v, name):
    L, n, C = w.shape
    assert len(parts) == L
    tr = _row_tile(n, 192)
    n_i = n // tr

    def body(*refs):
        p_refs = refs[:L]
        w_ref, m_ref, v_ref, g_ref, d_ref, nm_ref, nv_ref = refs[L:]
        for k in range(L):
            @pl.when(pl.program_id(0) == k)
            def _(k=k):
                acc = p_refs[k][0].astype(F32)
                for s in range(1, N_DEV):
                    acc = acc + p_refs[k][s].astype(F32)
                g_ref[...] = acc

        d_ref[...], nm_ref[...], nv_ref[...] = _adam_step(w_ref[...], g_ref[...], m_ref[...], v_ref[...])

    def part_spec(k):
        return pl.BlockSpec((N_DEV, tr, C),
                            lambda l, i: (0, jnp.where(l == k, i, jnp.where(l < k, 0, n_i - 1)), 0))

    blk = pl.BlockSpec((None, tr, C), lambda l, i: (l, i, 0))
    return pl.pallas_call(
        body, name=name, grid=(L, n_i),
        in_specs=[part_spec(k) for k in range(L)] + [blk] * 3, out_specs=[blk] * 4,
        out_shape=[jax.ShapeDtypeStruct((L, n, C), F32)] * 4,
        compiler_params=_cp("arbitrary", "arbitrary"),
    )(*parts, w, m, v)


def _adamw(w, g, m, v, name):
    R, C = w.shape
    tr = _row_tile(R, 256)

    def body(w_ref, g_ref, m_ref, v_ref, d_ref, nm_ref, nv_ref):
        d_ref[...], nm_ref[...], nv_ref[...] = _adam_step(w_ref[...], g_ref[...], m_ref[...], v_ref[...])

    blk = pl.BlockSpec((tr, C), lambda i: (i, 0))
    return pl.pallas_call(
        body, name=name, grid=(R // tr,),
        in_specs=[blk] * 4, out_specs=[blk] * 3,
        out_shape=[jax.ShapeDtypeStruct((R, C), F32)] * 3,
        compiler_params=_cp("parallel"),
    )(w, g, m, v)


def _me():
    return lax.axis_index("x"), lax.axis_index("y"), lax.axis_index("c")


def _block_rows(ref, dev, n):
    start = (4 * dev[0] + 2 * dev[1] + dev[2]) * n
    if len(ref.shape) == 2:
        return ref.at[pl.ds(start, n), :]
    return ref.at[:, pl.ds(start, n), :]


def _all_gather(shards):
    na = len(shards)
    ns = [s.shape[-2] for s in shards]

    def body(*refs):
        ins, outs = refs[:na], refs[na:2 * na]
        send_sems, recv_sems, local_sems = refs[2 * na:]
        x, y, c = _me()
        me, sibling = (x, y, c), (x, y, 1 - c)
        chips = [(1 - x, y), (x, 1 - y), (1 - x, 1 - y)]

        def copy(a, k, block, to, src=None):
            dst = _block_rows(outs[a], block, ns[a])
            return pltpu.make_async_remote_copy(
                src_ref=dst if src is None else src, dst_ref=dst,
                send_sem=send_sems.at[a, k], recv_sem=recv_sems.at[a, k], device_id=to, device_id_type=MESH)

        mine = [pltpu.make_async_copy(ins[a], _block_rows(outs[a], me, ns[a]), local_sems.at[a]) for a in range(na)]
        for cp in mine:
            cp.start()
        first = []
        for a in range(na):
            first.append(copy(a, 0, me, sibling, src=ins[a]))
            first += [copy(a, 1 + j, me, (*chip, c), src=ins[a]) for j, chip in enumerate(chips)]
        for cp in first:
            cp.start()
        passed = []
        for j, chip in enumerate(chips):
            for a in range(na):
                copy(a, 1 + j, (*chip, c), me).wait_recv()
                fwd = copy(a, 4 + j, (*chip, c), sibling)
                fwd.start()
                passed.append(fwd)
        for a in range(na):
            copy(a, 0, sibling, me).wait_recv()
            for j, chip in enumerate(chips):
                copy(a, 4 + j, (*chip, 1 - c), me).wait_recv()
        for cp in first + passed:
            cp.wait_send()
        for cp in mine:
            cp.wait()

    out_shape = [jax.ShapeDtypeStruct(s.shape[:-2] + (N_DEV * s.shape[-2], s.shape[-1]), s.dtype) for s in shards]
    return pl.pallas_call(
        body, name="weights_all_gather",
        in_specs=[ANY] * na, out_specs=[ANY] * na, out_shape=out_shape,
        scratch_shapes=[pltpu.SemaphoreType.DMA((na, 7)), pltpu.SemaphoreType.DMA((na, 7)),
                        pltpu.SemaphoreType.DMA((na,))],
    )(*shards)


def _split_copies(gather, srcs, lands, send_sems, recv_sems, ns):
    x, y, c = _me()
    me = (x, y, c)
    my_slot = 4 * x + 2 * y + c
    copies = []
    for mask in range(1, N_DEV):
        peer = (x ^ (mask >> 2), y ^ ((mask >> 1) & 1), c ^ (mask & 1))
        for a in range(len(srcs)):
            if gather:
                src, dst = srcs[a], _block_rows(lands[a], me, ns[a])
            else:
                src, dst = _block_rows(srcs[a], peer, ns[a]), lands[a].at[my_slot]
            sem = a * (N_DEV - 1) + mask - 1
            copies.append(pltpu.make_async_remote_copy(
                src_ref=src, dst_ref=dst, send_sem=send_sems.at[sem], recv_sem=recv_sems.at[sem],
                device_id=peer, device_id_type=MESH))
    return copies


HBM_SPEC = pl.BlockSpec(memory_space=pltpu.HBM)
SEM_SPEC = pl.BlockSpec(memory_space=pltpu.SEMAPHORE)


def _split_start(gather, srcs, name, dep=None):
    na = len(srcs)
    x, y, c = _me()
    mine = 4 * x + 2 * y + c
    if gather:
        ns = [s.shape[-2] for s in srcs]
        lands = [lax.dynamic_update_slice(
            lax.empty(s.shape[:-2] + (N_DEV * s.shape[-2], s.shape[-1]), s.dtype), s,
            (0,) * (s.ndim - 2) + (mine * s.shape[-2], 0)) for s in srcs]
    else:
        ns = [s.shape[-2] // N_DEV for s in srcs]
        lands = [lax.dynamic_update_slice(
            lax.empty((N_DEV, n, s.shape[-1]), s.dtype),
            lax.dynamic_slice(s, (mine * n, 0), (n, s.shape[-1]))[None], (mine, 0, 0)) for s, n in zip(srcs, ns)]
    n_in = 2 * na + (dep is not None)

    def body(*refs):
        send_sems, recv_sems = refs[n_in], refs[n_in + 1]
        for cp in _split_copies(gather, refs[:na], refs[na:2 * na], send_sems, recv_sems, ns):
            cp.start()
        refs[-1][...] = jnp.zeros_like(refs[-1])

    hbm = lambda a: pltpu.with_memory_space_constraint(a, pltpu.HBM)
    args = [hbm(a) for a in srcs] + [hbm(a) for a in lands] + ([dep] if dep is not None else [])
    out = pl.pallas_call(
        body, name=name,
        in_specs=[HBM_SPEC] * (2 * na) + ([ANY] if dep is not None else []),
        out_specs=[SEM_SPEC, SEM_SPEC] + [HBM_SPEC] * (2 * na) + [pl.BlockSpec(memory_space=pltpu.VMEM)],
        out_shape=[pltpu.SemaphoreType.DMA((na * (N_DEV - 1),)), pltpu.SemaphoreType.DMA((na * (N_DEV - 1),))]
        + [pltpu.HBM(a.shape, a.dtype) for a in srcs + lands] + [jax.ShapeDtypeStruct((8, LANES), F32)],
        input_output_aliases={i: 2 + i for i in range(2 * na)},
        compiler_params=pltpu.CompilerParams(has_side_effects=pltpu.SideEffectType.DATAFLOW_SIDE_EFFECTING),
    )(*args)
    return (gather, ns, out[0], out[1], list(out[2:2 + na]), list(out[2 + na:2 + 2 * na])), out[-1]


def _split_wait(handle, after, name):
    gather, ns, send, recv, srcs, lands = handle
    na = len(srcs)

    def body(*refs):
        send_sems, recv_sems = refs[2 * na], refs[2 * na + 1]
        for cp in _split_copies(gather, refs[:na], refs[na:2 * na], send_sems, recv_sems, ns):
            cp.wait_send()
            cp.wait_recv()

    out = pl.pallas_call(
        body, name=name,
        in_specs=[HBM_SPEC] * (2 * na) + [SEM_SPEC, SEM_SPEC, ANY],
        out_specs=[HBM_SPEC] * (2 * na),
        out_shape=[pltpu.HBM(a.shape, a.dtype) for a in srcs + lands],
        input_output_aliases={i: i for i in range(2 * na)},
        compiler_params=pltpu.CompilerParams(has_side_effects=pltpu.SideEffectType.DATAFLOW_SIDE_EFFECTING),
    )(*srcs, *lands, send, recv, after)
    return list(out[:na]), list(out[na:])


def _sum_blocks(parts):
    n, R, C = parts.shape

    def body(p_ref, o_ref):
        acc = p_ref[0]
        for k in range(1, n):
            acc = acc + p_ref[k]
        o_ref[...] = acc

    return pl.pallas_call(
        body, name="small_sum",
        in_specs=[pl.BlockSpec(memory_space=pltpu.VMEM)], out_specs=pl.BlockSpec(memory_space=pltpu.VMEM),
        out_shape=jax.ShapeDtypeStruct((R, C), F32),
        compiler_params=pltpu.CompilerParams(vmem_limit_bytes=VMEM_LIMIT),
    )(parts)


def _pack(arrays):
    flat = jnp.concatenate([a.reshape(-1) for a in arrays])
    pad = (-flat.shape[0]) % (8 * LANES)
    return jnp.pad(flat, (0, pad)).reshape(-1, LANES)


def _unpack(buf, shapes):
    flat = buf.reshape(-1)
    out, off = [], 0
    for s in shapes:
        n = 1
        for d in s:
            n *= d
        out.append(flat[off:off + n].reshape(s))
        off += n
    return out


def _ffn_index(layer, second):
    return (2 * layer + second) * 3


def kernel(x, g_ffn1, w_ffn1_gate, w_ffn1_up, w_ffn1_down, g_mix, w_in_ab, conv_w, conv_b, ln_a_g, ln_a_b, ln_v_g, ln_v_b, sp_w, sp_b, w_out_ab, w_qkv, w_o, g_ffn2, w_ffn2_gate, w_ffn2_up, w_ffn2_down, g_final, loss_target, m_g_ffn1, m_w_ffn1_gate, m_w_ffn1_up, m_w_ffn1_down, m_g_mix, m_w_in_ab, m_conv_w, m_conv_b, m_ln_a_g, m_ln_a_b, m_ln_v_g, m_ln_v_b, m_sp_w, m_sp_b, m_w_out_ab, m_w_qkv, m_w_o, m_g_ffn2, m_w_ffn2_gate, m_w_ffn2_up, m_w_ffn2_down, m_g_final, v_g_ffn1, v_w_ffn1_gate, v_w_ffn1_up, v_w_ffn1_down, v_g_mix, v_w_in_ab, v_conv_w, v_conv_b, v_ln_a_g, v_ln_a_b, v_ln_v_g, v_ln_v_b, v_sp_w, v_sp_b, v_w_out_ab, v_w_qkv, v_w_o, v_g_ffn2, v_w_ffn2_gate, v_w_ffn2_up, v_w_ffn2_down, v_g_final):
    n_seq, S, D = x.shape
    T = n_seq * S
    depth = g_ffn1.shape[0]
    assert depth == 2 and D == D_MODEL
    my_block = 4 * lax.axis_index("x") + 2 * lax.axis_index("y") + lax.axis_index("c")

    ffn_parts = []
    for l in range(depth):
        for gate, up, down in ((w_ffn1_gate, w_ffn1_up, w_ffn1_down), (w_ffn2_gate, w_ffn2_up, w_ffn2_down)):
            ffn_parts += [gate[l].T, up[l].T, down[l]]
    ffn_shard = lambda k: jnp.stack(ffn_parts[3 * k:3 * k + 3]).astype(BF16)
    conv_w_pad = jnp.zeros((HALO, conv_w.shape[2]), F32).at[:CONV_WIDTH].set(conv_w[0]).T
    w_ffn = [None] * (2 * depth)
    w_ffn[0], conv_w_t = _all_gather([ffn_shard(0), conv_w_pad])
    conv_w_full = conv_w_t.T[:CONV_WIDTH]
    shards_b = [w_out_ab[0].astype(BF16), ffn_shard(1)]
    shards_d = [w_qkv[0].T.astype(BF16), w_o[0].astype(BF16), ffn_shard(3)]
    gather_a, token = _split_start(True, [w_in_ab[0].T.astype(BF16)], "gather_a_start", dep=conv_w_t)
    gather_b, token = _split_start(True, shards_b, "gather_b_start", dep=token)
    gather_c, token = _split_start(True, [ffn_shard(2)], "gather_c_start", dep=token)
    gather_d, token = _split_start(True, shards_d, "gather_d_start", dep=token)

    def gathered(handle, after, name):
        return _split_wait(handle, after, name)[1]

    row = lambda a: a.reshape(1, -1)
    tril = jnp.tril(jnp.ones((CHUNK, CHUNK), dtype=bool))
    ws = jnp.where(tril[None], sp_w[0], 0.0).astype(BF16)
    ws_t = jnp.swapaxes(ws, 1, 2)
    bias2d = jnp.repeat(sp_b[0].T, DB, axis=1)
    conv_b2, lag, lab = row(conv_b[0]), row(ln_a_g[0]), row(ln_a_b[0])
    lvg, lvb = row(ln_v_g[0]), row(ln_v_b[0])

    x0 = x.reshape(T, D)
    target = loss_target.reshape(T, D)
    saved = []
    xc = x0
    for l in range(depth):
        xa, a1, b1, h1 = _ffn_fwd(xc, row(g_ffn1[l]), w_ffn[2 * l], 0, f"ffn1_fwd_{l}", dep=token)
        if l % 2 == 0:
            w_in_t, = gathered(gather_a, xa, "gather_a_wait")
            z, hm = _mm_nt(xa, w_in_t, row(g_mix[l]), F32, "mix_in_proj")
            cat, conv_out = _mix_fwd(z, conv_w_full, conv_b2, lag, lab, lvg, lvb, ws, bias2d, n_seq)
            w_out, w_ffn[1] = gathered(gather_b, cat, "gather_b_wait")
            out_proj = (cat, w_out)
            mixer = (z, hm, cat, conv_out)
        else:
            w_qkv_t, w_o_full, w_ffn[3] = gathered(gather_d, xa, "gather_d_wait")
            qkv, hm = _mm_nt(xa, w_qkv_t, row(g_mix[l]), BF16, "qkv_proj")
            o, att = _attn_fwd(qkv, n_seq)
            out_proj = (o, w_o_full)
            mixer = (qkv, hm, o, att)
        xb, xn, a2, b2, h2 = _proj_ffn_fwd(*out_proj, xa, row(g_ffn2[l]), w_ffn[2 * l + 1], 0, f"ffn2_fwd_{l}")
        saved.append((xc, a1, b1, h1, xa, mixer, xb, a2, b2, h2))
        xc = xn
        if l == 0:
            w_ffn[2], = gathered(gather_c, xc, "gather_c_wait")

    g, loss_part, dg_final = _loss_head(xc, row(g_final), target)

    dg_ffn1, dg_ffn2, dg_mix = [None] * depth, [None] * depth, [None] * depth
    exchanges = {}
    token = None

    def ffn_back(g, xin, gvec, a, b, h, k, tag, token):
        g, dg, da, db, s, gh = _ffn_bwd(g, xin, gvec, a, b, w_ffn[k], 0, f"ffn{tag}_bwd", dep=token)
        if k == 0:
            return g, dg, (da, db, s, gh, h)
        dws = [_mm_tn(da, h, f"dw_gate{tag}"), _mm_tn(db, h, f"dw_up{tag}"), _mm_tn(s, gh, f"dw_down{tag}")]
        exchanges[f"ffn{k}"], token = _split_start(False, dws, f"exchange_ffn{tag}_start")
        return g, dg, token

    for l in reversed(range(depth)):
        xin, a1, b1, h1, xa, mixer, xb, a2, b2, h2 = saved[l]
        g, dg_ffn2[l], token = ffn_back(g, xb, row(g_ffn2[l]), a2, b2, h2, 2 * l + 1, f"2_{l}", token)
        if l % 2 == 0:
            z, hm, cat, conv_out = mixer
            dcat = _mm_nt(g, w_out, None, F32, "mix_out_bwd", dep=token)
            d_w_out = _mm_tn(cat, g, "dw_out")
            (da1, dz_uv, d_lag, d_lab, d_lvg, d_lvb, d_ws, d_sb) = _mix_bwd_rows(
                dcat, z, conv_out, lag, lab, lvg, lvb, ws, ws_t, bias2d, n_seq)
            dz_a, d_cw, d_cb = _mix_bwd_conv(da1, z, conv_w_full, n_seq)
            d_w_in_t = jnp.concatenate([_mm_tn(dz_a, hm, "dw_in_a"), _mm_tn(dz_uv, hm, "dw_in_uv")])
            exchanges["mix"], token = _split_start(False, [d_w_out, d_w_in_t], "exchange_mix_start")
            g, dg_mix[l] = _mm_nn_rmsbwd([dz_a, dz_uv], w_in_t, xa, row(g_mix[l]), g, "mix_in_bwd", dep=token)
        else:
            qkv, hm, o, att = mixer
            do = _mm_nt(g, w_o_full, None, BF16, "attn_out_bwd", dep=token)
            d_w_o = _mm_tn(o, g, "dw_o")
            dq, dk, dv = _attn_bwd(qkv, do, att, n_seq)
            d_w_qkv_t = jnp.concatenate([_mm_tn(dq, hm, "dw_q"), _mm_tn(dk, hm, "dw_k"), _mm_tn(dv, hm, "dw_v")])
            exchanges["attn"], token = _split_start(False, [d_w_o, d_w_qkv_t], "exchange_attn_start")
            g, dg_mix[l] = _mm_nn_rmsbwd([dq, dk, dv], w_qkv_t, xa, row(g_mix[l]), g, "qkv_bwd", dep=token)
        g, dg_ffn1[l], token = ffn_back(g, xin, row(g_ffn1[l]), a1, b1, h1, 2 * l, f"1_{l}", token)
    grad_x = g.reshape(n_seq, S, D)

    small = [jnp.concatenate(dg_ffn1), jnp.concatenate(dg_mix), d_cw, d_cb, d_lag, d_lab, d_lvg, d_lvb,
             jnp.where(tril[None], d_ws, 0.0), d_sb.T, jnp.concatenate(dg_ffn2), dg_final, loss_part[:, :1]]
    small_shapes = [(depth, D), (depth, D), (CONV_WIDTH, CA), (1, CA), (1, CA), (1, CA), (1, GB, DB), (1, GB, DB),
                    (1, GB, CHUNK, CHUNK), (1, GB, CHUNK), (depth, D), (D,), ()]
    da, db, s, gh, h = token
    small_gather, token = _split_start(True, [_pack(small)], "small_gather_start")

    for which, lhs, rhs in ((2, s, gh), (1, db, h), (0, da, h)):
        dw = _mm_tn(lhs, rhs, f"dw_ffn0_{which}", dep=token, out_dtype=BF16)
        exchanges[f"ffn0_{which}"], token = _split_start(False, [dw], f"exchange_ffn0_{which}_start")

    small_all, = _split_wait(small_gather, token, "small_gather_wait")[1]
    red = _unpack(_sum_blocks(small_all.reshape(N_DEV, -1, LANES)), small_shapes)
    (gr_g_ffn1, gr_g_mix, gr_cw_full, gr_cb, gr_lag, gr_lab, gr_lvg, gr_lvb, gr_sp_w, gr_sp_b,
     gr_g_ffn2, gr_g_final, loss) = red
    n_cw = conv_w.shape[2]
    gr_cw = lax.dynamic_slice(gr_cw_full, (0, my_block * n_cw), (CONV_WIDTH, n_cw))[None]

    def landed(key, after):
        return _split_wait(exchanges[key], after, f"exchange_{key}_wait")[1]

    parts_ffn = [None] * (6 * depth)
    for k in range(1, 2 * depth):
        parts_ffn[3 * k:3 * k + 3] = landed(f"ffn{k}", token)
    parts_out, parts_in = landed("mix", token)
    parts_o, parts_qkv = landed("attn", token)

    grads = {
        "g_ffn1": gr_g_ffn1, "g_mix": gr_g_mix, "conv_w": gr_cw, "conv_b": gr_cb, "ln_a_g": gr_lag,
        "ln_a_b": gr_lab, "ln_v_g": gr_lvg, "ln_v_b": gr_lvb, "sp_w": gr_sp_w, "sp_b": gr_sp_b,
        "g_ffn2": gr_g_ffn2, "g_final": gr_g_final,
    }
    weights = dict(g_ffn1=g_ffn1, w_ffn1_gate=w_ffn1_gate, w_ffn1_up=w_ffn1_up, w_ffn1_down=w_ffn1_down, g_mix=g_mix,
                   w_in_ab=w_in_ab, conv_w=conv_w, conv_b=conv_b, ln_a_g=ln_a_g, ln_a_b=ln_a_b, ln_v_g=ln_v_g,
                   ln_v_b=ln_v_b, sp_w=sp_w, sp_b=sp_b, w_out_ab=w_out_ab, w_qkv=w_qkv, w_o=w_o, g_ffn2=g_ffn2,
                   w_ffn2_gate=w_ffn2_gate, w_ffn2_up=w_ffn2_up, w_ffn2_down=w_ffn2_down, g_final=g_final)
    m_in = dict(g_ffn1=m_g_ffn1, w_ffn1_gate=m_w_ffn1_gate, w_ffn1_up=m_w_ffn1_up, w_ffn1_down=m_w_ffn1_down,
                g_mix=m_g_mix, w_in_ab=m_w_in_ab, conv_w=m_conv_w, conv_b=m_conv_b, ln_a_g=m_ln_a_g, ln_a_b=m_ln_a_b,
                ln_v_g=m_ln_v_g, ln_v_b=m_ln_v_b, sp_w=m_sp_w, sp_b=m_sp_b, w_out_ab=m_w_out_ab, w_qkv=m_w_qkv,
                w_o=m_w_o, g_ffn2=m_g_ffn2, w_ffn2_gate=m_w_ffn2_gate, w_ffn2_up=m_w_ffn2_up,
                w_ffn2_down=m_w_ffn2_down, g_final=m_g_final)
    v_in = dict(g_ffn1=v_g_ffn1, w_ffn1_gate=v_w_ffn1_gate, w_ffn1_up=v_w_ffn1_up, w_ffn1_down=v_w_ffn1_down,
                g_mix=v_g_mix, w_in_ab=v_w_in_ab, conv_w=v_conv_w, conv_b=v_conv_b, ln_a_g=v_ln_a_g, ln_a_b=v_ln_a_b,
                ln_v_g=v_ln_v_g, ln_v_b=v_ln_v_b, sp_w=v_sp_w, sp_b=v_sp_b, w_out_ab=v_w_out_ab, w_qkv=v_w_qkv,
                w_o=v_w_o, g_ffn2=v_g_ffn2, w_ffn2_gate=v_w_ffn2_gate, w_ffn2_up=v_w_ffn2_up,
                w_ffn2_down=v_w_ffn2_down, g_final=v_g_final)
    names = list(weights)
    grads = {n: grads[n].reshape(weights[n].shape) for n in grads}

    delta, new_m, new_v = {}, {}, {}

    def adamw_big(n, parts):
        if weights[n].shape[-1] == D:
            view = back = lambda a: a
        else:
            view = back = lambda a: jnp.swapaxes(a, 1, 2)
        out = _adamw_parts(parts, view(weights[n]), view(m_in[n]), view(v_in[n]), f"adamw_{n}")
        grads[n], delta[n], new_m[n], new_v[n] = [back(a) for a in out]

    adamw_big("w_in_ab", [parts_in])
    adamw_big("w_out_ab", [parts_out])
    adamw_big("w_qkv", [parts_qkv])
    adamw_big("w_o", [parts_o])
    kinds = ("gate", "up", "down")
    for which, kind in enumerate(kinds):
        adamw_big(f"w_ffn2_{kind}", [parts_ffn[_ffn_index(l, 1) + which] for l in range(depth)])
    big = [n for n in names if n.startswith("w_")]
    after = jnp.concatenate([delta[n].reshape(-1)[:1] for n in big if n in delta]).reshape(1, -1)
    for which in (2, 1, 0):
        parts_ffn[which], = landed(f"ffn0_{which}", after)
    for which, kind in enumerate(kinds):
        adamw_big(f"w_ffn1_{kind}", [parts_ffn[_ffn_index(l, 0) + which] for l in range(depth)])
    little = [n for n in names if n not in big]
    shapes = [weights[n].shape for n in little]
    d, nm, nv = _adamw(_pack([weights[n] for n in little]), _pack([grads[n] for n in little]),
                       _pack([m_in[n] for n in little]), _pack([v_in[n] for n in little]), "adamw_small")
    for n, dd, mm, vv in zip(little, _unpack(d, shapes), _unpack(nm, shapes), _unpack(nv, shapes)):
        delta[n], new_m[n], new_v[n] = dd, mm, vv

    return (loss, grad_x, *[grads[n] for n in names], *[delta[n] for n in names],
            *[new_m[n] for n in names], *[new_v[n] for n in names])
```

```python
import jax
import jax.numpy as jnp
from jax import lax
from jax.experimental import pallas as pl
from jax.experimental.pallas import tpu as pltpu

F32 = jnp.float32
BF16 = jnp.bfloat16

D_MODEL = 1024
CA = 512
CB = 512
GB = 4
DB = 128
CHUNK = 128
CONV_WIDTH = 31
N_HEADS = 16
HEAD_DIM = 64
EPS = 1e-6
N_DEV = 8
LANES = 128
SUBLANES = 8
QB = 128
ATT_TQ = 1024
FFN_TN = 2816
FFN_TM = 256
HALO = 32
CONV_ROWS = 32
ATT_SCALE = HEAD_DIM ** -0.5

ADAM_LR = 0.001
ADAM_B1 = 0.9
ADAM_B2 = 0.999
ADAM_EPS = 1e-08
ADAM_WD = 0.01
ADAM_STEP = 10

NT = (((1,), (1,)), ((), ()))
NN = (((1,), (0,)), ((), ()))
TN = (((0,), (0,)), ((), ()))
MESH = pl.DeviceIdType.MESH
ANY = pl.BlockSpec(memory_space=pl.ANY)
VMEM_LIMIT = 60 * 1024 * 1024


def _dot(a, b, dims):
    return lax.dot_general(a, b, dims, preferred_element_type=F32)


def _cp(*sem):
    return pltpu.CompilerParams(dimension_semantics=sem, vmem_limit_bytes=VMEM_LIMIT)


def _pcall(body, *, in_specs, args, dep=None, **kw):
    if dep is not None:
        n_in = len(in_specs)
        inner = body

        def body(*refs):
            inner(*refs[:n_in], *refs[n_in + 1:])

        in_specs = list(in_specs) + [ANY]
        args = tuple(args) + (dep,)
    return pl.pallas_call(body, in_specs=list(in_specs), **kw)(*args)


def _tile(n, want):
    if n <= want:
        return n
    t = want - want % LANES
    while t > LANES and n % t:
        t -= LANES
    assert n % t == 0, (n, want)
    return t


def _sigmoid(x):
    return 0.5 * jnp.tanh(0.5 * x) + 0.5


def _rstd(x):
    return lax.rsqrt(jnp.mean(x * x, axis=-1, keepdims=True) + EPS)


def _rms_bwd(x, g, dh):
    r = _rstd(x)
    u = dh * g
    dx = r * (u - x * (r * r) * jnp.mean(u * x, axis=-1, keepdims=True))
    dg = jnp.sum(dh * x * r, axis=0, keepdims=True)
    return dx, dg


def _ln_fwd(x, g, b):
    mu = jnp.mean(x, axis=-1, keepdims=True)
    xc = x - mu
    r = lax.rsqrt(jnp.mean(xc * xc, axis=-1, keepdims=True) + EPS)
    xh = xc * r
    return xh * g + b, xh, r


def _ln_bwd(dy, xh, r, g):
    dxh = dy * g
    return r * (dxh - jnp.mean(dxh, axis=-1, keepdims=True)
                - xh * jnp.mean(dxh * xh, axis=-1, keepdims=True))


def _ffn_fwd(x, g, wall, base, name, dep=None):
    T, D = x.shape
    F = wall.shape[1]
    tm, tn = _tile(T, FFN_TM), _tile(F, FFN_TN)
    n_j = F // tn

    def body(x_ref, g_ref, wg_ref, wu_ref, wd_ref, xo_ref, a_ref, b_ref, h_ref, acc_ref):
        j = pl.program_id(1)

        @pl.when(j == 0)
        def _():
            xv = x_ref[...]
            h_ref[...] = (xv * _rstd(xv) * g_ref[...]).astype(BF16)
            acc_ref[...] = jnp.zeros_like(acc_ref)

        h = h_ref[...]
        a = _dot(h, wg_ref[...], NT)
        b = _dot(h, wu_ref[...], NT)
        a_ref[...] = a.astype(BF16)
        b_ref[...] = b.astype(BF16)
        s = (a * _sigmoid(a) * b).astype(BF16)
        acc_ref[...] += _dot(s, wd_ref[...], NN)

        @pl.when(j == n_j - 1)
        def _():
            xo_ref[...] = x_ref[...] + 0.5 * acc_ref[...]

    single = pl.Buffered(1) if n_j == 1 else None
    wspec = lambda k: pl.BlockSpec((None, tn, D), lambda i, j: (base + k, j, 0), pipeline_mode=single)
    return _pcall(
        body, name=name, grid=(T // tm, n_j), dep=dep, args=(x, g, wall, wall, wall),
        in_specs=[pl.BlockSpec((tm, D), lambda i, j: (i, 0)), pl.BlockSpec((1, D), lambda i, j: (0, 0)),
                  wspec(0), wspec(1), wspec(2)],
        out_specs=[pl.BlockSpec((tm, D), lambda i, j: (i, 0)), pl.BlockSpec((tm, tn), lambda i, j: (i, j)),
                   pl.BlockSpec((tm, tn), lambda i, j: (i, j)), pl.BlockSpec((tm, D), lambda i, j: (i, 0))],
        out_shape=[jax.ShapeDtypeStruct((T, D), F32), jax.ShapeDtypeStruct((T, F), BF16),
                   jax.ShapeDtypeStruct((T, F), BF16), jax.ShapeDtypeStruct((T, D), BF16)],
        scratch_shapes=[pltpu.VMEM((tm, D), F32)],
        compiler_params=_cp("parallel", "arbitrary"),
    )


def _proj_ffn_fwd(act, w, resid, g, wall, base, name):
    T, D = resid.shape
    K = act.shape[1]
    F = wall.shape[1]
    tm = _tile(T, FFN_TM)

    def body(act_ref, w_ref, r_ref, g_ref, wg_ref, wu_ref, wd_ref, x_ref, xo_ref, a_ref, b_ref, h_ref):
        xv = r_ref[...] + _dot(act_ref[...].astype(BF16), w_ref[...], NN)
        x_ref[...] = xv
        h = (xv * _rstd(xv) * g_ref[...]).astype(BF16)
        h_ref[...] = h
        a = _dot(h, wg_ref[...], NT)
        b = _dot(h, wu_ref[...], NT)
        a_ref[...] = a.astype(BF16)
        b_ref[...] = b.astype(BF16)
        s = (a * _sigmoid(a) * b).astype(BF16)
        xo_ref[...] = x_ref[...] + 0.5 * _dot(s, wd_ref[...], NN)

    one = pl.Buffered(1)
    wspec = lambda k: pl.BlockSpec((None, F, D), lambda i: (base + k, 0, 0), pipeline_mode=one)
    row = pl.BlockSpec((tm, D), lambda i: (i, 0))
    hid = pl.BlockSpec((tm, F), lambda i: (i, 0))
    return pl.pallas_call(
        body, name=name, grid=(T // tm,),
        in_specs=[pl.BlockSpec((tm, K), lambda i: (i, 0)), pl.BlockSpec((K, D), lambda i: (0, 0), pipeline_mode=one),
                  row, pl.BlockSpec((1, D), lambda i: (0, 0)), wspec(0), wspec(1), wspec(2)],
        out_specs=[row, row, hid, hid, row],
        out_shape=[jax.ShapeDtypeStruct((T, D), F32), jax.ShapeDtypeStruct((T, D), F32),
                   jax.ShapeDtypeStruct((T, F), BF16), jax.ShapeDtypeStruct((T, F), BF16),
                   jax.ShapeDtypeStruct((T, D), BF16)],
        compiler_params=_cp("parallel"),
    )(act, w, resid, g, wall, wall, wall)


def _proj_ffn_loss(act, w, resid, g, wall, base, g_final, target, name):
    T, D = resid.shape
    K = act.shape[1]
    F = wall.shape[1]
    tm = _tile(T, FFN_TM)

    def body(act_ref, w_ref, r_ref, g_ref, wg_ref, wu_ref, wd_ref, gf_ref, t_ref,
             x_ref, a_ref, b_ref, h_ref, dx_ref, loss_ref, dgf_ref):
        @pl.when(pl.program_id(0) == 0)
        def _():
            loss_ref[...] = jnp.zeros_like(loss_ref)
            dgf_ref[...] = jnp.zeros_like(dgf_ref)

        xv = r_ref[...] + _dot(act_ref[...].astype(BF16), w_ref[...], NN)
        x_ref[...] = xv
        h = (xv * _rstd(xv) * g_ref[...]).astype(BF16)
        h_ref[...] = h
        a = _dot(h, wg_ref[...], NT)
        b = _dot(h, wu_ref[...], NT)
        a_ref[...] = a.astype(BF16)
        b_ref[...] = b.astype(BF16)
        s = (a * _sigmoid(a) * b).astype(BF16)
        xo = x_ref[...] + 0.5 * _dot(s, wd_ref[...], NN)
        gf = gf_ref[...]
        e = xo * _rstd(xo) * gf - t_ref[...]
        per_tok = jnp.sum(e * e, axis=-1, keepdims=True) * (1.0 / D)
        loss_ref[...] += 0.5 * jnp.sum(per_tok, axis=0, keepdims=True)
        dx, dg = _rms_bwd(xo, gf, e * (1.0 / D))
        dx_ref[...] = dx
        dgf_ref[...] += dg

    one = pl.Buffered(1)
    wspec = lambda k: pl.BlockSpec((None, F, D), lambda i: (base + k, 0, 0), pipeline_mode=one)
    row = pl.BlockSpec((tm, D), lambda i: (i, 0))
    hid = pl.BlockSpec((tm, F), lambda i: (i, 0))
    vec = pl.BlockSpec((1, D), lambda i: (0, 0))
    return pl.pallas_call(
        body, name=name, grid=(T // tm,),
        in_specs=[pl.BlockSpec((tm, K), lambda i: (i, 0)), pl.BlockSpec((K, D), lambda i: (0, 0), pipeline_mode=one),
                  row, vec, wspec(0), wspec(1), wspec(2), vec, row],
        out_specs=[row, hid, hid, row, row, pl.BlockSpec((1, LANES), lambda i: (0, 0)), vec],
        out_shape=[jax.ShapeDtypeStruct((T, D), F32), jax.ShapeDtypeStruct((T, F), BF16),
                   jax.ShapeDtypeStruct((T, F), BF16), jax.ShapeDtypeStruct((T, D), BF16),
                   jax.ShapeDtypeStruct((T, D), F32), jax.ShapeDtypeStruct((1, LANES), F32),
                   jax.ShapeDtypeStruct((1, D), F32)],
        compiler_params=_cp("arbitrary"),
    )(act, w, resid, g, wall, wall, wall, g_final, target)


def _ffn_bwd(go, x, g, a, b, wall, base, name, dep=None):
    T, D = x.shape
    F = wall.shape[1]
    tm, tn = _tile(T, FFN_TM), _tile(F, FFN_TN)
    n_j = F // tn

    def body(go_ref, x_ref, g_ref, a_ref, b_ref, wg_ref, wu_ref, wd_ref,
             gx_ref, dg_ref, da_ref, db_ref, s_ref, gh_ref, acc_ref):
        i, j = pl.program_id(0), pl.program_id(1)

        @pl.when(j == 0)
        def _():
            gh_ref[...] = (0.5 * go_ref[...]).astype(BF16)
            acc_ref[...] = jnp.zeros_like(acc_ref)

        @pl.when((i == 0) & (j == 0))
        def _():
            dg_ref[...] = jnp.zeros_like(dg_ref)

        ds = _dot(gh_ref[...], wd_ref[...], NT)
        av = a_ref[...].astype(F32)
        bv = b_ref[...].astype(F32)
        sig = _sigmoid(av)
        sl = av * sig
        dab = ((ds * bv) * (sig + sl * (1.0 - sig))).astype(BF16)
        dbb = (ds * sl).astype(BF16)
        s_ref[...] = (sl * bv).astype(BF16)
        da_ref[...] = dab
        db_ref[...] = dbb
        acc_ref[...] += _dot(dab, wg_ref[...], NN) + _dot(dbb, wu_ref[...], NN)

        @pl.when(j == n_j - 1)
        def _():
            dx, dg = _rms_bwd(x_ref[...], g_ref[...], acc_ref[...])
            gx_ref[...] = go_ref[...] + dx
            dg_ref[...] += dg

    single = pl.Buffered(1) if n_j == 1 else None
    wspec = lambda k: pl.BlockSpec((None, tn, D), lambda i, j: (base + k, j, 0), pipeline_mode=single)
    row = pl.BlockSpec((tm, D), lambda i, j: (i, 0))
    hid = pl.BlockSpec((tm, tn), lambda i, j: (i, j))
    vec = pl.BlockSpec((1, D), lambda i, j: (0, 0))
    return _pcall(
        body, name=name, grid=(T // tm, n_j), dep=dep, args=(go, x, g, a, b, wall, wall, wall),
        in_specs=[row, row, vec, hid, hid, wspec(0), wspec(1), wspec(2)],
        out_specs=[row, vec, hid, hid, hid, row],
        out_shape=[jax.ShapeDtypeStruct((T, D), F32), jax.ShapeDtypeStruct((1, D), F32),
                   jax.ShapeDtypeStruct((T, F), BF16), jax.ShapeDtypeStruct((T, F), BF16),
                   jax.ShapeDtypeStruct((T, F), BF16), jax.ShapeDtypeStruct((T, D), BF16)],
        scratch_shapes=[pltpu.VMEM((tm, D), F32)],
        compiler_params=_cp("arbitrary", "arbitrary"),
    )


def _mm_tn(a, b, name, dep=None, out_dtype=F32):
    T, M = a.shape
    N = b.shape[1]
    tmm, tk = _tile(M, 1536), _tile(T, 2048)
    n_k = T // tk
    narrow = out_dtype != F32

    def body(a_ref, b_ref, o_ref, *scratch):
        acc_ref = scratch[0] if narrow else o_ref

        @pl.when(pl.program_id(1) == 0)
        def _():
            acc_ref[...] = jnp.zeros_like(acc_ref)

        acc_ref[...] += _dot(a_ref[...].astype(BF16), b_ref[...].astype(BF16), TN)
        if narrow:
            @pl.when(pl.program_id(1) == n_k - 1)
            def _():
                o_ref[...] = acc_ref[...].astype(out_dtype)

    return _pcall(
        body, name=name, grid=(M // tmm, n_k), dep=dep, args=(a, b),
        in_specs=[pl.BlockSpec((tk, tmm), lambda m, k: (k, m)), pl.BlockSpec((tk, N), lambda m, k: (k, 0))],
        out_specs=pl.BlockSpec((tmm, N), lambda m, k: (m, 0)),
        out_shape=jax.ShapeDtypeStruct((M, N), out_dtype),
        scratch_shapes=[pltpu.VMEM((tmm, N), F32)] if narrow else [],
        compiler_params=_cp("parallel", "arbitrary"),
    )


def _mm_nt(x, wt, g, out_dtype, name, dep=None):
    T, K = x.shape
    N = wt.shape[0]
    tm, tn = _tile(T, 512), N
    norm = g is not None

    def body(*refs):
        if norm:
            x_ref, g_ref, w_ref, o_ref, h_ref = refs
        else:
            x_ref, w_ref, o_ref, h_ref = refs

        @pl.when(pl.program_id(1) == 0)
        def _():
            xv = x_ref[...].astype(F32)
            if norm:
                xv = xv * _rstd(xv) * g_ref[...]
            h_ref[...] = xv.astype(BF16)

        o_ref[...] = _dot(h_ref[...], w_ref[...], NT).astype(out_dtype)

    row = pl.BlockSpec((tm, K), lambda i, j: (i, 0))
    wsp = pl.BlockSpec((tn, K), lambda i, j: (j, 0))
    osp = pl.BlockSpec((tm, tn), lambda i, j: (i, j))
    if norm:
        return pl.pallas_call(
            body, name=name, grid=(T // tm, N // tn),
            in_specs=[row, pl.BlockSpec((1, K), lambda i, j: (0, 0)), wsp],
            out_specs=[osp, row],
            out_shape=[jax.ShapeDtypeStruct((T, N), out_dtype), jax.ShapeDtypeStruct((T, K), BF16)],
            compiler_params=_cp("parallel", "arbitrary"),
        )(x, g, wt)
    return _pcall(
        body, name=name, grid=(T // tm, N // tn), dep=dep, args=(x, wt),
        in_specs=[row, wsp], out_specs=osp,
        out_shape=jax.ShapeDtypeStruct((T, N), out_dtype),
        scratch_shapes=[pltpu.VMEM((tm, K), BF16)],
        compiler_params=_cp("parallel", "arbitrary"),
    )


def _mm_nn_res(act, w, resid, name):
    T, K = act.shape
    D = w.shape[1]
    tm = _tile(T, 512)

    def body(a_ref, w_ref, r_ref, o_ref):
        o_ref[...] = r_ref[...] + _dot(a_ref[...].astype(BF16), w_ref[...], NN)

    return pl.pallas_call(
        body, name=name, grid=(T // tm,),
        in_specs=[pl.BlockSpec((tm, K), lambda i: (i, 0)), pl.BlockSpec((K, D), lambda i: (0, 0)),
                  pl.BlockSpec((tm, D), lambda i: (i, 0))],
        out_specs=pl.BlockSpec((tm, D), lambda i: (i, 0)),
        out_shape=jax.ShapeDtypeStruct((T, D), F32),
        compiler_params=_cp("parallel"),
    )(act, w, resid)


def _mm_nn_rmsbwd(acts, w, x, g, gprev, name, dep=None):
    T = acts[0].shape[0]
    ks = [a.shape[1] for a in acts]
    K, D = w.shape
    assert sum(ks) == K
    tm = _tile(T, 512)
    na = len(acts)

    def body(*refs):
        a_refs = refs[:na]
        w_ref, x_ref, g_ref, gp_ref, o_ref, dg_ref = refs[na:]

        @pl.when(pl.program_id(0) == 0)
        def _():
            dg_ref[...] = jnp.zeros_like(dg_ref)

        dh, off = None, 0
        for a_ref, k in zip(a_refs, ks):
            part = _dot(a_ref[...].astype(BF16), w_ref[off:off + k, :], NN)
            dh = part if dh is None else dh + part
            off += k
        dx, dg = _rms_bwd(x_ref[...], g_ref[...], dh)
        o_ref[...] = gp_ref[...] + dx
        dg_ref[...] += dg

    row = pl.BlockSpec((tm, D), lambda i: (i, 0))
    vec = pl.BlockSpec((1, D), lambda i: (0, 0))
    return _pcall(
        body, name=name, grid=(T // tm,), dep=dep, args=(*acts, w, x, g, gprev),
        in_specs=[pl.BlockSpec((tm, k), lambda i: (i, 0)) for k in ks]
        + [pl.BlockSpec((K, D), lambda i: (0, 0)), row, vec, row],
        out_specs=[row, vec],
        out_shape=[jax.ShapeDtypeStruct((T, D), F32), jax.ShapeDtypeStruct((1, D), F32)],
        compiler_params=_cp("arbitrary"),
    )


def _loss_head(x, g, target):
    T, D = x.shape
    tm = _tile(T, 512)

    def body(x_ref, g_ref, t_ref, dx_ref, loss_ref, dg_ref):
        @pl.when(pl.program_id(0) == 0)
        def _():
            loss_ref[...] = jnp.zeros_like(loss_ref)
            dg_ref[...] = jnp.zeros_like(dg_ref)

        xv = x_ref[...]
        gv = g_ref[...]
        e = xv * _rstd(xv) * gv - t_ref[...]
        per_tok = jnp.sum(e * e, axis=-1, keepdims=True) * (1.0 / D)
        loss_ref[...] += 0.5 * jnp.sum(per_tok, axis=0, keepdims=True)
        dx, dg = _rms_bwd(xv, gv, e * (1.0 / D))
        dx_ref[...] = dx
        dg_ref[...] += dg

    row = pl.BlockSpec((tm, D), lambda i: (i, 0))
    vec = pl.BlockSpec((1, D), lambda i: (0, 0))
    return pl.pallas_call(
        body, name="loss_head", grid=(T // tm,),
        in_specs=[row, vec, row],
        out_specs=[row, pl.BlockSpec((1, LANES), lambda i: (0, 0)), vec],
        out_shape=[jax.ShapeDtypeStruct((T, D), F32), jax.ShapeDtypeStruct((1, LANES), F32),
                   jax.ShapeDtypeStruct((1, D), F32)],
        compiler_params=_cp("arbitrary"),
    )(x, g, target)


def _log_gates(z):
    neg_abs = lax.bitcast_convert_type(lax.bitcast_convert_type(z, jnp.uint32) | jnp.uint32(0x80000000), F32)
    ls = jnp.minimum(z, 0.0) - jnp.log(1.0 + jnp.exp(neg_abs))
    return ls, ls - z


def _cumsum_mm(v, u):
    return _dot(v.astype(BF16), u, NN)


def _half_rowsum(v):
    n = v.shape[0]
    s0 = jnp.sum(v[:, :QB], axis=1, keepdims=True)
    s1 = jnp.sum(v[:, QB:], axis=1, keepdims=True)
    return jnp.concatenate([jnp.broadcast_to(s0, (n, QB)), jnp.broadcast_to(s1, (n, QB))], axis=1)


def _stack_heads(src_ref, dst_ref, n_blk):
    m0 = lax.broadcasted_iota(jnp.int32, (1, LANES), 1) < HEAD_DIM

    def fill(c, carry):
        blk = src_ref[pl.ds(pl.multiple_of(c * QB, QB), QB), :]
        zero = jnp.zeros_like(blk)
        dst_ref[c, 0:QB, :] = jnp.where(m0, blk, zero)
        dst_ref[c, QB:2 * QB, :] = jnp.where(m0, zero, blk)
        return carry

    lax.fori_loop(0, n_blk, fill, 0)


def _diag_mask(tq, j):
    n = tq - j * QB
    row = lax.broadcasted_iota(jnp.int32, (n, 2 * QB), 0)
    col = lax.broadcasted_iota(jnp.int32, (n, 2 * QB), 1)
    return (col & (QB - 1)) < row


def _tri_blockdiag(upper):
    r = lax.broadcasted_iota(jnp.int32, (2 * QB, 2 * QB), 0)
    c = lax.broadcasted_iota(jnp.int32, (2 * QB, 2 * QB), 1)
    same = (r // QB) == (c // QB)
    return (same & ((r > c) if upper else (r < c))).astype(BF16)


def _attn_tiles(T, n_seq):
    S = T // n_seq
    tq = ATT_TQ if S % ATT_TQ == 0 else 2 * QB
    assert S % tq == 0
    return S, tq, tq // QB, S // tq, S // QB


def _attn_fwd(qkv, n_seq):
    T = qkv.shape[0]
    S, tq, r, n_q, n_k = _attn_tiles(T, n_seq)
    n_p = D_MODEL // LANES
    n_steps = n_seq * n_p * n_q
    u_suffix = _tri_blockdiag(True)

    def body(q_ref, k_ref, v_ref, u_ref, o_ref, a_hbm, kk_ref, vv_ref, lr_s, acc_s, a_stage, sems):
        qi = pl.program_id(2)
        group = (pl.program_id(0) * n_p + pl.program_id(1)) * n_q + qi

        @pl.when(qi == 0)
        def _():
            _stack_heads(k_ref, kk_ref, n_k)
            _stack_heads(v_ref, vv_ref, n_k)

        u = u_ref[...]
        lr_s[...] = jnp.zeros_like(lr_s)
        acc_s[...] = jnp.zeros_like(acc_s)
        base = ((group // n_q) * (n_q * (n_q + 1) // 2) + (qi * (qi + 1)) // 2) % 2

        def saves(first_kj, half):
            return [pltpu.make_async_copy(a_stage.at[half, j], a_hbm.at[group, first_kj - j], sems.at[half])
                    for j in range(r)]

        @pl.when(group >= 2)
        def _():
            for cp in saves(0, base):
                cp.wait()

        a_stage[base] = jnp.zeros_like(a_stage[0])

        def step(kj, half, j, rows, q, mask, lr, acc):
            ls, lk = _log_gates(_dot(q, kk_ref[kj], NT))
            if mask is not None:
                lk = jnp.where(mask, lk, 0.0)
            a = jnp.exp(ls + _cumsum_mm(lk, u) + lr)
            if mask is not None:
                a = jnp.where(mask, a, 0.0)
            a = a.astype(BF16)
            a_stage[half, j, rows, :] = a
            return lr + _half_rowsum(lk), acc + _dot(a, vv_ref[kj], NN)

        last = (qi + 1) * r - 1
        for n in range(r):
            rows = slice((r - 1 - n) * QB, tq)
            lr, acc = step(last - n, base, n, rows, q_ref[rows, :] * ATT_SCALE, _diag_mask(tq, r - 1 - n),
                           lr_s[rows, :], acc_s[rows, :])
            lr_s[rows, :] = lr
            acc_s[rows, :] = acc
        for cp in saves(last, base):
            cp.start()

        q = q_ref[...] * ATT_SCALE

        def off(it, carry):
            half = (base + it + 1) % 2
            first = (qi - it) * r - 1
            for cp in saves(first, half):
                cp.wait()
            lr, acc = lr_s[...], acc_s[...]
            for j in range(r):
                lr, acc = step(first - j, half, j, slice(0, tq), q, None, lr, acc)
            lr_s[...] = lr
            acc_s[...] = acc
            for cp in saves(first, half):
                cp.start()
            return carry

        lax.fori_loop(0, qi, off, 0)

        @pl.when(group == n_steps - 1)
        def _():
            for cp in saves(0, (base + qi) % 2):
                cp.wait()
            if n_steps * n_q > 1:
                for cp in saves(0, (base + qi + 1) % 2):
                    cp.wait()

        o_ref[...] = acc_s[...].astype(BF16)

    return pl.pallas_call(
        body, name="attn_fwd", grid=(n_seq, n_p, n_q),
        in_specs=[pl.BlockSpec((tq, LANES), lambda b, p, qi: (b * n_q + qi, p)),
                  pl.BlockSpec((S, LANES), lambda b, p, qi: (b, n_p + p)),
                  pl.BlockSpec((S, LANES), lambda b, p, qi: (b, 2 * n_p + p)),
                  pl.BlockSpec((2 * QB, 2 * QB), lambda b, p, qi: (0, 0))],
        out_specs=[pl.BlockSpec((tq, LANES), lambda b, p, qi: (b * n_q + qi, p)), ANY],
        out_shape=[jax.ShapeDtypeStruct((T, D_MODEL), BF16),
                   jax.ShapeDtypeStruct((n_seq * n_p * n_q, n_k, tq, 2 * QB), BF16)],
        scratch_shapes=[pltpu.VMEM((n_k, 2 * QB, LANES), BF16), pltpu.VMEM((n_k, 2 * QB, LANES), BF16),
                        pltpu.VMEM((tq, 2 * QB), F32), pltpu.VMEM((tq, LANES), F32),
                        pltpu.VMEM((2, r, tq, 2 * QB), BF16), pltpu.SemaphoreType.DMA((2,))],
        compiler_params=_cp("arbitrary", "arbitrary", "arbitrary"),
    )(qkv, qkv, qkv, u_suffix)


def _attn_bwd(qkv, do, a_saved, n_seq):
    T = qkv.shape[0]
    S, tq, r, n_q, n_k = _attn_tiles(T, n_seq)
    n_p = D_MODEL // LANES
    n_steps = n_seq * n_p * n_q
    u_prefix = _tri_blockdiag(False)

    def body(q_ref, k_ref, v_ref, do_ref, u_ref, a_hbm, dq_ref, dk_out, dv_out,
             kk_ref, vv_ref, cg_s, dq_s, dk_ref, dv_ref, a_stage, sems):
        qi = pl.program_id(2)
        group = (pl.program_id(0) * n_p + pl.program_id(1)) * n_q + qi

        base = ((group // n_q) * (n_q * (n_q + 1) // 2) + (qi * (qi + 1)) // 2) % 2

        def fetches(grp, g, half):
            return [pltpu.make_async_copy(a_hbm.at[grp, g * r + j], a_stage.at[half, j], sems.at[half])
                    for j in range(r)]

        @pl.when(group == 0)
        def _():
            for cp in fetches(group, 0, 0):
                cp.start()

        @pl.when(qi == 0)
        def _():
            _stack_heads(k_ref, kk_ref, n_k)
            _stack_heads(v_ref, vv_ref, n_k)
            dk_ref[...] = jnp.zeros_like(dk_ref)
            dv_ref[...] = jnp.zeros_like(dv_ref)

        u = u_ref[...]
        cg_s[...] = jnp.zeros_like(cg_s)
        dq_s[...] = jnp.zeros_like(dq_s)

        def step(kj, half, j, rows, q, dov, mask, cg, dq):
            kk = kk_ref[kj]
            beta = _sigmoid(_dot(q, kk, NT))
            a = a_stage[half, j, rows, :]
            g = a.astype(F32) * _dot(dov, vv_ref[kj], NT)
            dz = g - (g + _dot(g.astype(BF16), u, NN) + cg) * beta
            if mask is not None:
                dz = jnp.where(mask, dz, 0.0)
            dz = dz.astype(BF16)
            keys = pl.ds(pl.multiple_of(kj * QB, QB), QB)
            dvt = _dot(dov_t[:, rows], a, NN)
            dv_ref[keys, :] += jnp.where(t0, dvt[:, :QB], dvt[:, QB:]).T
            dkt = _dot(q_t[:, rows], dz, NN)
            dk_ref[keys, :] += jnp.where(t0, dkt[:, :QB], dkt[:, QB:]).T
            return cg + _half_rowsum(g), dq + _dot(dz, kk, NN)

        q = q_ref[...] * ATT_SCALE
        dov = do_ref[...]
        q_t = q.astype(F32).T.astype(BF16)
        dov_t = dov.astype(F32).T.astype(BF16)
        t0 = lax.broadcasted_iota(jnp.int32, (LANES, 1), 0) < HEAD_DIM

        def off(it, carry):
            half = (base + it) % 2
            for cp in fetches(group, it, half):
                cp.wait()
            for cp in fetches(group, it + 1, 1 - half):
                cp.start()
            cg, dq = cg_s[...], dq_s[...]
            for j in range(r):
                cg, dq = step(it * r + j, half, j, slice(0, tq), q, dov, None, cg, dq)
            cg_s[...] = cg
            dq_s[...] = dq
            return carry

        lax.fori_loop(0, qi, off, 0)

        half = (base + qi) % 2
        for cp in fetches(group, qi, half):
            cp.wait()

        @pl.when(group < n_steps - 1)
        def _():
            for cp in fetches(group + 1, 0, 1 - half):
                cp.start()

        for j in range(r):
            rows = slice(j * QB, tq)
            cg, dq = step(qi * r + j, half, j, rows, q_ref[rows, :] * ATT_SCALE, do_ref[rows, :],
                          _diag_mask(tq, j), cg_s[rows, :], dq_s[rows, :])
            cg_s[rows, :] = cg
            dq_s[rows, :] = dq
        dq_ref[...] = (dq_s[...] * ATT_SCALE).astype(BF16)

        @pl.when(qi == n_q - 1)
        def _():
            dk_out[...] = dk_ref[...].astype(BF16)
            dv_out[...] = dv_ref[...].astype(BF16)

    qspec = pl.BlockSpec((tq, LANES), lambda b, p, qi: (b * n_q + qi, p))
    seq = lambda off: pl.BlockSpec((S, LANES), lambda b, p, qi: (b, off + p))
    return pl.pallas_call(
        body, name="attn_bwd", grid=(n_seq, n_p, n_q),
        in_specs=[qspec, seq(n_p), seq(2 * n_p), qspec,
                  pl.BlockSpec((2 * QB, 2 * QB), lambda b, p, qi: (0, 0)), ANY],
        out_specs=[qspec, seq(0), seq(0)],
        out_shape=[jax.ShapeDtypeStruct((T, D_MODEL), BF16)] * 3,
        scratch_shapes=[pltpu.VMEM((n_k, 2 * QB, LANES), BF16), pltpu.VMEM((n_k, 2 * QB, LANES), BF16),
                        pltpu.VMEM((tq, 2 * QB), F32), pltpu.VMEM((tq, LANES), F32),
                        pltpu.VMEM((S, LANES), F32), pltpu.VMEM((S, LANES), F32),
                        pltpu.VMEM((2, r, tq, 2 * QB), BF16), pltpu.SemaphoreType.DMA((2,))],
        compiler_params=_cp("arbitrary", "arbitrary", "arbitrary"),
    )(qkv, qkv, qkv, do, u_prefix, a_saved)


def _shifted_copies(sh_ref):
    rows = sh_ref.shape[1] - SUBLANES
    for s in range(1, SUBLANES):
        sh_ref[s, 0:rows, :] = sh_ref[0, s:s + rows, :]


def _shifted(sh_ref, start, n):
    s = start % SUBLANES
    return sh_ref[s, start - s:start - s + n, :]


def _glu_with_halo(av_ref, ag_ref, avh_ref, agh_ref, a0_s, first, ts):
    hal = avh_ref[...] * _sigmoid(agh_ref[...])
    a0_s[0, 0:HALO, :] = jnp.where(first, 0.0, hal)
    a0_s[0, HALO:HALO + ts, :] = av_ref[...] * _sigmoid(ag_ref[...])
    _shifted_copies(a0_s)


def _mix_specs(ts, n_r, with_left):
    blk = lambda c: pl.BlockSpec((ts, CA), lambda b, r: (b * n_r + r, c))
    per = ts // HALO
    left = lambda c: pl.BlockSpec((HALO, CA), lambda b, r: (jnp.maximum((b * n_r + r) * per - 1, 0), c))
    return blk, (left if with_left else None)


def _mix_fwd(z, conv_w, conv_b, ln_a_g, ln_a_b, ln_v_g, ln_v_b, ws, bias2d, n_seq):
    T = z.shape[0]
    S = T // n_seq
    ts = _tile(S, 512)
    n_r = S // ts
    shift = HALO - (CONV_WIDTH - 1)

    def body(av_ref, ag_ref, avh_ref, agh_ref, u_ref, v_ref, cw_ref, cb_ref, lag_ref, lab_ref,
             lvg_ref, lvb_ref, ws_ref, bias_ref, cat_ref, a1_ref, a0_s):
        _glu_with_halo(av_ref, ag_ref, avh_ref, agh_ref, a0_s, pl.program_id(1) == 0, ts)
        for rb in range(ts // CONV_ROWS):
            base = rb * CONV_ROWS
            acc = jnp.broadcast_to(cb_ref[...], (CONV_ROWS, CA))
            for k in range(CONV_WIDTH):
                acc = acc + cw_ref[k:k + 1, :] * _shifted(a0_s, base + shift + k, CONV_ROWS)
            a1_ref[base:base + CONV_ROWS, :] = acc
        y, _, _ = _ln_fwd(a1_ref[...], lag_ref[...], lab_ref[...])
        cat_ref[:, 0:CA] = (y * _sigmoid(y)).astype(BF16)
        for gi in range(GB):
            sl = slice(gi * DB, (gi + 1) * DB)
            v1, _, _ = _ln_fwd(v_ref[:, sl], lvg_ref[:, sl], lvb_ref[:, sl])
            v1 = v1.astype(BF16)
            for c in range(ts // CHUNK):
                rs = slice(c * CHUNK, (c + 1) * CHUNK)
                v2 = _dot(ws_ref[gi], v1[rs], NN) + bias_ref[:, sl]
                cat_ref[rs, CA + gi * DB:CA + (gi + 1) * DB] = (u_ref[rs, sl] * v2).astype(BF16)

    blk, left = _mix_specs(ts, n_r, True)
    vec = pl.BlockSpec((1, CA), lambda b, r: (0, 0))
    return pl.pallas_call(
        body, name="mix_fwd", grid=(n_seq, n_r),
        in_specs=[blk(0), blk(1), left(0), left(1), blk(2), blk(3),
                  pl.BlockSpec((CONV_WIDTH, CA), lambda b, r: (0, 0)), vec, vec, vec, vec, vec,
                  pl.BlockSpec((GB, CHUNK, CHUNK), lambda b, r: (0, 0, 0)),
                  pl.BlockSpec((CHUNK, CB), lambda b, r: (0, 0))],
        out_specs=[pl.BlockSpec((ts, CA + CB), lambda b, r: (b * n_r + r, 0)), blk(0)],
        out_shape=[jax.ShapeDtypeStruct((T, CA + CB), BF16), jax.ShapeDtypeStruct((T, CA), F32)],
        scratch_shapes=[pltpu.VMEM((SUBLANES, HALO + ts, CA), F32)],
        compiler_params=_cp("parallel", "parallel"),
    )(z, z, z, z, z, z, conv_w, conv_b, ln_a_g, ln_a_b, ln_v_g, ln_v_b, ws, bias2d)


def _mix_bwd_rows(dcat, z, a1, ln_a_g, ln_a_b, ln_v_g, ln_v_b, ws, ws_t, bias2d, n_seq):
    T = z.shape[0]
    S = T // n_seq
    ts = _tile(S, 512)
    n_r = S // ts

    def body(dc_ref, u_ref, v_ref, a1_ref, lag_ref, lab_ref, lvg_ref, lvb_ref, ws_ref, wst_ref, bias_ref,
             da1_ref, dz_ref, dlag_ref, dlab_ref, dlvg_ref, dlvb_ref, dws_ref, dsb_ref, dv1_s, dbias_s):
        first = (pl.program_id(0) == 0) & (pl.program_id(1) == 0)
        last = (pl.program_id(0) == n_seq - 1) & (pl.program_id(1) == n_r - 1)

        @pl.when(first)
        def _():
            for ref in (dlag_ref, dlab_ref, dlvg_ref, dlvb_ref, dws_ref, dbias_s):
                ref[...] = jnp.zeros_like(ref)

        lag = lag_ref[...]
        y, xh, r = _ln_fwd(a1_ref[...], lag, lab_ref[...])
        sig = _sigmoid(y)
        dy = dc_ref[:, 0:CA] * (sig * (1.0 + y * (1.0 - sig)))
        dlag_ref[...] += jnp.sum(dy * xh, axis=0, keepdims=True)
        dlab_ref[...] += jnp.sum(dy, axis=0, keepdims=True)
        da1_ref[...] = _ln_bwd(dy, xh, r, lag)

        tril = (lax.broadcasted_iota(jnp.int32, (CHUNK, CHUNK), 0)
                >= lax.broadcasted_iota(jnp.int32, (CHUNK, CHUNK), 1))
        for gi in range(GB):
            sl = slice(gi * DB, (gi + 1) * DB)
            lvg = lvg_ref[:, sl]
            v1, vh, vr = _ln_fwd(v_ref[:, sl], lvg, lvb_ref[:, sl])
            v1 = v1.astype(BF16)
            for c in range(ts // CHUNK):
                rs = slice(c * CHUNK, (c + 1) * CHUNK)
                v2 = _dot(ws_ref[gi], v1[rs], NN) + bias_ref[:, sl]
                dbo = dc_ref[rs, CA + gi * DB:CA + (gi + 1) * DB]
                dz_ref[rs, sl] = (dbo * v2).astype(BF16)
                dv2 = dbo * u_ref[rs, sl]
                dbias_s[:, sl] += dv2
                dv2b = dv2.astype(BF16)
                dws_ref[gi] += jnp.where(tril, _dot(dv2b, v1[rs], NT), 0.0)
                dv1_s[rs, :] = _dot(wst_ref[gi], dv2b, NN)
            dv1 = dv1_s[...]
            dlvg_ref[:, sl] += jnp.sum(dv1 * vh, axis=0, keepdims=True)
            dlvb_ref[:, sl] += jnp.sum(dv1, axis=0, keepdims=True)
            dz_ref[:, CB + gi * DB:CB + (gi + 1) * DB] = _ln_bwd(dv1, vh, vr, lvg).astype(BF16)

        @pl.when(last)
        def _():
            col = lax.broadcasted_iota(jnp.int32, (CHUNK, GB), 1)
            out = jnp.zeros((CHUNK, GB), F32)
            for gi in range(GB):
                s = jnp.sum(dbias_s[:, gi * DB:(gi + 1) * DB], axis=1, keepdims=True)
                out = out + jnp.where(col == gi, s, 0.0)
            dsb_ref[...] = out

    blk, _ = _mix_specs(ts, n_r, False)
    vec = pl.BlockSpec((1, CA), lambda b, r: (0, 0))
    mat = pl.BlockSpec((GB, CHUNK, CHUNK), lambda b, r: (0, 0, 0))
    wide = pl.BlockSpec((ts, CA + CB), lambda b, r: (b * n_r + r, 0))
    return pl.pallas_call(
        body, name="mix_bwd_rows", grid=(n_seq, n_r),
        in_specs=[wide, blk(2), blk(3), blk(0), vec, vec, vec, vec, mat, mat,
                  pl.BlockSpec((CHUNK, CB), lambda b, r: (0, 0))],
        out_specs=[blk(0), wide, vec, vec, vec, vec, mat, pl.BlockSpec((CHUNK, GB), lambda b, r: (0, 0))],
        out_shape=[jax.ShapeDtypeStruct((T, CA), F32), jax.ShapeDtypeStruct((T, 2 * CB), BF16)]
        + [jax.ShapeDtypeStruct((1, CA), F32)] * 4
        + [jax.ShapeDtypeStruct((GB, CHUNK, CHUNK), F32), jax.ShapeDtypeStruct((CHUNK, GB), F32)],
        scratch_shapes=[pltpu.VMEM((ts, DB), F32), pltpu.VMEM((CHUNK, CB), F32)],
        compiler_params=_cp("arbitrary", "arbitrary"),
    )(dcat, z, z, a1, ln_a_g, ln_a_b, ln_v_g, ln_v_b, ws, ws_t, bias2d)


def _mix_bwd_conv(da1, z, conv_w, n_seq):
    T = z.shape[0]
    S = T // n_seq
    ts = _tile(S, 512)
    n_r = S // ts
    per = ts // HALO
    shift = HALO - (CONV_WIDTH - 1)
    fold = CONV_ROWS // 8

    def body(d_ref, dh_ref, av_ref, ag_ref, avh_ref, agh_ref, cw_ref,
             dz_ref, dcw_ref, dcb_ref, a0_s, d1_s, da0_s, dw8_s):
        first = (pl.program_id(0) == 0) & (pl.program_id(1) == 0)
        last = (pl.program_id(0) == n_seq - 1) & (pl.program_id(1) == n_r - 1)

        @pl.when(first)
        def _():
            dw8_s[...] = jnp.zeros_like(dw8_s)
            dcb_ref[...] = jnp.zeros_like(dcb_ref)

        _glu_with_halo(av_ref, ag_ref, avh_ref, agh_ref, a0_s, pl.program_id(1) == 0, ts)
        d1_s[0, 0:ts, :] = d_ref[...]
        d1_s[0, ts:ts + HALO, :] = jnp.where(pl.program_id(1) == n_r - 1, 0.0, dh_ref[...])
        _shifted_copies(d1_s)
        dcb_ref[...] += jnp.sum(d_ref[...], axis=0, keepdims=True)
        for rb in range(ts // CONV_ROWS):
            base = rb * CONV_ROWS
            dcur = d1_s[0, base:base + CONV_ROWS, :]
            acc = jnp.zeros((CONV_ROWS, CA), F32)
            for k in range(CONV_WIDTH):
                back = CONV_WIDTH - 1 - k
                acc = acc + cw_ref[k:k + 1, :] * _shifted(d1_s, base + back, CONV_ROWS)
                prod = dcur * _shifted(a0_s, base + shift + k, CONV_ROWS)
                part = prod[0:8]
                for f in range(1, fold):
                    part = part + prod[8 * f:8 * f + 8]
                dw8_s[k] += part
            da0_s[base:base + CONV_ROWS, :] = acc
        da0 = da0_s[...]
        sig = _sigmoid(ag_ref[...])
        dz_ref[:, 0:CA] = (da0 * sig).astype(BF16)
        dz_ref[:, CA:2 * CA] = (da0 * av_ref[...] * sig * (1.0 - sig)).astype(BF16)

        @pl.when(last)
        def _():
            for k in range(CONV_WIDTH):
                dcw_ref[k:k + 1, :] = jnp.sum(dw8_s[k], axis=0, keepdims=True)

    blk, left = _mix_specs(ts, n_r, True)
    n_halo_blocks = T // HALO
    right = pl.BlockSpec((HALO, CA), lambda b, r: (jnp.minimum((b * n_r + r + 1) * per, n_halo_blocks - 1), 0))
    return pl.pallas_call(
        body, name="mix_bwd_conv", grid=(n_seq, n_r),
        in_specs=[blk(0), right, blk(0), blk(1), left(0), left(1),
                  pl.BlockSpec((CONV_WIDTH, CA), lambda b, r: (0, 0))],
        out_specs=[pl.BlockSpec((ts, 2 * CA), lambda b, r: (b * n_r + r, 0)),
                   pl.BlockSpec((CONV_WIDTH, CA), lambda b, r: (0, 0)), pl.BlockSpec((1, CA), lambda b, r: (0, 0))],
        out_shape=[jax.ShapeDtypeStruct((T, 2 * CA), BF16), jax.ShapeDtypeStruct((CONV_WIDTH, CA), F32),
                   jax.ShapeDtypeStruct((1, CA), F32)],
        scratch_shapes=[pltpu.VMEM((SUBLANES, HALO + ts, CA), F32), pltpu.VMEM((SUBLANES, ts + HALO, CA), F32),
                        pltpu.VMEM((ts, CA), F32), pltpu.VMEM((CONV_WIDTH, 8, CA), F32)],
        compiler_params=_cp("arbitrary", "arbitrary"),
    )(da1, da1, z, z, z, z, conv_w)


def _row_tile(R, want):
    t = min(R, want)
    t -= t % 8
    while t > 8 and R % t:
        t -= 8
    return t if t >= 8 and R % t == 0 else R


def _adam_step(w, g, m, v):
    nm = ADAM_B1 * m + (1.0 - ADAM_B1) * g
    nv = ADAM_B2 * v + (1.0 - ADAM_B2) * (g * g)
    m_hat = nm / (1.0 - ADAM_B1 ** ADAM_STEP)
    v_hat = nv / (1.0 - ADAM_B2 ** ADAM_STEP)
    return -ADAM_LR * (m_hat / (jnp.sqrt(v_hat) + ADAM_EPS) + ADAM_WD * w), nm, nv


def _adamw_parts(parts, w, m, v, name):
    L, n, C = w.shape
    assert len(parts) == L
    tr = _row_tile(n, 192)
    n_i = n // tr

    def body(*refs):
        p_refs = refs[:L]
        w_ref, m_ref, v_ref, g_ref, d_ref, nm_ref, nv_ref = refs[L:]
        for k in range(L):
            @pl.when(pl.program_id(0) == k)
            def _(k=k):
                acc = p_refs[k][0].astype(F32)
                for s in range(1, N_DEV):
                    acc = acc + p_refs[k][s].astype(F32)
                g_ref[...] = acc

        d_ref[...], nm_ref[...], nv_ref[...] = _adam_step(w_ref[...], g_ref[...], m_ref[...], v_ref[...])

    def part_spec(k):
        return pl.BlockSpec((N_DEV, tr, C),
                            lambda l, i: (0, jnp.where(l == k, i, jnp.where(l < k, 0, n_i - 1)), 0))

    blk = pl.BlockSpec((None, tr, C), lambda l, i: (l, i, 0))
    return pl.pallas_call(
        body, name=name, grid=(L, n_i),
        in_specs=[part_spec(k) for k in range(L)] + [blk] * 3, out_specs=[blk] * 4,
        out_shape=[jax.ShapeDtypeStruct((L, n, C), F32)] * 4,
        compiler_params=_cp("arbitrary", "arbitrary"),
    )(*parts, w, m, v)


def _adamw(w, g, m, v, name):
    R, C = w.shape
    tr = _row_tile(R, 256)

    def body(w_ref, g_ref, m_ref, v_ref, d_ref, nm_ref, nv_ref):
        d_ref[...], nm_ref[...], nv_ref[...] = _adam_step(w_ref[...], g_ref[...], m_ref[...], v_ref[...])

    blk = pl.BlockSpec((tr, C), lambda i: (i, 0))
    return pl.pallas_call(
        body, name=name, grid=(R // tr,),
        in_specs=[blk] * 4, out_specs=[blk] * 3,
        out_shape=[jax.ShapeDtypeStruct((R, C), F32)] * 3,
        compiler_params=_cp("parallel"),
    )(w, g, m, v)


def _me():
    return lax.axis_index("x"), lax.axis_index("y"), lax.axis_index("c")


def _block_rows(ref, dev, n):
    start = (4 * dev[0] + 2 * dev[1] + dev[2]) * n
    if len(ref.shape) == 2:
        return ref.at[pl.ds(start, n), :]
    return ref.at[:, pl.ds(start, n), :]


def _all_gather(shards):
    na = len(shards)
    ns = [s.shape[-2] for s in shards]

    def body(*refs):
        ins, outs = refs[:na], refs[na:2 * na]
        send_sems, recv_sems, local_sems = refs[2 * na:]
        x, y, c = _me()
        me, sibling = (x, y, c), (x, y, 1 - c)
        chips = [(1 - x, y), (x, 1 - y), (1 - x, 1 - y)]

        def copy(a, k, block, to, src=None):
            dst = _block_rows(outs[a], block, ns[a])
            return pltpu.make_async_remote_copy(
                src_ref=dst if src is None else src, dst_ref=dst,
                send_sem=send_sems.at[a, k], recv_sem=recv_sems.at[a, k], device_id=to, device_id_type=MESH)

        mine = [pltpu.make_async_copy(ins[a], _block_rows(outs[a], me, ns[a]), local_sems.at[a]) for a in range(na)]
        for cp in mine:
            cp.start()
        first = []
        for a in range(na):
            first.append(copy(a, 0, me, sibling, src=ins[a]))
            first += [copy(a, 1 + j, me, (*chip, c), src=ins[a]) for j, chip in enumerate(chips)]
        for cp in first:
            cp.start()
        passed = []
        for j, chip in enumerate(chips):
            for a in range(na):
                copy(a, 1 + j, (*chip, c), me).wait_recv()
                fwd = copy(a, 4 + j, (*chip, c), sibling)
                fwd.start()
                passed.append(fwd)
        for a in range(na):
            copy(a, 0, sibling, me).wait_recv()
            for j, chip in enumerate(chips):
                copy(a, 4 + j, (*chip, 1 - c), me).wait_recv()
        for cp in first + passed:
            cp.wait_send()
        for cp in mine:
            cp.wait()

    out_shape = [jax.ShapeDtypeStruct(s.shape[:-2] + (N_DEV * s.shape[-2], s.shape[-1]), s.dtype) for s in shards]
    return pl.pallas_call(
        body, name="weights_all_gather",
        in_specs=[ANY] * na, out_specs=[ANY] * na, out_shape=out_shape,
        scratch_shapes=[pltpu.SemaphoreType.DMA((na, 7)), pltpu.SemaphoreType.DMA((na, 7)),
                        pltpu.SemaphoreType.DMA((na,))],
    )(*shards)


def _split_copies(gather, srcs, lands, send_sems, recv_sems, ns):
    x, y, c = _me()
    me = (x, y, c)
    my_slot = 4 * x + 2 * y + c
    copies = []
    for mask in range(1, N_DEV):
        peer = (x ^ (mask >> 2), y ^ ((mask >> 1) & 1), c ^ (mask & 1))
        for a in range(len(srcs)):
            if gather:
                src, dst = srcs[a], _block_rows(lands[a], me, ns[a])
            else:
                src, dst = _block_rows(srcs[a], peer, ns[a]), lands[a].at[my_slot]
            sem = a * (N_DEV - 1) + mask - 1
            copies.append(pltpu.make_async_remote_copy(
                src_ref=src, dst_ref=dst, send_sem=send_sems.at[sem], recv_sem=recv_sems.at[sem],
                device_id=peer, device_id_type=MESH))
    return copies


HBM_SPEC = pl.BlockSpec(memory_space=pltpu.HBM)
SEM_SPEC = pl.BlockSpec(memory_space=pltpu.SEMAPHORE)


def _split_start(gather, srcs, name, dep=None):
    na = len(srcs)
    x, y, c = _me()
    mine = 4 * x + 2 * y + c
    if gather:
        ns = [s.shape[-2] for s in srcs]
        lands = [lax.dynamic_update_slice(
            lax.empty(s.shape[:-2] + (N_DEV * s.shape[-2], s.shape[-1]), s.dtype), s,
            (0,) * (s.ndim - 2) + (mine * s.shape[-2], 0)) for s in srcs]
    else:
        ns = [s.shape[-2] // N_DEV for s in srcs]
        lands = [lax.dynamic_update_slice(
            lax.empty((N_DEV, n, s.shape[-1]), s.dtype),
            lax.dynamic_slice(s, (mine * n, 0), (n, s.shape[-1]))[None], (mine, 0, 0)) for s, n in zip(srcs, ns)]
    n_in = 2 * na + (dep is not None)

    def body(*refs):
        send_sems, recv_sems = refs[n_in], refs[n_in + 1]
        for cp in _split_copies(gather, refs[:na], refs[na:2 * na], send_sems, recv_sems, ns):
            cp.start()
        refs[-1][...] = jnp.zeros_like(refs[-1])

    hbm = lambda a: pltpu.with_memory_space_constraint(a, pltpu.HBM)
    args = [hbm(a) for a in srcs] + [hbm(a) for a in lands] + ([dep] if dep is not None else [])
    out = pl.pallas_call(
        body, name=name,
        in_specs=[HBM_SPEC] * (2 * na) + ([ANY] if dep is not None else []),
        out_specs=[SEM_SPEC, SEM_SPEC] + [HBM_SPEC] * (2 * na) + [pl.BlockSpec(memory_space=pltpu.VMEM)],
        out_shape=[pltpu.SemaphoreType.DMA((na * (N_DEV - 1),)), pltpu.SemaphoreType.DMA((na * (N_DEV - 1),))]
        + [pltpu.HBM(a.shape, a.dtype) for a in srcs + lands] + [jax.ShapeDtypeStruct((8, LANES), F32)],
        input_output_aliases={i: 2 + i for i in range(2 * na)},
        compiler_params=pltpu.CompilerParams(has_side_effects=pltpu.SideEffectType.DATAFLOW_SIDE_EFFECTING),
    )(*args)
    return (gather, ns, out[0], out[1], list(out[2:2 + na]), list(out[2 + na:2 + 2 * na])), out[-1]


def _split_wait(handle, after, name):
    gather, ns, send, recv, srcs, lands = handle
    na = len(srcs)

    def body(*refs):
        send_sems, recv_sems = refs[2 * na], refs[2 * na + 1]
        for cp in _split_copies(gather, refs[:na], refs[na:2 * na], send_sems, recv_sems, ns):
            cp.wait_send()
            cp.wait_recv()

    out = pl.pallas_call(
        body, name=name,
        in_specs=[HBM_SPEC] * (2 * na) + [SEM_SPEC, SEM_SPEC, ANY],
        out_specs=[HBM_SPEC] * (2 * na),
        out_shape=[pltpu.HBM(a.shape, a.dtype) for a in srcs + lands],
        input_output_aliases={i: i for i in range(2 * na)},
        compiler_params=pltpu.CompilerParams(has_side_effects=pltpu.SideEffectType.DATAFLOW_SIDE_EFFECTING),
    )(*srcs, *lands, send, recv, after)
    return list(out[:na]), list(out[na:])


def _sum_blocks(parts):
    n, R, C = parts.shape

    def body(p_ref, o_ref):
        acc = p_ref[0]
        for k in range(1, n):
            acc = acc + p_ref[k]
        o_ref[...] = acc

    return pl.pallas_call(
        body, name="small_sum",
        in_specs=[pl.BlockSpec(memory_space=pltpu.VMEM)], out_specs=pl.BlockSpec(memory_space=pltpu.VMEM),
        out_shape=jax.ShapeDtypeStruct((R, C), F32),
        compiler_params=pltpu.CompilerParams(vmem_limit_bytes=VMEM_LIMIT),
    )(parts)


def _pack(arrays):
    flat = jnp.concatenate([a.reshape(-1) for a in arrays])
    pad = (-flat.shape[0]) % (8 * LANES)
    return jnp.pad(flat, (0, pad)).reshape(-1, LANES)


def _unpack(buf, shapes):
    flat = buf.reshape(-1)
    out, off = [], 0
    for s in shapes:
        n = 1
        for d in s:
            n *= d
        out.append(flat[off:off + n].reshape(s))
        off += n
    return out


def _ffn_index(layer, second):
    return (2 * layer + second) * 3


def kernel(x, g_ffn1, w_ffn1_gate, w_ffn1_up, w_ffn1_down, g_mix, w_in_ab, conv_w, conv_b, ln_a_g, ln_a_b, ln_v_g, ln_v_b, sp_w, sp_b, w_out_ab, w_qkv, w_o, g_ffn2, w_ffn2_gate, w_ffn2_up, w_ffn2_down, g_final, loss_target, m_g_ffn1, m_w_ffn1_gate, m_w_ffn1_up, m_w_ffn1_down, m_g_mix, m_w_in_ab, m_conv_w, m_conv_b, m_ln_a_g, m_ln_a_b, m_ln_v_g, m_ln_v_b, m_sp_w, m_sp_b, m_w_out_ab, m_w_qkv, m_w_o, m_g_ffn2, m_w_ffn2_gate, m_w_ffn2_up, m_w_ffn2_down, m_g_final, v_g_ffn1, v_w_ffn1_gate, v_w_ffn1_up, v_w_ffn1_down, v_g_mix, v_w_in_ab, v_conv_w, v_conv_b, v_ln_a_g, v_ln_a_b, v_ln_v_g, v_ln_v_b, v_sp_w, v_sp_b, v_w_out_ab, v_w_qkv, v_w_o, v_g_ffn2, v_w_ffn2_gate, v_w_ffn2_up, v_w_ffn2_down, v_g_final):
    n_seq, S, D = x.shape
    T = n_seq * S
    depth = g_ffn1.shape[0]
    assert depth == 2 and D == D_MODEL
    my_block = 4 * lax.axis_index("x") + 2 * lax.axis_index("y") + lax.axis_index("c")

    ffn_parts = []
    for l in range(depth):
        for gate, up, down in ((w_ffn1_gate, w_ffn1_up, w_ffn1_down), (w_ffn2_gate, w_ffn2_up, w_ffn2_down)):
            ffn_parts += [gate[l].T, up[l].T, down[l]]
    ffn_shard = lambda k: jnp.stack(ffn_parts[3 * k:3 * k + 3]).astype(BF16)
    conv_w_pad = jnp.zeros((HALO, conv_w.shape[2]), F32).at[:CONV_WIDTH].set(conv_w[0]).T
    w_ffn = [None] * (2 * depth)
    w_ffn[0], conv_w_t = _all_gather([ffn_shard(0), conv_w_pad])
    conv_w_full = conv_w_t.T[:CONV_WIDTH]
    shards_b = [w_out_ab[0].astype(BF16), ffn_shard(1)]
    shards_d = [w_qkv[0].T.astype(BF16), w_o[0].astype(BF16), ffn_shard(3)]
    gather_a, token = _split_start(True, [w_in_ab[0].T.astype(BF16)], "gather_a_start", dep=conv_w_t)
    gather_b, token = _split_start(True, shards_b, "gather_b_start", dep=token)
    gather_c, token = _split_start(True, [ffn_shard(2)], "gather_c_start", dep=token)
    gather_d, token = _split_start(True, shards_d, "gather_d_start", dep=token)

    def gathered(handle, after, name):
        return _split_wait(handle, after, name)[1]

    row = lambda a: a.reshape(1, -1)
    tril = jnp.tril(jnp.ones((CHUNK, CHUNK), dtype=bool))
    ws = jnp.where(tril[None], sp_w[0], 0.0).astype(BF16)
    ws_t = jnp.swapaxes(ws, 1, 2)
    bias2d = jnp.repeat(sp_b[0].T, DB, axis=1)
    conv_b2, lag, lab = row(conv_b[0]), row(ln_a_g[0]), row(ln_a_b[0])
    lvg, lvb = row(ln_v_g[0]), row(ln_v_b[0])

    x0 = x.reshape(T, D)
    target = loss_target.reshape(T, D)
    saved = []
    xc = x0
    for l in range(depth):
        xa, a1, b1, h1 = _ffn_fwd(xc, row(g_ffn1[l]), w_ffn[2 * l], 0, f"ffn1_fwd_{l}", dep=token)
        if l % 2 == 0:
            w_in_t, = gathered(gather_a, xa, "gather_a_wait")
            z, hm = _mm_nt(xa, w_in_t, row(g_mix[l]), F32, "mix_in_proj")
            cat, conv_out = _mix_fwd(z, conv_w_full, conv_b2, lag, lab, lvg, lvb, ws, bias2d, n_seq)
            w_out, w_ffn[1] = gathered(gather_b, cat, "gather_b_wait")
            out_proj = (cat, w_out)
            mixer = (z, hm, cat, conv_out)
        else:
            w_qkv_t, w_o_full, w_ffn[3] = gathered(gather_d, xa, "gather_d_wait")
            qkv, hm = _mm_nt(xa, w_qkv_t, row(g_mix[l]), BF16, "qkv_proj")
            o, att = _attn_fwd(qkv, n_seq)
            out_proj = (o, w_o_full)
            mixer = (qkv, hm, o, att)
        if l < depth - 1:
            xb, xn, a2, b2, h2 = _proj_ffn_fwd(*out_proj, xa, row(g_ffn2[l]), w_ffn[2 * l + 1], 0, f"ffn2_fwd_{l}")
        else:
            xb, a2, b2, h2, g, loss_part, dg_final = _proj_ffn_loss(
                *out_proj, xa, row(g_ffn2[l]), w_ffn[2 * l + 1], 0, row(g_final), target, f"ffn2_loss_{l}")
        saved.append((xc, a1, b1, h1, xa, mixer, xb, a2, b2, h2))
        xc = xn
        if l == 0:
            w_ffn[2], = gathered(gather_c, xc, "gather_c_wait")

    dg_ffn1, dg_ffn2, dg_mix = [None] * depth, [None] * depth, [None] * depth
    exchanges = {}
    token = None

    def ffn_back(g, xin, gvec, a, b, h, k, tag, token):
        g, dg, da, db, s, gh = _ffn_bwd(g, xin, gvec, a, b, w_ffn[k], 0, f"ffn{tag}_bwd", dep=token)
        if k == 0:
            return g, dg, (da, db, s, gh, h)
        dws = [_mm_tn(da, h, f"dw_gate{tag}"), _mm_tn(db, h, f"dw_up{tag}"), _mm_tn(s, gh, f"dw_down{tag}")]
        exchanges[f"ffn{k}"], token = _split_start(False, dws, f"exchange_ffn{tag}_start")
        return g, dg, token

    for l in reversed(range(depth)):
        xin, a1, b1, h1, xa, mixer, xb, a2, b2, h2 = saved[l]
        g, dg_ffn2[l], token = ffn_back(g, xb, row(g_ffn2[l]), a2, b2, h2, 2 * l + 1, f"2_{l}", token)
        if l % 2 == 0:
            z, hm, cat, conv_out = mixer
            dcat = _mm_nt(g, w_out, None, F32, "mix_out_bwd", dep=token)
            d_w_out = _mm_tn(cat, g, "dw_out")
            (da1, dz_uv, d_lag, d_lab, d_lvg, d_lvb, d_ws, d_sb) = _mix_bwd_rows(
                dcat, z, conv_out, lag, lab, lvg, lvb, ws, ws_t, bias2d, n_seq)
            dz_a, d_cw, d_cb = _mix_bwd_conv(da1, z, conv_w_full, n_seq)
            d_w_in_t = jnp.concatenate([_mm_tn(dz_a, hm, "dw_in_a"), _mm_tn(dz_uv, hm, "dw_in_uv")])
            exchanges["mix"], token = _split_start(False, [d_w_out, d_w_in_t], "exchange_mix_start")
            g, dg_mix[l] = _mm_nn_rmsbwd([dz_a, dz_uv], w_in_t, xa, row(g_mix[l]), g, "mix_in_bwd", dep=token)
        else:
            qkv, hm, o, att = mixer
            do = _mm_nt(g, w_o_full, None, BF16, "attn_out_bwd", dep=token)
            d_w_o = _mm_tn(o, g, "dw_o")
            dq, dk, dv = _attn_bwd(qkv, do, att, n_seq)
            d_w_qkv_t = jnp.concatenate([_mm_tn(dq, hm, "dw_q"), _mm_tn(dk, hm, "dw_k"), _mm_tn(dv, hm, "dw_v")])
            exchanges["attn"], token = _split_start(False, [d_w_o, d_w_qkv_t], "exchange_attn_start")
            g, dg_mix[l] = _mm_nn_rmsbwd([dq, dk, dv], w_qkv_t, xa, row(g_mix[l]), g, "qkv_bwd", dep=token)
        g, dg_ffn1[l], token = ffn_back(g, xin, row(g_ffn1[l]), a1, b1, h1, 2 * l, f"1_{l}", token)
    grad_x = g.reshape(n_seq, S, D)

    small = [jnp.concatenate(dg_ffn1), jnp.concatenate(dg_mix), d_cw, d_cb, d_lag, d_lab, d_lvg, d_lvb,
             jnp.where(tril[None], d_ws, 0.0), d_sb.T, jnp.concatenate(dg_ffn2), dg_final, loss_part[:, :1]]
    small_shapes = [(depth, D), (depth, D), (CONV_WIDTH, CA), (1, CA), (1, CA), (1, CA), (1, GB, DB), (1, GB, DB),
                    (1, GB, CHUNK, CHUNK), (1, GB, CHUNK), (depth, D), (D,), ()]
    da, db, s, gh, h = token
    small_gather, token = _split_start(True, [_pack(small)], "small_gather_start")

    for which, lhs, rhs in ((2, s, gh), (1, db, h), (0, da, h)):
        dw = _mm_tn(lhs, rhs, f"dw_ffn0_{which}", dep=token, out_dtype=BF16)
        exchanges[f"ffn0_{which}"], token = _split_start(False, [dw], f"exchange_ffn0_{which}_start")

    small_all, = _split_wait(small_gather, token, "small_gather_wait")[1]
    red = _unpack(_sum_blocks(small_all.reshape(N_DEV, -1, LANES)), small_shapes)
    (gr_g_ffn1, gr_g_mix, gr_cw_full, gr_cb, gr_lag, gr_lab, gr_lvg, gr_lvb, gr_sp_w, gr_sp_b,
     gr_g_ffn2, gr_g_final, loss) = red
    n_cw = conv_w.shape[2]
    gr_cw = lax.dynamic_slice(gr_cw_full, (0, my_block * n_cw), (CONV_WIDTH, n_cw))[None]

    def landed(key, after):
        return _split_wait(exchanges[key], after, f"exchange_{key}_wait")[1]

    parts_ffn = [None] * (6 * depth)
    for k in range(1, 2 * depth):
        parts_ffn[3 * k:3 * k + 3] = landed(f"ffn{k}", token)
    parts_out, parts_in = landed("mix", token)
    parts_o, parts_qkv = landed("attn", token)

    grads = {
        "g_ffn1": gr_g_ffn1, "g_mix": gr_g_mix, "conv_w": gr_cw, "conv_b": gr_cb, "ln_a_g": gr_lag,
        "ln_a_b": gr_lab, "ln_v_g": gr_lvg, "ln_v_b": gr_lvb, "sp_w": gr_sp_w, "sp_b": gr_sp_b,
        "g_ffn2": gr_g_ffn2, "g_final": gr_g_final,
    }
    weights = dict(g_ffn1=g_ffn1, w_ffn1_gate=w_ffn1_gate, w_ffn1_up=w_ffn1_up, w_ffn1_down=w_ffn1_down, g_mix=g_mix,
                   w_in_ab=w_in_ab, conv_w=conv_w, conv_b=conv_b, ln_a_g=ln_a_g, ln_a_b=ln_a_b, ln_v_g=ln_v_g,
                   ln_v_b=ln_v_b, sp_w=sp_w, sp_b=sp_b, w_out_ab=w_out_ab, w_qkv=w_qkv, w_o=w_o, g_ffn2=g_ffn2,
                   w_ffn2_gate=w_ffn2_gate, w_ffn2_up=w_ffn2_up, w_ffn2_down=w_ffn2_down, g_final=g_final)
    m_in = dict(g_ffn1=m_g_ffn1, w_ffn1_gate=m_w_ffn1_gate, w_ffn1_up=m_w_ffn1_up, w_ffn1_down=m_w_ffn1_down,
                g_mix=m_g_mix, w_in_ab=m_w_in_ab, conv_w=m_conv_w, conv_b=m_conv_b, ln_a_g=m_ln_a_g, ln_a_b=m_ln_a_b,
                ln_v_g=m_ln_v_g, ln_v_b=m_ln_v_b, sp_w=m_sp_w, sp_b=m_sp_b, w_out_ab=m_w_out_ab, w_qkv=m_w_qkv,
                w_o=m_w_o, g_ffn2=m_g_ffn2, w_ffn2_gate=m_w_ffn2_gate, w_ffn2_up=m_w_ffn2_up,
                w_ffn2_down=m_w_ffn2_down, g_final=m_g_final)
    v_in = dict(g_ffn1=v_g_ffn1, w_ffn1_gate=v_w_ffn1_gate, w_ffn1_up=v_w_ffn1_up, w_ffn1_down=v_w_ffn1_down,
                g_mix=v_g_mix, w_in_ab=v_w_in_ab, conv_w=v_conv_w, conv_b=v_conv_b, ln_a_g=v_ln_a_g, ln_a_b=v_ln_a_b,
                ln_v_g=v_ln_v_g, ln_v_b=v_ln_v_b, sp_w=v_sp_w, sp_b=v_sp_b, w_out_ab=v_w_out_ab, w_qkv=v_w_qkv,
                w_o=v_w_o, g_ffn2=v_g_ffn2, w_ffn2_gate=v_w_ffn2_gate, w_ffn2_up=v_w_ffn2_up,
                w_ffn2_down=v_w_ffn2_down, g_final=v_g_final)
    names = list(weights)
    grads = {n: grads[n].reshape(weights[n].shape) for n in grads}

    delta, new_m, new_v = {}, {}, {}

    def adamw_big(n, parts):
        if weights[n].shape[-1] == D:
            view = back = lambda a: a
        else:
            view = back = lambda a: jnp.swapaxes(a, 1, 2)
        out = _adamw_parts(parts, view(weights[n]), view(m_in[n]), view(v_in[n]), f"adamw_{n}")
        grads[n], delta[n], new_m[n], new_v[n] = [back(a) for a in out]

    adamw_big("w_in_ab", [parts_in])
    adamw_big("w_out_ab", [parts_out])
    adamw_big("w_qkv", [parts_qkv])
    adamw_big("w_o", [parts_o])
    kinds = ("gate", "up", "down")
    for which, kind in enumerate(kinds):
        adamw_big(f"w_ffn2_{kind}", [parts_ffn[_ffn_index(l, 1) + which] for l in range(depth)])
    big = [n for n in names if n.startswith("w_")]
    after = jnp.concatenate([delta[n].reshape(-1)[:1] for n in big if n in delta]).reshape(1, -1)
    for which in (2, 1, 0):
        parts_ffn[which], = landed(f"ffn0_{which}", after)
    for which, kind in enumerate(kinds):
        adamw_big(f"w_ffn1_{kind}", [parts_ffn[_ffn_index(l, 0) + which] for l in range(depth)])
    little = [n for n in names if n not in big]
    shapes = [weights[n].shape for n in little]
    d, nm, nv = _adamw(_pack([weights[n] for n in little]), _pack([grads[n] for n in little]),
                       _pack([m_in[n] for n in little]), _pack([v_in[n] for n in little]), "adamw_small")
    for n, dd, mm, vv in zip(little, _unpack(d, shapes), _unpack(nm, shapes), _unpack(nv, shapes)):
        delta[n], new_m[n], new_v[n] = dd, mm, vv

    return (loss, grad_x, *[grads[n] for n in names], *[delta[n] for n in names],
            *[new_m[n] for n in names], *[new_v[n] for n in names])
```
